```python
import jax, jax.numpy as jnp
from jax import lax
import numpy as np

D_MODEL = 1024
BATCH = 8
SEQ = 8192
DEPTH = 1

LRU_WIDTH = D_MODEL
LRU_HEADS = 16
LRU_BLOCK = LRU_WIDTH // LRU_HEADS
CONV_WIDTH = 4
CONV_LEFT = 2
RGLRU_C = 8.0
N_DIR = 2
N_HEADS = 16
N_KV_HEADS = 4
HEAD_DIM = 64
GROUP = N_HEADS // N_KV_HEADS
WINDOW = 128
BLOCK = 128
D_FF = ((8 * D_MODEL // 3 + 255) // 256) * 256
N_BRANCH = 2
Q_W = N_HEADS * HEAD_DIM
KV_W = N_KV_HEADS * HEAD_DIM
IN_W = 2 * LRU_WIDTH + Q_W + 2 * KV_W + N_BRANCH * D_MODEL
EPS = 1e-6
NEG_INF = -1e30

kernel_name = "hybrid_rglru_swa_gated_encoder"


def rmsnorm(x, g):
    xf = x.astype(jnp.float32)
    y = xf * lax.rsqrt(jnp.mean(xf * xf, axis=-1, keepdims=True) + EPS)
    return (y * g.astype(jnp.float32)).astype(x.dtype)


def centred_depthwise_conv(u, w, b):
    s = u.shape[1]
    up = jnp.pad(u, ((0, 0), (CONV_LEFT, CONV_WIDTH - 1 - CONV_LEFT), (0, 0)))
    out = up[:, 0:s] * w[0]
    for k in range(1, CONV_WIDTH):
        out = out + up[:, k:k + s] * w[k]
    return out + b


def _linear_combine(left, right):
    a1, b1 = left
    a2, b2 = right
    return a1 * a2, a2 * b1 + b2


def rg_lru(u, lam, wa, ba, wx, bx, reverse):
    bsz, s, c = u.shape
    ub = u.reshape(bsz, s, LRU_HEADS, LRU_BLOCK)
    r = jax.nn.sigmoid(jnp.einsum("bshi,hij->bshj", ub, wa.astype(jnp.float32)).reshape(bsz, s, c) + ba.astype(jnp.float32))
    i = jax.nn.sigmoid(jnp.einsum("bshi,hij->bshj", ub, wx.astype(jnp.float32)).reshape(bsz, s, c) + bx.astype(jnp.float32))
    log_a = -RGLRU_C * r * jax.nn.softplus(-lam.astype(jnp.float32))
    a = jnp.exp(log_a)
    beta = jnp.sqrt(jnp.maximum(-jnp.expm1(2.0 * log_a), 0.0))
    _, h = lax.associative_scan(_linear_combine, (a, beta * (i * u)), axis=1, reverse=reverse)
    return h


def banded_alibi_sink_attention(q, k, v, sink):
    bsz, s = q.shape[0], q.shape[1]
    nb = s // BLOCK
    qb = (q.astype(jnp.float32) * (HEAD_DIM ** -0.5)).reshape(bsz, nb, BLOCK, N_KV_HEADS, GROUP, HEAD_DIM)

    def key_blocks(t):
        tp = jnp.pad(t.astype(jnp.float32), ((0, 0), (BLOCK, BLOCK), (0, 0), (0, 0)))
        tp = tp.reshape(bsz, nb + 2, BLOCK, N_KV_HEADS, HEAD_DIM)
        return jnp.concatenate([tp[:, j:j + nb] for j in range(3)], axis=2)

    kb = key_blocks(k)
    vb = key_blocks(v)
    scores = jnp.einsum("bnqkgd,bnskd->bnkgqs", qb, kb)

    q_loc = jnp.arange(BLOCK)
    k_loc = jnp.arange(3 * BLOCK)
    dist = q_loc[:, None] + BLOCK - k_loc[None, :]
    kpos = jnp.arange(nb)[:, None] * BLOCK - BLOCK + k_loc[None, :]
    valid = (jnp.abs(dist) <= WINDOW)[None] & ((kpos >= 0) & (kpos < s))[:, None, :]

    slopes = jnp.exp2(-8.0 * (jnp.arange(N_HEADS, dtype=jnp.float32) + 1.0) / N_HEADS)
    alibi = -slopes.reshape(N_KV_HEADS, GROUP, 1, 1) * jnp.abs(dist).astype(jnp.float32)
    scores = jnp.where(valid[None, :, None, None], scores + alibi, NEG_INF)

    sink_l = sink.astype(jnp.float32).reshape(1, 1, N_KV_HEADS, GROUP, 1, 1)
    m = jnp.maximum(jnp.max(scores, axis=-1, keepdims=True), sink_l)
    p = jnp.exp(scores - m)
    denom = jnp.sum(p, axis=-1, keepdims=True) + jnp.exp(sink_l - m)
    o = jnp.einsum("bnkgqs,bnskd->bnqkgd", p / denom, vb)
    return o.reshape(bsz, s, Q_W)


def _fwd_setup_inputs(seed: int = 0) -> dict:
    key = jax.random.key(seed)
    ks = jax.random.split(key, 20)
    f32 = jnp.float32
    x = jax.random.normal(ks[0], (BATCH, SEQ, D_MODEL), f32)
    norm_mix_g = 1.0 + 0.05 * jax.random.normal(ks[1], (DEPTH, D_MODEL), f32)
    w_in = jax.random.normal(ks[2], (DEPTH, D_MODEL, IN_W), f32) * D_MODEL ** -0.5
    b_gate = 0.01 * jax.random.normal(ks[3], (DEPTH, N_BRANCH * D_MODEL), f32)
    conv_w = jax.random.normal(ks[4], (DEPTH, CONV_WIDTH, LRU_WIDTH), f32) * CONV_WIDTH ** -0.5
    conv_b = 0.01 * jax.random.normal(ks[5], (DEPTH, LRU_WIDTH), f32)
    u = jax.random.uniform(ks[6], (DEPTH, N_DIR, LRU_WIDTH), f32, minval=0.9, maxval=0.999)
    p = u ** (1.0 / RGLRU_C)
    lru_lambda = jnp.log(p) - jnp.log1p(-p)
    lru_wa = jax.random.normal(ks[7], (DEPTH, N_DIR, LRU_HEADS, LRU_BLOCK, LRU_BLOCK), f32) * LRU_BLOCK ** -0.5
    lru_ba = 0.01 * jax.random.normal(ks[8], (DEPTH, N_DIR, LRU_WIDTH), f32)
    lru_wx = jax.random.normal(ks[9], (DEPTH, N_DIR, LRU_HEADS, LRU_BLOCK, LRU_BLOCK), f32) * LRU_BLOCK ** -0.5
    lru_bx = 0.01 * jax.random.normal(ks[10], (DEPTH, N_DIR, LRU_WIDTH), f32)
    attn_sink = 0.5 * jax.random.normal(ks[11], (DEPTH, N_HEADS), f32)
    w_out = jax.random.normal(ks[12], (DEPTH, D_MODEL, D_MODEL), f32) * D_MODEL ** -0.5
    norm_ffn_g = 1.0 + 0.05 * jax.random.normal(ks[13], (DEPTH, D_MODEL), f32)
    w_ffn_in = jax.random.normal(ks[14], (DEPTH, D_MODEL, 2 * D_FF), f32) * D_MODEL ** -0.5
    w_ffn_out = jax.random.normal(ks[15], (DEPTH, D_FF, D_MODEL), f32) * D_FF ** -0.5
    norm_final_g = 1.0 + 0.05 * jax.random.normal(ks[16], (D_MODEL,), f32)
    return {"x": x, "norm_mix_g": norm_mix_g, "w_in": w_in, "b_gate": b_gate,
            "conv_w": conv_w, "conv_b": conv_b, "lru_lambda": lru_lambda,
            "lru_wa": lru_wa, "lru_ba": lru_ba, "lru_wx": lru_wx, "lru_bx": lru_bx,
            "attn_sink": attn_sink, "w_out": w_out, "norm_ffn_g": norm_ffn_g,
            "w_ffn_in": w_ffn_in, "w_ffn_out": w_ffn_out, "norm_final_g": norm_final_g}


def _fwd_reference(x, norm_mix_g, w_in, b_gate, conv_w, conv_b, lru_lambda, lru_wa, lru_ba,
              lru_wx, lru_bx, attn_sink, w_out, norm_ffn_g, w_ffn_in, w_ffn_out, norm_final_g):
    bsz, s, _ = x.shape
    splits = [LRU_WIDTH, 2 * LRU_WIDTH, 2 * LRU_WIDTH + Q_W, 2 * LRU_WIDTH + Q_W + KV_W,
              2 * LRU_WIDTH + Q_W + 2 * KV_W]
    for l in range(DEPTH):
        xn = rmsnorm(x, norm_mix_g[l])
        proj = xn @ w_in[l]
        u, g_lru, q, k, v, z = jnp.split(proj, splits, axis=-1)

        uc = centred_depthwise_conv(u, conv_w[l], conv_b[l]).astype(jnp.float32)
        h_fwd = rg_lru(uc, lru_lambda[l, 0], lru_wa[l, 0], lru_ba[l, 0], lru_wx[l, 0], lru_bx[l, 0], False)
        h_bwd = rg_lru(uc, lru_lambda[l, 1], lru_wa[l, 1], lru_ba[l, 1], lru_wx[l, 1], lru_bx[l, 1], True)
        y_a = ((h_fwd + h_bwd) * jax.nn.gelu(g_lru.astype(jnp.float32))).astype(x.dtype)

        y_b = banded_alibi_sink_attention(
            q.reshape(bsz, s, N_HEADS, HEAD_DIM),
            k.reshape(bsz, s, N_KV_HEADS, HEAD_DIM),
            v.reshape(bsz, s, N_KV_HEADS, HEAD_DIM),
            attn_sink[l]).astype(x.dtype)

        gates = jax.nn.sigmoid(z + b_gate[l]).reshape(bsz, s, N_BRANCH, D_MODEL)
        merged = gates[:, :, 0] * y_a + gates[:, :, 1] * y_b
        x = x + merged @ w_out[l]

        xn2 = rmsnorm(x, norm_ffn_g[l])
        gu = xn2 @ w_ffn_in[l]
        ff_gate, ff_up = jnp.split(gu, [D_FF], axis=-1)
        x = x + (jax.nn.silu(ff_gate) * ff_up) @ w_ffn_out[l]
    return rmsnorm(x, norm_final_g)


import jax as _jax
import jax.numpy as _jnp

TWIN_FORMAT = 'train_step'
FWD_PARAMS = ['x', 'norm_mix_g', 'w_in', 'b_gate', 'conv_w', 'conv_b', 'lru_lambda', 'lru_wa', 'lru_ba', 'lru_wx', 'lru_bx', 'attn_sink', 'w_out', 'norm_ffn_g', 'w_ffn_in', 'w_ffn_out', 'norm_final_g']
TWIN_WEIGHTS = ['norm_mix_g', 'w_in', 'b_gate', 'conv_w', 'conv_b', 'lru_lambda', 'lru_wa', 'lru_ba', 'lru_wx', 'lru_bx', 'attn_sink', 'w_out', 'norm_ffn_g', 'w_ffn_in', 'w_ffn_out', 'norm_final_g']
TWIN_DIFF_INPUT = 'x'
TWIN_INPUTS = ['x', 'norm_mix_g', 'w_in', 'b_gate', 'conv_w', 'conv_b', 'lru_lambda', 'lru_wa', 'lru_ba', 'lru_wx', 'lru_bx', 'attn_sink', 'w_out', 'norm_ffn_g', 'w_ffn_in', 'w_ffn_out', 'norm_final_g', 'loss_target', 'm_norm_mix_g', 'm_w_in', 'm_b_gate', 'm_conv_w', 'm_conv_b', 'm_lru_lambda', 'm_lru_wa', 'm_lru_ba', 'm_lru_wx', 'm_lru_bx', 'm_attn_sink', 'm_w_out', 'm_norm_ffn_g', 'm_w_ffn_in', 'm_w_ffn_out', 'm_norm_final_g', 'v_norm_mix_g', 'v_w_in', 'v_b_gate', 'v_conv_w', 'v_conv_b', 'v_lru_lambda', 'v_lru_wa', 'v_lru_ba', 'v_lru_wx', 'v_lru_bx', 'v_attn_sink', 'v_w_out', 'v_norm_ffn_g', 'v_w_ffn_in', 'v_w_ffn_out', 'v_norm_final_g']
TWIN_OUTPUTS = ['loss', 'grad_x', 'grad_norm_mix_g', 'grad_w_in', 'grad_b_gate', 'grad_conv_w', 'grad_conv_b', 'grad_lru_lambda', 'grad_lru_wa', 'grad_lru_ba', 'grad_lru_wx', 'grad_lru_bx', 'grad_attn_sink', 'grad_w_out', 'grad_norm_ffn_g', 'grad_w_ffn_in', 'grad_w_ffn_out', 'grad_norm_final_g', 'delta_norm_mix_g', 'delta_w_in', 'delta_b_gate', 'delta_conv_w', 'delta_conv_b', 'delta_lru_lambda', 'delta_lru_wa', 'delta_lru_ba', 'delta_lru_wx', 'delta_lru_bx', 'delta_attn_sink', 'delta_w_out', 'delta_norm_ffn_g', 'delta_w_ffn_in', 'delta_w_ffn_out', 'delta_norm_final_g', 'new_m_norm_mix_g', 'new_m_w_in', 'new_m_b_gate', 'new_m_conv_w', 'new_m_conv_b', 'new_m_lru_lambda', 'new_m_lru_wa', 'new_m_lru_ba', 'new_m_lru_wx', 'new_m_lru_bx', 'new_m_attn_sink', 'new_m_w_out', 'new_m_norm_ffn_g', 'new_m_w_ffn_in', 'new_m_w_ffn_out', 'new_m_norm_final_g', 'new_v_norm_mix_g', 'new_v_w_in', 'new_v_b_gate', 'new_v_conv_w', 'new_v_conv_b', 'new_v_lru_lambda', 'new_v_lru_wa', 'new_v_lru_ba', 'new_v_lru_wx', 'new_v_lru_bx', 'new_v_attn_sink', 'new_v_w_out', 'new_v_norm_ffn_g', 'new_v_w_ffn_in', 'new_v_w_ffn_out', 'new_v_norm_final_g']
TWIN_LEAF_KINDS = {'loss': 'loss', 'grad_x': 'grad_x', 'grad_norm_mix_g': 'grad_w', 'grad_w_in': 'grad_w', 'grad_b_gate': 'grad_w', 'grad_conv_w': 'grad_w', 'grad_conv_b': 'grad_w', 'grad_lru_lambda': 'grad_w', 'grad_lru_wa': 'grad_w', 'grad_lru_ba': 'grad_w', 'grad_lru_wx': 'grad_w', 'grad_lru_bx': 'grad_w', 'grad_attn_sink': 'grad_w', 'grad_w_out': 'grad_w', 'grad_norm_ffn_g': 'grad_w', 'grad_w_ffn_in': 'grad_w', 'grad_w_ffn_out': 'grad_w', 'grad_norm_final_g': 'grad_w', 'delta_norm_mix_g': 'delta_w', 'delta_w_in': 'delta_w', 'delta_b_gate': 'delta_w', 'delta_conv_w': 'delta_w', 'delta_conv_b': 'delta_w', 'delta_lru_lambda': 'delta_w', 'delta_lru_wa': 'delta_w', 'delta_lru_ba': 'delta_w', 'delta_lru_wx': 'delta_w', 'delta_lru_bx': 'delta_w', 'delta_attn_sink': 'delta_w', 'delta_w_out': 'delta_w', 'delta_norm_ffn_g': 'delta_w', 'delta_w_ffn_in': 'delta_w', 'delta_w_ffn_out': 'delta_w', 'delta_norm_final_g': 'delta_w', 'new_m_norm_mix_g': 'new_m', 'new_m_w_in': 'new_m', 'new_m_b_gate': 'new_m', 'new_m_conv_w': 'new_m', 'new_m_conv_b': 'new_m', 'new_m_lru_lambda': 'new_m', 'new_m_lru_wa': 'new_m', 'new_m_lru_ba': 'new_m', 'new_m_lru_wx': 'new_m', 'new_m_lru_bx': 'new_m', 'new_m_attn_sink': 'new_m', 'new_m_w_out': 'new_m', 'new_m_norm_ffn_g': 'new_m', 'new_m_w_ffn_in': 'new_m', 'new_m_w_ffn_out': 'new_m', 'new_m_norm_final_g': 'new_m', 'new_v_norm_mix_g': 'new_v', 'new_v_w_in': 'new_v', 'new_v_b_gate': 'new_v', 'new_v_conv_w': 'new_v', 'new_v_conv_b': 'new_v', 'new_v_lru_lambda': 'new_v', 'new_v_lru_wa': 'new_v', 'new_v_lru_ba': 'new_v', 'new_v_lru_wx': 'new_v', 'new_v_lru_bx': 'new_v', 'new_v_attn_sink': 'new_v', 'new_v_w_out': 'new_v', 'new_v_norm_ffn_g': 'new_v', 'new_v_w_ffn_in': 'new_v', 'new_v_w_ffn_out': 'new_v', 'new_v_norm_final_g': 'new_v'}


def _forward(args):
    return _fwd_reference(*[args[k] for k in FWD_PARAMS])


def _output_shape():
    def fwd():
        inp = _fwd_setup_inputs(0)
        return _fwd_reference(*[inp[k] for k in FWD_PARAMS])
    out = _jax.eval_shape(fwd)
    return out.shape, out.dtype

N_MICROBATCH = 1
ADAM_LR = 0.001
ADAM_B1 = 0.9
ADAM_B2 = 0.999
ADAM_EPS = 1e-08
ADAM_WD = 0.01
ADAM_STEP = 10
PER_EXAMPLE_BATCH_AXIS = {'x': 0, 'loss_target': 0}
SHARED_INPUTS = []
_WEIGHT_DTYPES = {'norm_mix_g': _jnp.float32, 'w_in': _jnp.float32, 'b_gate': _jnp.float32, 'conv_w': _jnp.float32, 'conv_b': _jnp.float32, 'lru_lambda': _jnp.float32, 'lru_wa': _jnp.float32, 'lru_ba': _jnp.float32, 'lru_wx': _jnp.float32, 'lru_bx': _jnp.float32, 'attn_sink': _jnp.float32, 'w_out': _jnp.float32, 'norm_ffn_g': _jnp.float32, 'w_ffn_in': _jnp.float32, 'w_ffn_out': _jnp.float32, 'norm_final_g': _jnp.float32}
MOMENT_SCALE = {'norm_mix_g': 1.438187e-01, 'w_in': 6.406677e-02, 'b_gate': 3.298560e-02, 'conv_w': 1.063136e-01, 'conv_b': 1.451161e+00, 'lru_lambda': 3.930376e-02, 'lru_wa': 2.884476e-02, 'lru_ba': 2.125680e-02, 'lru_wx': 5.421732e-02, 'lru_bx': 2.510325e-02, 'attn_sink': 7.240732e-02, 'w_out': 1.103423e-01, 'norm_ffn_g': 1.911176e-01, 'w_ffn_in': 7.655820e-02, 'w_ffn_out': 1.250623e-01, 'norm_final_g': 6.412987e+01}


def _to_microbatches(a, axis):
    t = _jnp.moveaxis(a, axis, 0)
    t = t.reshape((N_MICROBATCH, t.shape[0] // N_MICROBATCH) + t.shape[1:])
    return _jnp.moveaxis(t, 1, axis + 1)


def setup_inputs(seed: int = 0) -> dict:
    inp = _fwd_setup_inputs(seed)
    key = _jax.random.fold_in(_jax.random.key(seed), 7919)
    shape, _ = _output_shape()
    out = dict(inp)
    out["loss_target"] = _jax.random.normal(_jax.random.fold_in(key, 0), shape, _jnp.float32)
    for i, name in enumerate(TWIN_WEIGHTS):
        w = inp[name].astype(_jnp.float32)
        if MOMENT_SCALE is None:
            s = _jnp.sqrt(_jnp.mean(_jnp.square(w)) + 1e-30)
        else:
            s = MOMENT_SCALE[name]
        km, kv = _jax.random.split(_jax.random.fold_in(key, i + 1))
        out[name] = w
        out["m_" + name] = s * _jax.random.normal(km, w.shape, _jnp.float32)
        out["v_" + name] = (s * s) * _jax.random.uniform(kv, w.shape, _jnp.float32, 0.5, 1.5)
    if N_MICROBATCH > 1:
        for name, axis in PER_EXAMPLE_BATCH_AXIS.items():
            out[name] = _to_microbatches(out[name], axis)
    return {'x': out['x'], 'norm_mix_g': out['norm_mix_g'], 'w_in': out['w_in'], 'b_gate': out['b_gate'], 'conv_w': out['conv_w'], 'conv_b': out['conv_b'], 'lru_lambda': out['lru_lambda'], 'lru_wa': out['lru_wa'], 'lru_ba': out['lru_ba'], 'lru_wx': out['lru_wx'], 'lru_bx': out['lru_bx'], 'attn_sink': out['attn_sink'], 'w_out': out['w_out'], 'norm_ffn_g': out['norm_ffn_g'], 'w_ffn_in': out['w_ffn_in'], 'w_ffn_out': out['w_ffn_out'], 'norm_final_g': out['norm_final_g'], 'loss_target': out['loss_target'], 'm_norm_mix_g': out['m_norm_mix_g'], 'm_w_in': out['m_w_in'], 'm_b_gate': out['m_b_gate'], 'm_conv_w': out['m_conv_w'], 'm_conv_b': out['m_conv_b'], 'm_lru_lambda': out['m_lru_lambda'], 'm_lru_wa': out['m_lru_wa'], 'm_lru_ba': out['m_lru_ba'], 'm_lru_wx': out['m_lru_wx'], 'm_lru_bx': out['m_lru_bx'], 'm_attn_sink': out['m_attn_sink'], 'm_w_out': out['m_w_out'], 'm_norm_ffn_g': out['m_norm_ffn_g'], 'm_w_ffn_in': out['m_w_ffn_in'], 'm_w_ffn_out': out['m_w_ffn_out'], 'm_norm_final_g': out['m_norm_final_g'], 'v_norm_mix_g': out['v_norm_mix_g'], 'v_w_in': out['v_w_in'], 'v_b_gate': out['v_b_gate'], 'v_conv_w': out['v_conv_w'], 'v_conv_b': out['v_conv_b'], 'v_lru_lambda': out['v_lru_lambda'], 'v_lru_wa': out['v_lru_wa'], 'v_lru_ba': out['v_lru_ba'], 'v_lru_wx': out['v_lru_wx'], 'v_lru_bx': out['v_lru_bx'], 'v_attn_sink': out['v_attn_sink'], 'v_w_out': out['v_w_out'], 'v_norm_ffn_g': out['v_norm_ffn_g'], 'v_w_ffn_in': out['v_w_ffn_in'], 'v_w_ffn_out': out['v_w_ffn_out'], 'v_norm_final_g': out['v_norm_final_g']}


def _loss(weights, diff, rest, loss_target):
    with _jax.named_scope("forward"):
        args = {**rest, TWIN_DIFF_INPUT: diff, **{k: w.astype(_WEIGHT_DTYPES[k]) for k, w in weights.items()}}
        y = _forward(args)
    with _jax.named_scope("loss_head"):
        err = _jnp.square(y.astype(_jnp.float32) - loss_target)
        return 0.5 * _jnp.sum(_jnp.mean(err, axis=-1)) if err.ndim else 0.5 * err


def _adamw(w, g, m, v):
    m = ADAM_B1 * m + (1.0 - ADAM_B1) * g
    v = ADAM_B2 * v + (1.0 - ADAM_B2) * _jnp.square(g)
    m_hat = m / (1.0 - ADAM_B1 ** ADAM_STEP)
    v_hat = v / (1.0 - ADAM_B2 ** ADAM_STEP)
    delta = -ADAM_LR * (m_hat / (_jnp.sqrt(v_hat) + ADAM_EPS) + ADAM_WD * w)
    return delta, m, v


def reference(x, norm_mix_g, w_in, b_gate, conv_w, conv_b, lru_lambda, lru_wa, lru_ba, lru_wx, lru_bx, attn_sink, w_out, norm_ffn_g, w_ffn_in, w_ffn_out, norm_final_g, loss_target, m_norm_mix_g, m_w_in, m_b_gate, m_conv_w, m_conv_b, m_lru_lambda, m_lru_wa, m_lru_ba, m_lru_wx, m_lru_bx, m_attn_sink, m_w_out, m_norm_ffn_g, m_w_ffn_in, m_w_ffn_out, m_norm_final_g, v_norm_mix_g, v_w_in, v_b_gate, v_conv_w, v_conv_b, v_lru_lambda, v_lru_wa, v_lru_ba, v_lru_wx, v_lru_bx, v_attn_sink, v_w_out, v_norm_ffn_g, v_w_ffn_in, v_w_ffn_out, v_norm_final_g):
    given = dict(x=x, norm_mix_g=norm_mix_g, w_in=w_in, b_gate=b_gate, conv_w=conv_w, conv_b=conv_b, lru_lambda=lru_lambda, lru_wa=lru_wa, lru_ba=lru_ba, lru_wx=lru_wx, lru_bx=lru_bx, attn_sink=attn_sink, w_out=w_out, norm_ffn_g=norm_ffn_g, w_ffn_in=w_ffn_in, w_ffn_out=w_ffn_out, norm_final_g=norm_final_g, loss_target=loss_target, m_norm_mix_g=m_norm_mix_g, m_w_in=m_w_in, m_b_gate=m_b_gate, m_conv_w=m_conv_w, m_conv_b=m_conv_b, m_lru_lambda=m_lru_lambda, m_lru_wa=m_lru_wa, m_lru_ba=m_lru_ba, m_lru_wx=m_lru_wx, m_lru_bx=m_lru_bx, m_attn_sink=m_attn_sink, m_w_out=m_w_out, m_norm_ffn_g=m_norm_ffn_g, m_w_ffn_in=m_w_ffn_in, m_w_ffn_out=m_w_ffn_out, m_norm_final_g=m_norm_final_g, v_norm_mix_g=v_norm_mix_g, v_w_in=v_w_in, v_b_gate=v_b_gate, v_conv_w=v_conv_w, v_conv_b=v_conv_b, v_lru_lambda=v_lru_lambda, v_lru_wa=v_lru_wa, v_lru_ba=v_lru_ba, v_lru_wx=v_lru_wx, v_lru_bx=v_lru_bx, v_attn_sink=v_attn_sink, v_w_out=v_w_out, v_norm_ffn_g=v_norm_ffn_g, v_w_ffn_in=v_w_ffn_in, v_w_ffn_out=v_w_ffn_out, v_norm_final_g=v_norm_final_g)
    weights = {n: given[n] for n in TWIN_WEIGHTS}
    shared = {n: given[n] for n in SHARED_INPUTS}
    per_example = {n: given[n] for n in ['x']}
    grad_fn = _jax.value_and_grad(_loss, argnums=(0, 1))

    def one_microbatch(ex, loss_target):
        ex = dict(ex)
        diff = ex.pop(TWIN_DIFF_INPUT)
        return grad_fn(weights, diff, {**shared, **ex}, loss_target)

    if N_MICROBATCH == 1:
        loss, (grad_w, grad_x) = one_microbatch(per_example, given["loss_target"])
    else:
        def body(carry, xs):
            loss_sum, grad_sum = carry
            l_k, (gw_k, gx_k) = one_microbatch(xs[0], xs[1])
            with _jax.named_scope("update"):
                return (loss_sum + l_k, _jax.tree.map(_jnp.add, grad_sum, gw_k)), gx_k

        init = (_jnp.zeros((), _jnp.float32), _jax.tree.map(_jnp.zeros_like, weights))
        (loss, grad_w), grad_x = _jax.lax.scan(body, init, (per_example, given["loss_target"]))
    with _jax.named_scope("update"):
        delta_w, new_m, new_v = {}, {}, {}
        for n in TWIN_WEIGHTS:
            delta_w[n], new_m[n], new_v[n] = _adamw(weights[n], grad_w[n], given["m_" + n], given["v_" + n])
    return (loss, grad_x, *[grad_w[n] for n in TWIN_WEIGHTS], *[delta_w[n] for n in TWIN_WEIGHTS],
            *[new_m[n] for n in TWIN_WEIGHTS], *[new_v[n] for n in TWIN_WEIGHTS])
```

```python
import functools
import math

import jax
import jax.numpy as jnp
from jax import lax
from jax.experimental import pallas as pl
from jax.experimental.pallas import tpu as pltpu

f32 = jnp.float32
bf16 = jnp.bfloat16

D = 1024
NH = 16
HD = 64
FF = 2816
INW = 5632
NCHIP = 4
SHW = INW // NCHIP
CW = 128
NCH = D // CW
BLK = 128
EPS = 1e-6
NEG_INF = -1e30
RGLRU_C = 8.0
ADAM_LR, ADAM_B1, ADAM_B2, ADAM_EPS, ADAM_WD, ADAM_STEP = 0.001, 0.9, 0.999, 1e-08, 0.01, 10
VMEM_LIMIT = 58 * 1024 * 1024
MESH = pl.DeviceIdType.MESH
ANY = pl.BlockSpec(memory_space=pl.ANY)

COL_U, COL_G, COL_Q, COL_K, COL_V, COL_Z0, COL_Z1 = 0, 4, 8, 12, 13, 14, 18


def _params(n_axes, vmem=False):
    return pltpu.CompilerParams(dimension_semantics=("arbitrary",) * n_axes,
                                vmem_limit_bytes=VMEM_LIMIT if vmem else None)


def _sds(shape, dtype):
    return jax.ShapeDtypeStruct(tuple(shape), dtype)


_DIMS = {"nn": (((1,), (0,)), ((), ())), "nt": (((1,), (1,)), ((), ())), "tn": (((0,), (0,)), ((), ()))}


def _mm(name, mode, a, a_spec, b, b_spec, out_shape, out_spec, grid, nk, acc_shape, add=None, add_spec=None):
    has_add = add is not None

    def body(*refs):
        a_ref, b_ref = refs[0], refs[1]
        add_ref = refs[2] if has_add else None
        o_ref = refs[2 + has_add]
        part = lax.dot_general(a_ref[...].astype(bf16), b_ref[...].astype(bf16), _DIMS[mode],
                               preferred_element_type=f32)
        if nk == 1:
            if has_add:
                part = add_ref[...] + part
            o_ref[...] = part.astype(o_ref.dtype)
            return
        acc_ref = refs[3 + has_add]
        k = pl.program_id(len(grid) - 1)

        @pl.when(k == 0)
        def _():
            acc_ref[...] = part

        @pl.when(k > 0)
        def _():
            acc_ref[...] += part

        @pl.when(k == nk - 1)
        def _():
            res = acc_ref[...]
            if has_add:
                res = add_ref[...] + res
            o_ref[...] = res.astype(o_ref.dtype)

    ins = [a, b] + ([add] if has_add else [])
    in_specs = [a_spec, b_spec] + ([add_spec] if has_add else [])
    scratch = [pltpu.VMEM(acc_shape, f32)] if nk > 1 else []
    return pl.pallas_call(body, name=name, grid=grid, in_specs=in_specs, out_specs=out_spec, out_shape=out_shape,
                          scratch_shapes=scratch, compiler_params=_params(len(grid), True))(*ins)


def _rms_matmul(name, x, g, w3, tm):
    S, K = x.shape
    G, _, Nw = w3.shape
    tm = min(tm, S)

    def body(x_ref, g_ref, w_ref, xn_ref, o_ref, xs_ref):
        @pl.when(pl.program_id(1) == 0)
        def _():
            xf = x_ref[...]
            r = lax.rsqrt(jnp.mean(xf * xf, axis=-1, keepdims=True) + EPS)
            xn = ((xf * r) * g_ref[...]).astype(bf16)
            xs_ref[...] = xn
            xn_ref[...] = xn

        o_ref[...] = jnp.dot(xs_ref[...], w_ref[...], preferred_element_type=f32)

    return pl.pallas_call(
        body, name=name, grid=(S // tm, G),
        in_specs=[pl.BlockSpec((tm, K), lambda i, j: (i, 0)), pl.BlockSpec((1, K), lambda i, j: (0, 0)),
                  pl.BlockSpec((None, K, Nw), lambda i, j: (j, 0, 0))],
        out_specs=[pl.BlockSpec((tm, K), lambda i, j: (i, 0)), pl.BlockSpec((tm, Nw), lambda i, j: (i, j))],
        out_shape=[_sds((S, K), bf16), _sds((S, G * Nw), f32)],
        scratch_shapes=[pltpu.VMEM((tm, K), bf16)], compiler_params=_params(2, True))(x, g, w3)


def _mm_residual(name, a, w, res, tm):
    S, K = a.shape
    N = w.shape[1]
    tm = min(tm, S)
    return _mm(name, "nn", a, pl.BlockSpec((tm, K), lambda i: (i, 0)), w, pl.BlockSpec((K, N), lambda i: (0, 0)),
               _sds((S, N), f32), pl.BlockSpec((tm, N), lambda i: (i, 0)), (S // tm,), 1, None,
               add=res, add_spec=pl.BlockSpec((tm, N), lambda i: (i, 0)))


def _mm_nt_resident(name, a, w, tm):
    S, K = a.shape
    N = w.shape[0]
    tm = min(tm, S)
    return _mm(name, "nt", a, pl.BlockSpec((tm, K), lambda i: (i, 0)), w, pl.BlockSpec((N, K), lambda i: (0, 0)),
               _sds((S, N), f32), pl.BlockSpec((tm, N), lambda i: (i, 0)), (S // tm,), 1, None)


def _mm_nt_groups(name, a, a_spec, w3, S, tm):
    G, Dout, Kw = w3.shape
    return _mm(name, "nt", a, a_spec, w3, pl.BlockSpec((None, Dout, Kw), lambda i, g: (g, 0, 0)),
               _sds((S, Dout), f32), pl.BlockSpec((tm, Dout), lambda i, g: (i, 0)), (S // tm, G), G, (tm, Dout))


def _mm_tn(name, a, a_spec, b, b_spec, out_shape, out_spec, grid, acc_shape):
    return _mm(name, "tn", a, a_spec, b, b_spec, out_shape, out_spec, grid, grid[-1], acc_shape)


def _sigmoid(x):
    return jax.nn.sigmoid(x)


_GELU_C = math.sqrt(2.0 / math.pi)


def _gelu_and_grad(x):
    v = _GELU_C * (x + 0.044715 * (x * x * x))
    t = jnp.tanh(v)
    gl = 0.5 * x * (1.0 + t)
    dgl = 0.5 * (1.0 + t) + 0.5 * x * (1.0 - t * t) * (_GELU_C * (1.0 + 3.0 * 0.044715 * (x * x)))
    return gl, dgl


def _neg_expm1(x):
    series = x * (1.0 + x * 0.5 * (1.0 + x * (1.0 / 3.0) * (1.0 + x * 0.25 * (1.0 + x * 0.2 * (
        1.0 + x * (1.0 / 6.0) * (1.0 + x * (1.0 / 7.0)))))))
    return -jnp.where(x > -0.25, series, jnp.exp(x) - 1.0)


def _merge_fwd(proj, b_gate, y_a, y_b, tm):
    S = proj.shape[0]
    tm = min(tm, S)

    def body(z0_ref, z1_ref, b0_ref, b1_ref, ya_ref, yb_ref, o_ref):
        g0 = _sigmoid(z0_ref[...] + b0_ref[...])
        g1 = _sigmoid(z1_ref[...] + b1_ref[...])
        o_ref[...] = (g0 * ya_ref[...] + g1 * yb_ref[...]).astype(bf16)

    blk = lambda off: pl.BlockSpec((tm, 256), lambda j, i: (i, off + j))
    vec = lambda off: pl.BlockSpec((1, 256), lambda j, i: (0, off + j))
    return pl.pallas_call(body, name="merge_fwd", grid=(4, S // tm),
                          in_specs=[blk(COL_Z0), blk(COL_Z1), vec(0), vec(4), blk(0), blk(0)],
                          out_specs=blk(0), out_shape=_sds((S, D), bf16),
                          compiler_params=_params(2))(proj, proj, b_gate, b_gate, y_a, y_b)


def _merge_bwd(proj, b_gate, y_a, y_b, dm, tm):
    S = proj.shape[0]
    tm = min(tm, S)

    def body(z0_ref, z1_ref, b0_ref, b1_ref, ya_ref, yb_ref, dm_ref, dz0_ref, dz1_ref, dya_ref, dyb_ref, db0_ref, db1_ref):
        g0 = _sigmoid(z0_ref[...] + b0_ref[...])
        g1 = _sigmoid(z1_ref[...] + b1_ref[...])
        d = dm_ref[...]
        dz0 = (d * ya_ref[...]) * (g0 * (1.0 - g0))
        dz1 = (d * yb_ref[...]) * (g1 * (1.0 - g1))
        dz0_ref[...] = dz0.astype(bf16)
        dz1_ref[...] = dz1.astype(bf16)
        dya_ref[...] = d * g0
        dyb_ref[...] = d * g1

        @pl.when(pl.program_id(1) == 0)
        def _():
            db0_ref[...] = jnp.zeros_like(db0_ref)
            db1_ref[...] = jnp.zeros_like(db1_ref)

        db0_ref[...] += jnp.sum(dz0, axis=0, keepdims=True)
        db1_ref[...] += jnp.sum(dz1, axis=0, keepdims=True)

    blk = lambda off: pl.BlockSpec((tm, 256), lambda j, i: (i, off + j))
    vec = lambda off: pl.BlockSpec((1, 256), lambda j, i: (0, off + j))
    return pl.pallas_call(
        body, name="merge_bwd", grid=(4, S // tm),
        in_specs=[blk(COL_Z0), blk(COL_Z1), vec(0), vec(4), blk(0), blk(0), blk(0)],
        out_specs=[blk(0), blk(0), blk(0), blk(0), vec(0), vec(0)],
        out_shape=[_sds((S, D), bf16), _sds((S, D), bf16), _sds((S, D), f32), _sds((S, D), f32),
                   _sds((1, D), f32), _sds((1, D), f32)],
        compiler_params=_params(2))(proj, proj, b_gate, b_gate, y_a, y_b, dm)


def _swiglu_fwd(gu, tm):
    S = gu.shape[0]
    tm = min(tm, S)
    nj = FF // 256

    def body(g_ref, u_ref, o_ref):
        g = g_ref[...]
        o_ref[...] = ((g * _sigmoid(g)) * u_ref[...]).astype(bf16)

    return pl.pallas_call(body, name="swiglu_fwd", grid=(nj, S // tm),
                          in_specs=[pl.BlockSpec((tm, 256), lambda j, i: (i, j)),
                                    pl.BlockSpec((tm, 256), lambda j, i: (i, nj + j))],
                          out_specs=pl.BlockSpec((tm, 256), lambda j, i: (i, j)), out_shape=_sds((S, FF), bf16),
                          compiler_params=_params(2))(gu, gu)


def _swiglu_bwd(gu, d_act, tm):
    S = gu.shape[0]
    tm = min(tm, S)
    nj = FF // 256

    def body(g_ref, u_ref, d_ref, o_ref):
        g = g_ref[...]
        s = _sigmoid(g)
        d = d_ref[...]
        o_ref[0] = ((d * u_ref[...]) * (s * (1.0 + g * (1.0 - s)))).astype(bf16)
        o_ref[1] = (d * (g * s)).astype(bf16)

    return pl.pallas_call(body, name="swiglu_bwd", grid=(nj, S // tm),
                          in_specs=[pl.BlockSpec((tm, 256), lambda j, i: (i, j)),
                                    pl.BlockSpec((tm, 256), lambda j, i: (i, nj + j)),
                                    pl.BlockSpec((tm, 256), lambda j, i: (i, j))],
                          out_specs=pl.BlockSpec((2, tm, 256), lambda j, i: (0, i, j)),
                          out_shape=_sds((2, S, FF), bf16), compiler_params=_params(2))(gu, gu, d_act)


def _final_loss_bwd(x2, g3, tgt, tm):
    S = x2.shape[0]
    tm = min(tm, S)

    def body(x_ref, g_ref, t_ref, dx_ref, loss_ref, dg_ref):
        @pl.when(pl.program_id(0) == 0)
        def _():
            loss_ref[...] = jnp.zeros_like(loss_ref)
            dg_ref[...] = jnp.zeros_like(dg_ref)

        x = x_ref[...]
        g = g_ref[...]
        r = lax.rsqrt(jnp.mean(x * x, axis=-1, keepdims=True) + EPS)
        xh = x * r
        err = xh * g - t_ref[...]
        row = jnp.mean(err * err, axis=-1, keepdims=True)
        loss_ref[...] += 0.5 * jnp.sum(row, axis=0, keepdims=True)
        dy = err * (1.0 / D)
        dg_ref[...] += jnp.sum(dy * xh, axis=0, keepdims=True)
        dxh = dy * g
        dx_ref[...] = r * (dxh - xh * jnp.mean(dxh * xh, axis=-1, keepdims=True))

    row_blk = pl.BlockSpec((tm, D), lambda i: (i, 0))
    vec = pl.BlockSpec((1, D), lambda i: (0, 0))
    return pl.pallas_call(body, name="final_loss_bwd", grid=(S // tm,), in_specs=[row_blk, vec, row_blk],
                          out_specs=[row_blk, pl.BlockSpec((1, 128), lambda i: (0, 0)), vec],
                          out_shape=[_sds((S, D), f32), _sds((1, 128), f32), _sds((1, D), f32)],
                          compiler_params=_params(1))(x2, g3, tgt)


def _rms_bwd(name, x, g, dxn, dres, tm):
    S = x.shape[0]
    tm = min(tm, S)

    def body(x_ref, g_ref, d_ref, r_ref, dx_ref, dg_ref):
        @pl.when(pl.program_id(0) == 0)
        def _():
            dg_ref[...] = jnp.zeros_like(dg_ref)

        x = x_ref[...]
        d = d_ref[...]
        r = lax.rsqrt(jnp.mean(x * x, axis=-1, keepdims=True) + EPS)
        xh = x * r
        dg_ref[...] += jnp.sum(d * xh, axis=0, keepdims=True)
        dxh = d * g_ref[...]
        dx_ref[...] = r_ref[...] + r * (dxh - xh * jnp.mean(dxh * xh, axis=-1, keepdims=True))

    row_blk = pl.BlockSpec((tm, D), lambda i: (i, 0))
    vec = pl.BlockSpec((1, D), lambda i: (0, 0))
    return pl.pallas_call(body, name=name, grid=(S // tm,), in_specs=[row_blk, vec, row_blk, row_blk],
                          out_specs=[row_blk, vec], out_shape=[_sds((S, D), f32), _sds((1, D), f32)],
                          compiler_params=_params(1))(x, g, dxn, dres)


LRU_TT = 256


def _halo(ref, i, S):
    nt = S // LRU_TT
    t0 = pl.multiple_of(i * LRU_TT, LRU_TT)
    p0 = pl.multiple_of(jnp.maximum(t0 - 8, 0), 8)
    n0 = pl.multiple_of(jnp.minimum(t0 + LRU_TT, S - 8), 8)
    prev = jnp.where(i > 0, ref[pl.ds(p0, 8), :], 0.0)
    nxt = jnp.where(i < nt - 1, ref[pl.ds(n0, 8), :], 0.0)
    return jnp.concatenate([prev, ref[pl.ds(t0, LRU_TT), :], nxt], axis=0)


def _shift(ext, k):
    n = LRU_TT + 16
    return pltpu.roll(ext, (-k) % n, 0)[8:8 + LRU_TT]


def _lru_gates(uc, wbd, ba, bx):
    pre = jnp.dot(uc.astype(bf16), wbd, preferred_element_type=f32)
    r_f = _sigmoid(pre[:, 0:CW] + ba[0:1])
    i_f = _sigmoid(pre[:, CW:2 * CW] + bx[0:1])
    r_b = _sigmoid(pre[:, 2 * CW:3 * CW] + ba[1:2])
    i_b = _sigmoid(pre[:, 3 * CW:4 * CW] + bx[1:2])
    return r_f, i_f, r_b, i_b


def _lru_coeffs(r, sp):
    log_a = (-RGLRU_C * r) * sp
    a = jnp.exp(log_a)
    beta = jnp.sqrt(jnp.maximum(_neg_expm1(2.0 * log_a), 0.0))
    return a, beta


def _conv_tile(u_ref, i, S, cw, cb):
    ext = _halo(u_ref, i, S)
    um2, um1, u0, up1 = _shift(ext, -2), _shift(ext, -1), ext[8:8 + LRU_TT], _shift(ext, 1)
    uc = um2 * cw[0:1] + um1 * cw[1:2] + u0 * cw[2:3] + up1 * cw[3:4] + cb
    return uc, (um2, um1, u0, up1)


def _scan_pair(S, fwd_a, fwd_b, fwd_out, rev_a, rev_b, rev_out):
    ng = S // 8
    idx = lax.broadcasted_iota(jnp.int32, (8, CW), 0)

    def local(a, b, rev):
        for sh in (1, 2, 4):
            if rev:
                keep = idx < 8 - sh
                amt = 8 - sh
            else:
                keep = idx >= sh
                amt = sh
            a_s = jnp.where(keep, pltpu.roll(a, amt, 0), 1.0)
            b_s = jnp.where(keep, pltpu.roll(b, amt, 0), 0.0)
            b = a * b_s + b
            a = a * a_s
        return a, b

    def step(g, carry):
        cf, cr = carry
        r0 = pl.multiple_of(g * 8, 8)
        a, b = local(fwd_a(r0), fwd_b(r0), False)
        h = a * cf + b
        fwd_out[pl.ds(r0, 8), :] = h
        cf = jnp.broadcast_to(h[7:8, :], (8, CW))
        r1 = pl.multiple_of((ng - 1 - g) * 8, 8)
        a, b = local(rev_a(r1), rev_b(r1), True)
        h = a * cr + b
        rev_out[pl.ds(r1, 8), :] = h
        cr = jnp.broadcast_to(h[0:1, :], (8, CW))
        return cf, cr

    zero = jnp.zeros((8, CW), f32)
    lax.fori_loop(0, ng, step, (zero, zero))


def _lru_specs(S):
    seq = lambda off: pl.BlockSpec((S, CW), lambda j: (0, off + j))
    par = lambda rows: pl.BlockSpec((rows, CW), lambda j: (0, j))
    return seq, par


def _lru_fwd(proj, conv_w, conv_b, lam, ba, bx, wbd):
    S = proj.shape[0]
    nt = S // LRU_TT

    def body(u_ref, g_ref, cw_ref, cb_ref, lam_ref, ba_ref, bx_ref, wbd_ref, y_ref, af_ref, bf_ref, ab_ref, bb_ref):
        cw, cb, ba_v, bx_v, wbd_v = cw_ref[...], cb_ref[...], ba_ref[...], bx_ref[...], wbd_ref[...]
        sp = jax.nn.softplus(-lam_ref[...])

        def phase1(i, c):
            uc, _ = _conv_tile(u_ref, i, S, cw, cb)
            r_f, i_f, r_b, i_b = _lru_gates(uc, wbd_v, ba_v, bx_v)
            rows = pl.ds(pl.multiple_of(i * LRU_TT, LRU_TT), LRU_TT)
            a, beta = _lru_coeffs(r_f, sp[0:1])
            af_ref[rows, :] = a
            bf_ref[rows, :] = beta * (i_f * uc)
            a, beta = _lru_coeffs(r_b, sp[1:2])
            ab_ref[rows, :] = a
            bb_ref[rows, :] = beta * (i_b * uc)
            return c

        lax.fori_loop(0, nt, phase1, 0)
        row8 = lambda ref: (lambda r0: ref[pl.ds(r0, 8), :])
        _scan_pair(S, row8(af_ref), row8(bf_ref), bf_ref, row8(ab_ref), row8(bb_ref), bb_ref)

        def phase3(i, c):
            rows = pl.ds(pl.multiple_of(i * LRU_TT, LRU_TT), LRU_TT)
            y_ref[rows, :] = (bf_ref[rows, :] + bb_ref[rows, :]) * jax.nn.gelu(g_ref[rows, :])
            return c

        lax.fori_loop(0, nt, phase3, 0)

    seq, par = _lru_specs(S)
    return pl.pallas_call(
        body, name="lru_fwd", grid=(NCH,),
        in_specs=[seq(0), seq(NCH), par(4), par(1), par(2), par(2), par(2),
                  pl.BlockSpec((None, CW, 4 * CW), lambda j: (j, 0, 0))],
        out_specs=seq(0), out_shape=_sds((S, D), f32),
        scratch_shapes=[pltpu.VMEM((S, CW), f32)] * 4, compiler_params=_params(1, True),
    )(proj, proj, conv_w, conv_b, lam, ba, bx, wbd)


def _lru_bwd(proj, dy, conv_w, conv_b, lam, ba, bx, wbd):
    S = proj.shape[0]
    nt = S // LRU_TT

    def body(u_ref, g_ref, dy_ref, cw_ref, cb_ref, lam_ref, ba_ref, bx_ref, wbd_ref,
             du_ref, dg_ref, dcw_ref, dcb_ref, dlam_ref, dba_ref, dbx_ref, dwbd_ref,
             af_ref, bf_ref, ab_ref, bb_ref):
        cw, cb, ba_v, bx_v, wbd_v = cw_ref[...], cb_ref[...], ba_ref[...], bx_ref[...], wbd_ref[...]
        lam_v = lam_ref[...]
        sp = jax.nn.softplus(-lam_v)

        def phase1(i, c):
            uc, _ = _conv_tile(u_ref, i, S, cw, cb)
            r_f, i_f, r_b, i_b = _lru_gates(uc, wbd_v, ba_v, bx_v)
            rows = pl.ds(pl.multiple_of(i * LRU_TT, LRU_TT), LRU_TT)
            a, beta = _lru_coeffs(r_f, sp[0:1])
            af_ref[rows, :] = a
            bf_ref[rows, :] = beta * (i_f * uc)
            a, beta = _lru_coeffs(r_b, sp[1:2])
            ab_ref[rows, :] = a
            bb_ref[rows, :] = beta * (i_b * uc)
            return c

        lax.fori_loop(0, nt, phase1, 0)
        row8 = lambda ref: (lambda r0: ref[pl.ds(r0, 8), :])
        _scan_pair(S, row8(af_ref), row8(bf_ref), bf_ref, row8(ab_ref), row8(bb_ref), bb_ref)

        def scaled_dh(a_ref):
            def f(r0):
                gl, _ = _gelu_and_grad(g_ref[pl.ds(r0, 8), :])
                return a_ref[pl.ds(r0, 8), :] * (dy_ref[pl.ds(r0, 8), :] * gl)
            return f

        _scan_pair(S, row8(ab_ref), scaled_dh(ab_ref), ab_ref, row8(af_ref), scaled_dh(af_ref), af_ref)

        dcw_ref[...] = jnp.zeros_like(dcw_ref)
        dcb_ref[...] = jnp.zeros_like(dcb_ref)
        dlam_ref[...] = jnp.zeros_like(dlam_ref)
        dba_ref[...] = jnp.zeros_like(dba_ref)
        dbx_ref[...] = jnp.zeros_like(dbx_ref)
        dwbd_ref[...] = jnp.zeros_like(dwbd_ref)

        def direction(uc, r, i_g, dht, h_nb, sp_d):
            a, beta = _lru_coeffs(r, sp_d)
            da = dht * h_nb
            dbeta = dht * (i_g * uc)
            d_iu = dht * beta
            dlog_a = da * a - (a * a) * (dbeta / beta)
            dr = dlog_a * (-RGLRU_C * sp_d)
            dsp = jnp.sum(dlog_a * (-RGLRU_C * r), axis=0, keepdims=True)
            dpre_r = dr * (r * (1.0 - r))
            dpre_i = (d_iu * uc) * (i_g * (1.0 - i_g))
            return dpre_r, dpre_i, d_iu * i_g, dsp

        def phase4(i, c):
            uc, (um2, um1, u0, up1) = _conv_tile(u_ref, i, S, cw, cb)
            r_f, i_f, r_b, i_b = _lru_gates(uc, wbd_v, ba_v, bx_v)
            rows = pl.ds(pl.multiple_of(i * LRU_TT, LRU_TT), LRU_TT)
            gl, dgl = _gelu_and_grad(g_ref[rows, :])
            dyt = dy_ref[rows, :]
            dh = dyt * gl
            dg_ref[rows, :] = ((dyt * (bf_ref[rows, :] + bb_ref[rows, :])) * dgl).astype(dg_ref.dtype)
            dht_f = dh + _shift(_halo(af_ref, i, S), 1)
            h_prev = _shift(_halo(bf_ref, i, S), -1)
            dht_b = dh + _shift(_halo(ab_ref, i, S), -1)
            h_next = _shift(_halo(bb_ref, i, S), 1)
            prf, pif, duc_f, dsp_f = direction(uc, r_f, i_f, dht_f, h_prev, sp[0:1])
            prb, pib, duc_b, dsp_b = direction(uc, r_b, i_b, dht_b, h_next, sp[1:2])
            dpre = jnp.concatenate([prf, pif, prb, pib], axis=1)
            dpre_b = dpre.astype(bf16)
            duc = (duc_f + duc_b) + lax.dot_general(dpre_b, wbd_v, _DIMS["nt"], preferred_element_type=f32)
            dwbd_ref[...] += lax.dot_general(uc.astype(bf16), dpre_b, _DIMS["tn"], preferred_element_type=f32)
            colsum = lambda v: jnp.sum(v, axis=0, keepdims=True)
            dba_ref[...] += jnp.concatenate([colsum(prf), colsum(prb)], axis=0)
            dbx_ref[...] += jnp.concatenate([colsum(pif), colsum(pib)], axis=0)
            dlam_ref[...] += jnp.concatenate([dsp_f, dsp_b], axis=0)
            dcb_ref[...] += colsum(duc)
            dcw_ref[...] += jnp.concatenate([colsum(duc * um2), colsum(duc * um1), colsum(duc * u0),
                                             colsum(duc * up1)], axis=0)
            af_ref[rows, :] = duc
            return c

        lax.fori_loop(0, nt, phase4, 0)
        dlam_ref[...] = dlam_ref[...] * (-_sigmoid(-lam_v))

        def phase5(i, c):
            ext = _halo(af_ref, i, S)
            rows = pl.ds(pl.multiple_of(i * LRU_TT, LRU_TT), LRU_TT)
            du = (_shift(ext, 2) * cw[0:1] + _shift(ext, 1) * cw[1:2] + ext[8:8 + LRU_TT] * cw[2:3]
                  + _shift(ext, -1) * cw[3:4])
            du_ref[rows, :] = du.astype(du_ref.dtype)
            return c

        lax.fori_loop(0, nt, phase5, 0)

    seq, par = _lru_specs(S)
    return pl.pallas_call(
        body, name="lru_bwd", grid=(NCH,),
        in_specs=[seq(0), seq(NCH), seq(0), par(4), par(1), par(2), par(2), par(2),
                  pl.BlockSpec((None, CW, 4 * CW), lambda j: (j, 0, 0))],
        out_specs=[seq(0), seq(0), par(4), par(1), par(2), par(2), par(2),
                   pl.BlockSpec((None, CW, 4 * CW), lambda j: (j, 0, 0))],
        out_shape=[_sds((S, D), bf16), _sds((S, D), bf16), _sds((4, D), f32), _sds((1, D), f32), _sds((2, D), f32),
                   _sds((2, D), f32), _sds((2, D), f32), _sds((NCH, CW, 4 * CW), f32)],
        scratch_shapes=[pltpu.VMEM((S, CW), f32)] * 4, compiler_params=_params(1, True),
    )(proj, proj, dy, conv_w, conv_b, lam, ba, bx, wbd)


_SLOPES = [2.0 ** (-8.0 * (h + 1) / NH) for h in range(NH)]


def _half_mask(shape, e):
    lane = lax.broadcasted_iota(jnp.int32, shape, 1)
    return (lane < HD) if e == 0 else (lane >= HD)


def _place(x, src, dst):
    if src != dst:
        x = pltpu.roll(x, HD, 1)
    return jnp.where(_half_mask(x.shape, dst), x, 0.0)


def _attn_geometry(n, S):
    tq = lax.broadcasted_iota(jnp.int32, (BLK, 3 * BLK), 0)
    sk = lax.broadcasted_iota(jnp.int32, (BLK, 3 * BLK), 1)
    dist = jnp.abs(tq + BLK - sk)
    kpos = n * BLK - BLK + sk
    valid = (dist <= BLK) & (kpos >= 0) & (kpos < S)
    return dist.astype(f32), valid


def _attn_probs(qc, km, absd, valid, slope, sink):
    s = lax.dot_general(qc, km, _DIMS["nt"], preferred_element_type=f32) * (HD ** -0.5)
    s = jnp.where(valid, s + (-slope) * absd, NEG_INF)
    m = jnp.maximum(jnp.max(s, axis=-1, keepdims=True), sink)
    p = jnp.exp(s - m)
    esink = jnp.exp(sink - m)
    den = jnp.sum(p, axis=-1, keepdims=True) + esink
    return p / den, esink / den


def _attn_specs(S):
    nb = S // BLK
    q_spec = pl.BlockSpec((BLK, D), lambda n: (n, 2))
    kv = lambda col: [pl.BlockSpec((BLK, 256), lambda n: (jnp.maximum(n - 1, 0), col)),
                      pl.BlockSpec((BLK, 256), lambda n: (n, col)),
                      pl.BlockSpec((BLK, 256), lambda n: (jnp.minimum(n + 1, nb - 1), col))]
    return nb, q_spec, kv(COL_K), kv(COL_V)


def _attn_fwd(proj, sink):
    S = proj.shape[0]
    nb, q_spec, k_specs, v_specs = _attn_specs(S)

    def body(sink_ref, q_ref, kp_ref, kc_ref, kn_ref, vp_ref, vc_ref, vn_ref, o_ref):
        absd, valid = _attn_geometry(pl.program_id(0), S)
        kcat = jnp.concatenate([kp_ref[...], kc_ref[...], kn_ref[...]], axis=0)
        vcat = jnp.concatenate([vp_ref[...], vc_ref[...], vn_ref[...]], axis=0)
        for pair in range(NH // 2):
            kvh = pair // 2
            kch = kcat[:, (kvh // 2) * 128:(kvh // 2 + 1) * 128]
            vch = vcat[:, (kvh // 2) * 128:(kvh // 2 + 1) * 128]
            qc = q_ref[:, pair * 128:(pair + 1) * 128].astype(bf16)
            acc = jnp.zeros((BLK, 128), f32)
            for e in range(2):
                h = 2 * pair + e
                km = _place(kch, kvh % 2, e).astype(bf16)
                vm = _place(vch, kvh % 2, e).astype(bf16)
                pn, _ = _attn_probs(qc, km, absd, valid, _SLOPES[h], sink_ref[0, h])
                acc = acc + jnp.dot(pn.astype(bf16), vm, preferred_element_type=f32)
            o_ref[:, pair * 128:(pair + 1) * 128] = acc

    return pl.pallas_call(
        body, name="attn_fwd", grid=(nb,),
        in_specs=[pl.BlockSpec(memory_space=pltpu.SMEM), q_spec] + k_specs + v_specs,
        out_specs=pl.BlockSpec((BLK, D), lambda n: (n, 0)), out_shape=_sds((S, D), f32),
        compiler_params=_params(1, True))(sink, proj, proj, proj, proj, proj, proj, proj)


def _attn_bwd(proj, sink, y_b, dy_b):
    S = proj.shape[0]
    nb, q_spec, k_specs, v_specs = _attn_specs(S)

    def body(sink_ref, q_ref, kp_ref, kc_ref, kn_ref, vp_ref, vc_ref, vn_ref, o_ref, do_ref,
             dq_ref, dk_ref, dv_ref, dsink_ref):
        n = pl.program_id(0)

        @pl.when(n == 0)
        def _():
            dk_ref[...] = jnp.zeros_like(dk_ref)
            dv_ref[...] = jnp.zeros_like(dv_ref)
            dsink_ref[...] = jnp.zeros_like(dsink_ref)

        absd, valid = _attn_geometry(n, S)
        kcat = jnp.concatenate([kp_ref[...], kc_ref[...], kn_ref[...]], axis=0)
        vcat = jnp.concatenate([vp_ref[...], vc_ref[...], vn_ref[...]], axis=0)
        dk_acc = [jnp.zeros((3 * BLK, 128), f32), jnp.zeros((3 * BLK, 128), f32)]
        dv_acc = [jnp.zeros((3 * BLK, 128), f32), jnp.zeros((3 * BLK, 128), f32)]
        scale = HD ** -0.5
        for pair in range(NH // 2):
            kvh = pair // 2
            ch = kvh // 2
            kch = kcat[:, ch * 128:(ch + 1) * 128]
            vch = vcat[:, ch * 128:(ch + 1) * 128]
            cols = slice(pair * 128, (pair + 1) * 128)
            qc = q_ref[:, cols].astype(bf16)
            do32 = do_ref[:, cols]
            dob = do32.astype(bf16)
            prod = do32 * o_ref[:, cols]
            dq_acc = jnp.zeros((BLK, 128), f32)
            for e in range(2):
                h = 2 * pair + e
                km = _place(kch, kvh % 2, e).astype(bf16)
                vm = _place(vch, kvh % 2, e).astype(bf16)
                pn, psink = _attn_probs(qc, km, absd, valid, _SLOPES[h], sink_ref[0, h])
                delta = jnp.sum(jnp.where(_half_mask(prod.shape, e), prod, 0.0), axis=-1, keepdims=True)
                dp = lax.dot_general(dob, vm, _DIMS["nt"], preferred_element_type=f32)
                ds = (pn * (dp - delta)).astype(bf16)
                dsink_ref[h:h + 1, :] += jnp.broadcast_to(-jnp.sum(psink * delta, axis=0, keepdims=True), (1, 128))
                dq_acc = dq_acc + jnp.dot(ds, km, preferred_element_type=f32) * scale
                ck = lax.dot_general(ds, qc, _DIMS["tn"], preferred_element_type=f32) * scale
                cv = lax.dot_general(pn.astype(bf16), dob, _DIMS["tn"], preferred_element_type=f32)
                dk_acc[ch] = dk_acc[ch] + _place(ck, e, kvh % 2)
                dv_acc[ch] = dv_acc[ch] + _place(cv, e, kvh % 2)
            dq_ref[:, cols] = dq_acc.astype(dq_ref.dtype)
        for j in range(3):
            blk = n + (j - 1)

            @pl.when((blk >= 0) & (blk < nb))
            def _():
                rows = pl.ds(pl.multiple_of(blk * BLK, BLK), BLK)
                for ch in range(2):
                    dk_ref[rows, ch * 128:(ch + 1) * 128] += dk_acc[ch][j * BLK:(j + 1) * BLK]
                    dv_ref[rows, ch * 128:(ch + 1) * 128] += dv_acc[ch][j * BLK:(j + 1) * BLK]

    row_blk = pl.BlockSpec((BLK, D), lambda n: (n, 0))
    full = pl.BlockSpec((S, 256), lambda n: (0, 0))
    return pl.pallas_call(
        body, name="attn_bwd", grid=(nb,),
        in_specs=[pl.BlockSpec(memory_space=pltpu.SMEM), q_spec] + k_specs + v_specs + [row_blk, row_blk],
        out_specs=[row_blk, full, full, pl.BlockSpec((NH, 128), lambda n: (0, 0))],
        out_shape=[_sds((S, D), bf16), _sds((S, 256), f32), _sds((S, 256), f32), _sds((NH, 128), f32)],
        compiler_params=_params(1, True))(sink, proj, proj, proj, proj, proj, proj, proj, y_b, dy_b)


def _adamw(name, w, g, m, v, tr):
    R, C = w.shape
    tr = min(tr, R)

    def body(w_ref, g_ref, m_ref, v_ref, d_ref, m2_ref, v2_ref):
        g = g_ref[...]
        m2 = ADAM_B1 * m_ref[...] + (1.0 - ADAM_B1) * g
        v2 = ADAM_B2 * v_ref[...] + (1.0 - ADAM_B2) * (g * g)
        m_hat = m2 / (1.0 - ADAM_B1 ** ADAM_STEP)
        v_hat = v2 / (1.0 - ADAM_B2 ** ADAM_STEP)
        d_ref[...] = -ADAM_LR * (m_hat / (jnp.sqrt(v_hat) + ADAM_EPS) + ADAM_WD * w_ref[...])
        m2_ref[...] = m2
        v2_ref[...] = v2

    blk = pl.BlockSpec((tr, C), lambda i: (i, 0))
    return pl.pallas_call(body, name=name, grid=(R // tr,), in_specs=[blk] * 4, out_specs=[blk] * 3,
                          out_shape=[_sds((R, C), f32)] * 3, compiler_params=_params(1))(w, g, m, v)


def _pair_sum(name, c_arr, g4, recv, th):
    _, _, h, w = g4.shape
    th = min(th, h)

    def body(c_ref, g_ref, r_ref, o_ref):
        o_ref[...] = g_ref[...] + r_ref[...]

    spec = pltpu.PrefetchScalarGridSpec(
        num_scalar_prefetch=1, grid=(NCHIP, h // th),
        in_specs=[pl.BlockSpec((None, None, th, w), lambda s, i, c_ref: (s, c_ref[0], i, 0)),
                  pl.BlockSpec((None, th, w), lambda s, i, c_ref: (s, i, 0))],
        out_specs=pl.BlockSpec((None, th, w), lambda s, i, c_ref: (s, i, 0)))
    return pl.pallas_call(body, name=name, grid_spec=spec, out_shape=_sds((NCHIP, h, w), f32),
                          compiler_params=_params(2))(c_arr, g4, recv)


def _add2(name, a, b):
    def body(a_ref, b_ref, o_ref):
        o_ref[...] = a_ref[...] + b_ref[...]
    return pl.pallas_call(body, name=name, out_shape=_sds(a.shape, f32))(a, b)


def _sum4(name, b4, th):
    _, h, w = b4.shape
    th = min(th, h)

    def body(b_ref, o_ref):
        o_ref[...] = ((b_ref[0] + b_ref[1]) + b_ref[2]) + b_ref[3]

    return pl.pallas_call(body, name=name, grid=(h // th,),
                          in_specs=[pl.BlockSpec((NCHIP, th, w), lambda i: (0, i, 0))],
                          out_specs=pl.BlockSpec((th, w), lambda i: (i, 0)), out_shape=_sds((h, w), f32),
                          compiler_params=_params(1, True))(b4)


def _coords():
    x, y, c = lax.axis_index("x"), lax.axis_index("y"), lax.axis_index("c")
    return x, y, c, [(1 - x, y), (x, 1 - y), (1 - x, 1 - y)]


def _gather_chips(arrs):
    n = len(arrs)

    def body(*refs):
        ins, outs = refs[:n], refs[n:2 * n]
        send_sems, recv_sems, local_sems = refs[2 * n:]
        x, y, c, chips = _coords()
        s = 2 * x + y
        local = [pltpu.make_async_copy(ins[a], outs[a].at[s], local_sems.at[a]) for a in range(n)]
        for cp in local:
            cp.start()

        def remote(k, a, slot, peer):
            return pltpu.make_async_remote_copy(src_ref=ins[a], dst_ref=outs[a].at[slot], send_sem=send_sems.at[k * n + a],
                                                recv_sem=recv_sems.at[k * n + a], device_id=peer, device_id_type=MESH)

        sends = [remote(k, a, s, (px, py, c)) for k, (px, py) in enumerate(chips) for a in range(n)]
        for cp in sends:
            cp.start()
        for k, (px, py) in enumerate(chips):
            for a in range(n):
                remote(k, a, 2 * px + py, (px, py, c)).wait_recv()
        for cp in sends:
            cp.wait_send()
        for cp in local:
            cp.wait()

    return pl.pallas_call(
        body, name="gather_weights", in_specs=[ANY] * n, out_specs=[ANY] * n,
        out_shape=[_sds((NCHIP,) + a.shape, a.dtype) for a in arrs],
        scratch_shapes=[pltpu.SemaphoreType.DMA((3 * n,)), pltpu.SemaphoreType.DMA((3 * n,)), pltpu.SemaphoreType.DMA((n,))],
    )(*arrs)


def _sibling_halves(g4s, small):
    n = len(g4s)

    def body(*refs):
        ins, small_ref = refs[:n], refs[n]
        outs, small_out = refs[n + 1:2 * n + 1], refs[2 * n + 1]
        send_sems, recv_sems = refs[2 * n + 2:]
        x, y, c, _ = _coords()
        sib = (x, y, 1 - c)

        def remote(a, half):
            src = small_ref if a == n else ins[a].at[:, half]
            dst = small_out if a == n else outs[a]
            return pltpu.make_async_remote_copy(src_ref=src, dst_ref=dst, send_sem=send_sems.at[a], recv_sem=recv_sems.at[a],
                                                device_id=sib, device_id_type=MESH)

        sends = [remote(a, 1 - c) for a in range(n + 1)]
        for cp in sends:
            cp.start()
        for a in range(n + 1):
            remote(a, c).wait_recv()
        for cp in sends:
            cp.wait_send()

    return pl.pallas_call(
        body, name="reduce_sibling", in_specs=[ANY] * (n + 1), out_specs=[ANY] * (n + 1),
        out_shape=[_sds((g.shape[0],) + g.shape[2:], f32) for g in g4s] + [_sds(small.shape, f32)],
        scratch_shapes=[pltpu.SemaphoreType.DMA((n + 1,)), pltpu.SemaphoreType.DMA((n + 1,))],
    )(*g4s, small)


def _exchange_chips(parts, small2):
    n = len(parts)

    def body(*refs):
        ins, small_ref = refs[:n], refs[n]
        outs, small_out = refs[n + 1:2 * n + 1], refs[2 * n + 1]
        send_sems, recv_sems, local_sems = refs[2 * n + 2:]
        x, y, c, chips = _coords()
        s = 2 * x + y

        def src_dst(a, dest_chip, slot):
            if a == n:
                return small_ref.at[c], small_out.at[slot]
            return ins[a].at[dest_chip], outs[a].at[slot]

        local = []
        for a in range(n + 1):
            src, dst = src_dst(a, s, s)
            local.append(pltpu.make_async_copy(src, dst, local_sems.at[a]))
        for cp in local:
            cp.start()

        def remote(k, a, dest_chip, slot, peer):
            src, dst = src_dst(a, dest_chip, slot)
            i = k * (n + 1) + a
            return pltpu.make_async_remote_copy(src_ref=src, dst_ref=dst, send_sem=send_sems.at[i], recv_sem=recv_sems.at[i],
                                                device_id=peer, device_id_type=MESH)

        sends = [remote(k, a, 2 * px + py, s, (px, py, c)) for k, (px, py) in enumerate(chips) for a in range(n + 1)]
        for cp in sends:
            cp.start()
        for k, (px, py) in enumerate(chips):
            for a in range(n + 1):
                remote(k, a, s, 2 * px + py, (px, py, c)).wait_recv()
        for cp in sends:
            cp.wait_send()
        for cp in local:
            cp.wait()

    m = 3 * (n + 1)
    return pl.pallas_call(
        body, name="reduce_chips", in_specs=[ANY] * (n + 1), out_specs=[ANY] * (n + 1),
        out_shape=[_sds(p.shape, f32) for p in parts] + [_sds((NCHIP,) + small2.shape[1:], f32)],
        scratch_shapes=[pltpu.SemaphoreType.DMA((m,)), pltpu.SemaphoreType.DMA((m,)), pltpu.SemaphoreType.DMA((n + 1,))],
    )(*parts, small2)


def _share_sibling(halves):
    n = len(halves)

    def body(*refs):
        ins, outs = refs[:n], refs[n:2 * n]
        send_sems, recv_sems, local_sems = refs[2 * n:]
        x, y, c, _ = _coords()
        sib = (x, y, 1 - c)
        local = [pltpu.make_async_copy(ins[a], outs[a].at[c], local_sems.at[a]) for a in range(n)]
        for cp in local:
            cp.start()

        def remote(a, slot):
            return pltpu.make_async_remote_copy(src_ref=ins[a], dst_ref=outs[a].at[slot], send_sem=send_sems.at[a],
                                                recv_sem=recv_sems.at[a], device_id=sib, device_id_type=MESH)

        sends = [remote(a, c) for a in range(n)]
        for cp in sends:
            cp.start()
        for a in range(n):
            remote(a, 1 - c).wait_recv()
        for cp in sends:
            cp.wait_send()
        for cp in local:
            cp.wait()

    return pl.pallas_call(
        body, name="reduce_share", in_specs=[ANY] * n, out_specs=[ANY] * n,
        out_shape=[_sds((2,) + h.shape, f32) for h in halves],
        scratch_shapes=[pltpu.SemaphoreType.DMA((n,)), pltpu.SemaphoreType.DMA((n,)), pltpu.SemaphoreType.DMA((n,))],
    )(*halves)


def _block_diag_pairs(w):
    w = w.reshape(NCH, 2, HD, HD)
    z = jnp.zeros((NCH, HD, HD), w.dtype)
    return jnp.concatenate([jnp.concatenate([w[:, 0], z], axis=2), jnp.concatenate([z, w[:, 1]], axis=2)], axis=1)


def _diag_blocks(m):
    return jnp.stack([m[:, :HD, :HD], m[:, HD:, HD:]], axis=1).reshape(NH, HD, HD)


def _pack(vs, rows):
    flat = jnp.concatenate([v.reshape(-1) for v in vs])
    return jnp.pad(flat, (0, rows * 128 - flat.shape[0])).reshape(rows, 128)


def _unpack(packed, shapes):
    flat = packed.reshape(-1)
    out, off = [], 0
    for shp in shapes:
        size = math.prod(shp)
        out.append(flat[off:off + size].reshape(shp))
        off += size
    return out


def _rows_for(sizes, multiple):
    rows = -(-sum(sizes) // 128)
    return -(-rows // multiple) * multiple


def kernel(x, norm_mix_g, w_in, b_gate, conv_w, conv_b, lru_lambda, lru_wa, lru_ba, lru_wx, lru_bx, attn_sink, w_out, norm_ffn_g, w_ffn_in, w_ffn_out, norm_final_g, loss_target, m_norm_mix_g, m_w_in, m_b_gate, m_conv_w, m_conv_b, m_lru_lambda, m_lru_wa, m_lru_ba, m_lru_wx, m_lru_bx, m_attn_sink, m_w_out, m_norm_ffn_g, m_w_ffn_in, m_w_ffn_out, m_norm_final_g, v_norm_mix_g, v_w_in, v_b_gate, v_conv_w, v_conv_b, v_lru_lambda, v_lru_wa, v_lru_ba, v_lru_wx, v_lru_bx, v_attn_sink, v_w_out, v_norm_ffn_g, v_w_ffn_in, v_w_ffn_out, v_norm_final_g):
    S = x.shape[1]
    xs = x[0]
    tgt = loss_target[0]
    cx, cy, cc = lax.axis_index("x"), lax.axis_index("y"), lax.axis_index("c")
    chip = 2 * cx + cy
    SW = D // NCHIP

    small_shard = _pack([conv_w[0], lru_lambda[0], lru_ba[0], lru_bx[0]], 24)
    w_in_g, w_ffn_in_g, w_out_g, w_ffn_out_g, small_g = _gather_chips(
        [w_in[0].astype(bf16), w_ffn_in[0].astype(bf16), w_out[0].astype(bf16), w_ffn_out[0].astype(bf16), small_shard])
    w_out_f = w_out_g.reshape(D, D)
    w_ffn_out_f = w_ffn_out_g.reshape(FF, D)
    small_parts = [_unpack(small_g[s], [(4, SW), (2, SW), (2, SW), (2, SW)]) for s in range(NCHIP)]
    conv_w_f, lam_f, ba_f, bx_f = [jnp.concatenate([small_parts[s][p] for s in range(NCHIP)], axis=1) for p in range(4)]
    wbd = jnp.concatenate([_block_diag_pairs(lru_wa[0, 0]), _block_diag_pairs(lru_wx[0, 0]),
                           _block_diag_pairs(lru_wa[0, 1]), _block_diag_pairs(lru_wx[0, 1])], axis=2).astype(bf16)
    conv_b_f = conv_b
    sink = attn_sink

    xn, proj = _rms_matmul("rms_proj", xs, norm_mix_g, w_in_g, 1024)
    y_a = _lru_fwd(proj, conv_w_f, conv_b_f, lam_f, ba_f, bx_f, wbd)
    y_b = _attn_fwd(proj, sink)
    merged = _merge_fwd(proj, b_gate, y_a, y_b, 512)
    x1 = _mm_residual("out_proj", merged, w_out_f, xs, 512)
    xn2, gu = _rms_matmul("rms_ffn_in", x1, norm_ffn_g, w_ffn_in_g, 1024)
    act = _swiglu_fwd(gu, 512)
    x2 = _mm_residual("ffn_out", act, w_ffn_out_f, x1, 512)
    dx2, loss_row, dg3 = _final_loss_bwd(x2, norm_final_g.reshape(1, D), tgt, 256)
    loss = lax.psum(loss_row[0, 0], ("x", "y", "c"))

    tm = min(512, S)
    tk = min(512, S)
    d_act = _mm_nt_resident("d_act", dx2, w_ffn_out_f, 512)
    gw_ffn_out = _mm_tn("dw_ffn_out", act, pl.BlockSpec((tk, SHW), lambda i, k: (k, i)),
                        dx2, pl.BlockSpec((tk, D), lambda i, k: (k, 0)),
                        _sds((FF, D), f32), pl.BlockSpec((SHW, D), lambda i, k: (i, 0)), (2, S // tk), (SHW, D))
    dgu = _swiglu_bwd(gu, d_act, 512)
    dxn2 = _mm_nt_groups("dxn2", dgu, pl.BlockSpec((None, tm, SHW), lambda i, g: (g // 2, i, g % 2)), w_ffn_in_g, S, tm)
    gw_ffn_in = _mm_tn("dw_ffn_in", xn2, pl.BlockSpec((tk, D), lambda g, k: (k, 0)),
                       dgu, pl.BlockSpec((None, tk, SHW), lambda g, k: (g // 2, k, g % 2)),
                       _sds((NCHIP, D, SHW), f32), pl.BlockSpec((None, D, SHW), lambda g, k: (g, 0, 0)),
                       (NCHIP, S // tk), (D, SHW))
    dx1, dg2 = _rms_bwd("rms_ffn_bwd", x1, norm_ffn_g, dxn2, dx2, 256)

    dmerged = _mm_nt_resident("d_merged", dx1, w_out_f, 512)
    gw_out = _mm_tn("dw_out", merged, pl.BlockSpec((tk, D), lambda i, k: (k, 0)),
                    dx1, pl.BlockSpec((tk, D), lambda i, k: (k, 0)),
                    _sds((D, D), f32), pl.BlockSpec((D, D), lambda i, k: (0, 0)), (1, S // tk), (D, D))
    dz0, dz1, dy_a, dy_b, db0, db1 = _merge_bwd(proj, b_gate, y_a, y_b, dmerged, 512)
    du, dgl, dcw, dcb, dlam, dba, dbx, dwbd = _lru_bwd(proj, dy_a, conv_w_f, conv_b_f, lam_f, ba_f, bx_f, wbd)
    dq, dk, dv, dsink = _attn_bwd(proj, sink, y_b, dy_b)
    dproj = jnp.concatenate([du, dgl, dq, dk.astype(bf16), dv.astype(bf16), dz0, dz1], axis=1)
    dxn = _mm_nt_groups("dxn", dproj, pl.BlockSpec((tm, SHW), lambda i, g: (i, g)), w_in_g, S, tm)
    gw_in = _mm_tn("dw_in", xn, pl.BlockSpec((tk, D), lambda g, k: (k, 0)),
                   dproj, pl.BlockSpec((tk, SHW), lambda g, k: (k, g)),
                   _sds((NCHIP, D, SHW), f32), pl.BlockSpec((None, D, SHW), lambda g, k: (g, 0, 0)),
                   (NCHIP, S // tk), (D, SHW))
    grad_x, dg1 = _rms_bwd("rms_mix_bwd", xs, norm_mix_g, dxn, dx1, 256)

    d_wa = jnp.stack([_diag_blocks(dwbd[:, :, 0:CW]), _diag_blocks(dwbd[:, :, 2 * CW:3 * CW])])
    d_wx = jnp.stack([_diag_blocks(dwbd[:, :, CW:2 * CW]), _diag_blocks(dwbd[:, :, 3 * CW:4 * CW])])
    small_full = [dg1, jnp.concatenate([db0, db1], axis=1), dcw, dcb, dlam, d_wa, dba, d_wx, dbx, dsink[:, 0], dg2, dg3]
    full_shapes = [(1, D), (1, 2 * D), (4, D), (1, D), (2, D), (2, NH, HD, HD), (2, D), (2, NH, HD, HD), (2, D), (NH,),
                   (1, D), (1, D)]
    rows_full = _rows_for([math.prod(s) for s in full_shapes], 16)
    small_vec = _pack(small_full, rows_full)

    big = [gw_in.reshape(NCHIP, 2, D // 2, SHW), gw_ffn_in.reshape(NCHIP, 2, D // 2, SHW),
           gw_out.reshape(NCHIP, 2, D // NCHIP // 2, D), gw_ffn_out.reshape(NCHIP, 2, FF // NCHIP // 2, D)]
    *recv_a, small_sib = _sibling_halves(big, small_vec)
    c_arr = cc.reshape(1).astype(jnp.int32)
    tiles = [256, 256, 128, 352]
    names = ["w_in", "w_ffn_in", "w_out", "w_ffn_out"]
    parts = [_pair_sum("pair_sum_" + nm, c_arr, g4, r, th) for nm, g4, r, th in zip(names, big, recv_a, tiles)]
    small_chip = _add2("pair_sum_small", small_vec, small_sib).reshape(2, rows_full // 2, 128)
    *recv_b, small_all = _exchange_chips(parts, small_chip)
    halves = [_sum4("chip_sum_" + nm, b4, th) for nm, b4, th in zip(names, recv_b, tiles)]
    halves.append(_sum4("chip_sum_small", small_all, rows_full // 2))
    g_in, g_ffn_in, g_out, g_ffn_out, g_small = _share_sibling(halves)
    g_in, g_ffn_in = g_in.reshape(D, SHW), g_ffn_in.reshape(D, SHW)
    g_out, g_ffn_out = g_out.reshape(D // NCHIP, D), g_ffn_out.reshape(FF // NCHIP, D)
    g_full = _unpack(g_small.reshape(rows_full, 128), full_shapes)

    out_big = {}
    for nm, w, g, m, v, tr in [("w_in", w_in, g_in, m_w_in, v_w_in, 256), ("w_ffn_in", w_ffn_in, g_ffn_in, m_w_ffn_in, v_w_ffn_in, 256),
                               ("w_out", w_out, g_out, m_w_out, v_w_out, 256), ("w_ffn_out", w_ffn_out, g_ffn_out, m_w_ffn_out, v_w_ffn_out, 352)]:
        d_, m_, v_ = _adamw("adamw_" + nm, w[0], g, m[0], v[0], tr)
        out_big[nm] = (g[None], d_[None], m_[None], v_[None])

    small_names = ["norm_mix_g", "b_gate", "conv_w", "conv_b", "lru_lambda", "lru_wa", "lru_ba", "lru_wx", "lru_bx", "attn_sink",
                   "norm_ffn_g", "norm_final_g"]
    sharded = {"conv_w", "lru_lambda", "lru_ba", "lru_bx"}
    small_w = [norm_mix_g, b_gate, conv_w, conv_b, lru_lambda, lru_wa, lru_ba, lru_wx, lru_bx, attn_sink, norm_ffn_g, norm_final_g]
    small_m = [m_norm_mix_g, m_b_gate, m_conv_w, m_conv_b, m_lru_lambda, m_lru_wa, m_lru_ba, m_lru_wx, m_lru_bx, m_attn_sink,
               m_norm_ffn_g, m_norm_final_g]
    small_v = [v_norm_mix_g, v_b_gate, v_conv_w, v_conv_b, v_lru_lambda, v_lru_wa, v_lru_ba, v_lru_wx, v_lru_bx, v_attn_sink,
               v_norm_ffn_g, v_norm_final_g]
    g_local = []
    for nm, g, w in zip(small_names, g_full, small_w):
        if nm in sharded:
            g = lax.dynamic_slice_in_dim(g, chip * SW, SW, axis=1)
        g_local.append(g.reshape(w.shape))
    local_shapes = [w.shape for w in small_w]
    rows_local = _rows_for([math.prod(s) for s in local_shapes], 8)
    d_s, m_s, v_s = _adamw("adamw_small", _pack(small_w, rows_local), _pack(g_local, rows_local),
                           _pack(small_m, rows_local), _pack(small_v, rows_local), rows_local)
    d_l, m_l, v_l = _unpack(d_s, local_shapes), _unpack(m_s, local_shapes), _unpack(v_s, local_shapes)
    res = {nm: (g_local[i], d_l[i], m_l[i], v_l[i]) for i, nm in enumerate(small_names)}
    res.update(out_big)

    order = ["norm_mix_g", "w_in", "b_gate", "conv_w", "conv_b", "lru_lambda", "lru_wa", "lru_ba", "lru_wx", "lru_bx", "attn_sink",
             "w_out", "norm_ffn_g", "w_ffn_in", "w_ffn_out", "norm_final_g"]
    outs = [loss, grad_x[None]]
    for k in range(4):
        outs += [res[nm][k] for nm in order]
    return tuple(outs)
```

```python
import functools
import math

import jax
import jax.numpy as jnp
from jax import lax
from jax.experimental import pallas as pl
from jax.experimental.pallas import tpu as pltpu

f32 = jnp.float32
bf16 = jnp.bfloat16

D = 1024
NH = 16
HD = 64
FF = 2816
INW = 5632
NCHIP = 4
SHW = INW // NCHIP
CW = 128
NCH = D // CW
BLK = 128
EPS = 1e-6
NEG_INF = -1e30
RGLRU_C = 8.0
ADAM_LR, ADAM_B1, ADAM_B2, ADAM_EPS, ADAM_WD, ADAM_STEP = 0.001, 0.9, 0.999, 1e-08, 0.01, 10
VMEM_LIMIT = 58 * 1024 * 1024
MESH = pl.DeviceIdType.MESH
ANY = pl.BlockSpec(memory_space=pl.ANY)

COL_U, COL_G, COL_Q, COL_K, COL_V, COL_Z0, COL_Z1 = 0, 4, 8, 12, 13, 14, 18


def _params(n_axes, vmem=False):
    return pltpu.CompilerParams(dimension_semantics=("arbitrary",) * n_axes,
                                vmem_limit_bytes=VMEM_LIMIT if vmem else None)


def _sds(shape, dtype):
    return jax.ShapeDtypeStruct(tuple(shape), dtype)


_DIMS = {"nn": (((1,), (0,)), ((), ())), "nt": (((1,), (1,)), ((), ())), "tn": (((0,), (0,)), ((), ()))}


def _mm(name, mode, a, a_spec, b, b_spec, out_shape, out_spec, grid, nk, acc_shape, add=None, add_spec=None):
    has_add = add is not None

    def body(*refs):
        a_ref, b_ref = refs[0], refs[1]
        add_ref = refs[2] if has_add else None
        o_ref = refs[2 + has_add]
        part = lax.dot_general(a_ref[...].astype(bf16), b_ref[...].astype(bf16), _DIMS[mode],
                               preferred_element_type=f32)
        if nk == 1:
            if has_add:
                part = add_ref[...] + part
            o_ref[...] = part.astype(o_ref.dtype)
            return
        acc_ref = refs[3 + has_add]
        k = pl.program_id(len(grid) - 1)

        @pl.when(k == 0)
        def _():
            acc_ref[...] = part

        @pl.when(k > 0)
        def _():
            acc_ref[...] += part

        @pl.when(k == nk - 1)
        def _():
            res = acc_ref[...]
            if has_add:
                res = add_ref[...] + res
            o_ref[...] = res.astype(o_ref.dtype)

    ins = [a, b] + ([add] if has_add else [])
    in_specs = [a_spec, b_spec] + ([add_spec] if has_add else [])
    scratch = [pltpu.VMEM(acc_shape, f32)] if nk > 1 else []
    return pl.pallas_call(body, name=name, grid=grid, in_specs=in_specs, out_specs=out_spec, out_shape=out_shape,
                          scratch_shapes=scratch, compiler_params=_params(len(grid), True))(*ins)


def _rms_matmul(name, x, g, w3, tm):
    S, K = x.shape
    G, _, Nw = w3.shape
    tm = min(tm, S)

    def body(x_ref, g_ref, w_ref, xn_ref, o_ref, xs_ref):
        @pl.when(pl.program_id(1) == 0)
        def _():
            xf = x_ref[...]
            r = lax.rsqrt(jnp.mean(xf * xf, axis=-1, keepdims=True) + EPS)
            xn = ((xf * r) * g_ref[...]).astype(bf16)
            xs_ref[...] = xn
            xn_ref[...] = xn

        o_ref[...] = jnp.dot(xs_ref[...], w_ref[...], preferred_element_type=f32)

    return pl.pallas_call(
        body, name=name, grid=(S // tm, G),
        in_specs=[pl.BlockSpec((tm, K), lambda i, j: (i, 0)), pl.BlockSpec((1, K), lambda i, j: (0, 0)),
                  pl.BlockSpec((None, K, Nw), lambda i, j: (j, 0, 0))],
        out_specs=[pl.BlockSpec((tm, K), lambda i, j: (i, 0)), pl.BlockSpec((tm, Nw), lambda i, j: (i, j))],
        out_shape=[_sds((S, K), bf16), _sds((S, G * Nw), f32)],
        scratch_shapes=[pltpu.VMEM((tm, K), bf16)], compiler_params=_params(2, True))(x, g, w3)


def _mm_residual(name, a, w, res, tm):
    S, K = a.shape
    N = w.shape[1]
    tm = min(tm, S)
    return _mm(name, "nn", a, pl.BlockSpec((tm, K), lambda i: (i, 0)), w, pl.BlockSpec((K, N), lambda i: (0, 0)),
               _sds((S, N), f32), pl.BlockSpec((tm, N), lambda i: (i, 0)), (S // tm,), 1, None,
               add=res, add_spec=pl.BlockSpec((tm, N), lambda i: (i, 0)))


def _mm_nt_resident(name, a, w, tm):
    S, K = a.shape
    N = w.shape[0]
    tm = min(tm, S)
    return _mm(name, "nt", a, pl.BlockSpec((tm, K), lambda i: (i, 0)), w, pl.BlockSpec((N, K), lambda i: (0, 0)),
               _sds((S, N), f32), pl.BlockSpec((tm, N), lambda i: (i, 0)), (S // tm,), 1, None)


def _mm_nt_groups(name, a, a_spec, w3, S, tm):
    G, Dout, Kw = w3.shape
    return _mm(name, "nt", a, a_spec, w3, pl.BlockSpec((None, Dout, Kw), lambda i, g: (g, 0, 0)),
               _sds((S, Dout), f32), pl.BlockSpec((tm, Dout), lambda i, g: (i, 0)), (S // tm, G), G, (tm, Dout))


def _mm_tn(name, a, a_spec, b, b_spec, out_shape, out_spec, grid, acc_shape):
    return _mm(name, "tn", a, a_spec, b, b_spec, out_shape, out_spec, grid, grid[-1], acc_shape)


def _sigmoid(x):
    return 0.5 * jnp.tanh(0.5 * x) + 0.5


_GELU_C = math.sqrt(2.0 / math.pi)


def _gelu_and_grad(x):
    v = _GELU_C * (x + 0.044715 * (x * x * x))
    t = jnp.tanh(v)
    gl = 0.5 * x * (1.0 + t)
    dgl = 0.5 * (1.0 + t) + 0.5 * x * (1.0 - t * t) * (_GELU_C * (1.0 + 3.0 * 0.044715 * (x * x)))
    return gl, dgl


def _neg_expm1(x):
    series = x * (1.0 + x * 0.5 * (1.0 + x * (1.0 / 3.0) * (1.0 + x * 0.25 * (1.0 + x * 0.2 * (
        1.0 + x * (1.0 / 6.0) * (1.0 + x * (1.0 / 7.0)))))))
    return -jnp.where(x > -0.25, series, jnp.exp(x) - 1.0)


def _merge_fwd(proj, b_gate, y_a, y_b, tm):
    S = proj.shape[0]
    tm = min(tm, S)

    def body(z0_ref, z1_ref, b0_ref, b1_ref, ya_ref, yb_ref, o_ref):
        g0 = _sigmoid(z0_ref[...] + b0_ref[...])
        g1 = _sigmoid(z1_ref[...] + b1_ref[...])
        o_ref[...] = (g0 * ya_ref[...] + g1 * yb_ref[...]).astype(bf16)

    blk = lambda off: pl.BlockSpec((tm, 256), lambda j, i: (i, off + j))
    vec = lambda off: pl.BlockSpec((1, 256), lambda j, i: (0, off + j))
    return pl.pallas_call(body, name="merge_fwd", grid=(4, S // tm),
                          in_specs=[blk(COL_Z0), blk(COL_Z1), vec(0), vec(4), blk(0), blk(0)],
                          out_specs=blk(0), out_shape=_sds((S, D), bf16),
                          compiler_params=_params(2))(proj, proj, b_gate, b_gate, y_a, y_b)


def _merge_bwd(proj, b_gate, y_a, y_b, dm, tm):
    S = proj.shape[0]
    tm = min(tm, S)

    def body(z0_ref, z1_ref, b0_ref, b1_ref, ya_ref, yb_ref, dm_ref, dz0_ref, dz1_ref, dya_ref, dyb_ref, db0_ref, db1_ref):
        g0 = _sigmoid(z0_ref[...] + b0_ref[...])
        g1 = _sigmoid(z1_ref[...] + b1_ref[...])
        d = dm_ref[...]
        dz0 = (d * ya_ref[...]) * (g0 * (1.0 - g0))
        dz1 = (d * yb_ref[...]) * (g1 * (1.0 - g1))
        dz0_ref[...] = dz0.astype(bf16)
        dz1_ref[...] = dz1.astype(bf16)
        dya_ref[...] = d * g0
        dyb_ref[...] = d * g1

        @pl.when(pl.program_id(1) == 0)
        def _():
            db0_ref[...] = jnp.zeros_like(db0_ref)
            db1_ref[...] = jnp.zeros_like(db1_ref)

        db0_ref[...] += jnp.sum(dz0, axis=0, keepdims=True)
        db1_ref[...] += jnp.sum(dz1, axis=0, keepdims=True)

    blk = lambda off: pl.BlockSpec((tm, 256), lambda j, i: (i, off + j))
    vec = lambda off: pl.BlockSpec((1, 256), lambda j, i: (0, off + j))
    return pl.pallas_call(
        body, name="merge_bwd", grid=(4, S // tm),
        in_specs=[blk(COL_Z0), blk(COL_Z1), vec(0), vec(4), blk(0), blk(0), blk(0)],
        out_specs=[blk(0), blk(0), blk(0), blk(0), vec(0), vec(0)],
        out_shape=[_sds((S, D), bf16), _sds((S, D), bf16), _sds((S, D), f32), _sds((S, D), f32),
                   _sds((1, D), f32), _sds((1, D), f32)],
        compiler_params=_params(2))(proj, proj, b_gate, b_gate, y_a, y_b, dm)


def _swiglu_fwd(gu, tm):
    S = gu.shape[0]
    tm = min(tm, S)
    nj = FF // 256

    def body(g_ref, u_ref, o_ref):
        g = g_ref[...]
        o_ref[...] = ((g * _sigmoid(g)) * u_ref[...]).astype(bf16)

    return pl.pallas_call(body, name="swiglu_fwd", grid=(nj, S // tm),
                          in_specs=[pl.BlockSpec((tm, 256), lambda j, i: (i, j)),
                                    pl.BlockSpec((tm, 256), lambda j, i: (i, nj + j))],
                          out_specs=pl.BlockSpec((tm, 256), lambda j, i: (i, j)), out_shape=_sds((S, FF), bf16),
                          compiler_params=_params(2))(gu, gu)


def _swiglu_bwd(gu, d_act, tm):
    S = gu.shape[0]
    tm = min(tm, S)
    nj = FF // 256

    def body(g_ref, u_ref, d_ref, o_ref):
        g = g_ref[...]
        s = _sigmoid(g)
        d = d_ref[...]
        o_ref[0] = ((d * u_ref[...]) * (s * (1.0 + g * (1.0 - s)))).astype(bf16)
        o_ref[1] = (d * (g * s)).astype(bf16)

    return pl.pallas_call(body, name="swiglu_bwd", grid=(nj, S // tm),
                          in_specs=[pl.BlockSpec((tm, 256), lambda j, i: (i, j)),
                                    pl.BlockSpec((tm, 256), lambda j, i: (i, nj + j)),
                                    pl.BlockSpec((tm, 256), lambda j, i: (i, j))],
                          out_specs=pl.BlockSpec((2, tm, 256), lambda j, i: (0, i, j)),
                          out_shape=_sds((2, S, FF), bf16), compiler_params=_params(2))(gu, gu, d_act)


def _final_loss_bwd(x2, g3, tgt, tm):
    S = x2.shape[0]
    tm = min(tm, S)

    def body(x_ref, g_ref, t_ref, dx_ref, loss_ref, dg_ref):
        @pl.when(pl.program_id(0) == 0)
        def _():
            loss_ref[...] = jnp.zeros_like(loss_ref)
            dg_ref[...] = jnp.zeros_like(dg_ref)

        x = x_ref[...]
        g = g_ref[...]
        r = lax.rsqrt(jnp.mean(x * x, axis=-1, keepdims=True) + EPS)
        xh = x * r
        err = xh * g - t_ref[...]
        row = jnp.mean(err * err, axis=-1, keepdims=True)
        loss_ref[...] += 0.5 * jnp.sum(row, axis=0, keepdims=True)
        dy = err * (1.0 / D)
        dg_ref[...] += jnp.sum(dy * xh, axis=0, keepdims=True)
        dxh = dy * g
        dx_ref[...] = r * (dxh - xh * jnp.mean(dxh * xh, axis=-1, keepdims=True))

    row_blk = pl.BlockSpec((tm, D), lambda i: (i, 0))
    vec = pl.BlockSpec((1, D), lambda i: (0, 0))
    return pl.pallas_call(body, name="final_loss_bwd", grid=(S // tm,), in_specs=[row_blk, vec, row_blk],
                          out_specs=[row_blk, pl.BlockSpec((1, 128), lambda i: (0, 0)), vec],
                          out_shape=[_sds((S, D), f32), _sds((1, 128), f32), _sds((1, D), f32)],
                          compiler_params=_params(1))(x2, g3, tgt)


def _rms_bwd(name, x, g, dxn, dres, tm):
    S = x.shape[0]
    tm = min(tm, S)

    def body(x_ref, g_ref, d_ref, r_ref, dx_ref, dg_ref):
        @pl.when(pl.program_id(0) == 0)
        def _():
            dg_ref[...] = jnp.zeros_like(dg_ref)

        x = x_ref[...]
        d = d_ref[...]
        r = lax.rsqrt(jnp.mean(x * x, axis=-1, keepdims=True) + EPS)
        xh = x * r
        dg_ref[...] += jnp.sum(d * xh, axis=0, keepdims=True)
        dxh = d * g_ref[...]
        dx_ref[...] = r_ref[...] + r * (dxh - xh * jnp.mean(dxh * xh, axis=-1, keepdims=True))

    row_blk = pl.BlockSpec((tm, D), lambda i: (i, 0))
    vec = pl.BlockSpec((1, D), lambda i: (0, 0))
    return pl.pallas_call(body, name=name, grid=(S // tm,), in_specs=[row_blk, vec, row_blk, row_blk],
                          out_specs=[row_blk, vec], out_shape=[_sds((S, D), f32), _sds((1, D), f32)],
                          compiler_params=_params(1))(x, g, dxn, dres)


LRU_TT = 256


def _halo(ref, i, S):
    nt = S // LRU_TT
    t0 = pl.multiple_of(i * LRU_TT, LRU_TT)
    p0 = pl.multiple_of(jnp.maximum(t0 - 8, 0), 8)
    n0 = pl.multiple_of(jnp.minimum(t0 + LRU_TT, S - 8), 8)
    prev = jnp.where(i > 0, ref[pl.ds(p0, 8), :], 0.0)
    nxt = jnp.where(i < nt - 1, ref[pl.ds(n0, 8), :], 0.0)
    return jnp.concatenate([prev, ref[pl.ds(t0, LRU_TT), :], nxt], axis=0)


def _shift(ext, k):
    n = LRU_TT + 16
    return pltpu.roll(ext, (-k) % n, 0)[8:8 + LRU_TT]


def _lru_gates(uc, wbd, ba, bx):
    pre = jnp.dot(uc.astype(bf16), wbd, preferred_element_type=f32)
    r_f = _sigmoid(pre[:, 0:CW] + ba[0:1])
    i_f = _sigmoid(pre[:, CW:2 * CW] + bx[0:1])
    r_b = _sigmoid(pre[:, 2 * CW:3 * CW] + ba[1:2])
    i_b = _sigmoid(pre[:, 3 * CW:4 * CW] + bx[1:2])
    return r_f, i_f, r_b, i_b


def _lru_coeffs(r, sp):
    log_a = (-RGLRU_C * r) * sp
    a = jnp.exp(log_a)
    beta = jnp.sqrt(jnp.maximum(_neg_expm1(2.0 * log_a), 0.0))
    return a, beta


def _conv_tile(u_ref, i, S, cw, cb):
    ext = _halo(u_ref, i, S)
    um2, um1, u0, up1 = _shift(ext, -2), _shift(ext, -1), ext[8:8 + LRU_TT], _shift(ext, 1)
    uc = um2 * cw[0:1] + um1 * cw[1:2] + u0 * cw[2:3] + up1 * cw[3:4] + cb
    return uc, (um2, um1, u0, up1)


def _scan_pair(S, fwd_a, fwd_b, fwd_out, rev_a, rev_b, rev_out):
    ng = S // 8
    idx = lax.broadcasted_iota(jnp.int32, (8, CW), 0)

    def local(a, b, rev):
        for sh in (1, 2, 4):
            if rev:
                keep = idx < 8 - sh
                amt = 8 - sh
            else:
                keep = idx >= sh
                amt = sh
            a_s = jnp.where(keep, pltpu.roll(a, amt, 0), 1.0)
            b_s = jnp.where(keep, pltpu.roll(b, amt, 0), 0.0)
            b = a * b_s + b
            a = a * a_s
        return a, b

    def step(g, carry):
        cf, cr = carry
        r0 = pl.multiple_of(g * 8, 8)
        a, b = local(fwd_a(r0), fwd_b(r0), False)
        h = a * cf + b
        fwd_out[pl.ds(r0, 8), :] = h
        cf = jnp.broadcast_to(h[7:8, :], (8, CW))
        r1 = pl.multiple_of((ng - 1 - g) * 8, 8)
        a, b = local(rev_a(r1), rev_b(r1), True)
        h = a * cr + b
        rev_out[pl.ds(r1, 8), :] = h
        cr = jnp.broadcast_to(h[0:1, :], (8, CW))
        return cf, cr

    zero = jnp.zeros((8, CW), f32)
    lax.fori_loop(0, ng, step, (zero, zero), unroll=4)


def _lru_specs(S):
    seq = lambda off: pl.BlockSpec((S, CW), lambda j: (0, off + j))
    par = lambda rows: pl.BlockSpec((rows, CW), lambda j: (0, j))
    return seq, par


def _lru_fwd(proj, conv_w, conv_b, lam, ba, bx, wbd):
    S = proj.shape[0]
    nt = S // LRU_TT

    def body(u_ref, g_ref, cw_ref, cb_ref, lam_ref, ba_ref, bx_ref, wbd_ref, y_ref, af_ref, bf_ref, ab_ref, bb_ref):
        cw, cb, ba_v, bx_v, wbd_v = cw_ref[...], cb_ref[...], ba_ref[...], bx_ref[...], wbd_ref[...]
        sp = jax.nn.softplus(-lam_ref[...])

        def phase1(i, c):
            uc, _ = _conv_tile(u_ref, i, S, cw, cb)
            r_f, i_f, r_b, i_b = _lru_gates(uc, wbd_v, ba_v, bx_v)
            rows = pl.ds(pl.multiple_of(i * LRU_TT, LRU_TT), LRU_TT)
            a, beta = _lru_coeffs(r_f, sp[0:1])
            af_ref[rows, :] = a
            bf_ref[rows, :] = beta * (i_f * uc)
            a, beta = _lru_coeffs(r_b, sp[1:2])
            ab_ref[rows, :] = a
            bb_ref[rows, :] = beta * (i_b * uc)
            return c

        lax.fori_loop(0, nt, phase1, 0)
        row8 = lambda ref: (lambda r0: ref[pl.ds(r0, 8), :])
        _scan_pair(S, row8(af_ref), row8(bf_ref), bf_ref, row8(ab_ref), row8(bb_ref), bb_ref)

        def phase3(i, c):
            rows = pl.ds(pl.multiple_of(i * LRU_TT, LRU_TT), LRU_TT)
            y_ref[rows, :] = (bf_ref[rows, :] + bb_ref[rows, :]) * jax.nn.gelu(g_ref[rows, :])
            return c

        lax.fori_loop(0, nt, phase3, 0)

    seq, par = _lru_specs(S)
    return pl.pallas_call(
        body, name="lru_fwd", grid=(NCH,),
        in_specs=[seq(0), seq(NCH), par(4), par(1), par(2), par(2), par(2),
                  pl.BlockSpec((None, CW, 4 * CW), lambda j: (j, 0, 0))],
        out_specs=seq(0), out_shape=_sds((S, D), f32),
        scratch_shapes=[pltpu.VMEM((S, CW), f32)] * 4, compiler_params=_params(1, True),
    )(proj, proj, conv_w, conv_b, lam, ba, bx, wbd)


def _lru_bwd(proj, dy, conv_w, conv_b, lam, ba, bx, wbd):
    S = proj.shape[0]
    nt = S // LRU_TT

    def body(u_ref, g_ref, dy_ref, cw_ref, cb_ref, lam_ref, ba_ref, bx_ref, wbd_ref,
             du_ref, dg_ref, dcw_ref, dcb_ref, dlam_ref, dba_ref, dbx_ref, dwbd_ref,
             af_ref, bf_ref, ab_ref, bb_ref):
        cw, cb, ba_v, bx_v, wbd_v = cw_ref[...], cb_ref[...], ba_ref[...], bx_ref[...], wbd_ref[...]
        lam_v = lam_ref[...]
        sp = jax.nn.softplus(-lam_v)

        def phase1(i, c):
            uc, _ = _conv_tile(u_ref, i, S, cw, cb)
            r_f, i_f, r_b, i_b = _lru_gates(uc, wbd_v, ba_v, bx_v)
            rows = pl.ds(pl.multiple_of(i * LRU_TT, LRU_TT), LRU_TT)
            a, beta = _lru_coeffs(r_f, sp[0:1])
            af_ref[rows, :] = a
            bf_ref[rows, :] = beta * (i_f * uc)
            a, beta = _lru_coeffs(r_b, sp[1:2])
            ab_ref[rows, :] = a
            bb_ref[rows, :] = beta * (i_b * uc)
            return c

        lax.fori_loop(0, nt, phase1, 0)
        row8 = lambda ref: (lambda r0: ref[pl.ds(r0, 8), :])
        _scan_pair(S, row8(af_ref), row8(bf_ref), bf_ref, row8(ab_ref), row8(bb_ref), bb_ref)

        def scaled_dh(a_ref):
            def f(r0):
                gl, _ = _gelu_and_grad(g_ref[pl.ds(r0, 8), :])
                return a_ref[pl.ds(r0, 8), :] * (dy_ref[pl.ds(r0, 8), :] * gl)
            return f

        _scan_pair(S, row8(ab_ref), scaled_dh(ab_ref), ab_ref, row8(af_ref), scaled_dh(af_ref), af_ref)

        dcw_ref[...] = jnp.zeros_like(dcw_ref)
        dcb_ref[...] = jnp.zeros_like(dcb_ref)
        dlam_ref[...] = jnp.zeros_like(dlam_ref)
        dba_ref[...] = jnp.zeros_like(dba_ref)
        dbx_ref[...] = jnp.zeros_like(dbx_ref)
        dwbd_ref[...] = jnp.zeros_like(dwbd_ref)

        def direction(uc, r, i_g, dht, h_nb, sp_d):
            a, beta = _lru_coeffs(r, sp_d)
            da = dht * h_nb
            dbeta = dht * (i_g * uc)
            d_iu = dht * beta
            dlog_a = da * a - (a * a) * (dbeta / beta)
            dr = dlog_a * (-RGLRU_C * sp_d)
            dsp = jnp.sum(dlog_a * (-RGLRU_C * r), axis=0, keepdims=True)
            dpre_r = dr * (r * (1.0 - r))
            dpre_i = (d_iu * uc) * (i_g * (1.0 - i_g))
            return dpre_r, dpre_i, d_iu * i_g, dsp

        def phase4(i, c):
            uc, (um2, um1, u0, up1) = _conv_tile(u_ref, i, S, cw, cb)
            r_f, i_f, r_b, i_b = _lru_gates(uc, wbd_v, ba_v, bx_v)
            rows = pl.ds(pl.multiple_of(i * LRU_TT, LRU_TT), LRU_TT)
            gl, dgl = _gelu_and_grad(g_ref[rows, :])
            dyt = dy_ref[rows, :]
            dh = dyt * gl
            dg_ref[rows, :] = ((dyt * (bf_ref[rows, :] + bb_ref[rows, :])) * dgl).astype(dg_ref.dtype)
            dht_f = dh + _shift(_halo(af_ref, i, S), 1)
            h_prev = _shift(_halo(bf_ref, i, S), -1)
            dht_b = dh + _shift(_halo(ab_ref, i, S), -1)
            h_next = _shift(_halo(bb_ref, i, S), 1)
            prf, pif, duc_f, dsp_f = direction(uc, r_f, i_f, dht_f, h_prev, sp[0:1])
            prb, pib, duc_b, dsp_b = direction(uc, r_b, i_b, dht_b, h_next, sp[1:2])
            dpre = jnp.concatenate([prf, pif, prb, pib], axis=1)
            dpre_b = dpre.astype(bf16)
            duc = (duc_f + duc_b) + lax.dot_general(dpre_b, wbd_v, _DIMS["nt"], preferred_element_type=f32)
            dwbd_ref[...] += lax.dot_general(uc.astype(bf16), dpre_b, _DIMS["tn"], preferred_element_type=f32)
            colsum = lambda v: jnp.sum(v, axis=0, keepdims=True)
            dba_ref[...] += jnp.concatenate([colsum(prf), colsum(prb)], axis=0)
            dbx_ref[...] += jnp.concatenate([colsum(pif), colsum(pib)], axis=0)
            dlam_ref[...] += jnp.concatenate([dsp_f, dsp_b], axis=0)
            dcb_ref[...] += colsum(duc)
            dcw_ref[...] += jnp.concatenate([colsum(duc * um2), colsum(duc * um1), colsum(duc * u0),
                                             colsum(duc * up1)], axis=0)
            af_ref[rows, :] = duc
            return c

        lax.fori_loop(0, nt, phase4, 0)
        dlam_ref[...] = dlam_ref[...] * (-_sigmoid(-lam_v))

        def phase5(i, c):
            ext = _halo(af_ref, i, S)
            rows = pl.ds(pl.multiple_of(i * LRU_TT, LRU_TT), LRU_TT)
            du = (_shift(ext, 2) * cw[0:1] + _shift(ext, 1) * cw[1:2] + ext[8:8 + LRU_TT] * cw[2:3]
                  + _shift(ext, -1) * cw[3:4])
            du_ref[rows, :] = du.astype(du_ref.dtype)
            return c

        lax.fori_loop(0, nt, phase5, 0)

    seq, par = _lru_specs(S)
    return pl.pallas_call(
        body, name="lru_bwd", grid=(NCH,),
        in_specs=[seq(0), seq(NCH), seq(0), par(4), par(1), par(2), par(2), par(2),
                  pl.BlockSpec((None, CW, 4 * CW), lambda j: (j, 0, 0))],
        out_specs=[seq(0), seq(0), par(4), par(1), par(2), par(2), par(2),
                   pl.BlockSpec((None, CW, 4 * CW), lambda j: (j, 0, 0))],
        out_shape=[_sds((S, D), bf16), _sds((S, D), bf16), _sds((4, D), f32), _sds((1, D), f32), _sds((2, D), f32),
                   _sds((2, D), f32), _sds((2, D), f32), _sds((NCH, CW, 4 * CW), f32)],
        scratch_shapes=[pltpu.VMEM((S, CW), f32)] * 4, compiler_params=_params(1, True),
    )(proj, proj, dy, conv_w, conv_b, lam, ba, bx, wbd)


_SLOPES = [2.0 ** (-8.0 * (h + 1) / NH) for h in range(NH)]


def _half_mask(shape, e):
    lane = lax.broadcasted_iota(jnp.int32, shape, 1)
    return (lane < HD) if e == 0 else (lane >= HD)


def _place(x, src, dst):
    if src != dst:
        x = pltpu.roll(x, HD, 1)
    return jnp.where(_half_mask(x.shape, dst), x, 0.0)


def _attn_geometry(n, S):
    tq = lax.broadcasted_iota(jnp.int32, (BLK, 3 * BLK), 0)
    sk = lax.broadcasted_iota(jnp.int32, (BLK, 3 * BLK), 1)
    dist = jnp.abs(tq + BLK - sk)
    kpos = n * BLK - BLK + sk
    valid = (dist <= BLK) & (kpos >= 0) & (kpos < S)
    return dist.astype(f32), valid


def _attn_probs(qc, km, absd, valid, slope, sink):
    s = lax.dot_general(qc, km, _DIMS["nt"], preferred_element_type=f32) * (HD ** -0.5)
    s = jnp.where(valid, s + (-slope) * absd, NEG_INF)
    m = jnp.maximum(jnp.max(s, axis=-1, keepdims=True), sink)
    p = jnp.exp(s - m)
    esink = jnp.exp(sink - m)
    den = jnp.sum(p, axis=-1, keepdims=True) + esink
    return p / den, esink / den


def _attn_specs(S):
    nb = S // BLK
    q_spec = pl.BlockSpec((BLK, D), lambda n: (n, 2))
    kv = lambda col: [pl.BlockSpec((BLK, 256), lambda n: (jnp.maximum(n - 1, 0), col)),
                      pl.BlockSpec((BLK, 256), lambda n: (n, col)),
                      pl.BlockSpec((BLK, 256), lambda n: (jnp.minimum(n + 1, nb - 1), col))]
    return nb, q_spec, kv(COL_K), kv(COL_V)


def _attn_fwd(proj, sink):
    S = proj.shape[0]
    nb, q_spec, k_specs, v_specs = _attn_specs(S)

    def body(sink_ref, q_ref, kp_ref, kc_ref, kn_ref, vp_ref, vc_ref, vn_ref, o_ref):
        absd, valid = _attn_geometry(pl.program_id(0), S)
        kcat = jnp.concatenate([kp_ref[...], kc_ref[...], kn_ref[...]], axis=0)
        vcat = jnp.concatenate([vp_ref[...], vc_ref[...], vn_ref[...]], axis=0)
        for pair in range(NH // 2):
            kvh = pair // 2
            kch = kcat[:, (kvh // 2) * 128:(kvh // 2 + 1) * 128]
            vch = vcat[:, (kvh // 2) * 128:(kvh // 2 + 1) * 128]
            qc = q_ref[:, pair * 128:(pair + 1) * 128].astype(bf16)
            acc = jnp.zeros((BLK, 128), f32)
            for e in range(2):
                h = 2 * pair + e
                km = _place(kch, kvh % 2, e).astype(bf16)
                vm = _place(vch, kvh % 2, e).astype(bf16)
                pn, _ = _attn_probs(qc, km, absd, valid, _SLOPES[h], sink_ref[0, h])
                acc = acc + jnp.dot(pn.astype(bf16), vm, preferred_element_type=f32)
            o_ref[:, pair * 128:(pair + 1) * 128] = acc

    return pl.pallas_call(
        body, name="attn_fwd", grid=(nb,),
        in_specs=[pl.BlockSpec(memory_space=pltpu.SMEM), q_spec] + k_specs + v_specs,
        out_specs=pl.BlockSpec((BLK, D), lambda n: (n, 0)), out_shape=_sds((S, D), f32),
        compiler_params=_params(1, True))(sink, proj, proj, proj, proj, proj, proj, proj)


def _attn_bwd(proj, sink, y_b, dy_b):
    S = proj.shape[0]
    nb, q_spec, k_specs, v_specs = _attn_specs(S)

    def body(sink_ref, q_ref, kp_ref, kc_ref, kn_ref, vp_ref, vc_ref, vn_ref, o_ref, do_ref,
             dq_ref, dk_ref, dv_ref, dsink_ref):
        n = pl.program_id(0)

        @pl.when(n == 0)
        def _():
            dk_ref[...] = jnp.zeros_like(dk_ref)
            dv_ref[...] = jnp.zeros_like(dv_ref)
            dsink_ref[...] = jnp.zeros_like(dsink_ref)

        absd, valid = _attn_geometry(n, S)
        kcat = jnp.concatenate([kp_ref[...], kc_ref[...], kn_ref[...]], axis=0)
        vcat = jnp.concatenate([vp_ref[...], vc_ref[...], vn_ref[...]], axis=0)
        dk_acc = [jnp.zeros((3 * BLK, 128), f32), jnp.zeros((3 * BLK, 128), f32)]
        dv_acc = [jnp.zeros((3 * BLK, 128), f32), jnp.zeros((3 * BLK, 128), f32)]
        scale = HD ** -0.5
        for pair in range(NH // 2):
            kvh = pair // 2
            ch = kvh // 2
            kch = kcat[:, ch * 128:(ch + 1) * 128]
            vch = vcat[:, ch * 128:(ch + 1) * 128]
            cols = slice(pair * 128, (pair + 1) * 128)
            qc = q_ref[:, cols].astype(bf16)
            do32 = do_ref[:, cols]
            dob = do32.astype(bf16)
            prod = do32 * o_ref[:, cols]
            dq_acc = jnp.zeros((BLK, 128), f32)
            for e in range(2):
                h = 2 * pair + e
                km = _place(kch, kvh % 2, e).astype(bf16)
                vm = _place(vch, kvh % 2, e).astype(bf16)
                pn, psink = _attn_probs(qc, km, absd, valid, _SLOPES[h], sink_ref[0, h])
                delta = jnp.sum(jnp.where(_half_mask(prod.shape, e), prod, 0.0), axis=-1, keepdims=True)
                dp = lax.dot_general(dob, vm, _DIMS["nt"], preferred_element_type=f32)
                ds = (pn * (dp - delta)).astype(bf16)
                dsink_ref[h:h + 1, :] += jnp.broadcast_to(-jnp.sum(psink * delta, axis=0, keepdims=True), (1, 128))
                dq_acc = dq_acc + jnp.dot(ds, km, preferred_element_type=f32) * scale
                ck = lax.dot_general(ds, qc, _DIMS["tn"], preferred_element_type=f32) * scale
                cv = lax.dot_general(pn.astype(bf16), dob, _DIMS["tn"], preferred_element_type=f32)
                dk_acc[ch] = dk_acc[ch] + _place(ck, e, kvh % 2)
                dv_acc[ch] = dv_acc[ch] + _place(cv, e, kvh % 2)
            dq_ref[:, cols] = dq_acc.astype(dq_ref.dtype)
        for j in range(3):
            blk = n + (j - 1)

            @pl.when((blk >= 0) & (blk < nb))
            def _():
                rows = pl.ds(pl.multiple_of(blk * BLK, BLK), BLK)
                for ch in range(2):
                    dk_ref[rows, ch * 128:(ch + 1) * 128] += dk_acc[ch][j * BLK:(j + 1) * BLK]
                    dv_ref[rows, ch * 128:(ch + 1) * 128] += dv_acc[ch][j * BLK:(j + 1) * BLK]

    row_blk = pl.BlockSpec((BLK, D), lambda n: (n, 0))
    full = pl.BlockSpec((S, 256), lambda n: (0, 0))
    return pl.pallas_call(
        body, name="attn_bwd", grid=(nb,),
        in_specs=[pl.BlockSpec(memory_space=pltpu.SMEM), q_spec] + k_specs + v_specs + [row_blk, row_blk],
        out_specs=[row_blk, full, full, pl.BlockSpec((NH, 128), lambda n: (0, 0))],
        out_shape=[_sds((S, D), bf16), _sds((S, 256), f32), _sds((S, 256), f32), _sds((NH, 128), f32)],
        compiler_params=_params(1, True))(sink, proj, proj, proj, proj, proj, proj, proj, y_b, dy_b)


def _adamw(name, w, g, m, v, tr):
    R, C = w.shape
    tr = min(tr, R)

    def body(w_ref, g_ref, m_ref, v_ref, d_ref, m2_ref, v2_ref):
        g = g_ref[...]
        m2 = ADAM_B1 * m_ref[...] + (1.0 - ADAM_B1) * g
        v2 = ADAM_B2 * v_ref[...] + (1.0 - ADAM_B2) * (g * g)
        m_hat = m2 / (1.0 - ADAM_B1 ** ADAM_STEP)
        v_hat = v2 / (1.0 - ADAM_B2 ** ADAM_STEP)
        d_ref[...] = -ADAM_LR * (m_hat / (jnp.sqrt(v_hat) + ADAM_EPS) + ADAM_WD * w_ref[...])
        m2_ref[...] = m2
        v2_ref[...] = v2

    blk = pl.BlockSpec((tr, C), lambda i: (i, 0))
    return pl.pallas_call(body, name=name, grid=(R // tr,), in_specs=[blk] * 4, out_specs=[blk] * 3,
                          out_shape=[_sds((R, C), f32)] * 3, compiler_params=_params(1))(w, g, m, v)


def _pair_sum(name, c_arr, g4, recv, th):
    _, _, h, w = g4.shape
    th = min(th, h)

    def body(c_ref, g_ref, r_ref, o_ref, ob_ref):
        p = g_ref[...] + r_ref[...]
        o_ref[...] = p
        ob_ref[...] = p.astype(bf16)

    blk = pl.BlockSpec((None, th, w), lambda s, i, c_ref: (s, i, 0))
    spec = pltpu.PrefetchScalarGridSpec(
        num_scalar_prefetch=1, grid=(NCHIP, h // th),
        in_specs=[pl.BlockSpec((None, None, th, w), lambda s, i, c_ref: (s, c_ref[0], i, 0)), blk],
        out_specs=[blk, blk])
    return pl.pallas_call(body, name=name, grid_spec=spec,
                          out_shape=[_sds((NCHIP, h, w), f32), _sds((NCHIP, h, w), bf16)],
                          compiler_params=_params(2))(c_arr, g4, recv)


def _chip_sum(name, chip_arr, own4, recv3, th):
    _, h, w = own4.shape
    th = min(th, h)

    def body(s_ref, o_ref, r_ref, out_ref):
        out_ref[...] = ((o_ref[...] + r_ref[0].astype(f32)) + r_ref[1].astype(f32)) + r_ref[2].astype(f32)

    spec = pltpu.PrefetchScalarGridSpec(
        num_scalar_prefetch=1, grid=(h // th,),
        in_specs=[pl.BlockSpec((None, th, w), lambda i, s_ref: (s_ref[0], i, 0)),
                  pl.BlockSpec((3, th, w), lambda i, s_ref: (0, i, 0))],
        out_specs=pl.BlockSpec((th, w), lambda i, s_ref: (i, 0)))
    return pl.pallas_call(body, name=name, grid_spec=spec, out_shape=_sds((h, w), f32),
                          compiler_params=_params(1, True))(chip_arr, own4, recv3)


def _adamw_halves(name, c_arr, w, g_own, g_recv, m, v, th):
    h, wd = g_own.shape
    th = min(th, h)

    def body(c_ref, w_ref, go_ref, gr_ref, m_ref, v_ref, g_ref, d_ref, m2_ref, v2_ref):
        g = jnp.where(c_ref[0] == pl.program_id(0), go_ref[...], gr_ref[...])
        m2 = ADAM_B1 * m_ref[...] + (1.0 - ADAM_B1) * g
        v2 = ADAM_B2 * v_ref[...] + (1.0 - ADAM_B2) * (g * g)
        m_hat = m2 / (1.0 - ADAM_B1 ** ADAM_STEP)
        v_hat = v2 / (1.0 - ADAM_B2 ** ADAM_STEP)
        g_ref[...] = g
        d_ref[...] = -ADAM_LR * (m_hat / (jnp.sqrt(v_hat) + ADAM_EPS) + ADAM_WD * w_ref[...])
        m2_ref[...] = m2
        v2_ref[...] = v2

    nt = h // th
    full = pl.BlockSpec((th, wd), lambda hh, i, c_ref: (hh * nt + i, 0))
    half = pl.BlockSpec((th, wd), lambda hh, i, c_ref: (i, 0))
    spec = pltpu.PrefetchScalarGridSpec(num_scalar_prefetch=1, grid=(2, nt),
                                        in_specs=[full, half, half, full, full], out_specs=[full] * 4)
    return pl.pallas_call(body, name=name, grid_spec=spec, out_shape=[_sds((2 * h, wd), f32)] * 4,
                          compiler_params=_params(2))(c_arr, w, g_own, g_recv, m, v)


def _add2(name, a, b):
    def body(a_ref, b_ref, o_ref):
        o_ref[...] = a_ref[...] + b_ref[...]
    return pl.pallas_call(body, name=name, out_shape=_sds(a.shape, f32))(a, b)


def _sum4(name, b4, th):
    _, h, w = b4.shape
    th = min(th, h)

    def body(b_ref, o_ref):
        o_ref[...] = ((b_ref[0] + b_ref[1]) + b_ref[2]) + b_ref[3]

    return pl.pallas_call(body, name=name, grid=(h // th,),
                          in_specs=[pl.BlockSpec((NCHIP, th, w), lambda i: (0, i, 0))],
                          out_specs=pl.BlockSpec((th, w), lambda i: (i, 0)), out_shape=_sds((h, w), f32),
                          compiler_params=_params(1, True))(b4)


def _coords():
    x, y, c = lax.axis_index("x"), lax.axis_index("y"), lax.axis_index("c")
    return x, y, c, [(1 - x, y), (x, 1 - y), (1 - x, 1 - y)]


def _gather_chips(arrs):
    n = len(arrs)

    def body(*refs):
        ins, outs = refs[:n], refs[n:2 * n]
        send_sems, recv_sems, local_sems = refs[2 * n:]
        x, y, c, chips = _coords()
        s = 2 * x + y
        sib = (x, y, 1 - c)
        local = [pltpu.make_async_copy(ins[a], outs[a].at[s], local_sems.at[a]) for a in range(n)]
        for cp in local:
            cp.start()

        def over_ici(k, a, slot, peer):
            return pltpu.make_async_remote_copy(src_ref=ins[a].at[c], dst_ref=outs[a].at[slot, c], send_sem=send_sems.at[k * n + a],
                                                recv_sem=recv_sems.at[k * n + a], device_id=peer, device_id_type=MESH)

        def to_sibling(k, a, slot, half):
            i = (3 + k) * n + a
            return pltpu.make_async_remote_copy(src_ref=outs[a].at[slot, half], dst_ref=outs[a].at[slot, half], send_sem=send_sems.at[i],
                                                recv_sem=recv_sems.at[i], device_id=sib, device_id_type=MESH)

        sends = [over_ici(k, a, s, (px, py, c)) for k, (px, py) in enumerate(chips) for a in range(n)]
        for cp in sends:
            cp.start()
        passed = []
        for k, (px, py) in enumerate(chips):
            for a in range(n):
                over_ici(k, a, 2 * px + py, (px, py, c)).wait_recv()
                cp = to_sibling(k, a, 2 * px + py, c)
                cp.start()
                passed.append(cp)
        for k, (px, py) in enumerate(chips):
            for a in range(n):
                to_sibling(k, a, 2 * px + py, 1 - c).wait_recv()
        for cp in sends + passed:
            cp.wait_send()
        for cp in local:
            cp.wait()

    return pl.pallas_call(
        body, name="gather_weights", in_specs=[ANY] * n, out_specs=[ANY] * n,
        out_shape=[_sds((NCHIP,) + a.shape, a.dtype) for a in arrs],
        scratch_shapes=[pltpu.SemaphoreType.DMA((6 * n,)), pltpu.SemaphoreType.DMA((6 * n,)), pltpu.SemaphoreType.DMA((n,))],
    )(*arrs)


def _sibling_halves(g4s, small):
    n = len(g4s)

    def body(*refs):
        ins, small_ref = refs[:n], refs[n]
        outs, small_out = refs[n + 1:2 * n + 1], refs[2 * n + 1]
        send_sems, recv_sems = refs[2 * n + 2:]
        x, y, c, _ = _coords()
        sib = (x, y, 1 - c)

        def remote(a, half):
            src = small_ref if a == n else ins[a].at[:, half]
            dst = small_out if a == n else outs[a]
            return pltpu.make_async_remote_copy(src_ref=src, dst_ref=dst, send_sem=send_sems.at[a], recv_sem=recv_sems.at[a],
                                                device_id=sib, device_id_type=MESH)

        sends = [remote(a, 1 - c) for a in range(n + 1)]
        for cp in sends:
            cp.start()
        for a in range(n + 1):
            remote(a, c).wait_recv()
        for cp in sends:
            cp.wait_send()

    return pl.pallas_call(
        body, name="reduce_sibling", in_specs=[ANY] * (n + 1), out_specs=[ANY] * (n + 1),
        out_shape=[_sds((g.shape[0],) + g.shape[2:], f32) for g in g4s] + [_sds(small.shape, f32)],
        scratch_shapes=[pltpu.SemaphoreType.DMA((n + 1,)), pltpu.SemaphoreType.DMA((n + 1,))],
    )(*g4s, small)


def _exchange_chips(parts, small2):
    n = len(parts)

    def body(*refs):
        ins, small_ref = refs[:n], refs[n]
        outs, small_out = refs[n + 1:2 * n + 1], refs[2 * n + 1]
        send_sems, recv_sems, local_sem = refs[2 * n + 2:]
        x, y, c, chips = _coords()
        s = 2 * x + y
        local = pltpu.make_async_copy(small_ref.at[c], small_out.at[s], local_sem)
        local.start()

        def remote(k, a, dest_chip, small_slot, peer):
            if a == n:
                src, dst = small_ref.at[c], small_out.at[small_slot]
            else:
                src, dst = ins[a].at[dest_chip], outs[a].at[k]
            i = k * (n + 1) + a
            return pltpu.make_async_remote_copy(src_ref=src, dst_ref=dst, send_sem=send_sems.at[i], recv_sem=recv_sems.at[i],
                                                device_id=peer, device_id_type=MESH)

        sends = [remote(k, a, 2 * px + py, s, (px, py, c)) for k, (px, py) in enumerate(chips) for a in range(n + 1)]
        for cp in sends:
            cp.start()
        for k, (px, py) in enumerate(chips):
            for a in range(n + 1):
                remote(k, a, s, 2 * px + py, (px, py, c)).wait_recv()
        for cp in sends:
            cp.wait_send()
        local.wait()

    m = 3 * (n + 1)
    return pl.pallas_call(
        body, name="reduce_chips", in_specs=[ANY] * (n + 1), out_specs=[ANY] * (n + 1),
        out_shape=[_sds((3,) + p.shape[1:], p.dtype) for p in parts] + [_sds((NCHIP,) + small2.shape[1:], f32)],
        scratch_shapes=[pltpu.SemaphoreType.DMA((m,)), pltpu.SemaphoreType.DMA((m,)), pltpu.SemaphoreType.DMA],
    )(*parts, small2)


def _share_sibling(halves):
    n = len(halves)

    def body(*refs):
        ins, outs = refs[:n], refs[n:2 * n]
        send_sems, recv_sems = refs[2 * n:]
        x, y, c, _ = _coords()
        sib = (x, y, 1 - c)
        sends = [pltpu.make_async_remote_copy(src_ref=ins[a], dst_ref=outs[a], send_sem=send_sems.at[a], recv_sem=recv_sems.at[a],
                                              device_id=sib, device_id_type=MESH) for a in range(n)]
        for cp in sends:
            cp.start()
        for cp in sends:
            cp.wait()

    return pl.pallas_call(
        body, name="reduce_share", in_specs=[ANY] * n, out_specs=[ANY] * n,
        out_shape=[_sds(h.shape, f32) for h in halves],
        scratch_shapes=[pltpu.SemaphoreType.DMA((n,)), pltpu.SemaphoreType.DMA((n,))],
    )(*halves)


def _block_diag_pairs(w):
    w = w.reshape(NCH, 2, HD, HD)
    z = jnp.zeros((NCH, HD, HD), w.dtype)
    return jnp.concatenate([jnp.concatenate([w[:, 0], z], axis=2), jnp.concatenate([z, w[:, 1]], axis=2)], axis=1)


def _diag_blocks(m):
    return jnp.stack([m[:, :HD, :HD], m[:, HD:, HD:]], axis=1).reshape(NH, HD, HD)


def _pack(vs, rows):
    flat = jnp.concatenate([v.reshape(-1) for v in vs])
    return jnp.pad(flat, (0, rows * 128 - flat.shape[0])).reshape(rows, 128)


def _unpack(packed, shapes):
    flat = packed.reshape(-1)
    out, off = [], 0
    for shp in shapes:
        size = math.prod(shp)
        out.append(flat[off:off + size].reshape(shp))
        off += size
    return out


def _rows_for(sizes, multiple):
    rows = -(-sum(sizes) // 128)
    return -(-rows // multiple) * multiple


def kernel(x, norm_mix_g, w_in, b_gate, conv_w, conv_b, lru_lambda, lru_wa, lru_ba, lru_wx, lru_bx, attn_sink, w_out, norm_ffn_g, w_ffn_in, w_ffn_out, norm_final_g, loss_target, m_norm_mix_g, m_w_in, m_b_gate, m_conv_w, m_conv_b, m_lru_lambda, m_lru_wa, m_lru_ba, m_lru_wx, m_lru_bx, m_attn_sink, m_w_out, m_norm_ffn_g, m_w_ffn_in, m_w_ffn_out, m_norm_final_g, v_norm_mix_g, v_w_in, v_b_gate, v_conv_w, v_conv_b, v_lru_lambda, v_lru_wa, v_lru_ba, v_lru_wx, v_lru_bx, v_attn_sink, v_w_out, v_norm_ffn_g, v_w_ffn_in, v_w_ffn_out, v_norm_final_g):
    S = x.shape[1]
    xs = x[0]
    tgt = loss_target[0]
    cx, cy, cc = lax.axis_index("x"), lax.axis_index("y"), lax.axis_index("c")
    chip = 2 * cx + cy
    SW = D // NCHIP

    small_shard = _pack([conv_w[0], lru_lambda[0], lru_ba[0], lru_bx[0]], 32)
    halves_of = lambda a: a.reshape(2, a.shape[0] // 2, a.shape[1])
    w_in_g, w_ffn_in_g, w_out_g, w_ffn_out_g, small_g = _gather_chips(
        [halves_of(w_in[0].astype(bf16)), halves_of(w_ffn_in[0].astype(bf16)), halves_of(w_out[0].astype(bf16)),
         halves_of(w_ffn_out[0].astype(bf16)), halves_of(small_shard)])
    w_in_g, w_ffn_in_g = w_in_g.reshape(NCHIP, D, SHW), w_ffn_in_g.reshape(NCHIP, D, SHW)
    w_out_f = w_out_g.reshape(D, D)
    w_ffn_out_f = w_ffn_out_g.reshape(FF, D)
    small_g = small_g.reshape(NCHIP, 32, 128)
    small_parts = [_unpack(small_g[s], [(4, SW), (2, SW), (2, SW), (2, SW)]) for s in range(NCHIP)]
    conv_w_f, lam_f, ba_f, bx_f = [jnp.concatenate([small_parts[s][p] for s in range(NCHIP)], axis=1) for p in range(4)]
    wbd = jnp.concatenate([_block_diag_pairs(lru_wa[0, 0]), _block_diag_pairs(lru_wx[0, 0]),
                           _block_diag_pairs(lru_wa[0, 1]), _block_diag_pairs(lru_wx[0, 1])], axis=2).astype(bf16)
    conv_b_f = conv_b
    sink = attn_sink

    xn, proj = _rms_matmul("rms_proj", xs, norm_mix_g, w_in_g, 1024)
    y_a = _lru_fwd(proj, conv_w_f, conv_b_f, lam_f, ba_f, bx_f, wbd)
    y_b = _attn_fwd(proj, sink)
    merged = _merge_fwd(proj, b_gate, y_a, y_b, 512)
    x1 = _mm_residual("out_proj", merged, w_out_f, xs, 512)
    xn2, gu = _rms_matmul("rms_ffn_in", x1, norm_ffn_g, w_ffn_in_g, 1024)
    act = _swiglu_fwd(gu, 512)
    x2 = _mm_residual("ffn_out", act, w_ffn_out_f, x1, 512)
    dx2, loss_row, dg3 = _final_loss_bwd(x2, norm_final_g.reshape(1, D), tgt, 256)
    loss = lax.psum(loss_row[0, 0], ("x", "y", "c"))

    tm = min(512, S)
    tk = min(512, S)
    d_act = _mm_nt_resident("d_act", dx2, w_ffn_out_f, 512)
    gw_ffn_out = _mm_tn("dw_ffn_out", act, pl.BlockSpec((tk, SHW), lambda i, k: (k, i)),
                        dx2, pl.BlockSpec((tk, D), lambda i, k: (k, 0)),
                        _sds((FF, D), f32), pl.BlockSpec((SHW, D), lambda i, k: (i, 0)), (2, S // tk), (SHW, D))
    dgu = _swiglu_bwd(gu, d_act, 512)
    dxn2 = _mm_nt_groups("dxn2", dgu, pl.BlockSpec((None, tm, SHW), lambda i, g: (g // 2, i, g % 2)), w_ffn_in_g, S, tm)
    gw_ffn_in = _mm_tn("dw_ffn_in", xn2, pl.BlockSpec((tk, D), lambda g, k: (k, 0)),
                       dgu, pl.BlockSpec((None, tk, SHW), lambda g, k: (g // 2, k, g % 2)),
                       _sds((NCHIP, D, SHW), f32), pl.BlockSpec((None, D, SHW), lambda g, k: (g, 0, 0)),
                       (NCHIP, S // tk), (D, SHW))
    dx1, dg2 = _rms_bwd("rms_ffn_bwd", x1, norm_ffn_g, dxn2, dx2, 256)

    dmerged = _mm_nt_resident("d_merged", dx1, w_out_f, 512)
    gw_out = _mm_tn("dw_out", merged, pl.BlockSpec((tk, D), lambda i, k: (k, 0)),
                    dx1, pl.BlockSpec((tk, D), lambda i, k: (k, 0)),
                    _sds((D, D), f32), pl.BlockSpec((D, D), lambda i, k: (0, 0)), (1, S // tk), (D, D))
    dz0, dz1, dy_a, dy_b, db0, db1 = _merge_bwd(proj, b_gate, y_a, y_b, dmerged, 512)
    du, dgl, dcw, dcb, dlam, dba, dbx, dwbd = _lru_bwd(proj, dy_a, conv_w_f, conv_b_f, lam_f, ba_f, bx_f, wbd)
    dq, dk, dv, dsink = _attn_bwd(proj, sink, y_b, dy_b)
    dproj = jnp.concatenate([du, dgl, dq, dk.astype(bf16), dv.astype(bf16), dz0, dz1], axis=1)
    dxn = _mm_nt_groups("dxn", dproj, pl.BlockSpec((tm, SHW), lambda i, g: (i, g)), w_in_g, S, tm)
    gw_in = _mm_tn("dw_in", xn, pl.BlockSpec((tk, D), lambda g, k: (k, 0)),
                   dproj, pl.BlockSpec((tk, SHW), lambda g, k: (k, g)),
                   _sds((NCHIP, D, SHW), f32), pl.BlockSpec((None, D, SHW), lambda g, k: (g, 0, 0)),
                   (NCHIP, S // tk), (D, SHW))
    grad_x, dg1 = _rms_bwd("rms_mix_bwd", xs, norm_mix_g, dxn, dx1, 256)

    d_wa = jnp.stack([_diag_blocks(dwbd[:, :, 0:CW]), _diag_blocks(dwbd[:, :, 2 * CW:3 * CW])])
    d_wx = jnp.stack([_diag_blocks(dwbd[:, :, CW:2 * CW]), _diag_blocks(dwbd[:, :, 3 * CW:4 * CW])])
    small_full = [dg1, jnp.concatenate([db0, db1], axis=1), dcw, dcb, dlam, d_wa, dba, d_wx, dbx, dsink[:, 0], dg2, dg3]
    full_shapes = [(1, D), (1, 2 * D), (4, D), (1, D), (2, D), (2, NH, HD, HD), (2, D), (2, NH, HD, HD), (2, D), (NH,),
                   (1, D), (1, D)]
    rows_full = _rows_for([math.prod(s) for s in full_shapes], 16)
    small_vec = _pack(small_full, rows_full)

    big = [gw_in.reshape(NCHIP, 2, D // 2, SHW), gw_ffn_in.reshape(NCHIP, 2, D // 2, SHW),
           gw_out.reshape(NCHIP, 2, D // NCHIP // 2, D), gw_ffn_out.reshape(NCHIP, 2, FF // NCHIP // 2, D)]
    *recv_a, small_sib = _sibling_halves(big, small_vec)
    c_arr = cc.reshape(1).astype(jnp.int32)
    tiles = [256, 256, 128, 352]
    names = ["w_in", "w_ffn_in", "w_out", "w_ffn_out"]
    pairs = [_pair_sum("pair_sum_" + nm, c_arr, g4, r, th) for nm, g4, r, th in zip(names, big, recv_a, tiles)]
    small_chip = _add2("pair_sum_small", small_vec, small_sib).reshape(2, rows_full // 2, 128)
    *recv_b, small_all = _exchange_chips([p[1] for p in pairs], small_chip)
    chip_arr = chip.reshape(1).astype(jnp.int32)
    halves = [_chip_sum("chip_sum_" + nm, chip_arr, p[0], r3, th) for nm, p, r3, th in zip(names, pairs, recv_b, tiles)]
    halves.append(_sum4("chip_sum_small", small_all, rows_full // 2))
    *recv_c, small_other = _share_sibling(halves)
    small_lo = jnp.where(cc == 0, halves[4], small_other)
    small_hi = jnp.where(cc == 0, small_other, halves[4])
    g_full = _unpack(jnp.concatenate([small_lo, small_hi], axis=0), full_shapes)

    out_big = {}
    for nm, w, g_own, g_recv, m, v, th in zip(names, [w_in, w_ffn_in, w_out, w_ffn_out], halves[:4], recv_c,
                                              [m_w_in, m_w_ffn_in, m_w_out, m_w_ffn_out],
                                              [v_w_in, v_w_ffn_in, v_w_out, v_w_ffn_out], tiles):
        g_, d_, m_, v_ = _adamw_halves("adamw_" + nm, c_arr, w[0], g_own, g_recv, m[0], v[0], th)
        out_big[nm] = (g_[None], d_[None], m_[None], v_[None])

    small_names = ["norm_mix_g", "b_gate", "conv_w", "conv_b", "lru_lambda", "lru_wa", "lru_ba", "lru_wx", "lru_bx", "attn_sink",
                   "norm_ffn_g", "norm_final_g"]
    sharded = {"conv_w", "lru_lambda", "lru_ba", "lru_bx"}
    small_w = [norm_mix_g, b_gate, conv_w, conv_b, lru_lambda, lru_wa, lru_ba, lru_wx, lru_bx, attn_sink, norm_ffn_g, norm_final_g]
    small_m = [m_norm_mix_g, m_b_gate, m_conv_w, m_conv_b, m_lru_lambda, m_lru_wa, m_lru_ba, m_lru_wx, m_lru_bx, m_attn_sink,
               m_norm_ffn_g, m_norm_final_g]
    small_v = [v_norm_mix_g, v_b_gate, v_conv_w, v_conv_b, v_lru_lambda, v_lru_wa, v_lru_ba, v_lru_wx, v_lru_bx, v_attn_sink,
               v_norm_ffn_g, v_norm_final_g]
    g_local = []
    for nm, g, w in zip(small_names, g_full, small_w):
        if nm in sharded:
            g = lax.dynamic_slice_in_dim(g, chip * SW, SW, axis=1)
        g_local.append(g.reshape(w.shape))
    local_shapes = [w.shape for w in small_w]
    rows_local = _rows_for([math.prod(s) for s in local_shapes], 8)
    d_s, m_s, v_s = _adamw("adamw_small", _pack(small_w, rows_local), _pack(g_local, rows_local),
                           _pack(small_m, rows_local), _pack(small_v, rows_local), rows_local)
    d_l, m_l, v_l = _unpack(d_s, local_shapes), _unpack(m_s, local_shapes), _unpack(v_s, local_shapes)
    res = {nm: (g_local[i], d_l[i], m_l[i], v_l[i]) for i, nm in enumerate(small_names)}
    res.update(out_big)

    order = ["norm_mix_g", "w_in", "b_gate", "conv_w", "conv_b", "lru_lambda", "lru_wa", "lru_ba", "lru_wx", "lru_bx", "attn_sink",
             "w_out", "norm_ffn_g", "w_ffn_in", "w_ffn_out", "norm_final_g"]
    outs = [loss, grad_x[None]]
    for k in range(4):
        outs += [res[nm][k] for nm in order]
    return tuple(outs)
```

```python
import functools
import math

import jax
import jax.numpy as jnp
from jax import lax
from jax.experimental import pallas as pl
from jax.experimental.pallas import tpu as pltpu

f32 = jnp.float32
bf16 = jnp.bfloat16

D = 1024
NH = 16
HD = 64
FF = 2816
INW = 5632
NCHIP = 4
SHW = INW // NCHIP
CW = 128
NCH = D // CW
BLK = 128
EPS = 1e-6
NEG_INF = -1e30
RGLRU_C = 8.0
ADAM_LR, ADAM_B1, ADAM_B2, ADAM_EPS, ADAM_WD, ADAM_STEP = 0.001, 0.9, 0.999, 1e-08, 0.01, 10
VMEM_LIMIT = 58 * 1024 * 1024
MESH = pl.DeviceIdType.MESH
ANY = pl.BlockSpec(memory_space=pl.ANY)

COL_U, COL_G, COL_Q, COL_K, COL_V, COL_Z0, COL_Z1 = 0, 4, 8, 12, 13, 14, 18


def _params(n_axes, vmem=False):
    return pltpu.CompilerParams(dimension_semantics=("arbitrary",) * n_axes,
                                vmem_limit_bytes=VMEM_LIMIT if vmem else None)


def _sds(shape, dtype):
    return jax.ShapeDtypeStruct(tuple(shape), dtype)


_DIMS = {"nn": (((1,), (0,)), ((), ())), "nt": (((1,), (1,)), ((), ())), "tn": (((0,), (0,)), ((), ()))}


def _mm(name, mode, a, a_spec, b, b_spec, out_shape, out_spec, grid, nk, acc_shape, add=None, add_spec=None):
    has_add = add is not None

    def body(*refs):
        a_ref, b_ref = refs[0], refs[1]
        add_ref = refs[2] if has_add else None
        o_ref = refs[2 + has_add]
        part = lax.dot_general(a_ref[...].astype(bf16), b_ref[...].astype(bf16), _DIMS[mode],
                               preferred_element_type=f32)
        if nk == 1:
            if has_add:
                part = add_ref[...] + part
            o_ref[...] = part.astype(o_ref.dtype)
            return
        acc_ref = refs[3 + has_add]
        k = pl.program_id(len(grid) - 1)

        @pl.when(k == 0)
        def _():
            acc_ref[...] = part

        @pl.when(k > 0)
        def _():
            acc_ref[...] += part

        @pl.when(k == nk - 1)
        def _():
            res = acc_ref[...]
            if has_add:
                res = add_ref[...] + res
            o_ref[...] = res.astype(o_ref.dtype)

    ins = [a, b] + ([add] if has_add else [])
    in_specs = [a_spec, b_spec] + ([add_spec] if has_add else [])
    scratch = [pltpu.VMEM(acc_shape, f32)] if nk > 1 else []
    return pl.pallas_call(body, name=name, grid=grid, in_specs=in_specs, out_specs=out_spec, out_shape=out_shape,
                          scratch_shapes=scratch, compiler_params=_params(len(grid), True))(*ins)


def _rms_matmul(name, x, g, w3, tm):
    S, K = x.shape
    G, _, Nw = w3.shape
    tm = min(tm, S)

    def body(x_ref, g_ref, w_ref, xn_ref, o_ref, xs_ref):
        @pl.when(pl.program_id(1) == 0)
        def _():
            xf = x_ref[...]
            r = lax.rsqrt(jnp.mean(xf * xf, axis=-1, keepdims=True) + EPS)
            xn = ((xf * r) * g_ref[...]).astype(bf16)
            xs_ref[...] = xn
            xn_ref[...] = xn

        o_ref[...] = jnp.dot(xs_ref[...], w_ref[...], preferred_element_type=f32)

    return pl.pallas_call(
        body, name=name, grid=(S // tm, G),
        in_specs=[pl.BlockSpec((tm, K), lambda i, j: (i, 0)), pl.BlockSpec((1, K), lambda i, j: (0, 0)),
                  pl.BlockSpec((None, K, Nw), lambda i, j: (j, 0, 0))],
        out_specs=[pl.BlockSpec((tm, K), lambda i, j: (i, 0)), pl.BlockSpec((tm, Nw), lambda i, j: (i, j))],
        out_shape=[_sds((S, K), bf16), _sds((S, G * Nw), f32)],
        scratch_shapes=[pltpu.VMEM((tm, K), bf16)], compiler_params=_params(2, True))(x, g, w3)


def _rms_matmul_swiglu(name, x, g, w3, tm):
    S, K = x.shape
    G, _, Nw = w3.shape
    tm = min(tm, S)
    half = G // 2

    def body(x_ref, g_ref, wg_ref, wu_ref, xn_ref, gu_ref, act_ref, xs_ref):
        @pl.when(pl.program_id(1) == 0)
        def _():
            xf = x_ref[...]
            r = lax.rsqrt(jnp.mean(xf * xf, axis=-1, keepdims=True) + EPS)
            xn = ((xf * r) * g_ref[...]).astype(bf16)
            xs_ref[...] = xn
            xn_ref[...] = xn

        xn = xs_ref[...]
        gate = jnp.dot(xn, wg_ref[...], preferred_element_type=f32)
        up = jnp.dot(xn, wu_ref[...], preferred_element_type=f32)
        gu_ref[0] = gate.astype(bf16)
        gu_ref[1] = up.astype(bf16)
        act_ref[...] = ((gate * _sigmoid(gate)) * up).astype(bf16)

    return pl.pallas_call(
        body, name=name, grid=(S // tm, half),
        in_specs=[pl.BlockSpec((tm, K), lambda i, j: (i, 0)), pl.BlockSpec((1, K), lambda i, j: (0, 0)),
                  pl.BlockSpec((None, K, Nw), lambda i, j: (j, 0, 0)),
                  pl.BlockSpec((None, K, Nw), lambda i, j: (half + j, 0, 0))],
        out_specs=[pl.BlockSpec((tm, K), lambda i, j: (i, 0)), pl.BlockSpec((2, tm, Nw), lambda i, j: (0, i, j)),
                   pl.BlockSpec((tm, Nw), lambda i, j: (i, j))],
        out_shape=[_sds((S, K), bf16), _sds((2, S, half * Nw), bf16), _sds((S, half * Nw), bf16)],
        scratch_shapes=[pltpu.VMEM((tm, K), bf16)], compiler_params=_params(2, True))(x, g, w3, w3)


def _mm_residual(name, a, w, res, tm):
    S, K = a.shape
    N = w.shape[1]
    tm = min(tm, S)
    return _mm(name, "nn", a, pl.BlockSpec((tm, K), lambda i: (i, 0)), w, pl.BlockSpec((K, N), lambda i: (0, 0)),
               _sds((S, N), f32), pl.BlockSpec((tm, N), lambda i: (i, 0)), (S // tm,), 1, None,
               add=res, add_spec=pl.BlockSpec((tm, N), lambda i: (i, 0)))


def _mm_nt_resident(name, a, w, tm):
    S, K = a.shape
    N = w.shape[0]
    tm = min(tm, S)
    return _mm(name, "nt", a, pl.BlockSpec((tm, K), lambda i: (i, 0)), w, pl.BlockSpec((N, K), lambda i: (0, 0)),
               _sds((S, N), f32), pl.BlockSpec((tm, N), lambda i: (i, 0)), (S // tm,), 1, None)


def _mm_nt_groups(name, a, a_spec, w3, S, tm):
    G, Dout, Kw = w3.shape
    return _mm(name, "nt", a, a_spec, w3, pl.BlockSpec((None, Dout, Kw), lambda i, g: (g, 0, 0)),
               _sds((S, Dout), f32), pl.BlockSpec((tm, Dout), lambda i, g: (i, 0)), (S // tm, G), G, (tm, Dout))


def _mm_tn(name, a, a_spec, b, b_spec, out_shape, out_spec, grid, acc_shape):
    return _mm(name, "tn", a, a_spec, b, b_spec, out_shape, out_spec, grid, grid[-1], acc_shape)


def _sigmoid(x):
    return 0.5 * jnp.tanh(0.5 * x) + 0.5


_GELU_C = math.sqrt(2.0 / math.pi)


def _gelu_and_grad(x):
    v = _GELU_C * (x + 0.044715 * (x * x * x))
    t = jnp.tanh(v)
    gl = 0.5 * x * (1.0 + t)
    dgl = 0.5 * (1.0 + t) + 0.5 * x * (1.0 - t * t) * (_GELU_C * (1.0 + 3.0 * 0.044715 * (x * x)))
    return gl, dgl


def _one_minus_exp2x(x, ex):
    y = 2.0 * x
    series = y * (1.0 + y * (0.5 + y * (1.0 / 6.0 + y * (1.0 / 24.0 + y * (1.0 / 120.0 + y * (1.0 / 720.0))))))
    return jnp.where(y > -0.125, -series, 1.0 - ex * ex)


def _merge_fwd(proj, b_gate, y_a, y_b, tm):
    S = proj.shape[0]
    tm = min(tm, S)

    def body(z0_ref, z1_ref, b0_ref, b1_ref, ya_ref, yb_ref, o_ref):
        g0 = _sigmoid(z0_ref[...] + b0_ref[...])
        g1 = _sigmoid(z1_ref[...] + b1_ref[...])
        o_ref[...] = (g0 * ya_ref[...] + g1 * yb_ref[...]).astype(bf16)

    blk = lambda off: pl.BlockSpec((tm, 256), lambda j, i: (i, off + j))
    vec = lambda off: pl.BlockSpec((1, 256), lambda j, i: (0, off + j))
    return pl.pallas_call(body, name="merge_fwd", grid=(4, S // tm),
                          in_specs=[blk(COL_Z0), blk(COL_Z1), vec(0), vec(4), blk(0), blk(0)],
                          out_specs=blk(0), out_shape=_sds((S, D), bf16),
                          compiler_params=_params(2))(proj, proj, b_gate, b_gate, y_a, y_b)


def _merge_bwd(proj, b_gate, y_a, y_b, dm, tm):
    S = proj.shape[0]
    tm = min(tm, S)

    def body(z0_ref, z1_ref, b0_ref, b1_ref, ya_ref, yb_ref, dm_ref, dz0_ref, dz1_ref, dya_ref, dyb_ref, db0_ref, db1_ref):
        g0 = _sigmoid(z0_ref[...] + b0_ref[...])
        g1 = _sigmoid(z1_ref[...] + b1_ref[...])
        d = dm_ref[...]
        dz0 = (d * ya_ref[...]) * (g0 * (1.0 - g0))
        dz1 = (d * yb_ref[...]) * (g1 * (1.0 - g1))
        dz0_ref[...] = dz0.astype(bf16)
        dz1_ref[...] = dz1.astype(bf16)
        dya_ref[...] = d * g0
        dyb_ref[...] = d * g1

        @pl.when(pl.program_id(1) == 0)
        def _():
            db0_ref[...] = jnp.zeros_like(db0_ref)
            db1_ref[...] = jnp.zeros_like(db1_ref)

        db0_ref[...] += jnp.sum(dz0, axis=0, keepdims=True)
        db1_ref[...] += jnp.sum(dz1, axis=0, keepdims=True)

    blk = lambda off: pl.BlockSpec((tm, 256), lambda j, i: (i, off + j))
    vec = lambda off: pl.BlockSpec((1, 256), lambda j, i: (0, off + j))
    return pl.pallas_call(
        body, name="merge_bwd", grid=(4, S // tm),
        in_specs=[blk(COL_Z0), blk(COL_Z1), vec(0), vec(4), blk(0), blk(0), blk(0)],
        out_specs=[blk(0), blk(0), blk(0), blk(0), vec(0), vec(0)],
        out_shape=[_sds((S, D), bf16), _sds((S, D), bf16), _sds((S, D), f32), _sds((S, D), f32),
                   _sds((1, D), f32), _sds((1, D), f32)],
        compiler_params=_params(2))(proj, proj, b_gate, b_gate, y_a, y_b, dm)


def _swiglu_bwd(dx, w, gu, tm):
    S, K = dx.shape
    tm = min(tm, S)

    def body(dx_ref, w_ref, gu_ref, o_ref):
        d = lax.dot_general(dx_ref[...].astype(bf16), w_ref[...], _DIMS["nt"], preferred_element_type=f32)
        g = gu_ref[0].astype(f32)
        u = gu_ref[1].astype(f32)
        s = _sigmoid(g)
        o_ref[0] = ((d * u) * (s * (1.0 + g * (1.0 - s)))).astype(bf16)
        o_ref[1] = (d * (g * s)).astype(bf16)

    stacked = pl.BlockSpec((2, tm, FF), lambda i: (0, i, 0))
    return pl.pallas_call(body, name="swiglu_bwd", grid=(S // tm,),
                          in_specs=[pl.BlockSpec((tm, K), lambda i: (i, 0)), pl.BlockSpec((FF, K), lambda i: (0, 0)), stacked],
                          out_specs=stacked, out_shape=_sds((2, S, FF), bf16),
                          compiler_params=_params(1, True))(dx, w, gu)


def _final_loss_bwd(x2, g3, tgt, tm):
    S = x2.shape[0]
    tm = min(tm, S)

    def body(x_ref, g_ref, t_ref, dx_ref, loss_ref, dg_ref):
        @pl.when(pl.program_id(0) == 0)
        def _():
            loss_ref[...] = jnp.zeros_like(loss_ref)
            dg_ref[...] = jnp.zeros_like(dg_ref)

        x = x_ref[...]
        g = g_ref[...]
        r = lax.rsqrt(jnp.mean(x * x, axis=-1, keepdims=True) + EPS)
        xh = x * r
        err = xh * g - t_ref[...]
        row = jnp.mean(err * err, axis=-1, keepdims=True)
        loss_ref[...] += 0.5 * jnp.sum(row, axis=0, keepdims=True)
        dy = err * (1.0 / D)
        dg_ref[...] += jnp.sum(dy * xh, axis=0, keepdims=True)
        dxh = dy * g
        dx_ref[...] = r * (dxh - xh * jnp.mean(dxh * xh, axis=-1, keepdims=True))

    row_blk = pl.BlockSpec((tm, D), lambda i: (i, 0))
    vec = pl.BlockSpec((1, D), lambda i: (0, 0))
    return pl.pallas_call(body, name="final_loss_bwd", grid=(S // tm,), in_specs=[row_blk, vec, row_blk],
                          out_specs=[row_blk, pl.BlockSpec((1, 128), lambda i: (0, 0)), vec],
                          out_shape=[_sds((S, D), f32), _sds((1, 128), f32), _sds((1, D), f32)],
                          compiler_params=_params(1))(x2, g3, tgt)


def _rms_bwd(name, x, g, dxn, dres, tm):
    S = x.shape[0]
    tm = min(tm, S)

    def body(x_ref, g_ref, d_ref, r_ref, dx_ref, dg_ref):
        @pl.when(pl.program_id(0) == 0)
        def _():
            dg_ref[...] = jnp.zeros_like(dg_ref)

        x = x_ref[...]
        d = d_ref[...]
        r = lax.rsqrt(jnp.mean(x * x, axis=-1, keepdims=True) + EPS)
        xh = x * r
        dg_ref[...] += jnp.sum(d * xh, axis=0, keepdims=True)
        dxh = d * g_ref[...]
        dx_ref[...] = r_ref[...] + r * (dxh - xh * jnp.mean(dxh * xh, axis=-1, keepdims=True))

    row_blk = pl.BlockSpec((tm, D), lambda i: (i, 0))
    vec = pl.BlockSpec((1, D), lambda i: (0, 0))
    return pl.pallas_call(body, name=name, grid=(S // tm,), in_specs=[row_blk, vec, row_blk, row_blk],
                          out_specs=[row_blk, vec], out_shape=[_sds((S, D), f32), _sds((1, D), f32)],
                          compiler_params=_params(1))(x, g, dxn, dres)


LRU_TT = 256
SCAN_UNROLL = 4


def _halo(ref, i, S):
    nt = S // LRU_TT
    t0 = pl.multiple_of(i * LRU_TT, LRU_TT)
    p0 = pl.multiple_of(jnp.maximum(t0 - 8, 0), 8)
    n0 = pl.multiple_of(jnp.minimum(t0 + LRU_TT, S - 8), 8)
    prev = jnp.where(i > 0, ref[pl.ds(p0, 8), :], 0.0)
    nxt = jnp.where(i < nt - 1, ref[pl.ds(n0, 8), :], 0.0)
    return jnp.concatenate([prev, ref[pl.ds(t0, LRU_TT), :], nxt], axis=0)


def _shift(ext, k):
    n = LRU_TT + 16
    return pltpu.roll(ext, (-k) % n, 0)[8:8 + LRU_TT]


def _lru_gates(uc, wbd, ba, bx):
    pre = jnp.dot(uc.astype(bf16), wbd, preferred_element_type=f32)
    r_f = _sigmoid(pre[:, 0:CW] + ba[0:1])
    i_f = _sigmoid(pre[:, CW:2 * CW] + bx[0:1])
    r_b = _sigmoid(pre[:, 2 * CW:3 * CW] + ba[1:2])
    i_b = _sigmoid(pre[:, 3 * CW:4 * CW] + bx[1:2])
    return r_f, i_f, r_b, i_b


def _lru_coeffs(r, sp):
    log_a = (-RGLRU_C * r) * sp
    a = jnp.exp(log_a)
    beta = jnp.sqrt(jnp.maximum(_one_minus_exp2x(log_a, a), 0.0))
    return a, beta


def _lru_coeffs_inv(r, sp):
    log_a = (-RGLRU_C * r) * sp
    a = jnp.exp(log_a)
    om = jnp.maximum(_one_minus_exp2x(log_a, a), 0.0)
    return a, jnp.sqrt(om), lax.rsqrt(om)


def _conv_tile(u_ref, i, S, cw, cb):
    ext = _halo(u_ref, i, S)
    um2, um1, u0, up1 = _shift(ext, -2), _shift(ext, -1), ext[8:8 + LRU_TT], _shift(ext, 1)
    uc = um2 * cw[0:1] + um1 * cw[1:2] + u0 * cw[2:3] + up1 * cw[3:4] + cb
    return uc, (um2, um1, u0, up1)


def _scan_pair(S, fwd_a, fwd_b, fwd_out, rev_a, rev_b, rev_out):
    ng = S // 8
    idx = lax.broadcasted_iota(jnp.int32, (8, CW), 0)

    def local(a, b, rev):
        for sh in (1, 2, 4):
            if rev:
                keep = idx < 8 - sh
                amt = 8 - sh
            else:
                keep = idx >= sh
                amt = sh
            a_s = jnp.where(keep, pltpu.roll(a, amt, 0), 1.0)
            b_s = jnp.where(keep, pltpu.roll(b, amt, 0), 0.0)
            b = a * b_s + b
            a = a * a_s
        return a, b

    def step(it, carry):
        cf, cr = carry
        fwd_rows = [pl.multiple_of((it * SCAN_UNROLL + j) * 8, 8) for j in range(SCAN_UNROLL)]
        rev_rows = [pl.multiple_of((ng - 1 - (it * SCAN_UNROLL + j)) * 8, 8) for j in range(SCAN_UNROLL)]
        fwd_loc = [local(fwd_a(r), fwd_b(r), False) for r in fwd_rows]
        rev_loc = [local(rev_a(r), rev_b(r), True) for r in rev_rows]
        for j in range(SCAN_UNROLL):
            a, b = fwd_loc[j]
            h = a * cf + b
            fwd_out[pl.ds(fwd_rows[j], 8), :] = h
            cf = jnp.broadcast_to(h[7:8, :], (8, CW))
            a, b = rev_loc[j]
            h = a * cr + b
            rev_out[pl.ds(rev_rows[j], 8), :] = h
            cr = jnp.broadcast_to(h[0:1, :], (8, CW))
        return cf, cr

    zero = jnp.zeros((8, CW), f32)
    lax.fori_loop(0, ng // SCAN_UNROLL, step, (zero, zero))


def _lru_specs(S):
    seq = lambda off: pl.BlockSpec((S, CW), lambda j: (0, off + j))
    par = lambda rows: pl.BlockSpec((rows, CW), lambda j: (0, j))
    return seq, par


def _lru_fwd(proj, conv_w, conv_b, lam, ba, bx, wbd):
    S = proj.shape[0]
    nt = S // LRU_TT

    def body(u_ref, g_ref, cw_ref, cb_ref, lam_ref, ba_ref, bx_ref, wbd_ref, y_ref, af_ref, bf_ref, ab_ref, bb_ref):
        cw, cb, ba_v, bx_v, wbd_v = cw_ref[...], cb_ref[...], ba_ref[...], bx_ref[...], wbd_ref[...]
        sp = jax.nn.softplus(-lam_ref[...])

        def phase1(i, c):
            uc, _ = _conv_tile(u_ref, i, S, cw, cb)
            r_f, i_f, r_b, i_b = _lru_gates(uc, wbd_v, ba_v, bx_v)
            rows = pl.ds(pl.multiple_of(i * LRU_TT, LRU_TT), LRU_TT)
            a, beta = _lru_coeffs(r_f, sp[0:1])
            af_ref[rows, :] = a
            bf_ref[rows, :] = beta * (i_f * uc)
            a, beta = _lru_coeffs(r_b, sp[1:2])
            ab_ref[rows, :] = a
            bb_ref[rows, :] = beta * (i_b * uc)
            return c

        lax.fori_loop(0, nt, phase1, 0)
        row8 = lambda ref: (lambda r0: ref[pl.ds(r0, 8), :])
        _scan_pair(S, row8(af_ref), row8(bf_ref), bf_ref, row8(ab_ref), row8(bb_ref), bb_ref)

        def phase3(i, c):
            rows = pl.ds(pl.multiple_of(i * LRU_TT, LRU_TT), LRU_TT)
            y_ref[rows, :] = (bf_ref[rows, :] + bb_ref[rows, :]) * jax.nn.gelu(g_ref[rows, :])
            return c

        lax.fori_loop(0, nt, phase3, 0)

    seq, par = _lru_specs(S)
    return pl.pallas_call(
        body, name="lru_fwd", grid=(NCH,),
        in_specs=[seq(0), seq(NCH), par(4), par(1), par(2), par(2), par(2),
                  pl.BlockSpec((None, CW, 4 * CW), lambda j: (j, 0, 0))],
        out_specs=seq(0), out_shape=_sds((S, D), f32),
        scratch_shapes=[pltpu.VMEM((S, CW), f32)] * 4, compiler_params=_params(1, True),
    )(proj, proj, conv_w, conv_b, lam, ba, bx, wbd)


def _lru_bwd(proj, dy, conv_w, conv_b, lam, ba, bx, wbd):
    S = proj.shape[0]
    nt = S // LRU_TT

    def body(u_ref, g_ref, dy_ref, cw_ref, cb_ref, lam_ref, ba_ref, bx_ref, wbd_ref,
             du_ref, dg_ref, dcw_ref, dcb_ref, dlam_ref, dba_ref, dbx_ref, dwbd_ref,
             af_ref, bf_ref, ab_ref, bb_ref):
        cw, cb, ba_v, bx_v, wbd_v = cw_ref[...], cb_ref[...], ba_ref[...], bx_ref[...], wbd_ref[...]
        lam_v = lam_ref[...]
        sp = jax.nn.softplus(-lam_v)

        def phase1(i, c):
            uc, _ = _conv_tile(u_ref, i, S, cw, cb)
            r_f, i_f, r_b, i_b = _lru_gates(uc, wbd_v, ba_v, bx_v)
            rows = pl.ds(pl.multiple_of(i * LRU_TT, LRU_TT), LRU_TT)
            a, beta = _lru_coeffs(r_f, sp[0:1])
            af_ref[rows, :] = a
            bf_ref[rows, :] = beta * (i_f * uc)
            a, beta = _lru_coeffs(r_b, sp[1:2])
            ab_ref[rows, :] = a
            bb_ref[rows, :] = beta * (i_b * uc)
            return c

        lax.fori_loop(0, nt, phase1, 0)
        row8 = lambda ref: (lambda r0: ref[pl.ds(r0, 8), :])
        _scan_pair(S, row8(af_ref), row8(bf_ref), bf_ref, row8(ab_ref), row8(bb_ref), bb_ref)

        def scaled_dh(a_ref):
            def f(r0):
                gl, _ = _gelu_and_grad(g_ref[pl.ds(r0, 8), :])
                return a_ref[pl.ds(r0, 8), :] * (dy_ref[pl.ds(r0, 8), :] * gl)
            return f

        _scan_pair(S, row8(ab_ref), scaled_dh(ab_ref), ab_ref, row8(af_ref), scaled_dh(af_ref), af_ref)

        dcw_ref[...] = jnp.zeros_like(dcw_ref)
        dcb_ref[...] = jnp.zeros_like(dcb_ref)
        dlam_ref[...] = jnp.zeros_like(dlam_ref)
        dba_ref[...] = jnp.zeros_like(dba_ref)
        dbx_ref[...] = jnp.zeros_like(dbx_ref)
        dwbd_ref[...] = jnp.zeros_like(dwbd_ref)

        def direction(uc, r, i_g, dht, h_nb, sp_d):
            a, beta, inv_beta = _lru_coeffs_inv(r, sp_d)
            da = dht * h_nb
            dbeta = dht * (i_g * uc)
            d_iu = dht * beta
            dlog_a = da * a - (a * a) * (dbeta * inv_beta)
            dr = dlog_a * (-RGLRU_C * sp_d)
            dsp = jnp.sum(dlog_a * (-RGLRU_C * r), axis=0, keepdims=True)
            dpre_r = dr * (r * (1.0 - r))
            dpre_i = (d_iu * uc) * (i_g * (1.0 - i_g))
            return dpre_r, dpre_i, d_iu * i_g, dsp

        def phase4(i, c):
            uc, (um2, um1, u0, up1) = _conv_tile(u_ref, i, S, cw, cb)
            r_f, i_f, r_b, i_b = _lru_gates(uc, wbd_v, ba_v, bx_v)
            rows = pl.ds(pl.multiple_of(i * LRU_TT, LRU_TT), LRU_TT)
            gl, dgl = _gelu_and_grad(g_ref[rows, :])
            dyt = dy_ref[rows, :]
            dh = dyt * gl
            dg_ref[rows, :] = ((dyt * (bf_ref[rows, :] + bb_ref[rows, :])) * dgl).astype(dg_ref.dtype)
            dht_f = dh + _shift(_halo(af_ref, i, S), 1)
            h_prev = _shift(_halo(bf_ref, i, S), -1)
            dht_b = dh + _shift(_halo(ab_ref, i, S), -1)
            h_next = _shift(_halo(bb_ref, i, S), 1)
            prf, pif, duc_f, dsp_f = direction(uc, r_f, i_f, dht_f, h_prev, sp[0:1])
            prb, pib, duc_b, dsp_b = direction(uc, r_b, i_b, dht_b, h_next, sp[1:2])
            dpre = jnp.concatenate([prf, pif, prb, pib], axis=1)
            dpre_b = dpre.astype(bf16)
            duc = (duc_f + duc_b) + lax.dot_general(dpre_b, wbd_v, _DIMS["nt"], preferred_element_type=f32)
            dwbd_ref[...] += lax.dot_general(uc.astype(bf16), dpre_b, _DIMS["tn"], preferred_element_type=f32)
            colsum = lambda v: jnp.sum(v, axis=0, keepdims=True)
            dba_ref[...] += jnp.concatenate([colsum(prf), colsum(prb)], axis=0)
            dbx_ref[...] += jnp.concatenate([colsum(pif), colsum(pib)], axis=0)
            dlam_ref[...] += jnp.concatenate([dsp_f, dsp_b], axis=0)
            dcb_ref[...] += colsum(duc)
            dcw_ref[...] += jnp.concatenate([colsum(duc * um2), colsum(duc * um1), colsum(duc * u0),
                                             colsum(duc * up1)], axis=0)
            af_ref[rows, :] = duc
            return c

        lax.fori_loop(0, nt, phase4, 0)
        dlam_ref[...] = dlam_ref[...] * (-_sigmoid(-lam_v))

        def phase5(i, c):
            ext = _halo(af_ref, i, S)
            rows = pl.ds(pl.multiple_of(i * LRU_TT, LRU_TT), LRU_TT)
            du = (_shift(ext, 2) * cw[0:1] + _shift(ext, 1) * cw[1:2] + ext[8:8 + LRU_TT] * cw[2:3]
                  + _shift(ext, -1) * cw[3:4])
            du_ref[rows, :] = du.astype(du_ref.dtype)
            return c

        lax.fori_loop(0, nt, phase5, 0)

    seq, par = _lru_specs(S)
    return pl.pallas_call(
        body, name="lru_bwd", grid=(NCH,),
        in_specs=[seq(0), seq(NCH), seq(0), par(4), par(1), par(2), par(2), par(2),
                  pl.BlockSpec((None, CW, 4 * CW), lambda j: (j, 0, 0))],
        out_specs=[seq(0), seq(0), par(4), par(1), par(2), par(2), par(2),
                   pl.BlockSpec((None, CW, 4 * CW), lambda j: (j, 0, 0))],
        out_shape=[_sds((S, D), bf16), _sds((S, D), bf16), _sds((4, D), f32), _sds((1, D), f32), _sds((2, D), f32),
                   _sds((2, D), f32), _sds((2, D), f32), _sds((NCH, CW, 4 * CW), f32)],
        scratch_shapes=[pltpu.VMEM((S, CW), f32)] * 4, compiler_params=_params(1, True),
    )(proj, proj, dy, conv_w, conv_b, lam, ba, bx, wbd)


_SLOPES = [2.0 ** (-8.0 * (h + 1) / NH) for h in range(NH)]


def _half_mask(shape, e):
    lane = lax.broadcasted_iota(jnp.int32, shape, 1)
    return (lane < HD) if e == 0 else (lane >= HD)


def _place(x, src, dst):
    if src != dst:
        x = pltpu.roll(x, HD, 1)
    return jnp.where(_half_mask(x.shape, dst), x, 0.0)


def _attn_geometry(n, S):
    tq = lax.broadcasted_iota(jnp.int32, (BLK, 3 * BLK), 0)
    sk = lax.broadcasted_iota(jnp.int32, (BLK, 3 * BLK), 1)
    dist = jnp.abs(tq + BLK - sk)
    kpos = n * BLK - BLK + sk
    valid = (dist <= BLK) & (kpos >= 0) & (kpos < S)
    return dist.astype(f32), valid


def _attn_probs(qc, km, absd, valid, slope, sink):
    s = lax.dot_general(qc, km, _DIMS["nt"], preferred_element_type=f32) * (HD ** -0.5)
    s = jnp.where(valid, s + (-slope) * absd, NEG_INF)
    m = jnp.maximum(jnp.max(s, axis=-1, keepdims=True), sink)
    p = jnp.exp(s - m)
    esink = jnp.exp(sink - m)
    den = jnp.sum(p, axis=-1, keepdims=True) + esink
    return p / den, esink / den


def _attn_specs(S):
    nb = S // BLK
    q_spec = pl.BlockSpec((BLK, D), lambda n: (n, 2))
    kv = lambda col: [pl.BlockSpec((BLK, 256), lambda n: (jnp.maximum(n - 1, 0), col)),
                      pl.BlockSpec((BLK, 256), lambda n: (n, col)),
                      pl.BlockSpec((BLK, 256), lambda n: (jnp.minimum(n + 1, nb - 1), col))]
    return nb, q_spec, kv(COL_K), kv(COL_V)


def _attn_fwd(proj, sink):
    S = proj.shape[0]
    nb, q_spec, k_specs, v_specs = _attn_specs(S)

    def body(sink_ref, q_ref, kp_ref, kc_ref, kn_ref, vp_ref, vc_ref, vn_ref, o_ref):
        absd, valid = _attn_geometry(pl.program_id(0), S)
        kcat = jnp.concatenate([kp_ref[...], kc_ref[...], kn_ref[...]], axis=0)
        vcat = jnp.concatenate([vp_ref[...], vc_ref[...], vn_ref[...]], axis=0)
        for pair in range(NH // 2):
            kvh = pair // 2
            kch = kcat[:, (kvh // 2) * 128:(kvh // 2 + 1) * 128]
            vch = vcat[:, (kvh // 2) * 128:(kvh // 2 + 1) * 128]
            qc = q_ref[:, pair * 128:(pair + 1) * 128].astype(bf16)
            acc = jnp.zeros((BLK, 128), f32)
            for e in range(2):
                h = 2 * pair + e
                km = _place(kch, kvh % 2, e).astype(bf16)
                vm = _place(vch, kvh % 2, e).astype(bf16)
                pn, _ = _attn_probs(qc, km, absd, valid, _SLOPES[h], sink_ref[0, h])
                acc = acc + jnp.dot(pn.astype(bf16), vm, preferred_element_type=f32)
            o_ref[:, pair * 128:(pair + 1) * 128] = acc

    return pl.pallas_call(
        body, name="attn_fwd", grid=(nb,),
        in_specs=[pl.BlockSpec(memory_space=pltpu.SMEM), q_spec] + k_specs + v_specs,
        out_specs=pl.BlockSpec((BLK, D), lambda n: (n, 0)), out_shape=_sds((S, D), f32),
        compiler_params=_params(1, True))(sink, proj, proj, proj, proj, proj, proj, proj)


def _attn_bwd(proj, sink, y_b, dy_b):
    S = proj.shape[0]
    nb, q_spec, k_specs, v_specs = _attn_specs(S)

    def body(sink_ref, q_ref, kp_ref, kc_ref, kn_ref, vp_ref, vc_ref, vn_ref, o_ref, do_ref,
             dq_ref, dk_ref, dv_ref, dsink_ref):
        n = pl.program_id(0)

        @pl.when(n == 0)
        def _():
            dk_ref[...] = jnp.zeros_like(dk_ref)
            dv_ref[...] = jnp.zeros_like(dv_ref)
            dsink_ref[...] = jnp.zeros_like(dsink_ref)

        absd, valid = _attn_geometry(n, S)
        kcat = jnp.concatenate([kp_ref[...], kc_ref[...], kn_ref[...]], axis=0)
        vcat = jnp.concatenate([vp_ref[...], vc_ref[...], vn_ref[...]], axis=0)
        dk_acc = [jnp.zeros((3 * BLK, 128), f32), jnp.zeros((3 * BLK, 128), f32)]
        dv_acc = [jnp.zeros((3 * BLK, 128), f32), jnp.zeros((3 * BLK, 128), f32)]
        scale = HD ** -0.5
        for pair in range(NH // 2):
            kvh = pair // 2
            ch = kvh // 2
            kch = kcat[:, ch * 128:(ch + 1) * 128]
            vch = vcat[:, ch * 128:(ch + 1) * 128]
            cols = slice(pair * 128, (pair + 1) * 128)
            qc = q_ref[:, cols].astype(bf16)
            do32 = do_ref[:, cols]
            dob = do32.astype(bf16)
            prod = do32 * o_ref[:, cols]
            dq_acc = jnp.zeros((BLK, 128), f32)
            for e in range(2):
                h = 2 * pair + e
                km = _place(kch, kvh % 2, e).astype(bf16)
                vm = _place(vch, kvh % 2, e).astype(bf16)
                pn, psink = _attn_probs(qc, km, absd, valid, _SLOPES[h], sink_ref[0, h])
                delta = jnp.sum(jnp.where(_half_mask(prod.shape, e), prod, 0.0), axis=-1, keepdims=True)
                dp = lax.dot_general(dob, vm, _DIMS["nt"], preferred_element_type=f32)
                ds = (pn * (dp - delta)).astype(bf16)
                dsink_ref[h:h + 1, :] += jnp.broadcast_to(-jnp.sum(psink * delta, axis=0, keepdims=True), (1, 128))
                dq_acc = dq_acc + jnp.dot(ds, km, preferred_element_type=f32) * scale
                ck = lax.dot_general(ds, qc, _DIMS["tn"], preferred_element_type=f32) * scale
                cv = lax.dot_general(pn.astype(bf16), dob, _DIMS["tn"], preferred_element_type=f32)
                dk_acc[ch] = dk_acc[ch] + _place(ck, e, kvh % 2)
                dv_acc[ch] = dv_acc[ch] + _place(cv, e, kvh % 2)
            dq_ref[:, cols] = dq_acc.astype(dq_ref.dtype)
        for j in range(3):
            blk = n + (j - 1)

            @pl.when((blk >= 0) & (blk < nb))
            def _():
                rows = pl.ds(pl.multiple_of(blk * BLK, BLK), BLK)
                for ch in range(2):
                    dk_ref[rows, ch * 128:(ch + 1) * 128] += dk_acc[ch][j * BLK:(j + 1) * BLK]
                    dv_ref[rows, ch * 128:(ch + 1) * 128] += dv_acc[ch][j * BLK:(j + 1) * BLK]

    row_blk = pl.BlockSpec((BLK, D), lambda n: (n, 0))
    full = pl.BlockSpec((S, 256), lambda n: (0, 0))
    return pl.pallas_call(
        body, name="attn_bwd", grid=(nb,),
        in_specs=[pl.BlockSpec(memory_space=pltpu.SMEM), q_spec] + k_specs + v_specs + [row_blk, row_blk],
        out_specs=[row_blk, full, full, pl.BlockSpec((NH, 128), lambda n: (0, 0))],
        out_shape=[_sds((S, D), bf16), _sds((S, 256), f32), _sds((S, 256), f32), _sds((NH, 128), f32)],
        compiler_params=_params(1, True))(sink, proj, proj, proj, proj, proj, proj, proj, y_b, dy_b)


def _adamw(name, w, g, m, v, tr):
    R, C = w.shape
    tr = min(tr, R)

    def body(w_ref, g_ref, m_ref, v_ref, d_ref, m2_ref, v2_ref):
        g = g_ref[...]
        m2 = ADAM_B1 * m_ref[...] + (1.0 - ADAM_B1) * g
        v2 = ADAM_B2 * v_ref[...] + (1.0 - ADAM_B2) * (g * g)
        m_hat = m2 / (1.0 - ADAM_B1 ** ADAM_STEP)
        v_hat = v2 / (1.0 - ADAM_B2 ** ADAM_STEP)
        d_ref[...] = -ADAM_LR * (m_hat / (jnp.sqrt(v_hat) + ADAM_EPS) + ADAM_WD * w_ref[...])
        m2_ref[...] = m2
        v2_ref[...] = v2

    blk = pl.BlockSpec((tr, C), lambda i: (i, 0))
    return pl.pallas_call(body, name=name, grid=(R // tr,), in_specs=[blk] * 4, out_specs=[blk] * 3,
                          out_shape=[_sds((R, C), f32)] * 3, compiler_params=_params(1))(w, g, m, v)


def _pair_sum(name, c_arr, g4, recv, th):
    _, _, h, w = g4.shape
    th = min(th, h)

    def body(c_ref, g_ref, r_ref, o_ref, ob_ref):
        p = g_ref[...] + r_ref[...]
        o_ref[...] = p
        ob_ref[...] = p.astype(bf16)

    blk = pl.BlockSpec((None, th, w), lambda s, i, c_ref: (s, i, 0))
    spec = pltpu.PrefetchScalarGridSpec(
        num_scalar_prefetch=1, grid=(NCHIP, h // th),
        in_specs=[pl.BlockSpec((None, None, th, w), lambda s, i, c_ref: (s, c_ref[0], i, 0)), blk],
        out_specs=[blk, blk])
    return pl.pallas_call(body, name=name, grid_spec=spec,
                          out_shape=[_sds((NCHIP, h, w), f32), _sds((NCHIP, h, w), bf16)],
                          compiler_params=_params(2))(c_arr, g4, recv)


def _chip_sum(name, chip_arr, own4, recv3, th):
    _, h, w = own4.shape
    th = min(th, h)

    def body(s_ref, o_ref, r_ref, out_ref):
        out_ref[...] = ((o_ref[...] + r_ref[0].astype(f32)) + r_ref[1].astype(f32)) + r_ref[2].astype(f32)

    spec = pltpu.PrefetchScalarGridSpec(
        num_scalar_prefetch=1, grid=(h // th,),
        in_specs=[pl.BlockSpec((None, th, w), lambda i, s_ref: (s_ref[0], i, 0)),
                  pl.BlockSpec((3, th, w), lambda i, s_ref: (0, i, 0))],
        out_specs=pl.BlockSpec((th, w), lambda i, s_ref: (i, 0)))
    return pl.pallas_call(body, name=name, grid_spec=spec, out_shape=_sds((h, w), f32),
                          compiler_params=_params(1, True))(chip_arr, own4, recv3)


def _adamw_halves(name, c_arr, w, g_own, g_recv, m, v, th):
    h, wd = g_own.shape
    th = min(th, h)

    def body(c_ref, w_ref, go_ref, gr_ref, m_ref, v_ref, g_ref, d_ref, m2_ref, v2_ref):
        g = jnp.where(c_ref[0] == pl.program_id(0), go_ref[...], gr_ref[...])
        m2 = ADAM_B1 * m_ref[...] + (1.0 - ADAM_B1) * g
        v2 = ADAM_B2 * v_ref[...] + (1.0 - ADAM_B2) * (g * g)
        m_hat = m2 / (1.0 - ADAM_B1 ** ADAM_STEP)
        v_hat = v2 / (1.0 - ADAM_B2 ** ADAM_STEP)
        g_ref[...] = g
        d_ref[...] = -ADAM_LR * (m_hat / (jnp.sqrt(v_hat) + ADAM_EPS) + ADAM_WD * w_ref[...])
        m2_ref[...] = m2
        v2_ref[...] = v2

    nt = h // th
    full = pl.BlockSpec((th, wd), lambda hh, i, c_ref: (hh * nt + i, 0))
    half = pl.BlockSpec((th, wd), lambda hh, i, c_ref: (i, 0))
    spec = pltpu.PrefetchScalarGridSpec(num_scalar_prefetch=1, grid=(2, nt),
                                        in_specs=[full, half, half, full, full], out_specs=[full] * 4)
    return pl.pallas_call(body, name=name, grid_spec=spec, out_shape=[_sds((2 * h, wd), f32)] * 4,
                          compiler_params=_params(2))(c_arr, w, g_own, g_recv, m, v)


def _add2(name, a, b):
    def body(a_ref, b_ref, o_ref):
        o_ref[...] = a_ref[...] + b_ref[...]
    return pl.pallas_call(body, name=name, out_shape=_sds(a.shape, f32))(a, b)


def _sum4(name, b4, th):
    _, h, w = b4.shape
    th = min(th, h)

    def body(b_ref, o_ref):
        o_ref[...] = ((b_ref[0] + b_ref[1]) + b_ref[2]) + b_ref[3]

    return pl.pallas_call(body, name=name, grid=(h // th,),
                          in_specs=[pl.BlockSpec((NCHIP, th, w), lambda i: (0, i, 0))],
                          out_specs=pl.BlockSpec((th, w), lambda i: (i, 0)), out_shape=_sds((h, w), f32),
                          compiler_params=_params(1, True))(b4)


def _coords():
    x, y, c = lax.axis_index("x"), lax.axis_index("y"), lax.axis_index("c")
    return x, y, c, [(1 - x, y), (x, 1 - y), (1 - x, 1 - y)]


def _gather_chips(arrs):
    n = len(arrs)

    def body(*refs):
        ins, outs = refs[:n], refs[n:2 * n]
        send_sems, recv_sems, local_sems = refs[2 * n:2 * n + 3]
        stage = refs[2 * n + 3:]
        x, y, c, chips = _coords()
        s = 2 * x + y
        sib = (x, y, 1 - c)
        load = [pltpu.make_async_copy(ins[a], stage[a], local_sems.at[a]) for a in range(n)]
        local = [pltpu.make_async_copy(stage[a], outs[a].at[s], local_sems.at[n + a]) for a in range(n)]
        for cp in load:
            cp.start()

        def over_ici(k, a, slot, peer):
            return pltpu.make_async_remote_copy(src_ref=ins[a].at[c], dst_ref=outs[a].at[slot, c], send_sem=send_sems.at[k * n + a],
                                                recv_sem=recv_sems.at[k * n + a], device_id=peer, device_id_type=MESH)

        def to_sibling(k, a, slot, half):
            i = (3 + k) * n + a
            return pltpu.make_async_remote_copy(src_ref=outs[a].at[slot, half], dst_ref=outs[a].at[slot, half], send_sem=send_sems.at[i],
                                                recv_sem=recv_sems.at[i], device_id=sib, device_id_type=MESH)

        sends = [over_ici(k, a, s, (px, py, c)) for k, (px, py) in enumerate(chips) for a in range(n)]
        for cp in sends:
            cp.start()
        for a in range(n):
            load[a].wait()
            local[a].start()
        passed = []
        for k, (px, py) in enumerate(chips):
            for a in range(n):
                over_ici(k, a, 2 * px + py, (px, py, c)).wait_recv()
                cp = to_sibling(k, a, 2 * px + py, c)
                cp.start()
                passed.append(cp)
        for k, (px, py) in enumerate(chips):
            for a in range(n):
                to_sibling(k, a, 2 * px + py, 1 - c).wait_recv()
        for cp in sends + passed:
            cp.wait_send()
        for cp in local:
            cp.wait()

    return pl.pallas_call(
        body, name="gather_weights", in_specs=[ANY] * n, out_specs=[ANY] * n,
        out_shape=[_sds((NCHIP,) + a.shape, a.dtype) for a in arrs],
        scratch_shapes=[pltpu.SemaphoreType.DMA((6 * n,)), pltpu.SemaphoreType.DMA((6 * n,)), pltpu.SemaphoreType.DMA((2 * n,))]
        + [pltpu.VMEM(a.shape, a.dtype) for a in arrs],
        compiler_params=pltpu.CompilerParams(vmem_limit_bytes=VMEM_LIMIT),
    )(*arrs)


def _sibling_halves(g4s, small):
    n = len(g4s)

    def body(*refs):
        ins, small_ref = refs[:n], refs[n]
        outs, small_out = refs[n + 1:2 * n + 1], refs[2 * n + 1]
        send_sems, recv_sems = refs[2 * n + 2:]
        x, y, c, _ = _coords()
        sib = (x, y, 1 - c)

        def remote(a, half):
            src = small_ref if a == n else ins[a].at[:, half]
            dst = small_out if a == n else outs[a]
            return pltpu.make_async_remote_copy(src_ref=src, dst_ref=dst, send_sem=send_sems.at[a], recv_sem=recv_sems.at[a],
                                                device_id=sib, device_id_type=MESH)

        sends = [remote(a, 1 - c) for a in range(n + 1)]
        for cp in sends:
            cp.start()
        for a in range(n + 1):
            remote(a, c).wait_recv()
        for cp in sends:
            cp.wait_send()

    return pl.pallas_call(
        body, name="reduce_sibling", in_specs=[ANY] * (n + 1), out_specs=[ANY] * (n + 1),
        out_shape=[_sds((g.shape[0],) + g.shape[2:], f32) for g in g4s] + [_sds(small.shape, f32)],
        scratch_shapes=[pltpu.SemaphoreType.DMA((n + 1,)), pltpu.SemaphoreType.DMA((n + 1,))],
    )(*g4s, small)


def _exchange_chips(parts, small2):
    n = len(parts)

    def body(*refs):
        ins, small_ref = refs[:n], refs[n]
        outs, small_out = refs[n + 1:2 * n + 1], refs[2 * n + 1]
        send_sems, recv_sems, local_sem = refs[2 * n + 2:]
        x, y, c, chips = _coords()
        s = 2 * x + y
        local = pltpu.make_async_copy(small_ref.at[c], small_out.at[s], local_sem)
        local.start()

        def remote(k, a, dest_chip, small_slot, peer):
            if a == n:
                src, dst = small_ref.at[c], small_out.at[small_slot]
            else:
                src, dst = ins[a].at[dest_chip], outs[a].at[k]
            i = k * (n + 1) + a
            return pltpu.make_async_remote_copy(src_ref=src, dst_ref=dst, send_sem=send_sems.at[i], recv_sem=recv_sems.at[i],
                                                device_id=peer, device_id_type=MESH)

        sends = [remote(k, a, 2 * px + py, s, (px, py, c)) for k, (px, py) in enumerate(chips) for a in range(n + 1)]
        for cp in sends:
            cp.start()
        for k, (px, py) in enumerate(chips):
            for a in range(n + 1):
                remote(k, a, s, 2 * px + py, (px, py, c)).wait_recv()
        for cp in sends:
            cp.wait_send()
        local.wait()

    m = 3 * (n + 1)
    return pl.pallas_call(
        body, name="reduce_chips", in_specs=[ANY] * (n + 1), out_specs=[ANY] * (n + 1),
        out_shape=[_sds((3,) + p.shape[1:], p.dtype) for p in parts] + [_sds((NCHIP,) + small2.shape[1:], f32)],
        scratch_shapes=[pltpu.SemaphoreType.DMA((m,)), pltpu.SemaphoreType.DMA((m,)), pltpu.SemaphoreType.DMA],
    )(*parts, small2)


def _share_sibling(halves):
    n = len(halves)

    def body(*refs):
        ins, outs = refs[:n], refs[n:2 * n]
        send_sems, recv_sems = refs[2 * n:]
        x, y, c, _ = _coords()
        sib = (x, y, 1 - c)
        sends = [pltpu.make_async_remote_copy(src_ref=ins[a], dst_ref=outs[a], send_sem=send_sems.at[a], recv_sem=recv_sems.at[a],
                                              device_id=sib, device_id_type=MESH) for a in range(n)]
        for cp in sends:
            cp.start()
        for cp in sends:
            cp.wait()

    return pl.pallas_call(
        body, name="reduce_share", in_specs=[ANY] * n, out_specs=[ANY] * n,
        out_shape=[_sds(h.shape, f32) for h in halves],
        scratch_shapes=[pltpu.SemaphoreType.DMA((n,)), pltpu.SemaphoreType.DMA((n,))],
    )(*halves)


def _block_diag_pairs(w):
    w = w.reshape(NCH, 2, HD, HD)
    z = jnp.zeros((NCH, HD, HD), w.dtype)
    return jnp.concatenate([jnp.concatenate([w[:, 0], z], axis=2), jnp.concatenate([z, w[:, 1]], axis=2)], axis=1)


def _diag_blocks(m):
    return jnp.stack([m[:, :HD, :HD], m[:, HD:, HD:]], axis=1).reshape(NH, HD, HD)


def _pack(vs, rows):
    flat = jnp.concatenate([v.reshape(-1) for v in vs])
    return jnp.pad(flat, (0, rows * 128 - flat.shape[0])).reshape(rows, 128)


def _unpack(packed, shapes):
    flat = packed.reshape(-1)
    out, off = [], 0
    for shp in shapes:
        size = math.prod(shp)
        out.append(flat[off:off + size].reshape(shp))
        off += size
    return out


def _rows_for(sizes, multiple):
    rows = -(-sum(sizes) // 128)
    return -(-rows // multiple) * multiple


def kernel(x, norm_mix_g, w_in, b_gate, conv_w, conv_b, lru_lambda, lru_wa, lru_ba, lru_wx, lru_bx, attn_sink, w_out, norm_ffn_g, w_ffn_in, w_ffn_out, norm_final_g, loss_target, m_norm_mix_g, m_w_in, m_b_gate, m_conv_w, m_conv_b, m_lru_lambda, m_lru_wa, m_lru_ba, m_lru_wx, m_lru_bx, m_attn_sink, m_w_out, m_norm_ffn_g, m_w_ffn_in, m_w_ffn_out, m_norm_final_g, v_norm_mix_g, v_w_in, v_b_gate, v_conv_w, v_conv_b, v_lru_lambda, v_lru_wa, v_lru_ba, v_lru_wx, v_lru_bx, v_attn_sink, v_w_out, v_norm_ffn_g, v_w_ffn_in, v_w_ffn_out, v_norm_final_g):
    S = x.shape[1]
    xs = x[0]
    tgt = loss_target[0]
    cx, cy, cc = lax.axis_index("x"), lax.axis_index("y"), lax.axis_index("c")
    chip = 2 * cx + cy
    SW = D // NCHIP

    small_shard = _pack([conv_w[0], lru_lambda[0], lru_ba[0], lru_bx[0]], 32)
    halves_of = lambda a: a.reshape(2, a.shape[0] // 2, a.shape[1])
    w_in_g, w_ffn_in_g, w_out_g, w_ffn_out_g, small_g = _gather_chips(
        [halves_of(w_in[0].astype(bf16)), halves_of(w_ffn_in[0].astype(bf16)), halves_of(w_out[0].astype(bf16)),
         halves_of(w_ffn_out[0].astype(bf16)), halves_of(small_shard)])
    w_in_g, w_ffn_in_g = w_in_g.reshape(NCHIP, D, SHW), w_ffn_in_g.reshape(NCHIP, D, SHW)
    w_out_f = w_out_g.reshape(D, D)
    w_ffn_out_f = w_ffn_out_g.reshape(FF, D)
    small_g = small_g.reshape(NCHIP, 32, 128)
    small_parts = [_unpack(small_g[s], [(4, SW), (2, SW), (2, SW), (2, SW)]) for s in range(NCHIP)]
    conv_w_f, lam_f, ba_f, bx_f = [jnp.concatenate([small_parts[s][p] for s in range(NCHIP)], axis=1) for p in range(4)]
    wbd = jnp.concatenate([_block_diag_pairs(lru_wa[0, 0]), _block_diag_pairs(lru_wx[0, 0]),
                           _block_diag_pairs(lru_wa[0, 1]), _block_diag_pairs(lru_wx[0, 1])], axis=2).astype(bf16)
    conv_b_f = conv_b
    sink = attn_sink

    xn, proj = _rms_matmul("rms_proj", xs, norm_mix_g, w_in_g, 1024)
    y_a = _lru_fwd(proj, conv_w_f, conv_b_f, lam_f, ba_f, bx_f, wbd)
    y_b = _attn_fwd(proj, sink)
    merged = _merge_fwd(proj, b_gate, y_a, y_b, 512)
    x1 = _mm_residual("out_proj", merged, w_out_f, xs, 512)
    xn2, gu, act = _rms_matmul_swiglu("rms_ffn_in", x1, norm_ffn_g, w_ffn_in_g, 1024)
    x2 = _mm_residual("ffn_out", act, w_ffn_out_f, x1, 512)
    dx2, loss_row, dg3 = _final_loss_bwd(x2, norm_final_g.reshape(1, D), tgt, 256)
    loss = lax.psum(loss_row[0, 0], ("x", "y", "c"))

    tm = min(512, S)
    tk = min(512, S)
    gw_ffn_out = _mm_tn("dw_ffn_out", act, pl.BlockSpec((tk, SHW), lambda i, k: (k, i)),
                        dx2, pl.BlockSpec((tk, D), lambda i, k: (k, 0)),
                        _sds((FF, D), f32), pl.BlockSpec((SHW, D), lambda i, k: (i, 0)), (2, S // tk), (SHW, D))
    dgu = _swiglu_bwd(dx2, w_ffn_out_f, gu, 256)
    dxn2 = _mm_nt_groups("dxn2", dgu, pl.BlockSpec((None, tm, SHW), lambda i, g: (g // 2, i, g % 2)), w_ffn_in_g, S, tm)
    gw_ffn_in = _mm_tn("dw_ffn_in", xn2, pl.BlockSpec((tk, D), lambda g, k: (k, 0)),
                       dgu, pl.BlockSpec((None, tk, SHW), lambda g, k: (g // 2, k, g % 2)),
                       _sds((NCHIP, D, SHW), f32), pl.BlockSpec((None, D, SHW), lambda g, k: (g, 0, 0)),
                       (NCHIP, S // tk), (D, SHW))
    dx1, dg2 = _rms_bwd("rms_ffn_bwd", x1, norm_ffn_g, dxn2, dx2, 256)

    dmerged = _mm_nt_resident("d_merged", dx1, w_out_f, 512)
    gw_out = _mm_tn("dw_out", merged, pl.BlockSpec((tk, D), lambda i, k: (k, 0)),
                    dx1, pl.BlockSpec((tk, D), lambda i, k: (k, 0)),
                    _sds((D, D), f32), pl.BlockSpec((D, D), lambda i, k: (0, 0)), (1, S // tk), (D, D))
    dz0, dz1, dy_a, dy_b, db0, db1 = _merge_bwd(proj, b_gate, y_a, y_b, dmerged, 512)
    du, dgl, dcw, dcb, dlam, dba, dbx, dwbd = _lru_bwd(proj, dy_a, conv_w_f, conv_b_f, lam_f, ba_f, bx_f, wbd)
    dq, dk, dv, dsink = _attn_bwd(proj, sink, y_b, dy_b)
    dproj = jnp.concatenate([du, dgl, dq, dk.astype(bf16), dv.astype(bf16), dz0, dz1], axis=1)
    dxn = _mm_nt_groups("dxn", dproj, pl.BlockSpec((tm, SHW), lambda i, g: (i, g)), w_in_g, S, tm)
    gw_in = _mm_tn("dw_in", xn, pl.BlockSpec((tk, D), lambda g, k: (k, 0)),
                   dproj, pl.BlockSpec((tk, SHW), lambda g, k: (k, g)),
                   _sds((NCHIP, D, SHW), f32), pl.BlockSpec((None, D, SHW), lambda g, k: (g, 0, 0)),
                   (NCHIP, S // tk), (D, SHW))
    grad_x, dg1 = _rms_bwd("rms_mix_bwd", xs, norm_mix_g, dxn, dx1, 256)

    d_wa = jnp.stack([_diag_blocks(dwbd[:, :, 0:CW]), _diag_blocks(dwbd[:, :, 2 * CW:3 * CW])])
    d_wx = jnp.stack([_diag_blocks(dwbd[:, :, CW:2 * CW]), _diag_blocks(dwbd[:, :, 3 * CW:4 * CW])])
    small_full = [dg1, jnp.concatenate([db0, db1], axis=1), dcw, dcb, dlam, d_wa, dba, d_wx, dbx, dsink[:, 0], dg2, dg3]
    full_shapes = [(1, D), (1, 2 * D), (4, D), (1, D), (2, D), (2, NH, HD, HD), (2, D), (2, NH, HD, HD), (2, D), (NH,),
                   (1, D), (1, D)]
    rows_full = _rows_for([math.prod(s) for s in full_shapes], 16)
    small_vec = _pack(small_full, rows_full)

    big = [gw_in.reshape(NCHIP, 2, D // 2, SHW), gw_ffn_in.reshape(NCHIP, 2, D // 2, SHW),
           gw_out.reshape(NCHIP, 2, D // NCHIP // 2, D), gw_ffn_out.reshape(NCHIP, 2, FF // NCHIP // 2, D)]
    *recv_a, small_sib = _sibling_halves(big, small_vec)
    c_arr = cc.reshape(1).astype(jnp.int32)
    tiles = [256, 256, 128, 352]
    names = ["w_in", "w_ffn_in", "w_out", "w_ffn_out"]
    pairs = [_pair_sum("pair_sum_" + nm, c_arr, g4, r, th) for nm, g4, r, th in zip(names, big, recv_a, tiles)]
    small_chip = _add2("pair_sum_small", small_vec, small_sib).reshape(2, rows_full // 2, 128)
    *recv_b, small_all = _exchange_chips([p[1] for p in pairs], small_chip)
    chip_arr = chip.reshape(1).astype(jnp.int32)
    halves = [_chip_sum("chip_sum_" + nm, chip_arr, p[0], r3, th) for nm, p, r3, th in zip(names, pairs, recv_b, tiles)]
    halves.append(_sum4("chip_sum_small", small_all, rows_full // 2))
    *recv_c, small_other = _share_sibling(halves)
    small_lo = jnp.where(cc == 0, halves[4], small_other)
    small_hi = jnp.where(cc == 0, small_other, halves[4])
    g_full = _unpack(jnp.concatenate([small_lo, small_hi], axis=0), full_shapes)

    out_big = {}
    for nm, w, g_own, g_recv, m, v, th in zip(names, [w_in, w_ffn_in, w_out, w_ffn_out], halves[:4], recv_c,
                                              [m_w_in, m_w_ffn_in, m_w_out, m_w_ffn_out],
                                              [v_w_in, v_w_ffn_in, v_w_out, v_w_ffn_out], tiles):
        g_, d_, m_, v_ = _adamw_halves("adamw_" + nm, c_arr, w[0], g_own, g_recv, m[0], v[0], th)
        out_big[nm] = (g_[None], d_[None], m_[None], v_[None])

    small_names = ["norm_mix_g", "b_gate", "conv_w", "conv_b", "lru_lambda", "lru_wa", "lru_ba", "lru_wx", "lru_bx", "attn_sink",
                   "norm_ffn_g", "norm_final_g"]
    sharded = {"conv_w", "lru_lambda", "lru_ba", "lru_bx"}
    small_w = [norm_mix_g, b_gate, conv_w, conv_b, lru_lambda, lru_wa, lru_ba, lru_wx, lru_bx, attn_sink, norm_ffn_g, norm_final_g]
    small_m = [m_norm_mix_g, m_b_gate, m_conv_w, m_conv_b, m_lru_lambda, m_lru_wa, m_lru_ba, m_lru_wx, m_lru_bx, m_attn_sink,
               m_norm_ffn_g, m_norm_final_g]
    small_v = [v_norm_mix_g, v_b_gate, v_conv_w, v_conv_b, v_lru_lambda, v_lru_wa, v_lru_ba, v_lru_wx, v_lru_bx, v_attn_sink,
               v_norm_ffn_g, v_norm_final_g]
    g_local = []
    for nm, g, w in zip(small_names, g_full, small_w):
        if nm in sharded:
            g = lax.dynamic_slice_in_dim(g, chip * SW, SW, axis=1)
        g_local.append(g.reshape(w.shape))
    local_shapes = [w.shape for w in small_w]
    rows_local = _rows_for([math.prod(s) for s in local_shapes], 8)
    d_s, m_s, v_s = _adamw("adamw_small", _pack(small_w, rows_local), _pack(g_local, rows_local),
                           _pack(small_m, rows_local), _pack(small_v, rows_local), rows_local)
    d_l, m_l, v_l = _unpack(d_s, local_shapes), _unpack(m_s, local_shapes), _unpack(v_s, local_shapes)
    res = {nm: (g_local[i], d_l[i], m_l[i], v_l[i]) for i, nm in enumerate(small_names)}
    res.update(out_big)

    order = ["norm_mix_g", "w_in", "b_gate", "conv_w", "conv_b", "lru_lambda", "lru_wa", "lru_ba", "lru_wx", "lru_bx", "attn_sink",
             "w_out", "norm_ffn_g", "w_ffn_in", "w_ffn_out", "norm_final_g"]
    outs = [loss, grad_x[None]]
    for k in range(4):
        outs += [res[nm][k] for nm in order]
    return tuple(outs)
```

```python
import functools
import math

import jax
import jax.numpy as jnp
from jax import lax
from jax.experimental import pallas as pl
from jax.experimental.pallas import tpu as pltpu

f32 = jnp.float32
bf16 = jnp.bfloat16

D = 1024
NH = 16
HD = 64
FF = 2816
INW = 5632
NCHIP = 4
SHW = INW // NCHIP
CW = 128
NCH = D // CW
BLK = 128
EPS = 1e-6
NEG_INF = -1e30
RGLRU_C = 8.0
ADAM_LR, ADAM_B1, ADAM_B2, ADAM_EPS, ADAM_WD, ADAM_STEP = 0.001, 0.9, 0.999, 1e-08, 0.01, 10
VMEM_LIMIT = 58 * 1024 * 1024
MESH = pl.DeviceIdType.MESH
ANY = pl.BlockSpec(memory_space=pl.ANY)

COL_U, COL_G, COL_Q, COL_K, COL_V, COL_Z0, COL_Z1 = 0, 4, 8, 12, 13, 14, 18


def _params(n_axes, vmem=False):
    return pltpu.CompilerParams(dimension_semantics=("arbitrary",) * n_axes,
                                vmem_limit_bytes=VMEM_LIMIT if vmem else None)


def _sds(shape, dtype):
    return jax.ShapeDtypeStruct(tuple(shape), dtype)


_DIMS = {"nn": (((1,), (0,)), ((), ())), "nt": (((1,), (1,)), ((), ())), "tn": (((0,), (0,)), ((), ()))}


def _mm(name, mode, a, a_spec, b, b_spec, out_shape, out_spec, grid, nk, acc_shape, add=None, add_spec=None):
    has_add = add is not None

    def body(*refs):
        a_ref, b_ref = refs[0], refs[1]
        add_ref = refs[2] if has_add else None
        o_ref = refs[2 + has_add]
        part = lax.dot_general(a_ref[...].astype(bf16), b_ref[...].astype(bf16), _DIMS[mode],
                               preferred_element_type=f32)
        if nk == 1:
            if has_add:
                part = add_ref[...] + part
            o_ref[...] = part.astype(o_ref.dtype)
            return
        acc_ref = refs[3 + has_add]
        k = pl.program_id(len(grid) - 1)

        @pl.when(k == 0)
        def _():
            acc_ref[...] = part

        @pl.when(k > 0)
        def _():
            acc_ref[...] += part

        @pl.when(k == nk - 1)
        def _():
            res = acc_ref[...]
            if has_add:
                res = add_ref[...] + res
            o_ref[...] = res.astype(o_ref.dtype)

    ins = [a, b] + ([add] if has_add else [])
    in_specs = [a_spec, b_spec] + ([add_spec] if has_add else [])
    scratch = [pltpu.VMEM(acc_shape, f32)] if nk > 1 else []
    return pl.pallas_call(body, name=name, grid=grid, in_specs=in_specs, out_specs=out_spec, out_shape=out_shape,
                          scratch_shapes=scratch, compiler_params=_params(len(grid), True))(*ins)


def _rms_matmul(name, x, g, w3, tm):
    S, K = x.shape
    G, _, Nw = w3.shape
    tm = min(tm, S)

    def body(x_ref, g_ref, w_ref, xn_ref, o_ref, xs_ref):
        @pl.when(pl.program_id(1) == 0)
        def _():
            xf = x_ref[...]
            r = lax.rsqrt(jnp.mean(xf * xf, axis=-1, keepdims=True) + EPS)
            xn = ((xf * r) * g_ref[...]).astype(bf16)
            xs_ref[...] = xn
            xn_ref[...] = xn

        o_ref[...] = jnp.dot(xs_ref[...], w_ref[...], preferred_element_type=f32)

    return pl.pallas_call(
        body, name=name, grid=(S // tm, G),
        in_specs=[pl.BlockSpec((tm, K), lambda i, j: (i, 0)), pl.BlockSpec((1, K), lambda i, j: (0, 0)),
                  pl.BlockSpec((None, K, Nw), lambda i, j: (j, 0, 0))],
        out_specs=[pl.BlockSpec((tm, K), lambda i, j: (i, 0)), pl.BlockSpec((tm, Nw), lambda i, j: (i, j))],
        out_shape=[_sds((S, K), bf16), _sds((S, G * Nw), f32)],
        scratch_shapes=[pltpu.VMEM((tm, K), bf16)], compiler_params=_params(2, True))(x, g, w3)


def _rms_matmul_swiglu(name, x, g, w3, tm):
    S, K = x.shape
    G, _, Nw = w3.shape
    tm = min(tm, S)
    half = G // 2

    def body(x_ref, g_ref, wg_ref, wu_ref, xn_ref, gu_ref, act_ref, xs_ref):
        @pl.when(pl.program_id(1) == 0)
        def _():
            xf = x_ref[...]
            r = lax.rsqrt(jnp.mean(xf * xf, axis=-1, keepdims=True) + EPS)
            xn = ((xf * r) * g_ref[...]).astype(bf16)
            xs_ref[...] = xn
            xn_ref[...] = xn

        xn = xs_ref[...]
        gate = jnp.dot(xn, wg_ref[...], preferred_element_type=f32)
        up = jnp.dot(xn, wu_ref[...], preferred_element_type=f32)
        gu_ref[0] = gate.astype(bf16)
        gu_ref[1] = up.astype(bf16)
        act_ref[...] = ((gate * _sigmoid(gate)) * up).astype(bf16)

    return pl.pallas_call(
        body, name=name, grid=(S // tm, half),
        in_specs=[pl.BlockSpec((tm, K), lambda i, j: (i, 0)), pl.BlockSpec((1, K), lambda i, j: (0, 0)),
                  pl.BlockSpec((None, K, Nw), lambda i, j: (j, 0, 0)),
                  pl.BlockSpec((None, K, Nw), lambda i, j: (half + j, 0, 0))],
        out_specs=[pl.BlockSpec((tm, K), lambda i, j: (i, 0)), pl.BlockSpec((2, tm, Nw), lambda i, j: (0, i, j)),
                   pl.BlockSpec((tm, Nw), lambda i, j: (i, j))],
        out_shape=[_sds((S, K), bf16), _sds((2, S, half * Nw), bf16), _sds((S, half * Nw), bf16)],
        scratch_shapes=[pltpu.VMEM((tm, K), bf16)], compiler_params=_params(2, True))(x, g, w3, w3)


def _mm_residual(name, a, w, res, tm):
    S, K = a.shape
    N = w.shape[1]
    tm = min(tm, S)
    return _mm(name, "nn", a, pl.BlockSpec((tm, K), lambda i: (i, 0)), w, pl.BlockSpec((K, N), lambda i: (0, 0)),
               _sds((S, N), f32), pl.BlockSpec((tm, N), lambda i: (i, 0)), (S // tm,), 1, None,
               add=res, add_spec=pl.BlockSpec((tm, N), lambda i: (i, 0)))


def _mm_nt_resident(name, a, w, tm):
    S, K = a.shape
    N = w.shape[0]
    tm = min(tm, S)
    return _mm(name, "nt", a, pl.BlockSpec((tm, K), lambda i: (i, 0)), w, pl.BlockSpec((N, K), lambda i: (0, 0)),
               _sds((S, N), f32), pl.BlockSpec((tm, N), lambda i: (i, 0)), (S // tm,), 1, None)


def _mm_nt_groups(name, a, a_spec, w3, S, tm):
    G, Dout, Kw = w3.shape
    return _mm(name, "nt", a, a_spec, w3, pl.BlockSpec((None, Dout, Kw), lambda i, g: (g, 0, 0)),
               _sds((S, Dout), f32), pl.BlockSpec((tm, Dout), lambda i, g: (i, 0)), (S // tm, G), G, (tm, Dout))


def _mm_tn(name, a, a_spec, b, b_spec, out_shape, out_spec, grid, acc_shape):
    return _mm(name, "tn", a, a_spec, b, b_spec, out_shape, out_spec, grid, grid[-1], acc_shape)


def _sigmoid(x):
    return 0.5 * jnp.tanh(0.5 * x) + 0.5


_GELU_C = math.sqrt(2.0 / math.pi)


def _gelu_and_grad(x):
    v = _GELU_C * (x + 0.044715 * (x * x * x))
    t = jnp.tanh(v)
    gl = 0.5 * x * (1.0 + t)
    dgl = 0.5 * (1.0 + t) + 0.5 * x * (1.0 - t * t) * (_GELU_C * (1.0 + 3.0 * 0.044715 * (x * x)))
    return gl, dgl


def _one_minus_exp2x(x, ex):
    y = 2.0 * x
    series = y * (1.0 + y * (0.5 + y * (1.0 / 6.0 + y * (1.0 / 24.0 + y * (1.0 / 120.0 + y * (1.0 / 720.0))))))
    return jnp.where(y > -0.125, -series, 1.0 - ex * ex)


def _merge_fwd(proj, b_gate, y_a, y_b, tm):
    S = proj.shape[0]
    tm = min(tm, S)

    def body(z0_ref, z1_ref, b0_ref, b1_ref, ya_ref, yb_ref, o_ref):
        g0 = _sigmoid(z0_ref[...] + b0_ref[...])
        g1 = _sigmoid(z1_ref[...] + b1_ref[...])
        o_ref[...] = (g0 * ya_ref[...] + g1 * yb_ref[...]).astype(bf16)

    blk = lambda off: pl.BlockSpec((tm, 256), lambda j, i: (i, off + j))
    vec = lambda off: pl.BlockSpec((1, 256), lambda j, i: (0, off + j))
    return pl.pallas_call(body, name="merge_fwd", grid=(4, S // tm),
                          in_specs=[blk(COL_Z0), blk(COL_Z1), vec(0), vec(4), blk(0), blk(0)],
                          out_specs=blk(0), out_shape=_sds((S, D), bf16),
                          compiler_params=_params(2))(proj, proj, b_gate, b_gate, y_a, y_b)


def _merge_bwd(proj, b_gate, y_a, y_b, dm, tm):
    S = proj.shape[0]
    tm = min(tm, S)

    def body(z0_ref, z1_ref, b0_ref, b1_ref, ya_ref, yb_ref, dm_ref, dz0_ref, dz1_ref, dya_ref, dyb_ref, db0_ref, db1_ref):
        g0 = _sigmoid(z0_ref[...] + b0_ref[...])
        g1 = _sigmoid(z1_ref[...] + b1_ref[...])
        d = dm_ref[...]
        dz0 = (d * ya_ref[...]) * (g0 * (1.0 - g0))
        dz1 = (d * yb_ref[...]) * (g1 * (1.0 - g1))
        dz0_ref[...] = dz0.astype(bf16)
        dz1_ref[...] = dz1.astype(bf16)
        dya_ref[...] = d * g0
        dyb_ref[...] = d * g1

        @pl.when(pl.program_id(1) == 0)
        def _():
            db0_ref[...] = jnp.zeros_like(db0_ref)
            db1_ref[...] = jnp.zeros_like(db1_ref)

        db0_ref[...] += jnp.sum(dz0, axis=0, keepdims=True)
        db1_ref[...] += jnp.sum(dz1, axis=0, keepdims=True)

    blk = lambda off: pl.BlockSpec((tm, 256), lambda j, i: (i, off + j))
    vec = lambda off: pl.BlockSpec((1, 256), lambda j, i: (0, off + j))
    return pl.pallas_call(
        body, name="merge_bwd", grid=(4, S // tm),
        in_specs=[blk(COL_Z0), blk(COL_Z1), vec(0), vec(4), blk(0), blk(0), blk(0)],
        out_specs=[blk(0), blk(0), blk(0), blk(0), vec(0), vec(0)],
        out_shape=[_sds((S, D), bf16), _sds((S, D), bf16), _sds((S, D), f32), _sds((S, D), f32),
                   _sds((1, D), f32), _sds((1, D), f32)],
        compiler_params=_params(2))(proj, proj, b_gate, b_gate, y_a, y_b, dm)


def _swiglu_bwd(dx, w, gu, tm):
    S, K = dx.shape
    tm = min(tm, S)

    def body(dx_ref, w_ref, gu_ref, o_ref):
        d = lax.dot_general(dx_ref[...].astype(bf16), w_ref[...], _DIMS["nt"], preferred_element_type=f32)
        g = gu_ref[0].astype(f32)
        u = gu_ref[1].astype(f32)
        s = _sigmoid(g)
        o_ref[0] = ((d * u) * (s * (1.0 + g * (1.0 - s)))).astype(bf16)
        o_ref[1] = (d * (g * s)).astype(bf16)

    stacked = pl.BlockSpec((2, tm, FF), lambda i: (0, i, 0))
    return pl.pallas_call(body, name="swiglu_bwd", grid=(S // tm,),
                          in_specs=[pl.BlockSpec((tm, K), lambda i: (i, 0)), pl.BlockSpec((FF, K), lambda i: (0, 0)), stacked],
                          out_specs=stacked, out_shape=_sds((2, S, FF), bf16),
                          compiler_params=_params(1, True))(dx, w, gu)


def _final_loss_bwd(x2, g3, tgt, tm):
    S = x2.shape[0]
    tm = min(tm, S)

    def body(x_ref, g_ref, t_ref, dx_ref, loss_ref, dg_ref):
        @pl.when(pl.program_id(0) == 0)
        def _():
            loss_ref[...] = jnp.zeros_like(loss_ref)
            dg_ref[...] = jnp.zeros_like(dg_ref)

        x = x_ref[...]
        g = g_ref[...]
        r = lax.rsqrt(jnp.mean(x * x, axis=-1, keepdims=True) + EPS)
        xh = x * r
        err = xh * g - t_ref[...]
        row = jnp.mean(err * err, axis=-1, keepdims=True)
        loss_ref[...] += 0.5 * jnp.sum(row, axis=0, keepdims=True)
        dy = err * (1.0 / D)
        dg_ref[...] += jnp.sum(dy * xh, axis=0, keepdims=True)
        dxh = dy * g
        dx_ref[...] = r * (dxh - xh * jnp.mean(dxh * xh, axis=-1, keepdims=True))

    row_blk = pl.BlockSpec((tm, D), lambda i: (i, 0))
    vec = pl.BlockSpec((1, D), lambda i: (0, 0))
    return pl.pallas_call(body, name="final_loss_bwd", grid=(S // tm,), in_specs=[row_blk, vec, row_blk],
                          out_specs=[row_blk, pl.BlockSpec((1, 128), lambda i: (0, 0)), vec],
                          out_shape=[_sds((S, D), f32), _sds((1, 128), f32), _sds((1, D), f32)],
                          compiler_params=_params(1))(x2, g3, tgt)


def _rms_bwd(name, x, g, dxn, dres, tm):
    S = x.shape[0]
    tm = min(tm, S)

    def body(x_ref, g_ref, d_ref, r_ref, dx_ref, dg_ref):
        @pl.when(pl.program_id(0) == 0)
        def _():
            dg_ref[...] = jnp.zeros_like(dg_ref)

        x = x_ref[...]
        d = d_ref[...]
        r = lax.rsqrt(jnp.mean(x * x, axis=-1, keepdims=True) + EPS)
        xh = x * r
        dg_ref[...] += jnp.sum(d * xh, axis=0, keepdims=True)
        dxh = d * g_ref[...]
        dx_ref[...] = r_ref[...] + r * (dxh - xh * jnp.mean(dxh * xh, axis=-1, keepdims=True))

    row_blk = pl.BlockSpec((tm, D), lambda i: (i, 0))
    vec = pl.BlockSpec((1, D), lambda i: (0, 0))
    return pl.pallas_call(body, name=name, grid=(S // tm,), in_specs=[row_blk, vec, row_blk, row_blk],
                          out_specs=[row_blk, vec], out_shape=[_sds((S, D), f32), _sds((1, D), f32)],
                          compiler_params=_params(1))(x, g, dxn, dres)


LRU_TT = 256
SCAN_UNROLL = 4


def _halo(ref, i, S):
    nt = S // LRU_TT
    t0 = pl.multiple_of(i * LRU_TT, LRU_TT)
    p0 = pl.multiple_of(jnp.maximum(t0 - 8, 0), 8)
    n0 = pl.multiple_of(jnp.minimum(t0 + LRU_TT, S - 8), 8)
    prev = jnp.where(i > 0, ref[pl.ds(p0, 8), :], 0.0)
    nxt = jnp.where(i < nt - 1, ref[pl.ds(n0, 8), :], 0.0)
    return jnp.concatenate([prev, ref[pl.ds(t0, LRU_TT), :], nxt], axis=0)


def _shift(ext, k):
    n = LRU_TT + 16
    return pltpu.roll(ext, (-k) % n, 0)[8:8 + LRU_TT]


def _lru_gates(uc, wbd, ba, bx):
    pre = jnp.dot(uc.astype(bf16), wbd, preferred_element_type=f32)
    r_f = _sigmoid(pre[:, 0:CW] + ba[0:1])
    i_f = _sigmoid(pre[:, CW:2 * CW] + bx[0:1])
    r_b = _sigmoid(pre[:, 2 * CW:3 * CW] + ba[1:2])
    i_b = _sigmoid(pre[:, 3 * CW:4 * CW] + bx[1:2])
    return r_f, i_f, r_b, i_b


def _lru_coeffs(r, sp):
    log_a = (-RGLRU_C * r) * sp
    a = jnp.exp(log_a)
    beta = jnp.sqrt(jnp.maximum(_one_minus_exp2x(log_a, a), 0.0))
    return a, beta


def _lru_coeffs_inv(r, sp):
    log_a = (-RGLRU_C * r) * sp
    a = jnp.exp(log_a)
    om = jnp.maximum(_one_minus_exp2x(log_a, a), 0.0)
    return a, jnp.sqrt(om), lax.rsqrt(om)


def _conv_tile(u_ref, i, S, cw, cb):
    ext = _halo(u_ref, i, S)
    um2, um1, u0, up1 = _shift(ext, -2), _shift(ext, -1), ext[8:8 + LRU_TT], _shift(ext, 1)
    uc = um2 * cw[0:1] + um1 * cw[1:2] + u0 * cw[2:3] + up1 * cw[3:4] + cb
    return uc, (um2, um1, u0, up1)


def _scan_pair(S, fwd_a, fwd_b, fwd_out, rev_a, rev_b, rev_out):
    ng = S // 8
    idx = lax.broadcasted_iota(jnp.int32, (8, CW), 0)

    def local(a, b, rev):
        for sh in (1, 2, 4):
            if rev:
                keep = idx < 8 - sh
                amt = 8 - sh
            else:
                keep = idx >= sh
                amt = sh
            a_s = jnp.where(keep, pltpu.roll(a, amt, 0), 1.0)
            b_s = jnp.where(keep, pltpu.roll(b, amt, 0), 0.0)
            b = a * b_s + b
            a = a * a_s
        return a, b

    def step(it, carry):
        cf, cr = carry
        fwd_rows = [pl.multiple_of((it * SCAN_UNROLL + j) * 8, 8) for j in range(SCAN_UNROLL)]
        rev_rows = [pl.multiple_of((ng - 1 - (it * SCAN_UNROLL + j)) * 8, 8) for j in range(SCAN_UNROLL)]
        fwd_loc = [local(fwd_a(r), fwd_b(r), False) for r in fwd_rows]
        rev_loc = [local(rev_a(r), rev_b(r), True) for r in rev_rows]
        for j in range(SCAN_UNROLL):
            a, b = fwd_loc[j]
            h = a * cf + b
            fwd_out[pl.ds(fwd_rows[j], 8), :] = h
            cf = jnp.broadcast_to(h[7:8, :], (8, CW))
            a, b = rev_loc[j]
            h = a * cr + b
            rev_out[pl.ds(rev_rows[j], 8), :] = h
            cr = jnp.broadcast_to(h[0:1, :], (8, CW))
        return cf, cr

    zero = jnp.zeros((8, CW), f32)
    lax.fori_loop(0, ng // SCAN_UNROLL, step, (zero, zero))


def _lru_specs(S):
    seq = lambda off: pl.BlockSpec((S, CW), lambda j: (0, off + j))
    par = lambda rows: pl.BlockSpec((rows, CW), lambda j: (0, j))
    return seq, par


def _lru_fwd(proj, conv_w, conv_b, lam, ba, bx, wbd):
    S = proj.shape[0]
    nt = S // LRU_TT

    def body(u_ref, g_ref, cw_ref, cb_ref, lam_ref, ba_ref, bx_ref, wbd_ref, y_ref, af_ref, bf_ref, ab_ref, bb_ref):
        cw, cb, ba_v, bx_v, wbd_v = cw_ref[...], cb_ref[...], ba_ref[...], bx_ref[...], wbd_ref[...]
        sp = jax.nn.softplus(-lam_ref[...])

        def phase1(i, c):
            uc, _ = _conv_tile(u_ref, i, S, cw, cb)
            r_f, i_f, r_b, i_b = _lru_gates(uc, wbd_v, ba_v, bx_v)
            rows = pl.ds(pl.multiple_of(i * LRU_TT, LRU_TT), LRU_TT)
            a, beta = _lru_coeffs(r_f, sp[0:1])
            af_ref[rows, :] = a
            bf_ref[rows, :] = beta * (i_f * uc)
            a, beta = _lru_coeffs(r_b, sp[1:2])
            ab_ref[rows, :] = a
            bb_ref[rows, :] = beta * (i_b * uc)
            return c

        lax.fori_loop(0, nt, phase1, 0)
        row8 = lambda ref: (lambda r0: ref[pl.ds(r0, 8), :])
        _scan_pair(S, row8(af_ref), row8(bf_ref), bf_ref, row8(ab_ref), row8(bb_ref), bb_ref)

        def phase3(i, c):
            rows = pl.ds(pl.multiple_of(i * LRU_TT, LRU_TT), LRU_TT)
            y_ref[rows, :] = (bf_ref[rows, :] + bb_ref[rows, :]) * jax.nn.gelu(g_ref[rows, :])
            return c

        lax.fori_loop(0, nt, phase3, 0)

    seq, par = _lru_specs(S)
    return pl.pallas_call(
        body, name="lru_fwd", grid=(NCH,),
        in_specs=[seq(0), seq(NCH), par(4), par(1), par(2), par(2), par(2),
                  pl.BlockSpec((None, CW, 4 * CW), lambda j: (j, 0, 0))],
        out_specs=seq(0), out_shape=_sds((S, D), f32),
        scratch_shapes=[pltpu.VMEM((S, CW), f32)] * 4, compiler_params=_params(1, True),
    )(proj, proj, conv_w, conv_b, lam, ba, bx, wbd)


def _lru_bwd(proj, dy, conv_w, conv_b, lam, ba, bx, wbd):
    S = proj.shape[0]
    nt = S // LRU_TT

    def body(u_ref, g_ref, dy_ref, cw_ref, cb_ref, lam_ref, ba_ref, bx_ref, wbd_ref,
             du_ref, dg_ref, dcw_ref, dcb_ref, dlam_ref, dba_ref, dbx_ref, dwbd_ref,
             af_ref, bf_ref, ab_ref, bb_ref):
        cw, cb, ba_v, bx_v, wbd_v = cw_ref[...], cb_ref[...], ba_ref[...], bx_ref[...], wbd_ref[...]
        lam_v = lam_ref[...]
        sp = jax.nn.softplus(-lam_v)

        def phase1(i, c):
            uc, _ = _conv_tile(u_ref, i, S, cw, cb)
            r_f, i_f, r_b, i_b = _lru_gates(uc, wbd_v, ba_v, bx_v)
            rows = pl.ds(pl.multiple_of(i * LRU_TT, LRU_TT), LRU_TT)
            a, beta = _lru_coeffs(r_f, sp[0:1])
            af_ref[rows, :] = a
            bf_ref[rows, :] = beta * (i_f * uc)
            a, beta = _lru_coeffs(r_b, sp[1:2])
            ab_ref[rows, :] = a
            bb_ref[rows, :] = beta * (i_b * uc)
            return c

        lax.fori_loop(0, nt, phase1, 0)
        row8 = lambda ref: (lambda r0: ref[pl.ds(r0, 8), :])
        _scan_pair(S, row8(af_ref), row8(bf_ref), bf_ref, row8(ab_ref), row8(bb_ref), bb_ref)

        def scaled_dh(a_ref):
            def f(r0):
                gl, _ = _gelu_and_grad(g_ref[pl.ds(r0, 8), :])
                return a_ref[pl.ds(r0, 8), :] * (dy_ref[pl.ds(r0, 8), :] * gl)
            return f

        _scan_pair(S, row8(ab_ref), scaled_dh(ab_ref), ab_ref, row8(af_ref), scaled_dh(af_ref), af_ref)

        dcw_ref[...] = jnp.zeros_like(dcw_ref)
        dcb_ref[...] = jnp.zeros_like(dcb_ref)
        dlam_ref[...] = jnp.zeros_like(dlam_ref)
        dba_ref[...] = jnp.zeros_like(dba_ref)
        dbx_ref[...] = jnp.zeros_like(dbx_ref)
        dwbd_ref[...] = jnp.zeros_like(dwbd_ref)

        def direction(uc, r, i_g, dht, h_nb, sp_d):
            a, beta, inv_beta = _lru_coeffs_inv(r, sp_d)
            da = dht * h_nb
            dbeta = dht * (i_g * uc)
            d_iu = dht * beta
            dlog_a = da * a - (a * a) * (dbeta * inv_beta)
            dr = dlog_a * (-RGLRU_C * sp_d)
            dsp = jnp.sum(dlog_a * (-RGLRU_C * r), axis=0, keepdims=True)
            dpre_r = dr * (r * (1.0 - r))
            dpre_i = (d_iu * uc) * (i_g * (1.0 - i_g))
            return dpre_r, dpre_i, d_iu * i_g, dsp

        def phase4(i, c):
            uc, (um2, um1, u0, up1) = _conv_tile(u_ref, i, S, cw, cb)
            r_f, i_f, r_b, i_b = _lru_gates(uc, wbd_v, ba_v, bx_v)
            rows = pl.ds(pl.multiple_of(i * LRU_TT, LRU_TT), LRU_TT)
            gl, dgl = _gelu_and_grad(g_ref[rows, :])
            dyt = dy_ref[rows, :]
            dh = dyt * gl
            dg_ref[rows, :] = ((dyt * (bf_ref[rows, :] + bb_ref[rows, :])) * dgl).astype(dg_ref.dtype)
            dht_f = dh + _shift(_halo(af_ref, i, S), 1)
            h_prev = _shift(_halo(bf_ref, i, S), -1)
            dht_b = dh + _shift(_halo(ab_ref, i, S), -1)
            h_next = _shift(_halo(bb_ref, i, S), 1)
            prf, pif, duc_f, dsp_f = direction(uc, r_f, i_f, dht_f, h_prev, sp[0:1])
            prb, pib, duc_b, dsp_b = direction(uc, r_b, i_b, dht_b, h_next, sp[1:2])
            dpre = jnp.concatenate([prf, pif, prb, pib], axis=1)
            dpre_b = dpre.astype(bf16)
            duc = (duc_f + duc_b) + lax.dot_general(dpre_b, wbd_v, _DIMS["nt"], preferred_element_type=f32)
            dwbd_ref[...] += lax.dot_general(uc.astype(bf16), dpre_b, _DIMS["tn"], preferred_element_type=f32)
            colsum = lambda v: jnp.sum(v, axis=0, keepdims=True)
            dba_ref[...] += jnp.concatenate([colsum(prf), colsum(prb)], axis=0)
            dbx_ref[...] += jnp.concatenate([colsum(pif), colsum(pib)], axis=0)
            dlam_ref[...] += jnp.concatenate([dsp_f, dsp_b], axis=0)
            dcb_ref[...] += colsum(duc)
            dcw_ref[...] += jnp.concatenate([colsum(duc * um2), colsum(duc * um1), colsum(duc * u0),
                                             colsum(duc * up1)], axis=0)
            af_ref[rows, :] = duc
            return c

        lax.fori_loop(0, nt, phase4, 0)
        dlam_ref[...] = dlam_ref[...] * (-_sigmoid(-lam_v))

        def phase5(i, c):
            ext = _halo(af_ref, i, S)
            rows = pl.ds(pl.multiple_of(i * LRU_TT, LRU_TT), LRU_TT)
            du = (_shift(ext, 2) * cw[0:1] + _shift(ext, 1) * cw[1:2] + ext[8:8 + LRU_TT] * cw[2:3]
                  + _shift(ext, -1) * cw[3:4])
            du_ref[rows, :] = du.astype(du_ref.dtype)
            return c

        lax.fori_loop(0, nt, phase5, 0)

    seq, par = _lru_specs(S)
    return pl.pallas_call(
        body, name="lru_bwd", grid=(NCH,),
        in_specs=[seq(0), seq(NCH), seq(0), par(4), par(1), par(2), par(2), par(2),
                  pl.BlockSpec((None, CW, 4 * CW), lambda j: (j, 0, 0))],
        out_specs=[seq(0), seq(0), par(4), par(1), par(2), par(2), par(2),
                   pl.BlockSpec((None, CW, 4 * CW), lambda j: (j, 0, 0))],
        out_shape=[_sds((S, D), bf16), _sds((S, D), bf16), _sds((4, D), f32), _sds((1, D), f32), _sds((2, D), f32),
                   _sds((2, D), f32), _sds((2, D), f32), _sds((NCH, CW, 4 * CW), f32)],
        scratch_shapes=[pltpu.VMEM((S, CW), f32)] * 4, compiler_params=_params(1, True),
    )(proj, proj, dy, conv_w, conv_b, lam, ba, bx, wbd)


_SLOPES = [2.0 ** (-8.0 * (h + 1) / NH) for h in range(NH)]


def _half_mask(shape, e):
    lane = lax.broadcasted_iota(jnp.int32, shape, 1)
    return (lane < HD) if e == 0 else (lane >= HD)


def _both_halves(x, src):
    return jnp.where(_half_mask(x.shape, src), x, pltpu.roll(x, HD, 1))


def _fold_halves(x, dst):
    return jnp.where(_half_mask(x.shape, dst), x + pltpu.roll(x, HD, 1), 0.0)


def _attn_base(n, S):
    tq = lax.broadcasted_iota(jnp.int32, (BLK, 3 * BLK), 0)
    sk = lax.broadcasted_iota(jnp.int32, (BLK, 3 * BLK), 1)
    dist = jnp.abs(tq + BLK - sk)
    kpos = n * BLK - BLK + sk
    valid = (dist <= BLK) & (kpos >= 0) & (kpos < S)
    return jnp.where(valid, -dist.astype(f32), NEG_INF)


def _group_heads(ref, kvh, scale):
    parts = []
    for i in range(4):
        pair = 2 * kvh + i // 2
        x = ref[:, pair * 128:(pair + 1) * 128]
        parts.append(jnp.where(_half_mask(x.shape, i % 2), x * scale, 0.0))
    return parts


def _stack_bf16(parts):
    return jnp.concatenate([p.astype(bf16) for p in parts], axis=0)


def _attn_softmax(s_raw, base, slope, sink):
    s = s_raw + slope * base
    m = jnp.maximum(jnp.max(s, axis=-1, keepdims=True), sink)
    p = jnp.exp(s - m)
    esink = jnp.exp(sink - m)
    inv = 1.0 / (jnp.sum(p, axis=-1, keepdims=True) + esink)
    return p, inv, esink * inv


def _attn_specs(S):
    nb = S // BLK
    q_spec = pl.BlockSpec((BLK, D), lambda n: (n, 2))
    kv = lambda col: [pl.BlockSpec((BLK, 256), lambda n: (jnp.maximum(n - 1, 0), col)),
                      pl.BlockSpec((BLK, 256), lambda n: (n, col)),
                      pl.BlockSpec((BLK, 256), lambda n: (jnp.minimum(n + 1, nb - 1), col))]
    return nb, q_spec, kv(COL_K), kv(COL_V)


def _attn_fwd(proj, sink):
    S = proj.shape[0]
    nb, q_spec, k_specs, v_specs = _attn_specs(S)

    def body(sink_ref, q_ref, kp_ref, kc_ref, kn_ref, vp_ref, vc_ref, vn_ref, o_ref):
        base = _attn_base(pl.program_id(0), S)
        kcat = jnp.concatenate([kp_ref[...], kc_ref[...], kn_ref[...]], axis=0)
        vcat = jnp.concatenate([vp_ref[...], vc_ref[...], vn_ref[...]], axis=0)
        even = _half_mask((BLK, 128), 0)
        for kvh in range(NH // 4):
            ch, off = kvh // 2, kvh % 2
            kb = _both_halves(kcat[:, ch * 128:(ch + 1) * 128], off).astype(bf16)
            vb = _both_halves(vcat[:, ch * 128:(ch + 1) * 128], off).astype(bf16)
            q4 = _stack_bf16(_group_heads(q_ref, kvh, HD ** -0.5))
            s4 = lax.dot_general(q4, kb, _DIMS["nt"], preferred_element_type=f32)
            ps, invs = [], []
            for i in range(4):
                h = 4 * kvh + i
                p, inv, _ = _attn_softmax(s4[i * BLK:(i + 1) * BLK], base, _SLOPES[h], sink_ref[0, h])
                ps.append(p)
                invs.append(inv)
            o4 = jnp.dot(_stack_bf16(ps), vb, preferred_element_type=f32)
            for pr in range(2):
                lo = o4[(2 * pr) * BLK:(2 * pr + 1) * BLK] * invs[2 * pr]
                hi = o4[(2 * pr + 1) * BLK:(2 * pr + 2) * BLK] * invs[2 * pr + 1]
                pair = 2 * kvh + pr
                o_ref[:, pair * 128:(pair + 1) * 128] = jnp.where(even, lo, hi)

    return pl.pallas_call(
        body, name="attn_fwd", grid=(nb,),
        in_specs=[pl.BlockSpec(memory_space=pltpu.SMEM), q_spec] + k_specs + v_specs,
        out_specs=pl.BlockSpec((BLK, D), lambda n: (n, 0)), out_shape=_sds((S, D), f32),
        compiler_params=_params(1, True))(sink, proj, proj, proj, proj, proj, proj, proj)


def _attn_bwd(proj, sink, y_b, dy_b):
    S = proj.shape[0]
    nb, q_spec, k_specs, v_specs = _attn_specs(S)

    def body(sink_ref, q_ref, kp_ref, kc_ref, kn_ref, vp_ref, vc_ref, vn_ref, o_ref, do_ref,
             dq_ref, dk_ref, dv_ref, dsink_ref):
        n = pl.program_id(0)

        @pl.when(n == 0)
        def _():
            dk_ref[...] = jnp.zeros_like(dk_ref)
            dv_ref[...] = jnp.zeros_like(dv_ref)
            dsink_ref[...] = jnp.zeros_like(dsink_ref)

        base = _attn_base(n, S)
        kcat = jnp.concatenate([kp_ref[...], kc_ref[...], kn_ref[...]], axis=0)
        vcat = jnp.concatenate([vp_ref[...], vc_ref[...], vn_ref[...]], axis=0)
        dk_acc = [jnp.zeros((3 * BLK, 128), f32), jnp.zeros((3 * BLK, 128), f32)]
        dv_acc = [jnp.zeros((3 * BLK, 128), f32), jnp.zeros((3 * BLK, 128), f32)]
        scale = HD ** -0.5
        even = _half_mask((BLK, 128), 0)
        for kvh in range(NH // 4):
            ch, off = kvh // 2, kvh % 2
            kb = _both_halves(kcat[:, ch * 128:(ch + 1) * 128], off).astype(bf16)
            vb = _both_halves(vcat[:, ch * 128:(ch + 1) * 128], off).astype(bf16)
            q_parts = _group_heads(q_ref, kvh, scale)
            d_parts = _group_heads(do_ref, kvh, 1.0)
            s4 = lax.dot_general(_stack_bf16(q_parts), kb, _DIMS["nt"], preferred_element_type=f32)
            dp4 = lax.dot_general(_stack_bf16(d_parts), vb, _DIMS["nt"], preferred_element_type=f32)
            ts, ps, qn, dn, invs = [], [], [], [], []
            for i in range(4):
                h = 4 * kvh + i
                pair = 2 * kvh + i // 2
                rows = slice(i * BLK, (i + 1) * BLK)
                p, inv, psink = _attn_softmax(s4[rows], base, _SLOPES[h], sink_ref[0, h])
                delta = jnp.sum(d_parts[i] * o_ref[:, pair * 128:(pair + 1) * 128], axis=-1, keepdims=True)
                dsink_ref[h:h + 1, :] += jnp.broadcast_to(-jnp.sum(psink * delta, axis=0, keepdims=True), (1, 128))
                ts.append(p * (dp4[rows] - delta))
                ps.append(p)
                qn.append(q_parts[i] * inv)
                dn.append(d_parts[i] * inv)
                invs.append(inv)
            t4 = _stack_bf16(ts)
            dq4 = jnp.dot(t4, kb, preferred_element_type=f32)
            for pr in range(2):
                lo = dq4[(2 * pr) * BLK:(2 * pr + 1) * BLK] * invs[2 * pr]
                hi = dq4[(2 * pr + 1) * BLK:(2 * pr + 2) * BLK] * invs[2 * pr + 1]
                pair = 2 * kvh + pr
                dq_ref[:, pair * 128:(pair + 1) * 128] = (jnp.where(even, lo, hi) * scale).astype(dq_ref.dtype)
            dk_both = lax.dot_general(t4, _stack_bf16(qn), _DIMS["tn"], preferred_element_type=f32)
            dv_both = lax.dot_general(_stack_bf16(ps), _stack_bf16(dn), _DIMS["tn"], preferred_element_type=f32)
            dk_acc[ch] = dk_acc[ch] + _fold_halves(dk_both, off)
            dv_acc[ch] = dv_acc[ch] + _fold_halves(dv_both, off)
        for j in range(3):
            blk = n + (j - 1)

            @pl.when((blk >= 0) & (blk < nb))
            def _():
                rows = pl.ds(pl.multiple_of(blk * BLK, BLK), BLK)
                for ch in range(2):
                    dk_ref[rows, ch * 128:(ch + 1) * 128] += dk_acc[ch][j * BLK:(j + 1) * BLK]
                    dv_ref[rows, ch * 128:(ch + 1) * 128] += dv_acc[ch][j * BLK:(j + 1) * BLK]

    row_blk = pl.BlockSpec((BLK, D), lambda n: (n, 0))
    full = pl.BlockSpec((S, 256), lambda n: (0, 0))
    return pl.pallas_call(
        body, name="attn_bwd", grid=(nb,),
        in_specs=[pl.BlockSpec(memory_space=pltpu.SMEM), q_spec] + k_specs + v_specs + [row_blk, row_blk],
        out_specs=[row_blk, full, full, pl.BlockSpec((NH, 128), lambda n: (0, 0))],
        out_shape=[_sds((S, D), bf16), _sds((S, 256), f32), _sds((S, 256), f32), _sds((NH, 128), f32)],
        compiler_params=_params(1, True))(sink, proj, proj, proj, proj, proj, proj, proj, y_b, dy_b)


def _adamw(name, w, g, m, v, tr):
    R, C = w.shape
    tr = min(tr, R)

    def body(w_ref, g_ref, m_ref, v_ref, d_ref, m2_ref, v2_ref):
        g = g_ref[...]
        m2 = ADAM_B1 * m_ref[...] + (1.0 - ADAM_B1) * g
        v2 = ADAM_B2 * v_ref[...] + (1.0 - ADAM_B2) * (g * g)
        m_hat = m2 / (1.0 - ADAM_B1 ** ADAM_STEP)
        v_hat = v2 / (1.0 - ADAM_B2 ** ADAM_STEP)
        d_ref[...] = -ADAM_LR * (m_hat / (jnp.sqrt(v_hat) + ADAM_EPS) + ADAM_WD * w_ref[...])
        m2_ref[...] = m2
        v2_ref[...] = v2

    blk = pl.BlockSpec((tr, C), lambda i: (i, 0))
    return pl.pallas_call(body, name=name, grid=(R // tr,), in_specs=[blk] * 4, out_specs=[blk] * 3,
                          out_shape=[_sds((R, C), f32)] * 3, compiler_params=_params(1))(w, g, m, v)


def _pair_sum(name, c_arr, g4, recv, th):
    _, _, h, w = g4.shape
    th = min(th, h)

    def body(c_ref, g_ref, r_ref, o_ref, ob_ref):
        p = g_ref[...] + r_ref[...]
        o_ref[...] = p
        ob_ref[...] = p.astype(bf16)

    blk = pl.BlockSpec((None, th, w), lambda s, i, c_ref: (s, i, 0))
    spec = pltpu.PrefetchScalarGridSpec(
        num_scalar_prefetch=1, grid=(NCHIP, h // th),
        in_specs=[pl.BlockSpec((None, None, th, w), lambda s, i, c_ref: (s, c_ref[0], i, 0)), blk],
        out_specs=[blk, blk])
    return pl.pallas_call(body, name=name, grid_spec=spec,
                          out_shape=[_sds((NCHIP, h, w), f32), _sds((NCHIP, h, w), bf16)],
                          compiler_params=_params(2))(c_arr, g4, recv)


def _chip_sum(name, chip_arr, own4, recv3, th):
    _, h, w = own4.shape
    th = min(th, h)

    def body(s_ref, o_ref, r_ref, out_ref):
        out_ref[...] = ((o_ref[...] + r_ref[0].astype(f32)) + r_ref[1].astype(f32)) + r_ref[2].astype(f32)

    spec = pltpu.PrefetchScalarGridSpec(
        num_scalar_prefetch=1, grid=(h // th,),
        in_specs=[pl.BlockSpec((None, th, w), lambda i, s_ref: (s_ref[0], i, 0)),
                  pl.BlockSpec((3, th, w), lambda i, s_ref: (0, i, 0))],
        out_specs=pl.BlockSpec((th, w), lambda i, s_ref: (i, 0)))
    return pl.pallas_call(body, name=name, grid_spec=spec, out_shape=_sds((h, w), f32),
                          compiler_params=_params(1, True))(chip_arr, own4, recv3)


def _adamw_halves(name, c_arr, w, g_own, g_recv, m, v, th):
    h, wd = g_own.shape
    th = min(th, h)

    def body(c_ref, w_ref, go_ref, gr_ref, m_ref, v_ref, g_ref, d_ref, m2_ref, v2_ref):
        g = jnp.where(c_ref[0] == pl.program_id(0), go_ref[...], gr_ref[...])
        m2 = ADAM_B1 * m_ref[...] + (1.0 - ADAM_B1) * g
        v2 = ADAM_B2 * v_ref[...] + (1.0 - ADAM_B2) * (g * g)
        m_hat = m2 / (1.0 - ADAM_B1 ** ADAM_STEP)
        v_hat = v2 / (1.0 - ADAM_B2 ** ADAM_STEP)
        g_ref[...] = g
        d_ref[...] = -ADAM_LR * (m_hat / (jnp.sqrt(v_hat) + ADAM_EPS) + ADAM_WD * w_ref[...])
        m2_ref[...] = m2
        v2_ref[...] = v2

    nt = h // th
    full = pl.BlockSpec((th, wd), lambda hh, i, c_ref: (hh * nt + i, 0))
    half = pl.BlockSpec((th, wd), lambda hh, i, c_ref: (i, 0))
    spec = pltpu.PrefetchScalarGridSpec(num_scalar_prefetch=1, grid=(2, nt),
                                        in_specs=[full, half, half, full, full], out_specs=[full] * 4)
    return pl.pallas_call(body, name=name, grid_spec=spec, out_shape=[_sds((2 * h, wd), f32)] * 4,
                          compiler_params=_params(2))(c_arr, w, g_own, g_recv, m, v)


def _add2(name, a, b):
    def body(a_ref, b_ref, o_ref):
        o_ref[...] = a_ref[...] + b_ref[...]
    return pl.pallas_call(body, name=name, out_shape=_sds(a.shape, f32))(a, b)


def _sum4(name, b4, th):
    _, h, w = b4.shape
    th = min(th, h)

    def body(b_ref, o_ref):
        o_ref[...] = ((b_ref[0] + b_ref[1]) + b_ref[2]) + b_ref[3]

    return pl.pallas_call(body, name=name, grid=(h // th,),
                          in_specs=[pl.BlockSpec((NCHIP, th, w), lambda i: (0, i, 0))],
                          out_specs=pl.BlockSpec((th, w), lambda i: (i, 0)), out_shape=_sds((h, w), f32),
                          compiler_params=_params(1, True))(b4)


def _coords():
    x, y, c = lax.axis_index("x"), lax.axis_index("y"), lax.axis_index("c")
    return x, y, c, [(1 - x, y), (x, 1 - y), (1 - x, 1 - y)]


def _gather_chips(arrs):
    n = len(arrs)

    def body(*refs):
        ins, outs = refs[:n], refs[n:2 * n]
        send_sems, recv_sems, local_sems = refs[2 * n:2 * n + 3]
        stage = refs[2 * n + 3:]
        x, y, c, chips = _coords()
        s = 2 * x + y
        sib = (x, y, 1 - c)
        load = [pltpu.make_async_copy(ins[a], stage[a], local_sems.at[a]) for a in range(n)]
        local = [pltpu.make_async_copy(stage[a], outs[a].at[s], local_sems.at[n + a]) for a in range(n)]
        for cp in load:
            cp.start()

        def over_ici(k, a, slot, peer):
            return pltpu.make_async_remote_copy(src_ref=ins[a].at[c], dst_ref=outs[a].at[slot, c], send_sem=send_sems.at[k * n + a],
                                                recv_sem=recv_sems.at[k * n + a], device_id=peer, device_id_type=MESH)

        def to_sibling(k, a, slot, half):
            i = (3 + k) * n + a
            return pltpu.make_async_remote_copy(src_ref=outs[a].at[slot, half], dst_ref=outs[a].at[slot, half], send_sem=send_sems.at[i],
                                                recv_sem=recv_sems.at[i], device_id=sib, device_id_type=MESH)

        sends = [over_ici(k, a, s, (px, py, c)) for k, (px, py) in enumerate(chips) for a in range(n)]
        for cp in sends:
            cp.start()
        for a in range(n):
            load[a].wait()
            local[a].start()
        passed = []
        for k, (px, py) in enumerate(chips):
            for a in range(n):
                over_ici(k, a, 2 * px + py, (px, py, c)).wait_recv()
                cp = to_sibling(k, a, 2 * px + py, c)
                cp.start()
                passed.append(cp)
        for k, (px, py) in enumerate(chips):
            for a in range(n):
                to_sibling(k, a, 2 * px + py, 1 - c).wait_recv()
        for cp in sends + passed:
            cp.wait_send()
        for cp in local:
            cp.wait()

    return pl.pallas_call(
        body, name="gather_weights", in_specs=[ANY] * n, out_specs=[ANY] * n,
        out_shape=[_sds((NCHIP,) + a.shape, a.dtype) for a in arrs],
        scratch_shapes=[pltpu.SemaphoreType.DMA((6 * n,)), pltpu.SemaphoreType.DMA((6 * n,)), pltpu.SemaphoreType.DMA((2 * n,))]
        + [pltpu.VMEM(a.shape, a.dtype) for a in arrs],
        compiler_params=pltpu.CompilerParams(vmem_limit_bytes=VMEM_LIMIT),
    )(*arrs)


def _sibling_halves(g4s, small):
    n = len(g4s)

    def body(*refs):
        ins, small_ref = refs[:n], refs[n]
        outs, small_out = refs[n + 1:2 * n + 1], refs[2 * n + 1]
        send_sems, recv_sems = refs[2 * n + 2:]
        x, y, c, _ = _coords()
        sib = (x, y, 1 - c)

        def remote(a, half):
            src = small_ref if a == n else ins[a].at[:, half]
            dst = small_out if a == n else outs[a]
            return pltpu.make_async_remote_copy(src_ref=src, dst_ref=dst, send_sem=send_sems.at[a], recv_sem=recv_sems.at[a],
                                                device_id=sib, device_id_type=MESH)

        sends = [remote(a, 1 - c) for a in range(n + 1)]
        for cp in sends:
            cp.start()
        for a in range(n + 1):
            remote(a, c).wait_recv()
        for cp in sends:
            cp.wait_send()

    return pl.pallas_call(
        body, name="reduce_sibling", in_specs=[ANY] * (n + 1), out_specs=[ANY] * (n + 1),
        out_shape=[_sds((g.shape[0],) + g.shape[2:], f32) for g in g4s] + [_sds(small.shape, f32)],
        scratch_shapes=[pltpu.SemaphoreType.DMA((n + 1,)), pltpu.SemaphoreType.DMA((n + 1,))],
    )(*g4s, small)


def _exchange_chips(parts, small2):
    n = len(parts)

    def body(*refs):
        ins, small_ref = refs[:n], refs[n]
        outs, small_out = refs[n + 1:2 * n + 1], refs[2 * n + 1]
        send_sems, recv_sems, local_sem = refs[2 * n + 2:]
        x, y, c, chips = _coords()
        s = 2 * x + y
        local = pltpu.make_async_copy(small_ref.at[c], small_out.at[s], local_sem)
        local.start()

        def remote(k, a, dest_chip, small_slot, peer):
            if a == n:
                src, dst = small_ref.at[c], small_out.at[small_slot]
            else:
                src, dst = ins[a].at[dest_chip], outs[a].at[k]
            i = k * (n + 1) + a
            return pltpu.make_async_remote_copy(src_ref=src, dst_ref=dst, send_sem=send_sems.at[i], recv_sem=recv_sems.at[i],
                                                device_id=peer, device_id_type=MESH)

        sends = [remote(k, a, 2 * px + py, s, (px, py, c)) for k, (px, py) in enumerate(chips) for a in range(n + 1)]
        for cp in sends:
            cp.start()
        for k, (px, py) in enumerate(chips):
            for a in range(n + 1):
                remote(k, a, s, 2 * px + py, (px, py, c)).wait_recv()
        for cp in sends:
            cp.wait_send()
        local.wait()

    m = 3 * (n + 1)
    return pl.pallas_call(
        body, name="reduce_chips", in_specs=[ANY] * (n + 1), out_specs=[ANY] * (n + 1),
        out_shape=[_sds((3,) + p.shape[1:], p.dtype) for p in parts] + [_sds((NCHIP,) + small2.shape[1:], f32)],
        scratch_shapes=[pltpu.SemaphoreType.DMA((m,)), pltpu.SemaphoreType.DMA((m,)), pltpu.SemaphoreType.DMA],
    )(*parts, small2)


def _share_sibling(halves):
    n = len(halves)

    def body(*refs):
        ins, outs = refs[:n], refs[n:2 * n]
        send_sems, recv_sems = refs[2 * n:]
        x, y, c, _ = _coords()
        sib = (x, y, 1 - c)
        sends = [pltpu.make_async_remote_copy(src_ref=ins[a], dst_ref=outs[a], send_sem=send_sems.at[a], recv_sem=recv_sems.at[a],
                                              device_id=sib, device_id_type=MESH) for a in range(n)]
        for cp in sends:
            cp.start()
        for cp in sends:
            cp.wait()

    return pl.pallas_call(
        body, name="reduce_share", in_specs=[ANY] * n, out_specs=[ANY] * n,
        out_shape=[_sds(h.shape, f32) for h in halves],
        scratch_shapes=[pltpu.SemaphoreType.DMA((n,)), pltpu.SemaphoreType.DMA((n,))],
    )(*halves)


def _block_diag_pairs(w):
    w = w.reshape(NCH, 2, HD, HD)
    z = jnp.zeros((NCH, HD, HD), w.dtype)
    return jnp.concatenate([jnp.concatenate([w[:, 0], z], axis=2), jnp.concatenate([z, w[:, 1]], axis=2)], axis=1)


def _diag_blocks(m):
    return jnp.stack([m[:, :HD, :HD], m[:, HD:, HD:]], axis=1).reshape(NH, HD, HD)


def _pack(vs, rows):
    flat = jnp.concatenate([v.reshape(-1) for v in vs])
    return jnp.pad(flat, (0, rows * 128 - flat.shape[0])).reshape(rows, 128)


def _unpack(packed, shapes):
    flat = packed.reshape(-1)
    out, off = [], 0
    for shp in shapes:
        size = math.prod(shp)
        out.append(flat[off:off + size].reshape(shp))
        off += size
    return out


def _rows_for(sizes, multiple):
    rows = -(-sum(sizes) // 128)
    return -(-rows // multiple) * multiple


def kernel(x, norm_mix_g, w_in, b_gate, conv_w, conv_b, lru_lambda, lru_wa, lru_ba, lru_wx, lru_bx, attn_sink, w_out, norm_ffn_g, w_ffn_in, w_ffn_out, norm_final_g, loss_target, m_norm_mix_g, m_w_in, m_b_gate, m_conv_w, m_conv_b, m_lru_lambda, m_lru_wa, m_lru_ba, m_lru_wx, m_lru_bx, m_attn_sink, m_w_out, m_norm_ffn_g, m_w_ffn_in, m_w_ffn_out, m_norm_final_g, v_norm_mix_g, v_w_in, v_b_gate, v_conv_w, v_conv_b, v_lru_lambda, v_lru_wa, v_lru_ba, v_lru_wx, v_lru_bx, v_attn_sink, v_w_out, v_norm_ffn_g, v_w_ffn_in, v_w_ffn_out, v_norm_final_g):
    S = x.shape[1]
    xs = x[0]
    tgt = loss_target[0]
    cx, cy, cc = lax.axis_index("x"), lax.axis_index("y"), lax.axis_index("c")
    chip = 2 * cx + cy
    SW = D // NCHIP

    small_shard = _pack([conv_w[0], lru_lambda[0], lru_ba[0], lru_bx[0]], 32)
    halves_of = lambda a: a.reshape(2, a.shape[0] // 2, a.shape[1])
    w_in_g, w_ffn_in_g, w_out_g, w_ffn_out_g, small_g = _gather_chips(
        [halves_of(w_in[0].astype(bf16)), halves_of(w_ffn_in[0].astype(bf16)), halves_of(w_out[0].astype(bf16)),
         halves_of(w_ffn_out[0].astype(bf16)), halves_of(small_shard)])
    w_in_g, w_ffn_in_g = w_in_g.reshape(NCHIP, D, SHW), w_ffn_in_g.reshape(NCHIP, D, SHW)
    w_out_f = w_out_g.reshape(D, D)
    w_ffn_out_f = w_ffn_out_g.reshape(FF, D)
    small_g = small_g.reshape(NCHIP, 32, 128)
    small_parts = [_unpack(small_g[s], [(4, SW), (2, SW), (2, SW), (2, SW)]) for s in range(NCHIP)]
    conv_w_f, lam_f, ba_f, bx_f = [jnp.concatenate([small_parts[s][p] for s in range(NCHIP)], axis=1) for p in range(4)]
    wbd = jnp.concatenate([_block_diag_pairs(lru_wa[0, 0]), _block_diag_pairs(lru_wx[0, 0]),
                           _block_diag_pairs(lru_wa[0, 1]), _block_diag_pairs(lru_wx[0, 1])], axis=2).astype(bf16)
    conv_b_f = conv_b
    sink = attn_sink

    xn, proj = _rms_matmul("rms_proj", xs, norm_mix_g, w_in_g, 1024)
    y_a = _lru_fwd(proj, conv_w_f, conv_b_f, lam_f, ba_f, bx_f, wbd)
    y_b = _attn_fwd(proj, sink)
    merged = _merge_fwd(proj, b_gate, y_a, y_b, 512)
    x1 = _mm_residual("out_proj", merged, w_out_f, xs, 512)
    xn2, gu, act = _rms_matmul_swiglu("rms_ffn_in", x1, norm_ffn_g, w_ffn_in_g, 1024)
    x2 = _mm_residual("ffn_out", act, w_ffn_out_f, x1, 512)
    dx2, loss_row, dg3 = _final_loss_bwd(x2, norm_final_g.reshape(1, D), tgt, 256)

    tm = min(512, S)
    tk = min(512, S)
    gw_ffn_out = _mm_tn("dw_ffn_out", act, pl.BlockSpec((tk, SHW), lambda i, k: (k, i)),
                        dx2, pl.BlockSpec((tk, D), lambda i, k: (k, 0)),
                        _sds((FF, D), f32), pl.BlockSpec((SHW, D), lambda i, k: (i, 0)), (2, S // tk), (SHW, D))
    dgu = _swiglu_bwd(dx2, w_ffn_out_f, gu, 256)
    dxn2 = _mm_nt_groups("dxn2", dgu, pl.BlockSpec((None, tm, SHW), lambda i, g: (g // 2, i, g % 2)), w_ffn_in_g, S, tm)
    gw_ffn_in = _mm_tn("dw_ffn_in", xn2, pl.BlockSpec((tk, D), lambda g, k: (k, 0)),
                       dgu, pl.BlockSpec((None, tk, SHW), lambda g, k: (g // 2, k, g % 2)),
                       _sds((NCHIP, D, SHW), f32), pl.BlockSpec((None, D, SHW), lambda g, k: (g, 0, 0)),
                       (NCHIP, S // tk), (D, SHW))
    dx1, dg2 = _rms_bwd("rms_ffn_bwd", x1, norm_ffn_g, dxn2, dx2, 256)

    dmerged = _mm_nt_resident("d_merged", dx1, w_out_f, 512)
    gw_out = _mm_tn("dw_out", merged, pl.BlockSpec((tk, D), lambda i, k: (k, 0)),
                    dx1, pl.BlockSpec((tk, D), lambda i, k: (k, 0)),
                    _sds((D, D), f32), pl.BlockSpec((D, D), lambda i, k: (0, 0)), (1, S // tk), (D, D))
    dz0, dz1, dy_a, dy_b, db0, db1 = _merge_bwd(proj, b_gate, y_a, y_b, dmerged, 512)
    du, dgl, dcw, dcb, dlam, dba, dbx, dwbd = _lru_bwd(proj, dy_a, conv_w_f, conv_b_f, lam_f, ba_f, bx_f, wbd)
    dq, dk, dv, dsink = _attn_bwd(proj, sink, y_b, dy_b)
    dproj = jnp.concatenate([du, dgl, dq, dk.astype(bf16), dv.astype(bf16), dz0, dz1], axis=1)
    dxn = _mm_nt_groups("dxn", dproj, pl.BlockSpec((tm, SHW), lambda i, g: (i, g)), w_in_g, S, tm)
    gw_in = _mm_tn("dw_in", xn, pl.BlockSpec((tk, D), lambda g, k: (k, 0)),
                   dproj, pl.BlockSpec((tk, SHW), lambda g, k: (k, g)),
                   _sds((NCHIP, D, SHW), f32), pl.BlockSpec((None, D, SHW), lambda g, k: (g, 0, 0)),
                   (NCHIP, S // tk), (D, SHW))
    grad_x, dg1 = _rms_bwd("rms_mix_bwd", xs, norm_mix_g, dxn, dx1, 256)

    d_wa = jnp.stack([_diag_blocks(dwbd[:, :, 0:CW]), _diag_blocks(dwbd[:, :, 2 * CW:3 * CW])])
    d_wx = jnp.stack([_diag_blocks(dwbd[:, :, CW:2 * CW]), _diag_blocks(dwbd[:, :, 3 * CW:4 * CW])])
    small_full = [dg1, jnp.concatenate([db0, db1], axis=1), dcw, dcb, dlam, d_wa, dba, d_wx, dbx, dsink[:, 0], dg2, dg3,
                  loss_row[0, 0:1]]
    full_shapes = [(1, D), (1, 2 * D), (4, D), (1, D), (2, D), (2, NH, HD, HD), (2, D), (2, NH, HD, HD), (2, D), (NH,),
                   (1, D), (1, D), (1,)]
    rows_full = _rows_for([math.prod(s) for s in full_shapes], 16)
    small_vec = _pack(small_full, rows_full)

    big = [gw_in.reshape(NCHIP, 2, D // 2, SHW), gw_ffn_in.reshape(NCHIP, 2, D // 2, SHW),
           gw_out.reshape(NCHIP, 2, D // NCHIP // 2, D), gw_ffn_out.reshape(NCHIP, 2, FF // NCHIP // 2, D)]
    *recv_a, small_sib = _sibling_halves(big, small_vec)
    c_arr = cc.reshape(1).astype(jnp.int32)
    tiles = [256, 256, 128, 352]
    names = ["w_in", "w_ffn_in", "w_out", "w_ffn_out"]
    pairs = [_pair_sum("pair_sum_" + nm, c_arr, g4, r, th) for nm, g4, r, th in zip(names, big, recv_a, tiles)]
    small_chip = _add2("pair_sum_small", small_vec, small_sib).reshape(2, rows_full // 2, 128)
    *recv_b, small_all = _exchange_chips([p[1] for p in pairs], small_chip)
    chip_arr = chip.reshape(1).astype(jnp.int32)
    halves = [_chip_sum("chip_sum_" + nm, chip_arr, p[0], r3, th) for nm, p, r3, th in zip(names, pairs, recv_b, tiles)]
    halves.append(_sum4("chip_sum_small", small_all, rows_full // 2))
    *recv_c, small_other = _share_sibling(halves)
    small_lo = jnp.where(cc == 0, halves[4], small_other)
    small_hi = jnp.where(cc == 0, small_other, halves[4])
    g_full = _unpack(jnp.concatenate([small_lo, small_hi], axis=0), full_shapes)

    out_big = {}
    for nm, w, g_own, g_recv, m, v, th in zip(names, [w_in, w_ffn_in, w_out, w_ffn_out], halves[:4], recv_c,
                                              [m_w_in, m_w_ffn_in, m_w_out, m_w_ffn_out],
                                              [v_w_in, v_w_ffn_in, v_w_out, v_w_ffn_out], tiles):
        g_, d_, m_, v_ = _adamw_halves("adamw_" + nm, c_arr, w[0], g_own, g_recv, m[0], v[0], th)
        out_big[nm] = (g_[None], d_[None], m_[None], v_[None])

    small_names = ["norm_mix_g", "b_gate", "conv_w", "conv_b", "lru_lambda", "lru_wa", "lru_ba", "lru_wx", "lru_bx", "attn_sink",
                   "norm_ffn_g", "norm_final_g"]
    sharded = {"conv_w", "lru_lambda", "lru_ba", "lru_bx"}
    small_w = [norm_mix_g, b_gate, conv_w, conv_b, lru_lambda, lru_wa, lru_ba, lru_wx, lru_bx, attn_sink, norm_ffn_g, norm_final_g]
    small_m = [m_norm_mix_g, m_b_gate, m_conv_w, m_conv_b, m_lru_lambda, m_lru_wa, m_lru_ba, m_lru_wx, m_lru_bx, m_attn_sink,
               m_norm_ffn_g, m_norm_final_g]
    small_v = [v_norm_mix_g, v_b_gate, v_conv_w, v_conv_b, v_lru_lambda, v_lru_wa, v_lru_ba, v_lru_wx, v_lru_bx, v_attn_sink,
               v_norm_ffn_g, v_norm_final_g]
    g_local = []
    for nm, g, w in zip(small_names, g_full, small_w):
        if nm in sharded:
            g = lax.dynamic_slice_in_dim(g, chip * SW, SW, axis=1)
        g_local.append(g.reshape(w.shape))
    local_shapes = [w.shape for w in small_w]
    rows_local = _rows_for([math.prod(s) for s in local_shapes], 8)
    d_s, m_s, v_s = _adamw("adamw_small", _pack(small_w, rows_local), _pack(g_local, rows_local),
                           _pack(small_m, rows_local), _pack(small_v, rows_local), rows_local)
    d_l, m_l, v_l = _unpack(d_s, local_shapes), _unpack(m_s, local_shapes), _unpack(v_s, local_shapes)
    res = {nm: (g_local[i], d_l[i], m_l[i], v_l[i]) for i, nm in enumerate(small_names)}
    res.update(out_big)

    order = ["norm_mix_g", "w_in", "b_gate", "conv_w", "conv_b", "lru_lambda", "lru_wa", "lru_ba", "lru_wx", "lru_bx", "attn_sink",
             "w_out", "norm_ffn_g", "w_ffn_in", "w_ffn_out", "norm_final_g"]
    outs = [g_full[-1][0], grad_x[None]]
    for k in range(4):
        outs += [res[nm][k] for nm in order]
    return tuple(outs)
```

```python
import functools
import math

import jax
import jax.numpy as jnp
from jax import lax
from jax.experimental import pallas as pl
from jax.experimental.pallas import tpu as pltpu

f32 = jnp.float32
bf16 = jnp.bfloat16

D = 1024
NH = 16
HD = 64
FF = 2816
INW = 5632
NCHIP = 4
SHW = INW // NCHIP
CW = 128
NCH = D // CW
BLK = 128
EPS = 1e-6
NEG_INF = -1e30
RGLRU_C = 8.0
ADAM_LR, ADAM_B1, ADAM_B2, ADAM_EPS, ADAM_WD, ADAM_STEP = 0.001, 0.9, 0.999, 1e-08, 0.01, 10
VMEM_LIMIT = 58 * 1024 * 1024
MESH = pl.DeviceIdType.MESH
ANY = pl.BlockSpec(memory_space=pl.ANY)

COL_U, COL_G, COL_Q, COL_K, COL_V, COL_Z0, COL_Z1 = 0, 4, 8, 12, 13, 14, 18


def _params(n_axes, vmem=False):
    return pltpu.CompilerParams(dimension_semantics=("arbitrary",) * n_axes,
                                vmem_limit_bytes=VMEM_LIMIT if vmem else None)


def _sds(shape, dtype):
    return jax.ShapeDtypeStruct(tuple(shape), dtype)


_DIMS = {"nn": (((1,), (0,)), ((), ())), "nt": (((1,), (1,)), ((), ())), "tn": (((0,), (0,)), ((), ()))}


def _mm(name, mode, a, a_spec, b, b_spec, out_shape, out_spec, grid, nk, acc_shape, add=None, add_spec=None):
    has_add = add is not None

    def body(*refs):
        a_ref, b_ref = refs[0], refs[1]
        add_ref = refs[2] if has_add else None
        o_ref = refs[2 + has_add]
        part = lax.dot_general(a_ref[...].astype(bf16), b_ref[...].astype(bf16), _DIMS[mode],
                               preferred_element_type=f32)
        if nk == 1:
            if has_add:
                part = add_ref[...] + part
            o_ref[...] = part.astype(o_ref.dtype)
            return
        acc_ref = refs[3 + has_add]
        k = pl.program_id(len(grid) - 1)

        @pl.when(k == 0)
        def _():
            acc_ref[...] = part

        @pl.when(k > 0)
        def _():
            acc_ref[...] += part

        @pl.when(k == nk - 1)
        def _():
            res = acc_ref[...]
            if has_add:
                res = add_ref[...] + res
            o_ref[...] = res.astype(o_ref.dtype)

    ins = [a, b] + ([add] if has_add else [])
    in_specs = [a_spec, b_spec] + ([add_spec] if has_add else [])
    scratch = [pltpu.VMEM(acc_shape, f32)] if nk > 1 else []
    return pl.pallas_call(body, name=name, grid=grid, in_specs=in_specs, out_specs=out_spec, out_shape=out_shape,
                          scratch_shapes=scratch, compiler_params=_params(len(grid), True))(*ins)


def _rms_matmul(name, x, g, w3, tm):
    S, K = x.shape
    G, _, Nw = w3.shape
    tm = min(tm, S)

    def body(x_ref, g_ref, w_ref, xn_ref, o_ref, xs_ref):
        @pl.when(pl.program_id(1) == 0)
        def _():
            xf = x_ref[...]
            r = lax.rsqrt(jnp.mean(xf * xf, axis=-1, keepdims=True) + EPS)
            xn = ((xf * r) * g_ref[...]).astype(bf16)
            xs_ref[...] = xn
            xn_ref[...] = xn

        o_ref[...] = jnp.dot(xs_ref[...], w_ref[...], preferred_element_type=f32)

    return pl.pallas_call(
        body, name=name, grid=(S // tm, G),
        in_specs=[pl.BlockSpec((tm, K), lambda i, j: (i, 0)), pl.BlockSpec((1, K), lambda i, j: (0, 0)),
                  pl.BlockSpec((None, K, Nw), lambda i, j: (j, 0, 0))],
        out_specs=[pl.BlockSpec((tm, K), lambda i, j: (i, 0)), pl.BlockSpec((tm, Nw), lambda i, j: (i, j))],
        out_shape=[_sds((S, K), bf16), _sds((S, G * Nw), f32)],
        scratch_shapes=[pltpu.VMEM((tm, K), bf16)], compiler_params=_params(2, True))(x, g, w3)


def _rms_matmul_swiglu(name, x, g, w3, tm):
    S, K = x.shape
    G, _, Nw = w3.shape
    tm = min(tm, S)
    half = G // 2

    def body(x_ref, g_ref, wg_ref, wu_ref, xn_ref, gu_ref, act_ref, xs_ref):
        @pl.when(pl.program_id(1) == 0)
        def _():
            xf = x_ref[...]
            r = lax.rsqrt(jnp.mean(xf * xf, axis=-1, keepdims=True) + EPS)
            xn = ((xf * r) * g_ref[...]).astype(bf16)
            xs_ref[...] = xn
            xn_ref[...] = xn

        xn = xs_ref[...]
        gate = jnp.dot(xn, wg_ref[...], preferred_element_type=f32)
        up = jnp.dot(xn, wu_ref[...], preferred_element_type=f32)
        gu_ref[0] = gate.astype(bf16)
        gu_ref[1] = up.astype(bf16)
        act_ref[...] = ((gate * _sigmoid(gate)) * up).astype(bf16)

    return pl.pallas_call(
        body, name=name, grid=(S // tm, half),
        in_specs=[pl.BlockSpec((tm, K), lambda i, j: (i, 0)), pl.BlockSpec((1, K), lambda i, j: (0, 0)),
                  pl.BlockSpec((None, K, Nw), lambda i, j: (j, 0, 0)),
                  pl.BlockSpec((None, K, Nw), lambda i, j: (half + j, 0, 0))],
        out_specs=[pl.BlockSpec((tm, K), lambda i, j: (i, 0)), pl.BlockSpec((2, tm, Nw), lambda i, j: (0, i, j)),
                   pl.BlockSpec((tm, Nw), lambda i, j: (i, j))],
        out_shape=[_sds((S, K), bf16), _sds((2, S, half * Nw), bf16), _sds((S, half * Nw), bf16)],
        scratch_shapes=[pltpu.VMEM((tm, K), bf16)], compiler_params=_params(2, True))(x, g, w3, w3)


def _mm_residual(name, a, w, res, tm):
    S, K = a.shape
    N = w.shape[1]
    tm = min(tm, S)
    return _mm(name, "nn", a, pl.BlockSpec((tm, K), lambda i: (i, 0)), w, pl.BlockSpec((K, N), lambda i: (0, 0)),
               _sds((S, N), f32), pl.BlockSpec((tm, N), lambda i: (i, 0)), (S // tm,), 1, None,
               add=res, add_spec=pl.BlockSpec((tm, N), lambda i: (i, 0)))


def _mm_nt_resident(name, a, w, tm):
    S, K = a.shape
    N = w.shape[0]
    tm = min(tm, S)
    return _mm(name, "nt", a, pl.BlockSpec((tm, K), lambda i: (i, 0)), w, pl.BlockSpec((N, K), lambda i: (0, 0)),
               _sds((S, N), f32), pl.BlockSpec((tm, N), lambda i: (i, 0)), (S // tm,), 1, None)


def _mm_nt_groups(name, a, a_spec, w3, S, tm):
    G, Dout, Kw = w3.shape
    return _mm(name, "nt", a, a_spec, w3, pl.BlockSpec((None, Dout, Kw), lambda i, g: (g, 0, 0)),
               _sds((S, Dout), f32), pl.BlockSpec((tm, Dout), lambda i, g: (i, 0)), (S // tm, G), G, (tm, Dout))


def _mm_tn(name, a, a_spec, b, b_spec, out_shape, out_spec, grid, acc_shape):
    return _mm(name, "tn", a, a_spec, b, b_spec, out_shape, out_spec, grid, grid[-1], acc_shape)


def _sigmoid(x):
    return 0.5 * jnp.tanh(0.5 * x) + 0.5


_GELU_C = math.sqrt(2.0 / math.pi)


def _gelu_and_grad(x):
    v = _GELU_C * (x + 0.044715 * (x * x * x))
    t = jnp.tanh(v)
    gl = 0.5 * x * (1.0 + t)
    dgl = 0.5 * (1.0 + t) + 0.5 * x * (1.0 - t * t) * (_GELU_C * (1.0 + 3.0 * 0.044715 * (x * x)))
    return gl, dgl


def _one_minus_exp2x(x, ex):
    y = 2.0 * x
    series = y * (1.0 + y * (0.5 + y * (1.0 / 6.0 + y * (1.0 / 24.0 + y * (1.0 / 120.0 + y * (1.0 / 720.0))))))
    return jnp.where(y > -0.125, -series, 1.0 - ex * ex)


def _merge_fwd(proj, b_gate, y_a, y_b, tm):
    S = proj.shape[0]
    tm = min(tm, S)

    def body(z0_ref, z1_ref, b0_ref, b1_ref, ya_ref, yb_ref, o_ref):
        g0 = _sigmoid(z0_ref[...] + b0_ref[...])
        g1 = _sigmoid(z1_ref[...] + b1_ref[...])
        o_ref[...] = (g0 * ya_ref[...] + g1 * yb_ref[...]).astype(bf16)

    blk = lambda off: pl.BlockSpec((tm, 256), lambda j, i: (i, off + j))
    vec = lambda off: pl.BlockSpec((1, 256), lambda j, i: (0, off + j))
    return pl.pallas_call(body, name="merge_fwd", grid=(4, S // tm),
                          in_specs=[blk(COL_Z0), blk(COL_Z1), vec(0), vec(4), blk(0), blk(0)],
                          out_specs=blk(0), out_shape=_sds((S, D), bf16),
                          compiler_params=_params(2))(proj, proj, b_gate, b_gate, y_a, y_b)


def _merge_bwd(proj, b_gate, y_a, y_b, dm, tm):
    S = proj.shape[0]
    tm = min(tm, S)

    def body(z0_ref, z1_ref, b0_ref, b1_ref, ya_ref, yb_ref, dm_ref, dz0_ref, dz1_ref, dya_ref, dyb_ref, db0_ref, db1_ref):
        g0 = _sigmoid(z0_ref[...] + b0_ref[...])
        g1 = _sigmoid(z1_ref[...] + b1_ref[...])
        d = dm_ref[...]
        dz0 = (d * ya_ref[...]) * (g0 * (1.0 - g0))
        dz1 = (d * yb_ref[...]) * (g1 * (1.0 - g1))
        dz0_ref[...] = dz0.astype(bf16)
        dz1_ref[...] = dz1.astype(bf16)
        dya_ref[...] = d * g0
        dyb_ref[...] = d * g1

        @pl.when(pl.program_id(1) == 0)
        def _():
            db0_ref[...] = jnp.zeros_like(db0_ref)
            db1_ref[...] = jnp.zeros_like(db1_ref)

        db0_ref[...] += jnp.sum(dz0, axis=0, keepdims=True)
        db1_ref[...] += jnp.sum(dz1, axis=0, keepdims=True)

    blk = lambda off: pl.BlockSpec((tm, 256), lambda j, i: (i, off + j))
    vec = lambda off: pl.BlockSpec((1, 256), lambda j, i: (0, off + j))
    return pl.pallas_call(
        body, name="merge_bwd", grid=(4, S // tm),
        in_specs=[blk(COL_Z0), blk(COL_Z1), vec(0), vec(4), blk(0), blk(0), blk(0)],
        out_specs=[blk(0), blk(0), blk(0), blk(0), vec(0), vec(0)],
        out_shape=[_sds((S, D), bf16), _sds((S, D), bf16), _sds((S, D), f32), _sds((S, D), f32),
                   _sds((1, D), f32), _sds((1, D), f32)],
        compiler_params=_params(2))(proj, proj, b_gate, b_gate, y_a, y_b, dm)


def _swiglu_bwd(dx, w, gu, tm):
    S, K = dx.shape
    tm = min(tm, S)

    def body(dx_ref, w_ref, gu_ref, o_ref):
        d = lax.dot_general(dx_ref[...].astype(bf16), w_ref[...], _DIMS["nt"], preferred_element_type=f32)
        g = gu_ref[0].astype(f32)
        u = gu_ref[1].astype(f32)
        s = _sigmoid(g)
        o_ref[0] = ((d * u) * (s * (1.0 + g * (1.0 - s)))).astype(bf16)
        o_ref[1] = (d * (g * s)).astype(bf16)

    stacked = pl.BlockSpec((2, tm, FF), lambda i: (0, i, 0))
    return pl.pallas_call(body, name="swiglu_bwd", grid=(S // tm,),
                          in_specs=[pl.BlockSpec((tm, K), lambda i: (i, 0)), pl.BlockSpec((FF, K), lambda i: (0, 0)), stacked],
                          out_specs=stacked, out_shape=_sds((2, S, FF), bf16),
                          compiler_params=_params(1, True))(dx, w, gu)


def _final_loss_bwd(x2, g3, tgt, tm):
    S = x2.shape[0]
    tm = min(tm, S)

    def body(x_ref, g_ref, t_ref, dx_ref, loss_ref, dg_ref):
        @pl.when(pl.program_id(0) == 0)
        def _():
            loss_ref[...] = jnp.zeros_like(loss_ref)
            dg_ref[...] = jnp.zeros_like(dg_ref)

        x = x_ref[...]
        g = g_ref[...]
        r = lax.rsqrt(jnp.mean(x * x, axis=-1, keepdims=True) + EPS)
        xh = x * r
        err = xh * g - t_ref[...]
        row = jnp.mean(err * err, axis=-1, keepdims=True)
        loss_ref[...] += 0.5 * jnp.sum(row, axis=0, keepdims=True)
        dy = err * (1.0 / D)
        dg_ref[...] += jnp.sum(dy * xh, axis=0, keepdims=True)
        dxh = dy * g
        dx_ref[...] = r * (dxh - xh * jnp.mean(dxh * xh, axis=-1, keepdims=True))

    row_blk = pl.BlockSpec((tm, D), lambda i: (i, 0))
    vec = pl.BlockSpec((1, D), lambda i: (0, 0))
    return pl.pallas_call(body, name="final_loss_bwd", grid=(S // tm,), in_specs=[row_blk, vec, row_blk],
                          out_specs=[row_blk, pl.BlockSpec((1, 128), lambda i: (0, 0)), vec],
                          out_shape=[_sds((S, D), f32), _sds((1, 128), f32), _sds((1, D), f32)],
                          compiler_params=_params(1))(x2, g3, tgt)


def _rms_bwd(name, x, g, dxn, dres, tm):
    S = x.shape[0]
    tm = min(tm, S)

    def body(x_ref, g_ref, d_ref, r_ref, dx_ref, dg_ref):
        @pl.when(pl.program_id(0) == 0)
        def _():
            dg_ref[...] = jnp.zeros_like(dg_ref)

        x = x_ref[...]
        d = d_ref[...]
        r = lax.rsqrt(jnp.mean(x * x, axis=-1, keepdims=True) + EPS)
        xh = x * r
        dg_ref[...] += jnp.sum(d * xh, axis=0, keepdims=True)
        dxh = d * g_ref[...]
        dx_ref[...] = r_ref[...] + r * (dxh - xh * jnp.mean(dxh * xh, axis=-1, keepdims=True))

    row_blk = pl.BlockSpec((tm, D), lambda i: (i, 0))
    vec = pl.BlockSpec((1, D), lambda i: (0, 0))
    return pl.pallas_call(body, name=name, grid=(S // tm,), in_specs=[row_blk, vec, row_blk, row_blk],
                          out_specs=[row_blk, vec], out_shape=[_sds((S, D), f32), _sds((1, D), f32)],
                          compiler_params=_params(1))(x, g, dxn, dres)


LRU_TT = 256
SCAN_UNROLL = 4


def _halo(ref, i, S):
    nt = S // LRU_TT
    t0 = pl.multiple_of(i * LRU_TT, LRU_TT)
    p0 = pl.multiple_of(jnp.maximum(t0 - 8, 0), 8)
    n0 = pl.multiple_of(jnp.minimum(t0 + LRU_TT, S - 8), 8)
    prev = jnp.where(i > 0, ref[pl.ds(p0, 8), :], 0.0)
    nxt = jnp.where(i < nt - 1, ref[pl.ds(n0, 8), :], 0.0)
    return jnp.concatenate([prev, ref[pl.ds(t0, LRU_TT), :], nxt], axis=0)


def _shift(ext, k):
    n = LRU_TT + 16
    return pltpu.roll(ext, (-k) % n, 0)[8:8 + LRU_TT]


def _lru_gates(uc, wbd, ba, bx):
    pre = jnp.dot(uc.astype(bf16), wbd, preferred_element_type=f32)
    r_f = _sigmoid(pre[:, 0:CW] + ba[0:1])
    i_f = _sigmoid(pre[:, CW:2 * CW] + bx[0:1])
    r_b = _sigmoid(pre[:, 2 * CW:3 * CW] + ba[1:2])
    i_b = _sigmoid(pre[:, 3 * CW:4 * CW] + bx[1:2])
    return r_f, i_f, r_b, i_b


def _lru_coeffs(r, sp):
    log_a = (-RGLRU_C * r) * sp
    a = jnp.exp(log_a)
    beta = jnp.sqrt(jnp.maximum(_one_minus_exp2x(log_a, a), 0.0))
    return a, beta


def _lru_coeffs_inv(r, sp):
    log_a = (-RGLRU_C * r) * sp
    a = jnp.exp(log_a)
    om = jnp.maximum(_one_minus_exp2x(log_a, a), 0.0)
    return a, jnp.sqrt(om), lax.rsqrt(om)


def _conv_tile(u_ref, i, S, cw, cb):
    ext = _halo(u_ref, i, S)
    um2, um1, u0, up1 = _shift(ext, -2), _shift(ext, -1), ext[8:8 + LRU_TT], _shift(ext, 1)
    uc = um2 * cw[0:1] + um1 * cw[1:2] + u0 * cw[2:3] + up1 * cw[3:4] + cb
    return uc, (um2, um1, u0, up1)


def _scan_pair(S, fwd_a, fwd_b, fwd_out, rev_a, rev_b, rev_out):
    ng = S // 8
    idx = lax.broadcasted_iota(jnp.int32, (8, CW), 0)

    def local(a, b, rev):
        for sh in (1, 2, 4):
            if rev:
                keep = idx < 8 - sh
                amt = 8 - sh
            else:
                keep = idx >= sh
                amt = sh
            a_s = jnp.where(keep, pltpu.roll(a, amt, 0), 1.0)
            b_s = jnp.where(keep, pltpu.roll(b, amt, 0), 0.0)
            b = a * b_s + b
            a = a * a_s
        return a, b

    def step(it, carry):
        cf, cr = carry
        fwd_rows = [pl.multiple_of((it * SCAN_UNROLL + j) * 8, 8) for j in range(SCAN_UNROLL)]
        rev_rows = [pl.multiple_of((ng - 1 - (it * SCAN_UNROLL + j)) * 8, 8) for j in range(SCAN_UNROLL)]
        fwd_loc = [local(fwd_a(r), fwd_b(r), False) for r in fwd_rows]
        rev_loc = [local(rev_a(r), rev_b(r), True) for r in rev_rows]
        for j in range(SCAN_UNROLL):
            a, b = fwd_loc[j]
            h = a * cf + b
            fwd_out[pl.ds(fwd_rows[j], 8), :] = h
            cf = jnp.broadcast_to(h[7:8, :], (8, CW))
            a, b = rev_loc[j]
            h = a * cr + b
            rev_out[pl.ds(rev_rows[j], 8), :] = h
            cr = jnp.broadcast_to(h[0:1, :], (8, CW))
        return cf, cr

    zero = jnp.zeros((8, CW), f32)
    lax.fori_loop(0, ng // SCAN_UNROLL, step, (zero, zero))


def _lru_specs(S):
    seq = lambda off: pl.BlockSpec((S, CW), lambda j: (0, off + j))
    par = lambda rows: pl.BlockSpec((rows, CW), lambda j: (0, j))
    return seq, par


def _lru_fwd(proj, conv_w, conv_b, lam, ba, bx, wbd):
    S = proj.shape[0]
    nt = S // LRU_TT

    def body(u_ref, g_ref, cw_ref, cb_ref, lam_ref, ba_ref, bx_ref, wbd_ref, y_ref, af_ref, bf_ref, ab_ref, bb_ref):
        cw, cb, ba_v, bx_v, wbd_v = cw_ref[...], cb_ref[...], ba_ref[...], bx_ref[...], wbd_ref[...]
        sp = jax.nn.softplus(-lam_ref[...])

        def phase1(i, c):
            uc, _ = _conv_tile(u_ref, i, S, cw, cb)
            r_f, i_f, r_b, i_b = _lru_gates(uc, wbd_v, ba_v, bx_v)
            rows = pl.ds(pl.multiple_of(i * LRU_TT, LRU_TT), LRU_TT)
            a, beta = _lru_coeffs(r_f, sp[0:1])
            af_ref[rows, :] = a
            bf_ref[rows, :] = beta * (i_f * uc)
            a, beta = _lru_coeffs(r_b, sp[1:2])
            ab_ref[rows, :] = a
            bb_ref[rows, :] = beta * (i_b * uc)
            return c

        lax.fori_loop(0, nt, phase1, 0)
        row8 = lambda ref: (lambda r0: ref[pl.ds(r0, 8), :])
        _scan_pair(S, row8(af_ref), row8(bf_ref), bf_ref, row8(ab_ref), row8(bb_ref), bb_ref)

        def phase3(i, c):
            rows = pl.ds(pl.multiple_of(i * LRU_TT, LRU_TT), LRU_TT)
            y_ref[rows, :] = (bf_ref[rows, :] + bb_ref[rows, :]) * jax.nn.gelu(g_ref[rows, :])
            return c

        lax.fori_loop(0, nt, phase3, 0)

    seq, par = _lru_specs(S)
    return pl.pallas_call(
        body, name="lru_fwd", grid=(NCH,),
        in_specs=[seq(0), seq(NCH), par(4), par(1), par(2), par(2), par(2),
                  pl.BlockSpec((None, CW, 4 * CW), lambda j: (j, 0, 0))],
        out_specs=seq(0), out_shape=_sds((S, D), f32),
        scratch_shapes=[pltpu.VMEM((S, CW), f32)] * 4, compiler_params=_params(1, True),
    )(proj, proj, conv_w, conv_b, lam, ba, bx, wbd)


def _lru_bwd(proj, dy, conv_w, conv_b, lam, ba, bx, wbd):
    S = proj.shape[0]
    nt = S // LRU_TT

    def body(u_ref, g_ref, dy_ref, cw_ref, cb_ref, lam_ref, ba_ref, bx_ref, wbd_ref,
             du_ref, dg_ref, dcw_ref, dcb_ref, dlam_ref, dba_ref, dbx_ref, dwbd_ref,
             af_ref, bf_ref, ab_ref, bb_ref):
        cw, cb, ba_v, bx_v, wbd_v = cw_ref[...], cb_ref[...], ba_ref[...], bx_ref[...], wbd_ref[...]
        lam_v = lam_ref[...]
        sp = jax.nn.softplus(-lam_v)

        def phase1(i, c):
            uc, _ = _conv_tile(u_ref, i, S, cw, cb)
            r_f, i_f, r_b, i_b = _lru_gates(uc, wbd_v, ba_v, bx_v)
            rows = pl.ds(pl.multiple_of(i * LRU_TT, LRU_TT), LRU_TT)
            a, beta = _lru_coeffs(r_f, sp[0:1])
            af_ref[rows, :] = a
            bf_ref[rows, :] = beta * (i_f * uc)
            a, beta = _lru_coeffs(r_b, sp[1:2])
            ab_ref[rows, :] = a
            bb_ref[rows, :] = beta * (i_b * uc)
            return c

        lax.fori_loop(0, nt, phase1, 0)
        row8 = lambda ref: (lambda r0: ref[pl.ds(r0, 8), :])
        _scan_pair(S, row8(af_ref), row8(bf_ref), bf_ref, row8(ab_ref), row8(bb_ref), bb_ref)

        def scaled_dh(a_ref):
            def f(r0):
                gl, _ = _gelu_and_grad(g_ref[pl.ds(r0, 8), :])
                return a_ref[pl.ds(r0, 8), :] * (dy_ref[pl.ds(r0, 8), :] * gl)
            return f

        _scan_pair(S, row8(ab_ref), scaled_dh(ab_ref), ab_ref, row8(af_ref), scaled_dh(af_ref), af_ref)

        dcw_ref[...] = jnp.zeros_like(dcw_ref)
        dcb_ref[...] = jnp.zeros_like(dcb_ref)
        dlam_ref[...] = jnp.zeros_like(dlam_ref)
        dba_ref[...] = jnp.zeros_like(dba_ref)
        dbx_ref[...] = jnp.zeros_like(dbx_ref)
        dwbd_ref[...] = jnp.zeros_like(dwbd_ref)

        def direction(uc, r, i_g, dht, h_nb, sp_d):
            a, beta, inv_beta = _lru_coeffs_inv(r, sp_d)
            da = dht * h_nb
            dbeta = dht * (i_g * uc)
            d_iu = dht * beta
            dlog_a = da * a - (a * a) * (dbeta * inv_beta)
            dr = dlog_a * (-RGLRU_C * sp_d)
            dsp = jnp.sum(dlog_a * (-RGLRU_C * r), axis=0, keepdims=True)
            dpre_r = dr * (r * (1.0 - r))
            dpre_i = (d_iu * uc) * (i_g * (1.0 - i_g))
            return dpre_r, dpre_i, d_iu * i_g, dsp

        def phase4(i, c):
            uc, (um2, um1, u0, up1) = _conv_tile(u_ref, i, S, cw, cb)
            r_f, i_f, r_b, i_b = _lru_gates(uc, wbd_v, ba_v, bx_v)
            rows = pl.ds(pl.multiple_of(i * LRU_TT, LRU_TT), LRU_TT)
            gl, dgl = _gelu_and_grad(g_ref[rows, :])
            dyt = dy_ref[rows, :]
            dh = dyt * gl
            dg_ref[rows, :] = ((dyt * (bf_ref[rows, :] + bb_ref[rows, :])) * dgl).astype(dg_ref.dtype)
            dht_f = dh + _shift(_halo(af_ref, i, S), 1)
            h_prev = _shift(_halo(bf_ref, i, S), -1)
            dht_b = dh + _shift(_halo(ab_ref, i, S), -1)
            h_next = _shift(_halo(bb_ref, i, S), 1)
            prf, pif, duc_f, dsp_f = direction(uc, r_f, i_f, dht_f, h_prev, sp[0:1])
            prb, pib, duc_b, dsp_b = direction(uc, r_b, i_b, dht_b, h_next, sp[1:2])
            dpre = jnp.concatenate([prf, pif, prb, pib], axis=1)
            dpre_b = dpre.astype(bf16)
            duc = (duc_f + duc_b) + lax.dot_general(dpre_b, wbd_v, _DIMS["nt"], preferred_element_type=f32)
            dwbd_ref[...] += lax.dot_general(uc.astype(bf16), dpre_b, _DIMS["tn"], preferred_element_type=f32)
            colsum = lambda v: jnp.sum(v, axis=0, keepdims=True)
            dba_ref[...] += jnp.concatenate([colsum(prf), colsum(prb)], axis=0)
            dbx_ref[...] += jnp.concatenate([colsum(pif), colsum(pib)], axis=0)
            dlam_ref[...] += jnp.concatenate([dsp_f, dsp_b], axis=0)
            dcb_ref[...] += colsum(duc)
            dcw_ref[...] += jnp.concatenate([colsum(duc * um2), colsum(duc * um1), colsum(duc * u0),
                                             colsum(duc * up1)], axis=0)
            af_ref[rows, :] = duc
            return c

        lax.fori_loop(0, nt, phase4, 0)
        dlam_ref[...] = dlam_ref[...] * (-_sigmoid(-lam_v))

        def phase5(i, c):
            ext = _halo(af_ref, i, S)
            rows = pl.ds(pl.multiple_of(i * LRU_TT, LRU_TT), LRU_TT)
            du = (_shift(ext, 2) * cw[0:1] + _shift(ext, 1) * cw[1:2] + ext[8:8 + LRU_TT] * cw[2:3]
                  + _shift(ext, -1) * cw[3:4])
            du_ref[rows, :] = du.astype(du_ref.dtype)
            return c

        lax.fori_loop(0, nt, phase5, 0)

    seq, par = _lru_specs(S)
    return pl.pallas_call(
        body, name="lru_bwd", grid=(NCH,),
        in_specs=[seq(0), seq(NCH), seq(0), par(4), par(1), par(2), par(2), par(2),
                  pl.BlockSpec((None, CW, 4 * CW), lambda j: (j, 0, 0))],
        out_specs=[seq(0), seq(0), par(4), par(1), par(2), par(2), par(2),
                   pl.BlockSpec((None, CW, 4 * CW), lambda j: (j, 0, 0))],
        out_shape=[_sds((S, D), bf16), _sds((S, D), bf16), _sds((4, D), f32), _sds((1, D), f32), _sds((2, D), f32),
                   _sds((2, D), f32), _sds((2, D), f32), _sds((NCH, CW, 4 * CW), f32)],
        scratch_shapes=[pltpu.VMEM((S, CW), f32)] * 4, compiler_params=_params(1, True),
    )(proj, proj, dy, conv_w, conv_b, lam, ba, bx, wbd)


_SLOPES = [2.0 ** (-8.0 * (h + 1) / NH) for h in range(NH)]


def _half_mask(shape, e):
    lane = lax.broadcasted_iota(jnp.int32, shape, 1)
    return (lane < HD) if e == 0 else (lane >= HD)


def _both_halves(x, src):
    return jnp.where(_half_mask(x.shape, src), x, pltpu.roll(x, HD, 1))


def _fold_halves(x, dst):
    return jnp.where(_half_mask(x.shape, dst), x + pltpu.roll(x, HD, 1), 0.0)


def _attn_base(n, S):
    tq = lax.broadcasted_iota(jnp.int32, (BLK, 3 * BLK), 0)
    sk = lax.broadcasted_iota(jnp.int32, (BLK, 3 * BLK), 1)
    dist = jnp.abs(tq + BLK - sk)
    kpos = n * BLK - BLK + sk
    valid = (dist <= BLK) & (kpos >= 0) & (kpos < S)
    return jnp.where(valid, -dist.astype(f32), NEG_INF)


def _group_heads(ref, kvh, scale):
    parts = []
    for i in range(4):
        pair = 2 * kvh + i // 2
        x = ref[:, pair * 128:(pair + 1) * 128]
        parts.append(jnp.where(_half_mask(x.shape, i % 2), x * scale, 0.0))
    return parts


def _stack_bf16(parts):
    return jnp.concatenate([p.astype(bf16) for p in parts], axis=0)


def _attn_softmax(s_raw, base, slope, sink):
    s = s_raw + slope * base
    m = jnp.maximum(jnp.max(s, axis=-1, keepdims=True), sink)
    p = jnp.exp(s - m)
    esink = jnp.exp(sink - m)
    inv = 1.0 / (jnp.sum(p, axis=-1, keepdims=True) + esink)
    return p, inv, esink * inv


def _attn_specs(S):
    nb = S // BLK
    q_spec = pl.BlockSpec((BLK, D), lambda n: (n, 2))
    kv = lambda col: [pl.BlockSpec((BLK, 256), lambda n: (jnp.maximum(n - 1, 0), col)),
                      pl.BlockSpec((BLK, 256), lambda n: (n, col)),
                      pl.BlockSpec((BLK, 256), lambda n: (jnp.minimum(n + 1, nb - 1), col))]
    return nb, q_spec, kv(COL_K), kv(COL_V)


def _attn_fwd(proj, sink):
    S = proj.shape[0]
    nb, q_spec, k_specs, v_specs = _attn_specs(S)

    def body(sink_ref, q_ref, kp_ref, kc_ref, kn_ref, vp_ref, vc_ref, vn_ref, o_ref):
        base = _attn_base(pl.program_id(0), S)
        kcat = jnp.concatenate([kp_ref[...], kc_ref[...], kn_ref[...]], axis=0)
        vcat = jnp.concatenate([vp_ref[...], vc_ref[...], vn_ref[...]], axis=0)
        even = _half_mask((BLK, 128), 0)
        for kvh in range(NH // 4):
            ch, off = kvh // 2, kvh % 2
            kb = _both_halves(kcat[:, ch * 128:(ch + 1) * 128], off).astype(bf16)
            vb = _both_halves(vcat[:, ch * 128:(ch + 1) * 128], off).astype(bf16)
            q4 = _stack_bf16(_group_heads(q_ref, kvh, HD ** -0.5))
            s4 = lax.dot_general(q4, kb, _DIMS["nt"], preferred_element_type=f32)
            ps, invs = [], []
            for i in range(4):
                h = 4 * kvh + i
                p, inv, _ = _attn_softmax(s4[i * BLK:(i + 1) * BLK], base, _SLOPES[h], sink_ref[0, h])
                ps.append(p)
                invs.append(inv)
            o4 = jnp.dot(_stack_bf16(ps), vb, preferred_element_type=f32)
            for pr in range(2):
                lo = o4[(2 * pr) * BLK:(2 * pr + 1) * BLK] * invs[2 * pr]
                hi = o4[(2 * pr + 1) * BLK:(2 * pr + 2) * BLK] * invs[2 * pr + 1]
                pair = 2 * kvh + pr
                o_ref[:, pair * 128:(pair + 1) * 128] = jnp.where(even, lo, hi)

    return pl.pallas_call(
        body, name="attn_fwd", grid=(nb,),
        in_specs=[pl.BlockSpec(memory_space=pltpu.SMEM), q_spec] + k_specs + v_specs,
        out_specs=pl.BlockSpec((BLK, D), lambda n: (n, 0)), out_shape=_sds((S, D), f32),
        compiler_params=_params(1, True))(sink, proj, proj, proj, proj, proj, proj, proj)


def _attn_bwd(proj, sink, y_b, dy_b):
    S = proj.shape[0]
    nb, q_spec, k_specs, v_specs = _attn_specs(S)

    def body(sink_ref, q_ref, kp_ref, kc_ref, kn_ref, vp_ref, vc_ref, vn_ref, o_ref, do_ref,
             dq_ref, dk_ref, dv_ref, dsink_ref):
        n = pl.program_id(0)

        @pl.when(n == 0)
        def _():
            dk_ref[...] = jnp.zeros_like(dk_ref)
            dv_ref[...] = jnp.zeros_like(dv_ref)
            dsink_ref[...] = jnp.zeros_like(dsink_ref)

        base = _attn_base(n, S)
        kcat = jnp.concatenate([kp_ref[...], kc_ref[...], kn_ref[...]], axis=0)
        vcat = jnp.concatenate([vp_ref[...], vc_ref[...], vn_ref[...]], axis=0)
        dk_acc = [jnp.zeros((3 * BLK, 128), f32), jnp.zeros((3 * BLK, 128), f32)]
        dv_acc = [jnp.zeros((3 * BLK, 128), f32), jnp.zeros((3 * BLK, 128), f32)]
        scale = HD ** -0.5
        even = _half_mask((BLK, 128), 0)
        for kvh in range(NH // 4):
            ch, off = kvh // 2, kvh % 2
            kb = _both_halves(kcat[:, ch * 128:(ch + 1) * 128], off).astype(bf16)
            vb = _both_halves(vcat[:, ch * 128:(ch + 1) * 128], off).astype(bf16)
            q_parts = _group_heads(q_ref, kvh, scale)
            d_parts = _group_heads(do_ref, kvh, 1.0)
            s4 = lax.dot_general(_stack_bf16(q_parts), kb, _DIMS["nt"], preferred_element_type=f32)
            dp4 = lax.dot_general(_stack_bf16(d_parts), vb, _DIMS["nt"], preferred_element_type=f32)
            ts, ps, qn, dn, invs = [], [], [], [], []
            for i in range(4):
                h = 4 * kvh + i
                pair = 2 * kvh + i // 2
                rows = slice(i * BLK, (i + 1) * BLK)
                p, inv, psink = _attn_softmax(s4[rows], base, _SLOPES[h], sink_ref[0, h])
                delta = jnp.sum(d_parts[i] * o_ref[:, pair * 128:(pair + 1) * 128], axis=-1, keepdims=True)
                dsink_ref[h:h + 1, :] += jnp.broadcast_to(-jnp.sum(psink * delta, axis=0, keepdims=True), (1, 128))
                ts.append(p * (dp4[rows] - delta))
                ps.append(p)
                qn.append(q_parts[i] * inv)
                dn.append(d_parts[i] * inv)
                invs.append(inv)
            t4 = _stack_bf16(ts)
            dq4 = jnp.dot(t4, kb, preferred_element_type=f32)
            for pr in range(2):
                lo = dq4[(2 * pr) * BLK:(2 * pr + 1) * BLK] * invs[2 * pr]
                hi = dq4[(2 * pr + 1) * BLK:(2 * pr + 2) * BLK] * invs[2 * pr + 1]
                pair = 2 * kvh + pr
                dq_ref[:, pair * 128:(pair + 1) * 128] = (jnp.where(even, lo, hi) * scale).astype(dq_ref.dtype)
            dk_both = lax.dot_general(t4, _stack_bf16(qn), _DIMS["tn"], preferred_element_type=f32)
            dv_both = lax.dot_general(_stack_bf16(ps), _stack_bf16(dn), _DIMS["tn"], preferred_element_type=f32)
            dk_acc[ch] = dk_acc[ch] + _fold_halves(dk_both, off)
            dv_acc[ch] = dv_acc[ch] + _fold_halves(dv_both, off)
        for j in range(3):
            blk = n + (j - 1)

            @pl.when((blk >= 0) & (blk < nb))
            def _():
                rows = pl.ds(pl.multiple_of(blk * BLK, BLK), BLK)
                for ch in range(2):
                    dk_ref[rows, ch * 128:(ch + 1) * 128] += dk_acc[ch][j * BLK:(j + 1) * BLK]
                    dv_ref[rows, ch * 128:(ch + 1) * 128] += dv_acc[ch][j * BLK:(j + 1) * BLK]

    row_blk = pl.BlockSpec((BLK, D), lambda n: (n, 0))
    full = pl.BlockSpec((S, 256), lambda n: (0, 0))
    return pl.pallas_call(
        body, name="attn_bwd", grid=(nb,),
        in_specs=[pl.BlockSpec(memory_space=pltpu.SMEM), q_spec] + k_specs + v_specs + [row_blk, row_blk],
        out_specs=[row_blk, full, full, pl.BlockSpec((NH, 128), lambda n: (0, 0))],
        out_shape=[_sds((S, D), bf16), _sds((S, 256), f32), _sds((S, 256), f32), _sds((NH, 128), f32)],
        compiler_params=_params(1, True))(sink, proj, proj, proj, proj, proj, proj, proj, y_b, dy_b)


def _adamw(name, w, g, m, v, tr):
    R, C = w.shape
    tr = min(tr, R)

    def body(w_ref, g_ref, m_ref, v_ref, d_ref, m2_ref, v2_ref):
        g = g_ref[...]
        m2 = ADAM_B1 * m_ref[...] + (1.0 - ADAM_B1) * g
        v2 = ADAM_B2 * v_ref[...] + (1.0 - ADAM_B2) * (g * g)
        m_hat = m2 / (1.0 - ADAM_B1 ** ADAM_STEP)
        v_hat = v2 / (1.0 - ADAM_B2 ** ADAM_STEP)
        d_ref[...] = -ADAM_LR * (m_hat / (jnp.sqrt(v_hat) + ADAM_EPS) + ADAM_WD * w_ref[...])
        m2_ref[...] = m2
        v2_ref[...] = v2

    blk = pl.BlockSpec((tr, C), lambda i: (i, 0))
    return pl.pallas_call(body, name=name, grid=(R // tr,), in_specs=[blk] * 4, out_specs=[blk] * 3,
                          out_shape=[_sds((R, C), f32)] * 3, compiler_params=_params(1))(w, g, m, v)


def _pair_sum(name, c_arr, g4, recv, th):
    _, _, h, w = g4.shape
    th = min(th, h)

    def body(c_ref, g_ref, r_ref, o_ref, ob_ref):
        p = g_ref[...] + r_ref[...]
        o_ref[...] = p
        ob_ref[...] = p.astype(bf16)

    blk = pl.BlockSpec((None, th, w), lambda s, i, c_ref: (s, i, 0))
    spec = pltpu.PrefetchScalarGridSpec(
        num_scalar_prefetch=1, grid=(NCHIP, h // th),
        in_specs=[pl.BlockSpec((None, None, th, w), lambda s, i, c_ref: (s, c_ref[0], i, 0)), blk],
        out_specs=[blk, blk])
    return pl.pallas_call(body, name=name, grid_spec=spec,
                          out_shape=[_sds((NCHIP, h, w), f32), _sds((NCHIP, h, w), bf16)],
                          compiler_params=_params(2))(c_arr, g4, recv)


def _chip_sum(name, chip_arr, own4, recv3, th):
    _, h, w = own4.shape
    th = min(th, h)

    def body(s_ref, o_ref, r_ref, out_ref):
        out_ref[...] = ((o_ref[...] + r_ref[0].astype(f32)) + r_ref[1].astype(f32)) + r_ref[2].astype(f32)

    spec = pltpu.PrefetchScalarGridSpec(
        num_scalar_prefetch=1, grid=(h // th,),
        in_specs=[pl.BlockSpec((None, th, w), lambda i, s_ref: (s_ref[0], i, 0)),
                  pl.BlockSpec((3, th, w), lambda i, s_ref: (0, i, 0))],
        out_specs=pl.BlockSpec((th, w), lambda i, s_ref: (i, 0)))
    return pl.pallas_call(body, name=name, grid_spec=spec, out_shape=_sds((h, w), f32),
                          compiler_params=_params(1, True))(chip_arr, own4, recv3)


def _adamw_halves(name, c_arr, w, g_own, g_recv, m, v, th):
    h, wd = g_own.shape
    th = min(th, h)

    def body(c_ref, w_ref, go_ref, gr_ref, m_ref, v_ref, g_ref, d_ref, m2_ref, v2_ref):
        g = jnp.where(c_ref[0] == pl.program_id(0), go_ref[...], gr_ref[...])
        m2 = ADAM_B1 * m_ref[...] + (1.0 - ADAM_B1) * g
        v2 = ADAM_B2 * v_ref[...] + (1.0 - ADAM_B2) * (g * g)
        m_hat = m2 / (1.0 - ADAM_B1 ** ADAM_STEP)
        v_hat = v2 / (1.0 - ADAM_B2 ** ADAM_STEP)
        g_ref[...] = g
        d_ref[...] = -ADAM_LR * (m_hat / (jnp.sqrt(v_hat) + ADAM_EPS) + ADAM_WD * w_ref[...])
        m2_ref[...] = m2
        v2_ref[...] = v2

    nt = h // th
    full = pl.BlockSpec((th, wd), lambda hh, i, c_ref: (hh * nt + i, 0))
    half = pl.BlockSpec((th, wd), lambda hh, i, c_ref: (i, 0))
    spec = pltpu.PrefetchScalarGridSpec(num_scalar_prefetch=1, grid=(2, nt),
                                        in_specs=[full, half, half, full, full], out_specs=[full] * 4)
    return pl.pallas_call(body, name=name, grid_spec=spec, out_shape=[_sds((2 * h, wd), f32)] * 4,
                          compiler_params=_params(2))(c_arr, w, g_own, g_recv, m, v)


def _add2(name, a, b):
    def body(a_ref, b_ref, o_ref):
        o_ref[...] = a_ref[...] + b_ref[...]
    return pl.pallas_call(body, name=name, out_shape=_sds(a.shape, f32))(a, b)


def _sum4(name, b4, th):
    _, h, w = b4.shape
    th = min(th, h)

    def body(b_ref, o_ref):
        o_ref[...] = ((b_ref[0] + b_ref[1]) + b_ref[2]) + b_ref[3]

    return pl.pallas_call(body, name=name, grid=(h // th,),
                          in_specs=[pl.BlockSpec((NCHIP, th, w), lambda i: (0, i, 0))],
                          out_specs=pl.BlockSpec((th, w), lambda i: (i, 0)), out_shape=_sds((h, w), f32),
                          compiler_params=_params(1, True))(b4)


def _coords():
    x, y, c = lax.axis_index("x"), lax.axis_index("y"), lax.axis_index("c")
    return x, y, c, [(1 - x, y), (x, 1 - y), (1 - x, 1 - y)]


def _gather_chips(arrs):
    n = len(arrs)

    def body(*refs):
        ins, outs = refs[:n], refs[n:2 * n]
        send_sems, recv_sems, local_sems = refs[2 * n:2 * n + 3]
        stage = refs[2 * n + 3:]
        x, y, c, chips = _coords()
        s = 2 * x + y
        sib = (x, y, 1 - c)
        load = [pltpu.make_async_copy(ins[a], stage[a], local_sems.at[a]) for a in range(n)]
        local = [pltpu.make_async_copy(stage[a], outs[a].at[s], local_sems.at[n + a]) for a in range(n)]
        for cp in load:
            cp.start()

        def over_ici(k, a, slot, peer):
            return pltpu.make_async_remote_copy(src_ref=ins[a].at[c], dst_ref=outs[a].at[slot, c], send_sem=send_sems.at[k * n + a],
                                                recv_sem=recv_sems.at[k * n + a], device_id=peer, device_id_type=MESH)

        def to_sibling(k, a, slot, half):
            i = (3 + k) * n + a
            return pltpu.make_async_remote_copy(src_ref=outs[a].at[slot, half], dst_ref=outs[a].at[slot, half], send_sem=send_sems.at[i],
                                                recv_sem=recv_sems.at[i], device_id=sib, device_id_type=MESH)

        sends = [over_ici(k, a, s, (px, py, c)) for k, (px, py) in enumerate(chips) for a in range(n)]
        for cp in sends:
            cp.start()
        for a in range(n):
            load[a].wait()
            local[a].start()
        passed = []
        for k, (px, py) in enumerate(chips):
            for a in range(n):
                over_ici(k, a, 2 * px + py, (px, py, c)).wait_recv()
                cp = to_sibling(k, a, 2 * px + py, c)
                cp.start()
                passed.append(cp)
        for k, (px, py) in enumerate(chips):
            for a in range(n):
                to_sibling(k, a, 2 * px + py, 1 - c).wait_recv()
        for cp in sends + passed:
            cp.wait_send()
        for cp in local:
            cp.wait()

    return pl.pallas_call(
        body, name="gather_weights", in_specs=[ANY] * n, out_specs=[ANY] * n,
        out_shape=[_sds((NCHIP,) + a.shape, a.dtype) for a in arrs],
        scratch_shapes=[pltpu.SemaphoreType.DMA((6 * n,)), pltpu.SemaphoreType.DMA((6 * n,)), pltpu.SemaphoreType.DMA((2 * n,))]
        + [pltpu.VMEM(a.shape, a.dtype) for a in arrs],
        compiler_params=pltpu.CompilerParams(vmem_limit_bytes=VMEM_LIMIT),
    )(*arrs)


HBM = pl.BlockSpec(memory_space=pltpu.HBM)
SEM = pl.BlockSpec(memory_space=pltpu.SEMAPHORE)
EFFECT = pltpu.SideEffectType.DATAFLOW_SIDE_EFFECTING


def _split_start(name, n_copies, make_copies, ins, land_shapes):
    ni, nl = len(ins), len(land_shapes)

    def body(*refs):
        in_refs, land_refs = refs[:ni], refs[ni:ni + nl]
        send_sems, recv_sems = refs[ni + nl], refs[ni + nl + 1]
        token = refs[-1]
        for cp in make_copies(in_refs, land_refs, send_sems, recv_sems):
            cp.start()
        token[...] = jnp.zeros_like(token)

    lands = [pltpu.with_memory_space_constraint(lax.empty(s.shape, s.dtype), pltpu.HBM) for s in land_shapes]
    res = pl.pallas_call(
        body, name=name,
        out_shape=(pltpu.SemaphoreType.DMA((n_copies,)), pltpu.SemaphoreType.DMA((n_copies,)),
                   *[pltpu.HBM(a.shape, a.dtype) for a in ins], *[pltpu.HBM(s.shape, s.dtype) for s in land_shapes],
                   _sds((8, 128), f32)),
        in_specs=[HBM] * (ni + nl), out_specs=(SEM, SEM, *[HBM] * (ni + nl), pl.BlockSpec(memory_space=pltpu.VMEM)),
        input_output_aliases={i: 2 + i for i in range(ni + nl)},
        compiler_params=pltpu.CompilerParams(has_side_effects=EFFECT),
    )(*[pltpu.with_memory_space_constraint(a, pltpu.HBM) for a in ins], *lands)
    return res[0], res[1], list(res[2:2 + ni]), list(res[2 + ni:2 + ni + nl]), res[-1]


def _split_wait(name, make_copies, send_sems, recv_sems, ins, lands, after):
    ni, nl = len(ins), len(lands)

    def body(*refs):
        in_refs, land_refs = refs[:ni], refs[ni:ni + nl]
        s_sems, r_sems = refs[ni + nl], refs[ni + nl + 1]
        for cp in make_copies(in_refs, land_refs, s_sems, r_sems):
            cp.wait_send()
            cp.wait_recv()

    res = pl.pallas_call(
        body, name=name, out_shape=tuple(pltpu.HBM(a.shape, a.dtype) for a in ins + lands),
        in_specs=[HBM] * (ni + nl) + [SEM, SEM, ANY], out_specs=tuple([HBM] * (ni + nl)),
        input_output_aliases={i: i for i in range(ni + nl)},
        compiler_params=pltpu.CompilerParams(has_side_effects=EFFECT),
    )(*ins, *lands, send_sems, recv_sems, after)
    return list(res[:ni]), list(res[ni:])


def _gather_copies(n):
    def make(in_refs, land_refs, send_sems, recv_sems):
        x, y, c, chips = _coords()
        s = 2 * x + y
        return [pltpu.make_async_remote_copy(src_ref=in_refs[a], dst_ref=land_refs[a].at[s], send_sem=send_sems.at[k * n + a],
                                             recv_sem=recv_sems.at[k * n + a], device_id=(px, py, c), device_id_type=MESH)
                for k, (px, py) in enumerate(chips) for a in range(n)]
    return make


def _sibling_half_copies(n):
    def make(in_refs, land_refs, send_sems, recv_sems):
        x, y, c, _ = _coords()
        return [pltpu.make_async_remote_copy(src_ref=in_refs[a].at[:, 1 - c], dst_ref=land_refs[a], send_sem=send_sems.at[a],
                                             recv_sem=recv_sems.at[a], device_id=(x, y, 1 - c), device_id_type=MESH)
                for a in range(n)]
    return make


def _chip_part_copies(n):
    def make(in_refs, land_refs, send_sems, recv_sems):
        x, y, c, chips = _coords()
        return [pltpu.make_async_remote_copy(src_ref=in_refs[a].at[2 * px + py], dst_ref=land_refs[a].at[k],
                                             send_sem=send_sems.at[k * n + a], recv_sem=recv_sems.at[k * n + a],
                                             device_id=(px, py, c), device_id_type=MESH)
                for k, (px, py) in enumerate(chips) for a in range(n)]
    return make


def _sibling_whole_copies(n):
    def make(in_refs, land_refs, send_sems, recv_sems):
        x, y, c, _ = _coords()
        return [pltpu.make_async_remote_copy(src_ref=in_refs[a], dst_ref=land_refs[a], send_sem=send_sems.at[a],
                                             recv_sem=recv_sems.at[a], device_id=(x, y, 1 - c), device_id_type=MESH)
                for a in range(n)]
    return make


def _place_own(chip_arr, owns, lands, steps):
    n = len(owns)

    def body(s_ref, *refs):
        for a in range(n):
            refs[2 * n + a][...] = refs[a][...]

    tiles = [o.shape[0] // steps for o in owns]
    spec = pltpu.PrefetchScalarGridSpec(
        num_scalar_prefetch=1, grid=(steps,),
        in_specs=[pl.BlockSpec((t, o.shape[1]), lambda i, s_ref: (i, 0)) for t, o in zip(tiles, owns)] + [ANY] * n,
        out_specs=[pl.BlockSpec((None, t, o.shape[1]), lambda i, s_ref: (s_ref[0], i, 0)) for t, o in zip(tiles, owns)])
    return pl.pallas_call(body, name="place_own", grid_spec=spec, out_shape=[_sds(l.shape, l.dtype) for l in lands],
                          input_output_aliases={1 + n + a: a for a in range(n)},
                          compiler_params=_params(1))(chip_arr, *owns, *lands)


def _sibling_halves(g4s, small):
    n = len(g4s)

    def body(*refs):
        ins, small_ref = refs[:n], refs[n]
        outs, small_out = refs[n + 1:2 * n + 1], refs[2 * n + 1]
        send_sems, recv_sems = refs[2 * n + 2:]
        x, y, c, _ = _coords()
        sib = (x, y, 1 - c)

        def remote(a, half):
            src = small_ref if a == n else ins[a].at[:, half]
            dst = small_out if a == n else outs[a]
            return pltpu.make_async_remote_copy(src_ref=src, dst_ref=dst, send_sem=send_sems.at[a], recv_sem=recv_sems.at[a],
                                                device_id=sib, device_id_type=MESH)

        sends = [remote(a, 1 - c) for a in range(n + 1)]
        for cp in sends:
            cp.start()
        for a in range(n + 1):
            remote(a, c).wait_recv()
        for cp in sends:
            cp.wait_send()

    return pl.pallas_call(
        body, name="reduce_sibling", in_specs=[ANY] * (n + 1), out_specs=[ANY] * (n + 1),
        out_shape=[_sds((g.shape[0],) + g.shape[2:], f32) for g in g4s] + [_sds(small.shape, f32)],
        scratch_shapes=[pltpu.SemaphoreType.DMA((n + 1,)), pltpu.SemaphoreType.DMA((n + 1,))],
    )(*g4s, small)


def _exchange_chips(parts, small2):
    n = len(parts)

    def body(*refs):
        ins, small_ref = refs[:n], refs[n]
        outs, small_out = refs[n + 1:2 * n + 1], refs[2 * n + 1]
        send_sems, recv_sems, local_sem = refs[2 * n + 2:]
        x, y, c, chips = _coords()
        s = 2 * x + y
        local = pltpu.make_async_copy(small_ref.at[c], small_out.at[s], local_sem)
        local.start()

        def remote(k, a, dest_chip, small_slot, peer):
            if a == n:
                src, dst = small_ref.at[c], small_out.at[small_slot]
            else:
                src, dst = ins[a].at[dest_chip], outs[a].at[k]
            i = k * (n + 1) + a
            return pltpu.make_async_remote_copy(src_ref=src, dst_ref=dst, send_sem=send_sems.at[i], recv_sem=recv_sems.at[i],
                                                device_id=peer, device_id_type=MESH)

        sends = [remote(k, a, 2 * px + py, s, (px, py, c)) for k, (px, py) in enumerate(chips) for a in range(n + 1)]
        for cp in sends:
            cp.start()
        for k, (px, py) in enumerate(chips):
            for a in range(n + 1):
                remote(k, a, s, 2 * px + py, (px, py, c)).wait_recv()
        for cp in sends:
            cp.wait_send()
        local.wait()

    m = 3 * (n + 1)
    return pl.pallas_call(
        body, name="reduce_chips", in_specs=[ANY] * (n + 1), out_specs=[ANY] * (n + 1),
        out_shape=[_sds((3,) + p.shape[1:], p.dtype) for p in parts] + [_sds((NCHIP,) + small2.shape[1:], f32)],
        scratch_shapes=[pltpu.SemaphoreType.DMA((m,)), pltpu.SemaphoreType.DMA((m,)), pltpu.SemaphoreType.DMA],
    )(*parts, small2)


def _share_sibling(halves):
    n = len(halves)

    def body(*refs):
        ins, outs = refs[:n], refs[n:2 * n]
        send_sems, recv_sems = refs[2 * n:]
        x, y, c, _ = _coords()
        sib = (x, y, 1 - c)
        sends = [pltpu.make_async_remote_copy(src_ref=ins[a], dst_ref=outs[a], send_sem=send_sems.at[a], recv_sem=recv_sems.at[a],
                                              device_id=sib, device_id_type=MESH) for a in range(n)]
        for cp in sends:
            cp.start()
        for cp in sends:
            cp.wait()

    return pl.pallas_call(
        body, name="reduce_share", in_specs=[ANY] * n, out_specs=[ANY] * n,
        out_shape=[_sds(h.shape, f32) for h in halves],
        scratch_shapes=[pltpu.SemaphoreType.DMA((n,)), pltpu.SemaphoreType.DMA((n,))],
    )(*halves)


def _block_diag_pairs(w):
    w = w.reshape(NCH, 2, HD, HD)
    z = jnp.zeros((NCH, HD, HD), w.dtype)
    return jnp.concatenate([jnp.concatenate([w[:, 0], z], axis=2), jnp.concatenate([z, w[:, 1]], axis=2)], axis=1)


def _diag_blocks(m):
    return jnp.stack([m[:, :HD, :HD], m[:, HD:, HD:]], axis=1).reshape(NH, HD, HD)


def _pack(vs, rows):
    flat = jnp.concatenate([v.reshape(-1) for v in vs])
    return jnp.pad(flat, (0, rows * 128 - flat.shape[0])).reshape(rows, 128)


def _unpack(packed, shapes):
    flat = packed.reshape(-1)
    out, off = [], 0
    for shp in shapes:
        size = math.prod(shp)
        out.append(flat[off:off + size].reshape(shp))
        off += size
    return out


def _rows_for(sizes, multiple):
    rows = -(-sum(sizes) // 128)
    return -(-rows // multiple) * multiple


def kernel(x, norm_mix_g, w_in, b_gate, conv_w, conv_b, lru_lambda, lru_wa, lru_ba, lru_wx, lru_bx, attn_sink, w_out, norm_ffn_g, w_ffn_in, w_ffn_out, norm_final_g, loss_target, m_norm_mix_g, m_w_in, m_b_gate, m_conv_w, m_conv_b, m_lru_lambda, m_lru_wa, m_lru_ba, m_lru_wx, m_lru_bx, m_attn_sink, m_w_out, m_norm_ffn_g, m_w_ffn_in, m_w_ffn_out, m_norm_final_g, v_norm_mix_g, v_w_in, v_b_gate, v_conv_w, v_conv_b, v_lru_lambda, v_lru_wa, v_lru_ba, v_lru_wx, v_lru_bx, v_attn_sink, v_w_out, v_norm_ffn_g, v_w_ffn_in, v_w_ffn_out, v_norm_final_g):
    S = x.shape[1]
    xs = x[0]
    tgt = loss_target[0]
    cx, cy, cc = lax.axis_index("x"), lax.axis_index("y"), lax.axis_index("c")
    chip = 2 * cx + cy
    SW = D // NCHIP

    small_shard = _pack([conv_w[0], lru_lambda[0], lru_ba[0], lru_bx[0]], 32)
    halves_of = lambda a: a.reshape(2, a.shape[0] // 2, a.shape[1])
    w_in_g, small_g = _gather_chips([halves_of(w_in[0].astype(bf16)), halves_of(small_shard)])
    w_in_g = w_in_g.reshape(NCHIP, D, SHW)
    small_g = small_g.reshape(NCHIP, 32, 128)
    late = [w_ffn_in[0].astype(bf16), w_out[0].astype(bf16), w_ffn_out[0].astype(bf16)]
    late_send, late_recv, late_src, late_land, late_token = _split_start(
        "gather_late_start", 9, _gather_copies(3), late, [_sds((NCHIP,) + a.shape, bf16) for a in late])
    small_parts = [_unpack(small_g[s], [(4, SW), (2, SW), (2, SW), (2, SW)]) for s in range(NCHIP)]
    conv_w_f, lam_f, ba_f, bx_f = [jnp.concatenate([small_parts[s][p] for s in range(NCHIP)], axis=1) for p in range(4)]
    wbd = jnp.concatenate([_block_diag_pairs(lru_wa[0, 0]), _block_diag_pairs(lru_wx[0, 0]),
                           _block_diag_pairs(lru_wa[0, 1]), _block_diag_pairs(lru_wx[0, 1])], axis=2).astype(bf16)
    conv_b_f = conv_b
    sink = attn_sink

    xn, proj = _rms_matmul("rms_proj", xs, norm_mix_g + late_token[0:1, 0:1], w_in_g, 1024)
    y_a = _lru_fwd(proj, conv_w_f, conv_b_f, lam_f, ba_f, bx_f, wbd)
    y_b = _attn_fwd(proj, sink)
    merged = _merge_fwd(proj, b_gate, y_a, y_b, 512)
    late_src, late_land = _split_wait("gather_late_wait", _gather_copies(3), late_send, late_recv, late_src, late_land, merged)
    chip_arr = chip.reshape(1).astype(jnp.int32)
    w_ffn_in_g, w_out_g, w_ffn_out_g = _place_own(chip_arr, late_src, late_land, 4)
    w_out_f = w_out_g.reshape(D, D)
    w_ffn_out_f = w_ffn_out_g.reshape(FF, D)
    x1 = _mm_residual("out_proj", merged, w_out_f, xs, 512)
    xn2, gu, act = _rms_matmul_swiglu("rms_ffn_in", x1, norm_ffn_g, w_ffn_in_g, 1024)
    x2 = _mm_residual("ffn_out", act, w_ffn_out_f, x1, 512)
    dx2, loss_row, dg3 = _final_loss_bwd(x2, norm_final_g.reshape(1, D), tgt, 256)

    tm = min(1024, S)
    tk = min(2048, S)
    gw_ffn_out = _mm_tn("dw_ffn_out", act, pl.BlockSpec((tk, SHW), lambda i, k: (k, i)),
                        dx2, pl.BlockSpec((tk, D), lambda i, k: (k, 0)),
                        _sds((FF, D), f32), pl.BlockSpec((SHW, D), lambda i, k: (i, 0)), (2, S // tk), (SHW, D))
    dgu = _swiglu_bwd(dx2, w_ffn_out_f, gu, 256)
    dxn2 = _mm_nt_groups("dxn2", dgu, pl.BlockSpec((None, tm, SHW), lambda i, g: (g // 2, i, g % 2)), w_ffn_in_g, S, tm)
    gw_ffn_in = _mm_tn("dw_ffn_in", xn2, pl.BlockSpec((tk, D), lambda g, k: (k, 0)),
                       dgu, pl.BlockSpec((None, tk, SHW), lambda g, k: (g // 2, k, g % 2)),
                       _sds((NCHIP, D, SHW), f32), pl.BlockSpec((None, D, SHW), lambda g, k: (g, 0, 0)),
                       (NCHIP, S // tk), (D, SHW))
    c_arr = cc.reshape(1).astype(jnp.int32)
    early_names, early_tiles = ["w_ffn_in", "w_ffn_out"], [256, 352]
    early = [gw_ffn_in.reshape(NCHIP, 2, D // 2, SHW), gw_ffn_out.reshape(NCHIP, 2, FF // NCHIP // 2, D)]
    ea_send, ea_recv, ea_src, ea_land, ea_token = _split_start(
        "reduce_early_sibling_start", 2, _sibling_half_copies(2), early,
        [_sds((NCHIP,) + g.shape[2:], f32) for g in early])
    dx1, dg2 = _rms_bwd("rms_ffn_bwd", x1, norm_ffn_g + ea_token[0:1, 0:1], dxn2, dx2, 256)

    dmerged = _mm_nt_resident("d_merged", dx1, w_out_f, 512)
    gw_out = _mm_tn("dw_out", merged, pl.BlockSpec((tk, D), lambda i, k: (k, 0)),
                    dx1, pl.BlockSpec((tk, D), lambda i, k: (k, 0)),
                    _sds((D, D), f32), pl.BlockSpec((D, D), lambda i, k: (0, 0)), (1, S // tk), (D, D))
    dz0, dz1, dy_a, dy_b, db0, db1 = _merge_bwd(proj, b_gate, y_a, y_b, dmerged, 512)
    ea_src, ea_land = _split_wait("reduce_early_sibling_wait", _sibling_half_copies(2), ea_send, ea_recv, ea_src, ea_land, dy_b)
    early_pairs = [_pair_sum("pair_sum_" + nm, c_arr, g4, r, th)
                   for nm, g4, r, th in zip(early_names, ea_src, ea_land, early_tiles)]
    eb_send, eb_recv, eb_src, eb_land, eb_token = _split_start(
        "reduce_early_chips_start", 6, _chip_part_copies(2), [p[1] for p in early_pairs],
        [_sds((3,) + p[1].shape[1:], bf16) for p in early_pairs])
    dq, dk, dv, dsink = _attn_bwd(proj, sink + eb_token[0:1, 0:1], y_b, dy_b)
    _, eb_land = _split_wait("reduce_early_chips_wait", _chip_part_copies(2), eb_send, eb_recv, eb_src, eb_land, dq)
    early_halves = [_chip_sum("chip_sum_" + nm, chip_arr, p[0], r3, th)
                    for nm, p, r3, th in zip(early_names, early_pairs, eb_land, early_tiles)]
    ec_send, ec_recv, ec_src, ec_land, ec_token = _split_start(
        "reduce_early_share_start", 2, _sibling_whole_copies(2), early_halves, [_sds(h.shape, f32) for h in early_halves])
    du, dgl, dcw, dcb, dlam, dba, dbx, dwbd = _lru_bwd(proj, dy_a, conv_w_f, conv_b_f + ec_token[0:1, 0:1], lam_f, ba_f, bx_f, wbd)
    early_halves, early_other = _split_wait("reduce_early_share_wait", _sibling_whole_copies(2), ec_send, ec_recv, ec_src, ec_land, du)
    dproj = jnp.concatenate([du, dgl, dq, dk.astype(bf16), dv.astype(bf16), dz0, dz1], axis=1)
    dxn = _mm_nt_groups("dxn", dproj, pl.BlockSpec((tm, SHW), lambda i, g: (i, g)), w_in_g, S, tm)
    gw_in = _mm_tn("dw_in", xn, pl.BlockSpec((tk, D), lambda g, k: (k, 0)),
                   dproj, pl.BlockSpec((tk, SHW), lambda g, k: (k, g)),
                   _sds((NCHIP, D, SHW), f32), pl.BlockSpec((None, D, SHW), lambda g, k: (g, 0, 0)),
                   (NCHIP, S // tk), (D, SHW))
    grad_x, dg1 = _rms_bwd("rms_mix_bwd", xs, norm_mix_g, dxn, dx1, 256)

    d_wa = jnp.stack([_diag_blocks(dwbd[:, :, 0:CW]), _diag_blocks(dwbd[:, :, 2 * CW:3 * CW])])
    d_wx = jnp.stack([_diag_blocks(dwbd[:, :, CW:2 * CW]), _diag_blocks(dwbd[:, :, 3 * CW:4 * CW])])
    small_full = [dg1, jnp.concatenate([db0, db1], axis=1), dcw, dcb, dlam, d_wa, dba, d_wx, dbx, dsink[:, 0], dg2, dg3,
                  loss_row[0, 0:1]]
    full_shapes = [(1, D), (1, 2 * D), (4, D), (1, D), (2, D), (2, NH, HD, HD), (2, D), (2, NH, HD, HD), (2, D), (NH,),
                   (1, D), (1, D), (1,)]
    rows_full = _rows_for([math.prod(s) for s in full_shapes], 16)
    small_vec = _pack(small_full, rows_full)

    late_names, late_tiles = ["w_in", "w_out"], [256, 128]
    big = [gw_in.reshape(NCHIP, 2, D // 2, SHW), gw_out.reshape(NCHIP, 2, D // NCHIP // 2, D)]
    *recv_a, small_sib = _sibling_halves(big, small_vec)
    pairs = [_pair_sum("pair_sum_" + nm, c_arr, g4, r, th) for nm, g4, r, th in zip(late_names, big, recv_a, late_tiles)]
    small_chip = _add2("pair_sum_small", small_vec, small_sib).reshape(2, rows_full // 2, 128)
    *recv_b, small_all = _exchange_chips([p[1] for p in pairs], small_chip)
    halves = [_chip_sum("chip_sum_" + nm, chip_arr, p[0], r3, th) for nm, p, r3, th in zip(late_names, pairs, recv_b, late_tiles)]
    halves.append(_sum4("chip_sum_small", small_all, rows_full // 2))
    *recv_c, small_other = _share_sibling(halves)
    small_lo = jnp.where(cc == 0, halves[2], small_other)
    small_hi = jnp.where(cc == 0, small_other, halves[2])
    g_full = _unpack(jnp.concatenate([small_lo, small_hi], axis=0), full_shapes)

    out_big = {}
    for nm, w, g_own, g_recv, m, v, th in zip(late_names + early_names, [w_in, w_out, w_ffn_in, w_ffn_out],
                                              halves[:2] + early_halves, recv_c + early_other,
                                              [m_w_in, m_w_out, m_w_ffn_in, m_w_ffn_out],
                                              [v_w_in, v_w_out, v_w_ffn_in, v_w_ffn_out], late_tiles + early_tiles):
        g_, d_, m_, v_ = _adamw_halves("adamw_" + nm, c_arr, w[0], g_own, g_recv, m[0], v[0], th)
        out_big[nm] = (g_[None], d_[None], m_[None], v_[None])

    small_names = ["norm_mix_g", "b_gate", "conv_w", "conv_b", "lru_lambda", "lru_wa", "lru_ba", "lru_wx", "lru_bx", "attn_sink",
                   "norm_ffn_g", "norm_final_g"]
    sharded = {"conv_w", "lru_lambda", "lru_ba", "lru_bx"}
    small_w = [norm_mix_g, b_gate, conv_w, conv_b, lru_lambda, lru_wa, lru_ba, lru_wx, lru_bx, attn_sink, norm_ffn_g, norm_final_g]
    small_m = [m_norm_mix_g, m_b_gate, m_conv_w, m_conv_b, m_lru_lambda, m_lru_wa, m_lru_ba, m_lru_wx, m_lru_bx, m_attn_sink,
               m_norm_ffn_g, m_norm_final_g]
    small_v = [v_norm_mix_g, v_b_gate, v_conv_w, v_conv_b, v_lru_lambda, v_lru_wa, v_lru_ba, v_lru_wx, v_lru_bx, v_attn_sink,
               v_norm_ffn_g, v_norm_final_g]
    g_local = []
    for nm, g, w in zip(small_names, g_full, small_w):
        if nm in sharded:
            g = lax.dynamic_slice_in_dim(g, chip * SW, SW, axis=1)
        g_local.append(g.reshape(w.shape))
    local_shapes = [w.shape for w in small_w]
    rows_local = _rows_for([math.prod(s) for s in local_shapes], 8)
    d_s, m_s, v_s = _adamw("adamw_small", _pack(small_w, rows_local), _pack(g_local, rows_local),
                           _pack(small_m, rows_local), _pack(small_v, rows_local), rows_local)
    d_l, m_l, v_l = _unpack(d_s, local_shapes), _unpack(m_s, local_shapes), _unpack(v_s, local_shapes)
    res = {nm: (g_local[i], d_l[i], m_l[i], v_l[i]) for i, nm in enumerate(small_names)}
    res.update(out_big)

    order = ["norm_mix_g", "w_in", "b_gate", "conv_w", "conv_b", "lru_lambda", "lru_wa", "lru_ba", "lru_wx", "lru_bx", "attn_sink",
             "w_out", "norm_ffn_g", "w_ffn_in", "w_ffn_out", "norm_final_g"]
    outs = [g_full[-1][0], grad_x[None]]
    for k in range(4):
        outs += [res[nm][k] for nm in order]
    return tuple(outs)
```

```python
import functools
import math

import jax
import jax.numpy as jnp
from jax import lax
from jax.experimental import pallas as pl
from jax.experimental.pallas import tpu as pltpu

f32 = jnp.float32
bf16 = jnp.bfloat16

D = 1024
NH = 16
HD = 64
FF = 2816
INW = 5632
NCHIP = 4
SHW = INW // NCHIP
CW = 128
NCH = D // CW
BLK = 128
EPS = 1e-6
NEG_INF = -1e30
RGLRU_C = 8.0
ADAM_LR, ADAM_B1, ADAM_B2, ADAM_EPS, ADAM_WD, ADAM_STEP = 0.001, 0.9, 0.999, 1e-08, 0.01, 10
VMEM_LIMIT = 58 * 1024 * 1024
MESH = pl.DeviceIdType.MESH
ANY = pl.BlockSpec(memory_space=pl.ANY)

COL_U, COL_G, COL_Q, COL_K, COL_V, COL_Z0, COL_Z1 = 0, 4, 8, 12, 13, 14, 18


def _params(n_axes, vmem=False):
    return pltpu.CompilerParams(dimension_semantics=("arbitrary",) * n_axes,
                                vmem_limit_bytes=VMEM_LIMIT if vmem else None)


def _sds(shape, dtype):
    return jax.ShapeDtypeStruct(tuple(shape), dtype)


_DIMS = {"nn": (((1,), (0,)), ((), ())), "nt": (((1,), (1,)), ((), ())), "tn": (((0,), (0,)), ((), ()))}


def _mm(name, mode, a, a_spec, b, b_spec, out_shape, out_spec, grid, nk, acc_shape, add=None, add_spec=None):
    has_add = add is not None

    def body(*refs):
        a_ref, b_ref = refs[0], refs[1]
        add_ref = refs[2] if has_add else None
        o_ref = refs[2 + has_add]
        part = lax.dot_general(a_ref[...].astype(bf16), b_ref[...].astype(bf16), _DIMS[mode],
                               preferred_element_type=f32)
        if nk == 1:
            if has_add:
                part = add_ref[...] + part
            o_ref[...] = part.astype(o_ref.dtype)
            return
        acc_ref = refs[3 + has_add]
        k = pl.program_id(len(grid) - 1)

        @pl.when(k == 0)
        def _():
            acc_ref[...] = part

        @pl.when(k > 0)
        def _():
            acc_ref[...] += part

        @pl.when(k == nk - 1)
        def _():
            res = acc_ref[...]
            if has_add:
                res = add_ref[...] + res
            o_ref[...] = res.astype(o_ref.dtype)

    ins = [a, b] + ([add] if has_add else [])
    in_specs = [a_spec, b_spec] + ([add_spec] if has_add else [])
    scratch = [pltpu.VMEM(acc_shape, f32)] if nk > 1 else []
    return pl.pallas_call(body, name=name, grid=grid, in_specs=in_specs, out_specs=out_spec, out_shape=out_shape,
                          scratch_shapes=scratch, compiler_params=_params(len(grid), True))(*ins)


def _rms_matmul(name, x, g, w3, tm):
    S, K = x.shape
    G, _, Nw = w3.shape
    tm = min(tm, S)

    def body(x_ref, g_ref, w_ref, xn_ref, o_ref, xs_ref):
        @pl.when(pl.program_id(1) == 0)
        def _():
            xf = x_ref[...]
            r = lax.rsqrt(jnp.mean(xf * xf, axis=-1, keepdims=True) + EPS)
            xn = ((xf * r) * g_ref[...]).astype(bf16)
            xs_ref[...] = xn
            xn_ref[...] = xn

        o_ref[...] = jnp.dot(xs_ref[...], w_ref[...], preferred_element_type=f32)

    return pl.pallas_call(
        body, name=name, grid=(S // tm, G),
        in_specs=[pl.BlockSpec((tm, K), lambda i, j: (i, 0)), pl.BlockSpec((1, K), lambda i, j: (0, 0)),
                  pl.BlockSpec((None, K, Nw), lambda i, j: (j, 0, 0))],
        out_specs=[pl.BlockSpec((tm, K), lambda i, j: (i, 0)), pl.BlockSpec((tm, Nw), lambda i, j: (i, j))],
        out_shape=[_sds((S, K), bf16), _sds((S, G * Nw), f32)],
        scratch_shapes=[pltpu.VMEM((tm, K), bf16)], compiler_params=_params(2, True))(x, g, w3)


def _rms_matmul_swiglu(name, x, g, w3, tm):
    S, K = x.shape
    G, _, Nw = w3.shape
    tm = min(tm, S)
    half = G // 2

    def body(x_ref, g_ref, wg_ref, wu_ref, xn_ref, gu_ref, act_ref, xs_ref):
        @pl.when(pl.program_id(1) == 0)
        def _():
            xf = x_ref[...]
            r = lax.rsqrt(jnp.mean(xf * xf, axis=-1, keepdims=True) + EPS)
            xn = ((xf * r) * g_ref[...]).astype(bf16)
            xs_ref[...] = xn
            xn_ref[...] = xn

        xn = xs_ref[...]
        gate = jnp.dot(xn, wg_ref[...], preferred_element_type=f32)
        up = jnp.dot(xn, wu_ref[...], preferred_element_type=f32)
        gu_ref[0] = gate.astype(bf16)
        gu_ref[1] = up.astype(bf16)
        act_ref[...] = ((gate * _sigmoid(gate)) * up).astype(bf16)

    return pl.pallas_call(
        body, name=name, grid=(S // tm, half),
        in_specs=[pl.BlockSpec((tm, K), lambda i, j: (i, 0)), pl.BlockSpec((1, K), lambda i, j: (0, 0)),
                  pl.BlockSpec((None, K, Nw), lambda i, j: (j, 0, 0)),
                  pl.BlockSpec((None, K, Nw), lambda i, j: (half + j, 0, 0))],
        out_specs=[pl.BlockSpec((tm, K), lambda i, j: (i, 0)), pl.BlockSpec((2, tm, Nw), lambda i, j: (0, i, j)),
                   pl.BlockSpec((tm, Nw), lambda i, j: (i, j))],
        out_shape=[_sds((S, K), bf16), _sds((2, S, half * Nw), bf16), _sds((S, half * Nw), bf16)],
        scratch_shapes=[pltpu.VMEM((tm, K), bf16)], compiler_params=_params(2, True))(x, g, w3, w3)


def _mm_residual(name, a, w, res, tm):
    S, K = a.shape
    N = w.shape[1]
    tm = min(tm, S)
    return _mm(name, "nn", a, pl.BlockSpec((tm, K), lambda i: (i, 0)), w, pl.BlockSpec((K, N), lambda i: (0, 0)),
               _sds((S, N), f32), pl.BlockSpec((tm, N), lambda i: (i, 0)), (S // tm,), 1, None,
               add=res, add_spec=pl.BlockSpec((tm, N), lambda i: (i, 0)))


def _mm_nt_resident(name, a, w, tm):
    S, K = a.shape
    N = w.shape[0]
    tm = min(tm, S)
    return _mm(name, "nt", a, pl.BlockSpec((tm, K), lambda i: (i, 0)), w, pl.BlockSpec((N, K), lambda i: (0, 0)),
               _sds((S, N), f32), pl.BlockSpec((tm, N), lambda i: (i, 0)), (S // tm,), 1, None)


def _mm_nt_groups(name, a, a_spec, w3, S, tm):
    G, Dout, Kw = w3.shape
    return _mm(name, "nt", a, a_spec, w3, pl.BlockSpec((None, Dout, Kw), lambda i, g: (g, 0, 0)),
               _sds((S, Dout), f32), pl.BlockSpec((tm, Dout), lambda i, g: (i, 0)), (S // tm, G), G, (tm, Dout))


def _mm_tn(name, a, a_spec, b, b_spec, out_shape, out_spec, grid, acc_shape):
    return _mm(name, "tn", a, a_spec, b, b_spec, out_shape, out_spec, grid, grid[-1], acc_shape)


def _sigmoid(x):
    return 0.5 * jnp.tanh(0.5 * x) + 0.5


_GELU_C = math.sqrt(2.0 / math.pi)


def _gelu_and_grad(x):
    v = _GELU_C * (x + 0.044715 * (x * x * x))
    t = jnp.tanh(v)
    gl = 0.5 * x * (1.0 + t)
    dgl = 0.5 * (1.0 + t) + 0.5 * x * (1.0 - t * t) * (_GELU_C * (1.0 + 3.0 * 0.044715 * (x * x)))
    return gl, dgl


def _one_minus_exp2x(x, ex):
    y = 2.0 * x
    series = y * (1.0 + y * (0.5 + y * (1.0 / 6.0 + y * (1.0 / 24.0))))
    return jnp.where(y > -1.0 / 64.0, -series, 1.0 - ex * ex)


def _merge_fwd(proj, b_gate, y_a, y_b, tm):
    S = proj.shape[0]
    tm = min(tm, S)

    def body(z0_ref, z1_ref, b0_ref, b1_ref, ya_ref, yb_ref, o_ref):
        g0 = _sigmoid(z0_ref[...] + b0_ref[...])
        g1 = _sigmoid(z1_ref[...] + b1_ref[...])
        o_ref[...] = (g0 * ya_ref[...] + g1 * yb_ref[...]).astype(bf16)

    blk = lambda off: pl.BlockSpec((tm, 256), lambda j, i: (i, off + j))
    vec = lambda off: pl.BlockSpec((1, 256), lambda j, i: (0, off + j))
    return pl.pallas_call(body, name="merge_fwd", grid=(4, S // tm),
                          in_specs=[blk(COL_Z0), blk(COL_Z1), vec(0), vec(4), blk(0), blk(0)],
                          out_specs=blk(0), out_shape=_sds((S, D), bf16),
                          compiler_params=_params(2))(proj, proj, b_gate, b_gate, y_a, y_b)


def _merge_bwd(proj, b_gate, y_a, y_b, dm, tm):
    S = proj.shape[0]
    tm = min(tm, S)

    def body(z0_ref, z1_ref, b0_ref, b1_ref, ya_ref, yb_ref, dm_ref, dz0_ref, dz1_ref, dya_ref, dyb_ref, db0_ref, db1_ref):
        g0 = _sigmoid(z0_ref[...] + b0_ref[...])
        g1 = _sigmoid(z1_ref[...] + b1_ref[...])
        d = dm_ref[...]
        dz0 = (d * ya_ref[...]) * (g0 * (1.0 - g0))
        dz1 = (d * yb_ref[...]) * (g1 * (1.0 - g1))
        dz0_ref[...] = dz0.astype(bf16)
        dz1_ref[...] = dz1.astype(bf16)
        dya_ref[...] = d * g0
        dyb_ref[...] = d * g1

        @pl.when(pl.program_id(1) == 0)
        def _():
            db0_ref[...] = jnp.zeros_like(db0_ref)
            db1_ref[...] = jnp.zeros_like(db1_ref)

        db0_ref[...] += jnp.sum(dz0, axis=0, keepdims=True)
        db1_ref[...] += jnp.sum(dz1, axis=0, keepdims=True)

    blk = lambda off: pl.BlockSpec((tm, 256), lambda j, i: (i, off + j))
    vec = lambda off: pl.BlockSpec((1, 256), lambda j, i: (0, off + j))
    return pl.pallas_call(
        body, name="merge_bwd", grid=(4, S // tm),
        in_specs=[blk(COL_Z0), blk(COL_Z1), vec(0), vec(4), blk(0), blk(0), blk(0)],
        out_specs=[blk(0), blk(0), blk(0), blk(0), vec(0), vec(0)],
        out_shape=[_sds((S, D), bf16), _sds((S, D), bf16), _sds((S, D), f32), _sds((S, D), f32),
                   _sds((1, D), f32), _sds((1, D), f32)],
        compiler_params=_params(2))(proj, proj, b_gate, b_gate, y_a, y_b, dm)


def _swiglu_bwd(dx, w, gu, tm):
    S, K = dx.shape
    tm = min(tm, S)

    def body(dx_ref, w_ref, gu_ref, o_ref):
        d = lax.dot_general(dx_ref[...].astype(bf16), w_ref[...], _DIMS["nt"], preferred_element_type=f32)
        g = gu_ref[0].astype(f32)
        u = gu_ref[1].astype(f32)
        s = _sigmoid(g)
        o_ref[0] = ((d * u) * (s * (1.0 + g * (1.0 - s)))).astype(bf16)
        o_ref[1] = (d * (g * s)).astype(bf16)

    stacked = pl.BlockSpec((2, tm, FF), lambda i: (0, i, 0))
    return pl.pallas_call(body, name="swiglu_bwd", grid=(S // tm,),
                          in_specs=[pl.BlockSpec((tm, K), lambda i: (i, 0)), pl.BlockSpec((FF, K), lambda i: (0, 0)), stacked],
                          out_specs=stacked, out_shape=_sds((2, S, FF), bf16),
                          compiler_params=_params(1, True))(dx, w, gu)


def _final_loss_bwd(x2, g3, tgt, tm):
    S = x2.shape[0]
    tm = min(tm, S)

    def body(x_ref, g_ref, t_ref, dx_ref, loss_ref, dg_ref):
        @pl.when(pl.program_id(0) == 0)
        def _():
            loss_ref[...] = jnp.zeros_like(loss_ref)
            dg_ref[...] = jnp.zeros_like(dg_ref)

        x = x_ref[...]
        g = g_ref[...]
        r = lax.rsqrt(jnp.mean(x * x, axis=-1, keepdims=True) + EPS)
        xh = x * r
        err = xh * g - t_ref[...]
        row = jnp.mean(err * err, axis=-1, keepdims=True)
        loss_ref[...] += 0.5 * jnp.sum(row, axis=0, keepdims=True)
        dy = err * (1.0 / D)
        dg_ref[...] += jnp.sum(dy * xh, axis=0, keepdims=True)
        dxh = dy * g
        dx_ref[...] = r * (dxh - xh * jnp.mean(dxh * xh, axis=-1, keepdims=True))

    row_blk = pl.BlockSpec((tm, D), lambda i: (i, 0))
    vec = pl.BlockSpec((1, D), lambda i: (0, 0))
    return pl.pallas_call(body, name="final_loss_bwd", grid=(S // tm,), in_specs=[row_blk, vec, row_blk],
                          out_specs=[row_blk, pl.BlockSpec((1, 128), lambda i: (0, 0)), vec],
                          out_shape=[_sds((S, D), f32), _sds((1, 128), f32), _sds((1, D), f32)],
                          compiler_params=_params(1))(x2, g3, tgt)


def _rms_bwd(name, x, g, dxn, dres, tm):
    S = x.shape[0]
    tm = min(tm, S)

    def body(x_ref, g_ref, d_ref, r_ref, dx_ref, dg_ref):
        @pl.when(pl.program_id(0) == 0)
        def _():
            dg_ref[...] = jnp.zeros_like(dg_ref)

        x = x_ref[...]
        d = d_ref[...]
        r = lax.rsqrt(jnp.mean(x * x, axis=-1, keepdims=True) + EPS)
        xh = x * r
        dg_ref[...] += jnp.sum(d * xh, axis=0, keepdims=True)
        dxh = d * g_ref[...]
        dx_ref[...] = r_ref[...] + r * (dxh - xh * jnp.mean(dxh * xh, axis=-1, keepdims=True))

    row_blk = pl.BlockSpec((tm, D), lambda i: (i, 0))
    vec = pl.BlockSpec((1, D), lambda i: (0, 0))
    return pl.pallas_call(body, name=name, grid=(S // tm,), in_specs=[row_blk, vec, row_blk, row_blk],
                          out_specs=[row_blk, vec], out_shape=[_sds((S, D), f32), _sds((1, D), f32)],
                          compiler_params=_params(1))(x, g, dxn, dres)


LRU_TT = 256
SCAN_UNROLL = 4


def _halo(ref, i, S):
    nt = S // LRU_TT
    t0 = pl.multiple_of(i * LRU_TT, LRU_TT)
    p0 = pl.multiple_of(jnp.maximum(t0 - 8, 0), 8)
    n0 = pl.multiple_of(jnp.minimum(t0 + LRU_TT, S - 8), 8)
    prev = jnp.where(i > 0, ref[pl.ds(p0, 8), :], 0.0)
    nxt = jnp.where(i < nt - 1, ref[pl.ds(n0, 8), :], 0.0)
    return jnp.concatenate([prev, ref[pl.ds(t0, LRU_TT), :], nxt], axis=0)


def _shift(ext, k):
    n = LRU_TT + 16
    return pltpu.roll(ext, (-k) % n, 0)[8:8 + LRU_TT]


def _lru_gates(uc, wbd, ba, bx):
    pre = jnp.dot(uc.astype(bf16), wbd, preferred_element_type=f32)
    r_f = _sigmoid(pre[:, 0:CW] + ba[0:1])
    i_f = _sigmoid(pre[:, CW:2 * CW] + bx[0:1])
    r_b = _sigmoid(pre[:, 2 * CW:3 * CW] + ba[1:2])
    i_b = _sigmoid(pre[:, 3 * CW:4 * CW] + bx[1:2])
    return r_f, i_f, r_b, i_b


def _lru_coeffs(r, sp):
    log_a = (-RGLRU_C * r) * sp
    a = jnp.exp(log_a)
    beta = jnp.sqrt(jnp.maximum(_one_minus_exp2x(log_a, a), 0.0))
    return a, beta


def _lru_coeffs_inv(r, sp):
    log_a = (-RGLRU_C * r) * sp
    a = jnp.exp(log_a)
    om = jnp.maximum(_one_minus_exp2x(log_a, a), 0.0)
    return a, jnp.sqrt(om), lax.rsqrt(om)


def _conv_tile(u_ref, i, S, cw, cb):
    ext = _halo(u_ref, i, S)
    um2, um1, u0, up1 = _shift(ext, -2), _shift(ext, -1), ext[8:8 + LRU_TT], _shift(ext, 1)
    uc = um2 * cw[0:1] + um1 * cw[1:2] + u0 * cw[2:3] + up1 * cw[3:4] + cb
    return uc, (um2, um1, u0, up1)


def _scan_pair(S, fwd_a, fwd_b, fwd_out, rev_a, rev_b, rev_out):
    ng = S // 8
    idx = lax.broadcasted_iota(jnp.int32, (8, CW), 0)

    def local(a, b, rev):
        for sh in (1, 2, 4):
            if rev:
                keep = idx < 8 - sh
                amt = 8 - sh
            else:
                keep = idx >= sh
                amt = sh
            a_s = jnp.where(keep, pltpu.roll(a, amt, 0), 1.0)
            b_s = jnp.where(keep, pltpu.roll(b, amt, 0), 0.0)
            b = a * b_s + b
            a = a * a_s
        return a, b

    def step(it, carry):
        cf, cr = carry
        fwd_rows = [pl.multiple_of((it * SCAN_UNROLL + j) * 8, 8) for j in range(SCAN_UNROLL)]
        rev_rows = [pl.multiple_of((ng - 1 - (it * SCAN_UNROLL + j)) * 8, 8) for j in range(SCAN_UNROLL)]
        fwd_loc = [local(fwd_a(r), fwd_b(r), False) for r in fwd_rows]
        rev_loc = [local(rev_a(r), rev_b(r), True) for r in rev_rows]
        for j in range(SCAN_UNROLL):
            a, b = fwd_loc[j]
            h = a * cf + b
            fwd_out[pl.ds(fwd_rows[j], 8), :] = h
            cf = jnp.broadcast_to(h[7:8, :], (8, CW))
            a, b = rev_loc[j]
            h = a * cr + b
            rev_out[pl.ds(rev_rows[j], 8), :] = h
            cr = jnp.broadcast_to(h[0:1, :], (8, CW))
        return cf, cr

    zero = jnp.zeros((8, CW), f32)
    lax.fori_loop(0, ng // SCAN_UNROLL, step, (zero, zero))


def _lru_specs(S):
    seq = lambda off: pl.BlockSpec((S, CW), lambda j: (0, off + j))
    par = lambda rows: pl.BlockSpec((rows, CW), lambda j: (0, j))
    return seq, par


def _lru_fwd(proj, conv_w, conv_b, lam, ba, bx, wbd):
    S = proj.shape[0]
    nt = S // LRU_TT

    def body(u_ref, g_ref, cw_ref, cb_ref, lam_ref, ba_ref, bx_ref, wbd_ref, y_ref, state_ref, af_ref, bf_ref, ab_ref, bb_ref,
             sems):
        cw, cb, ba_v, bx_v, wbd_v = cw_ref[...], cb_ref[...], ba_ref[...], bx_ref[...], wbd_ref[...]
        sp = jax.nn.softplus(-lam_ref[...])
        cols = pl.ds(pl.multiple_of(pl.program_id(0) * CW, CW), CW)
        save = [pltpu.make_async_copy(ref, state_ref.at[k, :, cols], sems.at[k])
                for k, ref in enumerate((af_ref, bf_ref, ab_ref, bb_ref))]

        def phase1(i, c):
            uc, _ = _conv_tile(u_ref, i, S, cw, cb)
            r_f, i_f, r_b, i_b = _lru_gates(uc, wbd_v, ba_v, bx_v)
            rows = pl.ds(pl.multiple_of(i * LRU_TT, LRU_TT), LRU_TT)
            a, beta = _lru_coeffs(r_f, sp[0:1])
            af_ref[rows, :] = a
            bf_ref[rows, :] = beta * (i_f * uc)
            a, beta = _lru_coeffs(r_b, sp[1:2])
            ab_ref[rows, :] = a
            bb_ref[rows, :] = beta * (i_b * uc)
            return c

        lax.fori_loop(0, nt, phase1, 0)
        row8 = lambda ref: (lambda r0: ref[pl.ds(r0, 8), :])
        _scan_pair(S, row8(af_ref), row8(bf_ref), bf_ref, row8(ab_ref), row8(bb_ref), bb_ref)
        for cp in save:
            cp.start()

        def phase3(i, c):
            rows = pl.ds(pl.multiple_of(i * LRU_TT, LRU_TT), LRU_TT)
            y_ref[rows, :] = (bf_ref[rows, :] + bb_ref[rows, :]) * jax.nn.gelu(g_ref[rows, :])
            return c

        lax.fori_loop(0, nt, phase3, 0)
        for cp in save:
            cp.wait()

    seq, par = _lru_specs(S)
    return pl.pallas_call(
        body, name="lru_fwd", grid=(NCH,),
        in_specs=[seq(0), seq(NCH), par(4), par(1), par(2), par(2), par(2),
                  pl.BlockSpec((None, CW, 4 * CW), lambda j: (j, 0, 0))],
        out_specs=[seq(0), ANY], out_shape=[_sds((S, D), f32), _sds((4, S, D), f32)],
        scratch_shapes=[pltpu.VMEM((S, CW), f32)] * 4 + [pltpu.SemaphoreType.DMA((4,))], compiler_params=_params(1, True),
    )(proj, proj, conv_w, conv_b, lam, ba, bx, wbd)


def _lru_bwd(proj, dy, state, conv_w, conv_b, lam, ba, bx, wbd):
    S = proj.shape[0]
    nt = S // LRU_TT

    def body(u_ref, g_ref, dy_ref, state_ref, cw_ref, cb_ref, lam_ref, ba_ref, bx_ref, wbd_ref,
             du_ref, dg_ref, dcw_ref, dcb_ref, dlam_ref, dba_ref, dbx_ref, dwbd_ref,
             af_ref, bf_ref, ab_ref, bb_ref, sems):
        cw, cb, ba_v, bx_v, wbd_v = cw_ref[...], cb_ref[...], ba_ref[...], bx_ref[...], wbd_ref[...]
        lam_v = lam_ref[...]
        sp = jax.nn.softplus(-lam_v)
        cols = pl.ds(pl.multiple_of(pl.program_id(0) * CW, CW), CW)
        load = [pltpu.make_async_copy(state_ref.at[k, :, cols], ref, sems.at[k])
                for k, ref in enumerate((af_ref, bf_ref, ab_ref, bb_ref))]
        for cp in load:
            cp.start()
        for cp in load:
            cp.wait()
        row8 = lambda ref: (lambda r0: ref[pl.ds(r0, 8), :])

        def scaled_dh(a_ref):
            def f(r0):
                gl, _ = _gelu_and_grad(g_ref[pl.ds(r0, 8), :])
                return a_ref[pl.ds(r0, 8), :] * (dy_ref[pl.ds(r0, 8), :] * gl)
            return f

        _scan_pair(S, row8(ab_ref), scaled_dh(ab_ref), ab_ref, row8(af_ref), scaled_dh(af_ref), af_ref)

        dcw_ref[...] = jnp.zeros_like(dcw_ref)
        dcb_ref[...] = jnp.zeros_like(dcb_ref)
        dlam_ref[...] = jnp.zeros_like(dlam_ref)
        dba_ref[...] = jnp.zeros_like(dba_ref)
        dbx_ref[...] = jnp.zeros_like(dbx_ref)
        dwbd_ref[...] = jnp.zeros_like(dwbd_ref)

        def direction(uc, r, i_g, dht, h_nb, sp_d):
            a, beta, inv_beta = _lru_coeffs_inv(r, sp_d)
            da = dht * h_nb
            dbeta = dht * (i_g * uc)
            d_iu = dht * beta
            dlog_a = da * a - (a * a) * (dbeta * inv_beta)
            dlr = dlog_a * r
            dsp = -RGLRU_C * jnp.sum(dlr, axis=0, keepdims=True)
            dpre_r = (dlr * (1.0 - r)) * (-RGLRU_C * sp_d)
            dpre_i = (d_iu * uc) * (i_g * (1.0 - i_g))
            return dpre_r, dpre_i, d_iu * i_g, dsp

        def phase4(i, c):
            uc, (um2, um1, u0, up1) = _conv_tile(u_ref, i, S, cw, cb)
            r_f, i_f, r_b, i_b = _lru_gates(uc, wbd_v, ba_v, bx_v)
            rows = pl.ds(pl.multiple_of(i * LRU_TT, LRU_TT), LRU_TT)
            gl, dgl = _gelu_and_grad(g_ref[rows, :])
            dyt = dy_ref[rows, :]
            dh = dyt * gl
            dg_ref[rows, :] = ((dyt * (bf_ref[rows, :] + bb_ref[rows, :])) * dgl).astype(dg_ref.dtype)
            dht_f = dh + _shift(_halo(af_ref, i, S), 1)
            h_prev = _shift(_halo(bf_ref, i, S), -1)
            dht_b = dh + _shift(_halo(ab_ref, i, S), -1)
            h_next = _shift(_halo(bb_ref, i, S), 1)
            prf, pif, duc_f, dsp_f = direction(uc, r_f, i_f, dht_f, h_prev, sp[0:1])
            prb, pib, duc_b, dsp_b = direction(uc, r_b, i_b, dht_b, h_next, sp[1:2])
            dpre = jnp.concatenate([prf, pif, prb, pib], axis=1)
            dpre_b = dpre.astype(bf16)
            duc = (duc_f + duc_b) + lax.dot_general(dpre_b, wbd_v, _DIMS["nt"], preferred_element_type=f32)
            dwbd_ref[...] += lax.dot_general(uc.astype(bf16), dpre_b, _DIMS["tn"], preferred_element_type=f32)
            colsum = lambda v: jnp.sum(v, axis=0, keepdims=True)
            dba_ref[...] += jnp.concatenate([colsum(prf), colsum(prb)], axis=0)
            dbx_ref[...] += jnp.concatenate([colsum(pif), colsum(pib)], axis=0)
            dlam_ref[...] += jnp.concatenate([dsp_f, dsp_b], axis=0)
            dcb_ref[...] += colsum(duc)
            dcw_ref[...] += jnp.concatenate([colsum(duc * um2), colsum(duc * um1), colsum(duc * u0),
                                             colsum(duc * up1)], axis=0)
            af_ref[rows, :] = duc
            return c

        lax.fori_loop(0, nt, phase4, 0)
        dlam_ref[...] = dlam_ref[...] * (-_sigmoid(-lam_v))

        def phase5(i, c):
            ext = _halo(af_ref, i, S)
            rows = pl.ds(pl.multiple_of(i * LRU_TT, LRU_TT), LRU_TT)
            du = (_shift(ext, 2) * cw[0:1] + _shift(ext, 1) * cw[1:2] + ext[8:8 + LRU_TT] * cw[2:3]
                  + _shift(ext, -1) * cw[3:4])
            du_ref[rows, :] = du.astype(du_ref.dtype)
            return c

        lax.fori_loop(0, nt, phase5, 0)

    seq, par = _lru_specs(S)
    return pl.pallas_call(
        body, name="lru_bwd", grid=(NCH,),
        in_specs=[seq(0), seq(NCH), seq(0), ANY, par(4), par(1), par(2), par(2), par(2),
                  pl.BlockSpec((None, CW, 4 * CW), lambda j: (j, 0, 0))],
        out_specs=[seq(0), seq(0), par(4), par(1), par(2), par(2), par(2),
                   pl.BlockSpec((None, CW, 4 * CW), lambda j: (j, 0, 0))],
        out_shape=[_sds((S, D), bf16), _sds((S, D), bf16), _sds((4, D), f32), _sds((1, D), f32), _sds((2, D), f32),
                   _sds((2, D), f32), _sds((2, D), f32), _sds((NCH, CW, 4 * CW), f32)],
        scratch_shapes=[pltpu.VMEM((S, CW), f32)] * 4 + [pltpu.SemaphoreType.DMA((4,))], compiler_params=_params(1, True),
    )(proj, proj, dy, state, conv_w, conv_b, lam, ba, bx, wbd)


_SLOPES = [2.0 ** (-8.0 * (h + 1) / NH) for h in range(NH)]


def _half_mask(shape, e):
    lane = lax.broadcasted_iota(jnp.int32, shape, 1)
    return (lane < HD) if e == 0 else (lane >= HD)


def _both_halves(x, src):
    return jnp.where(_half_mask(x.shape, src), x, pltpu.roll(x, HD, 1))


def _fold_halves(x, dst):
    return jnp.where(_half_mask(x.shape, dst), x + pltpu.roll(x, HD, 1), 0.0)


def _attn_base(n, S):
    tq = lax.broadcasted_iota(jnp.int32, (BLK, 3 * BLK), 0)
    sk = lax.broadcasted_iota(jnp.int32, (BLK, 3 * BLK), 1)
    dist = jnp.abs(tq + BLK - sk)
    kpos = n * BLK - BLK + sk
    valid = (dist <= BLK) & (kpos >= 0) & (kpos < S)
    return jnp.where(valid, -dist.astype(f32), NEG_INF)


def _group_heads(ref, kvh, scale):
    parts = []
    for i in range(4):
        pair = 2 * kvh + i // 2
        x = ref[:, pair * 128:(pair + 1) * 128]
        parts.append(jnp.where(_half_mask(x.shape, i % 2), x * scale, 0.0))
    return parts


def _stack_bf16(parts):
    return jnp.concatenate([p.astype(bf16) for p in parts], axis=0)


def _attn_softmax(s_raw, base, slope, sink):
    s = s_raw + slope * base
    m = jnp.maximum(jnp.max(s, axis=-1, keepdims=True), sink)
    p = jnp.exp(s - m)
    esink = jnp.exp(sink - m)
    inv = 1.0 / (jnp.sum(p, axis=-1, keepdims=True) + esink)
    return p, inv, esink * inv


def _attn_specs(S):
    nb = S // BLK
    q_spec = pl.BlockSpec((BLK, D), lambda n: (n, 2))
    kv = lambda col: [pl.BlockSpec((BLK, 256), lambda n: (jnp.maximum(n - 1, 0), col)),
                      pl.BlockSpec((BLK, 256), lambda n: (n, col)),
                      pl.BlockSpec((BLK, 256), lambda n: (jnp.minimum(n + 1, nb - 1), col))]
    return nb, q_spec, kv(COL_K), kv(COL_V)


def _attn_fwd(proj, sink):
    S = proj.shape[0]
    nb, q_spec, k_specs, v_specs = _attn_specs(S)

    def body(sink_ref, q_ref, kp_ref, kc_ref, kn_ref, vp_ref, vc_ref, vn_ref, o_ref):
        base = _attn_base(pl.program_id(0), S)
        kcat = jnp.concatenate([kp_ref[...], kc_ref[...], kn_ref[...]], axis=0)
        vcat = jnp.concatenate([vp_ref[...], vc_ref[...], vn_ref[...]], axis=0)
        even = _half_mask((BLK, 128), 0)
        for kvh in range(NH // 4):
            ch, off = kvh // 2, kvh % 2
            kb = _both_halves(kcat[:, ch * 128:(ch + 1) * 128], off).astype(bf16)
            vb = _both_halves(vcat[:, ch * 128:(ch + 1) * 128], off).astype(bf16)
            q4 = _stack_bf16(_group_heads(q_ref, kvh, HD ** -0.5))
            s4 = lax.dot_general(q4, kb, _DIMS["nt"], preferred_element_type=f32)
            ps, invs = [], []
            for i in range(4):
                h = 4 * kvh + i
                p, inv, _ = _attn_softmax(s4[i * BLK:(i + 1) * BLK], base, _SLOPES[h], sink_ref[0, h])
                ps.append(p)
                invs.append(inv)
            o4 = jnp.dot(_stack_bf16(ps), vb, preferred_element_type=f32)
            for pr in range(2):
                lo = o4[(2 * pr) * BLK:(2 * pr + 1) * BLK] * invs[2 * pr]
                hi = o4[(2 * pr + 1) * BLK:(2 * pr + 2) * BLK] * invs[2 * pr + 1]
                pair = 2 * kvh + pr
                o_ref[:, pair * 128:(pair + 1) * 128] = jnp.where(even, lo, hi)

    return pl.pallas_call(
        body, name="attn_fwd", grid=(nb,),
        in_specs=[pl.BlockSpec(memory_space=pltpu.SMEM), q_spec] + k_specs + v_specs,
        out_specs=pl.BlockSpec((BLK, D), lambda n: (n, 0)), out_shape=_sds((S, D), f32),
        compiler_params=_params(1, True))(sink, proj, proj, proj, proj, proj, proj, proj)


def _attn_bwd(proj, sink, y_b, dy_b):
    S = proj.shape[0]
    nb, q_spec, k_specs, v_specs = _attn_specs(S)

    def body(sink_ref, q_ref, kp_ref, kc_ref, kn_ref, vp_ref, vc_ref, vn_ref, o_ref, do_ref,
             dq_ref, dk_ref, dv_ref, dsink_ref):
        n = pl.program_id(0)

        @pl.when(n == 0)
        def _():
            dk_ref[...] = jnp.zeros_like(dk_ref)
            dv_ref[...] = jnp.zeros_like(dv_ref)
            dsink_ref[...] = jnp.zeros_like(dsink_ref)

        base = _attn_base(n, S)
        kcat = jnp.concatenate([kp_ref[...], kc_ref[...], kn_ref[...]], axis=0)
        vcat = jnp.concatenate([vp_ref[...], vc_ref[...], vn_ref[...]], axis=0)
        dk_acc = [jnp.zeros((3 * BLK, 128), f32), jnp.zeros((3 * BLK, 128), f32)]
        dv_acc = [jnp.zeros((3 * BLK, 128), f32), jnp.zeros((3 * BLK, 128), f32)]
        scale = HD ** -0.5
        even = _half_mask((BLK, 128), 0)
        for kvh in range(NH // 4):
            ch, off = kvh // 2, kvh % 2
            kb = _both_halves(kcat[:, ch * 128:(ch + 1) * 128], off).astype(bf16)
            vb = _both_halves(vcat[:, ch * 128:(ch + 1) * 128], off).astype(bf16)
            q_parts = _group_heads(q_ref, kvh, scale)
            d_parts = _group_heads(do_ref, kvh, 1.0)
            s4 = lax.dot_general(_stack_bf16(q_parts), kb, _DIMS["nt"], preferred_element_type=f32)
            dp4 = lax.dot_general(_stack_bf16(d_parts), vb, _DIMS["nt"], preferred_element_type=f32)
            ts, ps, qn, dn, invs = [], [], [], [], []
            for i in range(4):
                h = 4 * kvh + i
                pair = 2 * kvh + i // 2
                rows = slice(i * BLK, (i + 1) * BLK)
                p, inv, psink = _attn_softmax(s4[rows], base, _SLOPES[h], sink_ref[0, h])
                delta = jnp.sum(d_parts[i] * o_ref[:, pair * 128:(pair + 1) * 128], axis=-1, keepdims=True)
                dsink_ref[h:h + 1, :] += jnp.broadcast_to(-jnp.sum(psink * delta, axis=0, keepdims=True), (1, 128))
                ts.append(p * (dp4[rows] - delta))
                ps.append(p)
                qn.append(q_parts[i] * inv)
                dn.append(d_parts[i] * inv)
                invs.append(inv)
            t4 = _stack_bf16(ts)
            dq4 = jnp.dot(t4, kb, preferred_element_type=f32)
            for pr in range(2):
                lo = dq4[(2 * pr) * BLK:(2 * pr + 1) * BLK] * invs[2 * pr]
                hi = dq4[(2 * pr + 1) * BLK:(2 * pr + 2) * BLK] * invs[2 * pr + 1]
                pair = 2 * kvh + pr
                dq_ref[:, pair * 128:(pair + 1) * 128] = (jnp.where(even, lo, hi) * scale).astype(dq_ref.dtype)
            dk_both = lax.dot_general(t4, _stack_bf16(qn), _DIMS["tn"], preferred_element_type=f32)
            dv_both = lax.dot_general(_stack_bf16(ps), _stack_bf16(dn), _DIMS["tn"], preferred_element_type=f32)
            dk_acc[ch] = dk_acc[ch] + _fold_halves(dk_both, off)
            dv_acc[ch] = dv_acc[ch] + _fold_halves(dv_both, off)
        for j in range(3):
            blk = n + (j - 1)

            @pl.when((blk >= 0) & (blk < nb))
            def _():
                rows = pl.ds(pl.multiple_of(blk * BLK, BLK), BLK)
                for ch in range(2):
                    dk_ref[rows, ch * 128:(ch + 1) * 128] += dk_acc[ch][j * BLK:(j + 1) * BLK]
                    dv_ref[rows, ch * 128:(ch + 1) * 128] += dv_acc[ch][j * BLK:(j + 1) * BLK]

    row_blk = pl.BlockSpec((BLK, D), lambda n: (n, 0))
    full = pl.BlockSpec((S, 256), lambda n: (0, 0))
    return pl.pallas_call(
        body, name="attn_bwd", grid=(nb,),
        in_specs=[pl.BlockSpec(memory_space=pltpu.SMEM), q_spec] + k_specs + v_specs + [row_blk, row_blk],
        out_specs=[row_blk, full, full, pl.BlockSpec((NH, 128), lambda n: (0, 0))],
        out_shape=[_sds((S, D), bf16), _sds((S, 256), f32), _sds((S, 256), f32), _sds((NH, 128), f32)],
        compiler_params=_params(1, True))(sink, proj, proj, proj, proj, proj, proj, proj, y_b, dy_b)


def _adamw(name, w, g, m, v, tr):
    R, C = w.shape
    tr = min(tr, R)

    def body(w_ref, g_ref, m_ref, v_ref, d_ref, m2_ref, v2_ref):
        g = g_ref[...]
        m2 = ADAM_B1 * m_ref[...] + (1.0 - ADAM_B1) * g
        v2 = ADAM_B2 * v_ref[...] + (1.0 - ADAM_B2) * (g * g)
        m_hat = m2 / (1.0 - ADAM_B1 ** ADAM_STEP)
        v_hat = v2 / (1.0 - ADAM_B2 ** ADAM_STEP)
        d_ref[...] = -ADAM_LR * (m_hat / (jnp.sqrt(v_hat) + ADAM_EPS) + ADAM_WD * w_ref[...])
        m2_ref[...] = m2
        v2_ref[...] = v2

    blk = pl.BlockSpec((tr, C), lambda i: (i, 0))
    return pl.pallas_call(body, name=name, grid=(R // tr,), in_specs=[blk] * 4, out_specs=[blk] * 3,
                          out_shape=[_sds((R, C), f32)] * 3, compiler_params=_params(1))(w, g, m, v)


def _pair_sum(name, c_arr, g4, recv, th):
    _, _, h, w = g4.shape
    th = min(th, h)

    def body(c_ref, g_ref, r_ref, o_ref, ob_ref):
        p = g_ref[...] + r_ref[...]
        o_ref[...] = p
        ob_ref[...] = p.astype(bf16)

    blk = pl.BlockSpec((None, th, w), lambda s, i, c_ref: (s, i, 0))
    spec = pltpu.PrefetchScalarGridSpec(
        num_scalar_prefetch=1, grid=(NCHIP, h // th),
        in_specs=[pl.BlockSpec((None, None, th, w), lambda s, i, c_ref: (s, c_ref[0], i, 0)), blk],
        out_specs=[blk, blk])
    return pl.pallas_call(body, name=name, grid_spec=spec,
                          out_shape=[_sds((NCHIP, h, w), f32), _sds((NCHIP, h, w), bf16)],
                          compiler_params=_params(2))(c_arr, g4, recv)


def _chip_sum(name, chip_arr, own4, recv3, th):
    _, h, w = own4.shape
    th = min(th, h)

    def body(s_ref, o_ref, r_ref, out_ref):
        out_ref[...] = ((o_ref[...] + r_ref[0].astype(f32)) + r_ref[1].astype(f32)) + r_ref[2].astype(f32)

    spec = pltpu.PrefetchScalarGridSpec(
        num_scalar_prefetch=1, grid=(h // th,),
        in_specs=[pl.BlockSpec((None, th, w), lambda i, s_ref: (s_ref[0], i, 0)),
                  pl.BlockSpec((3, th, w), lambda i, s_ref: (0, i, 0))],
        out_specs=pl.BlockSpec((th, w), lambda i, s_ref: (i, 0)))
    return pl.pallas_call(body, name=name, grid_spec=spec, out_shape=_sds((h, w), f32),
                          compiler_params=_params(1, True))(chip_arr, own4, recv3)


def _adamw_halves(name, c_arr, w, g_own, g_recv, m, v, th):
    h, wd = g_own.shape
    th = min(th, h)

    def body(c_ref, w_ref, go_ref, gr_ref, m_ref, v_ref, g_ref, d_ref, m2_ref, v2_ref):
        g = jnp.where(c_ref[0] == pl.program_id(0), go_ref[...], gr_ref[...])
        m2 = ADAM_B1 * m_ref[...] + (1.0 - ADAM_B1) * g
        v2 = ADAM_B2 * v_ref[...] + (1.0 - ADAM_B2) * (g * g)
        m_hat = m2 / (1.0 - ADAM_B1 ** ADAM_STEP)
        v_hat = v2 / (1.0 - ADAM_B2 ** ADAM_STEP)
        g_ref[...] = g
        d_ref[...] = -ADAM_LR * (m_hat / (jnp.sqrt(v_hat) + ADAM_EPS) + ADAM_WD * w_ref[...])
        m2_ref[...] = m2
        v2_ref[...] = v2

    nt = h // th
    full = pl.BlockSpec((th, wd), lambda hh, i, c_ref: (hh * nt + i, 0))
    half = pl.BlockSpec((th, wd), lambda hh, i, c_ref: (i, 0))
    spec = pltpu.PrefetchScalarGridSpec(num_scalar_prefetch=1, grid=(2, nt),
                                        in_specs=[full, half, half, full, full], out_specs=[full] * 4)
    return pl.pallas_call(body, name=name, grid_spec=spec, out_shape=[_sds((2 * h, wd), f32)] * 4,
                          compiler_params=_params(2))(c_arr, w, g_own, g_recv, m, v)


def _add2(name, a, b):
    def body(a_ref, b_ref, o_ref):
        o_ref[...] = a_ref[...] + b_ref[...]
    return pl.pallas_call(body, name=name, out_shape=_sds(a.shape, f32))(a, b)


def _sum4(name, b4, th):
    _, h, w = b4.shape
    th = min(th, h)

    def body(b_ref, o_ref):
        o_ref[...] = ((b_ref[0] + b_ref[1]) + b_ref[2]) + b_ref[3]

    return pl.pallas_call(body, name=name, grid=(h // th,),
                          in_specs=[pl.BlockSpec((NCHIP, th, w), lambda i: (0, i, 0))],
                          out_specs=pl.BlockSpec((th, w), lambda i: (i, 0)), out_shape=_sds((h, w), f32),
                          compiler_params=_params(1, True))(b4)


def _coords():
    x, y, c = lax.axis_index("x"), lax.axis_index("y"), lax.axis_index("c")
    return x, y, c, [(1 - x, y), (x, 1 - y), (1 - x, 1 - y)]


def _gather_chips(arrs):
    n = len(arrs)

    def body(*refs):
        ins, outs = refs[:n], refs[n:2 * n]
        send_sems, recv_sems, local_sems = refs[2 * n:2 * n + 3]
        stage = refs[2 * n + 3:]
        x, y, c, chips = _coords()
        s = 2 * x + y
        sib = (x, y, 1 - c)
        load = [pltpu.make_async_copy(ins[a], stage[a], local_sems.at[a]) for a in range(n)]
        local = [pltpu.make_async_copy(stage[a], outs[a].at[s], local_sems.at[n + a]) for a in range(n)]
        for cp in load:
            cp.start()

        def over_ici(k, a, slot, peer):
            return pltpu.make_async_remote_copy(src_ref=ins[a].at[c], dst_ref=outs[a].at[slot, c], send_sem=send_sems.at[k * n + a],
                                                recv_sem=recv_sems.at[k * n + a], device_id=peer, device_id_type=MESH)

        def to_sibling(k, a, slot, half):
            i = (3 + k) * n + a
            return pltpu.make_async_remote_copy(src_ref=outs[a].at[slot, half], dst_ref=outs[a].at[slot, half], send_sem=send_sems.at[i],
                                                recv_sem=recv_sems.at[i], device_id=sib, device_id_type=MESH)

        sends = [over_ici(k, a, s, (px, py, c)) for k, (px, py) in enumerate(chips) for a in range(n)]
        for cp in sends:
            cp.start()
        for a in range(n):
            load[a].wait()
            local[a].start()
        passed = []
        for k, (px, py) in enumerate(chips):
            for a in range(n):
                over_ici(k, a, 2 * px + py, (px, py, c)).wait_recv()
                cp = to_sibling(k, a, 2 * px + py, c)
                cp.start()
                passed.append(cp)
        for k, (px, py) in enumerate(chips):
            for a in range(n):
                to_sibling(k, a, 2 * px + py, 1 - c).wait_recv()
        for cp in sends + passed:
            cp.wait_send()
        for cp in local:
            cp.wait()

    return pl.pallas_call(
        body, name="gather_weights", in_specs=[ANY] * n, out_specs=[ANY] * n,
        out_shape=[_sds((NCHIP,) + a.shape, a.dtype) for a in arrs],
        scratch_shapes=[pltpu.SemaphoreType.DMA((6 * n,)), pltpu.SemaphoreType.DMA((6 * n,)), pltpu.SemaphoreType.DMA((2 * n,))]
        + [pltpu.VMEM(a.shape, a.dtype) for a in arrs],
        compiler_params=pltpu.CompilerParams(vmem_limit_bytes=VMEM_LIMIT),
    )(*arrs)


HBM = pl.BlockSpec(memory_space=pltpu.HBM)
SEM = pl.BlockSpec(memory_space=pltpu.SEMAPHORE)
EFFECT = pltpu.SideEffectType.DATAFLOW_SIDE_EFFECTING


def _split_start(name, n_copies, make_copies, ins, land_shapes):
    ni, nl = len(ins), len(land_shapes)

    def body(*refs):
        in_refs, land_refs = refs[:ni], refs[ni:ni + nl]
        send_sems, recv_sems = refs[ni + nl], refs[ni + nl + 1]
        token = refs[-1]
        for cp in make_copies(in_refs, land_refs, send_sems, recv_sems):
            cp.start()
        token[...] = jnp.zeros_like(token)

    lands = [pltpu.with_memory_space_constraint(lax.empty(s.shape, s.dtype), pltpu.HBM) for s in land_shapes]
    res = pl.pallas_call(
        body, name=name,
        out_shape=(pltpu.SemaphoreType.DMA((n_copies,)), pltpu.SemaphoreType.DMA((n_copies,)),
                   *[pltpu.HBM(a.shape, a.dtype) for a in ins], *[pltpu.HBM(s.shape, s.dtype) for s in land_shapes],
                   _sds((8, 128), f32)),
        in_specs=[HBM] * (ni + nl), out_specs=(SEM, SEM, *[HBM] * (ni + nl), pl.BlockSpec(memory_space=pltpu.VMEM)),
        input_output_aliases={i: 2 + i for i in range(ni + nl)},
        compiler_params=pltpu.CompilerParams(has_side_effects=EFFECT),
    )(*[pltpu.with_memory_space_constraint(a, pltpu.HBM) for a in ins], *lands)
    return res[0], res[1], list(res[2:2 + ni]), list(res[2 + ni:2 + ni + nl]), res[-1]


def _split_wait(name, make_copies, send_sems, recv_sems, ins, lands, after):
    ni, nl = len(ins), len(lands)

    def body(*refs):
        in_refs, land_refs = refs[:ni], refs[ni:ni + nl]
        s_sems, r_sems = refs[ni + nl], refs[ni + nl + 1]
        for cp in make_copies(in_refs, land_refs, s_sems, r_sems):
            cp.wait_send()
            cp.wait_recv()

    res = pl.pallas_call(
        body, name=name, out_shape=tuple(pltpu.HBM(a.shape, a.dtype) for a in ins + lands),
        in_specs=[HBM] * (ni + nl) + [SEM, SEM, ANY], out_specs=tuple([HBM] * (ni + nl)),
        input_output_aliases={i: i for i in range(ni + nl)},
        compiler_params=pltpu.CompilerParams(has_side_effects=EFFECT),
    )(*ins, *lands, send_sems, recv_sems, after)
    return list(res[:ni]), list(res[ni:])


def _gather_copies(n):
    def make(in_refs, land_refs, send_sems, recv_sems):
        x, y, c, chips = _coords()
        s = 2 * x + y
        return [pltpu.make_async_remote_copy(src_ref=in_refs[a], dst_ref=land_refs[a].at[s], send_sem=send_sems.at[k * n + a],
                                             recv_sem=recv_sems.at[k * n + a], device_id=(px, py, c), device_id_type=MESH)
                for k, (px, py) in enumerate(chips) for a in range(n)]
    return make


def _sibling_half_copies(n):
    def make(in_refs, land_refs, send_sems, recv_sems):
        x, y, c, _ = _coords()
        return [pltpu.make_async_remote_copy(src_ref=in_refs[a].at[:, 1 - c], dst_ref=land_refs[a], send_sem=send_sems.at[a],
                                             recv_sem=recv_sems.at[a], device_id=(x, y, 1 - c), device_id_type=MESH)
                for a in range(n)]
    return make


def _chip_part_copies(n):
    def make(in_refs, land_refs, send_sems, recv_sems):
        x, y, c, chips = _coords()
        return [pltpu.make_async_remote_copy(src_ref=in_refs[a].at[2 * px + py], dst_ref=land_refs[a].at[k],
                                             send_sem=send_sems.at[k * n + a], recv_sem=recv_sems.at[k * n + a],
                                             device_id=(px, py, c), device_id_type=MESH)
                for k, (px, py) in enumerate(chips) for a in range(n)]
    return make


def _sibling_whole_copies(n):
    def make(in_refs, land_refs, send_sems, recv_sems):
        x, y, c, _ = _coords()
        return [pltpu.make_async_remote_copy(src_ref=in_refs[a], dst_ref=land_refs[a], send_sem=send_sems.at[a],
                                             recv_sem=recv_sems.at[a], device_id=(x, y, 1 - c), device_id_type=MESH)
                for a in range(n)]
    return make


def _place_own(chip_arr, owns, lands, steps):
    n = len(owns)

    def body(s_ref, *refs):
        for a in range(n):
            refs[2 * n + a][...] = refs[a][...]

    tiles = [o.shape[0] // steps for o in owns]
    spec = pltpu.PrefetchScalarGridSpec(
        num_scalar_prefetch=1, grid=(steps,),
        in_specs=[pl.BlockSpec((t, o.shape[1]), lambda i, s_ref: (i, 0)) for t, o in zip(tiles, owns)] + [ANY] * n,
        out_specs=[pl.BlockSpec((None, t, o.shape[1]), lambda i, s_ref: (s_ref[0], i, 0)) for t, o in zip(tiles, owns)])
    return pl.pallas_call(body, name="place_own", grid_spec=spec, out_shape=[_sds(l.shape, l.dtype) for l in lands],
                          input_output_aliases={1 + n + a: a for a in range(n)},
                          compiler_params=_params(1))(chip_arr, *owns, *lands)


def _sibling_halves(g4s, small):
    n = len(g4s)

    def body(*refs):
        ins, small_ref = refs[:n], refs[n]
        outs, small_out = refs[n + 1:2 * n + 1], refs[2 * n + 1]
        send_sems, recv_sems = refs[2 * n + 2:]
        x, y, c, _ = _coords()
        sib = (x, y, 1 - c)

        def remote(a, half):
            src = small_ref if a == n else ins[a].at[:, half]
            dst = small_out if a == n else outs[a]
            return pltpu.make_async_remote_copy(src_ref=src, dst_ref=dst, send_sem=send_sems.at[a], recv_sem=recv_sems.at[a],
                                                device_id=sib, device_id_type=MESH)

        sends = [remote(a, 1 - c) for a in range(n + 1)]
        for cp in sends:
            cp.start()
        for a in range(n + 1):
            remote(a, c).wait_recv()
        for cp in sends:
            cp.wait_send()

    return pl.pallas_call(
        body, name="reduce_sibling", in_specs=[ANY] * (n + 1), out_specs=[ANY] * (n + 1),
        out_shape=[_sds((g.shape[0],) + g.shape[2:], f32) for g in g4s] + [_sds(small.shape, f32)],
        scratch_shapes=[pltpu.SemaphoreType.DMA((n + 1,)), pltpu.SemaphoreType.DMA((n + 1,))],
    )(*g4s, small)


def _exchange_chips(parts, small2):
    n = len(parts)

    def body(*refs):
        ins, small_ref = refs[:n], refs[n]
        outs, small_out = refs[n + 1:2 * n + 1], refs[2 * n + 1]
        send_sems, recv_sems, local_sem = refs[2 * n + 2:]
        x, y, c, chips = _coords()
        s = 2 * x + y
        local = pltpu.make_async_copy(small_ref.at[c], small_out.at[s], local_sem)
        local.start()

        def remote(k, a, dest_chip, small_slot, peer):
            if a == n:
                src, dst = small_ref.at[c], small_out.at[small_slot]
            else:
                src, dst = ins[a].at[dest_chip], outs[a].at[k]
            i = k * (n + 1) + a
            return pltpu.make_async_remote_copy(src_ref=src, dst_ref=dst, send_sem=send_sems.at[i], recv_sem=recv_sems.at[i],
                                                device_id=peer, device_id_type=MESH)

        sends = [remote(k, a, 2 * px + py, s, (px, py, c)) for k, (px, py) in enumerate(chips) for a in range(n + 1)]
        for cp in sends:
            cp.start()
        for k, (px, py) in enumerate(chips):
            for a in range(n + 1):
                remote(k, a, s, 2 * px + py, (px, py, c)).wait_recv()
        for cp in sends:
            cp.wait_send()
        local.wait()

    m = 3 * (n + 1)
    return pl.pallas_call(
        body, name="reduce_chips", in_specs=[ANY] * (n + 1), out_specs=[ANY] * (n + 1),
        out_shape=[_sds((3,) + p.shape[1:], p.dtype) for p in parts] + [_sds((NCHIP,) + small2.shape[1:], f32)],
        scratch_shapes=[pltpu.SemaphoreType.DMA((m,)), pltpu.SemaphoreType.DMA((m,)), pltpu.SemaphoreType.DMA],
    )(*parts, small2)


def _share_sibling(halves):
    n = len(halves)

    def body(*refs):
        ins, outs = refs[:n], refs[n:2 * n]
        send_sems, recv_sems = refs[2 * n:]
        x, y, c, _ = _coords()
        sib = (x, y, 1 - c)
        sends = [pltpu.make_async_remote_copy(src_ref=ins[a], dst_ref=outs[a], send_sem=send_sems.at[a], recv_sem=recv_sems.at[a],
                                              device_id=sib, device_id_type=MESH) for a in range(n)]
        for cp in sends:
            cp.start()
        for cp in sends:
            cp.wait()

    return pl.pallas_call(
        body, name="reduce_share", in_specs=[ANY] * n, out_specs=[ANY] * n,
        out_shape=[_sds(h.shape, f32) for h in halves],
        scratch_shapes=[pltpu.SemaphoreType.DMA((n,)), pltpu.SemaphoreType.DMA((n,))],
    )(*halves)


def _block_diag_pairs(w):
    w = w.reshape(NCH, 2, HD, HD)
    z = jnp.zeros((NCH, HD, HD), w.dtype)
    return jnp.concatenate([jnp.concatenate([w[:, 0], z], axis=2), jnp.concatenate([z, w[:, 1]], axis=2)], axis=1)


def _diag_blocks(m):
    return jnp.stack([m[:, :HD, :HD], m[:, HD:, HD:]], axis=1).reshape(NH, HD, HD)


def _pack(vs, rows):
    flat = jnp.concatenate([v.reshape(-1) for v in vs])
    return jnp.pad(flat, (0, rows * 128 - flat.shape[0])).reshape(rows, 128)


def _unpack(packed, shapes):
    flat = packed.reshape(-1)
    out, off = [], 0
    for shp in shapes:
        size = math.prod(shp)
        out.append(flat[off:off + size].reshape(shp))
        off += size
    return out


def _rows_for(sizes, multiple):
    rows = -(-sum(sizes) // 128)
    return -(-rows // multiple) * multiple


def kernel(x, norm_mix_g, w_in, b_gate, conv_w, conv_b, lru_lambda, lru_wa, lru_ba, lru_wx, lru_bx, attn_sink, w_out, norm_ffn_g, w_ffn_in, w_ffn_out, norm_final_g, loss_target, m_norm_mix_g, m_w_in, m_b_gate, m_conv_w, m_conv_b, m_lru_lambda, m_lru_wa, m_lru_ba, m_lru_wx, m_lru_bx, m_attn_sink, m_w_out, m_norm_ffn_g, m_w_ffn_in, m_w_ffn_out, m_norm_final_g, v_norm_mix_g, v_w_in, v_b_gate, v_conv_w, v_conv_b, v_lru_lambda, v_lru_wa, v_lru_ba, v_lru_wx, v_lru_bx, v_attn_sink, v_w_out, v_norm_ffn_g, v_w_ffn_in, v_w_ffn_out, v_norm_final_g):
    S = x.shape[1]
    xs = x[0]
    tgt = loss_target[0]
    cx, cy, cc = lax.axis_index("x"), lax.axis_index("y"), lax.axis_index("c")
    chip = 2 * cx + cy
    SW = D // NCHIP

    small_shard = _pack([conv_w[0], lru_lambda[0], lru_ba[0], lru_bx[0]], 32)
    halves_of = lambda a: a.reshape(2, a.shape[0] // 2, a.shape[1])
    w_in_g, small_g = _gather_chips([halves_of(w_in[0].astype(bf16)), halves_of(small_shard)])
    w_in_g = w_in_g.reshape(NCHIP, D, SHW)
    small_g = small_g.reshape(NCHIP, 32, 128)
    late = [w_ffn_in[0].astype(bf16), w_out[0].astype(bf16), w_ffn_out[0].astype(bf16)]
    late_send, late_recv, late_src, late_land, late_token = _split_start(
        "gather_late_start", 9, _gather_copies(3), late, [_sds((NCHIP,) + a.shape, bf16) for a in late])
    small_parts = [_unpack(small_g[s], [(4, SW), (2, SW), (2, SW), (2, SW)]) for s in range(NCHIP)]
    conv_w_f, lam_f, ba_f, bx_f = [jnp.concatenate([small_parts[s][p] for s in range(NCHIP)], axis=1) for p in range(4)]
    wbd = jnp.concatenate([_block_diag_pairs(lru_wa[0, 0]), _block_diag_pairs(lru_wx[0, 0]),
                           _block_diag_pairs(lru_wa[0, 1]), _block_diag_pairs(lru_wx[0, 1])], axis=2).astype(bf16)
    conv_b_f = conv_b
    sink = attn_sink

    xn, proj = _rms_matmul("rms_proj", xs, norm_mix_g + late_token[0:1, 0:1], w_in_g, 1024)
    y_a, lru_state = _lru_fwd(proj, conv_w_f, conv_b_f, lam_f, ba_f, bx_f, wbd)
    y_b = _attn_fwd(proj, sink)
    merged = _merge_fwd(proj, b_gate, y_a, y_b, 512)
    late_src, late_land = _split_wait("gather_late_wait", _gather_copies(3), late_send, late_recv, late_src, late_land, merged)
    chip_arr = chip.reshape(1).astype(jnp.int32)
    w_ffn_in_g, w_out_g, w_ffn_out_g = _place_own(chip_arr, late_src, late_land, 4)
    w_out_f = w_out_g.reshape(D, D)
    w_ffn_out_f = w_ffn_out_g.reshape(FF, D)
    x1 = _mm_residual("out_proj", merged, w_out_f, xs, 512)
    xn2, gu, act = _rms_matmul_swiglu("rms_ffn_in", x1, norm_ffn_g, w_ffn_in_g, 1024)
    x2 = _mm_residual("ffn_out", act, w_ffn_out_f, x1, 512)
    dx2, loss_row, dg3 = _final_loss_bwd(x2, norm_final_g.reshape(1, D), tgt, 256)

    tm = min(1024, S)
    tk = min(2048, S)
    gw_ffn_out = _mm_tn("dw_ffn_out", act, pl.BlockSpec((tk, SHW), lambda i, k: (k, i)),
                        dx2, pl.BlockSpec((tk, D), lambda i, k: (k, 0)),
                        _sds((FF, D), f32), pl.BlockSpec((SHW, D), lambda i, k: (i, 0)), (2, S // tk), (SHW, D))
    dgu = _swiglu_bwd(dx2, w_ffn_out_f, gu, 256)
    dxn2 = _mm_nt_groups("dxn2", dgu, pl.BlockSpec((None, tm, SHW), lambda i, g: (g // 2, i, g % 2)), w_ffn_in_g, S, tm)
    gw_ffn_in = _mm_tn("dw_ffn_in", xn2, pl.BlockSpec((tk, D), lambda g, k: (k, 0)),
                       dgu, pl.BlockSpec((None, tk, SHW), lambda g, k: (g // 2, k, g % 2)),
                       _sds((NCHIP, D, SHW), f32), pl.BlockSpec((None, D, SHW), lambda g, k: (g, 0, 0)),
                       (NCHIP, S // tk), (D, SHW))
    c_arr = cc.reshape(1).astype(jnp.int32)
    early_names, early_tiles = ["w_ffn_in", "w_ffn_out"], [256, 352]
    early = [gw_ffn_in.reshape(NCHIP, 2, D // 2, SHW), gw_ffn_out.reshape(NCHIP, 2, FF // NCHIP // 2, D)]
    ea_send, ea_recv, ea_src, ea_land, ea_token = _split_start(
        "reduce_early_sibling_start", 2, _sibling_half_copies(2), early,
        [_sds((NCHIP,) + g.shape[2:], f32) for g in early])
    dx1, dg2 = _rms_bwd("rms_ffn_bwd", x1, norm_ffn_g + ea_token[0:1, 0:1], dxn2, dx2, 256)

    dmerged = _mm_nt_resident("d_merged", dx1, w_out_f, 512)
    gw_out = _mm_tn("dw_out", merged, pl.BlockSpec((tk, D), lambda i, k: (k, 0)),
                    dx1, pl.BlockSpec((tk, D), lambda i, k: (k, 0)),
                    _sds((D, D), f32), pl.BlockSpec((D, D), lambda i, k: (0, 0)), (1, S // tk), (D, D))
    dz0, dz1, dy_a, dy_b, db0, db1 = _merge_bwd(proj, b_gate, y_a, y_b, dmerged, 512)
    ea_src, ea_land = _split_wait("reduce_early_sibling_wait", _sibling_half_copies(2), ea_send, ea_recv, ea_src, ea_land, dy_b)
    early_pairs = [_pair_sum("pair_sum_" + nm, c_arr, g4, r, th)
                   for nm, g4, r, th in zip(early_names, ea_src, ea_land, early_tiles)]
    eb_send, eb_recv, eb_src, eb_land, eb_token = _split_start(
        "reduce_early_chips_start", 6, _chip_part_copies(2), [p[1] for p in early_pairs],
        [_sds((3,) + p[1].shape[1:], bf16) for p in early_pairs])
    dq, dk, dv, dsink = _attn_bwd(proj, sink + eb_token[0:1, 0:1], y_b, dy_b)
    _, eb_land = _split_wait("reduce_early_chips_wait", _chip_part_copies(2), eb_send, eb_recv, eb_src, eb_land, dq)
    early_halves = [_chip_sum("chip_sum_" + nm, chip_arr, p[0], r3, th)
                    for nm, p, r3, th in zip(early_names, early_pairs, eb_land, early_tiles)]
    ec_send, ec_recv, ec_src, ec_land, ec_token = _split_start(
        "reduce_early_share_start", 2, _sibling_whole_copies(2), early_halves, [_sds(h.shape, f32) for h in early_halves])
    du, dgl, dcw, dcb, dlam, dba, dbx, dwbd = _lru_bwd(proj, dy_a, lru_state, conv_w_f, conv_b_f + ec_token[0:1, 0:1], lam_f, ba_f, bx_f, wbd)
    early_halves, early_other = _split_wait("reduce_early_share_wait", _sibling_whole_copies(2), ec_send, ec_recv, ec_src, ec_land, du)
    dproj = jnp.concatenate([du, dgl, dq, dk.astype(bf16), dv.astype(bf16), dz0, dz1], axis=1)
    dxn = _mm_nt_groups("dxn", dproj, pl.BlockSpec((tm, SHW), lambda i, g: (i, g)), w_in_g, S, tm)
    gw_in = _mm_tn("dw_in", xn, pl.BlockSpec((tk, D), lambda g, k: (k, 0)),
                   dproj, pl.BlockSpec((tk, SHW), lambda g, k: (k, g)),
                   _sds((NCHIP, D, SHW), f32), pl.BlockSpec((None, D, SHW), lambda g, k: (g, 0, 0)),
                   (NCHIP, S // tk), (D, SHW))
    grad_x, dg1 = _rms_bwd("rms_mix_bwd", xs, norm_mix_g, dxn, dx1, 256)

    d_wa = jnp.stack([_diag_blocks(dwbd[:, :, 0:CW]), _diag_blocks(dwbd[:, :, 2 * CW:3 * CW])])
    d_wx = jnp.stack([_diag_blocks(dwbd[:, :, CW:2 * CW]), _diag_blocks(dwbd[:, :, 3 * CW:4 * CW])])
    small_full = [dg1, jnp.concatenate([db0, db1], axis=1), dcw, dcb, dlam, d_wa, dba, d_wx, dbx, dsink[:, 0], dg2, dg3,
                  loss_row[0, 0:1]]
    full_shapes = [(1, D), (1, 2 * D), (4, D), (1, D), (2, D), (2, NH, HD, HD), (2, D), (2, NH, HD, HD), (2, D), (NH,),
                   (1, D), (1, D), (1,)]
    rows_full = _rows_for([math.prod(s) for s in full_shapes], 16)
    small_vec = _pack(small_full, rows_full)

    late_names, late_tiles = ["w_in", "w_out"], [256, 128]
    big = [gw_in.reshape(NCHIP, 2, D // 2, SHW), gw_out.reshape(NCHIP, 2, D // NCHIP // 2, D)]
    *recv_a, small_sib = _sibling_halves(big, small_vec)
    pairs = [_pair_sum("pair_sum_" + nm, c_arr, g4, r, th) for nm, g4, r, th in zip(late_names, big, recv_a, late_tiles)]
    small_chip = _add2("pair_sum_small", small_vec, small_sib).reshape(2, rows_full // 2, 128)
    *recv_b, small_all = _exchange_chips([p[1] for p in pairs], small_chip)
    halves = [_chip_sum("chip_sum_" + nm, chip_arr, p[0], r3, th) for nm, p, r3, th in zip(late_names, pairs, recv_b, late_tiles)]
    halves.append(_sum4("chip_sum_small", small_all, rows_full // 2))
    *recv_c, small_other = _share_sibling(halves)
    small_lo = jnp.where(cc == 0, halves[2], small_other)
    small_hi = jnp.where(cc == 0, small_other, halves[2])
    g_full = _unpack(jnp.concatenate([small_lo, small_hi], axis=0), full_shapes)

    out_big = {}
    for nm, w, g_own, g_recv, m, v, th in zip(late_names + early_names, [w_in, w_out, w_ffn_in, w_ffn_out],
                                              halves[:2] + early_halves, recv_c + early_other,
                                              [m_w_in, m_w_out, m_w_ffn_in, m_w_ffn_out],
                                              [v_w_in, v_w_out, v_w_ffn_in, v_w_ffn_out], late_tiles + early_tiles):
        g_, d_, m_, v_ = _adamw_halves("adamw_" + nm, c_arr, w[0], g_own, g_recv, m[0], v[0], th)
        out_big[nm] = (g_[None], d_[None], m_[None], v_[None])

    small_names = ["norm_mix_g", "b_gate", "conv_w", "conv_b", "lru_lambda", "lru_wa", "lru_ba", "lru_wx", "lru_bx", "attn_sink",
                   "norm_ffn_g", "norm_final_g"]
    sharded = {"conv_w", "lru_lambda", "lru_ba", "lru_bx"}
    small_w = [norm_mix_g, b_gate, conv_w, conv_b, lru_lambda, lru_wa, lru_ba, lru_wx, lru_bx, attn_sink, norm_ffn_g, norm_final_g]
    small_m = [m_norm_mix_g, m_b_gate, m_conv_w, m_conv_b, m_lru_lambda, m_lru_wa, m_lru_ba, m_lru_wx, m_lru_bx, m_attn_sink,
               m_norm_ffn_g, m_norm_final_g]
    small_v = [v_norm_mix_g, v_b_gate, v_conv_w, v_conv_b, v_lru_lambda, v_lru_wa, v_lru_ba, v_lru_wx, v_lru_bx, v_attn_sink,
               v_norm_ffn_g, v_norm_final_g]
    g_local = []
    for nm, g, w in zip(small_names, g_full, small_w):
        if nm in sharded:
            g = lax.dynamic_slice_in_dim(g, chip * SW, SW, axis=1)
        g_local.append(g.reshape(w.shape))
    local_shapes = [w.shape for w in small_w]
    rows_local = _rows_for([math.prod(s) for s in local_shapes], 8)
    d_s, m_s, v_s = _adamw("adamw_small", _pack(small_w, rows_local), _pack(g_local, rows_local),
                           _pack(small_m, rows_local), _pack(small_v, rows_local), rows_local)
    d_l, m_l, v_l = _unpack(d_s, local_shapes), _unpack(m_s, local_shapes), _unpack(v_s, local_shapes)
    res = {nm: (g_local[i], d_l[i], m_l[i], v_l[i]) for i, nm in enumerate(small_names)}
    res.update(out_big)

    order = ["norm_mix_g", "w_in", "b_gate", "conv_w", "conv_b", "lru_lambda", "lru_wa", "lru_ba", "lru_wx", "lru_bx", "attn_sink",
             "w_out", "norm_ffn_g", "w_ffn_in", "w_ffn_out", "norm_final_g"]
    outs = [g_full[-1][0], grad_x[None]]
    for k in range(4):
        outs += [res[nm][k] for nm in order]
    return tuple(outs)
```

```python
import functools
import math

import jax
import jax.numpy as jnp
from jax import lax
from jax.experimental import pallas as pl
from jax.experimental.pallas import tpu as pltpu

f32 = jnp.float32
bf16 = jnp.bfloat16

D = 1024
NH = 16
HD = 64
FF = 2816
INW = 5632
NCHIP = 4
SHW = INW // NCHIP
CW = 128
NCH = D // CW
BLK = 128
EPS = 1e-6
NEG_INF = -1e30
RGLRU_C = 8.0
ADAM_LR, ADAM_B1, ADAM_B2, ADAM_EPS, ADAM_WD, ADAM_STEP = 0.001, 0.9, 0.999, 1e-08, 0.01, 10
VMEM_LIMIT = 58 * 1024 * 1024
MESH = pl.DeviceIdType.MESH
ANY = pl.BlockSpec(memory_space=pl.ANY)

COL_U, COL_G, COL_Q, COL_K, COL_V, COL_Z0, COL_Z1 = 0, 4, 8, 12, 13, 14, 18


def _params(n_axes, vmem=False):
    return pltpu.CompilerParams(dimension_semantics=("arbitrary",) * n_axes,
                                vmem_limit_bytes=VMEM_LIMIT if vmem else None)


def _sds(shape, dtype):
    return jax.ShapeDtypeStruct(tuple(shape), dtype)


_DIMS = {"nn": (((1,), (0,)), ((), ())), "nt": (((1,), (1,)), ((), ())), "tn": (((0,), (0,)), ((), ()))}


def _mm(name, mode, a, a_spec, b, b_spec, out_shape, out_spec, grid, nk, acc_shape, add=None, add_spec=None):
    has_add = add is not None

    def body(*refs):
        a_ref, b_ref = refs[0], refs[1]
        add_ref = refs[2] if has_add else None
        o_ref = refs[2 + has_add]
        part = lax.dot_general(a_ref[...].astype(bf16), b_ref[...].astype(bf16), _DIMS[mode],
                               preferred_element_type=f32)
        if nk == 1:
            if has_add:
                part = add_ref[...] + part
            o_ref[...] = part.astype(o_ref.dtype)
            return
        acc_ref = refs[3 + has_add]
        k = pl.program_id(len(grid) - 1)

        @pl.when(k == 0)
        def _():
            acc_ref[...] = part

        @pl.when(k > 0)
        def _():
            acc_ref[...] += part

        @pl.when(k == nk - 1)
        def _():
            res = acc_ref[...]
            if has_add:
                res = add_ref[...] + res
            o_ref[...] = res.astype(o_ref.dtype)

    ins = [a, b] + ([add] if has_add else [])
    in_specs = [a_spec, b_spec] + ([add_spec] if has_add else [])
    scratch = [pltpu.VMEM(acc_shape, f32)] if nk > 1 else []
    return pl.pallas_call(body, name=name, grid=grid, in_specs=in_specs, out_specs=out_spec, out_shape=out_shape,
                          scratch_shapes=scratch, compiler_params=_params(len(grid), True))(*ins)


def _rms_matmul(name, x, g, w3, tm):
    S, K = x.shape
    G, _, Nw = w3.shape
    tm = min(tm, S)

    def body(x_ref, g_ref, w_ref, xn_ref, o_ref, xs_ref):
        @pl.when(pl.program_id(1) == 0)
        def _():
            xf = x_ref[...]
            r = lax.rsqrt(jnp.mean(xf * xf, axis=-1, keepdims=True) + EPS)
            xn = ((xf * r) * g_ref[...]).astype(bf16)
            xs_ref[...] = xn
            xn_ref[...] = xn

        o_ref[...] = jnp.dot(xs_ref[...], w_ref[...], preferred_element_type=f32).astype(bf16)

    return pl.pallas_call(
        body, name=name, grid=(S // tm, G),
        in_specs=[pl.BlockSpec((tm, K), lambda i, j: (i, 0)), pl.BlockSpec((1, K), lambda i, j: (0, 0)),
                  pl.BlockSpec((None, K, Nw), lambda i, j: (j, 0, 0))],
        out_specs=[pl.BlockSpec((tm, K), lambda i, j: (i, 0)), pl.BlockSpec((tm, Nw), lambda i, j: (i, j))],
        out_shape=[_sds((S, K), bf16), _sds((S, G * Nw), bf16)],
        scratch_shapes=[pltpu.VMEM((tm, K), bf16)], compiler_params=_params(2, True))(x, g, w3)


def _rms_matmul_swiglu(name, x, g, w3, tm):
    S, K = x.shape
    G, _, Nw = w3.shape
    tm = min(tm, S)
    half = G // 2

    def body(x_ref, g_ref, wg_ref, wu_ref, xn_ref, gu_ref, act_ref, xs_ref):
        @pl.when(pl.program_id(1) == 0)
        def _():
            xf = x_ref[...]
            r = lax.rsqrt(jnp.mean(xf * xf, axis=-1, keepdims=True) + EPS)
            xn = ((xf * r) * g_ref[...]).astype(bf16)
            xs_ref[...] = xn
            xn_ref[...] = xn

        xn = xs_ref[...]
        gate = jnp.dot(xn, wg_ref[...], preferred_element_type=f32)
        up = jnp.dot(xn, wu_ref[...], preferred_element_type=f32)
        gu_ref[0] = gate.astype(bf16)
        gu_ref[1] = up.astype(bf16)
        act_ref[...] = ((gate * _sigmoid(gate)) * up).astype(bf16)

    return pl.pallas_call(
        body, name=name, grid=(S // tm, half),
        in_specs=[pl.BlockSpec((tm, K), lambda i, j: (i, 0)), pl.BlockSpec((1, K), lambda i, j: (0, 0)),
                  pl.BlockSpec((None, K, Nw), lambda i, j: (j, 0, 0)),
                  pl.BlockSpec((None, K, Nw), lambda i, j: (half + j, 0, 0))],
        out_specs=[pl.BlockSpec((tm, K), lambda i, j: (i, 0)), pl.BlockSpec((2, tm, Nw), lambda i, j: (0, i, j)),
                   pl.BlockSpec((tm, Nw), lambda i, j: (i, j))],
        out_shape=[_sds((S, K), bf16), _sds((2, S, half * Nw), bf16), _sds((S, half * Nw), bf16)],
        scratch_shapes=[pltpu.VMEM((tm, K), bf16)], compiler_params=_params(2, True))(x, g, w3, w3)


def _mm_residual(name, a, w, res, tm):
    S, K = a.shape
    N = w.shape[1]
    tm = min(tm, S)
    return _mm(name, "nn", a, pl.BlockSpec((tm, K), lambda i: (i, 0)), w, pl.BlockSpec((K, N), lambda i: (0, 0)),
               _sds((S, N), f32), pl.BlockSpec((tm, N), lambda i: (i, 0)), (S // tm,), 1, None,
               add=res, add_spec=pl.BlockSpec((tm, N), lambda i: (i, 0)))


def _mm_nt_resident(name, a, w, tm):
    S, K = a.shape
    N = w.shape[0]
    tm = min(tm, S)
    return _mm(name, "nt", a, pl.BlockSpec((tm, K), lambda i: (i, 0)), w, pl.BlockSpec((N, K), lambda i: (0, 0)),
               _sds((S, N), f32), pl.BlockSpec((tm, N), lambda i: (i, 0)), (S // tm,), 1, None)


def _mm_nt_groups(name, a, a_spec, w3, S, tm):
    G, Dout, Kw = w3.shape
    return _mm(name, "nt", a, a_spec, w3, pl.BlockSpec((None, Dout, Kw), lambda i, g: (g, 0, 0)),
               _sds((S, Dout), f32), pl.BlockSpec((tm, Dout), lambda i, g: (i, 0)), (S // tm, G), G, (tm, Dout))


def _mm_tn(name, a, a_spec, b, b_spec, out_shape, out_spec, grid, acc_shape):
    return _mm(name, "tn", a, a_spec, b, b_spec, out_shape, out_spec, grid, grid[-1], acc_shape)


def _sigmoid(x):
    return 0.5 * jnp.tanh(0.5 * x) + 0.5


_GELU_C = math.sqrt(2.0 / math.pi)


def _gelu_and_grad(x):
    v = _GELU_C * (x + 0.044715 * (x * x * x))
    t = jnp.tanh(v)
    gl = 0.5 * x * (1.0 + t)
    dgl = 0.5 * (1.0 + t) + 0.5 * x * (1.0 - t * t) * (_GELU_C * (1.0 + 3.0 * 0.044715 * (x * x)))
    return gl, dgl


def _one_minus_exp2x(x, ex):
    y = 2.0 * x
    series = y * (1.0 + y * (0.5 + y * (1.0 / 6.0 + y * (1.0 / 24.0))))
    return jnp.where(y > -1.0 / 64.0, -series, 1.0 - ex * ex)


def _merge_fwd(proj, b_gate, y_a, y_b, tm):
    S = proj.shape[0]
    tm = min(tm, S)

    def body(z0_ref, z1_ref, b0_ref, b1_ref, ya_ref, yb_ref, o_ref):
        g0 = _sigmoid(z0_ref[...].astype(f32) + b0_ref[...])
        g1 = _sigmoid(z1_ref[...].astype(f32) + b1_ref[...])
        o_ref[...] = (g0 * ya_ref[...].astype(f32) + g1 * yb_ref[...].astype(f32)).astype(bf16)

    blk = lambda off: pl.BlockSpec((tm, 256), lambda j, i: (i, off + j))
    vec = lambda off: pl.BlockSpec((1, 256), lambda j, i: (0, off + j))
    return pl.pallas_call(body, name="merge_fwd", grid=(4, S // tm),
                          in_specs=[blk(COL_Z0), blk(COL_Z1), vec(0), vec(4), blk(0), blk(0)],
                          out_specs=blk(0), out_shape=_sds((S, D), bf16),
                          compiler_params=_params(2))(proj, proj, b_gate, b_gate, y_a, y_b)


def _merge_bwd(proj, b_gate, y_a, y_b, dm, tm):
    S = proj.shape[0]
    tm = min(tm, S)

    def body(z0_ref, z1_ref, b0_ref, b1_ref, ya_ref, yb_ref, dm_ref, dz0_ref, dz1_ref, dya_ref, dyb_ref, db0_ref, db1_ref):
        g0 = _sigmoid(z0_ref[...].astype(f32) + b0_ref[...])
        g1 = _sigmoid(z1_ref[...].astype(f32) + b1_ref[...])
        d = dm_ref[...]
        dz0 = (d * ya_ref[...].astype(f32)) * (g0 * (1.0 - g0))
        dz1 = (d * yb_ref[...].astype(f32)) * (g1 * (1.0 - g1))
        dz0_ref[...] = dz0.astype(bf16)
        dz1_ref[...] = dz1.astype(bf16)
        dya_ref[...] = (d * g0).astype(bf16)
        dyb_ref[...] = (d * g1).astype(bf16)

        @pl.when(pl.program_id(1) == 0)
        def _():
            db0_ref[...] = jnp.zeros_like(db0_ref)
            db1_ref[...] = jnp.zeros_like(db1_ref)

        db0_ref[...] += jnp.sum(dz0, axis=0, keepdims=True)
        db1_ref[...] += jnp.sum(dz1, axis=0, keepdims=True)

    blk = lambda off: pl.BlockSpec((tm, 256), lambda j, i: (i, off + j))
    vec = lambda off: pl.BlockSpec((1, 256), lambda j, i: (0, off + j))
    return pl.pallas_call(
        body, name="merge_bwd", grid=(4, S // tm),
        in_specs=[blk(COL_Z0), blk(COL_Z1), vec(0), vec(4), blk(0), blk(0), blk(0)],
        out_specs=[blk(0), blk(0), blk(0), blk(0), vec(0), vec(0)],
        out_shape=[_sds((S, D), bf16), _sds((S, D), bf16), _sds((S, D), bf16), _sds((S, D), bf16),
                   _sds((1, D), f32), _sds((1, D), f32)],
        compiler_params=_params(2))(proj, proj, b_gate, b_gate, y_a, y_b, dm)


def _swiglu_bwd(dx, w, gu, tm):
    S, K = dx.shape
    tm = min(tm, S)

    def body(dx_ref, w_ref, gu_ref, o_ref):
        d = lax.dot_general(dx_ref[...].astype(bf16), w_ref[...], _DIMS["nt"], preferred_element_type=f32)
        g = gu_ref[0].astype(f32)
        u = gu_ref[1].astype(f32)
        s = _sigmoid(g)
        o_ref[0] = ((d * u) * (s * (1.0 + g * (1.0 - s)))).astype(bf16)
        o_ref[1] = (d * (g * s)).astype(bf16)

    stacked = pl.BlockSpec((2, tm, FF), lambda i: (0, i, 0))
    return pl.pallas_call(body, name="swiglu_bwd", grid=(S // tm,),
                          in_specs=[pl.BlockSpec((tm, K), lambda i: (i, 0)), pl.BlockSpec((FF, K), lambda i: (0, 0)), stacked],
                          out_specs=stacked, out_shape=_sds((2, S, FF), bf16),
                          compiler_params=_params(1, True))(dx, w, gu)


def _final_loss_bwd(x2, g3, tgt, tm):
    S = x2.shape[0]
    tm = min(tm, S)

    def body(x_ref, g_ref, t_ref, dx_ref, loss_ref, dg_ref):
        @pl.when(pl.program_id(0) == 0)
        def _():
            loss_ref[...] = jnp.zeros_like(loss_ref)
            dg_ref[...] = jnp.zeros_like(dg_ref)

        x = x_ref[...]
        g = g_ref[...]
        r = lax.rsqrt(jnp.mean(x * x, axis=-1, keepdims=True) + EPS)
        xh = x * r
        err = xh * g - t_ref[...]
        row = jnp.mean(err * err, axis=-1, keepdims=True)
        loss_ref[...] += 0.5 * jnp.sum(row, axis=0, keepdims=True)
        dy = err * (1.0 / D)
        dg_ref[...] += jnp.sum(dy * xh, axis=0, keepdims=True)
        dxh = dy * g
        dx_ref[...] = r * (dxh - xh * jnp.mean(dxh * xh, axis=-1, keepdims=True))

    row_blk = pl.BlockSpec((tm, D), lambda i: (i, 0))
    vec = pl.BlockSpec((1, D), lambda i: (0, 0))
    return pl.pallas_call(body, name="final_loss_bwd", grid=(S // tm,), in_specs=[row_blk, vec, row_blk],
                          out_specs=[row_blk, pl.BlockSpec((1, 128), lambda i: (0, 0)), vec],
                          out_shape=[_sds((S, D), f32), _sds((1, 128), f32), _sds((1, D), f32)],
                          compiler_params=_params(1))(x2, g3, tgt)


def _rms_bwd(name, x, g, dxn, dres, tm):
    S = x.shape[0]
    tm = min(tm, S)

    def body(x_ref, g_ref, d_ref, r_ref, dx_ref, dg_ref):
        @pl.when(pl.program_id(0) == 0)
        def _():
            dg_ref[...] = jnp.zeros_like(dg_ref)

        x = x_ref[...]
        d = d_ref[...]
        r = lax.rsqrt(jnp.mean(x * x, axis=-1, keepdims=True) + EPS)
        xh = x * r
        dg_ref[...] += jnp.sum(d * xh, axis=0, keepdims=True)
        dxh = d * g_ref[...]
        dx_ref[...] = r_ref[...] + r * (dxh - xh * jnp.mean(dxh * xh, axis=-1, keepdims=True))

    row_blk = pl.BlockSpec((tm, D), lambda i: (i, 0))
    vec = pl.BlockSpec((1, D), lambda i: (0, 0))
    return pl.pallas_call(body, name=name, grid=(S // tm,), in_specs=[row_blk, vec, row_blk, row_blk],
                          out_specs=[row_blk, vec], out_shape=[_sds((S, D), f32), _sds((1, D), f32)],
                          compiler_params=_params(1))(x, g, dxn, dres)


LRU_TT = 256
SCAN_UNROLL = 4


HALO = 16


def _halo(ref, i, S):
    nt = S // LRU_TT
    t0 = pl.multiple_of(i * LRU_TT, LRU_TT)
    p0 = pl.multiple_of(jnp.maximum(t0 - HALO, 0), HALO)
    n0 = pl.multiple_of(jnp.minimum(t0 + LRU_TT, S - HALO), HALO)
    prev = jnp.where(i > 0, ref[pl.ds(p0, HALO), :].astype(f32), 0.0)
    nxt = jnp.where(i < nt - 1, ref[pl.ds(n0, HALO), :].astype(f32), 0.0)
    return jnp.concatenate([prev, ref[pl.ds(t0, LRU_TT), :].astype(f32), nxt], axis=0)


def _shift(ext, k):
    n = LRU_TT + 2 * HALO
    return pltpu.roll(ext, (-k) % n, 0)[HALO:HALO + LRU_TT]


def _lru_gates(uc, wbd, ba, bx):
    pre = jnp.dot(uc.astype(bf16), wbd, preferred_element_type=f32)
    r_f = _sigmoid(pre[:, 0:CW] + ba[0:1])
    i_f = _sigmoid(pre[:, CW:2 * CW] + bx[0:1])
    r_b = _sigmoid(pre[:, 2 * CW:3 * CW] + ba[1:2])
    i_b = _sigmoid(pre[:, 3 * CW:4 * CW] + bx[1:2])
    return r_f, i_f, r_b, i_b


def _lru_coeffs(r, sp):
    log_a = (-RGLRU_C * r) * sp
    a = jnp.exp(log_a)
    beta = jnp.sqrt(jnp.maximum(_one_minus_exp2x(log_a, a), 0.0))
    return a, beta


def _lru_coeffs_inv(r, sp):
    log_a = (-RGLRU_C * r) * sp
    a = jnp.exp(log_a)
    om = jnp.maximum(_one_minus_exp2x(log_a, a), 0.0)
    return a, jnp.sqrt(om), lax.rsqrt(om)


def _conv_tile(u_ref, i, S, cw, cb):
    ext = _halo(u_ref, i, S)
    um2, um1, u0, up1 = _shift(ext, -2), _shift(ext, -1), ext[HALO:HALO + LRU_TT], _shift(ext, 1)
    uc = um2 * cw[0:1] + um1 * cw[1:2] + u0 * cw[2:3] + up1 * cw[3:4] + cb
    return uc, (um2, um1, u0, up1)


def _scan_pair(S, fwd_a, fwd_b, fwd_out, rev_a, rev_b, rev_out):
    ng = S // 8
    idx = lax.broadcasted_iota(jnp.int32, (8, CW), 0)

    def local(a, b, rev):
        for sh in (1, 2, 4):
            if rev:
                keep = idx < 8 - sh
                amt = 8 - sh
            else:
                keep = idx >= sh
                amt = sh
            a_s = jnp.where(keep, pltpu.roll(a, amt, 0), 1.0)
            b_s = jnp.where(keep, pltpu.roll(b, amt, 0), 0.0)
            b = a * b_s + b
            a = a * a_s
        return a, b

    def step(it, carry):
        cf, cr = carry
        fwd_rows = [pl.multiple_of((it * SCAN_UNROLL + j) * 8, 8) for j in range(SCAN_UNROLL)]
        rev_rows = [pl.multiple_of((ng - 1 - (it * SCAN_UNROLL + j)) * 8, 8) for j in range(SCAN_UNROLL)]
        fwd_loc = [local(fwd_a(r), fwd_b(r), False) for r in fwd_rows]
        rev_loc = [local(rev_a(r), rev_b(r), True) for r in rev_rows]
        for j in range(SCAN_UNROLL):
            a, b = fwd_loc[j]
            h = a * cf + b
            fwd_out[pl.ds(fwd_rows[j], 8), :] = h
            cf = jnp.broadcast_to(h[7:8, :], (8, CW))
            a, b = rev_loc[j]
            h = a * cr + b
            rev_out[pl.ds(rev_rows[j], 8), :] = h
            cr = jnp.broadcast_to(h[0:1, :], (8, CW))
        return cf, cr

    zero = jnp.zeros((8, CW), f32)
    lax.fori_loop(0, ng // SCAN_UNROLL, step, (zero, zero))


def _lru_specs(S):
    seq = lambda off: pl.BlockSpec((S, CW), lambda j: (0, off + j))
    par = lambda rows: pl.BlockSpec((rows, CW), lambda j: (0, j))
    return seq, par


def _lru_fwd(proj, conv_w, conv_b, lam, ba, bx, wbd):
    S = proj.shape[0]
    nt = S // LRU_TT

    def body(u_ref, g_ref, cw_ref, cb_ref, lam_ref, ba_ref, bx_ref, wbd_ref, y_ref, state_ref, af_ref, bf_ref, ab_ref, bb_ref,
             sems):
        cw, cb, ba_v, bx_v, wbd_v = cw_ref[...], cb_ref[...], ba_ref[...], bx_ref[...], wbd_ref[...]
        sp = jax.nn.softplus(-lam_ref[...])
        cols = pl.ds(pl.multiple_of(pl.program_id(0) * CW, CW), CW)
        save = [pltpu.make_async_copy(ref, state_ref.at[k, :, cols], sems.at[k])
                for k, ref in enumerate((af_ref, bf_ref, ab_ref, bb_ref))]

        def phase1(i, c):
            uc, _ = _conv_tile(u_ref, i, S, cw, cb)
            r_f, i_f, r_b, i_b = _lru_gates(uc, wbd_v, ba_v, bx_v)
            rows = pl.ds(pl.multiple_of(i * LRU_TT, LRU_TT), LRU_TT)
            a, beta = _lru_coeffs(r_f, sp[0:1])
            af_ref[rows, :] = a
            bf_ref[rows, :] = beta * (i_f * uc)
            a, beta = _lru_coeffs(r_b, sp[1:2])
            ab_ref[rows, :] = a
            bb_ref[rows, :] = beta * (i_b * uc)
            return c

        lax.fori_loop(0, nt, phase1, 0)
        row8 = lambda ref: (lambda r0: ref[pl.ds(r0, 8), :])
        _scan_pair(S, row8(af_ref), row8(bf_ref), bf_ref, row8(ab_ref), row8(bb_ref), bb_ref)
        for cp in save:
            cp.start()

        def phase3(i, c):
            rows = pl.ds(pl.multiple_of(i * LRU_TT, LRU_TT), LRU_TT)
            y = (bf_ref[rows, :] + bb_ref[rows, :]) * jax.nn.gelu(g_ref[rows, :].astype(f32))
            y_ref[rows, :] = y.astype(y_ref.dtype)
            return c

        lax.fori_loop(0, nt, phase3, 0)
        for cp in save:
            cp.wait()

    seq, par = _lru_specs(S)
    return pl.pallas_call(
        body, name="lru_fwd", grid=(NCH,),
        in_specs=[seq(0), seq(NCH), par(4), par(1), par(2), par(2), par(2),
                  pl.BlockSpec((None, CW, 4 * CW), lambda j: (j, 0, 0))],
        out_specs=[seq(0), ANY], out_shape=[_sds((S, D), bf16), _sds((4, S, D), f32)],
        scratch_shapes=[pltpu.VMEM((S, CW), f32)] * 4 + [pltpu.SemaphoreType.DMA((4,))], compiler_params=_params(1, True),
    )(proj, proj, conv_w, conv_b, lam, ba, bx, wbd)


def _lru_bwd(proj, dy, state, conv_w, conv_b, lam, ba, bx, wbd):
    S = proj.shape[0]
    nt = S // LRU_TT

    def body(u_ref, g_ref, dy_ref, state_ref, cw_ref, cb_ref, lam_ref, ba_ref, bx_ref, wbd_ref,
             du_ref, dg_ref, dcw_ref, dcb_ref, dlam_ref, dba_ref, dbx_ref, dwbd_ref,
             af_ref, bf_ref, ab_ref, bb_ref, dh_ref, sems):
        cw, cb, ba_v, bx_v, wbd_v = cw_ref[...], cb_ref[...], ba_ref[...], bx_ref[...], wbd_ref[...]
        lam_v = lam_ref[...]
        sp = jax.nn.softplus(-lam_v)
        cols = pl.ds(pl.multiple_of(pl.program_id(0) * CW, CW), CW)
        load = [pltpu.make_async_copy(state_ref.at[k, :, cols], ref, sems.at[k])
                for k, ref in enumerate((af_ref, bf_ref, ab_ref, bb_ref))]
        for cp in load:
            cp.start()
        for cp in load:
            cp.wait()
        row8 = lambda ref: (lambda r0: ref[pl.ds(r0, 8), :])

        def phase0(i, c):
            rows = pl.ds(pl.multiple_of(i * LRU_TT, LRU_TT), LRU_TT)
            gl, dgl = _gelu_and_grad(g_ref[rows, :].astype(f32))
            dyt = dy_ref[rows, :].astype(f32)
            dh_ref[rows, :] = dyt * gl
            dg_ref[rows, :] = ((dyt * (bf_ref[rows, :] + bb_ref[rows, :])) * dgl).astype(dg_ref.dtype)
            return c

        lax.fori_loop(0, nt, phase0, 0)

        def scaled_dh(a_ref):
            def f(r0):
                return a_ref[pl.ds(r0, 8), :] * dh_ref[pl.ds(r0, 8), :]
            return f

        _scan_pair(S, row8(ab_ref), scaled_dh(ab_ref), ab_ref, row8(af_ref), scaled_dh(af_ref), af_ref)

        dcw_ref[...] = jnp.zeros_like(dcw_ref)
        dcb_ref[...] = jnp.zeros_like(dcb_ref)
        dlam_ref[...] = jnp.zeros_like(dlam_ref)
        dba_ref[...] = jnp.zeros_like(dba_ref)
        dbx_ref[...] = jnp.zeros_like(dbx_ref)
        dwbd_ref[...] = jnp.zeros_like(dwbd_ref)

        def direction(uc, r, i_g, dht, h_nb, sp_d):
            a, beta, inv_beta = _lru_coeffs_inv(r, sp_d)
            da = dht * h_nb
            dbeta = dht * (i_g * uc)
            d_iu = dht * beta
            dlog_a = da * a - (a * a) * (dbeta * inv_beta)
            dlr = dlog_a * r
            dsp = -RGLRU_C * jnp.sum(dlr, axis=0, keepdims=True)
            dpre_r = (dlr * (1.0 - r)) * (-RGLRU_C * sp_d)
            dpre_i = (d_iu * uc) * (i_g * (1.0 - i_g))
            return dpre_r, dpre_i, d_iu * i_g, dsp

        def phase4(i, c):
            uc, (um2, um1, u0, up1) = _conv_tile(u_ref, i, S, cw, cb)
            r_f, i_f, r_b, i_b = _lru_gates(uc, wbd_v, ba_v, bx_v)
            rows = pl.ds(pl.multiple_of(i * LRU_TT, LRU_TT), LRU_TT)
            dh = dh_ref[rows, :]
            dht_f = dh + _shift(_halo(af_ref, i, S), 1)
            h_prev = _shift(_halo(bf_ref, i, S), -1)
            dht_b = dh + _shift(_halo(ab_ref, i, S), -1)
            h_next = _shift(_halo(bb_ref, i, S), 1)
            prf, pif, duc_f, dsp_f = direction(uc, r_f, i_f, dht_f, h_prev, sp[0:1])
            prb, pib, duc_b, dsp_b = direction(uc, r_b, i_b, dht_b, h_next, sp[1:2])
            dpre = jnp.concatenate([prf, pif, prb, pib], axis=1)
            dpre_b = dpre.astype(bf16)
            duc = (duc_f + duc_b) + lax.dot_general(dpre_b, wbd_v, _DIMS["nt"], preferred_element_type=f32)
            dwbd_ref[...] += lax.dot_general(uc.astype(bf16), dpre_b, _DIMS["tn"], preferred_element_type=f32)
            colsum = lambda v: jnp.sum(v, axis=0, keepdims=True)
            dba_ref[...] += jnp.concatenate([colsum(prf), colsum(prb)], axis=0)
            dbx_ref[...] += jnp.concatenate([colsum(pif), colsum(pib)], axis=0)
            dlam_ref[...] += jnp.concatenate([dsp_f, dsp_b], axis=0)
            dcb_ref[...] += colsum(duc)
            dcw_ref[...] += jnp.concatenate([colsum(duc * um2), colsum(duc * um1), colsum(duc * u0),
                                             colsum(duc * up1)], axis=0)
            af_ref[rows, :] = duc
            return c

        lax.fori_loop(0, nt, phase4, 0)
        dlam_ref[...] = dlam_ref[...] * (-_sigmoid(-lam_v))

        def phase5(i, c):
            ext = _halo(af_ref, i, S)
            rows = pl.ds(pl.multiple_of(i * LRU_TT, LRU_TT), LRU_TT)
            du = (_shift(ext, 2) * cw[0:1] + _shift(ext, 1) * cw[1:2] + ext[HALO:HALO + LRU_TT] * cw[2:3]
                  + _shift(ext, -1) * cw[3:4])
            du_ref[rows, :] = du.astype(du_ref.dtype)
            return c

        lax.fori_loop(0, nt, phase5, 0)

    seq, par = _lru_specs(S)
    return pl.pallas_call(
        body, name="lru_bwd", grid=(NCH,),
        in_specs=[seq(0), seq(NCH), seq(0), ANY, par(4), par(1), par(2), par(2), par(2),
                  pl.BlockSpec((None, CW, 4 * CW), lambda j: (j, 0, 0))],
        out_specs=[seq(0), seq(0), par(4), par(1), par(2), par(2), par(2),
                   pl.BlockSpec((None, CW, 4 * CW), lambda j: (j, 0, 0))],
        out_shape=[_sds((S, D), bf16), _sds((S, D), bf16), _sds((4, D), f32), _sds((1, D), f32), _sds((2, D), f32),
                   _sds((2, D), f32), _sds((2, D), f32), _sds((NCH, CW, 4 * CW), f32)],
        scratch_shapes=[pltpu.VMEM((S, CW), f32)] * 5 + [pltpu.SemaphoreType.DMA((4,))], compiler_params=_params(1, True),
    )(proj, proj, dy, state, conv_w, conv_b, lam, ba, bx, wbd)


_SLOPES = [2.0 ** (-8.0 * (h + 1) / NH) for h in range(NH)]


def _half_mask(shape, e):
    lane = lax.broadcasted_iota(jnp.int32, shape, 1)
    return (lane < HD) if e == 0 else (lane >= HD)


def _both_halves(x, src):
    return jnp.where(_half_mask(x.shape, src), x, pltpu.roll(x, HD, 1))


def _fold_halves(x, dst):
    return jnp.where(_half_mask(x.shape, dst), x + pltpu.roll(x, HD, 1), 0.0)


def _attn_base(n, S):
    tq = lax.broadcasted_iota(jnp.int32, (BLK, 3 * BLK), 0)
    sk = lax.broadcasted_iota(jnp.int32, (BLK, 3 * BLK), 1)
    dist = jnp.abs(tq + BLK - sk)
    kpos = n * BLK - BLK + sk
    valid = (dist <= BLK) & (kpos >= 0) & (kpos < S)
    return jnp.where(valid, -dist.astype(f32), NEG_INF)


def _group_heads(ref, kvh, scale):
    parts = []
    for i in range(4):
        pair = 2 * kvh + i // 2
        x = ref[:, pair * 128:(pair + 1) * 128].astype(f32)
        parts.append(jnp.where(_half_mask(x.shape, i % 2), x * scale, 0.0))
    return parts


def _stack_bf16(parts):
    return jnp.concatenate([p.astype(bf16) for p in parts], axis=0)


def _attn_softmax(s_raw, base, slope, sink):
    s = s_raw + slope * base
    m = jnp.maximum(jnp.max(s, axis=-1, keepdims=True), sink)
    p = jnp.exp(s - m)
    esink = jnp.exp(sink - m)
    inv = 1.0 / (jnp.sum(p, axis=-1, keepdims=True) + esink)
    return p, inv, esink * inv


def _attn_specs(S):
    nb = S // BLK
    q_spec = pl.BlockSpec((BLK, D), lambda n: (n, 2))
    kv = lambda col: [pl.BlockSpec((BLK, 256), lambda n: (jnp.maximum(n - 1, 0), col)),
                      pl.BlockSpec((BLK, 256), lambda n: (n, col)),
                      pl.BlockSpec((BLK, 256), lambda n: (jnp.minimum(n + 1, nb - 1), col))]
    return nb, q_spec, kv(COL_K), kv(COL_V)


def _attn_fwd(proj, sink):
    S = proj.shape[0]
    nb, q_spec, k_specs, v_specs = _attn_specs(S)

    def body(sink_ref, q_ref, kp_ref, kc_ref, kn_ref, vp_ref, vc_ref, vn_ref, o_ref):
        base = _attn_base(pl.program_id(0), S)
        kcat = jnp.concatenate([kp_ref[...], kc_ref[...], kn_ref[...]], axis=0).astype(f32)
        vcat = jnp.concatenate([vp_ref[...], vc_ref[...], vn_ref[...]], axis=0).astype(f32)
        even = _half_mask((BLK, 128), 0)
        for kvh in range(NH // 4):
            ch, off = kvh // 2, kvh % 2
            kb = _both_halves(kcat[:, ch * 128:(ch + 1) * 128], off).astype(bf16)
            vb = _both_halves(vcat[:, ch * 128:(ch + 1) * 128], off).astype(bf16)
            q4 = _stack_bf16(_group_heads(q_ref, kvh, HD ** -0.5))
            s4 = lax.dot_general(q4, kb, _DIMS["nt"], preferred_element_type=f32)
            ps, invs = [], []
            for i in range(4):
                h = 4 * kvh + i
                p, inv, _ = _attn_softmax(s4[i * BLK:(i + 1) * BLK], base, _SLOPES[h], sink_ref[0, h])
                ps.append(p)
                invs.append(inv)
            o4 = jnp.dot(_stack_bf16(ps), vb, preferred_element_type=f32)
            for pr in range(2):
                lo = o4[(2 * pr) * BLK:(2 * pr + 1) * BLK] * invs[2 * pr]
                hi = o4[(2 * pr + 1) * BLK:(2 * pr + 2) * BLK] * invs[2 * pr + 1]
                pair = 2 * kvh + pr
                o_ref[:, pair * 128:(pair + 1) * 128] = jnp.where(even, lo, hi).astype(o_ref.dtype)

    return pl.pallas_call(
        body, name="attn_fwd", grid=(nb,),
        in_specs=[pl.BlockSpec(memory_space=pltpu.SMEM), q_spec] + k_specs + v_specs,
        out_specs=pl.BlockSpec((BLK, D), lambda n: (n, 0)), out_shape=_sds((S, D), bf16),
        compiler_params=_params(1, True))(sink, proj, proj, proj, proj, proj, proj, proj)


def _attn_bwd(proj, sink, y_b, dy_b):
    S = proj.shape[0]
    nb, q_spec, k_specs, v_specs = _attn_specs(S)

    def body(sink_ref, q_ref, kp_ref, kc_ref, kn_ref, vp_ref, vc_ref, vn_ref, o_ref, do_ref,
             dq_ref, dk_ref, dv_ref, dsink_ref):
        n = pl.program_id(0)

        @pl.when(n == 0)
        def _():
            dk_ref[...] = jnp.zeros_like(dk_ref)
            dv_ref[...] = jnp.zeros_like(dv_ref)
            dsink_ref[...] = jnp.zeros_like(dsink_ref)

        base = _attn_base(n, S)
        kcat = jnp.concatenate([kp_ref[...], kc_ref[...], kn_ref[...]], axis=0).astype(f32)
        vcat = jnp.concatenate([vp_ref[...], vc_ref[...], vn_ref[...]], axis=0).astype(f32)
        dk_acc = [jnp.zeros((3 * BLK, 128), f32), jnp.zeros((3 * BLK, 128), f32)]
        dv_acc = [jnp.zeros((3 * BLK, 128), f32), jnp.zeros((3 * BLK, 128), f32)]
        scale = HD ** -0.5
        even = _half_mask((BLK, 128), 0)
        for kvh in range(NH // 4):
            ch, off = kvh // 2, kvh % 2
            kb = _both_halves(kcat[:, ch * 128:(ch + 1) * 128], off).astype(bf16)
            vb = _both_halves(vcat[:, ch * 128:(ch + 1) * 128], off).astype(bf16)
            q_parts = _group_heads(q_ref, kvh, scale)
            d_parts = _group_heads(do_ref, kvh, 1.0)
            s4 = lax.dot_general(_stack_bf16(q_parts), kb, _DIMS["nt"], preferred_element_type=f32)
            dp4 = lax.dot_general(_stack_bf16(d_parts), vb, _DIMS["nt"], preferred_element_type=f32)
            ts, ps, qn, dn, invs = [], [], [], [], []
            for i in range(4):
                h = 4 * kvh + i
                pair = 2 * kvh + i // 2
                rows = slice(i * BLK, (i + 1) * BLK)
                p, inv, psink = _attn_softmax(s4[rows], base, _SLOPES[h], sink_ref[0, h])
                delta = jnp.sum(d_parts[i] * o_ref[:, pair * 128:(pair + 1) * 128].astype(f32), axis=-1, keepdims=True)
                dsink_ref[h:h + 1, :] += jnp.broadcast_to(-jnp.sum(psink * delta, axis=0, keepdims=True), (1, 128))
                ts.append(p * (dp4[rows] - delta))
                ps.append(p)
                qn.append(q_parts[i] * inv)
                dn.append(d_parts[i] * inv)
                invs.append(inv)
            t4 = _stack_bf16(ts)
            dq4 = jnp.dot(t4, kb, preferred_element_type=f32)
            for pr in range(2):
                lo = dq4[(2 * pr) * BLK:(2 * pr + 1) * BLK] * invs[2 * pr]
                hi = dq4[(2 * pr + 1) * BLK:(2 * pr + 2) * BLK] * invs[2 * pr + 1]
                pair = 2 * kvh + pr
                dq_ref[:, pair * 128:(pair + 1) * 128] = (jnp.where(even, lo, hi) * scale).astype(dq_ref.dtype)
            dk_both = lax.dot_general(t4, _stack_bf16(qn), _DIMS["tn"], preferred_element_type=f32)
            dv_both = lax.dot_general(_stack_bf16(ps), _stack_bf16(dn), _DIMS["tn"], preferred_element_type=f32)
            dk_acc[ch] = dk_acc[ch] + _fold_halves(dk_both, off)
            dv_acc[ch] = dv_acc[ch] + _fold_halves(dv_both, off)
        for j in range(3):
            blk = n + (j - 1)

            @pl.when((blk >= 0) & (blk < nb))
            def _():
                rows = pl.ds(pl.multiple_of(blk * BLK, BLK), BLK)
                for ch in range(2):
                    dk_ref[rows, ch * 128:(ch + 1) * 128] += dk_acc[ch][j * BLK:(j + 1) * BLK]
                    dv_ref[rows, ch * 128:(ch + 1) * 128] += dv_acc[ch][j * BLK:(j + 1) * BLK]

    row_blk = pl.BlockSpec((BLK, D), lambda n: (n, 0))
    full = pl.BlockSpec((S, 256), lambda n: (0, 0))
    return pl.pallas_call(
        body, name="attn_bwd", grid=(nb,),
        in_specs=[pl.BlockSpec(memory_space=pltpu.SMEM), q_spec] + k_specs + v_specs + [row_blk, row_blk],
        out_specs=[row_blk, full, full, pl.BlockSpec((NH, 128), lambda n: (0, 0))],
        out_shape=[_sds((S, D), bf16), _sds((S, 256), f32), _sds((S, 256), f32), _sds((NH, 128), f32)],
        compiler_params=_params(1, True))(sink, proj, proj, proj, proj, proj, proj, proj, y_b, dy_b)


def _adamw(name, w, g, m, v, tr):
    R, C = w.shape
    tr = min(tr, R)

    def body(w_ref, g_ref, m_ref, v_ref, d_ref, m2_ref, v2_ref):
        g = g_ref[...]
        m2 = ADAM_B1 * m_ref[...] + (1.0 - ADAM_B1) * g
        v2 = ADAM_B2 * v_ref[...] + (1.0 - ADAM_B2) * (g * g)
        m_hat = m2 / (1.0 - ADAM_B1 ** ADAM_STEP)
        v_hat = v2 / (1.0 - ADAM_B2 ** ADAM_STEP)
        d_ref[...] = -ADAM_LR * (m_hat / (jnp.sqrt(v_hat) + ADAM_EPS) + ADAM_WD * w_ref[...])
        m2_ref[...] = m2
        v2_ref[...] = v2

    blk = pl.BlockSpec((tr, C), lambda i: (i, 0))
    return pl.pallas_call(body, name=name, grid=(R // tr,), in_specs=[blk] * 4, out_specs=[blk] * 3,
                          out_shape=[_sds((R, C), f32)] * 3, compiler_params=_params(1))(w, g, m, v)


def _pair_sum(name, c_arr, g4, recv, th):
    _, _, h, w = g4.shape
    th = min(th, h)

    def body(c_ref, g_ref, r_ref, o_ref, ob_ref):
        p = g_ref[...] + r_ref[...]
        o_ref[...] = p
        ob_ref[...] = p.astype(bf16)

    blk = pl.BlockSpec((None, th, w), lambda s, i, c_ref: (s, i, 0))
    spec = pltpu.PrefetchScalarGridSpec(
        num_scalar_prefetch=1, grid=(NCHIP, h // th),
        in_specs=[pl.BlockSpec((None, None, th, w), lambda s, i, c_ref: (s, c_ref[0], i, 0)), blk],
        out_specs=[blk, blk])
    return pl.pallas_call(body, name=name, grid_spec=spec,
                          out_shape=[_sds((NCHIP, h, w), f32), _sds((NCHIP, h, w), bf16)],
                          compiler_params=_params(2))(c_arr, g4, recv)


def _chip_sum(name, chip_arr, own4, recv3, th):
    _, h, w = own4.shape
    th = min(th, h)

    def body(s_ref, o_ref, r_ref, out_ref):
        out_ref[...] = ((o_ref[...] + r_ref[0].astype(f32)) + r_ref[1].astype(f32)) + r_ref[2].astype(f32)

    spec = pltpu.PrefetchScalarGridSpec(
        num_scalar_prefetch=1, grid=(h // th,),
        in_specs=[pl.BlockSpec((None, th, w), lambda i, s_ref: (s_ref[0], i, 0)),
                  pl.BlockSpec((3, th, w), lambda i, s_ref: (0, i, 0))],
        out_specs=pl.BlockSpec((th, w), lambda i, s_ref: (i, 0)))
    return pl.pallas_call(body, name=name, grid_spec=spec, out_shape=_sds((h, w), f32),
                          compiler_params=_params(1, True))(chip_arr, own4, recv3)


def _adamw_halves(name, c_arr, w, g_own, g_recv, m, v, th):
    h, wd = g_own.shape
    th = min(th, h)

    def body(c_ref, w_ref, go_ref, gr_ref, m_ref, v_ref, g_ref, d_ref, m2_ref, v2_ref):
        g = jnp.where(c_ref[0] == pl.program_id(0), go_ref[...], gr_ref[...])
        m2 = ADAM_B1 * m_ref[...] + (1.0 - ADAM_B1) * g
        v2 = ADAM_B2 * v_ref[...] + (1.0 - ADAM_B2) * (g * g)
        m_hat = m2 / (1.0 - ADAM_B1 ** ADAM_STEP)
        v_hat = v2 / (1.0 - ADAM_B2 ** ADAM_STEP)
        g_ref[...] = g
        d_ref[...] = -ADAM_LR * (m_hat / (jnp.sqrt(v_hat) + ADAM_EPS) + ADAM_WD * w_ref[...])
        m2_ref[...] = m2
        v2_ref[...] = v2

    nt = h // th
    full = pl.BlockSpec((th, wd), lambda hh, i, c_ref: (hh * nt + i, 0))
    half = pl.BlockSpec((th, wd), lambda hh, i, c_ref: (i, 0))
    spec = pltpu.PrefetchScalarGridSpec(num_scalar_prefetch=1, grid=(2, nt),
                                        in_specs=[full, half, half, full, full], out_specs=[full] * 4)
    return pl.pallas_call(body, name=name, grid_spec=spec, out_shape=[_sds((2 * h, wd), f32)] * 4,
                          compiler_params=_params(2))(c_arr, w, g_own, g_recv, m, v)


def _add2(name, a, b):
    def body(a_ref, b_ref, o_ref):
        o_ref[...] = a_ref[...] + b_ref[...]
    return pl.pallas_call(body, name=name, out_shape=_sds(a.shape, f32))(a, b)


def _sum4(name, b4, th):
    _, h, w = b4.shape
    th = min(th, h)

    def body(b_ref, o_ref):
        o_ref[...] = ((b_ref[0] + b_ref[1]) + b_ref[2]) + b_ref[3]

    return pl.pallas_call(body, name=name, grid=(h // th,),
                          in_specs=[pl.BlockSpec((NCHIP, th, w), lambda i: (0, i, 0))],
                          out_specs=pl.BlockSpec((th, w), lambda i: (i, 0)), out_shape=_sds((h, w), f32),
                          compiler_params=_params(1, True))(b4)


def _coords():
    x, y, c = lax.axis_index("x"), lax.axis_index("y"), lax.axis_index("c")
    return x, y, c, [(1 - x, y), (x, 1 - y), (1 - x, 1 - y)]


def _gather_chips(arrs):
    n = len(arrs)

    def body(*refs):
        ins, outs = refs[:n], refs[n:2 * n]
        send_sems, recv_sems, local_sems = refs[2 * n:2 * n + 3]
        stage = refs[2 * n + 3:]
        x, y, c, chips = _coords()
        s = 2 * x + y
        sib = (x, y, 1 - c)
        load = [pltpu.make_async_copy(ins[a], stage[a], local_sems.at[a]) for a in range(n)]
        local = [pltpu.make_async_copy(stage[a], outs[a].at[s], local_sems.at[n + a]) for a in range(n)]
        for cp in load:
            cp.start()

        def over_ici(k, a, slot, peer):
            return pltpu.make_async_remote_copy(src_ref=ins[a].at[c], dst_ref=outs[a].at[slot, c], send_sem=send_sems.at[k * n + a],
                                                recv_sem=recv_sems.at[k * n + a], device_id=peer, device_id_type=MESH)

        def to_sibling(k, a, slot, half):
            i = (3 + k) * n + a
            return pltpu.make_async_remote_copy(src_ref=outs[a].at[slot, half], dst_ref=outs[a].at[slot, half], send_sem=send_sems.at[i],
                                                recv_sem=recv_sems.at[i], device_id=sib, device_id_type=MESH)

        sends = [over_ici(k, a, s, (px, py, c)) for k, (px, py) in enumerate(chips) for a in range(n)]
        for cp in sends:
            cp.start()
        for a in range(n):
            load[a].wait()
            local[a].start()
        passed = []
        for k, (px, py) in enumerate(chips):
            for a in range(n):
                over_ici(k, a, 2 * px + py, (px, py, c)).wait_recv()
                cp = to_sibling(k, a, 2 * px + py, c)
                cp.start()
                passed.append(cp)
        for k, (px, py) in enumerate(chips):
            for a in range(n):
                to_sibling(k, a, 2 * px + py, 1 - c).wait_recv()
        for cp in sends + passed:
            cp.wait_send()
        for cp in local:
            cp.wait()

    return pl.pallas_call(
        body, name="gather_weights", in_specs=[ANY] * n, out_specs=[ANY] * n,
        out_shape=[_sds((NCHIP,) + a.shape, a.dtype) for a in arrs],
        scratch_shapes=[pltpu.SemaphoreType.DMA((6 * n,)), pltpu.SemaphoreType.DMA((6 * n,)), pltpu.SemaphoreType.DMA((2 * n,))]
        + [pltpu.VMEM(a.shape, a.dtype) for a in arrs],
        compiler_params=pltpu.CompilerParams(vmem_limit_bytes=VMEM_LIMIT),
    )(*arrs)


HBM = pl.BlockSpec(memory_space=pltpu.HBM)
SEM = pl.BlockSpec(memory_space=pltpu.SEMAPHORE)
EFFECT = pltpu.SideEffectType.DATAFLOW_SIDE_EFFECTING


def _split_start(name, n_copies, make_copies, ins, land_shapes):
    ni, nl = len(ins), len(land_shapes)

    def body(*refs):
        in_refs, land_refs = refs[:ni], refs[ni:ni + nl]
        send_sems, recv_sems = refs[ni + nl], refs[ni + nl + 1]
        token = refs[-1]
        for cp in make_copies(in_refs, land_refs, send_sems, recv_sems):
            cp.start()
        token[...] = jnp.zeros_like(token)

    lands = [pltpu.with_memory_space_constraint(lax.empty(s.shape, s.dtype), pltpu.HBM) for s in land_shapes]
    res = pl.pallas_call(
        body, name=name,
        out_shape=(pltpu.SemaphoreType.DMA((n_copies,)), pltpu.SemaphoreType.DMA((n_copies,)),
                   *[pltpu.HBM(a.shape, a.dtype) for a in ins], *[pltpu.HBM(s.shape, s.dtype) for s in land_shapes],
                   _sds((8, 128), f32)),
        in_specs=[HBM] * (ni + nl), out_specs=(SEM, SEM, *[HBM] * (ni + nl), pl.BlockSpec(memory_space=pltpu.VMEM)),
        input_output_aliases={i: 2 + i for i in range(ni + nl)},
        compiler_params=pltpu.CompilerParams(has_side_effects=EFFECT),
    )(*[pltpu.with_memory_space_constraint(a, pltpu.HBM) for a in ins], *lands)
    return res[0], res[1], list(res[2:2 + ni]), list(res[2 + ni:2 + ni + nl]), res[-1]


def _split_wait(name, make_copies, send_sems, recv_sems, ins, lands, after):
    ni, nl = len(ins), len(lands)

    def body(*refs):
        in_refs, land_refs = refs[:ni], refs[ni:ni + nl]
        s_sems, r_sems = refs[ni + nl], refs[ni + nl + 1]
        for cp in make_copies(in_refs, land_refs, s_sems, r_sems):
            cp.wait_send()
            cp.wait_recv()

    res = pl.pallas_call(
        body, name=name, out_shape=tuple(pltpu.HBM(a.shape, a.dtype) for a in ins + lands),
        in_specs=[HBM] * (ni + nl) + [SEM, SEM, ANY], out_specs=tuple([HBM] * (ni + nl)),
        input_output_aliases={i: i for i in range(ni + nl)},
        compiler_params=pltpu.CompilerParams(has_side_effects=EFFECT),
    )(*ins, *lands, send_sems, recv_sems, after)
    return list(res[:ni]), list(res[ni:])


def _gather_copies(n):
    def make(in_refs, land_refs, send_sems, recv_sems):
        x, y, c, chips = _coords()
        s = 2 * x + y
        return [pltpu.make_async_remote_copy(src_ref=in_refs[a], dst_ref=land_refs[a].at[s], send_sem=send_sems.at[k * n + a],
                                             recv_sem=recv_sems.at[k * n + a], device_id=(px, py, c), device_id_type=MESH)
                for k, (px, py) in enumerate(chips) for a in range(n)]
    return make


def _sibling_half_copies(n):
    def make(in_refs, land_refs, send_sems, recv_sems):
        x, y, c, _ = _coords()
        return [pltpu.make_async_remote_copy(src_ref=in_refs[a].at[:, 1 - c], dst_ref=land_refs[a], send_sem=send_sems.at[a],
                                             recv_sem=recv_sems.at[a], device_id=(x, y, 1 - c), device_id_type=MESH)
                for a in range(n)]
    return make


def _chip_part_copies(n):
    def make(in_refs, land_refs, send_sems, recv_sems):
        x, y, c, chips = _coords()
        return [pltpu.make_async_remote_copy(src_ref=in_refs[a].at[2 * px + py], dst_ref=land_refs[a].at[k],
                                             send_sem=send_sems.at[k * n + a], recv_sem=recv_sems.at[k * n + a],
                                             device_id=(px, py, c), device_id_type=MESH)
                for k, (px, py) in enumerate(chips) for a in range(n)]
    return make


def _sibling_whole_copies(n):
    def make(in_refs, land_refs, send_sems, recv_sems):
        x, y, c, _ = _coords()
        return [pltpu.make_async_remote_copy(src_ref=in_refs[a], dst_ref=land_refs[a], send_sem=send_sems.at[a],
                                             recv_sem=recv_sems.at[a], device_id=(x, y, 1 - c), device_id_type=MESH)
                for a in range(n)]
    return make


def _place_own(chip_arr, owns, lands, steps):
    n = len(owns)

    def body(s_ref, *refs):
        for a in range(n):
            refs[2 * n + a][...] = refs[a][...]

    tiles = [o.shape[0] // steps for o in owns]
    spec = pltpu.PrefetchScalarGridSpec(
        num_scalar_prefetch=1, grid=(steps,),
        in_specs=[pl.BlockSpec((t, o.shape[1]), lambda i, s_ref: (i, 0)) for t, o in zip(tiles, owns)] + [ANY] * n,
        out_specs=[pl.BlockSpec((None, t, o.shape[1]), lambda i, s_ref: (s_ref[0], i, 0)) for t, o in zip(tiles, owns)])
    return pl.pallas_call(body, name="place_own", grid_spec=spec, out_shape=[_sds(l.shape, l.dtype) for l in lands],
                          input_output_aliases={1 + n + a: a for a in range(n)},
                          compiler_params=_params(1))(chip_arr, *owns, *lands)


def _sibling_halves(g4s, small):
    n = len(g4s)

    def body(*refs):
        ins, small_ref = refs[:n], refs[n]
        outs, small_out = refs[n + 1:2 * n + 1], refs[2 * n + 1]
        send_sems, recv_sems = refs[2 * n + 2:]
        x, y, c, _ = _coords()
        sib = (x, y, 1 - c)

        def remote(a, half):
            src = small_ref if a == n else ins[a].at[:, half]
            dst = small_out if a == n else outs[a]
            return pltpu.make_async_remote_copy(src_ref=src, dst_ref=dst, send_sem=send_sems.at[a], recv_sem=recv_sems.at[a],
                                                device_id=sib, device_id_type=MESH)

        sends = [remote(a, 1 - c) for a in range(n + 1)]
        for cp in sends:
            cp.start()
        for a in range(n + 1):
            remote(a, c).wait_recv()
        for cp in sends:
            cp.wait_send()

    return pl.pallas_call(
        body, name="reduce_sibling", in_specs=[ANY] * (n + 1), out_specs=[ANY] * (n + 1),
        out_shape=[_sds((g.shape[0],) + g.shape[2:], f32) for g in g4s] + [_sds(small.shape, f32)],
        scratch_shapes=[pltpu.SemaphoreType.DMA((n + 1,)), pltpu.SemaphoreType.DMA((n + 1,))],
    )(*g4s, small)


def _exchange_chips(parts, small2):
    n = len(parts)

    def body(*refs):
        ins, small_ref = refs[:n], refs[n]
        outs, small_out = refs[n + 1:2 * n + 1], refs[2 * n + 1]
        send_sems, recv_sems, local_sem = refs[2 * n + 2:]
        x, y, c, chips = _coords()
        s = 2 * x + y
        local = pltpu.make_async_copy(small_ref.at[c], small_out.at[s], local_sem)
        local.start()

        def remote(k, a, dest_chip, small_slot, peer):
            if a == n:
                src, dst = small_ref.at[c], small_out.at[small_slot]
            else:
                src, dst = ins[a].at[dest_chip], outs[a].at[k]
            i = k * (n + 1) + a
            return pltpu.make_async_remote_copy(src_ref=src, dst_ref=dst, send_sem=send_sems.at[i], recv_sem=recv_sems.at[i],
                                                device_id=peer, device_id_type=MESH)

        sends = [remote(k, a, 2 * px + py, s, (px, py, c)) for k, (px, py) in enumerate(chips) for a in range(n + 1)]
        for cp in sends:
            cp.start()
        for k, (px, py) in enumerate(chips):
            for a in range(n + 1):
                remote(k, a, s, 2 * px + py, (px, py, c)).wait_recv()
        for cp in sends:
            cp.wait_send()
        local.wait()

    m = 3 * (n + 1)
    return pl.pallas_call(
        body, name="reduce_chips", in_specs=[ANY] * (n + 1), out_specs=[ANY] * (n + 1),
        out_shape=[_sds((3,) + p.shape[1:], p.dtype) for p in parts] + [_sds((NCHIP,) + small2.shape[1:], f32)],
        scratch_shapes=[pltpu.SemaphoreType.DMA((m,)), pltpu.SemaphoreType.DMA((m,)), pltpu.SemaphoreType.DMA],
    )(*parts, small2)


def _share_sibling(halves):
    n = len(halves)

    def body(*refs):
        ins, outs = refs[:n], refs[n:2 * n]
        send_sems, recv_sems = refs[2 * n:]
        x, y, c, _ = _coords()
        sib = (x, y, 1 - c)
        sends = [pltpu.make_async_remote_copy(src_ref=ins[a], dst_ref=outs[a], send_sem=send_sems.at[a], recv_sem=recv_sems.at[a],
                                              device_id=sib, device_id_type=MESH) for a in range(n)]
        for cp in sends:
            cp.start()
        for cp in sends:
            cp.wait()

    return pl.pallas_call(
        body, name="reduce_share", in_specs=[ANY] * n, out_specs=[ANY] * n,
        out_shape=[_sds(h.shape, f32) for h in halves],
        scratch_shapes=[pltpu.SemaphoreType.DMA((n,)), pltpu.SemaphoreType.DMA((n,))],
    )(*halves)


def _block_diag_pairs(w):
    w = w.reshape(NCH, 2, HD, HD)
    z = jnp.zeros((NCH, HD, HD), w.dtype)
    return jnp.concatenate([jnp.concatenate([w[:, 0], z], axis=2), jnp.concatenate([z, w[:, 1]], axis=2)], axis=1)


def _diag_blocks(m):
    return jnp.stack([m[:, :HD, :HD], m[:, HD:, HD:]], axis=1).reshape(NH, HD, HD)


def _pack(vs, rows):
    flat = jnp.concatenate([v.reshape(-1) for v in vs])
    return jnp.pad(flat, (0, rows * 128 - flat.shape[0])).reshape(rows, 128)


def _unpack(packed, shapes):
    flat = packed.reshape(-1)
    out, off = [], 0
    for shp in shapes:
        size = math.prod(shp)
        out.append(flat[off:off + size].reshape(shp))
        off += size
    return out


def _rows_for(sizes, multiple):
    rows = -(-sum(sizes) // 128)
    return -(-rows // multiple) * multiple


def kernel(x, norm_mix_g, w_in, b_gate, conv_w, conv_b, lru_lambda, lru_wa, lru_ba, lru_wx, lru_bx, attn_sink, w_out, norm_ffn_g, w_ffn_in, w_ffn_out, norm_final_g, loss_target, m_norm_mix_g, m_w_in, m_b_gate, m_conv_w, m_conv_b, m_lru_lambda, m_lru_wa, m_lru_ba, m_lru_wx, m_lru_bx, m_attn_sink, m_w_out, m_norm_ffn_g, m_w_ffn_in, m_w_ffn_out, m_norm_final_g, v_norm_mix_g, v_w_in, v_b_gate, v_conv_w, v_conv_b, v_lru_lambda, v_lru_wa, v_lru_ba, v_lru_wx, v_lru_bx, v_attn_sink, v_w_out, v_norm_ffn_g, v_w_ffn_in, v_w_ffn_out, v_norm_final_g):
    S = x.shape[1]
    xs = x[0]
    tgt = loss_target[0]
    cx, cy, cc = lax.axis_index("x"), lax.axis_index("y"), lax.axis_index("c")
    chip = 2 * cx + cy
    SW = D // NCHIP

    small_shard = _pack([conv_w[0], lru_lambda[0], lru_ba[0], lru_bx[0]], 32)
    halves_of = lambda a: a.reshape(2, a.shape[0] // 2, a.shape[1])
    w_in_g, small_g = _gather_chips([halves_of(w_in[0].astype(bf16)), halves_of(small_shard)])
    w_in_g = w_in_g.reshape(NCHIP, D, SHW)
    small_g = small_g.reshape(NCHIP, 32, 128)
    late = [w_ffn_in[0].astype(bf16), w_out[0].astype(bf16), w_ffn_out[0].astype(bf16)]
    late_send, late_recv, late_src, late_land, late_token = _split_start(
        "gather_late_start", 9, _gather_copies(3), late, [_sds((NCHIP,) + a.shape, bf16) for a in late])
    small_parts = [_unpack(small_g[s], [(4, SW), (2, SW), (2, SW), (2, SW)]) for s in range(NCHIP)]
    conv_w_f, lam_f, ba_f, bx_f = [jnp.concatenate([small_parts[s][p] for s in range(NCHIP)], axis=1) for p in range(4)]
    wbd = jnp.concatenate([_block_diag_pairs(lru_wa[0, 0]), _block_diag_pairs(lru_wx[0, 0]),
                           _block_diag_pairs(lru_wa[0, 1]), _block_diag_pairs(lru_wx[0, 1])], axis=2).astype(bf16)
    conv_b_f = conv_b
    sink = attn_sink

    xn, proj = _rms_matmul("rms_proj", xs, norm_mix_g + late_token[0:1, 0:1], w_in_g, 1024)
    y_a, lru_state = _lru_fwd(proj, conv_w_f, conv_b_f, lam_f, ba_f, bx_f, wbd)
    y_b = _attn_fwd(proj, sink)
    merged = _merge_fwd(proj, b_gate, y_a, y_b, 512)
    late_src, late_land = _split_wait("gather_late_wait", _gather_copies(3), late_send, late_recv, late_src, late_land, merged)
    chip_arr = chip.reshape(1).astype(jnp.int32)
    w_ffn_in_g, w_out_g, w_ffn_out_g = _place_own(chip_arr, late_src, late_land, 4)
    w_out_f = w_out_g.reshape(D, D)
    w_ffn_out_f = w_ffn_out_g.reshape(FF, D)
    x1 = _mm_residual("out_proj", merged, w_out_f, xs, 512)
    xn2, gu, act = _rms_matmul_swiglu("rms_ffn_in", x1, norm_ffn_g, w_ffn_in_g, 1024)
    x2 = _mm_residual("ffn_out", act, w_ffn_out_f, x1, 512)
    dx2, loss_row, dg3 = _final_loss_bwd(x2, norm_final_g.reshape(1, D), tgt, 256)

    tm = min(1024, S)
    tk = min(2048, S)
    gw_ffn_out = _mm_tn("dw_ffn_out", act, pl.BlockSpec((tk, SHW), lambda i, k: (k, i)),
                        dx2, pl.BlockSpec((tk, D), lambda i, k: (k, 0)),
                        _sds((FF, D), f32), pl.BlockSpec((SHW, D), lambda i, k: (i, 0)), (2, S // tk), (SHW, D))
    dgu = _swiglu_bwd(dx2, w_ffn_out_f, gu, 256)
    dxn2 = _mm_nt_groups("dxn2", dgu, pl.BlockSpec((None, tm, SHW), lambda i, g: (g // 2, i, g % 2)), w_ffn_in_g, S, tm)
    gw_ffn_in = _mm_tn("dw_ffn_in", xn2, pl.BlockSpec((tk, D), lambda g, k: (k, 0)),
                       dgu, pl.BlockSpec((None, tk, SHW), lambda g, k: (g // 2, k, g % 2)),
                       _sds((NCHIP, D, SHW), f32), pl.BlockSpec((None, D, SHW), lambda g, k: (g, 0, 0)),
                       (NCHIP, S // tk), (D, SHW))
    c_arr = cc.reshape(1).astype(jnp.int32)
    early_names, early_tiles = ["w_ffn_in", "w_ffn_out"], [256, 352]
    early = [gw_ffn_in.reshape(NCHIP, 2, D // 2, SHW), gw_ffn_out.reshape(NCHIP, 2, FF // NCHIP // 2, D)]
    ea_send, ea_recv, ea_src, ea_land, ea_token = _split_start(
        "reduce_early_sibling_start", 2, _sibling_half_copies(2), early,
        [_sds((NCHIP,) + g.shape[2:], f32) for g in early])
    dx1, dg2 = _rms_bwd("rms_ffn_bwd", x1, norm_ffn_g + ea_token[0:1, 0:1], dxn2, dx2, 256)

    dmerged = _mm_nt_resident("d_merged", dx1, w_out_f, 512)
    gw_out = _mm_tn("dw_out", merged, pl.BlockSpec((tk, D), lambda i, k: (k, 0)),
                    dx1, pl.BlockSpec((tk, D), lambda i, k: (k, 0)),
                    _sds((D, D), f32), pl.BlockSpec((D, D), lambda i, k: (0, 0)), (1, S // tk), (D, D))
    dz0, dz1, dy_a, dy_b, db0, db1 = _merge_bwd(proj, b_gate, y_a, y_b, dmerged, 512)
    ea_src, ea_land = _split_wait("reduce_early_sibling_wait", _sibling_half_copies(2), ea_send, ea_recv, ea_src, ea_land, dy_b)
    early_pairs = [_pair_sum("pair_sum_" + nm, c_arr, g4, r, th)
                   for nm, g4, r, th in zip(early_names, ea_src, ea_land, early_tiles)]
    eb_send, eb_recv, eb_src, eb_land, eb_token = _split_start(
        "reduce_early_chips_start", 6, _chip_part_copies(2), [p[1] for p in early_pairs],
        [_sds((3,) + p[1].shape[1:], bf16) for p in early_pairs])
    dq, dk, dv, dsink = _attn_bwd(proj, sink + eb_token[0:1, 0:1], y_b, dy_b)
    _, eb_land = _split_wait("reduce_early_chips_wait", _chip_part_copies(2), eb_send, eb_recv, eb_src, eb_land, dq)
    early_halves = [_chip_sum("chip_sum_" + nm, chip_arr, p[0], r3, th)
                    for nm, p, r3, th in zip(early_names, early_pairs, eb_land, early_tiles)]
    ec_send, ec_recv, ec_src, ec_land, ec_token = _split_start(
        "reduce_early_share_start", 2, _sibling_whole_copies(2), early_halves, [_sds(h.shape, f32) for h in early_halves])
    du, dgl, dcw, dcb, dlam, dba, dbx, dwbd = _lru_bwd(proj, dy_a, lru_state, conv_w_f, conv_b_f + ec_token[0:1, 0:1], lam_f, ba_f, bx_f, wbd)
    early_halves, early_other = _split_wait("reduce_early_share_wait", _sibling_whole_copies(2), ec_send, ec_recv, ec_src, ec_land, du)
    dproj = jnp.concatenate([du, dgl, dq, dk.astype(bf16), dv.astype(bf16), dz0, dz1], axis=1)
    dxn = _mm_nt_groups("dxn", dproj, pl.BlockSpec((tm, SHW), lambda i, g: (i, g)), w_in_g, S, tm)
    gw_in = _mm_tn("dw_in", xn, pl.BlockSpec((tk, D), lambda g, k: (k, 0)),
                   dproj, pl.BlockSpec((tk, SHW), lambda g, k: (k, g)),
                   _sds((NCHIP, D, SHW), f32), pl.BlockSpec((None, D, SHW), lambda g, k: (g, 0, 0)),
                   (NCHIP, S // tk), (D, SHW))
    grad_x, dg1 = _rms_bwd("rms_mix_bwd", xs, norm_mix_g, dxn, dx1, 256)

    d_wa = jnp.stack([_diag_blocks(dwbd[:, :, 0:CW]), _diag_blocks(dwbd[:, :, 2 * CW:3 * CW])])
    d_wx = jnp.stack([_diag_blocks(dwbd[:, :, CW:2 * CW]), _diag_blocks(dwbd[:, :, 3 * CW:4 * CW])])
    small_full = [dg1, jnp.concatenate([db0, db1], axis=1), dcw, dcb, dlam, d_wa, dba, d_wx, dbx, dsink[:, 0], dg2, dg3,
                  loss_row[0, 0:1]]
    full_shapes = [(1, D), (1, 2 * D), (4, D), (1, D), (2, D), (2, NH, HD, HD), (2, D), (2, NH, HD, HD), (2, D), (NH,),
                   (1, D), (1, D), (1,)]
    rows_full = _rows_for([math.prod(s) for s in full_shapes], 16)
    small_vec = _pack(small_full, rows_full)

    late_names, late_tiles = ["w_in", "w_out"], [256, 128]
    big = [gw_in.reshape(NCHIP, 2, D // 2, SHW), gw_out.reshape(NCHIP, 2, D // NCHIP // 2, D)]
    *recv_a, small_sib = _sibling_halves(big, small_vec)
    pairs = [_pair_sum("pair_sum_" + nm, c_arr, g4, r, th) for nm, g4, r, th in zip(late_names, big, recv_a, late_tiles)]
    small_chip = _add2("pair_sum_small", small_vec, small_sib).reshape(2, rows_full // 2, 128)
    *recv_b, small_all = _exchange_chips([p[1] for p in pairs], small_chip)
    halves = [_chip_sum("chip_sum_" + nm, chip_arr, p[0], r3, th) for nm, p, r3, th in zip(late_names, pairs, recv_b, late_tiles)]
    halves.append(_sum4("chip_sum_small", small_all, rows_full // 2))
    *recv_c, small_other = _share_sibling(halves)
    small_lo = jnp.where(cc == 0, halves[2], small_other)
    small_hi = jnp.where(cc == 0, small_other, halves[2])
    g_full = _unpack(jnp.concatenate([small_lo, small_hi], axis=0), full_shapes)

    out_big = {}
    for nm, w, g_own, g_recv, m, v, th in zip(late_names + early_names, [w_in, w_out, w_ffn_in, w_ffn_out],
                                              halves[:2] + early_halves, recv_c + early_other,
                                              [m_w_in, m_w_out, m_w_ffn_in, m_w_ffn_out],
                                              [v_w_in, v_w_out, v_w_ffn_in, v_w_ffn_out], late_tiles + early_tiles):
        g_, d_, m_, v_ = _adamw_halves("adamw_" + nm, c_arr, w[0], g_own, g_recv, m[0], v[0], th)
        out_big[nm] = (g_[None], d_[None], m_[None], v_[None])

    small_names = ["norm_mix_g", "b_gate", "conv_w", "conv_b", "lru_lambda", "lru_wa", "lru_ba", "lru_wx", "lru_bx", "attn_sink",
                   "norm_ffn_g", "norm_final_g"]
    sharded = {"conv_w", "lru_lambda", "lru_ba", "lru_bx"}
    small_w = [norm_mix_g, b_gate, conv_w, conv_b, lru_lambda, lru_wa, lru_ba, lru_wx, lru_bx, attn_sink, norm_ffn_g, norm_final_g]
    small_m = [m_norm_mix_g, m_b_gate, m_conv_w, m_conv_b, m_lru_lambda, m_lru_wa, m_lru_ba, m_lru_wx, m_lru_bx, m_attn_sink,
               m_norm_ffn_g, m_norm_final_g]
    small_v = [v_norm_mix_g, v_b_gate, v_conv_w, v_conv_b, v_lru_lambda, v_lru_wa, v_lru_ba, v_lru_wx, v_lru_bx, v_attn_sink,
               v_norm_ffn_g, v_norm_final_g]
    g_local = []
    for nm, g, w in zip(small_names, g_full, small_w):
        if nm in sharded:
            g = lax.dynamic_slice_in_dim(g, chip * SW, SW, axis=1)
        g_local.append(g.reshape(w.shape))
    local_shapes = [w.shape for w in small_w]
    rows_local = _rows_for([math.prod(s) for s in local_shapes], 8)
    d_s, m_s, v_s = _adamw("adamw_small", _pack(small_w, rows_local), _pack(g_local, rows_local),
                           _pack(small_m, rows_local), _pack(small_v, rows_local), rows_local)
    d_l, m_l, v_l = _unpack(d_s, local_shapes), _unpack(m_s, local_shapes), _unpack(v_s, local_shapes)
    res = {nm: (g_local[i], d_l[i], m_l[i], v_l[i]) for i, nm in enumerate(small_names)}
    res.update(out_big)

    order = ["norm_mix_g", "w_in", "b_gate", "conv_w", "conv_b", "lru_lambda", "lru_wa", "lru_ba", "lru_wx", "lru_bx", "attn_sink",
             "w_out", "norm_ffn_g", "w_ffn_in", "w_ffn_out", "norm_final_g"]
    outs = [g_full[-1][0], grad_x[None]]
    for k in range(4):
        outs += [res[nm][k] for nm in order]
    return tuple(outs)
```

```python
import functools
import math

import jax
import jax.numpy as jnp
from jax import lax
from jax.experimental import pallas as pl
from jax.experimental.pallas import tpu as pltpu

f32 = jnp.float32
bf16 = jnp.bfloat16

D = 1024
NH = 16
HD = 64
FF = 2816
INW = 5632
NCHIP = 4
SHW = INW // NCHIP
CW = 128
NCH = D // CW
BLK = 128
EPS = 1e-6
NEG_INF = -1e30
RGLRU_C = 8.0
ADAM_LR, ADAM_B1, ADAM_B2, ADAM_EPS, ADAM_WD, ADAM_STEP = 0.001, 0.9, 0.999, 1e-08, 0.01, 10
VMEM_LIMIT = 58 * 1024 * 1024
MESH = pl.DeviceIdType.MESH
ANY = pl.BlockSpec(memory_space=pl.ANY)

COL_U, COL_G, COL_Q, COL_K, COL_V, COL_Z0, COL_Z1 = 0, 4, 8, 12, 13, 14, 18


def _params(n_axes, vmem=False):
    return pltpu.CompilerParams(dimension_semantics=("arbitrary",) * n_axes,
                                vmem_limit_bytes=VMEM_LIMIT if vmem else None)


def _sds(shape, dtype):
    return jax.ShapeDtypeStruct(tuple(shape), dtype)


_DIMS = {"nn": (((1,), (0,)), ((), ())), "nt": (((1,), (1,)), ((), ())), "tn": (((0,), (0,)), ((), ()))}


def _mm(name, mode, a, a_spec, b, b_spec, out_shape, out_spec, grid, nk, acc_shape, add=None, add_spec=None):
    has_add = add is not None

    def body(*refs):
        a_ref, b_ref = refs[0], refs[1]
        add_ref = refs[2] if has_add else None
        o_ref = refs[2 + has_add]
        part = lax.dot_general(a_ref[...].astype(bf16), b_ref[...].astype(bf16), _DIMS[mode],
                               preferred_element_type=f32)
        if nk == 1:
            if has_add:
                part = add_ref[...] + part
            o_ref[...] = part.astype(o_ref.dtype)
            return
        acc_ref = refs[3 + has_add]
        k = pl.program_id(len(grid) - 1)

        @pl.when(k == 0)
        def _():
            acc_ref[...] = part

        @pl.when(k > 0)
        def _():
            acc_ref[...] += part

        @pl.when(k == nk - 1)
        def _():
            res = acc_ref[...]
            if has_add:
                res = add_ref[...] + res
            o_ref[...] = res.astype(o_ref.dtype)

    ins = [a, b] + ([add] if has_add else [])
    in_specs = [a_spec, b_spec] + ([add_spec] if has_add else [])
    scratch = [pltpu.VMEM(acc_shape, f32)] if nk > 1 else []
    return pl.pallas_call(body, name=name, grid=grid, in_specs=in_specs, out_specs=out_spec, out_shape=out_shape,
                          scratch_shapes=scratch, compiler_params=_params(len(grid), True))(*ins)


def _rms_matmul(name, x, g, w3, tm):
    S, K = x.shape
    G, _, Nw = w3.shape
    tm = min(tm, S)

    def body(x_ref, g_ref, w_ref, xn_ref, o_ref, xs_ref):
        @pl.when(pl.program_id(1) == 0)
        def _():
            xf = x_ref[...]
            r = lax.rsqrt(jnp.mean(xf * xf, axis=-1, keepdims=True) + EPS)
            xn = ((xf * r) * g_ref[...]).astype(bf16)
            xs_ref[...] = xn
            xn_ref[...] = xn

        o_ref[...] = jnp.dot(xs_ref[...], w_ref[...], preferred_element_type=f32).astype(bf16)

    return pl.pallas_call(
        body, name=name, grid=(S // tm, G),
        in_specs=[pl.BlockSpec((tm, K), lambda i, j: (i, 0)), pl.BlockSpec((1, K), lambda i, j: (0, 0)),
                  pl.BlockSpec((None, K, Nw), lambda i, j: (j, 0, 0))],
        out_specs=[pl.BlockSpec((tm, K), lambda i, j: (i, 0)), pl.BlockSpec((tm, Nw), lambda i, j: (i, j))],
        out_shape=[_sds((S, K), bf16), _sds((S, G * Nw), bf16)],
        scratch_shapes=[pltpu.VMEM((tm, K), bf16)], compiler_params=_params(2, True))(x, g, w3)


def _rms_matmul_swiglu(name, x, g, w3, tm):
    S, K = x.shape
    G, _, Nw = w3.shape
    tm = min(tm, S)
    half = G // 2

    def body(x_ref, g_ref, wg_ref, wu_ref, xn_ref, gu_ref, act_ref, xs_ref):
        @pl.when(pl.program_id(1) == 0)
        def _():
            xf = x_ref[...]
            r = lax.rsqrt(jnp.mean(xf * xf, axis=-1, keepdims=True) + EPS)
            xn = ((xf * r) * g_ref[...]).astype(bf16)
            xs_ref[...] = xn
            xn_ref[...] = xn

        xn = xs_ref[...]
        gate = jnp.dot(xn, wg_ref[...], preferred_element_type=f32)
        up = jnp.dot(xn, wu_ref[...], preferred_element_type=f32)
        gu_ref[0] = gate.astype(bf16)
        gu_ref[1] = up.astype(bf16)
        act_ref[...] = ((gate * _sigmoid(gate)) * up).astype(bf16)

    return pl.pallas_call(
        body, name=name, grid=(S // tm, half),
        in_specs=[pl.BlockSpec((tm, K), lambda i, j: (i, 0)), pl.BlockSpec((1, K), lambda i, j: (0, 0)),
                  pl.BlockSpec((None, K, Nw), lambda i, j: (j, 0, 0)),
                  pl.BlockSpec((None, K, Nw), lambda i, j: (half + j, 0, 0))],
        out_specs=[pl.BlockSpec((tm, K), lambda i, j: (i, 0)), pl.BlockSpec((2, tm, Nw), lambda i, j: (0, i, j)),
                   pl.BlockSpec((tm, Nw), lambda i, j: (i, j))],
        out_shape=[_sds((S, K), bf16), _sds((2, S, half * Nw), bf16), _sds((S, half * Nw), bf16)],
        scratch_shapes=[pltpu.VMEM((tm, K), bf16)], compiler_params=_params(2, True))(x, g, w3, w3)


def _mm_residual(name, a, w, res, tm):
    S, K = a.shape
    N = w.shape[1]
    tm = min(tm, S)
    return _mm(name, "nn", a, pl.BlockSpec((tm, K), lambda i: (i, 0)), w, pl.BlockSpec((K, N), lambda i: (0, 0)),
               _sds((S, N), f32), pl.BlockSpec((tm, N), lambda i: (i, 0)), (S // tm,), 1, None,
               add=res, add_spec=pl.BlockSpec((tm, N), lambda i: (i, 0)))


def _mm_nt_resident(name, a, w, tm):
    S, K = a.shape
    N = w.shape[0]
    tm = min(tm, S)
    return _mm(name, "nt", a, pl.BlockSpec((tm, K), lambda i: (i, 0)), w, pl.BlockSpec((N, K), lambda i: (0, 0)),
               _sds((S, N), f32), pl.BlockSpec((tm, N), lambda i: (i, 0)), (S // tm,), 1, None)


def _mm_nt_groups(name, a, a_spec, w3, S, tm):
    G, Dout, Kw = w3.shape
    return _mm(name, "nt", a, a_spec, w3, pl.BlockSpec((None, Dout, Kw), lambda i, g: (g, 0, 0)),
               _sds((S, Dout), f32), pl.BlockSpec((tm, Dout), lambda i, g: (i, 0)), (S // tm, G), G, (tm, Dout))


def _mm_tn(name, a, a_spec, b, b_spec, out_shape, out_spec, grid, acc_shape):
    return _mm(name, "tn", a, a_spec, b, b_spec, out_shape, out_spec, grid, grid[-1], acc_shape)


def _sigmoid(x):
    return 0.5 * jnp.tanh(0.5 * x) + 0.5


_GELU_C = math.sqrt(2.0 / math.pi)


def _gelu_and_grad(x):
    v = _GELU_C * (x + 0.044715 * (x * x * x))
    t = jnp.tanh(v)
    gl = 0.5 * x * (1.0 + t)
    dgl = 0.5 * (1.0 + t) + 0.5 * x * (1.0 - t * t) * (_GELU_C * (1.0 + 3.0 * 0.044715 * (x * x)))
    return gl, dgl


def _one_minus_exp2x(x, ex):
    y = 2.0 * x
    series = y * (1.0 + y * (0.5 + y * (1.0 / 6.0 + y * (1.0 / 24.0))))
    return jnp.where(y > -1.0 / 64.0, -series, 1.0 - ex * ex)


def _merge_fwd(proj, b_gate, y_a, y_b, tm):
    S = proj.shape[0]
    tm = min(tm, S)

    def body(z0_ref, z1_ref, b0_ref, b1_ref, ya_ref, yb_ref, o_ref):
        g0 = _sigmoid(z0_ref[...].astype(f32) + b0_ref[...])
        g1 = _sigmoid(z1_ref[...].astype(f32) + b1_ref[...])
        o_ref[...] = (g0 * ya_ref[...].astype(f32) + g1 * yb_ref[...].astype(f32)).astype(bf16)

    blk = lambda off: pl.BlockSpec((tm, 256), lambda j, i: (i, off + j))
    vec = lambda off: pl.BlockSpec((1, 256), lambda j, i: (0, off + j))
    return pl.pallas_call(body, name="merge_fwd", grid=(4, S // tm),
                          in_specs=[blk(COL_Z0), blk(COL_Z1), vec(0), vec(4), blk(0), blk(0)],
                          out_specs=blk(0), out_shape=_sds((S, D), bf16),
                          compiler_params=_params(2))(proj, proj, b_gate, b_gate, y_a, y_b)


def _merge_bwd(proj, b_gate, y_a, y_b, dm, tm):
    S = proj.shape[0]
    tm = min(tm, S)

    def body(z0_ref, z1_ref, b0_ref, b1_ref, ya_ref, yb_ref, dm_ref, dz0_ref, dz1_ref, dya_ref, dyb_ref, db0_ref, db1_ref):
        g0 = _sigmoid(z0_ref[...].astype(f32) + b0_ref[...])
        g1 = _sigmoid(z1_ref[...].astype(f32) + b1_ref[...])
        d = dm_ref[...]
        dz0 = (d * ya_ref[...].astype(f32)) * (g0 * (1.0 - g0))
        dz1 = (d * yb_ref[...].astype(f32)) * (g1 * (1.0 - g1))
        dz0_ref[...] = dz0.astype(bf16)
        dz1_ref[...] = dz1.astype(bf16)
        dya_ref[...] = (d * g0).astype(bf16)
        dyb_ref[...] = (d * g1).astype(bf16)

        @pl.when(pl.program_id(1) == 0)
        def _():
            db0_ref[...] = jnp.zeros_like(db0_ref)
            db1_ref[...] = jnp.zeros_like(db1_ref)

        db0_ref[...] += jnp.sum(dz0, axis=0, keepdims=True)
        db1_ref[...] += jnp.sum(dz1, axis=0, keepdims=True)

    blk = lambda off: pl.BlockSpec((tm, 256), lambda j, i: (i, off + j))
    vec = lambda off: pl.BlockSpec((1, 256), lambda j, i: (0, off + j))
    return pl.pallas_call(
        body, name="merge_bwd", grid=(4, S // tm),
        in_specs=[blk(COL_Z0), blk(COL_Z1), vec(0), vec(4), blk(0), blk(0), blk(0)],
        out_specs=[blk(0), blk(0), blk(0), blk(0), vec(0), vec(0)],
        out_shape=[_sds((S, D), bf16), _sds((S, D), bf16), _sds((S, D), bf16), _sds((S, D), bf16),
                   _sds((1, D), f32), _sds((1, D), f32)],
        compiler_params=_params(2))(proj, proj, b_gate, b_gate, y_a, y_b, dm)


def _swiglu_bwd(dx, w, gu, tm):
    S, K = dx.shape
    tm = min(tm, S)

    def body(dx_ref, w_ref, gu_ref, o_ref):
        d = lax.dot_general(dx_ref[...].astype(bf16), w_ref[...], _DIMS["nt"], preferred_element_type=f32)
        g = gu_ref[0].astype(f32)
        u = gu_ref[1].astype(f32)
        s = _sigmoid(g)
        o_ref[0] = ((d * u) * (s * (1.0 + g * (1.0 - s)))).astype(bf16)
        o_ref[1] = (d * (g * s)).astype(bf16)

    stacked = pl.BlockSpec((2, tm, FF), lambda i: (0, i, 0))
    return pl.pallas_call(body, name="swiglu_bwd", grid=(S // tm,),
                          in_specs=[pl.BlockSpec((tm, K), lambda i: (i, 0)), pl.BlockSpec((FF, K), lambda i: (0, 0)), stacked],
                          out_specs=stacked, out_shape=_sds((2, S, FF), bf16),
                          compiler_params=_params(1, True))(dx, w, gu)


def _final_loss_bwd(x2, g3, tgt, tm):
    S = x2.shape[0]
    tm = min(tm, S)

    def body(x_ref, g_ref, t_ref, dx_ref, loss_ref, dg_ref):
        @pl.when(pl.program_id(0) == 0)
        def _():
            loss_ref[...] = jnp.zeros_like(loss_ref)
            dg_ref[...] = jnp.zeros_like(dg_ref)

        x = x_ref[...]
        g = g_ref[...]
        r = lax.rsqrt(jnp.mean(x * x, axis=-1, keepdims=True) + EPS)
        xh = x * r
        err = xh * g - t_ref[...]
        row = jnp.mean(err * err, axis=-1, keepdims=True)
        loss_ref[...] += 0.5 * jnp.sum(row, axis=0, keepdims=True)
        dy = err * (1.0 / D)
        dg_ref[...] += jnp.sum(dy * xh, axis=0, keepdims=True)
        dxh = dy * g
        dx_ref[...] = r * (dxh - xh * jnp.mean(dxh * xh, axis=-1, keepdims=True))

    row_blk = pl.BlockSpec((tm, D), lambda i: (i, 0))
    vec = pl.BlockSpec((1, D), lambda i: (0, 0))
    return pl.pallas_call(body, name="final_loss_bwd", grid=(S // tm,), in_specs=[row_blk, vec, row_blk],
                          out_specs=[row_blk, pl.BlockSpec((1, 128), lambda i: (0, 0)), vec],
                          out_shape=[_sds((S, D), f32), _sds((1, 128), f32), _sds((1, D), f32)],
                          compiler_params=_params(1))(x2, g3, tgt)


def _rms_bwd(name, x, g, dxn, dres, tm):
    S = x.shape[0]
    tm = min(tm, S)

    def body(x_ref, g_ref, d_ref, r_ref, dx_ref, dg_ref):
        @pl.when(pl.program_id(0) == 0)
        def _():
            dg_ref[...] = jnp.zeros_like(dg_ref)

        x = x_ref[...]
        d = d_ref[...]
        r = lax.rsqrt(jnp.mean(x * x, axis=-1, keepdims=True) + EPS)
        xh = x * r
        dg_ref[...] += jnp.sum(d * xh, axis=0, keepdims=True)
        dxh = d * g_ref[...]
        dx_ref[...] = r_ref[...] + r * (dxh - xh * jnp.mean(dxh * xh, axis=-1, keepdims=True))

    row_blk = pl.BlockSpec((tm, D), lambda i: (i, 0))
    vec = pl.BlockSpec((1, D), lambda i: (0, 0))
    return pl.pallas_call(body, name=name, grid=(S // tm,), in_specs=[row_blk, vec, row_blk, row_blk],
                          out_specs=[row_blk, vec], out_shape=[_sds((S, D), f32), _sds((1, D), f32)],
                          compiler_params=_params(1))(x, g, dxn, dres)


LRU_TT = 256
SCAN_UNROLL = 4


HALO = 16


def _halo(ref, i, S):
    nt = S // LRU_TT
    t0 = pl.multiple_of(i * LRU_TT, LRU_TT)
    p0 = pl.multiple_of(jnp.maximum(t0 - HALO, 0), HALO)
    n0 = pl.multiple_of(jnp.minimum(t0 + LRU_TT, S - HALO), HALO)
    prev = jnp.where(i > 0, ref[pl.ds(p0, HALO), :].astype(f32), 0.0)
    nxt = jnp.where(i < nt - 1, ref[pl.ds(n0, HALO), :].astype(f32), 0.0)
    return jnp.concatenate([prev, ref[pl.ds(t0, LRU_TT), :].astype(f32), nxt], axis=0)


def _shift(ext, k):
    n = LRU_TT + 2 * HALO
    return pltpu.roll(ext, (-k) % n, 0)[HALO:HALO + LRU_TT]


def _lru_gates(uc, wbd, ba, bx):
    pre = jnp.dot(uc.astype(bf16), wbd, preferred_element_type=f32)
    r_f = _sigmoid(pre[:, 0:CW] + ba[0:1])
    i_f = _sigmoid(pre[:, CW:2 * CW] + bx[0:1])
    r_b = _sigmoid(pre[:, 2 * CW:3 * CW] + ba[1:2])
    i_b = _sigmoid(pre[:, 3 * CW:4 * CW] + bx[1:2])
    return r_f, i_f, r_b, i_b


def _lru_coeffs(r, sp):
    log_a = (-RGLRU_C * r) * sp
    a = jnp.exp(log_a)
    beta = jnp.sqrt(jnp.maximum(_one_minus_exp2x(log_a, a), 0.0))
    return a, beta


def _lru_coeffs_inv(r, sp):
    log_a = (-RGLRU_C * r) * sp
    a = jnp.exp(log_a)
    om = jnp.maximum(_one_minus_exp2x(log_a, a), 0.0)
    return a, jnp.sqrt(om), lax.rsqrt(om)


def _conv_tile(u_ref, i, S, cw, cb):
    ext = _halo(u_ref, i, S)
    um2, um1, u0, up1 = _shift(ext, -2), _shift(ext, -1), ext[HALO:HALO + LRU_TT], _shift(ext, 1)
    uc = um2 * cw[0:1] + um1 * cw[1:2] + u0 * cw[2:3] + up1 * cw[3:4] + cb
    return uc, (um2, um1, u0, up1)


def _scan_pair(S, fwd_a, fwd_b, fwd_out, rev_a, rev_b, rev_out):
    ng = S // 8
    idx = lax.broadcasted_iota(jnp.int32, (8, CW), 0)

    def local(a, b, rev):
        for sh in (1, 2, 4):
            if rev:
                keep = idx < 8 - sh
                amt = 8 - sh
            else:
                keep = idx >= sh
                amt = sh
            a_s = jnp.where(keep, pltpu.roll(a, amt, 0), 1.0)
            b_s = jnp.where(keep, pltpu.roll(b, amt, 0), 0.0)
            b = a * b_s + b
            a = a * a_s
        return a, b

    def step(it, carry):
        cf, cr = carry
        fwd_rows = [pl.multiple_of((it * SCAN_UNROLL + j) * 8, 8) for j in range(SCAN_UNROLL)]
        rev_rows = [pl.multiple_of((ng - 1 - (it * SCAN_UNROLL + j)) * 8, 8) for j in range(SCAN_UNROLL)]
        fwd_loc = [local(fwd_a(r), fwd_b(r), False) for r in fwd_rows]
        rev_loc = [local(rev_a(r), rev_b(r), True) for r in rev_rows]
        for j in range(SCAN_UNROLL):
            a, b = fwd_loc[j]
            h = a * cf + b
            fwd_out[pl.ds(fwd_rows[j], 8), :] = h
            cf = jnp.broadcast_to(h[7:8, :], (8, CW))
            a, b = rev_loc[j]
            h = a * cr + b
            rev_out[pl.ds(rev_rows[j], 8), :] = h
            cr = jnp.broadcast_to(h[0:1, :], (8, CW))
        return cf, cr

    zero = jnp.zeros((8, CW), f32)
    lax.fori_loop(0, ng // SCAN_UNROLL, step, (zero, zero))


def _lru_specs(S):
    seq = lambda off: pl.BlockSpec((S, CW), lambda j: (0, off + j))
    par = lambda rows: pl.BlockSpec((rows, CW), lambda j: (0, j))
    return seq, par


def _lru_fwd(proj, conv_w, conv_b, lam, ba, bx, wbd):
    S = proj.shape[0]
    nt = S // LRU_TT

    def body(u_ref, g_ref, cw_ref, cb_ref, lam_ref, ba_ref, bx_ref, wbd_ref, y_ref, state_ref, af_ref, bf_ref, ab_ref, bb_ref,
             sems):
        cw, cb, ba_v, bx_v, wbd_v = cw_ref[...], cb_ref[...], ba_ref[...], bx_ref[...], wbd_ref[...]
        sp = jax.nn.softplus(-lam_ref[...])
        cols = pl.ds(pl.multiple_of(pl.program_id(0) * CW, CW), CW)
        save = [pltpu.make_async_copy(ref, state_ref.at[k, :, cols], sems.at[k])
                for k, ref in enumerate((af_ref, bf_ref, ab_ref, bb_ref))]

        def phase1(i, c):
            uc, _ = _conv_tile(u_ref, i, S, cw, cb)
            r_f, i_f, r_b, i_b = _lru_gates(uc, wbd_v, ba_v, bx_v)
            rows = pl.ds(pl.multiple_of(i * LRU_TT, LRU_TT), LRU_TT)
            a, beta = _lru_coeffs(r_f, sp[0:1])
            af_ref[rows, :] = a
            bf_ref[rows, :] = beta * (i_f * uc)
            a, beta = _lru_coeffs(r_b, sp[1:2])
            ab_ref[rows, :] = a
            bb_ref[rows, :] = beta * (i_b * uc)
            return c

        lax.fori_loop(0, nt, phase1, 0)
        row8 = lambda ref: (lambda r0: ref[pl.ds(r0, 8), :])
        _scan_pair(S, row8(af_ref), row8(bf_ref), bf_ref, row8(ab_ref), row8(bb_ref), bb_ref)
        for cp in save:
            cp.start()

        def phase3(i, c):
            rows = pl.ds(pl.multiple_of(i * LRU_TT, LRU_TT), LRU_TT)
            y = (bf_ref[rows, :] + bb_ref[rows, :]) * jax.nn.gelu(g_ref[rows, :].astype(f32))
            y_ref[rows, :] = y.astype(y_ref.dtype)
            return c

        lax.fori_loop(0, nt, phase3, 0)
        for cp in save:
            cp.wait()

    seq, par = _lru_specs(S)
    return pl.pallas_call(
        body, name="lru_fwd", grid=(NCH,),
        in_specs=[seq(0), seq(NCH), par(4), par(1), par(2), par(2), par(2),
                  pl.BlockSpec((None, CW, 4 * CW), lambda j: (j, 0, 0))],
        out_specs=[seq(0), ANY], out_shape=[_sds((S, D), bf16), _sds((4, S, D), f32)],
        scratch_shapes=[pltpu.VMEM((S, CW), f32)] * 4 + [pltpu.SemaphoreType.DMA((4,))], compiler_params=_params(1, True),
    )(proj, proj, conv_w, conv_b, lam, ba, bx, wbd)


def _lru_bwd(proj, dy, state, conv_w, conv_b, lam, ba, bx, wbd):
    S = proj.shape[0]
    nt = S // LRU_TT

    def body(u_ref, g_ref, dy_ref, state_ref, cw_ref, cb_ref, lam_ref, ba_ref, bx_ref, wbd_ref,
             du_ref, dg_ref, dcw_ref, dcb_ref, dlam_ref, dba_ref, dbx_ref, dwbd_ref,
             af_ref, bf_ref, ab_ref, bb_ref, dh_ref, sems):
        cw, cb, ba_v, bx_v, wbd_v = cw_ref[...], cb_ref[...], ba_ref[...], bx_ref[...], wbd_ref[...]
        lam_v = lam_ref[...]
        sp = jax.nn.softplus(-lam_v)
        cols = pl.ds(pl.multiple_of(pl.program_id(0) * CW, CW), CW)
        load = [pltpu.make_async_copy(state_ref.at[k, :, cols], ref, sems.at[k])
                for k, ref in enumerate((af_ref, bf_ref, ab_ref, bb_ref))]
        for cp in load:
            cp.start()
        for cp in load:
            cp.wait()
        row8 = lambda ref: (lambda r0: ref[pl.ds(r0, 8), :])

        def phase0(i, c):
            rows = pl.ds(pl.multiple_of(i * LRU_TT, LRU_TT), LRU_TT)
            gl, dgl = _gelu_and_grad(g_ref[rows, :].astype(f32))
            dyt = dy_ref[rows, :].astype(f32)
            dh_ref[rows, :] = dyt * gl
            dg_ref[rows, :] = ((dyt * (bf_ref[rows, :] + bb_ref[rows, :])) * dgl).astype(dg_ref.dtype)
            return c

        lax.fori_loop(0, nt, phase0, 0)

        def scaled_dh(a_ref):
            def f(r0):
                return a_ref[pl.ds(r0, 8), :] * dh_ref[pl.ds(r0, 8), :]
            return f

        _scan_pair(S, row8(ab_ref), scaled_dh(ab_ref), ab_ref, row8(af_ref), scaled_dh(af_ref), af_ref)

        dcw_ref[...] = jnp.zeros_like(dcw_ref)
        dcb_ref[...] = jnp.zeros_like(dcb_ref)
        dlam_ref[...] = jnp.zeros_like(dlam_ref)
        dba_ref[...] = jnp.zeros_like(dba_ref)
        dbx_ref[...] = jnp.zeros_like(dbx_ref)
        dwbd_ref[...] = jnp.zeros_like(dwbd_ref)

        def direction(uc, r, i_g, dht, h_nb, sp_d):
            a, beta, inv_beta = _lru_coeffs_inv(r, sp_d)
            da = dht * h_nb
            dbeta = dht * (i_g * uc)
            d_iu = dht * beta
            dlog_a = da * a - (a * a) * (dbeta * inv_beta)
            dlr = dlog_a * r
            dsp = -RGLRU_C * jnp.sum(dlr, axis=0, keepdims=True)
            dpre_r = (dlr * (1.0 - r)) * (-RGLRU_C * sp_d)
            dpre_i = (d_iu * uc) * (i_g * (1.0 - i_g))
            return dpre_r, dpre_i, d_iu * i_g, dsp

        def phase4(i, c):
            uc, (um2, um1, u0, up1) = _conv_tile(u_ref, i, S, cw, cb)
            r_f, i_f, r_b, i_b = _lru_gates(uc, wbd_v, ba_v, bx_v)
            rows = pl.ds(pl.multiple_of(i * LRU_TT, LRU_TT), LRU_TT)
            dh = dh_ref[rows, :]
            dht_f = dh + _shift(_halo(af_ref, i, S), 1)
            h_prev = _shift(_halo(bf_ref, i, S), -1)
            dht_b = dh + _shift(_halo(ab_ref, i, S), -1)
            h_next = _shift(_halo(bb_ref, i, S), 1)
            prf, pif, duc_f, dsp_f = direction(uc, r_f, i_f, dht_f, h_prev, sp[0:1])
            prb, pib, duc_b, dsp_b = direction(uc, r_b, i_b, dht_b, h_next, sp[1:2])
            dpre = jnp.concatenate([prf, pif, prb, pib], axis=1)
            dpre_b = dpre.astype(bf16)
            duc = (duc_f + duc_b) + lax.dot_general(dpre_b, wbd_v, _DIMS["nt"], preferred_element_type=f32)
            dwbd_ref[...] += lax.dot_general(uc.astype(bf16), dpre_b, _DIMS["tn"], preferred_element_type=f32)
            colsum = lambda v: jnp.sum(v, axis=0, keepdims=True)
            dba_ref[...] += jnp.concatenate([colsum(prf), colsum(prb)], axis=0)
            dbx_ref[...] += jnp.concatenate([colsum(pif), colsum(pib)], axis=0)
            dlam_ref[...] += jnp.concatenate([dsp_f, dsp_b], axis=0)
            dcb_ref[...] += colsum(duc)
            dcw_ref[...] += jnp.concatenate([colsum(duc * um2), colsum(duc * um1), colsum(duc * u0),
                                             colsum(duc * up1)], axis=0)
            af_ref[rows, :] = duc
            return c

        lax.fori_loop(0, nt, phase4, 0)
        dlam_ref[...] = dlam_ref[...] * (-_sigmoid(-lam_v))

        def phase5(i, c):
            ext = _halo(af_ref, i, S)
            rows = pl.ds(pl.multiple_of(i * LRU_TT, LRU_TT), LRU_TT)
            du = (_shift(ext, 2) * cw[0:1] + _shift(ext, 1) * cw[1:2] + ext[HALO:HALO + LRU_TT] * cw[2:3]
                  + _shift(ext, -1) * cw[3:4])
            du_ref[rows, :] = du.astype(du_ref.dtype)
            return c

        lax.fori_loop(0, nt, phase5, 0)

    seq, par = _lru_specs(S)
    return pl.pallas_call(
        body, name="lru_bwd", grid=(NCH,),
        in_specs=[seq(0), seq(NCH), seq(0), ANY, par(4), par(1), par(2), par(2), par(2),
                  pl.BlockSpec((None, CW, 4 * CW), lambda j: (j, 0, 0))],
        out_specs=[seq(0), seq(0), par(4), par(1), par(2), par(2), par(2),
                   pl.BlockSpec((None, CW, 4 * CW), lambda j: (j, 0, 0))],
        out_shape=[_sds((S, D), bf16), _sds((S, D), bf16), _sds((4, D), f32), _sds((1, D), f32), _sds((2, D), f32),
                   _sds((2, D), f32), _sds((2, D), f32), _sds((NCH, CW, 4 * CW), f32)],
        scratch_shapes=[pltpu.VMEM((S, CW), f32)] * 5 + [pltpu.SemaphoreType.DMA((4,))], compiler_params=_params(1, True),
    )(proj, proj, dy, state, conv_w, conv_b, lam, ba, bx, wbd)


_SLOPES = [2.0 ** (-8.0 * (h + 1) / NH) for h in range(NH)]


def _half_mask(shape, e):
    lane = lax.broadcasted_iota(jnp.int32, shape, 1)
    return (lane < HD) if e == 0 else (lane >= HD)


def _both_halves(x, src):
    return jnp.where(_half_mask(x.shape, src), x, pltpu.roll(x, HD, 1))


def _fold_halves(x, dst):
    return jnp.where(_half_mask(x.shape, dst), x + pltpu.roll(x, HD, 1), 0.0)


def _attn_base(n, S):
    tq = lax.broadcasted_iota(jnp.int32, (BLK, 3 * BLK), 0)
    sk = lax.broadcasted_iota(jnp.int32, (BLK, 3 * BLK), 1)
    dist = jnp.abs(tq + BLK - sk)
    kpos = n * BLK - BLK + sk
    valid = (dist <= BLK) & (kpos >= 0) & (kpos < S)
    return jnp.where(valid, -dist.astype(f32), NEG_INF)


def _group_heads(ref, kvh, scale):
    parts = []
    for i in range(4):
        pair = 2 * kvh + i // 2
        x = ref[:, pair * 128:(pair + 1) * 128].astype(f32)
        parts.append(jnp.where(_half_mask(x.shape, i % 2), x * scale, 0.0))
    return parts


def _stack_bf16(parts):
    return jnp.concatenate([p.astype(bf16) for p in parts], axis=0)


def _attn_softmax(s_raw, base, slope, sink):
    s = s_raw + slope * base
    m = jnp.maximum(jnp.max(s, axis=-1, keepdims=True), sink)
    p = jnp.exp(s - m)
    esink = jnp.exp(sink - m)
    inv = 1.0 / (jnp.sum(p, axis=-1, keepdims=True) + esink)
    return p, inv, esink * inv


def _attn_specs(S):
    nb = S // BLK
    q_spec = pl.BlockSpec((BLK, D), lambda n: (n, 2))
    kv = lambda col: [pl.BlockSpec((BLK, 256), lambda n: (jnp.maximum(n - 1, 0), col)),
                      pl.BlockSpec((BLK, 256), lambda n: (n, col)),
                      pl.BlockSpec((BLK, 256), lambda n: (jnp.minimum(n + 1, nb - 1), col))]
    return nb, q_spec, kv(COL_K), kv(COL_V)


def _attn_fwd(proj, sink):
    S = proj.shape[0]
    nb, q_spec, k_specs, v_specs = _attn_specs(S)

    def body(sink_ref, q_ref, kp_ref, kc_ref, kn_ref, vp_ref, vc_ref, vn_ref, o_ref):
        base = _attn_base(pl.program_id(0), S)
        kcat = jnp.concatenate([kp_ref[...], kc_ref[...], kn_ref[...]], axis=0).astype(f32)
        vcat = jnp.concatenate([vp_ref[...], vc_ref[...], vn_ref[...]], axis=0).astype(f32)
        even = _half_mask((BLK, 128), 0)
        for kvh in range(NH // 4):
            ch, off = kvh // 2, kvh % 2
            kb = _both_halves(kcat[:, ch * 128:(ch + 1) * 128], off).astype(bf16)
            vb = _both_halves(vcat[:, ch * 128:(ch + 1) * 128], off).astype(bf16)
            q4 = _stack_bf16(_group_heads(q_ref, kvh, HD ** -0.5))
            s4 = lax.dot_general(q4, kb, _DIMS["nt"], preferred_element_type=f32)
            ps, invs = [], []
            for i in range(4):
                h = 4 * kvh + i
                p, inv, _ = _attn_softmax(s4[i * BLK:(i + 1) * BLK], base, _SLOPES[h], sink_ref[0, h])
                ps.append(p)
                invs.append(inv)
            o4 = jnp.dot(_stack_bf16(ps), vb, preferred_element_type=f32)
            for pr in range(2):
                lo = o4[(2 * pr) * BLK:(2 * pr + 1) * BLK] * invs[2 * pr]
                hi = o4[(2 * pr + 1) * BLK:(2 * pr + 2) * BLK] * invs[2 * pr + 1]
                pair = 2 * kvh + pr
                o_ref[:, pair * 128:(pair + 1) * 128] = jnp.where(even, lo, hi).astype(o_ref.dtype)

    return pl.pallas_call(
        body, name="attn_fwd", grid=(nb,),
        in_specs=[pl.BlockSpec(memory_space=pltpu.SMEM), q_spec] + k_specs + v_specs,
        out_specs=pl.BlockSpec((BLK, D), lambda n: (n, 0)), out_shape=_sds((S, D), bf16),
        compiler_params=_params(1, True))(sink, proj, proj, proj, proj, proj, proj, proj)


def _attn_bwd(proj, sink, y_b, dy_b):
    S = proj.shape[0]
    nb, q_spec, k_specs, v_specs = _attn_specs(S)

    def body(sink_ref, q_ref, kp_ref, kc_ref, kn_ref, vp_ref, vc_ref, vn_ref, o_ref, do_ref,
             dq_ref, dk_ref, dv_ref, dsink_ref):
        n = pl.program_id(0)

        @pl.when(n == 0)
        def _():
            dk_ref[...] = jnp.zeros_like(dk_ref)
            dv_ref[...] = jnp.zeros_like(dv_ref)
            dsink_ref[...] = jnp.zeros_like(dsink_ref)

        base = _attn_base(n, S)
        kcat = jnp.concatenate([kp_ref[...], kc_ref[...], kn_ref[...]], axis=0).astype(f32)
        vcat = jnp.concatenate([vp_ref[...], vc_ref[...], vn_ref[...]], axis=0).astype(f32)
        dk_acc = [jnp.zeros((3 * BLK, 128), f32), jnp.zeros((3 * BLK, 128), f32)]
        dv_acc = [jnp.zeros((3 * BLK, 128), f32), jnp.zeros((3 * BLK, 128), f32)]
        scale = HD ** -0.5
        even = _half_mask((BLK, 128), 0)
        for kvh in range(NH // 4):
            ch, off = kvh // 2, kvh % 2
            kb = _both_halves(kcat[:, ch * 128:(ch + 1) * 128], off).astype(bf16)
            vb = _both_halves(vcat[:, ch * 128:(ch + 1) * 128], off).astype(bf16)
            q_parts = _group_heads(q_ref, kvh, scale)
            d_parts = _group_heads(do_ref, kvh, 1.0)
            s4 = lax.dot_general(_stack_bf16(q_parts), kb, _DIMS["nt"], preferred_element_type=f32)
            dp4 = lax.dot_general(_stack_bf16(d_parts), vb, _DIMS["nt"], preferred_element_type=f32)
            ts, ps, qn, dn, invs = [], [], [], [], []
            for i in range(4):
                h = 4 * kvh + i
                pair = 2 * kvh + i // 2
                rows = slice(i * BLK, (i + 1) * BLK)
                p, inv, psink = _attn_softmax(s4[rows], base, _SLOPES[h], sink_ref[0, h])
                delta = jnp.sum(d_parts[i] * o_ref[:, pair * 128:(pair + 1) * 128].astype(f32), axis=-1, keepdims=True)
                dsink_ref[h:h + 1, :] += jnp.broadcast_to(-jnp.sum(psink * delta, axis=0, keepdims=True), (1, 128))
                ts.append(p * (dp4[rows] - delta))
                ps.append(p)
                qn.append(q_parts[i] * inv)
                dn.append(d_parts[i] * inv)
                invs.append(inv)
            t4 = _stack_bf16(ts)
            dq4 = jnp.dot(t4, kb, preferred_element_type=f32)
            for pr in range(2):
                lo = dq4[(2 * pr) * BLK:(2 * pr + 1) * BLK] * invs[2 * pr]
                hi = dq4[(2 * pr + 1) * BLK:(2 * pr + 2) * BLK] * invs[2 * pr + 1]
                pair = 2 * kvh + pr
                dq_ref[:, pair * 128:(pair + 1) * 128] = (jnp.where(even, lo, hi) * scale).astype(dq_ref.dtype)
            dk_both = lax.dot_general(t4, _stack_bf16(qn), _DIMS["tn"], preferred_element_type=f32)
            dv_both = lax.dot_general(_stack_bf16(ps), _stack_bf16(dn), _DIMS["tn"], preferred_element_type=f32)
            dk_acc[ch] = dk_acc[ch] + _fold_halves(dk_both, off)
            dv_acc[ch] = dv_acc[ch] + _fold_halves(dv_both, off)
        for j in range(3):
            blk = n + (j - 1)

            @pl.when((blk >= 0) & (blk < nb))
            def _():
                rows = pl.ds(pl.multiple_of(blk * BLK, BLK), BLK)
                for ch in range(2):
                    dk_ref[rows, ch * 128:(ch + 1) * 128] += dk_acc[ch][j * BLK:(j + 1) * BLK]
                    dv_ref[rows, ch * 128:(ch + 1) * 128] += dv_acc[ch][j * BLK:(j + 1) * BLK]

    row_blk = pl.BlockSpec((BLK, D), lambda n: (n, 0))
    full = pl.BlockSpec((S, 256), lambda n: (0, 0))
    return pl.pallas_call(
        body, name="attn_bwd", grid=(nb,),
        in_specs=[pl.BlockSpec(memory_space=pltpu.SMEM), q_spec] + k_specs + v_specs + [row_blk, row_blk],
        out_specs=[row_blk, full, full, pl.BlockSpec((NH, 128), lambda n: (0, 0))],
        out_shape=[_sds((S, D), bf16), _sds((S, 256), f32), _sds((S, 256), f32), _sds((NH, 128), f32)],
        compiler_params=_params(1, True))(sink, proj, proj, proj, proj, proj, proj, proj, y_b, dy_b)


def _adamw(name, w, g, m, v, tr):
    R, C = w.shape
    tr = min(tr, R)

    def body(w_ref, g_ref, m_ref, v_ref, d_ref, m2_ref, v2_ref):
        g = g_ref[...]
        m2 = ADAM_B1 * m_ref[...] + (1.0 - ADAM_B1) * g
        v2 = ADAM_B2 * v_ref[...] + (1.0 - ADAM_B2) * (g * g)
        m_hat = m2 / (1.0 - ADAM_B1 ** ADAM_STEP)
        v_hat = v2 / (1.0 - ADAM_B2 ** ADAM_STEP)
        d_ref[...] = -ADAM_LR * (m_hat / (jnp.sqrt(v_hat) + ADAM_EPS) + ADAM_WD * w_ref[...])
        m2_ref[...] = m2
        v2_ref[...] = v2

    blk = pl.BlockSpec((tr, C), lambda i: (i, 0))
    return pl.pallas_call(body, name=name, grid=(R // tr,), in_specs=[blk] * 4, out_specs=[blk] * 3,
                          out_shape=[_sds((R, C), f32)] * 3, compiler_params=_params(1))(w, g, m, v)


def _pair_sum(name, c_arr, g4, recv, th):
    _, _, h, w = g4.shape
    th = min(th, h)

    def body(c_ref, g_ref, r_ref, o_ref, ob_ref):
        p = g_ref[...] + r_ref[...]
        o_ref[...] = p
        ob_ref[...] = p.astype(bf16)

    blk = pl.BlockSpec((None, th, w), lambda s, i, c_ref: (s, i, 0))
    spec = pltpu.PrefetchScalarGridSpec(
        num_scalar_prefetch=1, grid=(NCHIP, h // th),
        in_specs=[pl.BlockSpec((None, None, th, w), lambda s, i, c_ref: (s, c_ref[0], i, 0)), blk],
        out_specs=[blk, blk])
    return pl.pallas_call(body, name=name, grid_spec=spec,
                          out_shape=[_sds((NCHIP, h, w), f32), _sds((NCHIP, h, w), bf16)],
                          compiler_params=_params(2))(c_arr, g4, recv)


def _chip_sum(name, chip_arr, own4, recv3, th):
    _, h, w = own4.shape
    th = min(th, h)

    def body(s_ref, o_ref, r_ref, out_ref):
        out_ref[...] = ((o_ref[...] + r_ref[0].astype(f32)) + r_ref[1].astype(f32)) + r_ref[2].astype(f32)

    spec = pltpu.PrefetchScalarGridSpec(
        num_scalar_prefetch=1, grid=(h // th,),
        in_specs=[pl.BlockSpec((None, th, w), lambda i, s_ref: (s_ref[0], i, 0)),
                  pl.BlockSpec((3, th, w), lambda i, s_ref: (0, i, 0))],
        out_specs=pl.BlockSpec((th, w), lambda i, s_ref: (i, 0)))
    return pl.pallas_call(body, name=name, grid_spec=spec, out_shape=_sds((h, w), f32),
                          compiler_params=_params(1, True))(chip_arr, own4, recv3)


def _adamw_halves(name, c_arr, w, g_own, g_recv, m, v, th):
    h, wd = g_own.shape
    th = min(th, h)

    def body(c_ref, w_ref, go_ref, gr_ref, m_ref, v_ref, g_ref, d_ref, m2_ref, v2_ref):
        g = jnp.where(c_ref[0] == pl.program_id(0), go_ref[...], gr_ref[...])
        m2 = ADAM_B1 * m_ref[...] + (1.0 - ADAM_B1) * g
        v2 = ADAM_B2 * v_ref[...] + (1.0 - ADAM_B2) * (g * g)
        m_hat = m2 / (1.0 - ADAM_B1 ** ADAM_STEP)
        v_hat = v2 / (1.0 - ADAM_B2 ** ADAM_STEP)
        g_ref[...] = g
        d_ref[...] = -ADAM_LR * (m_hat / (jnp.sqrt(v_hat) + ADAM_EPS) + ADAM_WD * w_ref[...])
        m2_ref[...] = m2
        v2_ref[...] = v2

    nt = h // th
    full = pl.BlockSpec((th, wd), lambda hh, i, c_ref: (hh * nt + i, 0))
    half = pl.BlockSpec((th, wd), lambda hh, i, c_ref: (i, 0))
    spec = pltpu.PrefetchScalarGridSpec(num_scalar_prefetch=1, grid=(2, nt),
                                        in_specs=[full, half, half, full, full], out_specs=[full] * 4)
    return pl.pallas_call(body, name=name, grid_spec=spec, out_shape=[_sds((2 * h, wd), f32)] * 4,
                          compiler_params=_params(2))(c_arr, w, g_own, g_recv, m, v)


def _add2(name, a, b):
    def body(a_ref, b_ref, o_ref):
        o_ref[...] = a_ref[...] + b_ref[...]
    return pl.pallas_call(body, name=name, out_shape=_sds(a.shape, f32))(a, b)


def _sum4(name, b4, th):
    _, h, w = b4.shape
    th = min(th, h)

    def body(b_ref, o_ref):
        o_ref[...] = ((b_ref[0] + b_ref[1]) + b_ref[2]) + b_ref[3]

    return pl.pallas_call(body, name=name, grid=(h // th,),
                          in_specs=[pl.BlockSpec((NCHIP, th, w), lambda i: (0, i, 0))],
                          out_specs=pl.BlockSpec((th, w), lambda i: (i, 0)), out_shape=_sds((h, w), f32),
                          compiler_params=_params(1, True))(b4)


def _coords():
    x, y, c = lax.axis_index("x"), lax.axis_index("y"), lax.axis_index("c")
    return x, y, c, [(1 - x, y), (x, 1 - y), (1 - x, 1 - y)]


def _gather_chips(arrs):
    n = len(arrs)

    def body(*refs):
        ins, outs = refs[:n], refs[n:2 * n]
        send_sems, recv_sems, local_sems = refs[2 * n:2 * n + 3]
        stage = refs[2 * n + 3:]
        x, y, c, chips = _coords()
        s = 2 * x + y
        sib = (x, y, 1 - c)
        load = [pltpu.make_async_copy(ins[a], stage[a], local_sems.at[a]) for a in range(n)]
        local = [pltpu.make_async_copy(stage[a], outs[a].at[s], local_sems.at[n + a]) for a in range(n)]
        for cp in load:
            cp.start()

        def over_ici(k, a, slot, peer):
            return pltpu.make_async_remote_copy(src_ref=ins[a].at[c], dst_ref=outs[a].at[slot, c], send_sem=send_sems.at[k * n + a],
                                                recv_sem=recv_sems.at[k * n + a], device_id=peer, device_id_type=MESH)

        def to_sibling(k, a, slot, half):
            i = (3 + k) * n + a
            return pltpu.make_async_remote_copy(src_ref=outs[a].at[slot, half], dst_ref=outs[a].at[slot, half], send_sem=send_sems.at[i],
                                                recv_sem=recv_sems.at[i], device_id=sib, device_id_type=MESH)

        sends = [over_ici(k, a, s, (px, py, c)) for k, (px, py) in enumerate(chips) for a in range(n)]
        for cp in sends:
            cp.start()
        for a in range(n):
            load[a].wait()
            local[a].start()
        passed = []
        for k, (px, py) in enumerate(chips):
            for a in range(n):
                over_ici(k, a, 2 * px + py, (px, py, c)).wait_recv()
                cp = to_sibling(k, a, 2 * px + py, c)
                cp.start()
                passed.append(cp)
        for k, (px, py) in enumerate(chips):
            for a in range(n):
                to_sibling(k, a, 2 * px + py, 1 - c).wait_recv()
        for cp in sends + passed:
            cp.wait_send()
        for cp in local:
            cp.wait()

    return pl.pallas_call(
        body, name="gather_weights", in_specs=[ANY] * n, out_specs=[ANY] * n,
        out_shape=[_sds((NCHIP,) + a.shape, a.dtype) for a in arrs],
        scratch_shapes=[pltpu.SemaphoreType.DMA((6 * n,)), pltpu.SemaphoreType.DMA((6 * n,)), pltpu.SemaphoreType.DMA((2 * n,))]
        + [pltpu.VMEM(a.shape, a.dtype) for a in arrs],
        compiler_params=pltpu.CompilerParams(vmem_limit_bytes=VMEM_LIMIT),
    )(*arrs)


HBM = pl.BlockSpec(memory_space=pltpu.HBM)
SEM = pl.BlockSpec(memory_space=pltpu.SEMAPHORE)
EFFECT = pltpu.SideEffectType.DATAFLOW_SIDE_EFFECTING


def _split_start(name, n_copies, make_copies, ins, land_shapes):
    ni, nl = len(ins), len(land_shapes)

    def body(*refs):
        in_refs, land_refs = refs[:ni], refs[ni:ni + nl]
        send_sems, recv_sems = refs[ni + nl], refs[ni + nl + 1]
        token = refs[-1]
        for cp in make_copies(in_refs, land_refs, send_sems, recv_sems):
            cp.start()
        token[...] = jnp.zeros_like(token)

    lands = [pltpu.with_memory_space_constraint(lax.empty(s.shape, s.dtype), pltpu.HBM) for s in land_shapes]
    res = pl.pallas_call(
        body, name=name,
        out_shape=(pltpu.SemaphoreType.DMA((n_copies,)), pltpu.SemaphoreType.DMA((n_copies,)),
                   *[pltpu.HBM(a.shape, a.dtype) for a in ins], *[pltpu.HBM(s.shape, s.dtype) for s in land_shapes],
                   _sds((8, 128), f32)),
        in_specs=[HBM] * (ni + nl), out_specs=(SEM, SEM, *[HBM] * (ni + nl), pl.BlockSpec(memory_space=pltpu.VMEM)),
        input_output_aliases={i: 2 + i for i in range(ni + nl)},
        compiler_params=pltpu.CompilerParams(has_side_effects=EFFECT),
    )(*[pltpu.with_memory_space_constraint(a, pltpu.HBM) for a in ins], *lands)
    return res[0], res[1], list(res[2:2 + ni]), list(res[2 + ni:2 + ni + nl]), res[-1]


def _split_wait(name, make_copies, send_sems, recv_sems, ins, lands, after):
    ni, nl = len(ins), len(lands)

    def body(*refs):
        in_refs, land_refs = refs[:ni], refs[ni:ni + nl]
        s_sems, r_sems = refs[ni + nl], refs[ni + nl + 1]
        for cp in make_copies(in_refs, land_refs, s_sems, r_sems):
            cp.wait_send()
            cp.wait_recv()

    res = pl.pallas_call(
        body, name=name, out_shape=tuple(pltpu.HBM(a.shape, a.dtype) for a in ins + lands),
        in_specs=[HBM] * (ni + nl) + [SEM, SEM, ANY], out_specs=tuple([HBM] * (ni + nl)),
        input_output_aliases={i: i for i in range(ni + nl)},
        compiler_params=pltpu.CompilerParams(has_side_effects=EFFECT),
    )(*ins, *lands, send_sems, recv_sems, after)
    return list(res[:ni]), list(res[ni:])


def _gather_copies(n):
    def make(in_refs, land_refs, send_sems, recv_sems):
        x, y, c, chips = _coords()
        s = 2 * x + y
        return [pltpu.make_async_remote_copy(src_ref=in_refs[a], dst_ref=land_refs[a].at[s], send_sem=send_sems.at[k * n + a],
                                             recv_sem=recv_sems.at[k * n + a], device_id=(px, py, c), device_id_type=MESH)
                for k, (px, py) in enumerate(chips) for a in range(n)]
    return make


def _sibling_half_copies(n):
    def make(in_refs, land_refs, send_sems, recv_sems):
        x, y, c, _ = _coords()
        return [pltpu.make_async_remote_copy(src_ref=in_refs[a].at[:, 1 - c], dst_ref=land_refs[a], send_sem=send_sems.at[a],
                                             recv_sem=recv_sems.at[a], device_id=(x, y, 1 - c), device_id_type=MESH)
                for a in range(n)]
    return make


def _chip_part_copies(n):
    def make(in_refs, land_refs, send_sems, recv_sems):
        x, y, c, chips = _coords()
        return [pltpu.make_async_remote_copy(src_ref=in_refs[a].at[2 * px + py], dst_ref=land_refs[a].at[k],
                                             send_sem=send_sems.at[k * n + a], recv_sem=recv_sems.at[k * n + a],
                                             device_id=(px, py, c), device_id_type=MESH)
                for k, (px, py) in enumerate(chips) for a in range(n)]
    return make


def _sibling_whole_copies(n):
    def make(in_refs, land_refs, send_sems, recv_sems):
        x, y, c, _ = _coords()
        return [pltpu.make_async_remote_copy(src_ref=in_refs[a], dst_ref=land_refs[a], send_sem=send_sems.at[a],
                                             recv_sem=recv_sems.at[a], device_id=(x, y, 1 - c), device_id_type=MESH)
                for a in range(n)]
    return make


def _place_own(chip_arr, owns, lands, steps):
    n = len(owns)

    def body(s_ref, *refs):
        for a in range(n):
            refs[2 * n + a][...] = refs[a][...]

    tiles = [o.shape[0] // steps for o in owns]
    spec = pltpu.PrefetchScalarGridSpec(
        num_scalar_prefetch=1, grid=(steps,),
        in_specs=[pl.BlockSpec((t, o.shape[1]), lambda i, s_ref: (i, 0)) for t, o in zip(tiles, owns)] + [ANY] * n,
        out_specs=[pl.BlockSpec((None, t, o.shape[1]), lambda i, s_ref: (s_ref[0], i, 0)) for t, o in zip(tiles, owns)])
    return pl.pallas_call(body, name="place_own", grid_spec=spec, out_shape=[_sds(l.shape, l.dtype) for l in lands],
                          input_output_aliases={1 + n + a: a for a in range(n)},
                          compiler_params=_params(1))(chip_arr, *owns, *lands)


def _sibling_halves(g4s, small):
    n = len(g4s)

    def body(*refs):
        ins, small_ref = refs[:n], refs[n]
        outs, small_out = refs[n + 1:2 * n + 1], refs[2 * n + 1]
        send_sems, recv_sems = refs[2 * n + 2:]
        x, y, c, _ = _coords()
        sib = (x, y, 1 - c)

        def remote(a, half):
            src = small_ref if a == n else ins[a].at[:, half]
            dst = small_out if a == n else outs[a]
            return pltpu.make_async_remote_copy(src_ref=src, dst_ref=dst, send_sem=send_sems.at[a], recv_sem=recv_sems.at[a],
                                                device_id=sib, device_id_type=MESH)

        sends = [remote(a, 1 - c) for a in range(n + 1)]
        for cp in sends:
            cp.start()
        for a in range(n + 1):
            remote(a, c).wait_recv()
        for cp in sends:
            cp.wait_send()

    return pl.pallas_call(
        body, name="reduce_sibling", in_specs=[ANY] * (n + 1), out_specs=[ANY] * (n + 1),
        out_shape=[_sds((g.shape[0],) + g.shape[2:], f32) for g in g4s] + [_sds(small.shape, f32)],
        scratch_shapes=[pltpu.SemaphoreType.DMA((n + 1,)), pltpu.SemaphoreType.DMA((n + 1,))],
    )(*g4s, small)


def _exchange_chips(parts, small2):
    n = len(parts)

    def body(*refs):
        ins, small_ref = refs[:n], refs[n]
        outs, small_out = refs[n + 1:2 * n + 1], refs[2 * n + 1]
        send_sems, recv_sems, local_sem = refs[2 * n + 2:]
        x, y, c, chips = _coords()
        s = 2 * x + y
        local = pltpu.make_async_copy(small_ref.at[c], small_out.at[s], local_sem)
        local.start()

        def remote(k, a, dest_chip, small_slot, peer):
            if a == n:
                src, dst = small_ref.at[c], small_out.at[small_slot]
            else:
                src, dst = ins[a].at[dest_chip], outs[a].at[k]
            i = k * (n + 1) + a
            return pltpu.make_async_remote_copy(src_ref=src, dst_ref=dst, send_sem=send_sems.at[i], recv_sem=recv_sems.at[i],
                                                device_id=peer, device_id_type=MESH)

        sends = [remote(k, a, 2 * px + py, s, (px, py, c)) for k, (px, py) in enumerate(chips) for a in range(n + 1)]
        for cp in sends:
            cp.start()
        for k, (px, py) in enumerate(chips):
            for a in range(n + 1):
                remote(k, a, s, 2 * px + py, (px, py, c)).wait_recv()
        for cp in sends:
            cp.wait_send()
        local.wait()

    m = 3 * (n + 1)
    return pl.pallas_call(
        body, name="reduce_chips", in_specs=[ANY] * (n + 1), out_specs=[ANY] * (n + 1),
        out_shape=[_sds((3,) + p.shape[1:], p.dtype) for p in parts] + [_sds((NCHIP,) + small2.shape[1:], f32)],
        scratch_shapes=[pltpu.SemaphoreType.DMA((m,)), pltpu.SemaphoreType.DMA((m,)), pltpu.SemaphoreType.DMA],
    )(*parts, small2)


def _share_sibling(halves):
    n = len(halves)

    def body(*refs):
        ins, outs = refs[:n], refs[n:2 * n]
        send_sems, recv_sems = refs[2 * n:]
        x, y, c, _ = _coords()
        sib = (x, y, 1 - c)
        sends = [pltpu.make_async_remote_copy(src_ref=ins[a], dst_ref=outs[a], send_sem=send_sems.at[a], recv_sem=recv_sems.at[a],
                                              device_id=sib, device_id_type=MESH) for a in range(n)]
        for cp in sends:
            cp.start()
        for cp in sends:
            cp.wait()

    return pl.pallas_call(
        body, name="reduce_share", in_specs=[ANY] * n, out_specs=[ANY] * n,
        out_shape=[_sds(h.shape, f32) for h in halves],
        scratch_shapes=[pltpu.SemaphoreType.DMA((n,)), pltpu.SemaphoreType.DMA((n,))],
    )(*halves)


def _block_diag_pairs(w):
    w = w.reshape(NCH, 2, HD, HD)
    z = jnp.zeros((NCH, HD, HD), w.dtype)
    return jnp.concatenate([jnp.concatenate([w[:, 0], z], axis=2), jnp.concatenate([z, w[:, 1]], axis=2)], axis=1)


def _diag_blocks(m):
    return jnp.stack([m[:, :HD, :HD], m[:, HD:, HD:]], axis=1).reshape(NH, HD, HD)


def _pack(vs, rows):
    flat = jnp.concatenate([v.reshape(-1) for v in vs])
    return jnp.pad(flat, (0, rows * 128 - flat.shape[0])).reshape(rows, 128)


def _unpack(packed, shapes):
    flat = packed.reshape(-1)
    out, off = [], 0
    for shp in shapes:
        size = math.prod(shp)
        out.append(flat[off:off + size].reshape(shp))
        off += size
    return out


def _rows_for(sizes, multiple):
    rows = -(-sum(sizes) // 128)
    return -(-rows // multiple) * multiple


def kernel(x, norm_mix_g, w_in, b_gate, conv_w, conv_b, lru_lambda, lru_wa, lru_ba, lru_wx, lru_bx, attn_sink, w_out, norm_ffn_g, w_ffn_in, w_ffn_out, norm_final_g, loss_target, m_norm_mix_g, m_w_in, m_b_gate, m_conv_w, m_conv_b, m_lru_lambda, m_lru_wa, m_lru_ba, m_lru_wx, m_lru_bx, m_attn_sink, m_w_out, m_norm_ffn_g, m_w_ffn_in, m_w_ffn_out, m_norm_final_g, v_norm_mix_g, v_w_in, v_b_gate, v_conv_w, v_conv_b, v_lru_lambda, v_lru_wa, v_lru_ba, v_lru_wx, v_lru_bx, v_attn_sink, v_w_out, v_norm_ffn_g, v_w_ffn_in, v_w_ffn_out, v_norm_final_g):
    S = x.shape[1]
    xs = x[0]
    tgt = loss_target[0]
    cx, cy, cc = lax.axis_index("x"), lax.axis_index("y"), lax.axis_index("c")
    chip = 2 * cx + cy
    SW = D // NCHIP

    small_shard = _pack([conv_w[0], lru_lambda[0], lru_ba[0], lru_bx[0]], 32)
    halves_of = lambda a: a.reshape(2, a.shape[0] // 2, a.shape[1])
    w_in_g, small_g = _gather_chips([halves_of(w_in[0].astype(bf16)), halves_of(small_shard)])
    w_in_g = w_in_g.reshape(NCHIP, D, SHW)
    small_g = small_g.reshape(NCHIP, 32, 128)
    late = [w_ffn_in[0].astype(bf16), w_out[0].astype(bf16), w_ffn_out[0].astype(bf16)]
    late_send, late_recv, late_src, late_land, late_token = _split_start(
        "gather_late_start", 9, _gather_copies(3), late, [_sds((NCHIP,) + a.shape, bf16) for a in late])
    small_parts = [_unpack(small_g[s], [(4, SW), (2, SW), (2, SW), (2, SW)]) for s in range(NCHIP)]
    conv_w_f, lam_f, ba_f, bx_f = [jnp.concatenate([small_parts[s][p] for s in range(NCHIP)], axis=1) for p in range(4)]
    wbd = jnp.concatenate([_block_diag_pairs(lru_wa[0, 0]), _block_diag_pairs(lru_wx[0, 0]),
                           _block_diag_pairs(lru_wa[0, 1]), _block_diag_pairs(lru_wx[0, 1])], axis=2).astype(bf16)
    conv_b_f = conv_b
    sink = attn_sink

    xn, proj = _rms_matmul("rms_proj", xs, norm_mix_g + late_token[0:1, 0:1], w_in_g, 1024)
    y_a, lru_state = _lru_fwd(proj, conv_w_f, conv_b_f, lam_f, ba_f, bx_f, wbd)
    y_b = _attn_fwd(proj, sink)
    merged = _merge_fwd(proj, b_gate, y_a, y_b, 512)
    late_src, late_land = _split_wait("gather_late_wait", _gather_copies(3), late_send, late_recv, late_src, late_land, merged)
    chip_arr = chip.reshape(1).astype(jnp.int32)
    w_ffn_in_g, w_out_g, w_ffn_out_g = _place_own(chip_arr, late_src, late_land, 4)
    w_out_f = w_out_g.reshape(D, D)
    w_ffn_out_f = w_ffn_out_g.reshape(FF, D)
    x1 = _mm_residual("out_proj", merged, w_out_f, xs, 512)
    xn2, gu, act = _rms_matmul_swiglu("rms_ffn_in", x1, norm_ffn_g, w_ffn_in_g, 1024)
    x2 = _mm_residual("ffn_out", act, w_ffn_out_f, x1, 512)
    dx2, loss_row, dg3 = _final_loss_bwd(x2, norm_final_g.reshape(1, D), tgt, 256)

    tm = min(1024, S)
    tk = min(2048, S)
    gw_ffn_out = _mm_tn("dw_ffn_out", act, pl.BlockSpec((tk, SHW), lambda i, k: (k, i)),
                        dx2, pl.BlockSpec((tk, D), lambda i, k: (k, 0)),
                        _sds((FF, D), f32), pl.BlockSpec((SHW, D), lambda i, k: (i, 0)), (2, S // tk), (SHW, D))
    dgu = _swiglu_bwd(dx2, w_ffn_out_f, gu, 256)
    dxn2 = _mm_nt_groups("dxn2", dgu, pl.BlockSpec((None, tm, SHW), lambda i, g: (g // 2, i, g % 2)), w_ffn_in_g, S, tm)
    gw_ffn_in = _mm_tn("dw_ffn_in", xn2, pl.BlockSpec((tk, D), lambda g, k: (k, 0)),
                       dgu, pl.BlockSpec((None, tk, SHW), lambda g, k: (g // 2, k, g % 2)),
                       _sds((NCHIP, D, SHW), f32), pl.BlockSpec((None, D, SHW), lambda g, k: (g, 0, 0)),
                       (NCHIP, S // tk), (D, SHW))
    c_arr = cc.reshape(1).astype(jnp.int32)
    early_names, early_tiles = ["w_ffn_in", "w_ffn_out"], [256, 352]
    early = [gw_ffn_in.reshape(NCHIP, 2, D // 2, SHW), gw_ffn_out.reshape(NCHIP, 2, FF // NCHIP // 2, D)]
    ea_send, ea_recv, ea_src, ea_land, ea_token = _split_start(
        "reduce_early_sibling_start", 2, _sibling_half_copies(2), early,
        [_sds((NCHIP,) + g.shape[2:], f32) for g in early])
    dx1, dg2 = _rms_bwd("rms_ffn_bwd", x1, norm_ffn_g + ea_token[0:1, 0:1], dxn2, dx2, 256)

    dmerged = _mm_nt_resident("d_merged", dx1, w_out_f, 512)
    gw_out = _mm_tn("dw_out", merged, pl.BlockSpec((tk, D), lambda i, k: (k, 0)),
                    dx1, pl.BlockSpec((tk, D), lambda i, k: (k, 0)),
                    _sds((D, D), f32), pl.BlockSpec((D, D), lambda i, k: (0, 0)), (1, S // tk), (D, D))
    dz0, dz1, dy_a, dy_b, db0, db1 = _merge_bwd(proj, b_gate, y_a, y_b, dmerged, 512)
    ea_src, ea_land = _split_wait("reduce_early_sibling_wait", _sibling_half_copies(2), ea_send, ea_recv, ea_src, ea_land, dy_b)
    early_pairs = [_pair_sum("pair_sum_" + nm, c_arr, g4, r, th)
                   for nm, g4, r, th in zip(early_names, ea_src, ea_land, early_tiles)]
    eb_send, eb_recv, eb_src, eb_land, eb_token = _split_start(
        "reduce_early_chips_start", 6, _chip_part_copies(2), [p[1] for p in early_pairs],
        [_sds((3,) + p[1].shape[1:], bf16) for p in early_pairs])
    dq, dk, dv, dsink = _attn_bwd(proj, sink + eb_token[0:1, 0:1], y_b, dy_b)
    _, eb_land = _split_wait("reduce_early_chips_wait", _chip_part_copies(2), eb_send, eb_recv, eb_src, eb_land, dq)
    early_halves = [_chip_sum("chip_sum_" + nm, chip_arr, p[0], r3, th)
                    for nm, p, r3, th in zip(early_names, early_pairs, eb_land, early_tiles)]
    ec_send, ec_recv, ec_src, ec_land, ec_token = _split_start(
        "reduce_early_share_start", 2, _sibling_whole_copies(2), early_halves, [_sds(h.shape, f32) for h in early_halves])
    du, dgl, dcw, dcb, dlam, dba, dbx, dwbd = _lru_bwd(proj, dy_a, lru_state, conv_w_f, conv_b_f + ec_token[0:1, 0:1], lam_f, ba_f, bx_f, wbd)
    early_halves, early_other = _split_wait("reduce_early_share_wait", _sibling_whole_copies(2), ec_send, ec_recv, ec_src, ec_land, du)
    dproj = jnp.concatenate([du, dgl, dq, dk.astype(bf16), dv.astype(bf16), dz0, dz1], axis=1)
    gw_in = _mm_tn("dw_in", xn, pl.BlockSpec((tk, D), lambda g, k: (k, 0)),
                   dproj, pl.BlockSpec((tk, SHW), lambda g, k: (k, g)),
                   _sds((NCHIP, D, SHW), f32), pl.BlockSpec((None, D, SHW), lambda g, k: (g, 0, 0)),
                   (NCHIP, S // tk), (D, SHW))
    wa_send, wa_recv, wa_src, wa_land, wa_token = _split_start(
        "reduce_w_in_sibling_start", 1, _sibling_half_copies(1), [gw_in.reshape(NCHIP, 2, D // 2, SHW)],
        [_sds((NCHIP, D // 2, SHW), f32)])
    dxn = _mm_nt_groups("dxn", dproj, pl.BlockSpec((tm, SHW), lambda i, g: (i, g)), w_in_g, S, tm)
    wa_src, wa_land = _split_wait("reduce_w_in_sibling_wait", _sibling_half_copies(1), wa_send, wa_recv, wa_src, wa_land, dxn)
    w_in_pair = _pair_sum("pair_sum_w_in", c_arr, wa_src[0], wa_land[0], 256)
    wb_send, wb_recv, wb_src, wb_land, wb_token = _split_start(
        "reduce_w_in_chips_start", 3, _chip_part_copies(1), [w_in_pair[1]], [_sds((3, D // 2, SHW), bf16)])
    grad_x, dg1 = _rms_bwd("rms_mix_bwd", xs, norm_mix_g + wb_token[0:1, 0:1], dxn, dx1, 256)
    _, wb_land = _split_wait("reduce_w_in_chips_wait", _chip_part_copies(1), wb_send, wb_recv, wb_src, wb_land, dg1)
    w_in_half = _chip_sum("chip_sum_w_in", chip_arr, w_in_pair[0], wb_land[0], 256)

    d_wa = jnp.stack([_diag_blocks(dwbd[:, :, 0:CW]), _diag_blocks(dwbd[:, :, 2 * CW:3 * CW])])
    d_wx = jnp.stack([_diag_blocks(dwbd[:, :, CW:2 * CW]), _diag_blocks(dwbd[:, :, 3 * CW:4 * CW])])
    small_full = [dg1, jnp.concatenate([db0, db1], axis=1), dcw, dcb, dlam, d_wa, dba, d_wx, dbx, dsink[:, 0], dg2, dg3,
                  loss_row[0, 0:1]]
    full_shapes = [(1, D), (1, 2 * D), (4, D), (1, D), (2, D), (2, NH, HD, HD), (2, D), (2, NH, HD, HD), (2, D), (NH,),
                   (1, D), (1, D), (1,)]
    rows_full = _rows_for([math.prod(s) for s in full_shapes], 16)
    small_vec = _pack(small_full, rows_full)

    late_names, late_tiles = ["w_in", "w_out"], [256, 128]
    big = [gw_out.reshape(NCHIP, 2, D // NCHIP // 2, D)]
    *recv_a, small_sib = _sibling_halves(big, small_vec)
    w_out_pair = _pair_sum("pair_sum_w_out", c_arr, big[0], recv_a[0], 128)
    small_chip = _add2("pair_sum_small", small_vec, small_sib).reshape(2, rows_full // 2, 128)
    *recv_b, small_all = _exchange_chips([w_out_pair[1]], small_chip)
    w_out_half = _chip_sum("chip_sum_w_out", chip_arr, w_out_pair[0], recv_b[0], 128)
    halves = [w_in_half, w_out_half, _sum4("chip_sum_small", small_all, rows_full // 2)]
    *recv_c, small_other = _share_sibling(halves)
    small_lo = jnp.where(cc == 0, halves[2], small_other)
    small_hi = jnp.where(cc == 0, small_other, halves[2])
    g_full = _unpack(jnp.concatenate([small_lo, small_hi], axis=0), full_shapes)

    out_big = {}
    for nm, w, g_own, g_recv, m, v, th in zip(late_names + early_names, [w_in, w_out, w_ffn_in, w_ffn_out],
                                              halves[:2] + early_halves, recv_c + early_other,
                                              [m_w_in, m_w_out, m_w_ffn_in, m_w_ffn_out],
                                              [v_w_in, v_w_out, v_w_ffn_in, v_w_ffn_out], late_tiles + early_tiles):
        g_, d_, m_, v_ = _adamw_halves("adamw_" + nm, c_arr, w[0], g_own, g_recv, m[0], v[0], th)
        out_big[nm] = (g_[None], d_[None], m_[None], v_[None])

    small_names = ["norm_mix_g", "b_gate", "conv_w", "conv_b", "lru_lambda", "lru_wa", "lru_ba", "lru_wx", "lru_bx", "attn_sink",
                   "norm_ffn_g", "norm_final_g"]
    sharded = {"conv_w", "lru_lambda", "lru_ba", "lru_bx"}
    small_w = [norm_mix_g, b_gate, conv_w, conv_b, lru_lambda, lru_wa, lru_ba, lru_wx, lru_bx, attn_sink, norm_ffn_g, norm_final_g]
    small_m = [m_norm_mix_g, m_b_gate, m_conv_w, m_conv_b, m_lru_lambda, m_lru_wa, m_lru_ba, m_lru_wx, m_lru_bx, m_attn_sink,
               m_norm_ffn_g, m_norm_final_g]
    small_v = [v_norm_mix_g, v_b_gate, v_conv_w, v_conv_b, v_lru_lambda, v_lru_wa, v_lru_ba, v_lru_wx, v_lru_bx, v_attn_sink,
               v_norm_ffn_g, v_norm_final_g]
    g_local = []
    for nm, g, w in zip(small_names, g_full, small_w):
        if nm in sharded:
            g = lax.dynamic_slice_in_dim(g, chip * SW, SW, axis=1)
        g_local.append(g.reshape(w.shape))
    local_shapes = [w.shape for w in small_w]
    rows_local = _rows_for([math.prod(s) for s in local_shapes], 8)
    d_s, m_s, v_s = _adamw("adamw_small", _pack(small_w, rows_local), _pack(g_local, rows_local),
                           _pack(small_m, rows_local), _pack(small_v, rows_local), rows_local)
    d_l, m_l, v_l = _unpack(d_s, local_shapes), _unpack(m_s, local_shapes), _unpack(v_s, local_shapes)
    res = {nm: (g_local[i], d_l[i], m_l[i], v_l[i]) for i, nm in enumerate(small_names)}
    res.update(out_big)

    order = ["norm_mix_g", "w_in", "b_gate", "conv_w", "conv_b", "lru_lambda", "lru_wa", "lru_ba", "lru_wx", "lru_bx", "attn_sink",
             "w_out", "norm_ffn_g", "w_ffn_in", "w_ffn_out", "norm_final_g"]
    outs = [g_full[-1][0], grad_x[None]]
    for k in range(4):
        outs += [res[nm][k] for nm in order]
    return tuple(outs)
```

```python
import functools
import math

import jax
import jax.numpy as jnp
from jax import lax
from jax.experimental import pallas as pl
from jax.experimental.pallas import tpu as pltpu

f32 = jnp.float32
bf16 = jnp.bfloat16

D = 1024
NH = 16
HD = 64
FF = 2816
INW = 5632
NCHIP = 4
SHW = INW // NCHIP
CW = 128
NCH = D // CW
BLK = 128
EPS = 1e-6
NEG_INF = -1e30
RGLRU_C = 8.0
ADAM_LR, ADAM_B1, ADAM_B2, ADAM_EPS, ADAM_WD, ADAM_STEP = 0.001, 0.9, 0.999, 1e-08, 0.01, 10
VMEM_LIMIT = 58 * 1024 * 1024
MESH = pl.DeviceIdType.MESH
ANY = pl.BlockSpec(memory_space=pl.ANY)

COL_U, COL_G, COL_Q, COL_K, COL_V, COL_Z0, COL_Z1 = 0, 4, 8, 12, 13, 14, 18


def _params(n_axes, vmem=False):
    return pltpu.CompilerParams(dimension_semantics=("arbitrary",) * n_axes,
                                vmem_limit_bytes=VMEM_LIMIT if vmem else None)


def _sds(shape, dtype):
    return jax.ShapeDtypeStruct(tuple(shape), dtype)


_DIMS = {"nn": (((1,), (0,)), ((), ())), "nt": (((1,), (1,)), ((), ())), "tn": (((0,), (0,)), ((), ()))}


def _mm(name, mode, a, a_spec, b, b_spec, out_shape, out_spec, grid, nk, acc_shape, add=None, add_spec=None):
    has_add = add is not None

    def body(*refs):
        a_ref, b_ref = refs[0], refs[1]
        add_ref = refs[2] if has_add else None
        o_ref = refs[2 + has_add]
        part = lax.dot_general(a_ref[...].astype(bf16), b_ref[...].astype(bf16), _DIMS[mode],
                               preferred_element_type=f32)
        if nk == 1:
            if has_add:
                part = add_ref[...] + part
            o_ref[...] = part.astype(o_ref.dtype)
            return
        acc_ref = refs[3 + has_add]
        k = pl.program_id(len(grid) - 1)

        @pl.when(k == 0)
        def _():
            acc_ref[...] = part

        @pl.when(k > 0)
        def _():
            acc_ref[...] += part

        @pl.when(k == nk - 1)
        def _():
            res = acc_ref[...]
            if has_add:
                res = add_ref[...] + res
            o_ref[...] = res.astype(o_ref.dtype)

    ins = [a, b] + ([add] if has_add else [])
    in_specs = [a_spec, b_spec] + ([add_spec] if has_add else [])
    scratch = [pltpu.VMEM(acc_shape, f32)] if nk > 1 else []
    return pl.pallas_call(body, name=name, grid=grid, in_specs=in_specs, out_specs=out_spec, out_shape=out_shape,
                          scratch_shapes=scratch, compiler_params=_params(len(grid), True))(*ins)


def _rms_matmul(name, x, g, w3, tm):
    S, K = x.shape
    G, _, Nw = w3.shape
    tm = min(tm, S)

    def body(x_ref, g_ref, w_ref, xn_ref, o_ref, xs_ref):
        @pl.when(pl.program_id(1) == 0)
        def _():
            xf = x_ref[...]
            r = lax.rsqrt(jnp.mean(xf * xf, axis=-1, keepdims=True) + EPS)
            xn = ((xf * r) * g_ref[...]).astype(bf16)
            xs_ref[...] = xn
            xn_ref[...] = xn

        o_ref[...] = jnp.dot(xs_ref[...], w_ref[...], preferred_element_type=f32).astype(bf16)

    return pl.pallas_call(
        body, name=name, grid=(S // tm, G),
        in_specs=[pl.BlockSpec((tm, K), lambda i, j: (i, 0)), pl.BlockSpec((1, K), lambda i, j: (0, 0)),
                  pl.BlockSpec((None, K, Nw), lambda i, j: (j, 0, 0))],
        out_specs=[pl.BlockSpec((tm, K), lambda i, j: (i, 0)), pl.BlockSpec((tm, Nw), lambda i, j: (i, j))],
        out_shape=[_sds((S, K), bf16), _sds((S, G * Nw), bf16)],
        scratch_shapes=[pltpu.VMEM((tm, K), bf16)], compiler_params=_params(2, True))(x, g, w3)


def _rms_matmul_swiglu(name, x, g, w3, tm):
    S, K = x.shape
    G, _, Nw = w3.shape
    tm = min(tm, S)
    half = G // 2

    def body(x_ref, g_ref, wg_ref, wu_ref, xn_ref, gu_ref, act_ref, xs_ref):
        @pl.when(pl.program_id(1) == 0)
        def _():
            xf = x_ref[...]
            r = lax.rsqrt(jnp.mean(xf * xf, axis=-1, keepdims=True) + EPS)
            xn = ((xf * r) * g_ref[...]).astype(bf16)
            xs_ref[...] = xn
            xn_ref[...] = xn

        xn = xs_ref[...]
        gate = jnp.dot(xn, wg_ref[...], preferred_element_type=f32)
        up = jnp.dot(xn, wu_ref[...], preferred_element_type=f32)
        gu_ref[0] = gate.astype(bf16)
        gu_ref[1] = up.astype(bf16)
        act_ref[...] = ((gate * _sigmoid(gate)) * up).astype(bf16)

    return pl.pallas_call(
        body, name=name, grid=(S // tm, half),
        in_specs=[pl.BlockSpec((tm, K), lambda i, j: (i, 0)), pl.BlockSpec((1, K), lambda i, j: (0, 0)),
                  pl.BlockSpec((None, K, Nw), lambda i, j: (j, 0, 0)),
                  pl.BlockSpec((None, K, Nw), lambda i, j: (half + j, 0, 0))],
        out_specs=[pl.BlockSpec((tm, K), lambda i, j: (i, 0)), pl.BlockSpec((2, tm, Nw), lambda i, j: (0, i, j)),
                   pl.BlockSpec((tm, Nw), lambda i, j: (i, j))],
        out_shape=[_sds((S, K), bf16), _sds((2, S, half * Nw), bf16), _sds((S, half * Nw), bf16)],
        scratch_shapes=[pltpu.VMEM((tm, K), bf16)], compiler_params=_params(2, True))(x, g, w3, w3)


def _mm_residual(name, a, w, res, tm):
    S, K = a.shape
    N = w.shape[1]
    tm = min(tm, S)
    return _mm(name, "nn", a, pl.BlockSpec((tm, K), lambda i: (i, 0)), w, pl.BlockSpec((K, N), lambda i: (0, 0)),
               _sds((S, N), f32), pl.BlockSpec((tm, N), lambda i: (i, 0)), (S // tm,), 1, None,
               add=res, add_spec=pl.BlockSpec((tm, N), lambda i: (i, 0)))


def _mm_nt_resident(name, a, w, tm):
    S, K = a.shape
    N = w.shape[0]
    tm = min(tm, S)
    return _mm(name, "nt", a, pl.BlockSpec((tm, K), lambda i: (i, 0)), w, pl.BlockSpec((N, K), lambda i: (0, 0)),
               _sds((S, N), f32), pl.BlockSpec((tm, N), lambda i: (i, 0)), (S // tm,), 1, None)


def _mm_nt_groups(name, a, a_spec, w3, S, tm):
    G, Dout, Kw = w3.shape
    return _mm(name, "nt", a, a_spec, w3, pl.BlockSpec((None, Dout, Kw), lambda i, g: (g, 0, 0)),
               _sds((S, Dout), f32), pl.BlockSpec((tm, Dout), lambda i, g: (i, 0)), (S // tm, G), G, (tm, Dout))


def _mm_tn(name, a, a_spec, b, b_spec, out_shape, out_spec, grid, acc_shape):
    return _mm(name, "tn", a, a_spec, b, b_spec, out_shape, out_spec, grid, grid[-1], acc_shape)


def _sigmoid(x):
    return 0.5 * jnp.tanh(0.5 * x) + 0.5


_GELU_C = math.sqrt(2.0 / math.pi)


def _gelu_and_grad(x):
    v = _GELU_C * (x + 0.044715 * (x * x * x))
    t = jnp.tanh(v)
    gl = 0.5 * x * (1.0 + t)
    dgl = 0.5 * (1.0 + t) + 0.5 * x * (1.0 - t * t) * (_GELU_C * (1.0 + 3.0 * 0.044715 * (x * x)))
    return gl, dgl


def _one_minus_exp2x(x, ex):
    y = 2.0 * x
    series = y * (1.0 + y * (0.5 + y * (1.0 / 6.0 + y * (1.0 / 24.0))))
    return jnp.where(y > -1.0 / 64.0, -series, 1.0 - ex * ex)


def _merge_fwd(proj, b_gate, y_a, y_b, tm):
    S = proj.shape[0]
    tm = min(tm, S)

    def body(z0_ref, z1_ref, b0_ref, b1_ref, ya_ref, yb_ref, o_ref):
        g0 = _sigmoid(z0_ref[...].astype(f32) + b0_ref[...])
        g1 = _sigmoid(z1_ref[...].astype(f32) + b1_ref[...])
        o_ref[...] = (g0 * ya_ref[...].astype(f32) + g1 * yb_ref[...].astype(f32)).astype(bf16)

    blk = lambda off: pl.BlockSpec((tm, 256), lambda j, i: (i, off + j))
    vec = lambda off: pl.BlockSpec((1, 256), lambda j, i: (0, off + j))
    return pl.pallas_call(body, name="merge_fwd", grid=(4, S // tm),
                          in_specs=[blk(COL_Z0), blk(COL_Z1), vec(0), vec(4), blk(0), blk(0)],
                          out_specs=blk(0), out_shape=_sds((S, D), bf16),
                          compiler_params=_params(2))(proj, proj, b_gate, b_gate, y_a, y_b)


def _merge_bwd(proj, b_gate, y_a, y_b, dm, tm):
    S = proj.shape[0]
    tm = min(tm, S)

    def body(z0_ref, z1_ref, b0_ref, b1_ref, ya_ref, yb_ref, dm_ref, dz0_ref, dz1_ref, dya_ref, dyb_ref, db0_ref, db1_ref):
        g0 = _sigmoid(z0_ref[...].astype(f32) + b0_ref[...])
        g1 = _sigmoid(z1_ref[...].astype(f32) + b1_ref[...])
        d = dm_ref[...]
        dz0 = (d * ya_ref[...].astype(f32)) * (g0 * (1.0 - g0))
        dz1 = (d * yb_ref[...].astype(f32)) * (g1 * (1.0 - g1))
        dz0_ref[...] = dz0.astype(bf16)
        dz1_ref[...] = dz1.astype(bf16)
        dya_ref[...] = (d * g0).astype(bf16)
        dyb_ref[...] = (d * g1).astype(bf16)

        @pl.when(pl.program_id(1) == 0)
        def _():
            db0_ref[...] = jnp.zeros_like(db0_ref)
            db1_ref[...] = jnp.zeros_like(db1_ref)

        db0_ref[...] += jnp.sum(dz0, axis=0, keepdims=True)
        db1_ref[...] += jnp.sum(dz1, axis=0, keepdims=True)

    blk = lambda off: pl.BlockSpec((tm, 256), lambda j, i: (i, off + j))
    vec = lambda off: pl.BlockSpec((1, 256), lambda j, i: (0, off + j))
    return pl.pallas_call(
        body, name="merge_bwd", grid=(4, S // tm),
        in_specs=[blk(COL_Z0), blk(COL_Z1), vec(0), vec(4), blk(0), blk(0), blk(0)],
        out_specs=[blk(0), blk(0), blk(0), blk(0), vec(0), vec(0)],
        out_shape=[_sds((S, D), bf16), _sds((S, D), bf16), _sds((S, D), bf16), _sds((S, D), bf16),
                   _sds((1, D), f32), _sds((1, D), f32)],
        compiler_params=_params(2))(proj, proj, b_gate, b_gate, y_a, y_b, dm)


def _swiglu_bwd(dx, w, gu, tm):
    S, K = dx.shape
    tm = min(tm, S)

    def body(dx_ref, w_ref, gu_ref, o_ref):
        d = lax.dot_general(dx_ref[...].astype(bf16), w_ref[...], _DIMS["nt"], preferred_element_type=f32)
        g = gu_ref[0].astype(f32)
        u = gu_ref[1].astype(f32)
        s = _sigmoid(g)
        o_ref[0] = ((d * u) * (s * (1.0 + g * (1.0 - s)))).astype(bf16)
        o_ref[1] = (d * (g * s)).astype(bf16)

    stacked = pl.BlockSpec((2, tm, FF), lambda i: (0, i, 0))
    return pl.pallas_call(body, name="swiglu_bwd", grid=(S // tm,),
                          in_specs=[pl.BlockSpec((tm, K), lambda i: (i, 0)), pl.BlockSpec((FF, K), lambda i: (0, 0)), stacked],
                          out_specs=stacked, out_shape=_sds((2, S, FF), bf16),
                          compiler_params=_params(1, True))(dx, w, gu)


def _final_loss_bwd(x2, g3, tgt, tm):
    S = x2.shape[0]
    tm = min(tm, S)

    def body(x_ref, g_ref, t_ref, dx_ref, loss_ref, dg_ref):
        @pl.when(pl.program_id(0) == 0)
        def _():
            loss_ref[...] = jnp.zeros_like(loss_ref)
            dg_ref[...] = jnp.zeros_like(dg_ref)

        x = x_ref[...]
        g = g_ref[...]
        r = lax.rsqrt(jnp.mean(x * x, axis=-1, keepdims=True) + EPS)
        xh = x * r
        err = xh * g - t_ref[...]
        row = jnp.mean(err * err, axis=-1, keepdims=True)
        loss_ref[...] += 0.5 * jnp.sum(row, axis=0, keepdims=True)
        dy = err * (1.0 / D)
        dg_ref[...] += jnp.sum(dy * xh, axis=0, keepdims=True)
        dxh = dy * g
        dx_ref[...] = r * (dxh - xh * jnp.mean(dxh * xh, axis=-1, keepdims=True))

    row_blk = pl.BlockSpec((tm, D), lambda i: (i, 0))
    vec = pl.BlockSpec((1, D), lambda i: (0, 0))
    return pl.pallas_call(body, name="final_loss_bwd", grid=(S // tm,), in_specs=[row_blk, vec, row_blk],
                          out_specs=[row_blk, pl.BlockSpec((1, 128), lambda i: (0, 0)), vec],
                          out_shape=[_sds((S, D), f32), _sds((1, 128), f32), _sds((1, D), f32)],
                          compiler_params=_params(1))(x2, g3, tgt)


def _rms_bwd(name, x, g, dxn, dres, tm):
    S = x.shape[0]
    tm = min(tm, S)

    def body(x_ref, g_ref, d_ref, r_ref, dx_ref, dg_ref):
        @pl.when(pl.program_id(0) == 0)
        def _():
            dg_ref[...] = jnp.zeros_like(dg_ref)

        x = x_ref[...]
        d = d_ref[...]
        r = lax.rsqrt(jnp.mean(x * x, axis=-1, keepdims=True) + EPS)
        xh = x * r
        dg_ref[...] += jnp.sum(d * xh, axis=0, keepdims=True)
        dxh = d * g_ref[...]
        dx_ref[...] = r_ref[...] + r * (dxh - xh * jnp.mean(dxh * xh, axis=-1, keepdims=True))

    row_blk = pl.BlockSpec((tm, D), lambda i: (i, 0))
    vec = pl.BlockSpec((1, D), lambda i: (0, 0))
    return pl.pallas_call(body, name=name, grid=(S // tm,), in_specs=[row_blk, vec, row_blk, row_blk],
                          out_specs=[row_blk, vec], out_shape=[_sds((S, D), f32), _sds((1, D), f32)],
                          compiler_params=_params(1))(x, g, dxn, dres)


LRU_TT = 256
SCAN_UNROLL = 4


HALO = 16


def _halo(ref, i, S):
    nt = S // LRU_TT
    t0 = pl.multiple_of(i * LRU_TT, LRU_TT)
    p0 = pl.multiple_of(jnp.maximum(t0 - HALO, 0), HALO)
    n0 = pl.multiple_of(jnp.minimum(t0 + LRU_TT, S - HALO), HALO)
    prev = jnp.where(i > 0, ref[pl.ds(p0, HALO), :].astype(f32), 0.0)
    nxt = jnp.where(i < nt - 1, ref[pl.ds(n0, HALO), :].astype(f32), 0.0)
    return jnp.concatenate([prev, ref[pl.ds(t0, LRU_TT), :].astype(f32), nxt], axis=0)


def _shift(ext, k):
    n = LRU_TT + 2 * HALO
    return pltpu.roll(ext, (-k) % n, 0)[HALO:HALO + LRU_TT]


def _lru_gates(uc, wbd, ba, bx):
    pre = jnp.dot(uc.astype(bf16), wbd, preferred_element_type=f32)
    r_f = _sigmoid(pre[:, 0:CW] + ba[0:1])
    i_f = _sigmoid(pre[:, CW:2 * CW] + bx[0:1])
    r_b = _sigmoid(pre[:, 2 * CW:3 * CW] + ba[1:2])
    i_b = _sigmoid(pre[:, 3 * CW:4 * CW] + bx[1:2])
    return r_f, i_f, r_b, i_b


def _lru_coeffs(r, sp):
    log_a = (-RGLRU_C * r) * sp
    a = jnp.exp(log_a)
    beta = jnp.sqrt(jnp.maximum(_one_minus_exp2x(log_a, a), 0.0))
    return a, beta


def _lru_coeffs_inv(r, sp):
    log_a = (-RGLRU_C * r) * sp
    a = jnp.exp(log_a)
    om = jnp.maximum(_one_minus_exp2x(log_a, a), 0.0)
    return a, jnp.sqrt(om), lax.rsqrt(om)


def _conv_tile(u_ref, i, S, cw, cb):
    ext = _halo(u_ref, i, S)
    um2, um1, u0, up1 = _shift(ext, -2), _shift(ext, -1), ext[HALO:HALO + LRU_TT], _shift(ext, 1)
    uc = um2 * cw[0:1] + um1 * cw[1:2] + u0 * cw[2:3] + up1 * cw[3:4] + cb
    return uc, (um2, um1, u0, up1)


def _scan_pair(S, fwd_a, fwd_b, fwd_out, rev_a, rev_b, rev_out):
    ng = S // 8
    idx = lax.broadcasted_iota(jnp.int32, (8, CW), 0)

    def local(a, b, rev):
        for sh in (1, 2, 4):
            if rev:
                keep = idx < 8 - sh
                amt = 8 - sh
            else:
                keep = idx >= sh
                amt = sh
            a_s = jnp.where(keep, pltpu.roll(a, amt, 0), 1.0)
            b_s = jnp.where(keep, pltpu.roll(b, amt, 0), 0.0)
            b = a * b_s + b
            a = a * a_s
        return a, b

    def step(it, carry):
        cf, cr = carry
        fwd_rows = [pl.multiple_of((it * SCAN_UNROLL + j) * 8, 8) for j in range(SCAN_UNROLL)]
        rev_rows = [pl.multiple_of((ng - 1 - (it * SCAN_UNROLL + j)) * 8, 8) for j in range(SCAN_UNROLL)]
        fwd_loc = [local(fwd_a(r), fwd_b(r), False) for r in fwd_rows]
        rev_loc = [local(rev_a(r), rev_b(r), True) for r in rev_rows]
        for j in range(SCAN_UNROLL):
            a, b = fwd_loc[j]
            h = a * cf + b
            fwd_out[pl.ds(fwd_rows[j], 8), :] = h
            cf = jnp.broadcast_to(h[7:8, :], (8, CW))
            a, b = rev_loc[j]
            h = a * cr + b
            rev_out[pl.ds(rev_rows[j], 8), :] = h
            cr = jnp.broadcast_to(h[0:1, :], (8, CW))
        return cf, cr

    zero = jnp.zeros((8, CW), f32)
    lax.fori_loop(0, ng // SCAN_UNROLL, step, (zero, zero))


def _lru_specs(S):
    seq = lambda off: pl.BlockSpec((S, CW), lambda j: (0, off + j))
    par = lambda rows: pl.BlockSpec((rows, CW), lambda j: (0, j))
    return seq, par


def _lru_fwd(proj, conv_w, conv_b, lam, ba, bx, wbd):
    S = proj.shape[0]
    nt = S // LRU_TT

    def body(u_ref, g_ref, cw_ref, cb_ref, lam_ref, ba_ref, bx_ref, wbd_ref, y_ref, state_ref, af_ref, bf_ref, ab_ref, bb_ref,
             sems):
        cw, cb, ba_v, bx_v, wbd_v = cw_ref[...], cb_ref[...], ba_ref[...], bx_ref[...], wbd_ref[...]
        sp = jax.nn.softplus(-lam_ref[...])
        cols = pl.ds(pl.multiple_of(pl.program_id(0) * CW, CW), CW)
        save = [pltpu.make_async_copy(ref, state_ref.at[k, :, cols], sems.at[k])
                for k, ref in enumerate((af_ref, bf_ref, ab_ref, bb_ref))]

        def phase1(i, c):
            uc, _ = _conv_tile(u_ref, i, S, cw, cb)
            r_f, i_f, r_b, i_b = _lru_gates(uc, wbd_v, ba_v, bx_v)
            rows = pl.ds(pl.multiple_of(i * LRU_TT, LRU_TT), LRU_TT)
            a, beta = _lru_coeffs(r_f, sp[0:1])
            af_ref[rows, :] = a
            bf_ref[rows, :] = beta * (i_f * uc)
            a, beta = _lru_coeffs(r_b, sp[1:2])
            ab_ref[rows, :] = a
            bb_ref[rows, :] = beta * (i_b * uc)
            return c

        lax.fori_loop(0, nt, phase1, 0)
        row8 = lambda ref: (lambda r0: ref[pl.ds(r0, 8), :])
        _scan_pair(S, row8(af_ref), row8(bf_ref), bf_ref, row8(ab_ref), row8(bb_ref), bb_ref)
        for cp in save:
            cp.start()

        def phase3(i, c):
            rows = pl.ds(pl.multiple_of(i * LRU_TT, LRU_TT), LRU_TT)
            y = (bf_ref[rows, :] + bb_ref[rows, :]) * jax.nn.gelu(g_ref[rows, :].astype(f32))
            y_ref[rows, :] = y.astype(y_ref.dtype)
            return c

        lax.fori_loop(0, nt, phase3, 0)
        for cp in save:
            cp.wait()

    seq, par = _lru_specs(S)
    return pl.pallas_call(
        body, name="lru_fwd", grid=(NCH,),
        in_specs=[seq(0), seq(NCH), par(4), par(1), par(2), par(2), par(2),
                  pl.BlockSpec((None, CW, 4 * CW), lambda j: (j, 0, 0))],
        out_specs=[seq(0), ANY], out_shape=[_sds((S, D), bf16), _sds((4, S, D), f32)],
        scratch_shapes=[pltpu.VMEM((S, CW), f32)] * 4 + [pltpu.SemaphoreType.DMA((4,))], compiler_params=_params(1, True),
    )(proj, proj, conv_w, conv_b, lam, ba, bx, wbd)


def _lru_bwd(proj, dy, state, conv_w, conv_b, lam, ba, bx, wbd):
    S = proj.shape[0]
    nt = S // LRU_TT

    def body(u_ref, g_ref, dy_ref, state_ref, cw_ref, cb_ref, lam_ref, ba_ref, bx_ref, wbd_ref,
             du_ref, dg_ref, dcw_ref, dcb_ref, dlam_ref, dba_ref, dbx_ref, dwbd_ref,
             af_ref, bf_ref, ab_ref, bb_ref, dh_ref, sems):
        cw, cb, ba_v, bx_v, wbd_v = cw_ref[...], cb_ref[...], ba_ref[...], bx_ref[...], wbd_ref[...]
        lam_v = lam_ref[...]
        sp = jax.nn.softplus(-lam_v)
        cols = pl.ds(pl.multiple_of(pl.program_id(0) * CW, CW), CW)
        load = [pltpu.make_async_copy(state_ref.at[k, :, cols], ref, sems.at[k])
                for k, ref in enumerate((af_ref, bf_ref, ab_ref, bb_ref))]
        for cp in load:
            cp.start()
        for cp in load:
            cp.wait()
        row8 = lambda ref: (lambda r0: ref[pl.ds(r0, 8), :])

        def phase0(i, c):
            rows = pl.ds(pl.multiple_of(i * LRU_TT, LRU_TT), LRU_TT)
            gl, dgl = _gelu_and_grad(g_ref[rows, :].astype(f32))
            dyt = dy_ref[rows, :].astype(f32)
            dh_ref[rows, :] = dyt * gl
            dg_ref[rows, :] = ((dyt * (bf_ref[rows, :] + bb_ref[rows, :])) * dgl).astype(dg_ref.dtype)
            return c

        lax.fori_loop(0, nt, phase0, 0)

        def scaled_dh(a_ref):
            def f(r0):
                return a_ref[pl.ds(r0, 8), :] * dh_ref[pl.ds(r0, 8), :]
            return f

        _scan_pair(S, row8(ab_ref), scaled_dh(ab_ref), ab_ref, row8(af_ref), scaled_dh(af_ref), af_ref)

        dcw_ref[...] = jnp.zeros_like(dcw_ref)
        dcb_ref[...] = jnp.zeros_like(dcb_ref)
        dlam_ref[...] = jnp.zeros_like(dlam_ref)
        dba_ref[...] = jnp.zeros_like(dba_ref)
        dbx_ref[...] = jnp.zeros_like(dbx_ref)
        dwbd_ref[...] = jnp.zeros_like(dwbd_ref)

        def direction(uc, r, i_g, dht, h_nb, sp_d):
            a, beta, inv_beta = _lru_coeffs_inv(r, sp_d)
            da = dht * h_nb
            dbeta = dht * (i_g * uc)
            d_iu = dht * beta
            dlog_a = da * a - (a * a) * (dbeta * inv_beta)
            dlr = dlog_a * r
            dsp = -RGLRU_C * jnp.sum(dlr, axis=0, keepdims=True)
            dpre_r = (dlr * (1.0 - r)) * (-RGLRU_C * sp_d)
            dpre_i = (d_iu * uc) * (i_g * (1.0 - i_g))
            return dpre_r, dpre_i, d_iu * i_g, dsp

        def phase4(i, c):
            uc, (um2, um1, u0, up1) = _conv_tile(u_ref, i, S, cw, cb)
            r_f, i_f, r_b, i_b = _lru_gates(uc, wbd_v, ba_v, bx_v)
            rows = pl.ds(pl.multiple_of(i * LRU_TT, LRU_TT), LRU_TT)
            dh = dh_ref[rows, :]
            dht_f = dh + _shift(_halo(af_ref, i, S), 1)
            h_prev = _shift(_halo(bf_ref, i, S), -1)
            dht_b = dh + _shift(_halo(ab_ref, i, S), -1)
            h_next = _shift(_halo(bb_ref, i, S), 1)
            prf, pif, duc_f, dsp_f = direction(uc, r_f, i_f, dht_f, h_prev, sp[0:1])
            prb, pib, duc_b, dsp_b = direction(uc, r_b, i_b, dht_b, h_next, sp[1:2])
            dpre = jnp.concatenate([prf, pif, prb, pib], axis=1)
            dpre_b = dpre.astype(bf16)
            duc = (duc_f + duc_b) + lax.dot_general(dpre_b, wbd_v, _DIMS["nt"], preferred_element_type=f32)
            dwbd_ref[...] += lax.dot_general(uc.astype(bf16), dpre_b, _DIMS["tn"], preferred_element_type=f32)
            colsum = lambda v: jnp.sum(v, axis=0, keepdims=True)
            dba_ref[...] += jnp.concatenate([colsum(prf), colsum(prb)], axis=0)
            dbx_ref[...] += jnp.concatenate([colsum(pif), colsum(pib)], axis=0)
            dlam_ref[...] += jnp.concatenate([dsp_f, dsp_b], axis=0)
            dcb_ref[...] += colsum(duc)
            dcw_ref[...] += jnp.concatenate([colsum(duc * um2), colsum(duc * um1), colsum(duc * u0),
                                             colsum(duc * up1)], axis=0)
            af_ref[rows, :] = duc
            return c

        lax.fori_loop(0, nt, phase4, 0)
        dlam_ref[...] = dlam_ref[...] * (-_sigmoid(-lam_v))

        def phase5(i, c):
            ext = _halo(af_ref, i, S)
            rows = pl.ds(pl.multiple_of(i * LRU_TT, LRU_TT), LRU_TT)
            du = (_shift(ext, 2) * cw[0:1] + _shift(ext, 1) * cw[1:2] + ext[HALO:HALO + LRU_TT] * cw[2:3]
                  + _shift(ext, -1) * cw[3:4])
            du_ref[rows, :] = du.astype(du_ref.dtype)
            return c

        lax.fori_loop(0, nt, phase5, 0)

    seq, par = _lru_specs(S)
    return pl.pallas_call(
        body, name="lru_bwd", grid=(NCH,),
        in_specs=[seq(0), seq(NCH), seq(0), ANY, par(4), par(1), par(2), par(2), par(2),
                  pl.BlockSpec((None, CW, 4 * CW), lambda j: (j, 0, 0))],
        out_specs=[seq(0), seq(0), par(4), par(1), par(2), par(2), par(2),
                   pl.BlockSpec((None, CW, 4 * CW), lambda j: (j, 0, 0))],
        out_shape=[_sds((S, D), bf16), _sds((S, D), bf16), _sds((4, D), f32), _sds((1, D), f32), _sds((2, D), f32),
                   _sds((2, D), f32), _sds((2, D), f32), _sds((NCH, CW, 4 * CW), f32)],
        scratch_shapes=[pltpu.VMEM((S, CW), f32)] * 5 + [pltpu.SemaphoreType.DMA((4,))], compiler_params=_params(1, True),
    )(proj, proj, dy, state, conv_w, conv_b, lam, ba, bx, wbd)


_SLOPES = [2.0 ** (-8.0 * (h + 1) / NH) for h in range(NH)]


def _half_mask(shape, e):
    lane = lax.broadcasted_iota(jnp.int32, shape, 1)
    return (lane < HD) if e == 0 else (lane >= HD)


def _both_halves(x, src):
    return jnp.where(_half_mask(x.shape, src), x, pltpu.roll(x, HD, 1))


def _fold_halves(x, dst):
    return jnp.where(_half_mask(x.shape, dst), x + pltpu.roll(x, HD, 1), 0.0)


def _attn_base(n, S):
    tq = lax.broadcasted_iota(jnp.int32, (BLK, 3 * BLK), 0)
    sk = lax.broadcasted_iota(jnp.int32, (BLK, 3 * BLK), 1)
    dist = jnp.abs(tq + BLK - sk)
    kpos = n * BLK - BLK + sk
    valid = (dist <= BLK) & (kpos >= 0) & (kpos < S)
    return jnp.where(valid, -dist.astype(f32), NEG_INF)


def _group_heads(ref, kvh, scale):
    parts = []
    for i in range(4):
        pair = 2 * kvh + i // 2
        x = ref[:, pair * 128:(pair + 1) * 128].astype(f32)
        parts.append(jnp.where(_half_mask(x.shape, i % 2), x * scale, 0.0))
    return parts


def _stack_bf16(parts):
    return jnp.concatenate([p.astype(bf16) for p in parts], axis=0)


def _attn_softmax(s_raw, base, slope, sink):
    s = s_raw + slope * base
    m = jnp.maximum(jnp.max(s, axis=-1, keepdims=True), sink)
    p = jnp.exp(s - m)
    esink = jnp.exp(sink - m)
    inv = 1.0 / (jnp.sum(p, axis=-1, keepdims=True) + esink)
    return p, inv, esink * inv


def _attn_specs(S):
    nb = S // BLK
    q_spec = pl.BlockSpec((BLK, D), lambda n: (n, 2))
    kv = lambda col: [pl.BlockSpec((BLK, 256), lambda n: (jnp.maximum(n - 1, 0), col)),
                      pl.BlockSpec((BLK, 256), lambda n: (n, col)),
                      pl.BlockSpec((BLK, 256), lambda n: (jnp.minimum(n + 1, nb - 1), col))]
    return nb, q_spec, kv(COL_K), kv(COL_V)


def _attn_fwd(proj, sink):
    S = proj.shape[0]
    nb, q_spec, k_specs, v_specs = _attn_specs(S)

    def body(sink_ref, q_ref, kp_ref, kc_ref, kn_ref, vp_ref, vc_ref, vn_ref, o_ref):
        base = _attn_base(pl.program_id(0), S)
        kcat = jnp.concatenate([kp_ref[...], kc_ref[...], kn_ref[...]], axis=0).astype(f32)
        vcat = jnp.concatenate([vp_ref[...], vc_ref[...], vn_ref[...]], axis=0).astype(f32)
        even = _half_mask((BLK, 128), 0)
        for kvh in range(NH // 4):
            ch, off = kvh // 2, kvh % 2
            kb = _both_halves(kcat[:, ch * 128:(ch + 1) * 128], off).astype(bf16)
            vb = _both_halves(vcat[:, ch * 128:(ch + 1) * 128], off).astype(bf16)
            q4 = _stack_bf16(_group_heads(q_ref, kvh, HD ** -0.5))
            s4 = lax.dot_general(q4, kb, _DIMS["nt"], preferred_element_type=f32)
            ps, invs = [], []
            for i in range(4):
                h = 4 * kvh + i
                p, inv, _ = _attn_softmax(s4[i * BLK:(i + 1) * BLK], base, _SLOPES[h], sink_ref[0, h])
                ps.append(p)
                invs.append(inv)
            o4 = jnp.dot(_stack_bf16(ps), vb, preferred_element_type=f32)
            for pr in range(2):
                lo = o4[(2 * pr) * BLK:(2 * pr + 1) * BLK] * invs[2 * pr]
                hi = o4[(2 * pr + 1) * BLK:(2 * pr + 2) * BLK] * invs[2 * pr + 1]
                pair = 2 * kvh + pr
                o_ref[:, pair * 128:(pair + 1) * 128] = jnp.where(even, lo, hi).astype(o_ref.dtype)

    return pl.pallas_call(
        body, name="attn_fwd", grid=(nb,),
        in_specs=[pl.BlockSpec(memory_space=pltpu.SMEM), q_spec] + k_specs + v_specs,
        out_specs=pl.BlockSpec((BLK, D), lambda n: (n, 0)), out_shape=_sds((S, D), bf16),
        compiler_params=_params(1, True))(sink, proj, proj, proj, proj, proj, proj, proj)


def _attn_bwd(proj, sink, y_b, dy_b):
    S = proj.shape[0]
    nb, q_spec, k_specs, v_specs = _attn_specs(S)

    def body(sink_ref, q_ref, kp_ref, kc_ref, kn_ref, vp_ref, vc_ref, vn_ref, o_ref, do_ref,
             dq_ref, dk_ref, dv_ref, dsink_ref):
        n = pl.program_id(0)

        @pl.when(n == 0)
        def _():
            dk_ref[...] = jnp.zeros_like(dk_ref)
            dv_ref[...] = jnp.zeros_like(dv_ref)
            dsink_ref[...] = jnp.zeros_like(dsink_ref)

        base = _attn_base(n, S)
        kcat = jnp.concatenate([kp_ref[...], kc_ref[...], kn_ref[...]], axis=0).astype(f32)
        vcat = jnp.concatenate([vp_ref[...], vc_ref[...], vn_ref[...]], axis=0).astype(f32)
        dk_acc = [jnp.zeros((3 * BLK, 128), f32), jnp.zeros((3 * BLK, 128), f32)]
        dv_acc = [jnp.zeros((3 * BLK, 128), f32), jnp.zeros((3 * BLK, 128), f32)]
        scale = HD ** -0.5
        even = _half_mask((BLK, 128), 0)
        for kvh in range(NH // 4):
            ch, off = kvh // 2, kvh % 2
            kb = _both_halves(kcat[:, ch * 128:(ch + 1) * 128], off).astype(bf16)
            vb = _both_halves(vcat[:, ch * 128:(ch + 1) * 128], off).astype(bf16)
            q_parts = _group_heads(q_ref, kvh, scale)
            d_parts = _group_heads(do_ref, kvh, 1.0)
            s4 = lax.dot_general(_stack_bf16(q_parts), kb, _DIMS["nt"], preferred_element_type=f32)
            dp4 = lax.dot_general(_stack_bf16(d_parts), vb, _DIMS["nt"], preferred_element_type=f32)
            ts, ps, qn, dn, invs = [], [], [], [], []
            for i in range(4):
                h = 4 * kvh + i
                pair = 2 * kvh + i // 2
                rows = slice(i * BLK, (i + 1) * BLK)
                p, inv, psink = _attn_softmax(s4[rows], base, _SLOPES[h], sink_ref[0, h])
                delta = jnp.sum(d_parts[i] * o_ref[:, pair * 128:(pair + 1) * 128].astype(f32), axis=-1, keepdims=True)
                dsink_ref[h:h + 1, :] += jnp.broadcast_to(-jnp.sum(psink * delta, axis=0, keepdims=True), (1, 128))
                ts.append(p * (dp4[rows] - delta))
                ps.append(p)
                qn.append(q_parts[i] * inv)
                dn.append(d_parts[i] * inv)
                invs.append(inv)
            t4 = _stack_bf16(ts)
            dq4 = jnp.dot(t4, kb, preferred_element_type=f32)
            for pr in range(2):
                lo = dq4[(2 * pr) * BLK:(2 * pr + 1) * BLK] * invs[2 * pr]
                hi = dq4[(2 * pr + 1) * BLK:(2 * pr + 2) * BLK] * invs[2 * pr + 1]
                pair = 2 * kvh + pr
                dq_ref[:, pair * 128:(pair + 1) * 128] = (jnp.where(even, lo, hi) * scale).astype(dq_ref.dtype)
            dk_both = lax.dot_general(t4, _stack_bf16(qn), _DIMS["tn"], preferred_element_type=f32)
            dv_both = lax.dot_general(_stack_bf16(ps), _stack_bf16(dn), _DIMS["tn"], preferred_element_type=f32)
            dk_acc[ch] = dk_acc[ch] + _fold_halves(dk_both, off)
            dv_acc[ch] = dv_acc[ch] + _fold_halves(dv_both, off)
        for j in range(3):
            blk = n + (j - 1)

            @pl.when((blk >= 0) & (blk < nb))
            def _():
                rows = pl.ds(pl.multiple_of(blk * BLK, BLK), BLK)
                for ch in range(2):
                    dk_ref[rows, ch * 128:(ch + 1) * 128] += dk_acc[ch][j * BLK:(j + 1) * BLK]
                    dv_ref[rows, ch * 128:(ch + 1) * 128] += dv_acc[ch][j * BLK:(j + 1) * BLK]

    row_blk = pl.BlockSpec((BLK, D), lambda n: (n, 0))
    full = pl.BlockSpec((S, 256), lambda n: (0, 0))
    return pl.pallas_call(
        body, name="attn_bwd", grid=(nb,),
        in_specs=[pl.BlockSpec(memory_space=pltpu.SMEM), q_spec] + k_specs + v_specs + [row_blk, row_blk],
        out_specs=[row_blk, full, full, pl.BlockSpec((NH, 128), lambda n: (0, 0))],
        out_shape=[_sds((S, D), bf16), _sds((S, 256), f32), _sds((S, 256), f32), _sds((NH, 128), f32)],
        compiler_params=_params(1, True))(sink, proj, proj, proj, proj, proj, proj, proj, y_b, dy_b)


def _adamw(name, w, g, m, v, tr):
    R, C = w.shape
    tr = min(tr, R)

    def body(w_ref, g_ref, m_ref, v_ref, d_ref, m2_ref, v2_ref):
        g = g_ref[...]
        m2 = ADAM_B1 * m_ref[...] + (1.0 - ADAM_B1) * g
        v2 = ADAM_B2 * v_ref[...] + (1.0 - ADAM_B2) * (g * g)
        m_hat = m2 / (1.0 - ADAM_B1 ** ADAM_STEP)
        v_hat = v2 / (1.0 - ADAM_B2 ** ADAM_STEP)
        d_ref[...] = -ADAM_LR * (m_hat / (jnp.sqrt(v_hat) + ADAM_EPS) + ADAM_WD * w_ref[...])
        m2_ref[...] = m2
        v2_ref[...] = v2

    blk = pl.BlockSpec((tr, C), lambda i: (i, 0))
    return pl.pallas_call(body, name=name, grid=(R // tr,), in_specs=[blk] * 4, out_specs=[blk] * 3,
                          out_shape=[_sds((R, C), f32)] * 3, compiler_params=_params(1))(w, g, m, v)


def _pair_sum(name, c_arr, g4, recv, th):
    _, _, h, w = g4.shape
    th = min(th, h)

    def body(c_ref, g_ref, r_ref, o_ref, ob_ref):
        p = g_ref[...] + r_ref[...]
        o_ref[...] = p
        ob_ref[...] = p.astype(bf16)

    blk = pl.BlockSpec((None, th, w), lambda s, i, c_ref: (s, i, 0))
    spec = pltpu.PrefetchScalarGridSpec(
        num_scalar_prefetch=1, grid=(NCHIP, h // th),
        in_specs=[pl.BlockSpec((None, None, th, w), lambda s, i, c_ref: (s, c_ref[0], i, 0)), blk],
        out_specs=[blk, blk])
    return pl.pallas_call(body, name=name, grid_spec=spec,
                          out_shape=[_sds((NCHIP, h, w), f32), _sds((NCHIP, h, w), bf16)],
                          compiler_params=_params(2))(c_arr, g4, recv)


def _chip_sum(name, chip_arr, own4, recv3, th):
    _, h, w = own4.shape
    th = min(th, h)

    def body(s_ref, o_ref, r_ref, out_ref):
        out_ref[...] = ((o_ref[...] + r_ref[0].astype(f32)) + r_ref[1].astype(f32)) + r_ref[2].astype(f32)

    spec = pltpu.PrefetchScalarGridSpec(
        num_scalar_prefetch=1, grid=(h // th,),
        in_specs=[pl.BlockSpec((None, th, w), lambda i, s_ref: (s_ref[0], i, 0)),
                  pl.BlockSpec((3, th, w), lambda i, s_ref: (0, i, 0))],
        out_specs=pl.BlockSpec((th, w), lambda i, s_ref: (i, 0)))
    return pl.pallas_call(body, name=name, grid_spec=spec, out_shape=_sds((h, w), f32),
                          compiler_params=_params(1, True))(chip_arr, own4, recv3)


def _adamw_halves(name, c_arr, w, g_own, g_recv, m, v, th):
    h, wd = g_own.shape
    th = min(th, h)

    def body(c_ref, w_ref, go_ref, gr_ref, m_ref, v_ref, g_ref, d_ref, m2_ref, v2_ref):
        g = jnp.where(c_ref[0] == pl.program_id(0), go_ref[...], gr_ref[...])
        m2 = ADAM_B1 * m_ref[...] + (1.0 - ADAM_B1) * g
        v2 = ADAM_B2 * v_ref[...] + (1.0 - ADAM_B2) * (g * g)
        m_hat = m2 / (1.0 - ADAM_B1 ** ADAM_STEP)
        v_hat = v2 / (1.0 - ADAM_B2 ** ADAM_STEP)
        g_ref[...] = g
        d_ref[...] = -ADAM_LR * (m_hat / (jnp.sqrt(v_hat) + ADAM_EPS) + ADAM_WD * w_ref[...])
        m2_ref[...] = m2
        v2_ref[...] = v2

    nt = h // th
    full = pl.BlockSpec((th, wd), lambda hh, i, c_ref: (hh * nt + i, 0))
    half = pl.BlockSpec((th, wd), lambda hh, i, c_ref: (i, 0))
    spec = pltpu.PrefetchScalarGridSpec(num_scalar_prefetch=1, grid=(2, nt),
                                        in_specs=[full, half, half, full, full], out_specs=[full] * 4)
    return pl.pallas_call(body, name=name, grid_spec=spec, out_shape=[_sds((2 * h, wd), f32)] * 4,
                          compiler_params=_params(2))(c_arr, w, g_own, g_recv, m, v)


def _add2(name, a, b):
    def body(a_ref, b_ref, o_ref):
        o_ref[...] = a_ref[...] + b_ref[...]
    return pl.pallas_call(body, name=name, out_shape=_sds(a.shape, f32))(a, b)


def _sum4(name, b4, th):
    _, h, w = b4.shape
    th = min(th, h)

    def body(b_ref, o_ref):
        o_ref[...] = ((b_ref[0] + b_ref[1]) + b_ref[2]) + b_ref[3]

    return pl.pallas_call(body, name=name, grid=(h // th,),
                          in_specs=[pl.BlockSpec((NCHIP, th, w), lambda i: (0, i, 0))],
                          out_specs=pl.BlockSpec((th, w), lambda i: (i, 0)), out_shape=_sds((h, w), f32),
                          compiler_params=_params(1, True))(b4)


def _coords():
    x, y, c = lax.axis_index("x"), lax.axis_index("y"), lax.axis_index("c")
    return x, y, c, [(1 - x, y), (x, 1 - y), (1 - x, 1 - y)]


def _gather_chips(arrs):
    n = len(arrs)

    def body(*refs):
        ins, outs = refs[:n], refs[n:2 * n]
        send_sems, recv_sems, local_sems = refs[2 * n:2 * n + 3]
        stage = refs[2 * n + 3:]
        x, y, c, chips = _coords()
        s = 2 * x + y
        sib = (x, y, 1 - c)
        load = [pltpu.make_async_copy(ins[a], stage[a], local_sems.at[a]) for a in range(n)]
        local = [pltpu.make_async_copy(stage[a], outs[a].at[s], local_sems.at[n + a]) for a in range(n)]
        for cp in load:
            cp.start()

        def over_ici(k, a, slot, peer):
            return pltpu.make_async_remote_copy(src_ref=ins[a].at[c], dst_ref=outs[a].at[slot, c], send_sem=send_sems.at[k * n + a],
                                                recv_sem=recv_sems.at[k * n + a], device_id=peer, device_id_type=MESH)

        def to_sibling(k, a, slot, half):
            i = (3 + k) * n + a
            return pltpu.make_async_remote_copy(src_ref=outs[a].at[slot, half], dst_ref=outs[a].at[slot, half], send_sem=send_sems.at[i],
                                                recv_sem=recv_sems.at[i], device_id=sib, device_id_type=MESH)

        sends = [over_ici(k, a, s, (px, py, c)) for k, (px, py) in enumerate(chips) for a in range(n)]
        for cp in sends:
            cp.start()
        for a in range(n):
            load[a].wait()
            local[a].start()
        passed = []
        for k, (px, py) in enumerate(chips):
            for a in range(n):
                over_ici(k, a, 2 * px + py, (px, py, c)).wait_recv()
                cp = to_sibling(k, a, 2 * px + py, c)
                cp.start()
                passed.append(cp)
        for k, (px, py) in enumerate(chips):
            for a in range(n):
                to_sibling(k, a, 2 * px + py, 1 - c).wait_recv()
        for cp in sends + passed:
            cp.wait_send()
        for cp in local:
            cp.wait()

    return pl.pallas_call(
        body, name="gather_weights", in_specs=[ANY] * n, out_specs=[ANY] * n,
        out_shape=[_sds((NCHIP,) + a.shape, a.dtype) for a in arrs],
        scratch_shapes=[pltpu.SemaphoreType.DMA((6 * n,)), pltpu.SemaphoreType.DMA((6 * n,)), pltpu.SemaphoreType.DMA((2 * n,))]
        + [pltpu.VMEM(a.shape, a.dtype) for a in arrs],
        compiler_params=pltpu.CompilerParams(vmem_limit_bytes=VMEM_LIMIT),
    )(*arrs)


HBM = pl.BlockSpec(memory_space=pltpu.HBM)
SEM = pl.BlockSpec(memory_space=pltpu.SEMAPHORE)
EFFECT = pltpu.SideEffectType.DATAFLOW_SIDE_EFFECTING


def _split_start(name, n_copies, make_copies, ins, land_shapes, after):
    ni, nl = len(ins), len(land_shapes)

    def body(*refs):
        in_refs, land_refs = refs[:ni], refs[ni:ni + nl]
        send_sems, recv_sems = refs[ni + nl + 1], refs[ni + nl + 2]
        token = refs[-1]
        for cp in make_copies(in_refs, land_refs, send_sems, recv_sems):
            cp.start()
        token[...] = jnp.zeros_like(token)

    lands = [pltpu.with_memory_space_constraint(lax.empty(s.shape, s.dtype), pltpu.HBM) for s in land_shapes]
    res = pl.pallas_call(
        body, name=name,
        out_shape=(pltpu.SemaphoreType.DMA((n_copies,)), pltpu.SemaphoreType.DMA((n_copies,)),
                   *[pltpu.HBM(a.shape, a.dtype) for a in ins], *[pltpu.HBM(s.shape, s.dtype) for s in land_shapes],
                   _sds((8, 128), f32)),
        in_specs=[HBM] * (ni + nl) + [ANY], out_specs=(SEM, SEM, *[HBM] * (ni + nl), pl.BlockSpec(memory_space=pltpu.VMEM)),
        input_output_aliases={i: 2 + i for i in range(ni + nl)},
        compiler_params=pltpu.CompilerParams(has_side_effects=EFFECT),
    )(*[pltpu.with_memory_space_constraint(a, pltpu.HBM) for a in ins], *lands, after)
    return res[0], res[1], list(res[2:2 + ni]), list(res[2 + ni:2 + ni + nl]), res[-1]


def _split_wait(name, make_copies, send_sems, recv_sems, ins, lands, after):
    ni, nl = len(ins), len(lands)

    def body(*refs):
        in_refs, land_refs = refs[:ni], refs[ni:ni + nl]
        s_sems, r_sems = refs[ni + nl], refs[ni + nl + 1]
        for cp in make_copies(in_refs, land_refs, s_sems, r_sems):
            cp.wait_send()
            cp.wait_recv()

    res = pl.pallas_call(
        body, name=name, out_shape=tuple(pltpu.HBM(a.shape, a.dtype) for a in ins + lands),
        in_specs=[HBM] * (ni + nl) + [SEM, SEM, ANY], out_specs=tuple([HBM] * (ni + nl)),
        input_output_aliases={i: i for i in range(ni + nl)},
        compiler_params=pltpu.CompilerParams(has_side_effects=EFFECT),
    )(*ins, *lands, send_sems, recv_sems, after)
    return list(res[:ni]), list(res[ni:])


def _gather_copies(n):
    def make(in_refs, land_refs, send_sems, recv_sems):
        x, y, c, chips = _coords()
        s = 2 * x + y
        return [pltpu.make_async_remote_copy(src_ref=in_refs[a], dst_ref=land_refs[a].at[s], send_sem=send_sems.at[k * n + a],
                                             recv_sem=recv_sems.at[k * n + a], device_id=(px, py, c), device_id_type=MESH)
                for k, (px, py) in enumerate(chips) for a in range(n)]
    return make


def _sibling_half_copies(n):
    def make(in_refs, land_refs, send_sems, recv_sems):
        x, y, c, _ = _coords()
        return [pltpu.make_async_remote_copy(src_ref=in_refs[a].at[:, 1 - c], dst_ref=land_refs[a], send_sem=send_sems.at[a],
                                             recv_sem=recv_sems.at[a], device_id=(x, y, 1 - c), device_id_type=MESH)
                for a in range(n)]
    return make


def _chip_part_copies(n):
    def make(in_refs, land_refs, send_sems, recv_sems):
        x, y, c, chips = _coords()
        return [pltpu.make_async_remote_copy(src_ref=in_refs[a].at[2 * px + py], dst_ref=land_refs[a].at[k],
                                             send_sem=send_sems.at[k * n + a], recv_sem=recv_sems.at[k * n + a],
                                             device_id=(px, py, c), device_id_type=MESH)
                for k, (px, py) in enumerate(chips) for a in range(n)]
    return make


def _sibling_whole_copies(n):
    def make(in_refs, land_refs, send_sems, recv_sems):
        x, y, c, _ = _coords()
        return [pltpu.make_async_remote_copy(src_ref=in_refs[a], dst_ref=land_refs[a], send_sem=send_sems.at[a],
                                             recv_sem=recv_sems.at[a], device_id=(x, y, 1 - c), device_id_type=MESH)
                for a in range(n)]
    return make


def _place_own(chip_arr, owns, lands, steps):
    n = len(owns)

    def body(s_ref, *refs):
        for a in range(n):
            refs[2 * n + a][...] = refs[a][...]

    tiles = [o.shape[0] // steps for o in owns]
    spec = pltpu.PrefetchScalarGridSpec(
        num_scalar_prefetch=1, grid=(steps,),
        in_specs=[pl.BlockSpec((t, o.shape[1]), lambda i, s_ref: (i, 0)) for t, o in zip(tiles, owns)] + [ANY] * n,
        out_specs=[pl.BlockSpec((None, t, o.shape[1]), lambda i, s_ref: (s_ref[0], i, 0)) for t, o in zip(tiles, owns)])
    return pl.pallas_call(body, name="place_own", grid_spec=spec, out_shape=[_sds(l.shape, l.dtype) for l in lands],
                          input_output_aliases={1 + n + a: a for a in range(n)},
                          compiler_params=_params(1))(chip_arr, *owns, *lands)


def _sibling_halves(g4s, small):
    n = len(g4s)

    def body(*refs):
        ins, small_ref = refs[:n], refs[n]
        outs, small_out = refs[n + 1:2 * n + 1], refs[2 * n + 1]
        send_sems, recv_sems = refs[2 * n + 2:]
        x, y, c, _ = _coords()
        sib = (x, y, 1 - c)

        def remote(a, half):
            src = small_ref if a == n else ins[a].at[:, half]
            dst = small_out if a == n else outs[a]
            return pltpu.make_async_remote_copy(src_ref=src, dst_ref=dst, send_sem=send_sems.at[a], recv_sem=recv_sems.at[a],
                                                device_id=sib, device_id_type=MESH)

        sends = [remote(a, 1 - c) for a in range(n + 1)]
        for cp in sends:
            cp.start()
        for a in range(n + 1):
            remote(a, c).wait_recv()
        for cp in sends:
            cp.wait_send()

    return pl.pallas_call(
        body, name="reduce_sibling", in_specs=[ANY] * (n + 1), out_specs=[ANY] * (n + 1),
        out_shape=[_sds((g.shape[0],) + g.shape[2:], f32) for g in g4s] + [_sds(small.shape, f32)],
        scratch_shapes=[pltpu.SemaphoreType.DMA((n + 1,)), pltpu.SemaphoreType.DMA((n + 1,))],
    )(*g4s, small)


def _exchange_chips(parts, small2):
    n = len(parts)

    def body(*refs):
        ins, small_ref = refs[:n], refs[n]
        outs, small_out = refs[n + 1:2 * n + 1], refs[2 * n + 1]
        send_sems, recv_sems, local_sem = refs[2 * n + 2:]
        x, y, c, chips = _coords()
        s = 2 * x + y
        local = pltpu.make_async_copy(small_ref.at[c], small_out.at[s], local_sem)
        local.start()

        def remote(k, a, dest_chip, small_slot, peer):
            if a == n:
                src, dst = small_ref.at[c], small_out.at[small_slot]
            else:
                src, dst = ins[a].at[dest_chip], outs[a].at[k]
            i = k * (n + 1) + a
            return pltpu.make_async_remote_copy(src_ref=src, dst_ref=dst, send_sem=send_sems.at[i], recv_sem=recv_sems.at[i],
                                                device_id=peer, device_id_type=MESH)

        sends = [remote(k, a, 2 * px + py, s, (px, py, c)) for k, (px, py) in enumerate(chips) for a in range(n + 1)]
        for cp in sends:
            cp.start()
        for k, (px, py) in enumerate(chips):
            for a in range(n + 1):
                remote(k, a, s, 2 * px + py, (px, py, c)).wait_recv()
        for cp in sends:
            cp.wait_send()
        local.wait()

    m = 3 * (n + 1)
    return pl.pallas_call(
        body, name="reduce_chips", in_specs=[ANY] * (n + 1), out_specs=[ANY] * (n + 1),
        out_shape=[_sds((3,) + p.shape[1:], p.dtype) for p in parts] + [_sds((NCHIP,) + small2.shape[1:], f32)],
        scratch_shapes=[pltpu.SemaphoreType.DMA((m,)), pltpu.SemaphoreType.DMA((m,)), pltpu.SemaphoreType.DMA],
    )(*parts, small2)


def _share_sibling(halves):
    n = len(halves)

    def body(*refs):
        ins, outs = refs[:n], refs[n:2 * n]
        send_sems, recv_sems = refs[2 * n:]
        x, y, c, _ = _coords()
        sib = (x, y, 1 - c)
        sends = [pltpu.make_async_remote_copy(src_ref=ins[a], dst_ref=outs[a], send_sem=send_sems.at[a], recv_sem=recv_sems.at[a],
                                              device_id=sib, device_id_type=MESH) for a in range(n)]
        for cp in sends:
            cp.start()
        for cp in sends:
            cp.wait()

    return pl.pallas_call(
        body, name="reduce_share", in_specs=[ANY] * n, out_specs=[ANY] * n,
        out_shape=[_sds(h.shape, f32) for h in halves],
        scratch_shapes=[pltpu.SemaphoreType.DMA((n,)), pltpu.SemaphoreType.DMA((n,))],
    )(*halves)


def _block_diag_pairs(w):
    w = w.reshape(NCH, 2, HD, HD)
    z = jnp.zeros((NCH, HD, HD), w.dtype)
    return jnp.concatenate([jnp.concatenate([w[:, 0], z], axis=2), jnp.concatenate([z, w[:, 1]], axis=2)], axis=1)


def _diag_blocks(m):
    return jnp.stack([m[:, :HD, :HD], m[:, HD:, HD:]], axis=1).reshape(NH, HD, HD)


def _pack(vs, rows):
    flat = jnp.concatenate([v.reshape(-1) for v in vs])
    return jnp.pad(flat, (0, rows * 128 - flat.shape[0])).reshape(rows, 128)


def _unpack(packed, shapes):
    flat = packed.reshape(-1)
    out, off = [], 0
    for shp in shapes:
        size = math.prod(shp)
        out.append(flat[off:off + size].reshape(shp))
        off += size
    return out


def _rows_for(sizes, multiple):
    rows = -(-sum(sizes) // 128)
    return -(-rows // multiple) * multiple


def kernel(x, norm_mix_g, w_in, b_gate, conv_w, conv_b, lru_lambda, lru_wa, lru_ba, lru_wx, lru_bx, attn_sink, w_out, norm_ffn_g, w_ffn_in, w_ffn_out, norm_final_g, loss_target, m_norm_mix_g, m_w_in, m_b_gate, m_conv_w, m_conv_b, m_lru_lambda, m_lru_wa, m_lru_ba, m_lru_wx, m_lru_bx, m_attn_sink, m_w_out, m_norm_ffn_g, m_w_ffn_in, m_w_ffn_out, m_norm_final_g, v_norm_mix_g, v_w_in, v_b_gate, v_conv_w, v_conv_b, v_lru_lambda, v_lru_wa, v_lru_ba, v_lru_wx, v_lru_bx, v_attn_sink, v_w_out, v_norm_ffn_g, v_w_ffn_in, v_w_ffn_out, v_norm_final_g):
    S = x.shape[1]
    xs = x[0]
    tgt = loss_target[0]
    cx, cy, cc = lax.axis_index("x"), lax.axis_index("y"), lax.axis_index("c")
    chip = 2 * cx + cy
    SW = D // NCHIP

    small_shard = _pack([conv_w[0], lru_lambda[0], lru_ba[0], lru_bx[0]], 32)
    halves_of = lambda a: a.reshape(2, a.shape[0] // 2, a.shape[1])
    w_in_g, small_g = _gather_chips([halves_of(w_in[0].astype(bf16)), halves_of(small_shard)])
    w_in_g = w_in_g.reshape(NCHIP, D, SHW)
    small_g = small_g.reshape(NCHIP, 32, 128)
    late = [w_ffn_in[0].astype(bf16), w_out[0].astype(bf16), w_ffn_out[0].astype(bf16)]
    late_send, late_recv, late_src, late_land, late_token = _split_start(
        "gather_late_start", 9, _gather_copies(3), late, [_sds((NCHIP,) + a.shape, bf16) for a in late], small_g)
    small_parts = [_unpack(small_g[s], [(4, SW), (2, SW), (2, SW), (2, SW)]) for s in range(NCHIP)]
    conv_w_f, lam_f, ba_f, bx_f = [jnp.concatenate([small_parts[s][p] for s in range(NCHIP)], axis=1) for p in range(4)]
    wbd = jnp.concatenate([_block_diag_pairs(lru_wa[0, 0]), _block_diag_pairs(lru_wx[0, 0]),
                           _block_diag_pairs(lru_wa[0, 1]), _block_diag_pairs(lru_wx[0, 1])], axis=2).astype(bf16)
    conv_b_f = conv_b
    sink = attn_sink

    xn, proj = _rms_matmul("rms_proj", xs, norm_mix_g + late_token[0:1, 0:1], w_in_g, 1024)
    y_a, lru_state = _lru_fwd(proj, conv_w_f, conv_b_f, lam_f, ba_f, bx_f, wbd)
    y_b = _attn_fwd(proj, sink)
    merged = _merge_fwd(proj, b_gate, y_a, y_b, 512)
    late_src, late_land = _split_wait("gather_late_wait", _gather_copies(3), late_send, late_recv, late_src, late_land, merged)
    chip_arr = chip.reshape(1).astype(jnp.int32)
    w_ffn_in_g, w_out_g, w_ffn_out_g = _place_own(chip_arr, late_src, late_land, 4)
    w_out_f = w_out_g.reshape(D, D)
    w_ffn_out_f = w_ffn_out_g.reshape(FF, D)
    x1 = _mm_residual("out_proj", merged, w_out_f, xs, 512)
    xn2, gu, act = _rms_matmul_swiglu("rms_ffn_in", x1, norm_ffn_g, w_ffn_in_g, 1024)
    x2 = _mm_residual("ffn_out", act, w_ffn_out_f, x1, 512)
    dx2, loss_row, dg3 = _final_loss_bwd(x2, norm_final_g.reshape(1, D), tgt, 256)

    tm = min(1024, S)
    tk = min(2048, S)
    gw_ffn_out = _mm_tn("dw_ffn_out", act, pl.BlockSpec((tk, SHW), lambda i, k: (k, i)),
                        dx2, pl.BlockSpec((tk, D), lambda i, k: (k, 0)),
                        _sds((FF, D), f32), pl.BlockSpec((SHW, D), lambda i, k: (i, 0)), (2, S // tk), (SHW, D))
    dgu = _swiglu_bwd(dx2, w_ffn_out_f, gu, 256)
    dxn2 = _mm_nt_groups("dxn2", dgu, pl.BlockSpec((None, tm, SHW), lambda i, g: (g // 2, i, g % 2)), w_ffn_in_g, S, tm)
    gw_ffn_in = _mm_tn("dw_ffn_in", xn2, pl.BlockSpec((tk, D), lambda g, k: (k, 0)),
                       dgu, pl.BlockSpec((None, tk, SHW), lambda g, k: (g // 2, k, g % 2)),
                       _sds((NCHIP, D, SHW), f32), pl.BlockSpec((None, D, SHW), lambda g, k: (g, 0, 0)),
                       (NCHIP, S // tk), (D, SHW))
    c_arr = cc.reshape(1).astype(jnp.int32)
    early_names, early_tiles = ["w_ffn_in", "w_ffn_out"], [256, 352]
    early = [gw_ffn_in.reshape(NCHIP, 2, D // 2, SHW), gw_ffn_out.reshape(NCHIP, 2, FF // NCHIP // 2, D)]
    ea_send, ea_recv, ea_src, ea_land, ea_token = _split_start(
        "reduce_early_sibling_start", 2, _sibling_half_copies(2), early,
        [_sds((NCHIP,) + g.shape[2:], f32) for g in early], dxn2)
    dx1, dg2 = _rms_bwd("rms_ffn_bwd", x1, norm_ffn_g + ea_token[0:1, 0:1], dxn2, dx2, 256)

    dmerged = _mm_nt_resident("d_merged", dx1, w_out_f, 512)
    gw_out = _mm_tn("dw_out", merged, pl.BlockSpec((tk, D), lambda i, k: (k, 0)),
                    dx1, pl.BlockSpec((tk, D), lambda i, k: (k, 0)),
                    _sds((D, D), f32), pl.BlockSpec((D, D), lambda i, k: (0, 0)), (1, S // tk), (D, D))
    dz0, dz1, dy_a, dy_b, db0, db1 = _merge_bwd(proj, b_gate, y_a, y_b, dmerged, 512)
    ea_src, ea_land = _split_wait("reduce_early_sibling_wait", _sibling_half_copies(2), ea_send, ea_recv, ea_src, ea_land, dy_b)
    early_pairs = [_pair_sum("pair_sum_" + nm, c_arr, g4, r, th)
                   for nm, g4, r, th in zip(early_names, ea_src, ea_land, early_tiles)]
    eb_send, eb_recv, eb_src, eb_land, eb_token = _split_start(
        "reduce_early_chips_start", 6, _chip_part_copies(2), [p[1] for p in early_pairs],
        [_sds((3,) + p[1].shape[1:], bf16) for p in early_pairs], early_pairs[0][0])
    dq, dk, dv, dsink = _attn_bwd(proj, sink + eb_token[0:1, 0:1], y_b, dy_b)
    _, eb_land = _split_wait("reduce_early_chips_wait", _chip_part_copies(2), eb_send, eb_recv, eb_src, eb_land, dq)
    early_halves = [_chip_sum("chip_sum_" + nm, chip_arr, p[0], r3, th)
                    for nm, p, r3, th in zip(early_names, early_pairs, eb_land, early_tiles)]
    ec_send, ec_recv, ec_src, ec_land, ec_token = _split_start(
        "reduce_early_share_start", 2, _sibling_whole_copies(2), early_halves, [_sds(h.shape, f32) for h in early_halves], dq)
    du, dgl, dcw, dcb, dlam, dba, dbx, dwbd = _lru_bwd(proj, dy_a, lru_state, conv_w_f, conv_b_f + ec_token[0:1, 0:1], lam_f, ba_f, bx_f, wbd)
    early_halves, early_other = _split_wait("reduce_early_share_wait", _sibling_whole_copies(2), ec_send, ec_recv, ec_src, ec_land, du)
    dproj = jnp.concatenate([du, dgl, dq, dk.astype(bf16), dv.astype(bf16), dz0, dz1], axis=1)
    gw_in = _mm_tn("dw_in", xn, pl.BlockSpec((tk, D), lambda g, k: (k, 0)),
                   dproj, pl.BlockSpec((tk, SHW), lambda g, k: (k, g)),
                   _sds((NCHIP, D, SHW), f32), pl.BlockSpec((None, D, SHW), lambda g, k: (g, 0, 0)),
                   (NCHIP, S // tk), (D, SHW))
    wa_send, wa_recv, wa_src, wa_land, wa_token = _split_start(
        "reduce_w_in_sibling_start", 1, _sibling_half_copies(1), [gw_in.reshape(NCHIP, 2, D // 2, SHW)],
        [_sds((NCHIP, D // 2, SHW), f32)], dproj)
    dxn =_mm_nt_groups("dxn", dproj, pl.BlockSpec((tm, SHW), lambda i, g: (i, g)), w_in_g, S, tm)
    wa_src, wa_land = _split_wait("reduce_w_in_sibling_wait", _sibling_half_copies(1), wa_send, wa_recv, wa_src, wa_land, dxn)
    w_in_pair = _pair_sum("pair_sum_w_in", c_arr, wa_src[0], wa_land[0], 256)
    wb_send, wb_recv, wb_src, wb_land, wb_token = _split_start(
        "reduce_w_in_chips_start", 3, _chip_part_copies(1), [w_in_pair[1]], [_sds((3, D // 2, SHW), bf16)], w_in_pair[0])
    grad_x, dg1 = _rms_bwd("rms_mix_bwd", xs, norm_mix_g + wb_token[0:1, 0:1], dxn, dx1, 256)
    _, wb_land = _split_wait("reduce_w_in_chips_wait", _chip_part_copies(1), wb_send, wb_recv, wb_src, wb_land, dg1)
    w_in_half = _chip_sum("chip_sum_w_in", chip_arr, w_in_pair[0], wb_land[0], 256)

    d_wa = jnp.stack([_diag_blocks(dwbd[:, :, 0:CW]), _diag_blocks(dwbd[:, :, 2 * CW:3 * CW])])
    d_wx = jnp.stack([_diag_blocks(dwbd[:, :, CW:2 * CW]), _diag_blocks(dwbd[:, :, 3 * CW:4 * CW])])
    small_full = [dg1, jnp.concatenate([db0, db1], axis=1), dcw, dcb, dlam, d_wa, dba, d_wx, dbx, dsink[:, 0], dg2, dg3,
                  loss_row[0, 0:1]]
    full_shapes = [(1, D), (1, 2 * D), (4, D), (1, D), (2, D), (2, NH, HD, HD), (2, D), (2, NH, HD, HD), (2, D), (NH,),
                   (1, D), (1, D), (1,)]
    rows_full = _rows_for([math.prod(s) for s in full_shapes], 16)
    small_vec = _pack(small_full, rows_full)

    late_names, late_tiles = ["w_in", "w_out"], [256, 128]
    big = [gw_out.reshape(NCHIP, 2, D // NCHIP // 2, D)]
    *recv_a, small_sib = _sibling_halves(big, small_vec)
    w_out_pair = _pair_sum("pair_sum_w_out", c_arr, big[0], recv_a[0], 128)
    small_chip = _add2("pair_sum_small", small_vec, small_sib).reshape(2, rows_full // 2, 128)
    *recv_b, small_all = _exchange_chips([w_out_pair[1]], small_chip)
    w_out_half = _chip_sum("chip_sum_w_out", chip_arr, w_out_pair[0], recv_b[0], 128)
    halves = [w_in_half, w_out_half, _sum4("chip_sum_small", small_all, rows_full // 2)]
    *recv_c, small_other = _share_sibling(halves)
    small_lo = jnp.where(cc == 0, halves[2], small_other)
    small_hi = jnp.where(cc == 0, small_other, halves[2])
    g_full = _unpack(jnp.concatenate([small_lo, small_hi], axis=0), full_shapes)

    out_big = {}
    for nm, w, g_own, g_recv, m, v, th in zip(late_names + early_names, [w_in, w_out, w_ffn_in, w_ffn_out],
                                              halves[:2] + early_halves, recv_c + early_other,
                                              [m_w_in, m_w_out, m_w_ffn_in, m_w_ffn_out],
                                              [v_w_in, v_w_out, v_w_ffn_in, v_w_ffn_out], late_tiles + early_tiles):
        g_, d_, m_, v_ = _adamw_halves("adamw_" + nm, c_arr, w[0], g_own, g_recv, m[0], v[0], th)
        out_big[nm] = (g_[None], d_[None], m_[None], v_[None])

    small_names = ["norm_mix_g", "b_gate", "conv_w", "conv_b", "lru_lambda", "lru_wa", "lru_ba", "lru_wx", "lru_bx", "attn_sink",
                   "norm_ffn_g", "norm_final_g"]
    sharded = {"conv_w", "lru_lambda", "lru_ba", "lru_bx"}
    small_w = [norm_mix_g, b_gate, conv_w, conv_b, lru_lambda, lru_wa, lru_ba, lru_wx, lru_bx, attn_sink, norm_ffn_g, norm_final_g]
    small_m = [m_norm_mix_g, m_b_gate, m_conv_w, m_conv_b, m_lru_lambda, m_lru_wa, m_lru_ba, m_lru_wx, m_lru_bx, m_attn_sink,
               m_norm_ffn_g, m_norm_final_g]
    small_v = [v_norm_mix_g, v_b_gate, v_conv_w, v_conv_b, v_lru_lambda, v_lru_wa, v_lru_ba, v_lru_wx, v_lru_bx, v_attn_sink,
               v_norm_ffn_g, v_norm_final_g]
    g_local = []
    for nm, g, w in zip(small_names, g_full, small_w):
        if nm in sharded:
            g = lax.dynamic_slice_in_dim(g, chip * SW, SW, axis=1)
        g_local.append(g.reshape(w.shape))
    local_shapes = [w.shape for w in small_w]
    rows_local = _rows_for([math.prod(s) for s in local_shapes], 8)
    d_s, m_s, v_s = _adamw("adamw_small", _pack(small_w, rows_local), _pack(g_local, rows_local),
                           _pack(small_m, rows_local), _pack(small_v, rows_local), rows_local)
    d_l, m_l, v_l = _unpack(d_s, local_shapes), _unpack(m_s, local_shapes), _unpack(v_s, local_shapes)
    res = {nm: (g_local[i], d_l[i], m_l[i], v_l[i]) for i, nm in enumerate(small_names)}
    res.update(out_big)

    order = ["norm_mix_g", "w_in", "b_gate", "conv_w", "conv_b", "lru_lambda", "lru_wa", "lru_ba", "lru_wx", "lru_bx", "attn_sink",
             "w_out", "norm_ffn_g", "w_ffn_in", "w_ffn_out", "norm_final_g"]
    outs = [g_full[-1][0], grad_x[None]]
    for k in range(4):
        outs += [res[nm][k] for nm in order]
    return tuple(outs)
```

```python
import functools
import math

import jax
import jax.numpy as jnp
from jax import lax
from jax.experimental import pallas as pl
from jax.experimental.pallas import tpu as pltpu

f32 = jnp.float32
bf16 = jnp.bfloat16

D = 1024
NH = 16
HD = 64
FF = 2816
INW = 5632
NCHIP = 4
SHW = INW // NCHIP
CW = 128
NCH = D // CW
BLK = 128
EPS = 1e-6
NEG_INF = -1e30
RGLRU_C = 8.0
ADAM_LR, ADAM_B1, ADAM_B2, ADAM_EPS, ADAM_WD, ADAM_STEP = 0.001, 0.9, 0.999, 1e-08, 0.01, 10
VMEM_LIMIT = 58 * 1024 * 1024
MESH = pl.DeviceIdType.MESH
ANY = pl.BlockSpec(memory_space=pl.ANY)

COL_U, COL_G, COL_Q, COL_K, COL_V, COL_Z0, COL_Z1 = 0, 4, 8, 12, 13, 14, 18
MERGE_W = 512
MERGE_Z0, MERGE_Z1 = (COL_Z0 * 256) // MERGE_W, (COL_Z1 * 256) // MERGE_W


def _params(n_axes, vmem=False):
    return pltpu.CompilerParams(dimension_semantics=("arbitrary",) * n_axes,
                                vmem_limit_bytes=VMEM_LIMIT if vmem else None)


def _sds(shape, dtype):
    return jax.ShapeDtypeStruct(tuple(shape), dtype)


_DIMS = {"nn": (((1,), (0,)), ((), ())), "nt": (((1,), (1,)), ((), ())), "tn": (((0,), (0,)), ((), ()))}


def _mm(name, mode, a, a_spec, b, b_spec, out_shape, out_spec, grid, nk, acc_shape, add=None, add_spec=None):
    has_add = add is not None

    def body(*refs):
        a_ref, b_ref = refs[0], refs[1]
        add_ref = refs[2] if has_add else None
        o_ref = refs[2 + has_add]
        part = lax.dot_general(a_ref[...].astype(bf16), b_ref[...].astype(bf16), _DIMS[mode],
                               preferred_element_type=f32)
        if nk == 1:
            if has_add:
                part = add_ref[...] + part
            o_ref[...] = part.astype(o_ref.dtype)
            return
        acc_ref = refs[3 + has_add]
        k = pl.program_id(len(grid) - 1)

        @pl.when(k == 0)
        def _():
            acc_ref[...] = part

        @pl.when(k > 0)
        def _():
            acc_ref[...] += part

        @pl.when(k == nk - 1)
        def _():
            res = acc_ref[...]
            if has_add:
                res = add_ref[...] + res
            o_ref[...] = res.astype(o_ref.dtype)

    ins = [a, b] + ([add] if has_add else [])
    in_specs = [a_spec, b_spec] + ([add_spec] if has_add else [])
    scratch = [pltpu.VMEM(acc_shape, f32)] if nk > 1 else []
    return pl.pallas_call(body, name=name, grid=grid, in_specs=in_specs, out_specs=out_spec, out_shape=out_shape,
                          scratch_shapes=scratch, compiler_params=_params(len(grid), True))(*ins)


def _rms_matmul(name, x, g, w3, tm):
    S, K = x.shape
    G, _, Nw = w3.shape
    tm = min(tm, S)

    def body(x_ref, g_ref, w_ref, xn_ref, o_ref, xs_ref):
        @pl.when(pl.program_id(1) == 0)
        def _():
            xf = x_ref[...]
            r = lax.rsqrt(jnp.mean(xf * xf, axis=-1, keepdims=True) + EPS)
            xn = ((xf * r) * g_ref[...]).astype(bf16)
            xs_ref[...] = xn
            xn_ref[...] = xn

        o_ref[...] = jnp.dot(xs_ref[...], w_ref[...], preferred_element_type=f32).astype(bf16)

    return pl.pallas_call(
        body, name=name, grid=(S // tm, G),
        in_specs=[pl.BlockSpec((tm, K), lambda i, j: (i, 0)), pl.BlockSpec((1, K), lambda i, j: (0, 0)),
                  pl.BlockSpec((None, K, Nw), lambda i, j: (j, 0, 0))],
        out_specs=[pl.BlockSpec((tm, K), lambda i, j: (i, 0)), pl.BlockSpec((tm, Nw), lambda i, j: (i, j))],
        out_shape=[_sds((S, K), bf16), _sds((S, G * Nw), bf16)],
        scratch_shapes=[pltpu.VMEM((tm, K), bf16)], compiler_params=_params(2, True))(x, g, w3)


def _rms_matmul_swiglu(name, x, g, w3, tm):
    S, K = x.shape
    G, _, Nw = w3.shape
    tm = min(tm, S)
    half = G // 2

    def body(x_ref, g_ref, wg_ref, wu_ref, xn_ref, gu_ref, act_ref, xs_ref):
        @pl.when(pl.program_id(1) == 0)
        def _():
            xf = x_ref[...]
            r = lax.rsqrt(jnp.mean(xf * xf, axis=-1, keepdims=True) + EPS)
            xn = ((xf * r) * g_ref[...]).astype(bf16)
            xs_ref[...] = xn
            xn_ref[...] = xn

        xn = xs_ref[...]
        gate = jnp.dot(xn, wg_ref[...], preferred_element_type=f32)
        up = jnp.dot(xn, wu_ref[...], preferred_element_type=f32)
        gu_ref[0] = gate.astype(bf16)
        gu_ref[1] = up.astype(bf16)
        act_ref[...] = ((gate * _sigmoid(gate)) * up).astype(bf16)

    return pl.pallas_call(
        body, name=name, grid=(S // tm, half),
        in_specs=[pl.BlockSpec((tm, K), lambda i, j: (i, 0)), pl.BlockSpec((1, K), lambda i, j: (0, 0)),
                  pl.BlockSpec((None, K, Nw), lambda i, j: (j, 0, 0)),
                  pl.BlockSpec((None, K, Nw), lambda i, j: (half + j, 0, 0))],
        out_specs=[pl.BlockSpec((tm, K), lambda i, j: (i, 0)), pl.BlockSpec((2, tm, Nw), lambda i, j: (0, i, j)),
                   pl.BlockSpec((tm, Nw), lambda i, j: (i, j))],
        out_shape=[_sds((S, K), bf16), _sds((2, S, half * Nw), bf16), _sds((S, half * Nw), bf16)],
        scratch_shapes=[pltpu.VMEM((tm, K), bf16)], compiler_params=_params(2, True))(x, g, w3, w3)


def _mm_residual(name, a, w, res, tm):
    S, K = a.shape
    N = w.shape[1]
    tm = min(tm, S)
    return _mm(name, "nn", a, pl.BlockSpec((tm, K), lambda i: (i, 0)), w, pl.BlockSpec((K, N), lambda i: (0, 0)),
               _sds((S, N), f32), pl.BlockSpec((tm, N), lambda i: (i, 0)), (S // tm,), 1, None,
               add=res, add_spec=pl.BlockSpec((tm, N), lambda i: (i, 0)))


def _mm_nt_resident(name, a, w, tm):
    S, K = a.shape
    N = w.shape[0]
    tm = min(tm, S)
    return _mm(name, "nt", a, pl.BlockSpec((tm, K), lambda i: (i, 0)), w, pl.BlockSpec((N, K), lambda i: (0, 0)),
               _sds((S, N), f32), pl.BlockSpec((tm, N), lambda i: (i, 0)), (S // tm,), 1, None)


def _mm_nt_groups(name, a, a_spec, w3, S, tm):
    G, Dout, Kw = w3.shape
    return _mm(name, "nt", a, a_spec, w3, pl.BlockSpec((None, Dout, Kw), lambda i, g: (g, 0, 0)),
               _sds((S, Dout), f32), pl.BlockSpec((tm, Dout), lambda i, g: (i, 0)), (S // tm, G), G, (tm, Dout))


def _mm_tn(name, a, a_spec, b, b_spec, out_shape, out_spec, grid, acc_shape):
    return _mm(name, "tn", a, a_spec, b, b_spec, out_shape, out_spec, grid, grid[-1], acc_shape)


def _sigmoid(x):
    return 0.5 * jnp.tanh(0.5 * x) + 0.5


_GELU_C = math.sqrt(2.0 / math.pi)


def _gelu_and_grad(x):
    v = _GELU_C * (x + 0.044715 * (x * x * x))
    t = jnp.tanh(v)
    gl = 0.5 * x * (1.0 + t)
    dgl = 0.5 * (1.0 + t) + 0.5 * x * (1.0 - t * t) * (_GELU_C * (1.0 + 3.0 * 0.044715 * (x * x)))
    return gl, dgl


def _one_minus_exp2x(x, ex):
    y = 2.0 * x
    series = y * (1.0 + y * (0.5 + y * (1.0 / 6.0 + y * (1.0 / 24.0))))
    return jnp.where(y > -1.0 / 64.0, -series, 1.0 - ex * ex)


def _merge_fwd(proj, b_gate, y_a, y_b, tm):
    S = proj.shape[0]
    tm = min(tm, S)

    def body(z0_ref, z1_ref, b0_ref, b1_ref, ya_ref, yb_ref, o_ref):
        g0 = _sigmoid(z0_ref[...].astype(f32) + b0_ref[...])
        g1 = _sigmoid(z1_ref[...].astype(f32) + b1_ref[...])
        o_ref[...] = (g0 * ya_ref[...].astype(f32) + g1 * yb_ref[...].astype(f32)).astype(bf16)

    blk = lambda off: pl.BlockSpec((tm, MERGE_W), lambda j, i: (i, off + j))
    vec = lambda off: pl.BlockSpec((1, MERGE_W), lambda j, i: (0, off + j))
    return pl.pallas_call(body, name="merge_fwd", grid=(D // MERGE_W, S // tm),
                          in_specs=[blk(MERGE_Z0), blk(MERGE_Z1), vec(0), vec(D // MERGE_W), blk(0), blk(0)],
                          out_specs=blk(0), out_shape=_sds((S, D), bf16),
                          compiler_params=_params(2))(proj, proj, b_gate, b_gate, y_a, y_b)


def _merge_bwd(proj, b_gate, y_a, y_b, dm, tm):
    S = proj.shape[0]
    tm = min(tm, S)

    def body(z0_ref, z1_ref, b0_ref, b1_ref, ya_ref, yb_ref, dm_ref, dz0_ref, dz1_ref, dya_ref, dyb_ref, db0_ref, db1_ref):
        g0 = _sigmoid(z0_ref[...].astype(f32) + b0_ref[...])
        g1 = _sigmoid(z1_ref[...].astype(f32) + b1_ref[...])
        d = dm_ref[...]
        dz0 = (d * ya_ref[...].astype(f32)) * (g0 * (1.0 - g0))
        dz1 = (d * yb_ref[...].astype(f32)) * (g1 * (1.0 - g1))
        dz0_ref[...] = dz0.astype(bf16)
        dz1_ref[...] = dz1.astype(bf16)
        dya_ref[...] = (d * g0).astype(bf16)
        dyb_ref[...] = (d * g1).astype(bf16)

        @pl.when(pl.program_id(1) == 0)
        def _():
            db0_ref[...] = jnp.zeros_like(db0_ref)
            db1_ref[...] = jnp.zeros_like(db1_ref)

        db0_ref[...] += jnp.sum(dz0, axis=0, keepdims=True)
        db1_ref[...] += jnp.sum(dz1, axis=0, keepdims=True)

    blk = lambda off: pl.BlockSpec((tm, MERGE_W), lambda j, i: (i, off + j))
    vec = lambda off: pl.BlockSpec((1, MERGE_W), lambda j, i: (0, off + j))
    return pl.pallas_call(
        body, name="merge_bwd", grid=(D // MERGE_W, S // tm),
        in_specs=[blk(MERGE_Z0), blk(MERGE_Z1), vec(0), vec(D // MERGE_W), blk(0), blk(0), blk(0)],
        out_specs=[blk(0), blk(0), blk(0), blk(0), vec(0), vec(0)],
        out_shape=[_sds((S, D), bf16), _sds((S, D), bf16), _sds((S, D), bf16), _sds((S, D), bf16),
                   _sds((1, D), f32), _sds((1, D), f32)],
        compiler_params=_params(2))(proj, proj, b_gate, b_gate, y_a, y_b, dm)


def _swiglu_bwd(dx, w, gu, tm):
    S, K = dx.shape
    tm = min(tm, S)

    def body(dx_ref, w_ref, gu_ref, o_ref):
        d = lax.dot_general(dx_ref[...].astype(bf16), w_ref[...], _DIMS["nt"], preferred_element_type=f32)
        g = gu_ref[0].astype(f32)
        u = gu_ref[1].astype(f32)
        s = _sigmoid(g)
        o_ref[0] = ((d * u) * (s * (1.0 + g * (1.0 - s)))).astype(bf16)
        o_ref[1] = (d * (g * s)).astype(bf16)

    stacked = pl.BlockSpec((2, tm, FF), lambda i: (0, i, 0))
    return pl.pallas_call(body, name="swiglu_bwd", grid=(S // tm,),
                          in_specs=[pl.BlockSpec((tm, K), lambda i: (i, 0)), pl.BlockSpec((FF, K), lambda i: (0, 0)), stacked],
                          out_specs=stacked, out_shape=_sds((2, S, FF), bf16),
                          compiler_params=_params(1, True))(dx, w, gu)


def _final_loss_bwd(x2, g3, tgt, tm):
    S = x2.shape[0]
    tm = min(tm, S)

    def body(x_ref, g_ref, t_ref, dx_ref, loss_ref, dg_ref):
        @pl.when(pl.program_id(0) == 0)
        def _():
            loss_ref[...] = jnp.zeros_like(loss_ref)
            dg_ref[...] = jnp.zeros_like(dg_ref)

        x = x_ref[...]
        g = g_ref[...]
        r = lax.rsqrt(jnp.mean(x * x, axis=-1, keepdims=True) + EPS)
        xh = x * r
        err = xh * g - t_ref[...]
        row = jnp.mean(err * err, axis=-1, keepdims=True)
        loss_ref[...] += 0.5 * jnp.sum(row, axis=0, keepdims=True)
        dy = err * (1.0 / D)
        dg_ref[...] += jnp.sum(dy * xh, axis=0, keepdims=True)
        dxh = dy * g
        dx_ref[...] = r * (dxh - xh * jnp.mean(dxh * xh, axis=-1, keepdims=True))

    row_blk = pl.BlockSpec((tm, D), lambda i: (i, 0))
    vec = pl.BlockSpec((1, D), lambda i: (0, 0))
    return pl.pallas_call(body, name="final_loss_bwd", grid=(S // tm,), in_specs=[row_blk, vec, row_blk],
                          out_specs=[row_blk, pl.BlockSpec((1, 128), lambda i: (0, 0)), vec],
                          out_shape=[_sds((S, D), f32), _sds((1, 128), f32), _sds((1, D), f32)],
                          compiler_params=_params(1))(x2, g3, tgt)


def _rms_bwd(name, x, g, dxn, dres, tm):
    S = x.shape[0]
    tm = min(tm, S)

    def body(x_ref, g_ref, d_ref, r_ref, dx_ref, dg_ref):
        @pl.when(pl.program_id(0) == 0)
        def _():
            dg_ref[...] = jnp.zeros_like(dg_ref)

        x = x_ref[...]
        d = d_ref[...]
        r = lax.rsqrt(jnp.mean(x * x, axis=-1, keepdims=True) + EPS)
        xh = x * r
        dg_ref[...] += jnp.sum(d * xh, axis=0, keepdims=True)
        dxh = d * g_ref[...]
        dx_ref[...] = r_ref[...] + r * (dxh - xh * jnp.mean(dxh * xh, axis=-1, keepdims=True))

    row_blk = pl.BlockSpec((tm, D), lambda i: (i, 0))
    vec = pl.BlockSpec((1, D), lambda i: (0, 0))
    return pl.pallas_call(body, name=name, grid=(S // tm,), in_specs=[row_blk, vec, row_blk, row_blk],
                          out_specs=[row_blk, vec], out_shape=[_sds((S, D), f32), _sds((1, D), f32)],
                          compiler_params=_params(1))(x, g, dxn, dres)


LRU_TT = 256
SCAN_UNROLL = 4


HALO = 16


def _halo(ref, i, S):
    nt = S // LRU_TT
    t0 = pl.multiple_of(i * LRU_TT, LRU_TT)
    p0 = pl.multiple_of(jnp.maximum(t0 - HALO, 0), HALO)
    n0 = pl.multiple_of(jnp.minimum(t0 + LRU_TT, S - HALO), HALO)
    prev = jnp.where(i > 0, ref[pl.ds(p0, HALO), :].astype(f32), 0.0)
    nxt = jnp.where(i < nt - 1, ref[pl.ds(n0, HALO), :].astype(f32), 0.0)
    return jnp.concatenate([prev, ref[pl.ds(t0, LRU_TT), :].astype(f32), nxt], axis=0)


def _shift(ext, k):
    n = LRU_TT + 2 * HALO
    return pltpu.roll(ext, (-k) % n, 0)[HALO:HALO + LRU_TT]


def _lru_gates(uc, wbd, ba, bx):
    pre = jnp.dot(uc.astype(bf16), wbd, preferred_element_type=f32)
    r_f = _sigmoid(pre[:, 0:CW] + ba[0:1])
    i_f = _sigmoid(pre[:, CW:2 * CW] + bx[0:1])
    r_b = _sigmoid(pre[:, 2 * CW:3 * CW] + ba[1:2])
    i_b = _sigmoid(pre[:, 3 * CW:4 * CW] + bx[1:2])
    return r_f, i_f, r_b, i_b


def _lru_coeffs(r, sp):
    log_a = (-RGLRU_C * r) * sp
    a = jnp.exp(log_a)
    beta = jnp.sqrt(jnp.maximum(_one_minus_exp2x(log_a, a), 0.0))
    return a, beta


def _lru_coeffs_inv(r, sp):
    log_a = (-RGLRU_C * r) * sp
    a = jnp.exp(log_a)
    om = jnp.maximum(_one_minus_exp2x(log_a, a), 0.0)
    return a, jnp.sqrt(om), lax.rsqrt(om)


def _conv_tile(u_ref, i, S, cw, cb):
    ext = _halo(u_ref, i, S)
    um2, um1, u0, up1 = _shift(ext, -2), _shift(ext, -1), ext[HALO:HALO + LRU_TT], _shift(ext, 1)
    uc = um2 * cw[0:1] + um1 * cw[1:2] + u0 * cw[2:3] + up1 * cw[3:4] + cb
    return uc, (um2, um1, u0, up1)


def _scan_pair(S, fwd_a, fwd_b, fwd_out, rev_a, rev_b, rev_out):
    ng = S // 8
    idx = lax.broadcasted_iota(jnp.int32, (8, CW), 0)

    def local(a, b, rev):
        for sh in (1, 2, 4):
            if rev:
                keep = idx < 8 - sh
                amt = 8 - sh
            else:
                keep = idx >= sh
                amt = sh
            a_s = jnp.where(keep, pltpu.roll(a, amt, 0), 1.0)
            b_s = jnp.where(keep, pltpu.roll(b, amt, 0), 0.0)
            b = a * b_s + b
            a = a * a_s
        return a, b

    def step(it, carry):
        cf, cr = carry
        fwd_rows = [pl.multiple_of((it * SCAN_UNROLL + j) * 8, 8) for j in range(SCAN_UNROLL)]
        rev_rows = [pl.multiple_of((ng - 1 - (it * SCAN_UNROLL + j)) * 8, 8) for j in range(SCAN_UNROLL)]
        fwd_loc = [local(fwd_a(r), fwd_b(r), False) for r in fwd_rows]
        rev_loc = [local(rev_a(r), rev_b(r), True) for r in rev_rows]
        for j in range(SCAN_UNROLL):
            a, b = fwd_loc[j]
            h = a * cf + b
            fwd_out[pl.ds(fwd_rows[j], 8), :] = h
            cf = jnp.broadcast_to(h[7:8, :], (8, CW))
            a, b = rev_loc[j]
            h = a * cr + b
            rev_out[pl.ds(rev_rows[j], 8), :] = h
            cr = jnp.broadcast_to(h[0:1, :], (8, CW))
        return cf, cr

    zero = jnp.zeros((8, CW), f32)
    lax.fori_loop(0, ng // SCAN_UNROLL, step, (zero, zero))


def _lru_specs(S):
    seq = lambda off: pl.BlockSpec((S, CW), lambda j: (0, off + j))
    par = lambda rows: pl.BlockSpec((rows, CW), lambda j: (0, j))
    return seq, par


def _lru_fwd(proj, conv_w, conv_b, lam, ba, bx, wbd):
    S = proj.shape[0]
    nt = S // LRU_TT

    def body(u_ref, g_ref, cw_ref, cb_ref, lam_ref, ba_ref, bx_ref, wbd_ref, y_ref, state_ref, af_ref, bf_ref, ab_ref, bb_ref,
             sems):
        cw, cb, ba_v, bx_v, wbd_v = cw_ref[...], cb_ref[...], ba_ref[...], bx_ref[...], wbd_ref[...]
        sp = jax.nn.softplus(-lam_ref[...])
        cols = pl.ds(pl.multiple_of(pl.program_id(0) * CW, CW), CW)
        save = [pltpu.make_async_copy(ref, state_ref.at[k, :, cols], sems.at[k])
                for k, ref in enumerate((af_ref, bf_ref, ab_ref, bb_ref))]

        def phase1(i, c):
            uc, _ = _conv_tile(u_ref, i, S, cw, cb)
            r_f, i_f, r_b, i_b = _lru_gates(uc, wbd_v, ba_v, bx_v)
            rows = pl.ds(pl.multiple_of(i * LRU_TT, LRU_TT), LRU_TT)
            a, beta = _lru_coeffs(r_f, sp[0:1])
            af_ref[rows, :] = a
            bf_ref[rows, :] = beta * (i_f * uc)
            a, beta = _lru_coeffs(r_b, sp[1:2])
            ab_ref[rows, :] = a
            bb_ref[rows, :] = beta * (i_b * uc)
            return c

        lax.fori_loop(0, nt, phase1, 0)
        row8 = lambda ref: (lambda r0: ref[pl.ds(r0, 8), :])
        _scan_pair(S, row8(af_ref), row8(bf_ref), bf_ref, row8(ab_ref), row8(bb_ref), bb_ref)
        for cp in save:
            cp.start()

        def phase3(i, c):
            rows = pl.ds(pl.multiple_of(i * LRU_TT, LRU_TT), LRU_TT)
            y = (bf_ref[rows, :] + bb_ref[rows, :]) * jax.nn.gelu(g_ref[rows, :].astype(f32))
            y_ref[rows, :] = y.astype(y_ref.dtype)
            return c

        lax.fori_loop(0, nt, phase3, 0)
        for cp in save:
            cp.wait()

    seq, par = _lru_specs(S)
    return pl.pallas_call(
        body, name="lru_fwd", grid=(NCH,),
        in_specs=[seq(0), seq(NCH), par(4), par(1), par(2), par(2), par(2),
                  pl.BlockSpec((None, CW, 4 * CW), lambda j: (j, 0, 0))],
        out_specs=[seq(0), ANY], out_shape=[_sds((S, D), bf16), _sds((4, S, D), f32)],
        scratch_shapes=[pltpu.VMEM((S, CW), f32)] * 4 + [pltpu.SemaphoreType.DMA((4,))], compiler_params=_params(1, True),
    )(proj, proj, conv_w, conv_b, lam, ba, bx, wbd)


def _lru_bwd(proj, dy, state, conv_w, conv_b, lam, ba, bx, wbd):
    S = proj.shape[0]
    nt = S // LRU_TT

    def body(u_ref, g_ref, dy_ref, state_ref, cw_ref, cb_ref, lam_ref, ba_ref, bx_ref, wbd_ref,
             du_ref, dg_ref, dcw_ref, dcb_ref, dlam_ref, dba_ref, dbx_ref, dwbd_ref,
             af_ref, bf_ref, ab_ref, bb_ref, dh_ref, sems):
        cw, cb, ba_v, bx_v, wbd_v = cw_ref[...], cb_ref[...], ba_ref[...], bx_ref[...], wbd_ref[...]
        lam_v = lam_ref[...]
        sp = jax.nn.softplus(-lam_v)
        cols = pl.ds(pl.multiple_of(pl.program_id(0) * CW, CW), CW)
        load = [pltpu.make_async_copy(state_ref.at[k, :, cols], ref, sems.at[k])
                for k, ref in enumerate((af_ref, bf_ref, ab_ref, bb_ref))]
        for cp in load:
            cp.start()
        for cp in load:
            cp.wait()
        row8 = lambda ref: (lambda r0: ref[pl.ds(r0, 8), :])

        def phase0(i, c):
            rows = pl.ds(pl.multiple_of(i * LRU_TT, LRU_TT), LRU_TT)
            gl, dgl = _gelu_and_grad(g_ref[rows, :].astype(f32))
            dyt = dy_ref[rows, :].astype(f32)
            dh_ref[rows, :] = dyt * gl
            dg_ref[rows, :] = ((dyt * (bf_ref[rows, :] + bb_ref[rows, :])) * dgl).astype(dg_ref.dtype)
            return c

        lax.fori_loop(0, nt, phase0, 0)

        def scaled_dh(a_ref):
            def f(r0):
                return a_ref[pl.ds(r0, 8), :] * dh_ref[pl.ds(r0, 8), :]
            return f

        _scan_pair(S, row8(ab_ref), scaled_dh(ab_ref), ab_ref, row8(af_ref), scaled_dh(af_ref), af_ref)

        dcw_ref[...] = jnp.zeros_like(dcw_ref)
        dcb_ref[...] = jnp.zeros_like(dcb_ref)
        dlam_ref[...] = jnp.zeros_like(dlam_ref)
        dba_ref[...] = jnp.zeros_like(dba_ref)
        dbx_ref[...] = jnp.zeros_like(dbx_ref)
        dwbd_ref[...] = jnp.zeros_like(dwbd_ref)

        def direction(uc, r, i_g, dht, h_nb, sp_d):
            a, beta, inv_beta = _lru_coeffs_inv(r, sp_d)
            da = dht * h_nb
            dbeta = dht * (i_g * uc)
            d_iu = dht * beta
            dlog_a = da * a - (a * a) * (dbeta * inv_beta)
            dlr = dlog_a * r
            dsp = -RGLRU_C * jnp.sum(dlr, axis=0, keepdims=True)
            dpre_r = (dlr * (1.0 - r)) * (-RGLRU_C * sp_d)
            dpre_i = (d_iu * uc) * (i_g * (1.0 - i_g))
            return dpre_r, dpre_i, d_iu * i_g, dsp

        def phase4(i, c):
            uc, (um2, um1, u0, up1) = _conv_tile(u_ref, i, S, cw, cb)
            r_f, i_f, r_b, i_b = _lru_gates(uc, wbd_v, ba_v, bx_v)
            rows = pl.ds(pl.multiple_of(i * LRU_TT, LRU_TT), LRU_TT)
            dh = dh_ref[rows, :]
            dht_f = dh + _shift(_halo(af_ref, i, S), 1)
            h_prev = _shift(_halo(bf_ref, i, S), -1)
            dht_b = dh + _shift(_halo(ab_ref, i, S), -1)
            h_next = _shift(_halo(bb_ref, i, S), 1)
            prf, pif, duc_f, dsp_f = direction(uc, r_f, i_f, dht_f, h_prev, sp[0:1])
            prb, pib, duc_b, dsp_b = direction(uc, r_b, i_b, dht_b, h_next, sp[1:2])
            dpre = jnp.concatenate([prf, pif, prb, pib], axis=1)
            dpre_b = dpre.astype(bf16)
            duc = (duc_f + duc_b) + lax.dot_general(dpre_b, wbd_v, _DIMS["nt"], preferred_element_type=f32)
            dwbd_ref[...] += lax.dot_general(uc.astype(bf16), dpre_b, _DIMS["tn"], preferred_element_type=f32)
            colsum = lambda v: jnp.sum(v, axis=0, keepdims=True)
            dba_ref[...] += jnp.concatenate([colsum(prf), colsum(prb)], axis=0)
            dbx_ref[...] += jnp.concatenate([colsum(pif), colsum(pib)], axis=0)
            dlam_ref[...] += jnp.concatenate([dsp_f, dsp_b], axis=0)
            dcb_ref[...] += colsum(duc)
            dcw_ref[...] += jnp.concatenate([colsum(duc * um2), colsum(duc * um1), colsum(duc * u0),
                                             colsum(duc * up1)], axis=0)
            af_ref[rows, :] = duc
            return c

        lax.fori_loop(0, nt, phase4, 0)
        dlam_ref[...] = dlam_ref[...] * (-_sigmoid(-lam_v))

        def phase5(i, c):
            ext = _halo(af_ref, i, S)
            rows = pl.ds(pl.multiple_of(i * LRU_TT, LRU_TT), LRU_TT)
            du = (_shift(ext, 2) * cw[0:1] + _shift(ext, 1) * cw[1:2] + ext[HALO:HALO + LRU_TT] * cw[2:3]
                  + _shift(ext, -1) * cw[3:4])
            du_ref[rows, :] = du.astype(du_ref.dtype)
            return c

        lax.fori_loop(0, nt, phase5, 0)

    seq, par = _lru_specs(S)
    return pl.pallas_call(
        body, name="lru_bwd", grid=(NCH,),
        in_specs=[seq(0), seq(NCH), seq(0), ANY, par(4), par(1), par(2), par(2), par(2),
                  pl.BlockSpec((None, CW, 4 * CW), lambda j: (j, 0, 0))],
        out_specs=[seq(0), seq(0), par(4), par(1), par(2), par(2), par(2),
                   pl.BlockSpec((None, CW, 4 * CW), lambda j: (j, 0, 0))],
        out_shape=[_sds((S, D), bf16), _sds((S, D), bf16), _sds((4, D), f32), _sds((1, D), f32), _sds((2, D), f32),
                   _sds((2, D), f32), _sds((2, D), f32), _sds((NCH, CW, 4 * CW), f32)],
        scratch_shapes=[pltpu.VMEM((S, CW), f32)] * 5 + [pltpu.SemaphoreType.DMA((4,))], compiler_params=_params(1, True),
    )(proj, proj, dy, state, conv_w, conv_b, lam, ba, bx, wbd)


_SLOPES = [2.0 ** (-8.0 * (h + 1) / NH) for h in range(NH)]


def _half_mask(shape, e):
    lane = lax.broadcasted_iota(jnp.int32, shape, 1)
    return (lane < HD) if e == 0 else (lane >= HD)


def _both_halves(x, src):
    return jnp.where(_half_mask(x.shape, src), x, pltpu.roll(x, HD, 1))


def _fold_halves(x, dst):
    return jnp.where(_half_mask(x.shape, dst), x + pltpu.roll(x, HD, 1), 0.0)


def _attn_base(n, S):
    tq = lax.broadcasted_iota(jnp.int32, (BLK, 3 * BLK), 0)
    sk = lax.broadcasted_iota(jnp.int32, (BLK, 3 * BLK), 1)
    dist = jnp.abs(tq + BLK - sk)
    kpos = n * BLK - BLK + sk
    valid = (dist <= BLK) & (kpos >= 0) & (kpos < S)
    return jnp.where(valid, -dist.astype(f32), NEG_INF)


def _group_heads(ref, kvh, scale):
    parts = []
    for i in range(4):
        pair = 2 * kvh + i // 2
        x = ref[:, pair * 128:(pair + 1) * 128].astype(f32)
        parts.append(jnp.where(_half_mask(x.shape, i % 2), x * scale, 0.0))
    return parts


def _stack_bf16(parts):
    return jnp.concatenate([p.astype(bf16) for p in parts], axis=0)


def _attn_softmax(s_raw, base, slope, sink):
    s = s_raw + slope * base
    m = jnp.maximum(jnp.max(s, axis=-1, keepdims=True), sink)
    p = jnp.exp(s - m)
    esink = jnp.exp(sink - m)
    inv = 1.0 / (jnp.sum(p, axis=-1, keepdims=True) + esink)
    return p, inv, esink * inv


def _attn_specs(S):
    nb = S // BLK
    q_spec = pl.BlockSpec((BLK, D), lambda n: (n, 2))
    kv = lambda col: [pl.BlockSpec((BLK, 256), lambda n: (jnp.maximum(n - 1, 0), col)),
                      pl.BlockSpec((BLK, 256), lambda n: (n, col)),
                      pl.BlockSpec((BLK, 256), lambda n: (jnp.minimum(n + 1, nb - 1), col))]
    return nb, q_spec, kv(COL_K), kv(COL_V)


def _attn_fwd(proj, sink):
    S = proj.shape[0]
    nb, q_spec, k_specs, v_specs = _attn_specs(S)

    def body(sink_ref, q_ref, kp_ref, kc_ref, kn_ref, vp_ref, vc_ref, vn_ref, o_ref):
        base = _attn_base(pl.program_id(0), S)
        kcat = jnp.concatenate([kp_ref[...], kc_ref[...], kn_ref[...]], axis=0).astype(f32)
        vcat = jnp.concatenate([vp_ref[...], vc_ref[...], vn_ref[...]], axis=0).astype(f32)
        even = _half_mask((BLK, 128), 0)
        for kvh in range(NH // 4):
            ch, off = kvh // 2, kvh % 2
            kb = _both_halves(kcat[:, ch * 128:(ch + 1) * 128], off).astype(bf16)
            vb = _both_halves(vcat[:, ch * 128:(ch + 1) * 128], off).astype(bf16)
            q4 = _stack_bf16(_group_heads(q_ref, kvh, HD ** -0.5))
            s4 = lax.dot_general(q4, kb, _DIMS["nt"], preferred_element_type=f32)
            ps, invs = [], []
            for i in range(4):
                h = 4 * kvh + i
                p, inv, _ = _attn_softmax(s4[i * BLK:(i + 1) * BLK], base, _SLOPES[h], sink_ref[0, h])
                ps.append(p)
                invs.append(inv)
            o4 = jnp.dot(_stack_bf16(ps), vb, preferred_element_type=f32)
            for pr in range(2):
                lo = o4[(2 * pr) * BLK:(2 * pr + 1) * BLK] * invs[2 * pr]
                hi = o4[(2 * pr + 1) * BLK:(2 * pr + 2) * BLK] * invs[2 * pr + 1]
                pair = 2 * kvh + pr
                o_ref[:, pair * 128:(pair + 1) * 128] = jnp.where(even, lo, hi).astype(o_ref.dtype)

    return pl.pallas_call(
        body, name="attn_fwd", grid=(nb,),
        in_specs=[pl.BlockSpec(memory_space=pltpu.SMEM), q_spec] + k_specs + v_specs,
        out_specs=pl.BlockSpec((BLK, D), lambda n: (n, 0)), out_shape=_sds((S, D), bf16),
        compiler_params=_params(1, True))(sink, proj, proj, proj, proj, proj, proj, proj)


def _attn_bwd(proj, sink, y_b, dy_b):
    S = proj.shape[0]
    nb, q_spec, k_specs, v_specs = _attn_specs(S)

    def body(sink_ref, q_ref, kp_ref, kc_ref, kn_ref, vp_ref, vc_ref, vn_ref, o_ref, do_ref,
             dq_ref, dk_ref, dv_ref, dsink_ref):
        n = pl.program_id(0)

        @pl.when(n == 0)
        def _():
            dk_ref[...] = jnp.zeros_like(dk_ref)
            dv_ref[...] = jnp.zeros_like(dv_ref)
            dsink_ref[...] = jnp.zeros_like(dsink_ref)

        base = _attn_base(n, S)
        kcat = jnp.concatenate([kp_ref[...], kc_ref[...], kn_ref[...]], axis=0).astype(f32)
        vcat = jnp.concatenate([vp_ref[...], vc_ref[...], vn_ref[...]], axis=0).astype(f32)
        dk_acc = [jnp.zeros((3 * BLK, 128), f32), jnp.zeros((3 * BLK, 128), f32)]
        dv_acc = [jnp.zeros((3 * BLK, 128), f32), jnp.zeros((3 * BLK, 128), f32)]
        scale = HD ** -0.5
        even = _half_mask((BLK, 128), 0)
        for kvh in range(NH // 4):
            ch, off = kvh // 2, kvh % 2
            kb = _both_halves(kcat[:, ch * 128:(ch + 1) * 128], off).astype(bf16)
            vb = _both_halves(vcat[:, ch * 128:(ch + 1) * 128], off).astype(bf16)
            q_parts = _group_heads(q_ref, kvh, scale)
            d_parts = _group_heads(do_ref, kvh, 1.0)
            s4 = lax.dot_general(_stack_bf16(q_parts), kb, _DIMS["nt"], preferred_element_type=f32)
            dp4 = lax.dot_general(_stack_bf16(d_parts), vb, _DIMS["nt"], preferred_element_type=f32)
            ts, ps, qn, dn, invs = [], [], [], [], []
            for i in range(4):
                h = 4 * kvh + i
                pair = 2 * kvh + i // 2
                rows = slice(i * BLK, (i + 1) * BLK)
                p, inv, psink = _attn_softmax(s4[rows], base, _SLOPES[h], sink_ref[0, h])
                delta = jnp.sum(d_parts[i] * o_ref[:, pair * 128:(pair + 1) * 128].astype(f32), axis=-1, keepdims=True)
                dsink_ref[h:h + 1, :] += jnp.broadcast_to(-jnp.sum(psink * delta, axis=0, keepdims=True), (1, 128))
                ts.append(p * (dp4[rows] - delta))
                ps.append(p)
                qn.append(q_parts[i] * inv)
                dn.append(d_parts[i] * inv)
                invs.append(inv)
            t4 = _stack_bf16(ts)
            dq4 = jnp.dot(t4, kb, preferred_element_type=f32)
            for pr in range(2):
                lo = dq4[(2 * pr) * BLK:(2 * pr + 1) * BLK] * invs[2 * pr]
                hi = dq4[(2 * pr + 1) * BLK:(2 * pr + 2) * BLK] * invs[2 * pr + 1]
                pair = 2 * kvh + pr
                dq_ref[:, pair * 128:(pair + 1) * 128] = (jnp.where(even, lo, hi) * scale).astype(dq_ref.dtype)
            dk_both = lax.dot_general(t4, _stack_bf16(qn), _DIMS["tn"], preferred_element_type=f32)
            dv_both = lax.dot_general(_stack_bf16(ps), _stack_bf16(dn), _DIMS["tn"], preferred_element_type=f32)
            dk_acc[ch] = dk_acc[ch] + _fold_halves(dk_both, off)
            dv_acc[ch] = dv_acc[ch] + _fold_halves(dv_both, off)
        for j in range(3):
            blk = n + (j - 1)

            @pl.when((blk >= 0) & (blk < nb))
            def _():
                rows = pl.ds(pl.multiple_of(blk * BLK, BLK), BLK)
                for ch in range(2):
                    dk_ref[rows, ch * 128:(ch + 1) * 128] += dk_acc[ch][j * BLK:(j + 1) * BLK]
                    dv_ref[rows, ch * 128:(ch + 1) * 128] += dv_acc[ch][j * BLK:(j + 1) * BLK]

    row_blk = pl.BlockSpec((BLK, D), lambda n: (n, 0))
    full = pl.BlockSpec((S, 256), lambda n: (0, 0))
    return pl.pallas_call(
        body, name="attn_bwd", grid=(nb,),
        in_specs=[pl.BlockSpec(memory_space=pltpu.SMEM), q_spec] + k_specs + v_specs + [row_blk, row_blk],
        out_specs=[row_blk, full, full, pl.BlockSpec((NH, 128), lambda n: (0, 0))],
        out_shape=[_sds((S, D), bf16), _sds((S, 256), f32), _sds((S, 256), f32), _sds((NH, 128), f32)],
        compiler_params=_params(1, True))(sink, proj, proj, proj, proj, proj, proj, proj, y_b, dy_b)


def _adamw(name, w, g, m, v, tr):
    R, C = w.shape
    tr = min(tr, R)

    def body(w_ref, g_ref, m_ref, v_ref, d_ref, m2_ref, v2_ref):
        g = g_ref[...]
        m2 = ADAM_B1 * m_ref[...] + (1.0 - ADAM_B1) * g
        v2 = ADAM_B2 * v_ref[...] + (1.0 - ADAM_B2) * (g * g)
        m_hat = m2 / (1.0 - ADAM_B1 ** ADAM_STEP)
        v_hat = v2 / (1.0 - ADAM_B2 ** ADAM_STEP)
        d_ref[...] = -ADAM_LR * (m_hat / (jnp.sqrt(v_hat) + ADAM_EPS) + ADAM_WD * w_ref[...])
        m2_ref[...] = m2
        v2_ref[...] = v2

    blk = pl.BlockSpec((tr, C), lambda i: (i, 0))
    return pl.pallas_call(body, name=name, grid=(R // tr,), in_specs=[blk] * 4, out_specs=[blk] * 3,
                          out_shape=[_sds((R, C), f32)] * 3, compiler_params=_params(1))(w, g, m, v)


def _pair_sum(name, c_arr, g4, recv, th):
    _, _, h, w = g4.shape
    th = min(th, h)

    def body(c_ref, g_ref, r_ref, o_ref, ob_ref):
        p = g_ref[...] + r_ref[...]
        o_ref[...] = p
        ob_ref[...] = p.astype(bf16)

    blk = pl.BlockSpec((None, th, w), lambda s, i, c_ref: (s, i, 0))
    spec = pltpu.PrefetchScalarGridSpec(
        num_scalar_prefetch=1, grid=(NCHIP, h // th),
        in_specs=[pl.BlockSpec((None, None, th, w), lambda s, i, c_ref: (s, c_ref[0], i, 0)), blk],
        out_specs=[blk, blk])
    return pl.pallas_call(body, name=name, grid_spec=spec,
                          out_shape=[_sds((NCHIP, h, w), f32), _sds((NCHIP, h, w), bf16)],
                          compiler_params=_params(2))(c_arr, g4, recv)


def _chip_sum(name, chip_arr, own4, recv3, th):
    _, h, w = own4.shape
    th = min(th, h)

    def body(s_ref, o_ref, r_ref, out_ref):
        out_ref[...] = ((o_ref[...] + r_ref[0].astype(f32)) + r_ref[1].astype(f32)) + r_ref[2].astype(f32)

    spec = pltpu.PrefetchScalarGridSpec(
        num_scalar_prefetch=1, grid=(h // th,),
        in_specs=[pl.BlockSpec((None, th, w), lambda i, s_ref: (s_ref[0], i, 0)),
                  pl.BlockSpec((3, th, w), lambda i, s_ref: (0, i, 0))],
        out_specs=pl.BlockSpec((th, w), lambda i, s_ref: (i, 0)))
    return pl.pallas_call(body, name=name, grid_spec=spec, out_shape=_sds((h, w), f32),
                          compiler_params=_params(1, True))(chip_arr, own4, recv3)


def _adamw_halves(name, c_arr, w, g_own, g_recv, m, v, th):
    h, wd = g_own.shape
    th = min(th, h)

    def body(c_ref, w_ref, go_ref, gr_ref, m_ref, v_ref, g_ref, d_ref, m2_ref, v2_ref):
        g = jnp.where(c_ref[0] == pl.program_id(0), go_ref[...], gr_ref[...])
        m2 = ADAM_B1 * m_ref[...] + (1.0 - ADAM_B1) * g
        v2 = ADAM_B2 * v_ref[...] + (1.0 - ADAM_B2) * (g * g)
        m_hat = m2 / (1.0 - ADAM_B1 ** ADAM_STEP)
        v_hat = v2 / (1.0 - ADAM_B2 ** ADAM_STEP)
        g_ref[...] = g
        d_ref[...] = -ADAM_LR * (m_hat / (jnp.sqrt(v_hat) + ADAM_EPS) + ADAM_WD * w_ref[...])
        m2_ref[...] = m2
        v2_ref[...] = v2

    nt = h // th
    full = pl.BlockSpec((th, wd), lambda hh, i, c_ref: (hh * nt + i, 0))
    half = pl.BlockSpec((th, wd), lambda hh, i, c_ref: (i, 0))
    spec = pltpu.PrefetchScalarGridSpec(num_scalar_prefetch=1, grid=(2, nt),
                                        in_specs=[full, half, half, full, full], out_specs=[full] * 4)
    return pl.pallas_call(body, name=name, grid_spec=spec, out_shape=[_sds((2 * h, wd), f32)] * 4,
                          compiler_params=_params(2))(c_arr, w, g_own, g_recv, m, v)


def _add2(name, a, b):
    def body(a_ref, b_ref, o_ref):
        o_ref[...] = a_ref[...] + b_ref[...]
    return pl.pallas_call(body, name=name, out_shape=_sds(a.shape, f32))(a, b)


def _sum4(name, b4, th):
    _, h, w = b4.shape
    th = min(th, h)

    def body(b_ref, o_ref):
        o_ref[...] = ((b_ref[0] + b_ref[1]) + b_ref[2]) + b_ref[3]

    return pl.pallas_call(body, name=name, grid=(h // th,),
                          in_specs=[pl.BlockSpec((NCHIP, th, w), lambda i: (0, i, 0))],
                          out_specs=pl.BlockSpec((th, w), lambda i: (i, 0)), out_shape=_sds((h, w), f32),
                          compiler_params=_params(1, True))(b4)


def _coords():
    x, y, c = lax.axis_index("x"), lax.axis_index("y"), lax.axis_index("c")
    return x, y, c, [(1 - x, y), (x, 1 - y), (1 - x, 1 - y)]


def _gather_chips(arrs):
    n = len(arrs)

    def body(*refs):
        ins, outs = refs[:n], refs[n:2 * n]
        send_sems, recv_sems, local_sems = refs[2 * n:2 * n + 3]
        stage = refs[2 * n + 3:]
        x, y, c, chips = _coords()
        s = 2 * x + y
        sib = (x, y, 1 - c)
        load = [pltpu.make_async_copy(ins[a], stage[a], local_sems.at[a]) for a in range(n)]
        local = [pltpu.make_async_copy(stage[a], outs[a].at[s], local_sems.at[n + a]) for a in range(n)]
        for cp in load:
            cp.start()

        def over_ici(k, a, slot, peer):
            return pltpu.make_async_remote_copy(src_ref=ins[a].at[c], dst_ref=outs[a].at[slot, c], send_sem=send_sems.at[k * n + a],
                                                recv_sem=recv_sems.at[k * n + a], device_id=peer, device_id_type=MESH)

        def to_sibling(k, a, slot, half):
            i = (3 + k) * n + a
            return pltpu.make_async_remote_copy(src_ref=outs[a].at[slot, half], dst_ref=outs[a].at[slot, half], send_sem=send_sems.at[i],
                                                recv_sem=recv_sems.at[i], device_id=sib, device_id_type=MESH)

        sends = [over_ici(k, a, s, (px, py, c)) for k, (px, py) in enumerate(chips) for a in range(n)]
        for cp in sends:
            cp.start()
        for a in range(n):
            load[a].wait()
            local[a].start()
        passed = []
        for k, (px, py) in enumerate(chips):
            for a in range(n):
                over_ici(k, a, 2 * px + py, (px, py, c)).wait_recv()
                cp = to_sibling(k, a, 2 * px + py, c)
                cp.start()
                passed.append(cp)
        for k, (px, py) in enumerate(chips):
            for a in range(n):
                to_sibling(k, a, 2 * px + py, 1 - c).wait_recv()
        for cp in sends + passed:
            cp.wait_send()
        for cp in local:
            cp.wait()

    return pl.pallas_call(
        body, name="gather_weights", in_specs=[ANY] * n, out_specs=[ANY] * n,
        out_shape=[_sds((NCHIP,) + a.shape, a.dtype) for a in arrs],
        scratch_shapes=[pltpu.SemaphoreType.DMA((6 * n,)), pltpu.SemaphoreType.DMA((6 * n,)), pltpu.SemaphoreType.DMA((2 * n,))]
        + [pltpu.VMEM(a.shape, a.dtype) for a in arrs],
        compiler_params=pltpu.CompilerParams(vmem_limit_bytes=VMEM_LIMIT),
    )(*arrs)


HBM = pl.BlockSpec(memory_space=pltpu.HBM)
SEM = pl.BlockSpec(memory_space=pltpu.SEMAPHORE)
EFFECT = pltpu.SideEffectType.DATAFLOW_SIDE_EFFECTING


def _split_start(name, n_copies, make_copies, ins, land_shapes, after):
    ni, nl = len(ins), len(land_shapes)

    def body(*refs):
        in_refs, land_refs = refs[:ni], refs[ni:ni + nl]
        send_sems, recv_sems = refs[ni + nl + 1], refs[ni + nl + 2]
        token = refs[-1]
        for cp in make_copies(in_refs, land_refs, send_sems, recv_sems):
            cp.start()
        token[...] = jnp.zeros_like(token)

    lands = [pltpu.with_memory_space_constraint(lax.empty(s.shape, s.dtype), pltpu.HBM) for s in land_shapes]
    res = pl.pallas_call(
        body, name=name,
        out_shape=(pltpu.SemaphoreType.DMA((n_copies,)), pltpu.SemaphoreType.DMA((n_copies,)),
                   *[pltpu.HBM(a.shape, a.dtype) for a in ins], *[pltpu.HBM(s.shape, s.dtype) for s in land_shapes],
                   _sds((8, 128), f32)),
        in_specs=[HBM] * (ni + nl) + [ANY], out_specs=(SEM, SEM, *[HBM] * (ni + nl), pl.BlockSpec(memory_space=pltpu.VMEM)),
        input_output_aliases={i: 2 + i for i in range(ni + nl)},
        compiler_params=pltpu.CompilerParams(has_side_effects=EFFECT),
    )(*[pltpu.with_memory_space_constraint(a, pltpu.HBM) for a in ins], *lands, after)
    return res[0], res[1], list(res[2:2 + ni]), list(res[2 + ni:2 + ni + nl]), res[-1]


def _split_wait(name, make_copies, send_sems, recv_sems, ins, lands, after):
    ni, nl = len(ins), len(lands)

    def body(*refs):
        in_refs, land_refs = refs[:ni], refs[ni:ni + nl]
        s_sems, r_sems = refs[ni + nl], refs[ni + nl + 1]
        for cp in make_copies(in_refs, land_refs, s_sems, r_sems):
            cp.wait_send()
            cp.wait_recv()

    res = pl.pallas_call(
        body, name=name, out_shape=tuple(pltpu.HBM(a.shape, a.dtype) for a in ins + lands),
        in_specs=[HBM] * (ni + nl) + [SEM, SEM, ANY], out_specs=tuple([HBM] * (ni + nl)),
        input_output_aliases={i: i for i in range(ni + nl)},
        compiler_params=pltpu.CompilerParams(has_side_effects=EFFECT),
    )(*ins, *lands, send_sems, recv_sems, after)
    return list(res[:ni]), list(res[ni:])


def _gather_copies(n):
    def make(in_refs, land_refs, send_sems, recv_sems):
        x, y, c, chips = _coords()
        s = 2 * x + y
        return [pltpu.make_async_remote_copy(src_ref=in_refs[a], dst_ref=land_refs[a].at[s], send_sem=send_sems.at[k * n + a],
                                             recv_sem=recv_sems.at[k * n + a], device_id=(px, py, c), device_id_type=MESH)
                for k, (px, py) in enumerate(chips) for a in range(n)]
    return make


def _sibling_half_copies(n):
    def make(in_refs, land_refs, send_sems, recv_sems):
        x, y, c, _ = _coords()
        return [pltpu.make_async_remote_copy(src_ref=in_refs[a].at[:, 1 - c], dst_ref=land_refs[a], send_sem=send_sems.at[a],
                                             recv_sem=recv_sems.at[a], device_id=(x, y, 1 - c), device_id_type=MESH)
                for a in range(n)]
    return make


def _chip_part_copies(n):
    def make(in_refs, land_refs, send_sems, recv_sems):
        x, y, c, chips = _coords()
        return [pltpu.make_async_remote_copy(src_ref=in_refs[a].at[2 * px + py], dst_ref=land_refs[a].at[k],
                                             send_sem=send_sems.at[k * n + a], recv_sem=recv_sems.at[k * n + a],
                                             device_id=(px, py, c), device_id_type=MESH)
                for k, (px, py) in enumerate(chips) for a in range(n)]
    return make


def _sibling_whole_copies(n):
    def make(in_refs, land_refs, send_sems, recv_sems):
        x, y, c, _ = _coords()
        return [pltpu.make_async_remote_copy(src_ref=in_refs[a], dst_ref=land_refs[a], send_sem=send_sems.at[a],
                                             recv_sem=recv_sems.at[a], device_id=(x, y, 1 - c), device_id_type=MESH)
                for a in range(n)]
    return make


def _place_own(chip_arr, owns, lands, steps):
    n = len(owns)

    def body(s_ref, *refs):
        for a in range(n):
            refs[2 * n + a][...] = refs[a][...]

    tiles = [o.shape[0] // steps for o in owns]
    spec = pltpu.PrefetchScalarGridSpec(
        num_scalar_prefetch=1, grid=(steps,),
        in_specs=[pl.BlockSpec((t, o.shape[1]), lambda i, s_ref: (i, 0)) for t, o in zip(tiles, owns)] + [ANY] * n,
        out_specs=[pl.BlockSpec((None, t, o.shape[1]), lambda i, s_ref: (s_ref[0], i, 0)) for t, o in zip(tiles, owns)])
    return pl.pallas_call(body, name="place_own", grid_spec=spec, out_shape=[_sds(l.shape, l.dtype) for l in lands],
                          input_output_aliases={1 + n + a: a for a in range(n)},
                          compiler_params=_params(1))(chip_arr, *owns, *lands)


def _sibling_halves(g4s, small):
    n = len(g4s)

    def body(*refs):
        ins, small_ref = refs[:n], refs[n]
        outs, small_out = refs[n + 1:2 * n + 1], refs[2 * n + 1]
        send_sems, recv_sems = refs[2 * n + 2:]
        x, y, c, _ = _coords()
        sib = (x, y, 1 - c)

        def remote(a, half):
            src = small_ref if a == n else ins[a].at[:, half]
            dst = small_out if a == n else outs[a]
            return pltpu.make_async_remote_copy(src_ref=src, dst_ref=dst, send_sem=send_sems.at[a], recv_sem=recv_sems.at[a],
                                                device_id=sib, device_id_type=MESH)

        sends = [remote(a, 1 - c) for a in range(n + 1)]
        for cp in sends:
            cp.start()
        for a in range(n + 1):
            remote(a, c).wait_recv()
        for cp in sends:
            cp.wait_send()

    return pl.pallas_call(
        body, name="reduce_sibling", in_specs=[ANY] * (n + 1), out_specs=[ANY] * (n + 1),
        out_shape=[_sds((g.shape[0],) + g.shape[2:], f32) for g in g4s] + [_sds(small.shape, f32)],
        scratch_shapes=[pltpu.SemaphoreType.DMA((n + 1,)), pltpu.SemaphoreType.DMA((n + 1,))],
    )(*g4s, small)


def _exchange_chips(parts, small2):
    n = len(parts)

    def body(*refs):
        ins, small_ref = refs[:n], refs[n]
        outs, small_out = refs[n + 1:2 * n + 1], refs[2 * n + 1]
        send_sems, recv_sems, local_sem = refs[2 * n + 2:]
        x, y, c, chips = _coords()
        s = 2 * x + y
        local = pltpu.make_async_copy(small_ref.at[c], small_out.at[s], local_sem)
        local.start()

        def remote(k, a, dest_chip, small_slot, peer):
            if a == n:
                src, dst = small_ref.at[c], small_out.at[small_slot]
            else:
                src, dst = ins[a].at[dest_chip], outs[a].at[k]
            i = k * (n + 1) + a
            return pltpu.make_async_remote_copy(src_ref=src, dst_ref=dst, send_sem=send_sems.at[i], recv_sem=recv_sems.at[i],
                                                device_id=peer, device_id_type=MESH)

        sends = [remote(k, a, 2 * px + py, s, (px, py, c)) for k, (px, py) in enumerate(chips) for a in range(n + 1)]
        for cp in sends:
            cp.start()
        for k, (px, py) in enumerate(chips):
            for a in range(n + 1):
                remote(k, a, s, 2 * px + py, (px, py, c)).wait_recv()
        for cp in sends:
            cp.wait_send()
        local.wait()

    m = 3 * (n + 1)
    return pl.pallas_call(
        body, name="reduce_chips", in_specs=[ANY] * (n + 1), out_specs=[ANY] * (n + 1),
        out_shape=[_sds((3,) + p.shape[1:], p.dtype) for p in parts] + [_sds((NCHIP,) + small2.shape[1:], f32)],
        scratch_shapes=[pltpu.SemaphoreType.DMA((m,)), pltpu.SemaphoreType.DMA((m,)), pltpu.SemaphoreType.DMA],
    )(*parts, small2)


def _share_sibling(halves):
    n = len(halves)

    def body(*refs):
        ins, outs = refs[:n], refs[n:2 * n]
        send_sems, recv_sems = refs[2 * n:]
        x, y, c, _ = _coords()
        sib = (x, y, 1 - c)
        sends = [pltpu.make_async_remote_copy(src_ref=ins[a], dst_ref=outs[a], send_sem=send_sems.at[a], recv_sem=recv_sems.at[a],
                                              device_id=sib, device_id_type=MESH) for a in range(n)]
        for cp in sends:
            cp.start()
        for cp in sends:
            cp.wait()

    return pl.pallas_call(
        body, name="reduce_share", in_specs=[ANY] * n, out_specs=[ANY] * n,
        out_shape=[_sds(h.shape, f32) for h in halves],
        scratch_shapes=[pltpu.SemaphoreType.DMA((n,)), pltpu.SemaphoreType.DMA((n,))],
    )(*halves)


def _block_diag_pairs(w):
    w = w.reshape(NCH, 2, HD, HD)
    z = jnp.zeros((NCH, HD, HD), w.dtype)
    return jnp.concatenate([jnp.concatenate([w[:, 0], z], axis=2), jnp.concatenate([z, w[:, 1]], axis=2)], axis=1)


def _diag_blocks(m):
    return jnp.stack([m[:, :HD, :HD], m[:, HD:, HD:]], axis=1).reshape(NH, HD, HD)


def _pack(vs, rows):
    flat = jnp.concatenate([v.reshape(-1) for v in vs])
    return jnp.pad(flat, (0, rows * 128 - flat.shape[0])).reshape(rows, 128)


def _unpack(packed, shapes):
    flat = packed.reshape(-1)
    out, off = [], 0
    for shp in shapes:
        size = math.prod(shp)
        out.append(flat[off:off + size].reshape(shp))
        off += size
    return out


def _rows_for(sizes, multiple):
    rows = -(-sum(sizes) // 128)
    return -(-rows // multiple) * multiple


def kernel(x, norm_mix_g, w_in, b_gate, conv_w, conv_b, lru_lambda, lru_wa, lru_ba, lru_wx, lru_bx, attn_sink, w_out, norm_ffn_g, w_ffn_in, w_ffn_out, norm_final_g, loss_target, m_norm_mix_g, m_w_in, m_b_gate, m_conv_w, m_conv_b, m_lru_lambda, m_lru_wa, m_lru_ba, m_lru_wx, m_lru_bx, m_attn_sink, m_w_out, m_norm_ffn_g, m_w_ffn_in, m_w_ffn_out, m_norm_final_g, v_norm_mix_g, v_w_in, v_b_gate, v_conv_w, v_conv_b, v_lru_lambda, v_lru_wa, v_lru_ba, v_lru_wx, v_lru_bx, v_attn_sink, v_w_out, v_norm_ffn_g, v_w_ffn_in, v_w_ffn_out, v_norm_final_g):
    S = x.shape[1]
    xs = x[0]
    tgt = loss_target[0]
    cx, cy, cc = lax.axis_index("x"), lax.axis_index("y"), lax.axis_index("c")
    chip = 2 * cx + cy
    SW = D // NCHIP

    small_shard = _pack([conv_w[0], lru_lambda[0], lru_ba[0], lru_bx[0]], 32)
    halves_of = lambda a: a.reshape(2, a.shape[0] // 2, a.shape[1])
    w_in_g, small_g = _gather_chips([halves_of(w_in[0].astype(bf16)), halves_of(small_shard)])
    w_in_g = w_in_g.reshape(NCHIP, D, SHW)
    small_g = small_g.reshape(NCHIP, 32, 128)
    late = [w_ffn_in[0].astype(bf16), w_out[0].astype(bf16), w_ffn_out[0].astype(bf16)]
    late_send, late_recv, late_src, late_land, late_token = _split_start(
        "gather_late_start", 9, _gather_copies(3), late, [_sds((NCHIP,) + a.shape, bf16) for a in late], small_g)
    small_parts = [_unpack(small_g[s], [(4, SW), (2, SW), (2, SW), (2, SW)]) for s in range(NCHIP)]
    conv_w_f, lam_f, ba_f, bx_f = [jnp.concatenate([small_parts[s][p] for s in range(NCHIP)], axis=1) for p in range(4)]
    wbd = jnp.concatenate([_block_diag_pairs(lru_wa[0, 0]), _block_diag_pairs(lru_wx[0, 0]),
                           _block_diag_pairs(lru_wa[0, 1]), _block_diag_pairs(lru_wx[0, 1])], axis=2).astype(bf16)
    conv_b_f = conv_b
    sink = attn_sink

    xn, proj = _rms_matmul("rms_proj", xs, norm_mix_g + late_token[0:1, 0:1], w_in_g, 1024)
    y_a, lru_state = _lru_fwd(proj, conv_w_f, conv_b_f, lam_f, ba_f, bx_f, wbd)
    y_b = _attn_fwd(proj, sink)
    merged = _merge_fwd(proj, b_gate, y_a, y_b, 1024)
    late_src, late_land = _split_wait("gather_late_wait", _gather_copies(3), late_send, late_recv, late_src, late_land, merged)
    chip_arr = chip.reshape(1).astype(jnp.int32)
    w_ffn_in_g, w_out_g, w_ffn_out_g = _place_own(chip_arr, late_src, late_land, 4)
    w_out_f = w_out_g.reshape(D, D)
    w_ffn_out_f = w_ffn_out_g.reshape(FF, D)
    x1 = _mm_residual("out_proj", merged, w_out_f, xs, 512)
    xn2, gu, act = _rms_matmul_swiglu("rms_ffn_in", x1, norm_ffn_g, w_ffn_in_g, 1024)
    x2 = _mm_residual("ffn_out", act, w_ffn_out_f, x1, 512)
    dx2, loss_row, dg3 = _final_loss_bwd(x2, norm_final_g.reshape(1, D), tgt, 512)

    tm = min(1024, S)
    tk = min(2048, S)
    gw_ffn_out = _mm_tn("dw_ffn_out", act, pl.BlockSpec((tk, SHW), lambda i, k: (k, i)),
                        dx2, pl.BlockSpec((tk, D), lambda i, k: (k, 0)),
                        _sds((FF, D), f32), pl.BlockSpec((SHW, D), lambda i, k: (i, 0)), (2, S // tk), (SHW, D))
    dgu = _swiglu_bwd(dx2, w_ffn_out_f, gu, 256)
    dxn2 = _mm_nt_groups("dxn2", dgu, pl.BlockSpec((None, tm, SHW), lambda i, g: (g // 2, i, g % 2)), w_ffn_in_g, S, tm)
    gw_ffn_in = _mm_tn("dw_ffn_in", xn2, pl.BlockSpec((tk, D), lambda g, k: (k, 0)),
                       dgu, pl.BlockSpec((None, tk, SHW), lambda g, k: (g // 2, k, g % 2)),
                       _sds((NCHIP, D, SHW), f32), pl.BlockSpec((None, D, SHW), lambda g, k: (g, 0, 0)),
                       (NCHIP, S // tk), (D, SHW))
    c_arr = cc.reshape(1).astype(jnp.int32)
    early_names, early_tiles = ["w_ffn_in", "w_ffn_out"], [256, 352]
    early = [gw_ffn_in.reshape(NCHIP, 2, D // 2, SHW), gw_ffn_out.reshape(NCHIP, 2, FF // NCHIP // 2, D)]
    ea_send, ea_recv, ea_src, ea_land, ea_token = _split_start(
        "reduce_early_sibling_start", 2, _sibling_half_copies(2), early,
        [_sds((NCHIP,) + g.shape[2:], f32) for g in early], dxn2)
    dx1, dg2 = _rms_bwd("rms_ffn_bwd", x1, norm_ffn_g + ea_token[0:1, 0:1], dxn2, dx2, 512)

    dmerged = _mm_nt_resident("d_merged", dx1, w_out_f, 512)
    gw_out = _mm_tn("dw_out", merged, pl.BlockSpec((tk, D), lambda i, k: (k, 0)),
                    dx1, pl.BlockSpec((tk, D), lambda i, k: (k, 0)),
                    _sds((D, D), f32), pl.BlockSpec((D, D), lambda i, k: (0, 0)), (1, S // tk), (D, D))
    dz0, dz1, dy_a, dy_b, db0, db1 = _merge_bwd(proj, b_gate, y_a, y_b, dmerged, 1024)
    ea_src, ea_land = _split_wait("reduce_early_sibling_wait", _sibling_half_copies(2), ea_send, ea_recv, ea_src, ea_land, dy_b)
    early_pairs = [_pair_sum("pair_sum_" + nm, c_arr, g4, r, th)
                   for nm, g4, r, th in zip(early_names, ea_src, ea_land, early_tiles)]
    eb_send, eb_recv, eb_src, eb_land, eb_token = _split_start(
        "reduce_early_chips_start", 6, _chip_part_copies(2), [p[1] for p in early_pairs],
        [_sds((3,) + p[1].shape[1:], bf16) for p in early_pairs], early_pairs[0][0])
    dq, dk, dv, dsink = _attn_bwd(proj, sink + eb_token[0:1, 0:1], y_b, dy_b)
    _, eb_land = _split_wait("reduce_early_chips_wait", _chip_part_copies(2), eb_send, eb_recv, eb_src, eb_land, dq)
    early_halves = [_chip_sum("chip_sum_" + nm, chip_arr, p[0], r3, th)
                    for nm, p, r3, th in zip(early_names, early_pairs, eb_land, early_tiles)]
    ec_send, ec_recv, ec_src, ec_land, ec_token = _split_start(
        "reduce_early_share_start", 2, _sibling_whole_copies(2), early_halves, [_sds(h.shape, f32) for h in early_halves], dq)
    du, dgl, dcw, dcb, dlam, dba, dbx, dwbd = _lru_bwd(proj, dy_a, lru_state, conv_w_f, conv_b_f + ec_token[0:1, 0:1], lam_f, ba_f, bx_f, wbd)
    early_halves, early_other = _split_wait("reduce_early_share_wait", _sibling_whole_copies(2), ec_send, ec_recv, ec_src, ec_land, du)
    dproj = jnp.concatenate([du, dgl, dq, dk.astype(bf16), dv.astype(bf16), dz0, dz1], axis=1)
    gw_in = _mm_tn("dw_in", xn, pl.BlockSpec((tk, D), lambda g, k: (k, 0)),
                   dproj, pl.BlockSpec((tk, SHW), lambda g, k: (k, g)),
                   _sds((NCHIP, D, SHW), f32), pl.BlockSpec((None, D, SHW), lambda g, k: (g, 0, 0)),
                   (NCHIP, S // tk), (D, SHW))
    wa_send, wa_recv, wa_src, wa_land, wa_token = _split_start(
        "reduce_w_in_sibling_start", 1, _sibling_half_copies(1), [gw_in.reshape(NCHIP, 2, D // 2, SHW)],
        [_sds((NCHIP, D // 2, SHW), f32)], dproj)
    dxn =_mm_nt_groups("dxn", dproj, pl.BlockSpec((tm, SHW), lambda i, g: (i, g)), w_in_g, S, tm)
    wa_src, wa_land = _split_wait("reduce_w_in_sibling_wait", _sibling_half_copies(1), wa_send, wa_recv, wa_src, wa_land, dxn)
    w_in_pair = _pair_sum("pair_sum_w_in", c_arr, wa_src[0], wa_land[0], 256)
    wb_send, wb_recv, wb_src, wb_land, wb_token = _split_start(
        "reduce_w_in_chips_start", 3, _chip_part_copies(1), [w_in_pair[1]], [_sds((3, D // 2, SHW), bf16)], w_in_pair[0])
    grad_x, dg1 = _rms_bwd("rms_mix_bwd", xs, norm_mix_g + wb_token[0:1, 0:1], dxn, dx1, 512)
    _, wb_land = _split_wait("reduce_w_in_chips_wait", _chip_part_copies(1), wb_send, wb_recv, wb_src, wb_land, dg1)
    w_in_half = _chip_sum("chip_sum_w_in", chip_arr, w_in_pair[0], wb_land[0], 256)

    d_wa = jnp.stack([_diag_blocks(dwbd[:, :, 0:CW]), _diag_blocks(dwbd[:, :, 2 * CW:3 * CW])])
    d_wx = jnp.stack([_diag_blocks(dwbd[:, :, CW:2 * CW]), _diag_blocks(dwbd[:, :, 3 * CW:4 * CW])])
    small_full = [dg1, jnp.concatenate([db0, db1], axis=1), dcw, dcb, dlam, d_wa, dba, d_wx, dbx, dsink[:, 0], dg2, dg3,
                  loss_row[0, 0:1]]
    full_shapes = [(1, D), (1, 2 * D), (4, D), (1, D), (2, D), (2, NH, HD, HD), (2, D), (2, NH, HD, HD), (2, D), (NH,),
                   (1, D), (1, D), (1,)]
    rows_full = _rows_for([math.prod(s) for s in full_shapes], 16)
    small_vec = _pack(small_full, rows_full)

    late_names, late_tiles = ["w_in", "w_out"], [256, 128]
    big = [gw_out.reshape(NCHIP, 2, D // NCHIP // 2, D)]
    *recv_a, small_sib = _sibling_halves(big, small_vec)
    w_out_pair = _pair_sum("pair_sum_w_out", c_arr, big[0], recv_a[0], 128)
    small_chip = _add2("pair_sum_small", small_vec, small_sib).reshape(2, rows_full // 2, 128)
    *recv_b, small_all = _exchange_chips([w_out_pair[1]], small_chip)
    w_out_half = _chip_sum("chip_sum_w_out", chip_arr, w_out_pair[0], recv_b[0], 128)
    halves = [w_in_half, w_out_half, _sum4("chip_sum_small", small_all, rows_full // 2)]
    *recv_c, small_other = _share_sibling(halves)
    small_lo = jnp.where(cc == 0, halves[2], small_other)
    small_hi = jnp.where(cc == 0, small_other, halves[2])
    g_full = _unpack(jnp.concatenate([small_lo, small_hi], axis=0), full_shapes)

    out_big = {}
    for nm, w, g_own, g_recv, m, v, th in zip(late_names + early_names, [w_in, w_out, w_ffn_in, w_ffn_out],
                                              halves[:2] + early_halves, recv_c + early_other,
                                              [m_w_in, m_w_out, m_w_ffn_in, m_w_ffn_out],
                                              [v_w_in, v_w_out, v_w_ffn_in, v_w_ffn_out], late_tiles + early_tiles):
        g_, d_, m_, v_ = _adamw_halves("adamw_" + nm, c_arr, w[0], g_own, g_recv, m[0], v[0], th)
        out_big[nm] = (g_[None], d_[None], m_[None], v_[None])

    small_names = ["norm_mix_g", "b_gate", "conv_w", "conv_b", "lru_lambda", "lru_wa", "lru_ba", "lru_wx", "lru_bx", "attn_sink",
                   "norm_ffn_g", "norm_final_g"]
    sharded = {"conv_w", "lru_lambda", "lru_ba", "lru_bx"}
    small_w = [norm_mix_g, b_gate, conv_w, conv_b, lru_lambda, lru_wa, lru_ba, lru_wx, lru_bx, attn_sink, norm_ffn_g, norm_final_g]
    small_m = [m_norm_mix_g, m_b_gate, m_conv_w, m_conv_b, m_lru_lambda, m_lru_wa, m_lru_ba, m_lru_wx, m_lru_bx, m_attn_sink,
               m_norm_ffn_g, m_norm_final_g]
    small_v = [v_norm_mix_g, v_b_gate, v_conv_w, v_conv_b, v_lru_lambda, v_lru_wa, v_lru_ba, v_lru_wx, v_lru_bx, v_attn_sink,
               v_norm_ffn_g, v_norm_final_g]
    g_local = []
    for nm, g, w in zip(small_names, g_full, small_w):
        if nm in sharded:
            g = lax.dynamic_slice_in_dim(g, chip * SW, SW, axis=1)
        g_local.append(g.reshape(w.shape))
    local_shapes = [w.shape for w in small_w]
    rows_local = _rows_for([math.prod(s) for s in local_shapes], 8)
    d_s, m_s, v_s = _adamw("adamw_small", _pack(small_w, rows_local), _pack(g_local, rows_local),
                           _pack(small_m, rows_local), _pack(small_v, rows_local), rows_local)
    d_l, m_l, v_l = _unpack(d_s, local_shapes), _unpack(m_s, local_shapes), _unpack(v_s, local_shapes)
    res = {nm: (g_local[i], d_l[i], m_l[i], v_l[i]) for i, nm in enumerate(small_names)}
    res.update(out_big)

    order = ["norm_mix_g", "w_in", "b_gate", "conv_w", "conv_b", "lru_lambda", "lru_wa", "lru_ba", "lru_wx", "lru_bx", "attn_sink",
             "w_out", "norm_ffn_g", "w_ffn_in", "w_ffn_out", "norm_final_g"]
    outs = [g_full[-1][0], grad_x[None]]
    for k in range(4):
        outs += [res[nm][k] for nm in order]
    return tuple(outs)
```

```python
import functools
import math

import jax
import jax.numpy as jnp
from jax import lax
from jax.experimental import pallas as pl
from jax.experimental.pallas import tpu as pltpu

f32 = jnp.float32
bf16 = jnp.bfloat16

D = 1024
NH = 16
HD = 64
FF = 2816
INW = 5632
NCHIP = 4
SHW = INW // NCHIP
CW = 128
NCH = D // CW
BLK = 128
EPS = 1e-6
NEG_INF = -1e30
RGLRU_C = 8.0
ADAM_LR, ADAM_B1, ADAM_B2, ADAM_EPS, ADAM_WD, ADAM_STEP = 0.001, 0.9, 0.999, 1e-08, 0.01, 10
VMEM_LIMIT = 58 * 1024 * 1024
MESH = pl.DeviceIdType.MESH
ANY = pl.BlockSpec(memory_space=pl.ANY)

COL_U, COL_G, COL_Q, COL_K, COL_V, COL_Z0, COL_Z1 = 0, 4, 8, 12, 13, 14, 18
MERGE_W = 512
MERGE_Z0, MERGE_Z1 = (COL_Z0 * 256) // MERGE_W, (COL_Z1 * 256) // MERGE_W


def _params(n_axes, vmem=False):
    return pltpu.CompilerParams(dimension_semantics=("arbitrary",) * n_axes,
                                vmem_limit_bytes=VMEM_LIMIT if vmem else None)


def _sds(shape, dtype):
    return jax.ShapeDtypeStruct(tuple(shape), dtype)


_DIMS = {"nn": (((1,), (0,)), ((), ())), "nt": (((1,), (1,)), ((), ())), "tn": (((0,), (0,)), ((), ()))}


def _mm(name, mode, a, a_spec, b, b_spec, out_shape, out_spec, grid, nk, acc_shape, add=None, add_spec=None):
    has_add = add is not None

    def body(*refs):
        a_ref, b_ref = refs[0], refs[1]
        add_ref = refs[2] if has_add else None
        o_ref = refs[2 + has_add]
        part = lax.dot_general(a_ref[...].astype(bf16), b_ref[...].astype(bf16), _DIMS[mode],
                               preferred_element_type=f32)
        if nk == 1:
            if has_add:
                part = add_ref[...] + part
            o_ref[...] = part.astype(o_ref.dtype)
            return
        acc_ref = refs[3 + has_add]
        k = pl.program_id(len(grid) - 1)

        @pl.when(k == 0)
        def _():
            acc_ref[...] = part

        @pl.when(k > 0)
        def _():
            acc_ref[...] += part

        @pl.when(k == nk - 1)
        def _():
            res = acc_ref[...]
            if has_add:
                res = add_ref[...] + res
            o_ref[...] = res.astype(o_ref.dtype)

    ins = [a, b] + ([add] if has_add else [])
    in_specs = [a_spec, b_spec] + ([add_spec] if has_add else [])
    scratch = [pltpu.VMEM(acc_shape, f32)] if nk > 1 else []
    return pl.pallas_call(body, name=name, grid=grid, in_specs=in_specs, out_specs=out_spec, out_shape=out_shape,
                          scratch_shapes=scratch, compiler_params=_params(len(grid), True))(*ins)


def _rms_matmul(name, x, g, w3, tm):
    S, K = x.shape
    G, _, Nw = w3.shape
    tm = min(tm, S)

    def body(x_ref, g_ref, w_ref, xn_ref, o_ref, xs_ref):
        @pl.when(pl.program_id(1) == 0)
        def _():
            xf = x_ref[...]
            r = lax.rsqrt(jnp.mean(xf * xf, axis=-1, keepdims=True) + EPS)
            xn = ((xf * r) * g_ref[...]).astype(bf16)
            xs_ref[...] = xn
            xn_ref[...] = xn

        o_ref[...] = jnp.dot(xs_ref[...], w_ref[...], preferred_element_type=f32).astype(bf16)

    return pl.pallas_call(
        body, name=name, grid=(S // tm, G),
        in_specs=[pl.BlockSpec((tm, K), lambda i, j: (i, 0)), pl.BlockSpec((1, K), lambda i, j: (0, 0)),
                  pl.BlockSpec((None, K, Nw), lambda i, j: (j, 0, 0))],
        out_specs=[pl.BlockSpec((tm, K), lambda i, j: (i, 0)), pl.BlockSpec((tm, Nw), lambda i, j: (i, j))],
        out_shape=[_sds((S, K), bf16), _sds((S, G * Nw), bf16)],
        scratch_shapes=[pltpu.VMEM((tm, K), bf16)], compiler_params=_params(2, True))(x, g, w3)


def _rms_matmul_swiglu(name, x, g, w3, tm):
    S, K = x.shape
    G, _, Nw = w3.shape
    tm = min(tm, S)
    half = G // 2

    def body(x_ref, g_ref, wg_ref, wu_ref, xn_ref, gu_ref, act_ref, xs_ref):
        @pl.when(pl.program_id(1) == 0)
        def _():
            xf = x_ref[...]
            r = lax.rsqrt(jnp.mean(xf * xf, axis=-1, keepdims=True) + EPS)
            xn = ((xf * r) * g_ref[...]).astype(bf16)
            xs_ref[...] = xn
            xn_ref[...] = xn

        xn = xs_ref[...]
        gate = jnp.dot(xn, wg_ref[...], preferred_element_type=f32)
        up = jnp.dot(xn, wu_ref[...], preferred_element_type=f32)
        gu_ref[0] = gate.astype(bf16)
        gu_ref[1] = up.astype(bf16)
        act_ref[...] = ((gate * _sigmoid(gate)) * up).astype(bf16)

    return pl.pallas_call(
        body, name=name, grid=(S // tm, half),
        in_specs=[pl.BlockSpec((tm, K), lambda i, j: (i, 0)), pl.BlockSpec((1, K), lambda i, j: (0, 0)),
                  pl.BlockSpec((None, K, Nw), lambda i, j: (j, 0, 0)),
                  pl.BlockSpec((None, K, Nw), lambda i, j: (half + j, 0, 0))],
        out_specs=[pl.BlockSpec((tm, K), lambda i, j: (i, 0)), pl.BlockSpec((2, tm, Nw), lambda i, j: (0, i, j)),
                   pl.BlockSpec((tm, Nw), lambda i, j: (i, j))],
        out_shape=[_sds((S, K), bf16), _sds((2, S, half * Nw), bf16), _sds((S, half * Nw), bf16)],
        scratch_shapes=[pltpu.VMEM((tm, K), bf16)], compiler_params=_params(2, True))(x, g, w3, w3)


def _mm_residual(name, a, w, res, tm):
    S, K = a.shape
    N = w.shape[1]
    tm = min(tm, S)
    return _mm(name, "nn", a, pl.BlockSpec((tm, K), lambda i: (i, 0)), w, pl.BlockSpec((K, N), lambda i: (0, 0)),
               _sds((S, N), f32), pl.BlockSpec((tm, N), lambda i: (i, 0)), (S // tm,), 1, None,
               add=res, add_spec=pl.BlockSpec((tm, N), lambda i: (i, 0)))


def _mm_nt_resident(name, a, w, tm):
    S, K = a.shape
    N = w.shape[0]
    tm = min(tm, S)
    return _mm(name, "nt", a, pl.BlockSpec((tm, K), lambda i: (i, 0)), w, pl.BlockSpec((N, K), lambda i: (0, 0)),
               _sds((S, N), f32), pl.BlockSpec((tm, N), lambda i: (i, 0)), (S // tm,), 1, None)


def _mm_nt_groups(name, a, a_spec, w3, S, tm):
    G, Dout, Kw = w3.shape
    return _mm(name, "nt", a, a_spec, w3, pl.BlockSpec((None, Dout, Kw), lambda i, g: (g, 0, 0)),
               _sds((S, Dout), f32), pl.BlockSpec((tm, Dout), lambda i, g: (i, 0)), (S // tm, G), G, (tm, Dout))


def _mm_tn(name, a, a_spec, b, b_spec, out_shape, out_spec, grid, acc_shape):
    return _mm(name, "tn", a, a_spec, b, b_spec, out_shape, out_spec, grid, grid[-1], acc_shape)


def _sigmoid(x):
    return 0.5 * jnp.tanh(0.5 * x) + 0.5


_GELU_C = math.sqrt(2.0 / math.pi)


def _gelu_and_grad(x):
    v = _GELU_C * (x + 0.044715 * (x * x * x))
    t = jnp.tanh(v)
    gl = 0.5 * x * (1.0 + t)
    dgl = 0.5 * (1.0 + t) + 0.5 * x * (1.0 - t * t) * (_GELU_C * (1.0 + 3.0 * 0.044715 * (x * x)))
    return gl, dgl


def _one_minus_exp2x(x, ex):
    y = 2.0 * x
    series = y * (1.0 + y * (0.5 + y * (1.0 / 6.0 + y * (1.0 / 24.0))))
    return jnp.where(y > -1.0 / 64.0, -series, 1.0 - ex * ex)


def _merge_fwd(proj, b_gate, y_a, y_b, tm):
    S = proj.shape[0]
    tm = min(tm, S)

    def body(z0_ref, z1_ref, b0_ref, b1_ref, ya_ref, yb_ref, o_ref):
        g0 = _sigmoid(z0_ref[...].astype(f32) + b0_ref[...])
        g1 = _sigmoid(z1_ref[...].astype(f32) + b1_ref[...])
        o_ref[...] = (g0 * ya_ref[...].astype(f32) + g1 * yb_ref[...].astype(f32)).astype(bf16)

    blk = lambda off: pl.BlockSpec((tm, MERGE_W), lambda j, i: (i, off + j))
    vec = lambda off: pl.BlockSpec((1, MERGE_W), lambda j, i: (0, off + j))
    return pl.pallas_call(body, name="merge_fwd", grid=(D // MERGE_W, S // tm),
                          in_specs=[blk(MERGE_Z0), blk(MERGE_Z1), vec(0), vec(D // MERGE_W), blk(0), blk(0)],
                          out_specs=blk(0), out_shape=_sds((S, D), bf16),
                          compiler_params=_params(2))(proj, proj, b_gate, b_gate, y_a, y_b)


def _merge_bwd(proj, b_gate, y_a, y_b, dm, tm):
    S = proj.shape[0]
    tm = min(tm, S)

    def body(z0_ref, z1_ref, b0_ref, b1_ref, ya_ref, yb_ref, dm_ref, dz0_ref, dz1_ref, dya_ref, dyb_ref, db0_ref, db1_ref):
        g0 = _sigmoid(z0_ref[...].astype(f32) + b0_ref[...])
        g1 = _sigmoid(z1_ref[...].astype(f32) + b1_ref[...])
        d = dm_ref[...]
        dz0 = (d * ya_ref[...].astype(f32)) * (g0 * (1.0 - g0))
        dz1 = (d * yb_ref[...].astype(f32)) * (g1 * (1.0 - g1))
        dz0_ref[...] = dz0.astype(bf16)
        dz1_ref[...] = dz1.astype(bf16)
        dya_ref[...] = (d * g0).astype(bf16)
        dyb_ref[...] = (d * g1).astype(bf16)

        @pl.when(pl.program_id(1) == 0)
        def _():
            db0_ref[...] = jnp.zeros_like(db0_ref)
            db1_ref[...] = jnp.zeros_like(db1_ref)

        db0_ref[...] += jnp.sum(dz0, axis=0, keepdims=True)
        db1_ref[...] += jnp.sum(dz1, axis=0, keepdims=True)

    blk = lambda off: pl.BlockSpec((tm, MERGE_W), lambda j, i: (i, off + j))
    vec = lambda off: pl.BlockSpec((1, MERGE_W), lambda j, i: (0, off + j))
    return pl.pallas_call(
        body, name="merge_bwd", grid=(D // MERGE_W, S // tm),
        in_specs=[blk(MERGE_Z0), blk(MERGE_Z1), vec(0), vec(D // MERGE_W), blk(0), blk(0), blk(0)],
        out_specs=[blk(0), blk(0), blk(0), blk(0), vec(0), vec(0)],
        out_shape=[_sds((S, D), bf16), _sds((S, D), bf16), _sds((S, D), bf16), _sds((S, D), bf16),
                   _sds((1, D), f32), _sds((1, D), f32)],
        compiler_params=_params(2))(proj, proj, b_gate, b_gate, y_a, y_b, dm)


def _swiglu_bwd(dx, w, gu, tm):
    S, K = dx.shape
    tm = min(tm, S)

    def body(dx_ref, w_ref, gu_ref, o_ref):
        d = lax.dot_general(dx_ref[...].astype(bf16), w_ref[...], _DIMS["nt"], preferred_element_type=f32)
        g = gu_ref[0].astype(f32)
        u = gu_ref[1].astype(f32)
        s = _sigmoid(g)
        o_ref[0] = ((d * u) * (s * (1.0 + g * (1.0 - s)))).astype(bf16)
        o_ref[1] = (d * (g * s)).astype(bf16)

    stacked = pl.BlockSpec((2, tm, FF), lambda i: (0, i, 0))
    return pl.pallas_call(body, name="swiglu_bwd", grid=(S // tm,),
                          in_specs=[pl.BlockSpec((tm, K), lambda i: (i, 0)), pl.BlockSpec((FF, K), lambda i: (0, 0)), stacked],
                          out_specs=stacked, out_shape=_sds((2, S, FF), bf16),
                          compiler_params=_params(1, True))(dx, w, gu)


def _final_loss_bwd(x2, g3, tgt, tm):
    S = x2.shape[0]
    tm = min(tm, S)

    def body(x_ref, g_ref, t_ref, dx_ref, loss_ref, dg_ref):
        @pl.when(pl.program_id(0) == 0)
        def _():
            loss_ref[...] = jnp.zeros_like(loss_ref)
            dg_ref[...] = jnp.zeros_like(dg_ref)

        x = x_ref[...]
        g = g_ref[...]
        r = lax.rsqrt(jnp.mean(x * x, axis=-1, keepdims=True) + EPS)
        xh = x * r
        err = xh * g - t_ref[...]
        row = jnp.mean(err * err, axis=-1, keepdims=True)
        loss_ref[...] += 0.5 * jnp.sum(row, axis=0, keepdims=True)
        dy = err * (1.0 / D)
        dg_ref[...] += jnp.sum(dy * xh, axis=0, keepdims=True)
        dxh = dy * g
        dx_ref[...] = r * (dxh - xh * jnp.mean(dxh * xh, axis=-1, keepdims=True))

    row_blk = pl.BlockSpec((tm, D), lambda i: (i, 0))
    vec = pl.BlockSpec((1, D), lambda i: (0, 0))
    return pl.pallas_call(body, name="final_loss_bwd", grid=(S // tm,), in_specs=[row_blk, vec, row_blk],
                          out_specs=[row_blk, pl.BlockSpec((1, 128), lambda i: (0, 0)), vec],
                          out_shape=[_sds((S, D), f32), _sds((1, 128), f32), _sds((1, D), f32)],
                          compiler_params=_params(1))(x2, g3, tgt)


def _rms_bwd(name, x, g, dxn, dres, tm):
    S = x.shape[0]
    tm = min(tm, S)

    def body(x_ref, g_ref, d_ref, r_ref, dx_ref, dg_ref):
        @pl.when(pl.program_id(0) == 0)
        def _():
            dg_ref[...] = jnp.zeros_like(dg_ref)

        x = x_ref[...]
        d = d_ref[...]
        r = lax.rsqrt(jnp.mean(x * x, axis=-1, keepdims=True) + EPS)
        xh = x * r
        dg_ref[...] += jnp.sum(d * xh, axis=0, keepdims=True)
        dxh = d * g_ref[...]
        dx_ref[...] = r_ref[...] + r * (dxh - xh * jnp.mean(dxh * xh, axis=-1, keepdims=True))

    row_blk = pl.BlockSpec((tm, D), lambda i: (i, 0))
    vec = pl.BlockSpec((1, D), lambda i: (0, 0))
    return pl.pallas_call(body, name=name, grid=(S // tm,), in_specs=[row_blk, vec, row_blk, row_blk],
                          out_specs=[row_blk, vec], out_shape=[_sds((S, D), f32), _sds((1, D), f32)],
                          compiler_params=_params(1))(x, g, dxn, dres)


LRU_TT = 256
SCAN_UNROLL = 8


HALO = 16


def _halo(ref, i, S):
    nt = S // LRU_TT
    t0 = pl.multiple_of(i * LRU_TT, LRU_TT)
    p0 = pl.multiple_of(jnp.maximum(t0 - HALO, 0), HALO)
    n0 = pl.multiple_of(jnp.minimum(t0 + LRU_TT, S - HALO), HALO)
    prev = jnp.where(i > 0, ref[pl.ds(p0, HALO), :].astype(f32), 0.0)
    nxt = jnp.where(i < nt - 1, ref[pl.ds(n0, HALO), :].astype(f32), 0.0)
    return jnp.concatenate([prev, ref[pl.ds(t0, LRU_TT), :].astype(f32), nxt], axis=0)


def _shift(ext, k):
    n = LRU_TT + 2 * HALO
    return pltpu.roll(ext, (-k) % n, 0)[HALO:HALO + LRU_TT]


def _lru_gates(uc, wbd, ba, bx):
    pre = jnp.dot(uc.astype(bf16), wbd, preferred_element_type=f32)
    r_f = _sigmoid(pre[:, 0:CW] + ba[0:1])
    i_f = _sigmoid(pre[:, CW:2 * CW] + bx[0:1])
    r_b = _sigmoid(pre[:, 2 * CW:3 * CW] + ba[1:2])
    i_b = _sigmoid(pre[:, 3 * CW:4 * CW] + bx[1:2])
    return r_f, i_f, r_b, i_b


def _lru_coeffs(r, sp):
    log_a = (-RGLRU_C * r) * sp
    a = jnp.exp(log_a)
    beta = jnp.sqrt(jnp.maximum(_one_minus_exp2x(log_a, a), 0.0))
    return a, beta


def _lru_coeffs_inv(r, sp):
    log_a = (-RGLRU_C * r) * sp
    a = jnp.exp(log_a)
    om = jnp.maximum(_one_minus_exp2x(log_a, a), 0.0)
    return a, jnp.sqrt(om), lax.rsqrt(om)


def _conv_tile(u_ref, i, S, cw, cb):
    ext = _halo(u_ref, i, S)
    um2, um1, u0, up1 = _shift(ext, -2), _shift(ext, -1), ext[HALO:HALO + LRU_TT], _shift(ext, 1)
    uc = um2 * cw[0:1] + um1 * cw[1:2] + u0 * cw[2:3] + up1 * cw[3:4] + cb
    return uc, (um2, um1, u0, up1)


def _scan_pair(S, fwd_a, fwd_b, fwd_out, rev_a, rev_b, rev_out):
    ng = S // 8
    idx = lax.broadcasted_iota(jnp.int32, (8, CW), 0)

    def local(a, b, rev):
        for sh in (1, 2, 4):
            if rev:
                keep = idx < 8 - sh
                amt = 8 - sh
            else:
                keep = idx >= sh
                amt = sh
            a_s = jnp.where(keep, pltpu.roll(a, amt, 0), 1.0)
            b_s = jnp.where(keep, pltpu.roll(b, amt, 0), 0.0)
            b = a * b_s + b
            a = a * a_s
        return a, b

    def step(it, carry):
        cf, cr = carry
        fwd_rows = [pl.multiple_of((it * SCAN_UNROLL + j) * 8, 8) for j in range(SCAN_UNROLL)]
        rev_rows = [pl.multiple_of((ng - 1 - (it * SCAN_UNROLL + j)) * 8, 8) for j in range(SCAN_UNROLL)]
        fwd_loc = [local(fwd_a(r), fwd_b(r), False) for r in fwd_rows]
        rev_loc = [local(rev_a(r), rev_b(r), True) for r in rev_rows]
        for j in range(SCAN_UNROLL):
            a, b = fwd_loc[j]
            h = a * cf + b
            fwd_out[pl.ds(fwd_rows[j], 8), :] = h
            cf = jnp.broadcast_to(h[7:8, :], (8, CW))
            a, b = rev_loc[j]
            h = a * cr + b
            rev_out[pl.ds(rev_rows[j], 8), :] = h
            cr = jnp.broadcast_to(h[0:1, :], (8, CW))
        return cf, cr

    zero = jnp.zeros((8, CW), f32)
    lax.fori_loop(0, ng // SCAN_UNROLL, step, (zero, zero))


def _lru_specs(S):
    seq = lambda off: pl.BlockSpec((S, CW), lambda j: (0, off + j))
    par = lambda rows: pl.BlockSpec((rows, CW), lambda j: (0, j))
    return seq, par


def _lru_fwd(proj, conv_w, conv_b, lam, ba, bx, wbd):
    S = proj.shape[0]
    nt = S // LRU_TT

    def body(u_ref, g_ref, cw_ref, cb_ref, lam_ref, ba_ref, bx_ref, wbd_ref, y_ref, state_ref, af_ref, bf_ref, ab_ref, bb_ref,
             sems):
        cw, cb, ba_v, bx_v, wbd_v = cw_ref[...], cb_ref[...], ba_ref[...], bx_ref[...], wbd_ref[...]
        sp = jax.nn.softplus(-lam_ref[...])
        cols = pl.ds(pl.multiple_of(pl.program_id(0) * CW, CW), CW)
        save = [pltpu.make_async_copy(ref, state_ref.at[k, :, cols], sems.at[k])
                for k, ref in enumerate((af_ref, bf_ref, ab_ref, bb_ref))]

        def phase1(i, c):
            uc, _ = _conv_tile(u_ref, i, S, cw, cb)
            r_f, i_f, r_b, i_b = _lru_gates(uc, wbd_v, ba_v, bx_v)
            rows = pl.ds(pl.multiple_of(i * LRU_TT, LRU_TT), LRU_TT)
            a, beta = _lru_coeffs(r_f, sp[0:1])
            af_ref[rows, :] = a
            bf_ref[rows, :] = beta * (i_f * uc)
            a, beta = _lru_coeffs(r_b, sp[1:2])
            ab_ref[rows, :] = a
            bb_ref[rows, :] = beta * (i_b * uc)
            return c

        lax.fori_loop(0, nt, phase1, 0)
        row8 = lambda ref: (lambda r0: ref[pl.ds(r0, 8), :])
        _scan_pair(S, row8(af_ref), row8(bf_ref), bf_ref, row8(ab_ref), row8(bb_ref), bb_ref)
        for cp in save:
            cp.start()

        def phase3(i, c):
            rows = pl.ds(pl.multiple_of(i * LRU_TT, LRU_TT), LRU_TT)
            y = (bf_ref[rows, :] + bb_ref[rows, :]) * jax.nn.gelu(g_ref[rows, :].astype(f32))
            y_ref[rows, :] = y.astype(y_ref.dtype)
            return c

        lax.fori_loop(0, nt, phase3, 0)
        for cp in save:
            cp.wait()

    seq, par = _lru_specs(S)
    return pl.pallas_call(
        body, name="lru_fwd", grid=(NCH,),
        in_specs=[seq(0), seq(NCH), par(4), par(1), par(2), par(2), par(2),
                  pl.BlockSpec((None, CW, 4 * CW), lambda j: (j, 0, 0))],
        out_specs=[seq(0), ANY], out_shape=[_sds((S, D), bf16), _sds((4, S, D), f32)],
        scratch_shapes=[pltpu.VMEM((S, CW), f32)] * 4 + [pltpu.SemaphoreType.DMA((4,))], compiler_params=_params(1, True),
    )(proj, proj, conv_w, conv_b, lam, ba, bx, wbd)


def _lru_bwd(proj, dy, state, conv_w, conv_b, lam, ba, bx, wbd):
    S = proj.shape[0]
    nt = S // LRU_TT

    def body(u_ref, g_ref, dy_ref, state_ref, cw_ref, cb_ref, lam_ref, ba_ref, bx_ref, wbd_ref,
             du_ref, dg_ref, dcw_ref, dcb_ref, dlam_ref, dba_ref, dbx_ref, dwbd_ref,
             af_ref, bf_ref, ab_ref, bb_ref, dh_ref, sems):
        cw, cb, ba_v, bx_v, wbd_v = cw_ref[...], cb_ref[...], ba_ref[...], bx_ref[...], wbd_ref[...]
        lam_v = lam_ref[...]
        sp = jax.nn.softplus(-lam_v)
        cols = pl.ds(pl.multiple_of(pl.program_id(0) * CW, CW), CW)
        load = [pltpu.make_async_copy(state_ref.at[k, :, cols], ref, sems.at[k])
                for k, ref in enumerate((af_ref, bf_ref, ab_ref, bb_ref))]
        for cp in load:
            cp.start()
        for cp in load:
            cp.wait()
        row8 = lambda ref: (lambda r0: ref[pl.ds(r0, 8), :])

        def phase0(i, c):
            rows = pl.ds(pl.multiple_of(i * LRU_TT, LRU_TT), LRU_TT)
            gl, dgl = _gelu_and_grad(g_ref[rows, :].astype(f32))
            dyt = dy_ref[rows, :].astype(f32)
            dh_ref[rows, :] = dyt * gl
            dg_ref[rows, :] = ((dyt * (bf_ref[rows, :] + bb_ref[rows, :])) * dgl).astype(dg_ref.dtype)
            return c

        lax.fori_loop(0, nt, phase0, 0)

        def scaled_dh(a_ref):
            def f(r0):
                return a_ref[pl.ds(r0, 8), :] * dh_ref[pl.ds(r0, 8), :]
            return f

        _scan_pair(S, row8(ab_ref), scaled_dh(ab_ref), ab_ref, row8(af_ref), scaled_dh(af_ref), af_ref)

        dcw_ref[...] = jnp.zeros_like(dcw_ref)
        dcb_ref[...] = jnp.zeros_like(dcb_ref)
        dlam_ref[...] = jnp.zeros_like(dlam_ref)
        dba_ref[...] = jnp.zeros_like(dba_ref)
        dbx_ref[...] = jnp.zeros_like(dbx_ref)
        dwbd_ref[...] = jnp.zeros_like(dwbd_ref)

        def direction(uc, r, i_g, dht, h_nb, sp_d):
            a, beta, inv_beta = _lru_coeffs_inv(r, sp_d)
            da = dht * h_nb
            dbeta = dht * (i_g * uc)
            d_iu = dht * beta
            dlog_a = da * a - (a * a) * (dbeta * inv_beta)
            dlr = dlog_a * r
            dsp = -RGLRU_C * jnp.sum(dlr, axis=0, keepdims=True)
            dpre_r = (dlr * (1.0 - r)) * (-RGLRU_C * sp_d)
            dpre_i = (d_iu * uc) * (i_g * (1.0 - i_g))
            return dpre_r, dpre_i, d_iu * i_g, dsp

        def phase4(i, c):
            uc, (um2, um1, u0, up1) = _conv_tile(u_ref, i, S, cw, cb)
            r_f, i_f, r_b, i_b = _lru_gates(uc, wbd_v, ba_v, bx_v)
            rows = pl.ds(pl.multiple_of(i * LRU_TT, LRU_TT), LRU_TT)
            dh = dh_ref[rows, :]
            dht_f = dh + _shift(_halo(af_ref, i, S), 1)
            h_prev = _shift(_halo(bf_ref, i, S), -1)
            dht_b = dh + _shift(_halo(ab_ref, i, S), -1)
            h_next = _shift(_halo(bb_ref, i, S), 1)
            prf, pif, duc_f, dsp_f = direction(uc, r_f, i_f, dht_f, h_prev, sp[0:1])
            prb, pib, duc_b, dsp_b = direction(uc, r_b, i_b, dht_b, h_next, sp[1:2])
            dpre = jnp.concatenate([prf, pif, prb, pib], axis=1)
            dpre_b = dpre.astype(bf16)
            duc = (duc_f + duc_b) + lax.dot_general(dpre_b, wbd_v, _DIMS["nt"], preferred_element_type=f32)
            dwbd_ref[...] += lax.dot_general(uc.astype(bf16), dpre_b, _DIMS["tn"], preferred_element_type=f32)
            colsum = lambda v: jnp.sum(v, axis=0, keepdims=True)
            dba_ref[...] += jnp.concatenate([colsum(prf), colsum(prb)], axis=0)
            dbx_ref[...] += jnp.concatenate([colsum(pif), colsum(pib)], axis=0)
            dlam_ref[...] += jnp.concatenate([dsp_f, dsp_b], axis=0)
            dcb_ref[...] += colsum(duc)
            dcw_ref[...] += jnp.concatenate([colsum(duc * um2), colsum(duc * um1), colsum(duc * u0),
                                             colsum(duc * up1)], axis=0)
            af_ref[rows, :] = duc
            return c

        lax.fori_loop(0, nt, phase4, 0)
        dlam_ref[...] = dlam_ref[...] * (-_sigmoid(-lam_v))

        def phase5(i, c):
            ext = _halo(af_ref, i, S)
            rows = pl.ds(pl.multiple_of(i * LRU_TT, LRU_TT), LRU_TT)
            du = (_shift(ext, 2) * cw[0:1] + _shift(ext, 1) * cw[1:2] + ext[HALO:HALO + LRU_TT] * cw[2:3]
                  + _shift(ext, -1) * cw[3:4])
            du_ref[rows, :] = du.astype(du_ref.dtype)
            return c

        lax.fori_loop(0, nt, phase5, 0)

    seq, par = _lru_specs(S)
    return pl.pallas_call(
        body, name="lru_bwd", grid=(NCH,),
        in_specs=[seq(0), seq(NCH), seq(0), ANY, par(4), par(1), par(2), par(2), par(2),
                  pl.BlockSpec((None, CW, 4 * CW), lambda j: (j, 0, 0))],
        out_specs=[seq(0), seq(0), par(4), par(1), par(2), par(2), par(2),
                   pl.BlockSpec((None, CW, 4 * CW), lambda j: (j, 0, 0))],
        out_shape=[_sds((S, D), bf16), _sds((S, D), bf16), _sds((4, D), f32), _sds((1, D), f32), _sds((2, D), f32),
                   _sds((2, D), f32), _sds((2, D), f32), _sds((NCH, CW, 4 * CW), f32)],
        scratch_shapes=[pltpu.VMEM((S, CW), f32)] * 5 + [pltpu.SemaphoreType.DMA((4,))], compiler_params=_params(1, True),
    )(proj, proj, dy, state, conv_w, conv_b, lam, ba, bx, wbd)


_SLOPES = [2.0 ** (-8.0 * (h + 1) / NH) for h in range(NH)]


def _half_mask(shape, e):
    lane = lax.broadcasted_iota(jnp.int32, shape, 1)
    return (lane < HD) if e == 0 else (lane >= HD)


def _both_halves(x, src):
    return jnp.where(_half_mask(x.shape, src), x, pltpu.roll(x, HD, 1))


def _attn_base(n, S):
    tq = lax.broadcasted_iota(jnp.int32, (BLK, 3 * BLK), 0)
    sk = lax.broadcasted_iota(jnp.int32, (BLK, 3 * BLK), 1)
    dist = jnp.abs(tq + BLK - sk)
    kpos = n * BLK - BLK + sk
    valid = (dist <= BLK) & (kpos >= 0) & (kpos < S)
    return jnp.where(valid, -dist.astype(f32), NEG_INF)


def _group_heads(ref, kvh, scale):
    parts = []
    for i in range(4):
        pair = 2 * kvh + i // 2
        x = ref[:, pair * 128:(pair + 1) * 128].astype(f32)
        parts.append(jnp.where(_half_mask(x.shape, i % 2), x * scale, 0.0))
    return parts


def _stack_bf16(parts):
    return jnp.concatenate([p.astype(bf16) for p in parts], axis=0)


def _attn_softmax(s_raw, base, slope, sink):
    s = s_raw + slope * base
    m = jnp.maximum(jnp.max(s, axis=-1, keepdims=True), sink)
    p = jnp.exp(s - m)
    esink = jnp.exp(sink - m)
    inv = 1.0 / (jnp.sum(p, axis=-1, keepdims=True) + esink)
    return p, inv, esink * inv


def _attn_specs(S):
    nb = S // BLK
    q_spec = pl.BlockSpec((BLK, D), lambda n: (n, 2))
    kv = lambda col: [pl.BlockSpec((BLK, 256), lambda n: (jnp.maximum(n - 1, 0), col)),
                      pl.BlockSpec((BLK, 256), lambda n: (n, col)),
                      pl.BlockSpec((BLK, 256), lambda n: (jnp.minimum(n + 1, nb - 1), col))]
    return nb, q_spec, kv(COL_K), kv(COL_V)


def _attn_fwd(proj, sink):
    S = proj.shape[0]
    nb, q_spec, k_specs, v_specs = _attn_specs(S)

    def body(sink_ref, q_ref, kp_ref, kc_ref, kn_ref, vp_ref, vc_ref, vn_ref, o_ref):
        base = _attn_base(pl.program_id(0), S)
        kcat = jnp.concatenate([kp_ref[...], kc_ref[...], kn_ref[...]], axis=0).astype(f32)
        vcat = jnp.concatenate([vp_ref[...], vc_ref[...], vn_ref[...]], axis=0).astype(f32)
        even = _half_mask((BLK, 128), 0)
        for kvh in range(NH // 4):
            ch, off = kvh // 2, kvh % 2
            kb = _both_halves(kcat[:, ch * 128:(ch + 1) * 128], off).astype(bf16)
            vb = _both_halves(vcat[:, ch * 128:(ch + 1) * 128], off).astype(bf16)
            q4 = _stack_bf16(_group_heads(q_ref, kvh, HD ** -0.5))
            s4 = lax.dot_general(q4, kb, _DIMS["nt"], preferred_element_type=f32)
            ps, invs = [], []
            for i in range(4):
                h = 4 * kvh + i
                p, inv, _ = _attn_softmax(s4[i * BLK:(i + 1) * BLK], base, _SLOPES[h], sink_ref[0, h])
                ps.append(p)
                invs.append(inv)
            o4 = jnp.dot(_stack_bf16(ps), vb, preferred_element_type=f32)
            for pr in range(2):
                lo = o4[(2 * pr) * BLK:(2 * pr + 1) * BLK] * invs[2 * pr]
                hi = o4[(2 * pr + 1) * BLK:(2 * pr + 2) * BLK] * invs[2 * pr + 1]
                pair = 2 * kvh + pr
                o_ref[:, pair * 128:(pair + 1) * 128] = jnp.where(even, lo, hi).astype(o_ref.dtype)

    return pl.pallas_call(
        body, name="attn_fwd", grid=(nb,),
        in_specs=[pl.BlockSpec(memory_space=pltpu.SMEM), q_spec] + k_specs + v_specs,
        out_specs=pl.BlockSpec((BLK, D), lambda n: (n, 0)), out_shape=_sds((S, D), bf16),
        compiler_params=_params(1, True))(sink, proj, proj, proj, proj, proj, proj, proj)


def _attn_bwd(proj, sink, y_b, dy_b):
    S = proj.shape[0]
    nb, q_spec, k_specs, v_specs = _attn_specs(S)

    def body(sink_ref, q_ref, kp_ref, kc_ref, kn_ref, vp_ref, vc_ref, vn_ref, o_ref, do_ref,
             dq_ref, dk_ref, dv_ref, dsink_ref):
        n = pl.program_id(0)

        @pl.when(n == 0)
        def _():
            dk_ref[...] = jnp.zeros_like(dk_ref)
            dv_ref[...] = jnp.zeros_like(dv_ref)
            dsink_ref[...] = jnp.zeros_like(dsink_ref)

        base = _attn_base(n, S)
        kcat = jnp.concatenate([kp_ref[...], kc_ref[...], kn_ref[...]], axis=0).astype(f32)
        vcat = jnp.concatenate([vp_ref[...], vc_ref[...], vn_ref[...]], axis=0).astype(f32)
        dk_rows, dv_rows = [[], []], [[], []]
        scale = HD ** -0.5
        even = _half_mask((BLK, 128), 0)
        for kvh in range(NH // 4):
            ch, off = kvh // 2, kvh % 2
            kb = _both_halves(kcat[:, ch * 128:(ch + 1) * 128], off).astype(bf16)
            vb = _both_halves(vcat[:, ch * 128:(ch + 1) * 128], off).astype(bf16)
            q_parts = _group_heads(q_ref, kvh, scale)
            d_parts = _group_heads(do_ref, kvh, 1.0)
            s4 = lax.dot_general(_stack_bf16(q_parts), kb, _DIMS["nt"], preferred_element_type=f32)
            dp4 = lax.dot_general(_stack_bf16(d_parts), vb, _DIMS["nt"], preferred_element_type=f32)
            ts, ps, qn, dn, invs = [], [], [], [], []
            for i in range(4):
                h = 4 * kvh + i
                pair = 2 * kvh + i // 2
                rows = slice(i * BLK, (i + 1) * BLK)
                p, inv, psink = _attn_softmax(s4[rows], base, _SLOPES[h], sink_ref[0, h])
                delta = jnp.sum(d_parts[i] * o_ref[:, pair * 128:(pair + 1) * 128].astype(f32), axis=-1, keepdims=True)
                dsink_ref[h:h + 1, :] += jnp.broadcast_to(-jnp.sum(psink * delta, axis=0, keepdims=True), (1, 128))
                ts.append(p * (dp4[rows] - delta))
                ps.append(p)
                qn.append(q_parts[i] * inv)
                dn.append(d_parts[i] * inv)
                invs.append(inv)
            t4 = _stack_bf16(ts)
            dq4 = jnp.dot(t4, kb, preferred_element_type=f32)
            for pr in range(2):
                lo = dq4[(2 * pr) * BLK:(2 * pr + 1) * BLK] * invs[2 * pr]
                hi = dq4[(2 * pr + 1) * BLK:(2 * pr + 2) * BLK] * invs[2 * pr + 1]
                pair = 2 * kvh + pr
                dq_ref[:, pair * 128:(pair + 1) * 128] = (jnp.where(even, lo, hi) * scale).astype(dq_ref.dtype)
            dk_t = lax.dot_general(_stack_bf16(qn), t4, _DIMS["tn"], preferred_element_type=f32)
            dv_t = lax.dot_general(_stack_bf16(dn), _stack_bf16(ps), _DIMS["tn"], preferred_element_type=f32)
            dk_rows[ch].append(dk_t[0:HD] + dk_t[HD:2 * HD])
            dv_rows[ch].append(dv_t[0:HD] + dv_t[HD:2 * HD])
        dk_acc = [jnp.concatenate(r, axis=0).T for r in dk_rows]
        dv_acc = [jnp.concatenate(r, axis=0).T for r in dv_rows]
        for j in range(3):
            blk = n + (j - 1)

            @pl.when((blk >= 0) & (blk < nb))
            def _():
                rows = pl.ds(pl.multiple_of(blk * BLK, BLK), BLK)
                for ch in range(2):
                    dk_ref[rows, ch * 128:(ch + 1) * 128] += dk_acc[ch][j * BLK:(j + 1) * BLK]
                    dv_ref[rows, ch * 128:(ch + 1) * 128] += dv_acc[ch][j * BLK:(j + 1) * BLK]

    row_blk = pl.BlockSpec((BLK, D), lambda n: (n, 0))
    full = pl.BlockSpec((S, 256), lambda n: (0, 0))
    return pl.pallas_call(
        body, name="attn_bwd", grid=(nb,),
        in_specs=[pl.BlockSpec(memory_space=pltpu.SMEM), q_spec] + k_specs + v_specs + [row_blk, row_blk],
        out_specs=[row_blk, full, full, pl.BlockSpec((NH, 128), lambda n: (0, 0))],
        out_shape=[_sds((S, D), bf16), _sds((S, 256), f32), _sds((S, 256), f32), _sds((NH, 128), f32)],
        compiler_params=_params(1, True))(sink, proj, proj, proj, proj, proj, proj, proj, y_b, dy_b)


def _adamw(name, w, g, m, v, tr):
    R, C = w.shape
    tr = min(tr, R)

    def body(w_ref, g_ref, m_ref, v_ref, d_ref, m2_ref, v2_ref):
        g = g_ref[...]
        m2 = ADAM_B1 * m_ref[...] + (1.0 - ADAM_B1) * g
        v2 = ADAM_B2 * v_ref[...] + (1.0 - ADAM_B2) * (g * g)
        m_hat = m2 / (1.0 - ADAM_B1 ** ADAM_STEP)
        v_hat = v2 / (1.0 - ADAM_B2 ** ADAM_STEP)
        d_ref[...] = -ADAM_LR * (m_hat / (jnp.sqrt(v_hat) + ADAM_EPS) + ADAM_WD * w_ref[...])
        m2_ref[...] = m2
        v2_ref[...] = v2

    blk = pl.BlockSpec((tr, C), lambda i: (i, 0))
    return pl.pallas_call(body, name=name, grid=(R // tr,), in_specs=[blk] * 4, out_specs=[blk] * 3,
                          out_shape=[_sds((R, C), f32)] * 3, compiler_params=_params(1))(w, g, m, v)


def _pair_sum(name, c_arr, g4, recv, th):
    _, _, h, w = g4.shape
    th = min(th, h)

    def body(c_ref, g_ref, r_ref, o_ref, ob_ref):
        p = g_ref[...] + r_ref[...]
        o_ref[...] = p
        ob_ref[...] = p.astype(bf16)

    blk = pl.BlockSpec((None, th, w), lambda s, i, c_ref: (s, i, 0))
    spec = pltpu.PrefetchScalarGridSpec(
        num_scalar_prefetch=1, grid=(NCHIP, h // th),
        in_specs=[pl.BlockSpec((None, None, th, w), lambda s, i, c_ref: (s, c_ref[0], i, 0)), blk],
        out_specs=[blk, blk])
    return pl.pallas_call(body, name=name, grid_spec=spec,
                          out_shape=[_sds((NCHIP, h, w), f32), _sds((NCHIP, h, w), bf16)],
                          compiler_params=_params(2))(c_arr, g4, recv)


def _chip_sum(name, chip_arr, own4, recv3, th):
    _, h, w = own4.shape
    th = min(th, h)

    def body(s_ref, o_ref, r_ref, out_ref):
        out_ref[...] = ((o_ref[...] + r_ref[0].astype(f32)) + r_ref[1].astype(f32)) + r_ref[2].astype(f32)

    spec = pltpu.PrefetchScalarGridSpec(
        num_scalar_prefetch=1, grid=(h // th,),
        in_specs=[pl.BlockSpec((None, th, w), lambda i, s_ref: (s_ref[0], i, 0)),
                  pl.BlockSpec((3, th, w), lambda i, s_ref: (0, i, 0))],
        out_specs=pl.BlockSpec((th, w), lambda i, s_ref: (i, 0)))
    return pl.pallas_call(body, name=name, grid_spec=spec, out_shape=_sds((h, w), f32),
                          compiler_params=_params(1, True))(chip_arr, own4, recv3)


def _adamw_halves(name, c_arr, w, g_own, g_recv, m, v, th):
    h, wd = g_own.shape
    th = min(th, h)

    def body(c_ref, w_ref, go_ref, gr_ref, m_ref, v_ref, g_ref, d_ref, m2_ref, v2_ref):
        g = jnp.where(c_ref[0] == pl.program_id(0), go_ref[...], gr_ref[...])
        m2 = ADAM_B1 * m_ref[...] + (1.0 - ADAM_B1) * g
        v2 = ADAM_B2 * v_ref[...] + (1.0 - ADAM_B2) * (g * g)
        m_hat = m2 / (1.0 - ADAM_B1 ** ADAM_STEP)
        v_hat = v2 / (1.0 - ADAM_B2 ** ADAM_STEP)
        g_ref[...] = g
        d_ref[...] = -ADAM_LR * (m_hat / (jnp.sqrt(v_hat) + ADAM_EPS) + ADAM_WD * w_ref[...])
        m2_ref[...] = m2
        v2_ref[...] = v2

    nt = h // th
    full = pl.BlockSpec((th, wd), lambda hh, i, c_ref: (hh * nt + i, 0))
    half = pl.BlockSpec((th, wd), lambda hh, i, c_ref: (i, 0))
    spec = pltpu.PrefetchScalarGridSpec(num_scalar_prefetch=1, grid=(2, nt),
                                        in_specs=[full, half, half, full, full], out_specs=[full] * 4)
    return pl.pallas_call(body, name=name, grid_spec=spec, out_shape=[_sds((2 * h, wd), f32)] * 4,
                          compiler_params=_params(2))(c_arr, w, g_own, g_recv, m, v)


def _add2(name, a, b):
    def body(a_ref, b_ref, o_ref):
        o_ref[...] = a_ref[...] + b_ref[...]
    return pl.pallas_call(body, name=name, out_shape=_sds(a.shape, f32))(a, b)


def _sum4(name, b4, th):
    _, h, w = b4.shape
    th = min(th, h)

    def body(b_ref, o_ref):
        o_ref[...] = ((b_ref[0] + b_ref[1]) + b_ref[2]) + b_ref[3]

    return pl.pallas_call(body, name=name, grid=(h // th,),
                          in_specs=[pl.BlockSpec((NCHIP, th, w), lambda i: (0, i, 0))],
                          out_specs=pl.BlockSpec((th, w), lambda i: (i, 0)), out_shape=_sds((h, w), f32),
                          compiler_params=_params(1, True))(b4)


def _coords():
    x, y, c = lax.axis_index("x"), lax.axis_index("y"), lax.axis_index("c")
    return x, y, c, [(1 - x, y), (x, 1 - y), (1 - x, 1 - y)]


def _gather_chips(arrs):
    n = len(arrs)

    def body(*refs):
        ins, outs = refs[:n], refs[n:2 * n]
        send_sems, recv_sems, local_sems = refs[2 * n:2 * n + 3]
        stage = refs[2 * n + 3:]
        x, y, c, chips = _coords()
        s = 2 * x + y
        sib = (x, y, 1 - c)
        load = [pltpu.make_async_copy(ins[a], stage[a], local_sems.at[a]) for a in range(n)]
        local = [pltpu.make_async_copy(stage[a], outs[a].at[s], local_sems.at[n + a]) for a in range(n)]
        for cp in load:
            cp.start()

        def over_ici(k, a, slot, peer):
            return pltpu.make_async_remote_copy(src_ref=ins[a].at[c], dst_ref=outs[a].at[slot, c], send_sem=send_sems.at[k * n + a],
                                                recv_sem=recv_sems.at[k * n + a], device_id=peer, device_id_type=MESH)

        def to_sibling(k, a, slot, half):
            i = (3 + k) * n + a
            return pltpu.make_async_remote_copy(src_ref=outs[a].at[slot, half], dst_ref=outs[a].at[slot, half], send_sem=send_sems.at[i],
                                                recv_sem=recv_sems.at[i], device_id=sib, device_id_type=MESH)

        sends = [over_ici(k, a, s, (px, py, c)) for k, (px, py) in enumerate(chips) for a in range(n)]
        for cp in sends:
            cp.start()
        for a in range(n):
            load[a].wait()
            local[a].start()
        passed = []
        for k, (px, py) in enumerate(chips):
            for a in range(n):
                over_ici(k, a, 2 * px + py, (px, py, c)).wait_recv()
                cp = to_sibling(k, a, 2 * px + py, c)
                cp.start()
                passed.append(cp)
        for k, (px, py) in enumerate(chips):
            for a in range(n):
                to_sibling(k, a, 2 * px + py, 1 - c).wait_recv()
        for cp in sends + passed:
            cp.wait_send()
        for cp in local:
            cp.wait()

    return pl.pallas_call(
        body, name="gather_weights", in_specs=[ANY] * n, out_specs=[ANY] * n,
        out_shape=[_sds((NCHIP,) + a.shape, a.dtype) for a in arrs],
        scratch_shapes=[pltpu.SemaphoreType.DMA((6 * n,)), pltpu.SemaphoreType.DMA((6 * n,)), pltpu.SemaphoreType.DMA((2 * n,))]
        + [pltpu.VMEM(a.shape, a.dtype) for a in arrs],
        compiler_params=pltpu.CompilerParams(vmem_limit_bytes=VMEM_LIMIT),
    )(*arrs)


HBM = pl.BlockSpec(memory_space=pltpu.HBM)
SEM = pl.BlockSpec(memory_space=pltpu.SEMAPHORE)
EFFECT = pltpu.SideEffectType.DATAFLOW_SIDE_EFFECTING


def _split_start(name, n_copies, make_copies, ins, land_shapes, after):
    ni, nl = len(ins), len(land_shapes)

    def body(*refs):
        in_refs, land_refs = refs[:ni], refs[ni:ni + nl]
        send_sems, recv_sems = refs[ni + nl + 1], refs[ni + nl + 2]
        token = refs[-1]
        for cp in make_copies(in_refs, land_refs, send_sems, recv_sems):
            cp.start()
        token[...] = jnp.zeros_like(token)

    lands = [pltpu.with_memory_space_constraint(lax.empty(s.shape, s.dtype), pltpu.HBM) for s in land_shapes]
    res = pl.pallas_call(
        body, name=name,
        out_shape=(pltpu.SemaphoreType.DMA((n_copies,)), pltpu.SemaphoreType.DMA((n_copies,)),
                   *[pltpu.HBM(a.shape, a.dtype) for a in ins], *[pltpu.HBM(s.shape, s.dtype) for s in land_shapes],
                   _sds((8, 128), f32)),
        in_specs=[HBM] * (ni + nl) + [ANY], out_specs=(SEM, SEM, *[HBM] * (ni + nl), pl.BlockSpec(memory_space=pltpu.VMEM)),
        input_output_aliases={i: 2 + i for i in range(ni + nl)},
        compiler_params=pltpu.CompilerParams(has_side_effects=EFFECT),
    )(*[pltpu.with_memory_space_constraint(a, pltpu.HBM) for a in ins], *lands, after)
    return res[0], res[1], list(res[2:2 + ni]), list(res[2 + ni:2 + ni + nl]), res[-1]


def _split_wait(name, make_copies, send_sems, recv_sems, ins, lands, after):
    ni, nl = len(ins), len(lands)

    def body(*refs):
        in_refs, land_refs = refs[:ni], refs[ni:ni + nl]
        s_sems, r_sems = refs[ni + nl], refs[ni + nl + 1]
        for cp in make_copies(in_refs, land_refs, s_sems, r_sems):
            cp.wait_send()
            cp.wait_recv()

    res = pl.pallas_call(
        body, name=name, out_shape=tuple(pltpu.HBM(a.shape, a.dtype) for a in ins + lands),
        in_specs=[HBM] * (ni + nl) + [SEM, SEM, ANY], out_specs=tuple([HBM] * (ni + nl)),
        input_output_aliases={i: i for i in range(ni + nl)},
        compiler_params=pltpu.CompilerParams(has_side_effects=EFFECT),
    )(*ins, *lands, send_sems, recv_sems, after)
    return list(res[:ni]), list(res[ni:])


def _gather_copies(n):
    def make(in_refs, land_refs, send_sems, recv_sems):
        x, y, c, chips = _coords()
        s = 2 * x + y
        return [pltpu.make_async_remote_copy(src_ref=in_refs[a], dst_ref=land_refs[a].at[s], send_sem=send_sems.at[k * n + a],
                                             recv_sem=recv_sems.at[k * n + a], device_id=(px, py, c), device_id_type=MESH)
                for k, (px, py) in enumerate(chips) for a in range(n)]
    return make


def _sibling_half_copies(n):
    def make(in_refs, land_refs, send_sems, recv_sems):
        x, y, c, _ = _coords()
        return [pltpu.make_async_remote_copy(src_ref=in_refs[a].at[:, 1 - c], dst_ref=land_refs[a], send_sem=send_sems.at[a],
                                             recv_sem=recv_sems.at[a], device_id=(x, y, 1 - c), device_id_type=MESH)
                for a in range(n)]
    return make


def _chip_part_copies(n):
    def make(in_refs, land_refs, send_sems, recv_sems):
        x, y, c, chips = _coords()
        return [pltpu.make_async_remote_copy(src_ref=in_refs[a].at[2 * px + py], dst_ref=land_refs[a].at[k],
                                             send_sem=send_sems.at[k * n + a], recv_sem=recv_sems.at[k * n + a],
                                             device_id=(px, py, c), device_id_type=MESH)
                for k, (px, py) in enumerate(chips) for a in range(n)]
    return make


def _sibling_whole_copies(n):
    def make(in_refs, land_refs, send_sems, recv_sems):
        x, y, c, _ = _coords()
        return [pltpu.make_async_remote_copy(src_ref=in_refs[a], dst_ref=land_refs[a], send_sem=send_sems.at[a],
                                             recv_sem=recv_sems.at[a], device_id=(x, y, 1 - c), device_id_type=MESH)
                for a in range(n)]
    return make


def _place_own(chip_arr, owns, lands, steps):
    n = len(owns)

    def body(s_ref, *refs):
        for a in range(n):
            refs[2 * n + a][...] = refs[a][...]

    tiles = [o.shape[0] // steps for o in owns]
    spec = pltpu.PrefetchScalarGridSpec(
        num_scalar_prefetch=1, grid=(steps,),
        in_specs=[pl.BlockSpec((t, o.shape[1]), lambda i, s_ref: (i, 0)) for t, o in zip(tiles, owns)] + [ANY] * n,
        out_specs=[pl.BlockSpec((None, t, o.shape[1]), lambda i, s_ref: (s_ref[0], i, 0)) for t, o in zip(tiles, owns)])
    return pl.pallas_call(body, name="place_own", grid_spec=spec, out_shape=[_sds(l.shape, l.dtype) for l in lands],
                          input_output_aliases={1 + n + a: a for a in range(n)},
                          compiler_params=_params(1))(chip_arr, *owns, *lands)


def _sibling_halves(g4s, small):
    n = len(g4s)

    def body(*refs):
        ins, small_ref = refs[:n], refs[n]
        outs, small_out = refs[n + 1:2 * n + 1], refs[2 * n + 1]
        send_sems, recv_sems = refs[2 * n + 2:]
        x, y, c, _ = _coords()
        sib = (x, y, 1 - c)

        def remote(a, half):
            src = small_ref if a == n else ins[a].at[:, half]
            dst = small_out if a == n else outs[a]
            return pltpu.make_async_remote_copy(src_ref=src, dst_ref=dst, send_sem=send_sems.at[a], recv_sem=recv_sems.at[a],
                                                device_id=sib, device_id_type=MESH)

        sends = [remote(a, 1 - c) for a in range(n + 1)]
        for cp in sends:
            cp.start()
        for a in range(n + 1):
            remote(a, c).wait_recv()
        for cp in sends:
            cp.wait_send()

    return pl.pallas_call(
        body, name="reduce_sibling", in_specs=[ANY] * (n + 1), out_specs=[ANY] * (n + 1),
        out_shape=[_sds((g.shape[0],) + g.shape[2:], f32) for g in g4s] + [_sds(small.shape, f32)],
        scratch_shapes=[pltpu.SemaphoreType.DMA((n + 1,)), pltpu.SemaphoreType.DMA((n + 1,))],
    )(*g4s, small)


def _exchange_chips(parts, small2):
    n = len(parts)

    def body(*refs):
        ins, small_ref = refs[:n], refs[n]
        outs, small_out = refs[n + 1:2 * n + 1], refs[2 * n + 1]
        send_sems, recv_sems, local_sem = refs[2 * n + 2:]
        x, y, c, chips = _coords()
        s = 2 * x + y
        local = pltpu.make_async_copy(small_ref.at[c], small_out.at[s], local_sem)
        local.start()

        def remote(k, a, dest_chip, small_slot, peer):
            if a == n:
                src, dst = small_ref.at[c], small_out.at[small_slot]
            else:
                src, dst = ins[a].at[dest_chip], outs[a].at[k]
            i = k * (n + 1) + a
            return pltpu.make_async_remote_copy(src_ref=src, dst_ref=dst, send_sem=send_sems.at[i], recv_sem=recv_sems.at[i],
                                                device_id=peer, device_id_type=MESH)

        sends = [remote(k, a, 2 * px + py, s, (px, py, c)) for k, (px, py) in enumerate(chips) for a in range(n + 1)]
        for cp in sends:
            cp.start()
        for k, (px, py) in enumerate(chips):
            for a in range(n + 1):
                remote(k, a, s, 2 * px + py, (px, py, c)).wait_recv()
        for cp in sends:
            cp.wait_send()
        local.wait()

    m = 3 * (n + 1)
    return pl.pallas_call(
        body, name="reduce_chips", in_specs=[ANY] * (n + 1), out_specs=[ANY] * (n + 1),
        out_shape=[_sds((3,) + p.shape[1:], p.dtype) for p in parts] + [_sds((NCHIP,) + small2.shape[1:], f32)],
        scratch_shapes=[pltpu.SemaphoreType.DMA((m,)), pltpu.SemaphoreType.DMA((m,)), pltpu.SemaphoreType.DMA],
    )(*parts, small2)


def _share_sibling(halves):
    n = len(halves)

    def body(*refs):
        ins, outs = refs[:n], refs[n:2 * n]
        send_sems, recv_sems = refs[2 * n:]
        x, y, c, _ = _coords()
        sib = (x, y, 1 - c)
        sends = [pltpu.make_async_remote_copy(src_ref=ins[a], dst_ref=outs[a], send_sem=send_sems.at[a], recv_sem=recv_sems.at[a],
                                              device_id=sib, device_id_type=MESH) for a in range(n)]
        for cp in sends:
            cp.start()
        for cp in sends:
            cp.wait()

    return pl.pallas_call(
        body, name="reduce_share", in_specs=[ANY] * n, out_specs=[ANY] * n,
        out_shape=[_sds(h.shape, f32) for h in halves],
        scratch_shapes=[pltpu.SemaphoreType.DMA((n,)), pltpu.SemaphoreType.DMA((n,))],
    )(*halves)


def _block_diag_pairs(w):
    w = w.reshape(NCH, 2, HD, HD)
    z = jnp.zeros((NCH, HD, HD), w.dtype)
    return jnp.concatenate([jnp.concatenate([w[:, 0], z], axis=2), jnp.concatenate([z, w[:, 1]], axis=2)], axis=1)


def _diag_blocks(m):
    return jnp.stack([m[:, :HD, :HD], m[:, HD:, HD:]], axis=1).reshape(NH, HD, HD)


def _pack(vs, rows):
    flat = jnp.concatenate([v.reshape(-1) for v in vs])
    return jnp.pad(flat, (0, rows * 128 - flat.shape[0])).reshape(rows, 128)


def _unpack(packed, shapes):
    flat = packed.reshape(-1)
    out, off = [], 0
    for shp in shapes:
        size = math.prod(shp)
        out.append(flat[off:off + size].reshape(shp))
        off += size
    return out


def _rows_for(sizes, multiple):
    rows = -(-sum(sizes) // 128)
    return -(-rows // multiple) * multiple


def kernel(x, norm_mix_g, w_in, b_gate, conv_w, conv_b, lru_lambda, lru_wa, lru_ba, lru_wx, lru_bx, attn_sink, w_out, norm_ffn_g, w_ffn_in, w_ffn_out, norm_final_g, loss_target, m_norm_mix_g, m_w_in, m_b_gate, m_conv_w, m_conv_b, m_lru_lambda, m_lru_wa, m_lru_ba, m_lru_wx, m_lru_bx, m_attn_sink, m_w_out, m_norm_ffn_g, m_w_ffn_in, m_w_ffn_out, m_norm_final_g, v_norm_mix_g, v_w_in, v_b_gate, v_conv_w, v_conv_b, v_lru_lambda, v_lru_wa, v_lru_ba, v_lru_wx, v_lru_bx, v_attn_sink, v_w_out, v_norm_ffn_g, v_w_ffn_in, v_w_ffn_out, v_norm_final_g):
    S = x.shape[1]
    xs = x[0]
    tgt = loss_target[0]
    cx, cy, cc = lax.axis_index("x"), lax.axis_index("y"), lax.axis_index("c")
    chip = 2 * cx + cy
    SW = D // NCHIP

    small_shard = _pack([conv_w[0], lru_lambda[0], lru_ba[0], lru_bx[0]], 32)
    halves_of = lambda a: a.reshape(2, a.shape[0] // 2, a.shape[1])
    w_in_g, small_g = _gather_chips([halves_of(w_in[0].astype(bf16)), halves_of(small_shard)])
    w_in_g = w_in_g.reshape(NCHIP, D, SHW)
    small_g = small_g.reshape(NCHIP, 32, 128)
    late = [w_ffn_in[0].astype(bf16), w_out[0].astype(bf16), w_ffn_out[0].astype(bf16)]
    late_send, late_recv, late_src, late_land, late_token = _split_start(
        "gather_late_start", 9, _gather_copies(3), late, [_sds((NCHIP,) + a.shape, bf16) for a in late], small_g)
    small_parts = [_unpack(small_g[s], [(4, SW), (2, SW), (2, SW), (2, SW)]) for s in range(NCHIP)]
    conv_w_f, lam_f, ba_f, bx_f = [jnp.concatenate([small_parts[s][p] for s in range(NCHIP)], axis=1) for p in range(4)]
    wbd = jnp.concatenate([_block_diag_pairs(lru_wa[0, 0]), _block_diag_pairs(lru_wx[0, 0]),
                           _block_diag_pairs(lru_wa[0, 1]), _block_diag_pairs(lru_wx[0, 1])], axis=2).astype(bf16)
    conv_b_f = conv_b
    sink = attn_sink

    xn, proj = _rms_matmul("rms_proj", xs, norm_mix_g + late_token[0:1, 0:1], w_in_g, 1024)
    y_a, lru_state = _lru_fwd(proj, conv_w_f, conv_b_f, lam_f, ba_f, bx_f, wbd)
    y_b = _attn_fwd(proj, sink)
    merged = _merge_fwd(proj, b_gate, y_a, y_b, 1024)
    late_src, late_land = _split_wait("gather_late_wait", _gather_copies(3), late_send, late_recv, late_src, late_land, merged)
    chip_arr = chip.reshape(1).astype(jnp.int32)
    w_ffn_in_g, w_out_g, w_ffn_out_g = _place_own(chip_arr, late_src, late_land, 4)
    w_out_f = w_out_g.reshape(D, D)
    w_ffn_out_f = w_ffn_out_g.reshape(FF, D)
    x1 = _mm_residual("out_proj", merged, w_out_f, xs, 512)
    xn2, gu, act = _rms_matmul_swiglu("rms_ffn_in", x1, norm_ffn_g, w_ffn_in_g, 1024)
    x2 = _mm_residual("ffn_out", act, w_ffn_out_f, x1, 512)
    dx2, loss_row, dg3 = _final_loss_bwd(x2, norm_final_g.reshape(1, D), tgt, 512)

    tm = min(1024, S)
    tk = min(2048, S)
    gw_ffn_out = _mm_tn("dw_ffn_out", act, pl.BlockSpec((tk, SHW), lambda i, k: (k, i)),
                        dx2, pl.BlockSpec((tk, D), lambda i, k: (k, 0)),
                        _sds((FF, D), f32), pl.BlockSpec((SHW, D), lambda i, k: (i, 0)), (2, S // tk), (SHW, D))
    dgu = _swiglu_bwd(dx2, w_ffn_out_f, gu, 256)
    dxn2 = _mm_nt_groups("dxn2", dgu, pl.BlockSpec((None, tm, SHW), lambda i, g: (g // 2, i, g % 2)), w_ffn_in_g, S, tm)
    gw_ffn_in = _mm_tn("dw_ffn_in", xn2, pl.BlockSpec((tk, D), lambda g, k: (k, 0)),
                       dgu, pl.BlockSpec((None, tk, SHW), lambda g, k: (g // 2, k, g % 2)),
                       _sds((NCHIP, D, SHW), f32), pl.BlockSpec((None, D, SHW), lambda g, k: (g, 0, 0)),
                       (NCHIP, S // tk), (D, SHW))
    c_arr = cc.reshape(1).astype(jnp.int32)
    early_names, early_tiles = ["w_ffn_in", "w_ffn_out"], [256, 352]
    early = [gw_ffn_in.reshape(NCHIP, 2, D // 2, SHW), gw_ffn_out.reshape(NCHIP, 2, FF // NCHIP // 2, D)]
    ea_send, ea_recv, ea_src, ea_land, ea_token = _split_start(
        "reduce_early_sibling_start", 2, _sibling_half_copies(2), early,
        [_sds((NCHIP,) + g.shape[2:], f32) for g in early], dxn2)
    dx1, dg2 = _rms_bwd("rms_ffn_bwd", x1, norm_ffn_g + ea_token[0:1, 0:1], dxn2, dx2, 512)

    dmerged = _mm_nt_resident("d_merged", dx1, w_out_f, 512)
    gw_out = _mm_tn("dw_out", merged, pl.BlockSpec((tk, D), lambda i, k: (k, 0)),
                    dx1, pl.BlockSpec((tk, D), lambda i, k: (k, 0)),
                    _sds((D, D), f32), pl.BlockSpec((D, D), lambda i, k: (0, 0)), (1, S // tk), (D, D))
    dz0, dz1, dy_a, dy_b, db0, db1 = _merge_bwd(proj, b_gate, y_a, y_b, dmerged, 1024)
    ea_src, ea_land = _split_wait("reduce_early_sibling_wait", _sibling_half_copies(2), ea_send, ea_recv, ea_src, ea_land, dy_b)
    early_pairs = [_pair_sum("pair_sum_" + nm, c_arr, g4, r, th)
                   for nm, g4, r, th in zip(early_names, ea_src, ea_land, early_tiles)]
    eb_send, eb_recv, eb_src, eb_land, eb_token = _split_start(
        "reduce_early_chips_start", 6, _chip_part_copies(2), [p[1] for p in early_pairs],
        [_sds((3,) + p[1].shape[1:], bf16) for p in early_pairs], early_pairs[0][0])
    dq, dk, dv, dsink = _attn_bwd(proj, sink + eb_token[0:1, 0:1], y_b, dy_b)
    _, eb_land = _split_wait("reduce_early_chips_wait", _chip_part_copies(2), eb_send, eb_recv, eb_src, eb_land, dq)
    early_halves = [_chip_sum("chip_sum_" + nm, chip_arr, p[0], r3, th)
                    for nm, p, r3, th in zip(early_names, early_pairs, eb_land, early_tiles)]
    ec_send, ec_recv, ec_src, ec_land, ec_token = _split_start(
        "reduce_early_share_start", 2, _sibling_whole_copies(2), early_halves, [_sds(h.shape, f32) for h in early_halves], dq)
    du, dgl, dcw, dcb, dlam, dba, dbx, dwbd = _lru_bwd(proj, dy_a, lru_state, conv_w_f, conv_b_f + ec_token[0:1, 0:1], lam_f, ba_f, bx_f, wbd)
    early_halves, early_other = _split_wait("reduce_early_share_wait", _sibling_whole_copies(2), ec_send, ec_recv, ec_src, ec_land, du)
    dproj = jnp.concatenate([du, dgl, dq, dk.astype(bf16), dv.astype(bf16), dz0, dz1], axis=1)
    gw_in = _mm_tn("dw_in", xn, pl.BlockSpec((tk, D), lambda g, k: (k, 0)),
                   dproj, pl.BlockSpec((tk, SHW), lambda g, k: (k, g)),
                   _sds((NCHIP, D, SHW), f32), pl.BlockSpec((None, D, SHW), lambda g, k: (g, 0, 0)),
                   (NCHIP, S // tk), (D, SHW))
    wa_send, wa_recv, wa_src, wa_land, wa_token = _split_start(
        "reduce_w_in_sibling_start", 1, _sibling_half_copies(1), [gw_in.reshape(NCHIP, 2, D // 2, SHW)],
        [_sds((NCHIP, D // 2, SHW), f32)], dproj)
    dxn =_mm_nt_groups("dxn", dproj, pl.BlockSpec((tm, SHW), lambda i, g: (i, g)), w_in_g, S, tm)
    wa_src, wa_land = _split_wait("reduce_w_in_sibling_wait", _sibling_half_copies(1), wa_send, wa_recv, wa_src, wa_land, dxn)
    w_in_pair = _pair_sum("pair_sum_w_in", c_arr, wa_src[0], wa_land[0], 256)
    wb_send, wb_recv, wb_src, wb_land, wb_token = _split_start(
        "reduce_w_in_chips_start", 3, _chip_part_copies(1), [w_in_pair[1]], [_sds((3, D // 2, SHW), bf16)], w_in_pair[0])
    grad_x, dg1 = _rms_bwd("rms_mix_bwd", xs, norm_mix_g + wb_token[0:1, 0:1], dxn, dx1, 512)
    _, wb_land = _split_wait("reduce_w_in_chips_wait", _chip_part_copies(1), wb_send, wb_recv, wb_src, wb_land, dg1)
    w_in_half = _chip_sum("chip_sum_w_in", chip_arr, w_in_pair[0], wb_land[0], 256)

    d_wa = jnp.stack([_diag_blocks(dwbd[:, :, 0:CW]), _diag_blocks(dwbd[:, :, 2 * CW:3 * CW])])
    d_wx = jnp.stack([_diag_blocks(dwbd[:, :, CW:2 * CW]), _diag_blocks(dwbd[:, :, 3 * CW:4 * CW])])
    small_full = [dg1, jnp.concatenate([db0, db1], axis=1), dcw, dcb, dlam, d_wa, dba, d_wx, dbx, dsink[:, 0], dg2, dg3,
                  loss_row[0, 0:1]]
    full_shapes = [(1, D), (1, 2 * D), (4, D), (1, D), (2, D), (2, NH, HD, HD), (2, D), (2, NH, HD, HD), (2, D), (NH,),
                   (1, D), (1, D), (1,)]
    rows_full = _rows_for([math.prod(s) for s in full_shapes], 16)
    small_vec = _pack(small_full, rows_full)

    late_names, late_tiles = ["w_in", "w_out"], [256, 128]
    big = [gw_out.reshape(NCHIP, 2, D // NCHIP // 2, D)]
    *recv_a, small_sib = _sibling_halves(big, small_vec)
    w_out_pair = _pair_sum("pair_sum_w_out", c_arr, big[0], recv_a[0], 128)
    small_chip = _add2("pair_sum_small", small_vec, small_sib).reshape(2, rows_full // 2, 128)
    *recv_b, small_all = _exchange_chips([w_out_pair[1]], small_chip)
    w_out_half = _chip_sum("chip_sum_w_out", chip_arr, w_out_pair[0], recv_b[0], 128)
    halves = [w_in_half, w_out_half, _sum4("chip_sum_small", small_all, rows_full // 2)]
    *recv_c, small_other = _share_sibling(halves)
    small_lo = jnp.where(cc == 0, halves[2], small_other)
    small_hi = jnp.where(cc == 0, small_other, halves[2])
    g_full = _unpack(jnp.concatenate([small_lo, small_hi], axis=0), full_shapes)

    out_big = {}
    for nm, w, g_own, g_recv, m, v, th in zip(late_names + early_names, [w_in, w_out, w_ffn_in, w_ffn_out],
                                              halves[:2] + early_halves, recv_c + early_other,
                                              [m_w_in, m_w_out, m_w_ffn_in, m_w_ffn_out],
                                              [v_w_in, v_w_out, v_w_ffn_in, v_w_ffn_out], late_tiles + early_tiles):
        g_, d_, m_, v_ = _adamw_halves("adamw_" + nm, c_arr, w[0], g_own, g_recv, m[0], v[0], th)
        out_big[nm] = (g_[None], d_[None], m_[None], v_[None])

    small_names = ["norm_mix_g", "b_gate", "conv_w", "conv_b", "lru_lambda", "lru_wa", "lru_ba", "lru_wx", "lru_bx", "attn_sink",
                   "norm_ffn_g", "norm_final_g"]
    sharded = {"conv_w", "lru_lambda", "lru_ba", "lru_bx"}
    small_w = [norm_mix_g, b_gate, conv_w, conv_b, lru_lambda, lru_wa, lru_ba, lru_wx, lru_bx, attn_sink, norm_ffn_g, norm_final_g]
    small_m = [m_norm_mix_g, m_b_gate, m_conv_w, m_conv_b, m_lru_lambda, m_lru_wa, m_lru_ba, m_lru_wx, m_lru_bx, m_attn_sink,
               m_norm_ffn_g, m_norm_final_g]
    small_v = [v_norm_mix_g, v_b_gate, v_conv_w, v_conv_b, v_lru_lambda, v_lru_wa, v_lru_ba, v_lru_wx, v_lru_bx, v_attn_sink,
               v_norm_ffn_g, v_norm_final_g]
    g_local = []
    for nm, g, w in zip(small_names, g_full, small_w):
        if nm in sharded:
            g = lax.dynamic_slice_in_dim(g, chip * SW, SW, axis=1)
        g_local.append(g.reshape(w.shape))
    local_shapes = [w.shape for w in small_w]
    rows_local = _rows_for([math.prod(s) for s in local_shapes], 8)
    d_s, m_s, v_s = _adamw("adamw_small", _pack(small_w, rows_local), _pack(g_local, rows_local),
                           _pack(small_m, rows_local), _pack(small_v, rows_local), rows_local)
    d_l, m_l, v_l = _unpack(d_s, local_shapes), _unpack(m_s, local_shapes), _unpack(v_s, local_shapes)
    res = {nm: (g_local[i], d_l[i], m_l[i], v_l[i]) for i, nm in enumerate(small_names)}
    res.update(out_big)

    order = ["norm_mix_g", "w_in", "b_gate", "conv_w", "conv_b", "lru_lambda", "lru_wa", "lru_ba", "lru_wx", "lru_bx", "attn_sink",
             "w_out", "norm_ffn_g", "w_ffn_in", "w_ffn_out", "norm_final_g"]
    outs = [g_full[-1][0], grad_x[None]]
    for k in range(4):
        outs += [res[nm][k] for nm in order]
    return tuple(outs)
```

```python
import functools
import math

import jax
import jax.numpy as jnp
from jax import lax
from jax.experimental import pallas as pl
from jax.experimental.pallas import tpu as pltpu

f32 = jnp.float32
bf16 = jnp.bfloat16

D = 1024
NH = 16
HD = 64
FF = 2816
INW = 5632
NCHIP = 4
SHW = INW // NCHIP
CW = 128
NCH = D // CW
BLK = 128
EPS = 1e-6
NEG_INF = -1e30
RGLRU_C = 8.0
ADAM_LR, ADAM_B1, ADAM_B2, ADAM_EPS, ADAM_WD, ADAM_STEP = 0.001, 0.9, 0.999, 1e-08, 0.01, 10
VMEM_LIMIT = 58 * 1024 * 1024
MESH = pl.DeviceIdType.MESH
ANY = pl.BlockSpec(memory_space=pl.ANY)

COL_U, COL_G, COL_Q, COL_K, COL_V, COL_Z0, COL_Z1 = 0, 4, 8, 12, 13, 14, 18
MERGE_W = 512
MERGE_Z0, MERGE_Z1 = (COL_Z0 * 256) // MERGE_W, (COL_Z1 * 256) // MERGE_W


def _params(n_axes, vmem=False):
    return pltpu.CompilerParams(dimension_semantics=("arbitrary",) * n_axes,
                                vmem_limit_bytes=VMEM_LIMIT if vmem else None)


def _sds(shape, dtype):
    return jax.ShapeDtypeStruct(tuple(shape), dtype)


_DIMS = {"nn": (((1,), (0,)), ((), ())), "nt": (((1,), (1,)), ((), ())), "tn": (((0,), (0,)), ((), ()))}


def _mm(name, mode, a, a_spec, b, b_spec, out_shape, out_spec, grid, nk, acc_shape, add=None, add_spec=None):
    has_add = add is not None

    def body(*refs):
        a_ref, b_ref = refs[0], refs[1]
        add_ref = refs[2] if has_add else None
        o_ref = refs[2 + has_add]
        part = lax.dot_general(a_ref[...].astype(bf16), b_ref[...].astype(bf16), _DIMS[mode],
                               preferred_element_type=f32)
        if nk == 1:
            if has_add:
                part = add_ref[...] + part
            o_ref[...] = part.astype(o_ref.dtype)
            return
        acc_ref = refs[3 + has_add]
        k = pl.program_id(len(grid) - 1)

        @pl.when(k == 0)
        def _():
            acc_ref[...] = part

        @pl.when(k > 0)
        def _():
            acc_ref[...] += part

        @pl.when(k == nk - 1)
        def _():
            res = acc_ref[...]
            if has_add:
                res = add_ref[...] + res
            o_ref[...] = res.astype(o_ref.dtype)

    ins = [a, b] + ([add] if has_add else [])
    in_specs = [a_spec, b_spec] + ([add_spec] if has_add else [])
    scratch = [pltpu.VMEM(acc_shape, f32)] if nk > 1 else []
    return pl.pallas_call(body, name=name, grid=grid, in_specs=in_specs, out_specs=out_spec, out_shape=out_shape,
                          scratch_shapes=scratch, compiler_params=_params(len(grid), True))(*ins)


def _rms_matmul(name, x, g, w3, tm):
    S, K = x.shape
    G, _, Nw = w3.shape
    tm = min(tm, S)

    def body(x_ref, g_ref, w_ref, xn_ref, o_ref, xs_ref):
        @pl.when(pl.program_id(1) == 0)
        def _():
            xf = x_ref[...]
            r = lax.rsqrt(jnp.mean(xf * xf, axis=-1, keepdims=True) + EPS)
            xn = ((xf * r) * g_ref[...]).astype(bf16)
            xs_ref[...] = xn
            xn_ref[...] = xn

        o_ref[...] = jnp.dot(xs_ref[...], w_ref[...], preferred_element_type=f32).astype(bf16)

    return pl.pallas_call(
        body, name=name, grid=(S // tm, G),
        in_specs=[pl.BlockSpec((tm, K), lambda i, j: (i, 0)), pl.BlockSpec((1, K), lambda i, j: (0, 0)),
                  pl.BlockSpec((None, K, Nw), lambda i, j: (j, 0, 0))],
        out_specs=[pl.BlockSpec((tm, K), lambda i, j: (i, 0)), pl.BlockSpec((tm, Nw), lambda i, j: (i, j))],
        out_shape=[_sds((S, K), bf16), _sds((S, G * Nw), bf16)],
        scratch_shapes=[pltpu.VMEM((tm, K), bf16)], compiler_params=_params(2, True))(x, g, w3)


def _rms_matmul_swiglu(name, x, g, w3, tm):
    S, K = x.shape
    G, _, Nw = w3.shape
    tm = min(tm, S)
    half = G // 2

    def body(x_ref, g_ref, wg_ref, wu_ref, xn_ref, gu_ref, act_ref, xs_ref):
        @pl.when(pl.program_id(1) == 0)
        def _():
            xf = x_ref[...]
            r = lax.rsqrt(jnp.mean(xf * xf, axis=-1, keepdims=True) + EPS)
            xn = ((xf * r) * g_ref[...]).astype(bf16)
            xs_ref[...] = xn
            xn_ref[...] = xn

        xn = xs_ref[...]
        gate = jnp.dot(xn, wg_ref[...], preferred_element_type=f32)
        up = jnp.dot(xn, wu_ref[...], preferred_element_type=f32)
        gu_ref[0] = gate.astype(bf16)
        gu_ref[1] = up.astype(bf16)
        act_ref[...] = ((gate * _sigmoid(gate)) * up).astype(bf16)

    return pl.pallas_call(
        body, name=name, grid=(S // tm, half),
        in_specs=[pl.BlockSpec((tm, K), lambda i, j: (i, 0)), pl.BlockSpec((1, K), lambda i, j: (0, 0)),
                  pl.BlockSpec((None, K, Nw), lambda i, j: (j, 0, 0)),
                  pl.BlockSpec((None, K, Nw), lambda i, j: (half + j, 0, 0))],
        out_specs=[pl.BlockSpec((tm, K), lambda i, j: (i, 0)), pl.BlockSpec((2, tm, Nw), lambda i, j: (0, i, j)),
                   pl.BlockSpec((tm, Nw), lambda i, j: (i, j))],
        out_shape=[_sds((S, K), bf16), _sds((2, S, half * Nw), bf16), _sds((S, half * Nw), bf16)],
        scratch_shapes=[pltpu.VMEM((tm, K), bf16)], compiler_params=_params(2, True))(x, g, w3, w3)


def _mm_residual(name, a, w, res, tm):
    S, K = a.shape
    N = w.shape[1]
    tm = min(tm, S)
    return _mm(name, "nn", a, pl.BlockSpec((tm, K), lambda i: (i, 0)), w, pl.BlockSpec((K, N), lambda i: (0, 0)),
               _sds((S, N), f32), pl.BlockSpec((tm, N), lambda i: (i, 0)), (S // tm,), 1, None,
               add=res, add_spec=pl.BlockSpec((tm, N), lambda i: (i, 0)))


def _mm_nt_resident(name, a, w, tm):
    S, K = a.shape
    N = w.shape[0]
    tm = min(tm, S)
    return _mm(name, "nt", a, pl.BlockSpec((tm, K), lambda i: (i, 0)), w, pl.BlockSpec((N, K), lambda i: (0, 0)),
               _sds((S, N), f32), pl.BlockSpec((tm, N), lambda i: (i, 0)), (S // tm,), 1, None)


def _mm_nt_groups(name, a, a_spec, w3, S, tm):
    G, Dout, Kw = w3.shape
    return _mm(name, "nt", a, a_spec, w3, pl.BlockSpec((None, Dout, Kw), lambda i, g: (g, 0, 0)),
               _sds((S, Dout), f32), pl.BlockSpec((tm, Dout), lambda i, g: (i, 0)), (S // tm, G), G, (tm, Dout))


def _mm_tn(name, a, a_spec, b, b_spec, out_shape, out_spec, grid, acc_shape):
    return _mm(name, "tn", a, a_spec, b, b_spec, out_shape, out_spec, grid, grid[-1], acc_shape)


def _sigmoid(x):
    return 0.5 * jnp.tanh(0.5 * x) + 0.5


_GELU_C = math.sqrt(2.0 / math.pi)


def _gelu_and_grad(x):
    v = _GELU_C * (x + 0.044715 * (x * x * x))
    t = jnp.tanh(v)
    gl = 0.5 * x * (1.0 + t)
    dgl = 0.5 * (1.0 + t) + 0.5 * x * (1.0 - t * t) * (_GELU_C * (1.0 + 3.0 * 0.044715 * (x * x)))
    return gl, dgl


def _one_minus_exp2x(x, ex):
    y = 2.0 * x
    series = y * (1.0 + y * (0.5 + y * (1.0 / 6.0 + y * (1.0 / 24.0))))
    return jnp.where(y > -1.0 / 64.0, -series, 1.0 - ex * ex)


def _merge_fwd(proj, b_gate, y_a, y_b, tm):
    S = proj.shape[0]
    tm = min(tm, S)

    def body(z0_ref, z1_ref, b0_ref, b1_ref, ya_ref, yb_ref, o_ref):
        g0 = _sigmoid(z0_ref[...].astype(f32) + b0_ref[...])
        g1 = _sigmoid(z1_ref[...].astype(f32) + b1_ref[...])
        o_ref[...] = (g0 * ya_ref[...].astype(f32) + g1 * yb_ref[...].astype(f32)).astype(bf16)

    blk = lambda off: pl.BlockSpec((tm, MERGE_W), lambda j, i: (i, off + j))
    vec = lambda off: pl.BlockSpec((1, MERGE_W), lambda j, i: (0, off + j))
    return pl.pallas_call(body, name="merge_fwd", grid=(D // MERGE_W, S // tm),
                          in_specs=[blk(MERGE_Z0), blk(MERGE_Z1), vec(0), vec(D // MERGE_W), blk(0), blk(0)],
                          out_specs=blk(0), out_shape=_sds((S, D), bf16),
                          compiler_params=_params(2))(proj, proj, b_gate, b_gate, y_a, y_b)


def _merge_bwd(proj, b_gate, y_a, y_b, dm, tm):
    S = proj.shape[0]
    tm = min(tm, S)
    per = D // MERGE_W

    def body(z_ref, b_ref, ya_ref, yb_ref, dm_ref, dz_ref, dy_ref, db_ref):
        first = pl.program_id(0) < per
        g = _sigmoid(z_ref[...].astype(f32) + b_ref[...])
        d = dm_ref[...]
        y = jnp.where(first, ya_ref[...], yb_ref[...]).astype(f32)
        dz = (d * y) * (g * (1.0 - g))
        dz_ref[...] = dz.astype(bf16)
        dy_ref[...] = (d * g).astype(bf16)

        @pl.when(pl.program_id(1) == 0)
        def _():
            db_ref[...] = jnp.zeros_like(db_ref)

        db_ref[...] += jnp.sum(dz, axis=0, keepdims=True)

    blk = lambda col: pl.BlockSpec((tm, MERGE_W), col)
    return pl.pallas_call(
        body, name="merge_bwd", grid=(2 * per, S // tm),
        in_specs=[blk(lambda p, i: (i, MERGE_Z0 + p)), pl.BlockSpec((1, MERGE_W), lambda p, i: (0, p)),
                  blk(lambda p, i: (i, jnp.minimum(p, per - 1))), blk(lambda p, i: (i, jnp.maximum(p - per, 0))),
                  blk(lambda p, i: (i, p % per))],
        out_specs=[blk(lambda p, i: (i, MERGE_Z0 + p)), pl.BlockSpec((None, tm, MERGE_W), lambda p, i: (p // per, i, p % per)),
                   pl.BlockSpec((1, MERGE_W), lambda p, i: (0, p))],
        out_shape=[_sds((S, INW), bf16), _sds((2, S, D), bf16), _sds((1, 2 * D), f32)],
        compiler_params=_params(2))(proj, b_gate, y_a, y_b, dm)


def _swiglu_bwd(dx, w, gu, tm):
    S, K = dx.shape
    tm = min(tm, S)

    def body(dx_ref, w_ref, gu_ref, o_ref):
        d = lax.dot_general(dx_ref[...].astype(bf16), w_ref[...], _DIMS["nt"], preferred_element_type=f32)
        g = gu_ref[0].astype(f32)
        u = gu_ref[1].astype(f32)
        s = _sigmoid(g)
        o_ref[0] = ((d * u) * (s * (1.0 + g * (1.0 - s)))).astype(bf16)
        o_ref[1] = (d * (g * s)).astype(bf16)

    stacked = pl.BlockSpec((2, tm, FF), lambda i: (0, i, 0))
    return pl.pallas_call(body, name="swiglu_bwd", grid=(S // tm,),
                          in_specs=[pl.BlockSpec((tm, K), lambda i: (i, 0)), pl.BlockSpec((FF, K), lambda i: (0, 0)), stacked],
                          out_specs=stacked, out_shape=_sds((2, S, FF), bf16),
                          compiler_params=_params(1, True))(dx, w, gu)


def _final_loss_bwd(x2, g3, tgt, tm):
    S = x2.shape[0]
    tm = min(tm, S)

    def body(x_ref, g_ref, t_ref, dx_ref, loss_ref, dg_ref):
        @pl.when(pl.program_id(0) == 0)
        def _():
            loss_ref[...] = jnp.zeros_like(loss_ref)
            dg_ref[...] = jnp.zeros_like(dg_ref)

        x = x_ref[...]
        g = g_ref[...]
        r = lax.rsqrt(jnp.mean(x * x, axis=-1, keepdims=True) + EPS)
        xh = x * r
        err = xh * g - t_ref[...]
        row = jnp.mean(err * err, axis=-1, keepdims=True)
        loss_ref[...] += 0.5 * jnp.sum(row, axis=0, keepdims=True)
        dy = err * (1.0 / D)
        dg_ref[...] += jnp.sum(dy * xh, axis=0, keepdims=True)
        dxh = dy * g
        dx_ref[...] = r * (dxh - xh * jnp.mean(dxh * xh, axis=-1, keepdims=True))

    row_blk = pl.BlockSpec((tm, D), lambda i: (i, 0))
    vec = pl.BlockSpec((1, D), lambda i: (0, 0))
    return pl.pallas_call(body, name="final_loss_bwd", grid=(S // tm,), in_specs=[row_blk, vec, row_blk],
                          out_specs=[row_blk, pl.BlockSpec((1, 128), lambda i: (0, 0)), vec],
                          out_shape=[_sds((S, D), f32), _sds((1, 128), f32), _sds((1, D), f32)],
                          compiler_params=_params(1))(x2, g3, tgt)


def _rms_bwd(name, x, g, dxn, dres, tm):
    S = x.shape[0]
    tm = min(tm, S)

    def body(x_ref, g_ref, d_ref, r_ref, dx_ref, dg_ref):
        @pl.when(pl.program_id(0) == 0)
        def _():
            dg_ref[...] = jnp.zeros_like(dg_ref)

        x = x_ref[...]
        d = d_ref[...]
        r = lax.rsqrt(jnp.mean(x * x, axis=-1, keepdims=True) + EPS)
        xh = x * r
        dg_ref[...] += jnp.sum(d * xh, axis=0, keepdims=True)
        dxh = d * g_ref[...]
        dx_ref[...] = r_ref[...] + r * (dxh - xh * jnp.mean(dxh * xh, axis=-1, keepdims=True))

    row_blk = pl.BlockSpec((tm, D), lambda i: (i, 0))
    vec = pl.BlockSpec((1, D), lambda i: (0, 0))
    return pl.pallas_call(body, name=name, grid=(S // tm,), in_specs=[row_blk, vec, row_blk, row_blk],
                          out_specs=[row_blk, vec], out_shape=[_sds((S, D), f32), _sds((1, D), f32)],
                          compiler_params=_params(1))(x, g, dxn, dres)


LRU_TT = 256
SCAN_UNROLL = 8


HALO = 16


def _halo(ref, i, S):
    nt = S // LRU_TT
    t0 = pl.multiple_of(i * LRU_TT, LRU_TT)
    p0 = pl.multiple_of(jnp.maximum(t0 - HALO, 0), HALO)
    n0 = pl.multiple_of(jnp.minimum(t0 + LRU_TT, S - HALO), HALO)
    prev = jnp.where(i > 0, ref[pl.ds(p0, HALO), :].astype(f32), 0.0)
    nxt = jnp.where(i < nt - 1, ref[pl.ds(n0, HALO), :].astype(f32), 0.0)
    return jnp.concatenate([prev, ref[pl.ds(t0, LRU_TT), :].astype(f32), nxt], axis=0)


def _shift(ext, k):
    n = LRU_TT + 2 * HALO
    return pltpu.roll(ext, (-k) % n, 0)[HALO:HALO + LRU_TT]


def _lru_gates(uc, wbd, ba, bx):
    pre = jnp.dot(uc.astype(bf16), wbd, preferred_element_type=f32)
    r_f = _sigmoid(pre[:, 0:CW] + ba[0:1])
    i_f = _sigmoid(pre[:, CW:2 * CW] + bx[0:1])
    r_b = _sigmoid(pre[:, 2 * CW:3 * CW] + ba[1:2])
    i_b = _sigmoid(pre[:, 3 * CW:4 * CW] + bx[1:2])
    return r_f, i_f, r_b, i_b


def _lru_coeffs(r, sp):
    log_a = (-RGLRU_C * r) * sp
    a = jnp.exp(log_a)
    beta = jnp.sqrt(jnp.maximum(_one_minus_exp2x(log_a, a), 0.0))
    return a, beta


def _lru_coeffs_inv(r, sp):
    log_a = (-RGLRU_C * r) * sp
    a = jnp.exp(log_a)
    om = jnp.maximum(_one_minus_exp2x(log_a, a), 0.0)
    return a, jnp.sqrt(om), lax.rsqrt(om)


def _conv_tile(u_ref, i, S, cw, cb):
    ext = _halo(u_ref, i, S)
    um2, um1, u0, up1 = _shift(ext, -2), _shift(ext, -1), ext[HALO:HALO + LRU_TT], _shift(ext, 1)
    uc = um2 * cw[0:1] + um1 * cw[1:2] + u0 * cw[2:3] + up1 * cw[3:4] + cb
    return uc, (um2, um1, u0, up1)


def _scan_pair(S, fwd_a, fwd_b, fwd_out, rev_a, rev_b, rev_out):
    ng = S // 8
    idx = lax.broadcasted_iota(jnp.int32, (8, CW), 0)

    def local(a, b, rev):
        for sh in (1, 2, 4):
            if rev:
                keep = idx < 8 - sh
                amt = 8 - sh
            else:
                keep = idx >= sh
                amt = sh
            a_s = jnp.where(keep, pltpu.roll(a, amt, 0), 1.0)
            b_s = jnp.where(keep, pltpu.roll(b, amt, 0), 0.0)
            b = a * b_s + b
            a = a * a_s
        return a, b

    def step(it, carry):
        cf, cr = carry
        fwd_rows = [pl.multiple_of((it * SCAN_UNROLL + j) * 8, 8) for j in range(SCAN_UNROLL)]
        rev_rows = [pl.multiple_of((ng - 1 - (it * SCAN_UNROLL + j)) * 8, 8) for j in range(SCAN_UNROLL)]
        fwd_loc = [local(fwd_a(r), fwd_b(r), False) for r in fwd_rows]
        rev_loc = [local(rev_a(r), rev_b(r), True) for r in rev_rows]
        for j in range(SCAN_UNROLL):
            a, b = fwd_loc[j]
            h = a * cf + b
            fwd_out[pl.ds(fwd_rows[j], 8), :] = h
            cf = jnp.broadcast_to(h[7:8, :], (8, CW))
            a, b = rev_loc[j]
            h = a * cr + b
            rev_out[pl.ds(rev_rows[j], 8), :] = h
            cr = jnp.broadcast_to(h[0:1, :], (8, CW))
        return cf, cr

    zero = jnp.zeros((8, CW), f32)
    lax.fori_loop(0, ng // SCAN_UNROLL, step, (zero, zero))


def _lru_specs(S):
    seq = lambda off: pl.BlockSpec((S, CW), lambda j: (0, off + j))
    par = lambda rows: pl.BlockSpec((rows, CW), lambda j: (0, j))
    return seq, par


def _lru_fwd(proj, conv_w, conv_b, lam, ba, bx, wbd):
    S = proj.shape[0]
    nt = S // LRU_TT

    def body(u_ref, g_ref, cw_ref, cb_ref, lam_ref, ba_ref, bx_ref, wbd_ref, y_ref, state_ref, af_ref, bf_ref, ab_ref, bb_ref,
             sems):
        cw, cb, ba_v, bx_v, wbd_v = cw_ref[...], cb_ref[...], ba_ref[...], bx_ref[...], wbd_ref[...]
        sp = jax.nn.softplus(-lam_ref[...])
        cols = pl.ds(pl.multiple_of(pl.program_id(0) * CW, CW), CW)
        save = [pltpu.make_async_copy(ref, state_ref.at[k, :, cols], sems.at[k])
                for k, ref in enumerate((af_ref, bf_ref, ab_ref, bb_ref))]

        def phase1(i, c):
            uc, _ = _conv_tile(u_ref, i, S, cw, cb)
            r_f, i_f, r_b, i_b = _lru_gates(uc, wbd_v, ba_v, bx_v)
            rows = pl.ds(pl.multiple_of(i * LRU_TT, LRU_TT), LRU_TT)
            a, beta = _lru_coeffs(r_f, sp[0:1])
            af_ref[rows, :] = a
            bf_ref[rows, :] = beta * (i_f * uc)
            a, beta = _lru_coeffs(r_b, sp[1:2])
            ab_ref[rows, :] = a
            bb_ref[rows, :] = beta * (i_b * uc)
            return c

        lax.fori_loop(0, nt, phase1, 0)
        row8 = lambda ref: (lambda r0: ref[pl.ds(r0, 8), :])
        _scan_pair(S, row8(af_ref), row8(bf_ref), bf_ref, row8(ab_ref), row8(bb_ref), bb_ref)
        for cp in save:
            cp.start()

        def phase3(i, c):
            rows = pl.ds(pl.multiple_of(i * LRU_TT, LRU_TT), LRU_TT)
            y = (bf_ref[rows, :] + bb_ref[rows, :]) * jax.nn.gelu(g_ref[rows, :].astype(f32))
            y_ref[rows, :] = y.astype(y_ref.dtype)
            return c

        lax.fori_loop(0, nt, phase3, 0)
        for cp in save:
            cp.wait()

    seq, par = _lru_specs(S)
    return pl.pallas_call(
        body, name="lru_fwd", grid=(NCH,),
        in_specs=[seq(0), seq(NCH), par(4), par(1), par(2), par(2), par(2),
                  pl.BlockSpec((None, CW, 4 * CW), lambda j: (j, 0, 0))],
        out_specs=[seq(0), ANY], out_shape=[_sds((S, D), bf16), _sds((4, S, D), f32)],
        scratch_shapes=[pltpu.VMEM((S, CW), f32)] * 4 + [pltpu.SemaphoreType.DMA((4,))], compiler_params=_params(1, True),
    )(proj, proj, conv_w, conv_b, lam, ba, bx, wbd)


def _lru_bwd(proj, dy, state, dproj, conv_w, conv_b, lam, ba, bx, wbd):
    S = proj.shape[0]
    nt = S // LRU_TT

    def body(u_ref, g_ref, dy_ref, state_ref, dproj_in, cw_ref, cb_ref, lam_ref, ba_ref, bx_ref, wbd_ref,
             dproj_ref, dcw_ref, dcb_ref, dlam_ref, dba_ref, dbx_ref, dwbd_ref,
             af_ref, bf_ref, ab_ref, bb_ref, dh_ref, du_ref, dg_ref, sems):
        cw, cb, ba_v, bx_v, wbd_v = cw_ref[...], cb_ref[...], ba_ref[...], bx_ref[...], wbd_ref[...]
        lam_v = lam_ref[...]
        sp = jax.nn.softplus(-lam_v)
        chunk = pl.program_id(0)

        def out_copies(j):
            c0 = pl.multiple_of(j * CW, CW)
            return [pltpu.make_async_copy(du_ref, dproj_ref.at[:, pl.ds(c0, CW)], sems.at[4]),
                    pltpu.make_async_copy(dg_ref, dproj_ref.at[:, pl.ds(D + c0, CW)], sems.at[5])]

        @pl.when(chunk >= 1)
        def _():
            for cp in out_copies(chunk - 1):
                cp.wait()

        cols = pl.ds(pl.multiple_of(chunk * CW, CW), CW)
        load = [pltpu.make_async_copy(state_ref.at[k, :, cols], ref, sems.at[k])
                for k, ref in enumerate((af_ref, bf_ref, ab_ref, bb_ref))]
        for cp in load:
            cp.start()
        for cp in load:
            cp.wait()
        row8 = lambda ref: (lambda r0: ref[pl.ds(r0, 8), :])

        def phase0(i, c):
            rows = pl.ds(pl.multiple_of(i * LRU_TT, LRU_TT), LRU_TT)
            gl, dgl = _gelu_and_grad(g_ref[rows, :].astype(f32))
            dyt = dy_ref[rows, :].astype(f32)
            dh_ref[rows, :] = dyt * gl
            dg_ref[rows, :] = ((dyt * (bf_ref[rows, :] + bb_ref[rows, :])) * dgl).astype(dg_ref.dtype)
            return c

        lax.fori_loop(0, nt, phase0, 0)

        def scaled_dh(a_ref):
            def f(r0):
                return a_ref[pl.ds(r0, 8), :] * dh_ref[pl.ds(r0, 8), :]
            return f

        _scan_pair(S, row8(ab_ref), scaled_dh(ab_ref), ab_ref, row8(af_ref), scaled_dh(af_ref), af_ref)

        dcw_ref[...] = jnp.zeros_like(dcw_ref)
        dcb_ref[...] = jnp.zeros_like(dcb_ref)
        dlam_ref[...] = jnp.zeros_like(dlam_ref)
        dba_ref[...] = jnp.zeros_like(dba_ref)
        dbx_ref[...] = jnp.zeros_like(dbx_ref)
        dwbd_ref[...] = jnp.zeros_like(dwbd_ref)

        def direction(uc, r, i_g, dht, h_nb, sp_d):
            a, beta, inv_beta = _lru_coeffs_inv(r, sp_d)
            da = dht * h_nb
            dbeta = dht * (i_g * uc)
            d_iu = dht * beta
            dlog_a = da * a - (a * a) * (dbeta * inv_beta)
            dlr = dlog_a * r
            dsp = -RGLRU_C * jnp.sum(dlr, axis=0, keepdims=True)
            dpre_r = (dlr * (1.0 - r)) * (-RGLRU_C * sp_d)
            dpre_i = (d_iu * uc) * (i_g * (1.0 - i_g))
            return dpre_r, dpre_i, d_iu * i_g, dsp

        def phase4(i, c):
            uc, (um2, um1, u0, up1) = _conv_tile(u_ref, i, S, cw, cb)
            r_f, i_f, r_b, i_b = _lru_gates(uc, wbd_v, ba_v, bx_v)
            rows = pl.ds(pl.multiple_of(i * LRU_TT, LRU_TT), LRU_TT)
            dh = dh_ref[rows, :]
            dht_f = dh + _shift(_halo(af_ref, i, S), 1)
            h_prev = _shift(_halo(bf_ref, i, S), -1)
            dht_b = dh + _shift(_halo(ab_ref, i, S), -1)
            h_next = _shift(_halo(bb_ref, i, S), 1)
            prf, pif, duc_f, dsp_f = direction(uc, r_f, i_f, dht_f, h_prev, sp[0:1])
            prb, pib, duc_b, dsp_b = direction(uc, r_b, i_b, dht_b, h_next, sp[1:2])
            dpre = jnp.concatenate([prf, pif, prb, pib], axis=1)
            dpre_b = dpre.astype(bf16)
            duc = (duc_f + duc_b) + lax.dot_general(dpre_b, wbd_v, _DIMS["nt"], preferred_element_type=f32)
            dwbd_ref[...] += lax.dot_general(uc.astype(bf16), dpre_b, _DIMS["tn"], preferred_element_type=f32)
            colsum = lambda v: jnp.sum(v, axis=0, keepdims=True)
            dba_ref[...] += jnp.concatenate([colsum(prf), colsum(prb)], axis=0)
            dbx_ref[...] += jnp.concatenate([colsum(pif), colsum(pib)], axis=0)
            dlam_ref[...] += jnp.concatenate([dsp_f, dsp_b], axis=0)
            dcb_ref[...] += colsum(duc)
            dcw_ref[...] += jnp.concatenate([colsum(duc * um2), colsum(duc * um1), colsum(duc * u0),
                                             colsum(duc * up1)], axis=0)
            af_ref[rows, :] = duc
            return c

        lax.fori_loop(0, nt, phase4, 0)
        dlam_ref[...] = dlam_ref[...] * (-_sigmoid(-lam_v))

        def phase5(i, c):
            ext = _halo(af_ref, i, S)
            rows = pl.ds(pl.multiple_of(i * LRU_TT, LRU_TT), LRU_TT)
            du = (_shift(ext, 2) * cw[0:1] + _shift(ext, 1) * cw[1:2] + ext[HALO:HALO + LRU_TT] * cw[2:3]
                  + _shift(ext, -1) * cw[3:4])
            du_ref[rows, :] = du.astype(du_ref.dtype)
            return c

        lax.fori_loop(0, nt, phase5, 0)
        for cp in out_copies(chunk):
            cp.start()

        @pl.when(chunk == NCH - 1)
        def _():
            for cp in out_copies(chunk):
                cp.wait()

    seq, par = _lru_specs(S)
    return pl.pallas_call(
        body, name="lru_bwd", grid=(NCH,),
        in_specs=[seq(0), seq(NCH), pl.BlockSpec((None, S, CW), lambda j: (0, 0, j)), ANY, ANY,
                  par(4), par(1), par(2), par(2), par(2), pl.BlockSpec((None, CW, 4 * CW), lambda j: (j, 0, 0))],
        out_specs=[ANY, par(4), par(1), par(2), par(2), par(2),
                   pl.BlockSpec((None, CW, 4 * CW), lambda j: (j, 0, 0))],
        out_shape=[_sds(dproj.shape, bf16), _sds((4, D), f32), _sds((1, D), f32), _sds((2, D), f32),
                   _sds((2, D), f32), _sds((2, D), f32), _sds((NCH, CW, 4 * CW), f32)],
        scratch_shapes=[pltpu.VMEM((S, CW), f32)] * 5 + [pltpu.VMEM((S, CW), bf16)] * 2 + [pltpu.SemaphoreType.DMA((6,))],
        input_output_aliases={4: 0}, compiler_params=_params(1, True),
    )(proj, proj, dy, state, dproj, conv_w, conv_b, lam, ba, bx, wbd)


_SLOPES = [2.0 ** (-8.0 * (h + 1) / NH) for h in range(NH)]


def _half_mask(shape, e):
    lane = lax.broadcasted_iota(jnp.int32, shape, 1)
    return (lane < HD) if e == 0 else (lane >= HD)


def _both_halves(x, src):
    return jnp.where(_half_mask(x.shape, src), x, pltpu.roll(x, HD, 1))


def _attn_base(n, S):
    tq = lax.broadcasted_iota(jnp.int32, (BLK, 3 * BLK), 0)
    sk = lax.broadcasted_iota(jnp.int32, (BLK, 3 * BLK), 1)
    dist = jnp.abs(tq + BLK - sk)
    kpos = n * BLK - BLK + sk
    valid = (dist <= BLK) & (kpos >= 0) & (kpos < S)
    return jnp.where(valid, -dist.astype(f32), NEG_INF)


def _group_heads(ref, kvh, scale):
    parts = []
    for i in range(4):
        pair = 2 * kvh + i // 2
        x = ref[:, pair * 128:(pair + 1) * 128].astype(f32)
        parts.append(jnp.where(_half_mask(x.shape, i % 2), x * scale, 0.0))
    return parts


def _stack_bf16(parts):
    return jnp.concatenate([p.astype(bf16) for p in parts], axis=0)


def _attn_softmax(s_raw, base, slope, sink):
    s = s_raw + slope * base
    m = jnp.maximum(jnp.max(s, axis=-1, keepdims=True), sink)
    p = jnp.exp(s - m)
    esink = jnp.exp(sink - m)
    inv = 1.0 / (jnp.sum(p, axis=-1, keepdims=True) + esink)
    return p, inv, esink * inv


def _attn_specs(S):
    nb = S // BLK
    q_spec = pl.BlockSpec((BLK, D), lambda n: (n, 2))
    kv = lambda col: [pl.BlockSpec((BLK, 256), lambda n: (jnp.maximum(n - 1, 0), col)),
                      pl.BlockSpec((BLK, 256), lambda n: (n, col)),
                      pl.BlockSpec((BLK, 256), lambda n: (jnp.minimum(n + 1, nb - 1), col))]
    return nb, q_spec, kv(COL_K), kv(COL_V)


def _attn_fwd(proj, sink):
    S = proj.shape[0]
    nb, q_spec, k_specs, v_specs = _attn_specs(S)

    def body(sink_ref, q_ref, kp_ref, kc_ref, kn_ref, vp_ref, vc_ref, vn_ref, o_ref):
        base = _attn_base(pl.program_id(0), S)
        kcat = jnp.concatenate([kp_ref[...], kc_ref[...], kn_ref[...]], axis=0).astype(f32)
        vcat = jnp.concatenate([vp_ref[...], vc_ref[...], vn_ref[...]], axis=0).astype(f32)
        even = _half_mask((BLK, 128), 0)
        for kvh in range(NH // 4):
            ch, off = kvh // 2, kvh % 2
            kb = _both_halves(kcat[:, ch * 128:(ch + 1) * 128], off).astype(bf16)
            vb = _both_halves(vcat[:, ch * 128:(ch + 1) * 128], off).astype(bf16)
            q4 = _stack_bf16(_group_heads(q_ref, kvh, HD ** -0.5))
            s4 = lax.dot_general(q4, kb, _DIMS["nt"], preferred_element_type=f32)
            ps, invs = [], []
            for i in range(4):
                h = 4 * kvh + i
                p, inv, _ = _attn_softmax(s4[i * BLK:(i + 1) * BLK], base, _SLOPES[h], sink_ref[0, h])
                ps.append(p)
                invs.append(inv)
            o4 = jnp.dot(_stack_bf16(ps), vb, preferred_element_type=f32)
            for pr in range(2):
                lo = o4[(2 * pr) * BLK:(2 * pr + 1) * BLK] * invs[2 * pr]
                hi = o4[(2 * pr + 1) * BLK:(2 * pr + 2) * BLK] * invs[2 * pr + 1]
                pair = 2 * kvh + pr
                o_ref[:, pair * 128:(pair + 1) * 128] = jnp.where(even, lo, hi).astype(o_ref.dtype)

    return pl.pallas_call(
        body, name="attn_fwd", grid=(nb,),
        in_specs=[pl.BlockSpec(memory_space=pltpu.SMEM), q_spec] + k_specs + v_specs,
        out_specs=pl.BlockSpec((BLK, D), lambda n: (n, 0)), out_shape=_sds((S, D), bf16),
        compiler_params=_params(1, True))(sink, proj, proj, proj, proj, proj, proj, proj)


def _attn_bwd(proj, sink, y_b, dy, dproj):
    S = proj.shape[0]
    nb, q_spec, k_specs, v_specs = _attn_specs(S)
    q_col, kv_col = COL_Q * 256, COL_K * 256

    def body(sink_ref, q_ref, kp_ref, kc_ref, kn_ref, vp_ref, vc_ref, vn_ref, o_ref, do_ref, dproj_in,
             dproj_ref, dsink_ref, dk_ref, dv_ref, dq_buf, kv_buf, sems):
        n = pl.program_id(0)
        slot = n % 2
        dq_ref = dq_buf.at[slot]

        def dq_copy(step):
            rows = pl.ds(pl.multiple_of(step * BLK, BLK), BLK)
            return pltpu.make_async_copy(dq_buf.at[step % 2], dproj_ref.at[rows, pl.ds(q_col, D)], sems.at[step % 2])

        @pl.when(n >= 2)
        def _():
            dq_copy(n - 2).wait()

        @pl.when(n == 0)
        def _():
            dk_ref[...] = jnp.zeros_like(dk_ref)
            dv_ref[...] = jnp.zeros_like(dv_ref)
            dsink_ref[...] = jnp.zeros_like(dsink_ref)

        base = _attn_base(n, S)
        kcat = jnp.concatenate([kp_ref[...], kc_ref[...], kn_ref[...]], axis=0).astype(f32)
        vcat = jnp.concatenate([vp_ref[...], vc_ref[...], vn_ref[...]], axis=0).astype(f32)
        dk_rows, dv_rows = [[], []], [[], []]
        scale = HD ** -0.5
        even = _half_mask((BLK, 128), 0)
        for kvh in range(NH // 4):
            ch, off = kvh // 2, kvh % 2
            kb = _both_halves(kcat[:, ch * 128:(ch + 1) * 128], off).astype(bf16)
            vb = _both_halves(vcat[:, ch * 128:(ch + 1) * 128], off).astype(bf16)
            q_parts = _group_heads(q_ref, kvh, scale)
            d_parts = _group_heads(do_ref, kvh, 1.0)
            s4 = lax.dot_general(_stack_bf16(q_parts), kb, _DIMS["nt"], preferred_element_type=f32)
            dp4 = lax.dot_general(_stack_bf16(d_parts), vb, _DIMS["nt"], preferred_element_type=f32)
            ts, ps, qn, dn, invs = [], [], [], [], []
            for i in range(4):
                h = 4 * kvh + i
                pair = 2 * kvh + i // 2
                rows = slice(i * BLK, (i + 1) * BLK)
                p, inv, psink = _attn_softmax(s4[rows], base, _SLOPES[h], sink_ref[0, h])
                delta = jnp.sum(d_parts[i] * o_ref[:, pair * 128:(pair + 1) * 128].astype(f32), axis=-1, keepdims=True)
                dsink_ref[h:h + 1, :] += jnp.broadcast_to(-jnp.sum(psink * delta, axis=0, keepdims=True), (1, 128))
                ts.append(p * (dp4[rows] - delta))
                ps.append(p)
                qn.append(q_parts[i] * inv)
                dn.append(d_parts[i] * inv)
                invs.append(inv)
            t4 = _stack_bf16(ts)
            dq4 = jnp.dot(t4, kb, preferred_element_type=f32)
            for pr in range(2):
                lo = dq4[(2 * pr) * BLK:(2 * pr + 1) * BLK] * invs[2 * pr]
                hi = dq4[(2 * pr + 1) * BLK:(2 * pr + 2) * BLK] * invs[2 * pr + 1]
                pair = 2 * kvh + pr
                dq_ref[:, pair * 128:(pair + 1) * 128] = (jnp.where(even, lo, hi) * scale).astype(dq_ref.dtype)
            dk_t = lax.dot_general(_stack_bf16(qn), t4, _DIMS["tn"], preferred_element_type=f32)
            dv_t = lax.dot_general(_stack_bf16(dn), _stack_bf16(ps), _DIMS["tn"], preferred_element_type=f32)
            dk_rows[ch].append(dk_t[0:HD] + dk_t[HD:2 * HD])
            dv_rows[ch].append(dv_t[0:HD] + dv_t[HD:2 * HD])
        dk_acc = [jnp.concatenate(r, axis=0).T for r in dk_rows]
        dv_acc = [jnp.concatenate(r, axis=0).T for r in dv_rows]
        for j in range(3):
            blk = n + (j - 1)

            @pl.when((blk >= 0) & (blk < nb))
            def _():
                rows = pl.ds(pl.multiple_of(blk * BLK, BLK), BLK)
                for ch in range(2):
                    dk_ref[rows, ch * 128:(ch + 1) * 128] += dk_acc[ch][j * BLK:(j + 1) * BLK]
                    dv_ref[rows, ch * 128:(ch + 1) * 128] += dv_acc[ch][j * BLK:(j + 1) * BLK]

        dq_copy(n).start()

        @pl.when(n == nb - 1)
        def _():
            def cast(i, c):
                rows = pl.ds(pl.multiple_of(i * 4 * BLK, 4 * BLK), 4 * BLK)
                kv_buf[rows, 0:256] = dk_ref[rows, :].astype(bf16)
                kv_buf[rows, 256:512] = dv_ref[rows, :].astype(bf16)
                return c

            lax.fori_loop(0, S // (4 * BLK), cast, 0)
            kv_copy = pltpu.make_async_copy(kv_buf, dproj_ref.at[:, pl.ds(kv_col, 512)], sems.at[2])
            kv_copy.start()
            if nb >= 2:
                dq_copy(n - 1).wait()
            dq_copy(n).wait()
            kv_copy.wait()

    row_blk = pl.BlockSpec((BLK, D), lambda n: (n, 0))
    return pl.pallas_call(
        body, name="attn_bwd", grid=(nb,),
        in_specs=[pl.BlockSpec(memory_space=pltpu.SMEM), q_spec] + k_specs + v_specs
        + [row_blk, pl.BlockSpec((None, BLK, D), lambda n: (1, n, 0)), ANY],
        out_specs=[ANY, pl.BlockSpec((NH, 128), lambda n: (0, 0))],
        out_shape=[_sds(dproj.shape, bf16), _sds((NH, 128), f32)],
        scratch_shapes=[pltpu.VMEM((S, 256), f32), pltpu.VMEM((S, 256), f32), pltpu.VMEM((2, BLK, D), bf16),
                        pltpu.VMEM((S, 512), bf16), pltpu.SemaphoreType.DMA((3,))],
        input_output_aliases={10: 0},
        compiler_params=_params(1, True))(sink, proj, proj, proj, proj, proj, proj, proj, y_b, dy, dproj)


def _adamw(name, w, g, m, v, tr):
    R, C = w.shape
    tr = min(tr, R)

    def body(w_ref, g_ref, m_ref, v_ref, d_ref, m2_ref, v2_ref):
        g = g_ref[...]
        m2 = ADAM_B1 * m_ref[...] + (1.0 - ADAM_B1) * g
        v2 = ADAM_B2 * v_ref[...] + (1.0 - ADAM_B2) * (g * g)
        m_hat = m2 / (1.0 - ADAM_B1 ** ADAM_STEP)
        v_hat = v2 / (1.0 - ADAM_B2 ** ADAM_STEP)
        d_ref[...] = -ADAM_LR * (m_hat / (jnp.sqrt(v_hat) + ADAM_EPS) + ADAM_WD * w_ref[...])
        m2_ref[...] = m2
        v2_ref[...] = v2

    blk = pl.BlockSpec((tr, C), lambda i: (i, 0))
    return pl.pallas_call(body, name=name, grid=(R // tr,), in_specs=[blk] * 4, out_specs=[blk] * 3,
                          out_shape=[_sds((R, C), f32)] * 3, compiler_params=_params(1))(w, g, m, v)


def _pair_sum(name, c_arr, g4, recv, th):
    _, _, h, w = g4.shape
    th = min(th, h)

    def body(c_ref, g_ref, r_ref, o_ref, ob_ref):
        p = g_ref[...] + r_ref[...]
        o_ref[...] = p
        ob_ref[...] = p.astype(bf16)

    blk = pl.BlockSpec((None, th, w), lambda s, i, c_ref: (s, i, 0))
    spec = pltpu.PrefetchScalarGridSpec(
        num_scalar_prefetch=1, grid=(NCHIP, h // th),
        in_specs=[pl.BlockSpec((None, None, th, w), lambda s, i, c_ref: (s, c_ref[0], i, 0)), blk],
        out_specs=[blk, blk])
    return pl.pallas_call(body, name=name, grid_spec=spec,
                          out_shape=[_sds((NCHIP, h, w), f32), _sds((NCHIP, h, w), bf16)],
                          compiler_params=_params(2))(c_arr, g4, recv)


def _chip_sum(name, chip_arr, own4, recv3, th):
    _, h, w = own4.shape
    th = min(th, h)

    def body(s_ref, o_ref, r_ref, out_ref):
        out_ref[...] = ((o_ref[...] + r_ref[0].astype(f32)) + r_ref[1].astype(f32)) + r_ref[2].astype(f32)

    spec = pltpu.PrefetchScalarGridSpec(
        num_scalar_prefetch=1, grid=(h // th,),
        in_specs=[pl.BlockSpec((None, th, w), lambda i, s_ref: (s_ref[0], i, 0)),
                  pl.BlockSpec((3, th, w), lambda i, s_ref: (0, i, 0))],
        out_specs=pl.BlockSpec((th, w), lambda i, s_ref: (i, 0)))
    return pl.pallas_call(body, name=name, grid_spec=spec, out_shape=_sds((h, w), f32),
                          compiler_params=_params(1, True))(chip_arr, own4, recv3)


def _adamw_halves(name, c_arr, w, g_own, g_recv, m, v, th):
    h, wd = g_own.shape
    th = min(th, h)

    def body(c_ref, w_ref, go_ref, gr_ref, m_ref, v_ref, g_ref, d_ref, m2_ref, v2_ref):
        g = jnp.where(c_ref[0] == pl.program_id(0), go_ref[...], gr_ref[...])
        m2 = ADAM_B1 * m_ref[...] + (1.0 - ADAM_B1) * g
        v2 = ADAM_B2 * v_ref[...] + (1.0 - ADAM_B2) * (g * g)
        m_hat = m2 / (1.0 - ADAM_B1 ** ADAM_STEP)
        v_hat = v2 / (1.0 - ADAM_B2 ** ADAM_STEP)
        g_ref[...] = g
        d_ref[...] = -ADAM_LR * (m_hat / (jnp.sqrt(v_hat) + ADAM_EPS) + ADAM_WD * w_ref[...])
        m2_ref[...] = m2
        v2_ref[...] = v2

    nt = h // th
    full = pl.BlockSpec((th, wd), lambda hh, i, c_ref: (hh * nt + i, 0))
    half = pl.BlockSpec((th, wd), lambda hh, i, c_ref: (i, 0))
    spec = pltpu.PrefetchScalarGridSpec(num_scalar_prefetch=1, grid=(2, nt),
                                        in_specs=[full, half, half, full, full], out_specs=[full] * 4)
    return pl.pallas_call(body, name=name, grid_spec=spec, out_shape=[_sds((2 * h, wd), f32)] * 4,
                          compiler_params=_params(2))(c_arr, w, g_own, g_recv, m, v)


def _add2(name, a, b):
    def body(a_ref, b_ref, o_ref):
        o_ref[...] = a_ref[...] + b_ref[...]
    return pl.pallas_call(body, name=name, out_shape=_sds(a.shape, f32))(a, b)


def _sum4(name, b4, th):
    _, h, w = b4.shape
    th = min(th, h)

    def body(b_ref, o_ref):
        o_ref[...] = ((b_ref[0] + b_ref[1]) + b_ref[2]) + b_ref[3]

    return pl.pallas_call(body, name=name, grid=(h // th,),
                          in_specs=[pl.BlockSpec((NCHIP, th, w), lambda i: (0, i, 0))],
                          out_specs=pl.BlockSpec((th, w), lambda i: (i, 0)), out_shape=_sds((h, w), f32),
                          compiler_params=_params(1, True))(b4)


def _coords():
    x, y, c = lax.axis_index("x"), lax.axis_index("y"), lax.axis_index("c")
    return x, y, c, [(1 - x, y), (x, 1 - y), (1 - x, 1 - y)]


def _gather_chips(arrs):
    n = len(arrs)

    def body(*refs):
        ins, outs = refs[:n], refs[n:2 * n]
        send_sems, recv_sems, local_sems = refs[2 * n:2 * n + 3]
        stage = refs[2 * n + 3:]
        x, y, c, chips = _coords()
        s = 2 * x + y
        sib = (x, y, 1 - c)
        load = [pltpu.make_async_copy(ins[a], stage[a], local_sems.at[a]) for a in range(n)]
        local = [pltpu.make_async_copy(stage[a], outs[a].at[s], local_sems.at[n + a]) for a in range(n)]
        for cp in load:
            cp.start()

        def over_ici(k, a, slot, peer):
            return pltpu.make_async_remote_copy(src_ref=ins[a].at[c], dst_ref=outs[a].at[slot, c], send_sem=send_sems.at[k * n + a],
                                                recv_sem=recv_sems.at[k * n + a], device_id=peer, device_id_type=MESH)

        def to_sibling(k, a, slot, half):
            i = (3 + k) * n + a
            return pltpu.make_async_remote_copy(src_ref=outs[a].at[slot, half], dst_ref=outs[a].at[slot, half], send_sem=send_sems.at[i],
                                                recv_sem=recv_sems.at[i], device_id=sib, device_id_type=MESH)

        sends = [over_ici(k, a, s, (px, py, c)) for k, (px, py) in enumerate(chips) for a in range(n)]
        for cp in sends:
            cp.start()
        for a in range(n):
            load[a].wait()
            local[a].start()
        passed = []
        for k, (px, py) in enumerate(chips):
            for a in range(n):
                over_ici(k, a, 2 * px + py, (px, py, c)).wait_recv()
                cp = to_sibling(k, a, 2 * px + py, c)
                cp.start()
                passed.append(cp)
        for k, (px, py) in enumerate(chips):
            for a in range(n):
                to_sibling(k, a, 2 * px + py, 1 - c).wait_recv()
        for cp in sends + passed:
            cp.wait_send()
        for cp in local:
            cp.wait()

    return pl.pallas_call(
        body, name="gather_weights", in_specs=[ANY] * n, out_specs=[ANY] * n,
        out_shape=[_sds((NCHIP,) + a.shape, a.dtype) for a in arrs],
        scratch_shapes=[pltpu.SemaphoreType.DMA((6 * n,)), pltpu.SemaphoreType.DMA((6 * n,)), pltpu.SemaphoreType.DMA((2 * n,))]
        + [pltpu.VMEM(a.shape, a.dtype) for a in arrs],
        compiler_params=pltpu.CompilerParams(vmem_limit_bytes=VMEM_LIMIT),
    )(*arrs)


HBM = pl.BlockSpec(memory_space=pltpu.HBM)
SEM = pl.BlockSpec(memory_space=pltpu.SEMAPHORE)
EFFECT = pltpu.SideEffectType.DATAFLOW_SIDE_EFFECTING


def _split_start(name, n_copies, make_copies, ins, land_shapes, after):
    ni, nl = len(ins), len(land_shapes)

    def body(*refs):
        in_refs, land_refs = refs[:ni], refs[ni:ni + nl]
        send_sems, recv_sems = refs[ni + nl + 1], refs[ni + nl + 2]
        token = refs[-1]
        for cp in make_copies(in_refs, land_refs, send_sems, recv_sems):
            cp.start()
        token[...] = jnp.zeros_like(token)

    lands = [pltpu.with_memory_space_constraint(lax.empty(s.shape, s.dtype), pltpu.HBM) for s in land_shapes]
    res = pl.pallas_call(
        body, name=name,
        out_shape=(pltpu.SemaphoreType.DMA((n_copies,)), pltpu.SemaphoreType.DMA((n_copies,)),
                   *[pltpu.HBM(a.shape, a.dtype) for a in ins], *[pltpu.HBM(s.shape, s.dtype) for s in land_shapes],
                   _sds((8, 128), f32)),
        in_specs=[HBM] * (ni + nl) + [ANY], out_specs=(SEM, SEM, *[HBM] * (ni + nl), pl.BlockSpec(memory_space=pltpu.VMEM)),
        input_output_aliases={i: 2 + i for i in range(ni + nl)},
        compiler_params=pltpu.CompilerParams(has_side_effects=EFFECT),
    )(*[pltpu.with_memory_space_constraint(a, pltpu.HBM) for a in ins], *lands, after)
    return res[0], res[1], list(res[2:2 + ni]), list(res[2 + ni:2 + ni + nl]), res[-1]


def _split_wait(name, make_copies, send_sems, recv_sems, ins, lands, after):
    ni, nl = len(ins), len(lands)

    def body(*refs):
        in_refs, land_refs = refs[:ni], refs[ni:ni + nl]
        s_sems, r_sems = refs[ni + nl], refs[ni + nl + 1]
        for cp in make_copies(in_refs, land_refs, s_sems, r_sems):
            cp.wait_send()
            cp.wait_recv()

    res = pl.pallas_call(
        body, name=name, out_shape=tuple(pltpu.HBM(a.shape, a.dtype) for a in ins + lands),
        in_specs=[HBM] * (ni + nl) + [SEM, SEM, ANY], out_specs=tuple([HBM] * (ni + nl)),
        input_output_aliases={i: i for i in range(ni + nl)},
        compiler_params=pltpu.CompilerParams(has_side_effects=EFFECT),
    )(*ins, *lands, send_sems, recv_sems, after)
    return list(res[:ni]), list(res[ni:])


def _gather_copies(n):
    def make(in_refs, land_refs, send_sems, recv_sems):
        x, y, c, chips = _coords()
        s = 2 * x + y
        return [pltpu.make_async_remote_copy(src_ref=in_refs[a], dst_ref=land_refs[a].at[s], send_sem=send_sems.at[k * n + a],
                                             recv_sem=recv_sems.at[k * n + a], device_id=(px, py, c), device_id_type=MESH)
                for k, (px, py) in enumerate(chips) for a in range(n)]
    return make


def _sibling_half_copies(n):
    def make(in_refs, land_refs, send_sems, recv_sems):
        x, y, c, _ = _coords()
        return [pltpu.make_async_remote_copy(src_ref=in_refs[a].at[:, 1 - c], dst_ref=land_refs[a], send_sem=send_sems.at[a],
                                             recv_sem=recv_sems.at[a], device_id=(x, y, 1 - c), device_id_type=MESH)
                for a in range(n)]
    return make


def _chip_part_copies(n):
    def make(in_refs, land_refs, send_sems, recv_sems):
        x, y, c, chips = _coords()
        return [pltpu.make_async_remote_copy(src_ref=in_refs[a].at[2 * px + py], dst_ref=land_refs[a].at[k],
                                             send_sem=send_sems.at[k * n + a], recv_sem=recv_sems.at[k * n + a],
                                             device_id=(px, py, c), device_id_type=MESH)
                for k, (px, py) in enumerate(chips) for a in range(n)]
    return make


def _sibling_whole_copies(n):
    def make(in_refs, land_refs, send_sems, recv_sems):
        x, y, c, _ = _coords()
        return [pltpu.make_async_remote_copy(src_ref=in_refs[a], dst_ref=land_refs[a], send_sem=send_sems.at[a],
                                             recv_sem=recv_sems.at[a], device_id=(x, y, 1 - c), device_id_type=MESH)
                for a in range(n)]
    return make


def _place_own(chip_arr, owns, lands, steps):
    n = len(owns)

    def body(s_ref, *refs):
        for a in range(n):
            refs[2 * n + a][...] = refs[a][...]

    tiles = [o.shape[0] // steps for o in owns]
    spec = pltpu.PrefetchScalarGridSpec(
        num_scalar_prefetch=1, grid=(steps,),
        in_specs=[pl.BlockSpec((t, o.shape[1]), lambda i, s_ref: (i, 0)) for t, o in zip(tiles, owns)] + [ANY] * n,
        out_specs=[pl.BlockSpec((None, t, o.shape[1]), lambda i, s_ref: (s_ref[0], i, 0)) for t, o in zip(tiles, owns)])
    return pl.pallas_call(body, name="place_own", grid_spec=spec, out_shape=[_sds(l.shape, l.dtype) for l in lands],
                          input_output_aliases={1 + n + a: a for a in range(n)},
                          compiler_params=_params(1))(chip_arr, *owns, *lands)


def _sibling_halves(g4s, small):
    n = len(g4s)

    def body(*refs):
        ins, small_ref = refs[:n], refs[n]
        outs, small_out = refs[n + 1:2 * n + 1], refs[2 * n + 1]
        send_sems, recv_sems = refs[2 * n + 2:]
        x, y, c, _ = _coords()
        sib = (x, y, 1 - c)

        def remote(a, half):
            src = small_ref if a == n else ins[a].at[:, half]
            dst = small_out if a == n else outs[a]
            return pltpu.make_async_remote_copy(src_ref=src, dst_ref=dst, send_sem=send_sems.at[a], recv_sem=recv_sems.at[a],
                                                device_id=sib, device_id_type=MESH)

        sends = [remote(a, 1 - c) for a in range(n + 1)]
        for cp in sends:
            cp.start()
        for a in range(n + 1):
            remote(a, c).wait_recv()
        for cp in sends:
            cp.wait_send()

    return pl.pallas_call(
        body, name="reduce_sibling", in_specs=[ANY] * (n + 1), out_specs=[ANY] * (n + 1),
        out_shape=[_sds((g.shape[0],) + g.shape[2:], f32) for g in g4s] + [_sds(small.shape, f32)],
        scratch_shapes=[pltpu.SemaphoreType.DMA((n + 1,)), pltpu.SemaphoreType.DMA((n + 1,))],
    )(*g4s, small)


def _exchange_chips(parts, small2):
    n = len(parts)

    def body(*refs):
        ins, small_ref = refs[:n], refs[n]
        outs, small_out = refs[n + 1:2 * n + 1], refs[2 * n + 1]
        send_sems, recv_sems, local_sem = refs[2 * n + 2:]
        x, y, c, chips = _coords()
        s = 2 * x + y
        local = pltpu.make_async_copy(small_ref.at[c], small_out.at[s], local_sem)
        local.start()

        def remote(k, a, dest_chip, small_slot, peer):
            if a == n:
                src, dst = small_ref.at[c], small_out.at[small_slot]
            else:
                src, dst = ins[a].at[dest_chip], outs[a].at[k]
            i = k * (n + 1) + a
            return pltpu.make_async_remote_copy(src_ref=src, dst_ref=dst, send_sem=send_sems.at[i], recv_sem=recv_sems.at[i],
                                                device_id=peer, device_id_type=MESH)

        sends = [remote(k, a, 2 * px + py, s, (px, py, c)) for k, (px, py) in enumerate(chips) for a in range(n + 1)]
        for cp in sends:
            cp.start()
        for k, (px, py) in enumerate(chips):
            for a in range(n + 1):
                remote(k, a, s, 2 * px + py, (px, py, c)).wait_recv()
        for cp in sends:
            cp.wait_send()
        local.wait()

    m = 3 * (n + 1)
    return pl.pallas_call(
        body, name="reduce_chips", in_specs=[ANY] * (n + 1), out_specs=[ANY] * (n + 1),
        out_shape=[_sds((3,) + p.shape[1:], p.dtype) for p in parts] + [_sds((NCHIP,) + small2.shape[1:], f32)],
        scratch_shapes=[pltpu.SemaphoreType.DMA((m,)), pltpu.SemaphoreType.DMA((m,)), pltpu.SemaphoreType.DMA],
    )(*parts, small2)


def _share_sibling(halves):
    n = len(halves)

    def body(*refs):
        ins, outs = refs[:n], refs[n:2 * n]
        send_sems, recv_sems = refs[2 * n:]
        x, y, c, _ = _coords()
        sib = (x, y, 1 - c)
        sends = [pltpu.make_async_remote_copy(src_ref=ins[a], dst_ref=outs[a], send_sem=send_sems.at[a], recv_sem=recv_sems.at[a],
                                              device_id=sib, device_id_type=MESH) for a in range(n)]
        for cp in sends:
            cp.start()
        for cp in sends:
            cp.wait()

    return pl.pallas_call(
        body, name="reduce_share", in_specs=[ANY] * n, out_specs=[ANY] * n,
        out_shape=[_sds(h.shape, f32) for h in halves],
        scratch_shapes=[pltpu.SemaphoreType.DMA((n,)), pltpu.SemaphoreType.DMA((n,))],
    )(*halves)


def _block_diag_pairs(w):
    w = w.reshape(NCH, 2, HD, HD)
    z = jnp.zeros((NCH, HD, HD), w.dtype)
    return jnp.concatenate([jnp.concatenate([w[:, 0], z], axis=2), jnp.concatenate([z, w[:, 1]], axis=2)], axis=1)


def _diag_blocks(m):
    return jnp.stack([m[:, :HD, :HD], m[:, HD:, HD:]], axis=1).reshape(NH, HD, HD)


def _pack(vs, rows):
    flat = jnp.concatenate([v.reshape(-1) for v in vs])
    return jnp.pad(flat, (0, rows * 128 - flat.shape[0])).reshape(rows, 128)


def _unpack(packed, shapes):
    flat = packed.reshape(-1)
    out, off = [], 0
    for shp in shapes:
        size = math.prod(shp)
        out.append(flat[off:off + size].reshape(shp))
        off += size
    return out


def _rows_for(sizes, multiple):
    rows = -(-sum(sizes) // 128)
    return -(-rows // multiple) * multiple


def kernel(x, norm_mix_g, w_in, b_gate, conv_w, conv_b, lru_lambda, lru_wa, lru_ba, lru_wx, lru_bx, attn_sink, w_out, norm_ffn_g, w_ffn_in, w_ffn_out, norm_final_g, loss_target, m_norm_mix_g, m_w_in, m_b_gate, m_conv_w, m_conv_b, m_lru_lambda, m_lru_wa, m_lru_ba, m_lru_wx, m_lru_bx, m_attn_sink, m_w_out, m_norm_ffn_g, m_w_ffn_in, m_w_ffn_out, m_norm_final_g, v_norm_mix_g, v_w_in, v_b_gate, v_conv_w, v_conv_b, v_lru_lambda, v_lru_wa, v_lru_ba, v_lru_wx, v_lru_bx, v_attn_sink, v_w_out, v_norm_ffn_g, v_w_ffn_in, v_w_ffn_out, v_norm_final_g):
    S = x.shape[1]
    xs = x[0]
    tgt = loss_target[0]
    cx, cy, cc = lax.axis_index("x"), lax.axis_index("y"), lax.axis_index("c")
    chip = 2 * cx + cy
    SW = D // NCHIP

    small_shard = _pack([conv_w[0], lru_lambda[0], lru_ba[0], lru_bx[0]], 32)
    halves_of = lambda a: a.reshape(2, a.shape[0] // 2, a.shape[1])
    w_in_g, small_g = _gather_chips([halves_of(w_in[0].astype(bf16)), halves_of(small_shard)])
    w_in_g = w_in_g.reshape(NCHIP, D, SHW)
    small_g = small_g.reshape(NCHIP, 32, 128)
    late = [w_ffn_in[0].astype(bf16), w_out[0].astype(bf16), w_ffn_out[0].astype(bf16)]
    late_send, late_recv, late_src, late_land, late_token = _split_start(
        "gather_late_start", 9, _gather_copies(3), late, [_sds((NCHIP,) + a.shape, bf16) for a in late], small_g)
    small_parts = [_unpack(small_g[s], [(4, SW), (2, SW), (2, SW), (2, SW)]) for s in range(NCHIP)]
    conv_w_f, lam_f, ba_f, bx_f = [jnp.concatenate([small_parts[s][p] for s in range(NCHIP)], axis=1) for p in range(4)]
    wbd = jnp.concatenate([_block_diag_pairs(lru_wa[0, 0]), _block_diag_pairs(lru_wx[0, 0]),
                           _block_diag_pairs(lru_wa[0, 1]), _block_diag_pairs(lru_wx[0, 1])], axis=2).astype(bf16)
    conv_b_f = conv_b
    sink = attn_sink

    xn, proj = _rms_matmul("rms_proj", xs, norm_mix_g + late_token[0:1, 0:1], w_in_g, 1024)
    y_a, lru_state = _lru_fwd(proj, conv_w_f, conv_b_f, lam_f, ba_f, bx_f, wbd)
    y_b = _attn_fwd(proj, sink)
    merged = _merge_fwd(proj, b_gate, y_a, y_b, 1024)
    late_src, late_land = _split_wait("gather_late_wait", _gather_copies(3), late_send, late_recv, late_src, late_land, merged)
    chip_arr = chip.reshape(1).astype(jnp.int32)
    w_ffn_in_g, w_out_g, w_ffn_out_g = _place_own(chip_arr, late_src, late_land, 4)
    w_out_f = w_out_g.reshape(D, D)
    w_ffn_out_f = w_ffn_out_g.reshape(FF, D)
    x1 = _mm_residual("out_proj", merged, w_out_f, xs, 512)
    xn2, gu, act = _rms_matmul_swiglu("rms_ffn_in", x1, norm_ffn_g, w_ffn_in_g, 1024)
    x2 = _mm_residual("ffn_out", act, w_ffn_out_f, x1, 512)
    dx2, loss_row, dg3 = _final_loss_bwd(x2, norm_final_g.reshape(1, D), tgt, 512)

    tm = min(1024, S)
    tk = min(2048, S)
    gw_ffn_out = _mm_tn("dw_ffn_out", act, pl.BlockSpec((tk, SHW), lambda i, k: (k, i)),
                        dx2, pl.BlockSpec((tk, D), lambda i, k: (k, 0)),
                        _sds((FF, D), f32), pl.BlockSpec((SHW, D), lambda i, k: (i, 0)), (2, S // tk), (SHW, D))
    dgu = _swiglu_bwd(dx2, w_ffn_out_f, gu, 256)
    dxn2 = _mm_nt_groups("dxn2", dgu, pl.BlockSpec((None, tm, SHW), lambda i, g: (g // 2, i, g % 2)), w_ffn_in_g, S, tm)
    gw_ffn_in = _mm_tn("dw_ffn_in", xn2, pl.BlockSpec((tk, D), lambda g, k: (k, 0)),
                       dgu, pl.BlockSpec((None, tk, SHW), lambda g, k: (g // 2, k, g % 2)),
                       _sds((NCHIP, D, SHW), f32), pl.BlockSpec((None, D, SHW), lambda g, k: (g, 0, 0)),
                       (NCHIP, S // tk), (D, SHW))
    c_arr = cc.reshape(1).astype(jnp.int32)
    early_names, early_tiles = ["w_ffn_in", "w_ffn_out"], [256, 352]
    early = [gw_ffn_in.reshape(NCHIP, 2, D // 2, SHW), gw_ffn_out.reshape(NCHIP, 2, FF // NCHIP // 2, D)]
    ea_send, ea_recv, ea_src, ea_land, ea_token = _split_start(
        "reduce_early_sibling_start", 2, _sibling_half_copies(2), early,
        [_sds((NCHIP,) + g.shape[2:], f32) for g in early], dxn2)
    dx1, dg2 = _rms_bwd("rms_ffn_bwd", x1, norm_ffn_g + ea_token[0:1, 0:1], dxn2, dx2, 512)

    dmerged = _mm_nt_resident("d_merged", dx1, w_out_f, 512)
    gw_out = _mm_tn("dw_out", merged, pl.BlockSpec((tk, D), lambda i, k: (k, 0)),
                    dx1, pl.BlockSpec((tk, D), lambda i, k: (k, 0)),
                    _sds((D, D), f32), pl.BlockSpec((D, D), lambda i, k: (0, 0)), (1, S // tk), (D, D))
    dproj, dy, db_gate = _merge_bwd(proj, b_gate, y_a, y_b, dmerged, 1024)
    ea_src, ea_land = _split_wait("reduce_early_sibling_wait", _sibling_half_copies(2), ea_send, ea_recv, ea_src, ea_land, dy)
    early_pairs = [_pair_sum("pair_sum_" + nm, c_arr, g4, r, th)
                   for nm, g4, r, th in zip(early_names, ea_src, ea_land, early_tiles)]
    eb_send, eb_recv, eb_src, eb_land, eb_token = _split_start(
        "reduce_early_chips_start", 6, _chip_part_copies(2), [p[1] for p in early_pairs],
        [_sds((3,) + p[1].shape[1:], bf16) for p in early_pairs], early_pairs[0][0])
    dproj, dsink = _attn_bwd(proj, sink + eb_token[0:1, 0:1], y_b, dy, dproj)
    _, eb_land = _split_wait("reduce_early_chips_wait", _chip_part_copies(2), eb_send, eb_recv, eb_src, eb_land, dsink)
    early_halves = [_chip_sum("chip_sum_" + nm, chip_arr, p[0], r3, th)
                    for nm, p, r3, th in zip(early_names, early_pairs, eb_land, early_tiles)]
    ec_send, ec_recv, ec_src, ec_land, ec_token = _split_start(
        "reduce_early_share_start", 2, _sibling_whole_copies(2), early_halves, [_sds(h.shape, f32) for h in early_halves], dsink)
    dproj, dcw, dcb, dlam, dba, dbx, dwbd = _lru_bwd(proj, dy, lru_state, dproj, conv_w_f, conv_b_f + ec_token[0:1, 0:1], lam_f,
                                                     ba_f, bx_f, wbd)
    early_halves, early_other = _split_wait("reduce_early_share_wait", _sibling_whole_copies(2), ec_send, ec_recv, ec_src, ec_land, dcb)
    gw_in = _mm_tn("dw_in", xn, pl.BlockSpec((tk, D), lambda g, k: (k, 0)),
                   dproj, pl.BlockSpec((tk, SHW), lambda g, k: (k, g)),
                   _sds((NCHIP, D, SHW), f32), pl.BlockSpec((None, D, SHW), lambda g, k: (g, 0, 0)),
                   (NCHIP, S // tk), (D, SHW))
    wa_send, wa_recv, wa_src, wa_land, wa_token = _split_start(
        "reduce_w_in_sibling_start", 1, _sibling_half_copies(1), [gw_in.reshape(NCHIP, 2, D // 2, SHW)],
        [_sds((NCHIP, D // 2, SHW), f32)], dproj)
    dxn =_mm_nt_groups("dxn", dproj, pl.BlockSpec((tm, SHW), lambda i, g: (i, g)), w_in_g, S, tm)
    wa_src, wa_land = _split_wait("reduce_w_in_sibling_wait", _sibling_half_copies(1), wa_send, wa_recv, wa_src, wa_land, dxn)
    w_in_pair = _pair_sum("pair_sum_w_in", c_arr, wa_src[0], wa_land[0], 256)
    wb_send, wb_recv, wb_src, wb_land, wb_token = _split_start(
        "reduce_w_in_chips_start", 3, _chip_part_copies(1), [w_in_pair[1]], [_sds((3, D // 2, SHW), bf16)], w_in_pair[0])
    grad_x, dg1 = _rms_bwd("rms_mix_bwd", xs, norm_mix_g + wb_token[0:1, 0:1], dxn, dx1, 512)
    _, wb_land = _split_wait("reduce_w_in_chips_wait", _chip_part_copies(1), wb_send, wb_recv, wb_src, wb_land, dg1)
    w_in_half = _chip_sum("chip_sum_w_in", chip_arr, w_in_pair[0], wb_land[0], 256)

    d_wa = jnp.stack([_diag_blocks(dwbd[:, :, 0:CW]), _diag_blocks(dwbd[:, :, 2 * CW:3 * CW])])
    d_wx = jnp.stack([_diag_blocks(dwbd[:, :, CW:2 * CW]), _diag_blocks(dwbd[:, :, 3 * CW:4 * CW])])
    small_full = [dg1, db_gate, dcw, dcb, dlam, d_wa, dba, d_wx, dbx, dsink[:, 0], dg2, dg3,
                  loss_row[0, 0:1]]
    full_shapes = [(1, D), (1, 2 * D), (4, D), (1, D), (2, D), (2, NH, HD, HD), (2, D), (2, NH, HD, HD), (2, D), (NH,),
                   (1, D), (1, D), (1,)]
    rows_full = _rows_for([math.prod(s) for s in full_shapes], 16)
    small_vec = _pack(small_full, rows_full)

    late_names, late_tiles = ["w_in", "w_out"], [256, 128]
    big = [gw_out.reshape(NCHIP, 2, D // NCHIP // 2, D)]
    *recv_a, small_sib = _sibling_halves(big, small_vec)
    w_out_pair = _pair_sum("pair_sum_w_out", c_arr, big[0], recv_a[0], 128)
    small_chip = _add2("pair_sum_small", small_vec, small_sib).reshape(2, rows_full // 2, 128)
    *recv_b, small_all = _exchange_chips([w_out_pair[1]], small_chip)
    w_out_half = _chip_sum("chip_sum_w_out", chip_arr, w_out_pair[0], recv_b[0], 128)
    halves = [w_in_half, w_out_half, _sum4("chip_sum_small", small_all, rows_full // 2)]
    *recv_c, small_other = _share_sibling(halves)
    small_lo = jnp.where(cc == 0, halves[2], small_other)
    small_hi = jnp.where(cc == 0, small_other, halves[2])
    g_full = _unpack(jnp.concatenate([small_lo, small_hi], axis=0), full_shapes)

    out_big = {}
    for nm, w, g_own, g_recv, m, v, th in zip(late_names + early_names, [w_in, w_out, w_ffn_in, w_ffn_out],
                                              halves[:2] + early_halves, recv_c + early_other,
                                              [m_w_in, m_w_out, m_w_ffn_in, m_w_ffn_out],
                                              [v_w_in, v_w_out, v_w_ffn_in, v_w_ffn_out], late_tiles + early_tiles):
        g_, d_, m_, v_ = _adamw_halves("adamw_" + nm, c_arr, w[0], g_own, g_recv, m[0], v[0], th)
        out_big[nm] = (g_[None], d_[None], m_[None], v_[None])

    small_names = ["norm_mix_g", "b_gate", "conv_w", "conv_b", "lru_lambda", "lru_wa", "lru_ba", "lru_wx", "lru_bx", "attn_sink",
                   "norm_ffn_g", "norm_final_g"]
    sharded = {"conv_w", "lru_lambda", "lru_ba", "lru_bx"}
    small_w = [norm_mix_g, b_gate, conv_w, conv_b, lru_lambda, lru_wa, lru_ba, lru_wx, lru_bx, attn_sink, norm_ffn_g, norm_final_g]
    small_m = [m_norm_mix_g, m_b_gate, m_conv_w, m_conv_b, m_lru_lambda, m_lru_wa, m_lru_ba, m_lru_wx, m_lru_bx, m_attn_sink,
               m_norm_ffn_g, m_norm_final_g]
    small_v = [v_norm_mix_g, v_b_gate, v_conv_w, v_conv_b, v_lru_lambda, v_lru_wa, v_lru_ba, v_lru_wx, v_lru_bx, v_attn_sink,
               v_norm_ffn_g, v_norm_final_g]
    g_local = []
    for nm, g, w in zip(small_names, g_full, small_w):
        if nm in sharded:
            g = lax.dynamic_slice_in_dim(g, chip * SW, SW, axis=1)
        g_local.append(g.reshape(w.shape))
    local_shapes = [w.shape for w in small_w]
    rows_local = _rows_for([math.prod(s) for s in local_shapes], 8)
    d_s, m_s, v_s = _adamw("adamw_small", _pack(small_w, rows_local), _pack(g_local, rows_local),
                           _pack(small_m, rows_local), _pack(small_v, rows_local), rows_local)
    d_l, m_l, v_l = _unpack(d_s, local_shapes), _unpack(m_s, local_shapes), _unpack(v_s, local_shapes)
    res = {nm: (g_local[i], d_l[i], m_l[i], v_l[i]) for i, nm in enumerate(small_names)}
    res.update(out_big)

    order = ["norm_mix_g", "w_in", "b_gate", "conv_w", "conv_b", "lru_lambda", "lru_wa", "lru_ba", "lru_wx", "lru_bx", "attn_sink",
             "w_out", "norm_ffn_g", "w_ffn_in", "w_ffn_out", "norm_final_g"]
    outs = [g_full[-1][0], grad_x[None]]
    for k in range(4):
        outs += [res[nm][k] for nm in order]
    return tuple(outs)
```

```python
import functools
import math

import jax
import jax.numpy as jnp
from jax import lax
from jax.experimental import pallas as pl
from jax.experimental.pallas import tpu as pltpu

f32 = jnp.float32
bf16 = jnp.bfloat16

D = 1024
NH = 16
HD = 64
FF = 2816
INW = 5632
NCHIP = 4
SHW = INW // NCHIP
CW = 128
NCH = D // CW
BLK = 128
EPS = 1e-6
NEG_INF = -1e30
RGLRU_C = 8.0
ADAM_LR, ADAM_B1, ADAM_B2, ADAM_EPS, ADAM_WD, ADAM_STEP = 0.001, 0.9, 0.999, 1e-08, 0.01, 10
VMEM_LIMIT = 58 * 1024 * 1024
MESH = pl.DeviceIdType.MESH
ANY = pl.BlockSpec(memory_space=pl.ANY)

COL_U, COL_G, COL_Q, COL_K, COL_V, COL_Z0, COL_Z1 = 0, 4, 8, 12, 13, 14, 18
MERGE_W = 512
MERGE_Z0, MERGE_Z1 = (COL_Z0 * 256) // MERGE_W, (COL_Z1 * 256) // MERGE_W


def _params(n_axes, vmem=False):
    return pltpu.CompilerParams(dimension_semantics=("arbitrary",) * n_axes,
                                vmem_limit_bytes=VMEM_LIMIT if vmem else None)


def _sds(shape, dtype):
    return jax.ShapeDtypeStruct(tuple(shape), dtype)


_DIMS = {"nn": (((1,), (0,)), ((), ())), "nt": (((1,), (1,)), ((), ())), "tn": (((0,), (0,)), ((), ()))}


def _mm(name, mode, a, a_spec, b, b_spec, out_shape, out_spec, grid, nk, acc_shape, add=None, add_spec=None):
    has_add = add is not None

    def body(*refs):
        a_ref, b_ref = refs[0], refs[1]
        add_ref = refs[2] if has_add else None
        o_ref = refs[2 + has_add]
        part = lax.dot_general(a_ref[...].astype(bf16), b_ref[...].astype(bf16), _DIMS[mode],
                               preferred_element_type=f32)
        if nk == 1:
            if has_add:
                part = add_ref[...] + part
            o_ref[...] = part.astype(o_ref.dtype)
            return
        acc_ref = refs[3 + has_add]
        k = pl.program_id(len(grid) - 1)

        @pl.when(k == 0)
        def _():
            acc_ref[...] = part

        @pl.when(k > 0)
        def _():
            acc_ref[...] += part

        @pl.when(k == nk - 1)
        def _():
            res = acc_ref[...]
            if has_add:
                res = add_ref[...] + res
            o_ref[...] = res.astype(o_ref.dtype)

    ins = [a, b] + ([add] if has_add else [])
    in_specs = [a_spec, b_spec] + ([add_spec] if has_add else [])
    scratch = [pltpu.VMEM(acc_shape, f32)] if nk > 1 else []
    return pl.pallas_call(body, name=name, grid=grid, in_specs=in_specs, out_specs=out_spec, out_shape=out_shape,
                          scratch_shapes=scratch, compiler_params=_params(len(grid), True))(*ins)


def _rms_matmul(name, x, g, w3, tm):
    S, K = x.shape
    G, _, Nw = w3.shape
    tm = min(tm, S)

    def body(x_ref, g_ref, w_ref, xn_ref, o_ref, xs_ref):
        @pl.when(pl.program_id(1) == 0)
        def _():
            xf = x_ref[...]
            r = lax.rsqrt(jnp.mean(xf * xf, axis=-1, keepdims=True) + EPS)
            xn = ((xf * r) * g_ref[...]).astype(bf16)
            xs_ref[...] = xn
            xn_ref[...] = xn

        o_ref[...] = jnp.dot(xs_ref[...], w_ref[...], preferred_element_type=f32).astype(bf16)

    return pl.pallas_call(
        body, name=name, grid=(S // tm, G),
        in_specs=[pl.BlockSpec((tm, K), lambda i, j: (i, 0)), pl.BlockSpec((1, K), lambda i, j: (0, 0)),
                  pl.BlockSpec((None, K, Nw), lambda i, j: (j, 0, 0))],
        out_specs=[pl.BlockSpec((tm, K), lambda i, j: (i, 0)), pl.BlockSpec((tm, Nw), lambda i, j: (i, j))],
        out_shape=[_sds((S, K), bf16), _sds((S, G * Nw), bf16)],
        scratch_shapes=[pltpu.VMEM((tm, K), bf16)], compiler_params=_params(2, True))(x, g, w3)


def _rms_matmul_swiglu(name, x, g, w3, tm):
    S, K = x.shape
    G, _, Nw = w3.shape
    tm = min(tm, S)
    half = G // 2

    def body(x_ref, g_ref, wg_ref, wu_ref, xn_ref, gu_ref, act_ref, xs_ref):
        @pl.when(pl.program_id(1) == 0)
        def _():
            xf = x_ref[...]
            r = lax.rsqrt(jnp.mean(xf * xf, axis=-1, keepdims=True) + EPS)
            xn = ((xf * r) * g_ref[...]).astype(bf16)
            xs_ref[...] = xn
            xn_ref[...] = xn

        xn = xs_ref[...]
        gate = jnp.dot(xn, wg_ref[...], preferred_element_type=f32)
        up = jnp.dot(xn, wu_ref[...], preferred_element_type=f32)
        gu_ref[0] = gate.astype(bf16)
        gu_ref[1] = up.astype(bf16)
        act_ref[...] = ((gate * _sigmoid(gate)) * up).astype(bf16)

    return pl.pallas_call(
        body, name=name, grid=(S // tm, half),
        in_specs=[pl.BlockSpec((tm, K), lambda i, j: (i, 0)), pl.BlockSpec((1, K), lambda i, j: (0, 0)),
                  pl.BlockSpec((None, K, Nw), lambda i, j: (j, 0, 0)),
                  pl.BlockSpec((None, K, Nw), lambda i, j: (half + j, 0, 0))],
        out_specs=[pl.BlockSpec((tm, K), lambda i, j: (i, 0)), pl.BlockSpec((2, tm, Nw), lambda i, j: (0, i, j)),
                   pl.BlockSpec((tm, Nw), lambda i, j: (i, j))],
        out_shape=[_sds((S, K), bf16), _sds((2, S, half * Nw), bf16), _sds((S, half * Nw), bf16)],
        scratch_shapes=[pltpu.VMEM((tm, K), bf16)], compiler_params=_params(2, True))(x, g, w3, w3)


def _mm_residual(name, a, w, res, tm):
    S, K = a.shape
    N = w.shape[1]
    tm = min(tm, S)
    return _mm(name, "nn", a, pl.BlockSpec((tm, K), lambda i: (i, 0)), w, pl.BlockSpec((K, N), lambda i: (0, 0)),
               _sds((S, N), f32), pl.BlockSpec((tm, N), lambda i: (i, 0)), (S // tm,), 1, None,
               add=res, add_spec=pl.BlockSpec((tm, N), lambda i: (i, 0)))


def _mm_nt_resident(name, a, w, tm):
    S, K = a.shape
    N = w.shape[0]
    tm = min(tm, S)
    return _mm(name, "nt", a, pl.BlockSpec((tm, K), lambda i: (i, 0)), w, pl.BlockSpec((N, K), lambda i: (0, 0)),
               _sds((S, N), f32), pl.BlockSpec((tm, N), lambda i: (i, 0)), (S // tm,), 1, None)


def _mm_nt_groups(name, a, a_spec, w3, S, tm):
    G, Dout, Kw = w3.shape
    return _mm(name, "nt", a, a_spec, w3, pl.BlockSpec((None, Dout, Kw), lambda i, g: (g, 0, 0)),
               _sds((S, Dout), f32), pl.BlockSpec((tm, Dout), lambda i, g: (i, 0)), (S // tm, G), G, (tm, Dout))


def _mm_tn(name, a, a_spec, b, b_spec, out_shape, out_spec, grid, acc_shape):
    return _mm(name, "tn", a, a_spec, b, b_spec, out_shape, out_spec, grid, grid[-1], acc_shape)


def _sigmoid(x):
    return 0.5 * jnp.tanh(0.5 * x) + 0.5


_GELU_C = math.sqrt(2.0 / math.pi)


def _gelu_and_grad(x):
    v = _GELU_C * (x + 0.044715 * (x * x * x))
    t = jnp.tanh(v)
    gl = 0.5 * x * (1.0 + t)
    dgl = 0.5 * (1.0 + t) + 0.5 * x * (1.0 - t * t) * (_GELU_C * (1.0 + 3.0 * 0.044715 * (x * x)))
    return gl, dgl


def _one_minus_exp2x(x, ex):
    y = 2.0 * x
    series = y * (1.0 + y * (0.5 + y * (1.0 / 6.0 + y * (1.0 / 24.0))))
    return jnp.where(y > -1.0 / 64.0, -series, 1.0 - ex * ex)


def _merge_fwd(proj, b_gate, y_a, y_b, tm):
    S = proj.shape[0]
    tm = min(tm, S)

    def body(z0_ref, z1_ref, b0_ref, b1_ref, ya_ref, yb_ref, o_ref):
        g0 = _sigmoid(z0_ref[...].astype(f32) + b0_ref[...])
        g1 = _sigmoid(z1_ref[...].astype(f32) + b1_ref[...])
        o_ref[...] = (g0 * ya_ref[...].astype(f32) + g1 * yb_ref[...].astype(f32)).astype(bf16)

    blk = lambda off: pl.BlockSpec((tm, MERGE_W), lambda j, i: (i, off + j))
    vec = lambda off: pl.BlockSpec((1, MERGE_W), lambda j, i: (0, off + j))
    return pl.pallas_call(body, name="merge_fwd", grid=(D // MERGE_W, S // tm),
                          in_specs=[blk(MERGE_Z0), blk(MERGE_Z1), vec(0), vec(D // MERGE_W), blk(0), blk(0)],
                          out_specs=blk(0), out_shape=_sds((S, D), bf16),
                          compiler_params=_params(2))(proj, proj, b_gate, b_gate, y_a, y_b)


def _merge_bwd(proj, b_gate, y_a, y_b, dm, tm):
    S = proj.shape[0]
    tm = min(tm, S)
    per = D // MERGE_W

    def body(z_ref, b_ref, ya_ref, yb_ref, dm_ref, dz_ref, dy_ref, db_ref):
        first = pl.program_id(0) < per
        g = _sigmoid(z_ref[...].astype(f32) + b_ref[...])
        d = dm_ref[...]
        y = jnp.where(first, ya_ref[...], yb_ref[...]).astype(f32)
        dz = (d * y) * (g * (1.0 - g))
        dz_ref[...] = dz.astype(bf16)
        dy_ref[...] = (d * g).astype(bf16)

        @pl.when(pl.program_id(1) == 0)
        def _():
            db_ref[...] = jnp.zeros_like(db_ref)

        db_ref[...] += jnp.sum(dz, axis=0, keepdims=True)

    blk = lambda col: pl.BlockSpec((tm, MERGE_W), col)
    return pl.pallas_call(
        body, name="merge_bwd", grid=(2 * per, S // tm),
        in_specs=[blk(lambda p, i: (i, MERGE_Z0 + p)), pl.BlockSpec((1, MERGE_W), lambda p, i: (0, p)),
                  blk(lambda p, i: (jnp.where(p < per, i, 0), jnp.minimum(p, per - 1))),
                  blk(lambda p, i: (jnp.where(p < per, 0, i), jnp.maximum(p - per, 0))),
                  blk(lambda p, i: (i, p % per))],
        out_specs=[blk(lambda p, i: (i, MERGE_Z0 + p)), pl.BlockSpec((None, tm, MERGE_W), lambda p, i: (p // per, i, p % per)),
                   pl.BlockSpec((1, MERGE_W), lambda p, i: (0, p))],
        out_shape=[_sds((S, INW), bf16), _sds((2, S, D), bf16), _sds((1, 2 * D), f32)],
        compiler_params=_params(2))(proj, b_gate, y_a, y_b, dm)


def _swiglu_bwd(dx, w, gu, tm):
    S, K = dx.shape
    tm = min(tm, S)

    def body(dx_ref, w_ref, gu_ref, o_ref):
        d = lax.dot_general(dx_ref[...].astype(bf16), w_ref[...], _DIMS["nt"], preferred_element_type=f32)
        g = gu_ref[0].astype(f32)
        u = gu_ref[1].astype(f32)
        s = _sigmoid(g)
        o_ref[0] = ((d * u) * (s * (1.0 + g * (1.0 - s)))).astype(bf16)
        o_ref[1] = (d * (g * s)).astype(bf16)

    stacked = pl.BlockSpec((2, tm, FF), lambda i: (0, i, 0))
    return pl.pallas_call(body, name="swiglu_bwd", grid=(S // tm,),
                          in_specs=[pl.BlockSpec((tm, K), lambda i: (i, 0)), pl.BlockSpec((FF, K), lambda i: (0, 0)), stacked],
                          out_specs=stacked, out_shape=_sds((2, S, FF), bf16),
                          compiler_params=_params(1, True))(dx, w, gu)


def _ffn_out_loss_bwd(act, w, x1, g3, tgt, tm):
    S, K = act.shape
    tm = min(tm, S)

    def body(a_ref, w_ref, r_ref, g_ref, t_ref, dx_ref, loss_ref, dg_ref):
        @pl.when(pl.program_id(0) == 0)
        def _():
            loss_ref[...] = jnp.zeros_like(loss_ref)
            dg_ref[...] = jnp.zeros_like(dg_ref)

        x = r_ref[...] + jnp.dot(a_ref[...], w_ref[...], preferred_element_type=f32)
        g = g_ref[...]
        r = lax.rsqrt(jnp.mean(x * x, axis=-1, keepdims=True) + EPS)
        xh = x * r
        err = xh * g - t_ref[...]
        row = jnp.mean(err * err, axis=-1, keepdims=True)
        loss_ref[...] += 0.5 * jnp.sum(row, axis=0, keepdims=True)
        dy = err * (1.0 / D)
        dg_ref[...] += jnp.sum(dy * xh, axis=0, keepdims=True)
        dxh = dy * g
        dx_ref[...] = r * (dxh - xh * jnp.mean(dxh * xh, axis=-1, keepdims=True))

    row_blk = pl.BlockSpec((tm, D), lambda i: (i, 0))
    vec = pl.BlockSpec((1, D), lambda i: (0, 0))
    return pl.pallas_call(body, name="ffn_out_loss_bwd", grid=(S // tm,),
                          in_specs=[pl.BlockSpec((tm, K), lambda i: (i, 0)), pl.BlockSpec((K, D), lambda i: (0, 0)),
                                    row_blk, vec, row_blk],
                          out_specs=[row_blk, pl.BlockSpec((1, 128), lambda i: (0, 0)), vec],
                          out_shape=[_sds((S, D), f32), _sds((1, 128), f32), _sds((1, D), f32)],
                          compiler_params=_params(1, True))(act, w, x1, g3, tgt)


def _rms_bwd(name, x, g, dxn, dres, tm):
    S = x.shape[0]
    tm = min(tm, S)

    def body(x_ref, g_ref, d_ref, r_ref, dx_ref, dg_ref):
        @pl.when(pl.program_id(0) == 0)
        def _():
            dg_ref[...] = jnp.zeros_like(dg_ref)

        x = x_ref[...]
        d = d_ref[...]
        r = lax.rsqrt(jnp.mean(x * x, axis=-1, keepdims=True) + EPS)
        xh = x * r
        dg_ref[...] += jnp.sum(d * xh, axis=0, keepdims=True)
        dxh = d * g_ref[...]
        dx_ref[...] = r_ref[...] + r * (dxh - xh * jnp.mean(dxh * xh, axis=-1, keepdims=True))

    row_blk = pl.BlockSpec((tm, D), lambda i: (i, 0))
    vec = pl.BlockSpec((1, D), lambda i: (0, 0))
    return pl.pallas_call(body, name=name, grid=(S // tm,), in_specs=[row_blk, vec, row_blk, row_blk],
                          out_specs=[row_blk, vec], out_shape=[_sds((S, D), f32), _sds((1, D), f32)],
                          compiler_params=_params(1))(x, g, dxn, dres)


def _mm_nt_rms_bwd(name, a, a_spec, w3, x, g, dres, tm):
    S = x.shape[0]
    G, Dout, Kw = w3.shape

    def body(a_ref, w_ref, x_ref, g_ref, r_ref, dx_ref, dg_ref, acc_ref):
        i, k = pl.program_id(0), pl.program_id(1)
        part = lax.dot_general(a_ref[...].astype(bf16), w_ref[...], _DIMS["nt"], preferred_element_type=f32)

        @pl.when(k == 0)
        def _():
            acc_ref[...] = part

        @pl.when(k > 0)
        def _():
            acc_ref[...] += part

        @pl.when(k == G - 1)
        def _():
            @pl.when(i == 0)
            def _():
                dg_ref[...] = jnp.zeros_like(dg_ref)

            for rows in (slice(0, tm // 2), slice(tm // 2, tm)):
                x_t = x_ref[rows, :]
                d = acc_ref[rows, :]
                r = lax.rsqrt(jnp.mean(x_t * x_t, axis=-1, keepdims=True) + EPS)
                xh = x_t * r
                dg_ref[...] += jnp.sum(d * xh, axis=0, keepdims=True)
                dxh = d * g_ref[...]
                dx_ref[rows, :] = r_ref[rows, :] + r * (dxh - xh * jnp.mean(dxh * xh, axis=-1, keepdims=True))

    row_blk = pl.BlockSpec((tm, Dout), lambda i, k: (i, 0))
    vec = pl.BlockSpec((1, Dout), lambda i, k: (0, 0))
    return pl.pallas_call(body, name=name, grid=(S // tm, G),
                          in_specs=[a_spec, pl.BlockSpec((None, Dout, Kw), lambda i, k: (k, 0, 0)), row_blk, vec, row_blk],
                          out_specs=[row_blk, vec], out_shape=[_sds((S, Dout), f32), _sds((1, Dout), f32)],
                          scratch_shapes=[pltpu.VMEM((tm, Dout), f32)], compiler_params=_params(2, True))(a, w3, x, g, dres)


LRU_TT = 256
SCAN_UNROLL = 8


HALO = 16


def _halo(ref, i, S):
    nt = S // LRU_TT
    t0 = pl.multiple_of(i * LRU_TT, LRU_TT)
    p0 = pl.multiple_of(jnp.maximum(t0 - HALO, 0), HALO)
    n0 = pl.multiple_of(jnp.minimum(t0 + LRU_TT, S - HALO), HALO)
    prev = jnp.where(i > 0, ref[pl.ds(p0, HALO), :].astype(f32), 0.0)
    nxt = jnp.where(i < nt - 1, ref[pl.ds(n0, HALO), :].astype(f32), 0.0)
    return jnp.concatenate([prev, ref[pl.ds(t0, LRU_TT), :].astype(f32), nxt], axis=0)


def _shift(ext, k):
    n = LRU_TT + 2 * HALO
    return pltpu.roll(ext, (-k) % n, 0)[HALO:HALO + LRU_TT]


def _lru_gates(uc, wbd, ba, bx):
    pre = jnp.dot(uc.astype(bf16), wbd, preferred_element_type=f32)
    r_f = _sigmoid(pre[:, 0:CW] + ba[0:1])
    i_f = _sigmoid(pre[:, CW:2 * CW] + bx[0:1])
    r_b = _sigmoid(pre[:, 2 * CW:3 * CW] + ba[1:2])
    i_b = _sigmoid(pre[:, 3 * CW:4 * CW] + bx[1:2])
    return r_f, i_f, r_b, i_b


def _lru_coeffs(r, sp):
    log_a = (-RGLRU_C * r) * sp
    a = jnp.exp(log_a)
    beta = jnp.sqrt(jnp.maximum(_one_minus_exp2x(log_a, a), 0.0))
    return a, beta


def _lru_coeffs_inv(r, sp):
    log_a = (-RGLRU_C * r) * sp
    a = jnp.exp(log_a)
    om = jnp.maximum(_one_minus_exp2x(log_a, a), 0.0)
    return a, jnp.sqrt(om), lax.rsqrt(om)


def _conv_tile(u_ref, i, S, cw, cb):
    ext = _halo(u_ref, i, S)
    um2, um1, u0, up1 = _shift(ext, -2), _shift(ext, -1), ext[HALO:HALO + LRU_TT], _shift(ext, 1)
    uc = um2 * cw[0:1] + um1 * cw[1:2] + u0 * cw[2:3] + up1 * cw[3:4] + cb
    return uc, (um2, um1, u0, up1)


def _scan_pair(S, fwd_a, fwd_b, fwd_out, rev_a, rev_b, rev_out):
    ng = S // 8
    idx = lax.broadcasted_iota(jnp.int32, (8, CW), 0)

    def local(a, b, rev):
        for sh in (1, 2, 4):
            if rev:
                keep = idx < 8 - sh
                amt = 8 - sh
            else:
                keep = idx >= sh
                amt = sh
            a_s = jnp.where(keep, pltpu.roll(a, amt, 0), 1.0)
            b_s = jnp.where(keep, pltpu.roll(b, amt, 0), 0.0)
            b = a * b_s + b
            a = a * a_s
        return a, b

    def step(it, carry):
        cf, cr = carry
        fwd_rows = [pl.multiple_of((it * SCAN_UNROLL + j) * 8, 8) for j in range(SCAN_UNROLL)]
        rev_rows = [pl.multiple_of((ng - 1 - (it * SCAN_UNROLL + j)) * 8, 8) for j in range(SCAN_UNROLL)]
        fwd_loc = [local(fwd_a(r), fwd_b(r), False) for r in fwd_rows]
        rev_loc = [local(rev_a(r), rev_b(r), True) for r in rev_rows]
        for j in range(SCAN_UNROLL):
            a, b = fwd_loc[j]
            h = a * cf + b
            fwd_out[pl.ds(fwd_rows[j], 8), :] = h
            cf = jnp.broadcast_to(h[7:8, :], (8, CW))
            a, b = rev_loc[j]
            h = a * cr + b
            rev_out[pl.ds(rev_rows[j], 8), :] = h
            cr = jnp.broadcast_to(h[0:1, :], (8, CW))
        return cf, cr

    zero = jnp.zeros((8, CW), f32)
    lax.fori_loop(0, ng // SCAN_UNROLL, step, (zero, zero))


def _lru_specs(S):
    seq = lambda off: pl.BlockSpec((S, CW), lambda j: (0, off + j))
    par = lambda rows: pl.BlockSpec((rows, CW), lambda j: (0, j))
    return seq, par


def _lru_fwd(proj, conv_w, conv_b, lam, ba, bx, wbd):
    S = proj.shape[0]
    nt = S // LRU_TT

    def body(u_ref, g_ref, cw_ref, cb_ref, lam_ref, ba_ref, bx_ref, wbd_ref, y_ref, state_ref, af_ref, bf_ref, ab_ref, bb_ref,
             sems):
        cw, cb, ba_v, bx_v, wbd_v = cw_ref[...], cb_ref[...], ba_ref[...], bx_ref[...], wbd_ref[...]
        sp = jax.nn.softplus(-lam_ref[...])
        cols = pl.ds(pl.multiple_of(pl.program_id(0) * CW, CW), CW)
        save = [pltpu.make_async_copy(ref, state_ref.at[k, :, cols], sems.at[k])
                for k, ref in enumerate((af_ref, bf_ref, ab_ref, bb_ref))]

        def phase1(i, c):
            uc, _ = _conv_tile(u_ref, i, S, cw, cb)
            r_f, i_f, r_b, i_b = _lru_gates(uc, wbd_v, ba_v, bx_v)
            rows = pl.ds(pl.multiple_of(i * LRU_TT, LRU_TT), LRU_TT)
            a, beta = _lru_coeffs(r_f, sp[0:1])
            af_ref[rows, :] = a
            bf_ref[rows, :] = beta * (i_f * uc)
            a, beta = _lru_coeffs(r_b, sp[1:2])
            ab_ref[rows, :] = a
            bb_ref[rows, :] = beta * (i_b * uc)
            return c

        lax.fori_loop(0, nt, phase1, 0)
        row8 = lambda ref: (lambda r0: ref[pl.ds(r0, 8), :])
        _scan_pair(S, row8(af_ref), row8(bf_ref), bf_ref, row8(ab_ref), row8(bb_ref), bb_ref)
        for cp in save:
            cp.start()

        def phase3(i, c):
            rows = pl.ds(pl.multiple_of(i * LRU_TT, LRU_TT), LRU_TT)
            y = (bf_ref[rows, :] + bb_ref[rows, :]) * jax.nn.gelu(g_ref[rows, :].astype(f32))
            y_ref[rows, :] = y.astype(y_ref.dtype)
            return c

        lax.fori_loop(0, nt, phase3, 0)
        for cp in save:
            cp.wait()

    seq, par = _lru_specs(S)
    return pl.pallas_call(
        body, name="lru_fwd", grid=(NCH,),
        in_specs=[seq(0), seq(NCH), par(4), par(1), par(2), par(2), par(2),
                  pl.BlockSpec((None, CW, 4 * CW), lambda j: (j, 0, 0))],
        out_specs=[seq(0), ANY], out_shape=[_sds((S, D), bf16), _sds((4, S, D), f32)],
        scratch_shapes=[pltpu.VMEM((S, CW), f32)] * 4 + [pltpu.SemaphoreType.DMA((4,))], compiler_params=_params(1, True),
    )(proj, proj, conv_w, conv_b, lam, ba, bx, wbd)


def _lru_bwd(proj, dy, state, dproj, conv_w, conv_b, lam, ba, bx, wbd):
    S = proj.shape[0]
    nt = S // LRU_TT

    def body(u_ref, g_ref, dy_ref, state_ref, dproj_in, cw_ref, cb_ref, lam_ref, ba_ref, bx_ref, wbd_ref,
             dproj_ref, dcw_ref, dcb_ref, dlam_ref, dba_ref, dbx_ref, dwbd_ref,
             af_ref, bf_ref, ab_ref, bb_ref, dh_ref, du_ref, dg_ref, sems):
        cw, cb, ba_v, bx_v, wbd_v = cw_ref[...], cb_ref[...], ba_ref[...], bx_ref[...], wbd_ref[...]
        lam_v = lam_ref[...]
        sp = jax.nn.softplus(-lam_v)
        chunk = pl.program_id(0)

        def out_copies(j):
            c0 = pl.multiple_of(j * CW, CW)
            return [pltpu.make_async_copy(du_ref, dproj_ref.at[:, pl.ds(c0, CW)], sems.at[4]),
                    pltpu.make_async_copy(dg_ref, dproj_ref.at[:, pl.ds(D + c0, CW)], sems.at[5])]

        @pl.when(chunk >= 1)
        def _():
            for cp in out_copies(chunk - 1):
                cp.wait()

        cols = pl.ds(pl.multiple_of(chunk * CW, CW), CW)
        load = [pltpu.make_async_copy(state_ref.at[k, :, cols], ref, sems.at[k])
                for k, ref in enumerate((af_ref, bf_ref, ab_ref, bb_ref))]
        for cp in load:
            cp.start()
        for cp in load:
            cp.wait()
        row8 = lambda ref: (lambda r0: ref[pl.ds(r0, 8), :])

        def phase0(i, c):
            rows = pl.ds(pl.multiple_of(i * LRU_TT, LRU_TT), LRU_TT)
            gl, dgl = _gelu_and_grad(g_ref[rows, :].astype(f32))
            dyt = dy_ref[rows, :].astype(f32)
            dh_ref[rows, :] = dyt * gl
            dg_ref[rows, :] = ((dyt * (bf_ref[rows, :] + bb_ref[rows, :])) * dgl).astype(dg_ref.dtype)
            return c

        lax.fori_loop(0, nt, phase0, 0)

        def scaled_dh(a_ref):
            def f(r0):
                return a_ref[pl.ds(r0, 8), :] * dh_ref[pl.ds(r0, 8), :]
            return f

        _scan_pair(S, row8(ab_ref), scaled_dh(ab_ref), ab_ref, row8(af_ref), scaled_dh(af_ref), af_ref)

        dcw_ref[...] = jnp.zeros_like(dcw_ref)
        dcb_ref[...] = jnp.zeros_like(dcb_ref)
        dlam_ref[...] = jnp.zeros_like(dlam_ref)
        dba_ref[...] = jnp.zeros_like(dba_ref)
        dbx_ref[...] = jnp.zeros_like(dbx_ref)
        dwbd_ref[...] = jnp.zeros_like(dwbd_ref)

        def direction(uc, r, i_g, dht, h_nb, sp_d):
            a, beta, inv_beta = _lru_coeffs_inv(r, sp_d)
            da = dht * h_nb
            dbeta = dht * (i_g * uc)
            d_iu = dht * beta
            dlog_a = da * a - (a * a) * (dbeta * inv_beta)
            dlr = dlog_a * r
            dsp = -RGLRU_C * jnp.sum(dlr, axis=0, keepdims=True)
            dpre_r = (dlr * (1.0 - r)) * (-RGLRU_C * sp_d)
            dpre_i = (d_iu * uc) * (i_g * (1.0 - i_g))
            return dpre_r, dpre_i, d_iu * i_g, dsp

        def phase4(i, c):
            uc, (um2, um1, u0, up1) = _conv_tile(u_ref, i, S, cw, cb)
            r_f, i_f, r_b, i_b = _lru_gates(uc, wbd_v, ba_v, bx_v)
            rows = pl.ds(pl.multiple_of(i * LRU_TT, LRU_TT), LRU_TT)
            dh = dh_ref[rows, :]
            dht_f = dh + _shift(_halo(af_ref, i, S), 1)
            h_prev = _shift(_halo(bf_ref, i, S), -1)
            dht_b = dh + _shift(_halo(ab_ref, i, S), -1)
            h_next = _shift(_halo(bb_ref, i, S), 1)
            prf, pif, duc_f, dsp_f = direction(uc, r_f, i_f, dht_f, h_prev, sp[0:1])
            prb, pib, duc_b, dsp_b = direction(uc, r_b, i_b, dht_b, h_next, sp[1:2])
            dpre = jnp.concatenate([prf, pif, prb, pib], axis=1)
            dpre_b = dpre.astype(bf16)
            duc = (duc_f + duc_b) + lax.dot_general(dpre_b, wbd_v, _DIMS["nt"], preferred_element_type=f32)
            dwbd_ref[...] += lax.dot_general(uc.astype(bf16), dpre_b, _DIMS["tn"], preferred_element_type=f32)
            colsum = lambda v: jnp.sum(v, axis=0, keepdims=True)
            dba_ref[...] += jnp.concatenate([colsum(prf), colsum(prb)], axis=0)
            dbx_ref[...] += jnp.concatenate([colsum(pif), colsum(pib)], axis=0)
            dlam_ref[...] += jnp.concatenate([dsp_f, dsp_b], axis=0)
            dcb_ref[...] += colsum(duc)
            dcw_ref[...] += jnp.concatenate([colsum(duc * um2), colsum(duc * um1), colsum(duc * u0),
                                             colsum(duc * up1)], axis=0)
            af_ref[rows, :] = duc
            return c

        lax.fori_loop(0, nt, phase4, 0)
        dlam_ref[...] = dlam_ref[...] * (-_sigmoid(-lam_v))

        def phase5(i, c):
            ext = _halo(af_ref, i, S)
            rows = pl.ds(pl.multiple_of(i * LRU_TT, LRU_TT), LRU_TT)
            du = (_shift(ext, 2) * cw[0:1] + _shift(ext, 1) * cw[1:2] + ext[HALO:HALO + LRU_TT] * cw[2:3]
                  + _shift(ext, -1) * cw[3:4])
            du_ref[rows, :] = du.astype(du_ref.dtype)
            return c

        lax.fori_loop(0, nt, phase5, 0)
        for cp in out_copies(chunk):
            cp.start()

        @pl.when(chunk == NCH - 1)
        def _():
            for cp in out_copies(chunk):
                cp.wait()

    seq, par = _lru_specs(S)
    return pl.pallas_call(
        body, name="lru_bwd", grid=(NCH,),
        in_specs=[seq(0), seq(NCH), pl.BlockSpec((None, S, CW), lambda j: (0, 0, j)), ANY, ANY,
                  par(4), par(1), par(2), par(2), par(2), pl.BlockSpec((None, CW, 4 * CW), lambda j: (j, 0, 0))],
        out_specs=[ANY, par(4), par(1), par(2), par(2), par(2),
                   pl.BlockSpec((None, CW, 4 * CW), lambda j: (j, 0, 0))],
        out_shape=[_sds(dproj.shape, bf16), _sds((4, D), f32), _sds((1, D), f32), _sds((2, D), f32),
                   _sds((2, D), f32), _sds((2, D), f32), _sds((NCH, CW, 4 * CW), f32)],
        scratch_shapes=[pltpu.VMEM((S, CW), f32)] * 5 + [pltpu.VMEM((S, CW), bf16)] * 2 + [pltpu.SemaphoreType.DMA((6,))],
        input_output_aliases={4: 0}, compiler_params=_params(1, True),
    )(proj, proj, dy, state, dproj, conv_w, conv_b, lam, ba, bx, wbd)


_SLOPES = [2.0 ** (-8.0 * (h + 1) / NH) for h in range(NH)]


def _half_mask(shape, e):
    lane = lax.broadcasted_iota(jnp.int32, shape, 1)
    return (lane < HD) if e == 0 else (lane >= HD)


def _both_halves(x, src):
    return jnp.where(_half_mask(x.shape, src), x, pltpu.roll(x, HD, 1))


def _attn_base(n, S):
    tq = lax.broadcasted_iota(jnp.int32, (BLK, 3 * BLK), 0)
    sk = lax.broadcasted_iota(jnp.int32, (BLK, 3 * BLK), 1)
    dist = jnp.abs(tq + BLK - sk)
    kpos = n * BLK - BLK + sk
    valid = (dist <= BLK) & (kpos >= 0) & (kpos < S)
    return jnp.where(valid, -dist.astype(f32), NEG_INF)


def _group_heads(ref, kvh, scale):
    parts = []
    for i in range(4):
        pair = 2 * kvh + i // 2
        x = ref[:, pair * 128:(pair + 1) * 128].astype(f32)
        parts.append(jnp.where(_half_mask(x.shape, i % 2), x * scale, 0.0))
    return parts


def _stack_bf16(parts):
    return jnp.concatenate([p.astype(bf16) for p in parts], axis=0)


def _attn_softmax(s_raw, base, slope, sink):
    s = s_raw + slope * base
    m = jnp.maximum(jnp.max(s, axis=-1, keepdims=True), sink)
    p = jnp.exp(s - m)
    esink = jnp.exp(sink - m)
    inv = 1.0 / (jnp.sum(p, axis=-1, keepdims=True) + esink)
    return p, inv, esink * inv


def _attn_specs(S):
    nb = S // BLK
    q_spec = pl.BlockSpec((BLK, D), lambda n: (n, 2))
    kv = lambda col: [pl.BlockSpec((BLK, 256), lambda n: (jnp.maximum(n - 1, 0), col)),
                      pl.BlockSpec((BLK, 256), lambda n: (n, col)),
                      pl.BlockSpec((BLK, 256), lambda n: (jnp.minimum(n + 1, nb - 1), col))]
    return nb, q_spec, kv(COL_K), kv(COL_V)


def _attn_fwd(proj, sink):
    S = proj.shape[0]
    nb, q_spec, k_specs, v_specs = _attn_specs(S)

    def body(sink_ref, q_ref, kp_ref, kc_ref, kn_ref, vp_ref, vc_ref, vn_ref, o_ref):
        base = _attn_base(pl.program_id(0), S)
        kcat = jnp.concatenate([kp_ref[...], kc_ref[...], kn_ref[...]], axis=0).astype(f32)
        vcat = jnp.concatenate([vp_ref[...], vc_ref[...], vn_ref[...]], axis=0).astype(f32)
        even = _half_mask((BLK, 128), 0)
        for kvh in range(NH // 4):
            ch, off = kvh // 2, kvh % 2
            kb = _both_halves(kcat[:, ch * 128:(ch + 1) * 128], off).astype(bf16)
            vb = _both_halves(vcat[:, ch * 128:(ch + 1) * 128], off).astype(bf16)
            q4 = _stack_bf16(_group_heads(q_ref, kvh, HD ** -0.5))
            s4 = lax.dot_general(q4, kb, _DIMS["nt"], preferred_element_type=f32)
            ps, invs = [], []
            for i in range(4):
                h = 4 * kvh + i
                p, inv, _ = _attn_softmax(s4[i * BLK:(i + 1) * BLK], base, _SLOPES[h], sink_ref[0, h])
                ps.append(p)
                invs.append(inv)
            o4 = jnp.dot(_stack_bf16(ps), vb, preferred_element_type=f32)
            for pr in range(2):
                lo = o4[(2 * pr) * BLK:(2 * pr + 1) * BLK] * invs[2 * pr]
                hi = o4[(2 * pr + 1) * BLK:(2 * pr + 2) * BLK] * invs[2 * pr + 1]
                pair = 2 * kvh + pr
                o_ref[:, pair * 128:(pair + 1) * 128] = jnp.where(even, lo, hi).astype(o_ref.dtype)

    return pl.pallas_call(
        body, name="attn_fwd", grid=(nb,),
        in_specs=[pl.BlockSpec(memory_space=pltpu.SMEM), q_spec] + k_specs + v_specs,
        out_specs=pl.BlockSpec((BLK, D), lambda n: (n, 0)), out_shape=_sds((S, D), bf16),
        compiler_params=_params(1, True))(sink, proj, proj, proj, proj, proj, proj, proj)


def _attn_bwd(proj, sink, y_b, dy, dproj):
    S = proj.shape[0]
    nb, q_spec, k_specs, v_specs = _attn_specs(S)
    q_col, kv_col = COL_Q * 256, COL_K * 256

    def body(sink_ref, q_ref, kp_ref, kc_ref, kn_ref, vp_ref, vc_ref, vn_ref, o_ref, do_ref, dproj_in,
             dproj_ref, dsink_ref, dk_ref, dv_ref, dq_buf, kv_buf, sems):
        n = pl.program_id(0)
        slot = n % 2
        dq_ref = dq_buf.at[slot]

        def dq_copy(step):
            rows = pl.ds(pl.multiple_of(step * BLK, BLK), BLK)
            return pltpu.make_async_copy(dq_buf.at[step % 2], dproj_ref.at[rows, pl.ds(q_col, D)], sems.at[step % 2])

        @pl.when(n >= 2)
        def _():
            dq_copy(n - 2).wait()

        @pl.when(n == 0)
        def _():
            dk_ref[...] = jnp.zeros_like(dk_ref)
            dv_ref[...] = jnp.zeros_like(dv_ref)
            dsink_ref[...] = jnp.zeros_like(dsink_ref)

        base = _attn_base(n, S)
        kcat = jnp.concatenate([kp_ref[...], kc_ref[...], kn_ref[...]], axis=0).astype(f32)
        vcat = jnp.concatenate([vp_ref[...], vc_ref[...], vn_ref[...]], axis=0).astype(f32)
        dk_rows, dv_rows = [[], []], [[], []]
        scale = HD ** -0.5
        even = _half_mask((BLK, 128), 0)
        for kvh in range(NH // 4):
            ch, off = kvh // 2, kvh % 2
            kb = _both_halves(kcat[:, ch * 128:(ch + 1) * 128], off).astype(bf16)
            vb = _both_halves(vcat[:, ch * 128:(ch + 1) * 128], off).astype(bf16)
            q_parts = _group_heads(q_ref, kvh, scale)
            d_parts = _group_heads(do_ref, kvh, 1.0)
            s4 = lax.dot_general(_stack_bf16(q_parts), kb, _DIMS["nt"], preferred_element_type=f32)
            dp4 = lax.dot_general(_stack_bf16(d_parts), vb, _DIMS["nt"], preferred_element_type=f32)
            ts, ps, qn, dn, invs = [], [], [], [], []
            for i in range(4):
                h = 4 * kvh + i
                pair = 2 * kvh + i // 2
                rows = slice(i * BLK, (i + 1) * BLK)
                p, inv, psink = _attn_softmax(s4[rows], base, _SLOPES[h], sink_ref[0, h])
                delta = jnp.sum(d_parts[i] * o_ref[:, pair * 128:(pair + 1) * 128].astype(f32), axis=-1, keepdims=True)
                dsink_ref[h:h + 1, :] += jnp.broadcast_to(-jnp.sum(psink * delta, axis=0, keepdims=True), (1, 128))
                ts.append(p * (dp4[rows] - delta))
                ps.append(p)
                qn.append(q_parts[i] * inv)
                dn.append(d_parts[i] * inv)
                invs.append(inv)
            t4 = _stack_bf16(ts)
            dq4 = jnp.dot(t4, kb, preferred_element_type=f32)
            for pr in range(2):
                lo = dq4[(2 * pr) * BLK:(2 * pr + 1) * BLK] * invs[2 * pr]
                hi = dq4[(2 * pr + 1) * BLK:(2 * pr + 2) * BLK] * invs[2 * pr + 1]
                pair = 2 * kvh + pr
                dq_ref[:, pair * 128:(pair + 1) * 128] = (jnp.where(even, lo, hi) * scale).astype(dq_ref.dtype)
            dk_t = lax.dot_general(_stack_bf16(qn), t4, _DIMS["tn"], preferred_element_type=f32)
            dv_t = lax.dot_general(_stack_bf16(dn), _stack_bf16(ps), _DIMS["tn"], preferred_element_type=f32)
            dk_rows[ch].append(dk_t[0:HD] + dk_t[HD:2 * HD])
            dv_rows[ch].append(dv_t[0:HD] + dv_t[HD:2 * HD])
        dk_acc = [jnp.concatenate(r, axis=0).T for r in dk_rows]
        dv_acc = [jnp.concatenate(r, axis=0).T for r in dv_rows]
        for j in range(3):
            blk = n + (j - 1)

            @pl.when((blk >= 0) & (blk < nb))
            def _():
                rows = pl.ds(pl.multiple_of(blk * BLK, BLK), BLK)
                for ch in range(2):
                    dk_ref[rows, ch * 128:(ch + 1) * 128] += dk_acc[ch][j * BLK:(j + 1) * BLK]
                    dv_ref[rows, ch * 128:(ch + 1) * 128] += dv_acc[ch][j * BLK:(j + 1) * BLK]

        dq_copy(n).start()

        @pl.when(n == nb - 1)
        def _():
            def cast(i, c):
                rows = pl.ds(pl.multiple_of(i * 4 * BLK, 4 * BLK), 4 * BLK)
                kv_buf[rows, 0:256] = dk_ref[rows, :].astype(bf16)
                kv_buf[rows, 256:512] = dv_ref[rows, :].astype(bf16)
                return c

            lax.fori_loop(0, S // (4 * BLK), cast, 0)
            kv_copy = pltpu.make_async_copy(kv_buf, dproj_ref.at[:, pl.ds(kv_col, 512)], sems.at[2])
            kv_copy.start()
            if nb >= 2:
                dq_copy(n - 1).wait()
            dq_copy(n).wait()
            kv_copy.wait()

    row_blk = pl.BlockSpec((BLK, D), lambda n: (n, 0))
    return pl.pallas_call(
        body, name="attn_bwd", grid=(nb,),
        in_specs=[pl.BlockSpec(memory_space=pltpu.SMEM), q_spec] + k_specs + v_specs
        + [row_blk, pl.BlockSpec((None, BLK, D), lambda n: (1, n, 0)), ANY],
        out_specs=[ANY, pl.BlockSpec((NH, 128), lambda n: (0, 0))],
        out_shape=[_sds(dproj.shape, bf16), _sds((NH, 128), f32)],
        scratch_shapes=[pltpu.VMEM((S, 256), f32), pltpu.VMEM((S, 256), f32), pltpu.VMEM((2, BLK, D), bf16),
                        pltpu.VMEM((S, 512), bf16), pltpu.SemaphoreType.DMA((3,))],
        input_output_aliases={10: 0},
        compiler_params=_params(1, True))(sink, proj, proj, proj, proj, proj, proj, proj, y_b, dy, dproj)


def _adamw(name, w, g, m, v, tr):
    R, C = w.shape
    tr = min(tr, R)

    def body(w_ref, g_ref, m_ref, v_ref, d_ref, m2_ref, v2_ref):
        g = g_ref[...]
        m2 = ADAM_B1 * m_ref[...] + (1.0 - ADAM_B1) * g
        v2 = ADAM_B2 * v_ref[...] + (1.0 - ADAM_B2) * (g * g)
        m_hat = m2 / (1.0 - ADAM_B1 ** ADAM_STEP)
        v_hat = v2 / (1.0 - ADAM_B2 ** ADAM_STEP)
        d_ref[...] = -ADAM_LR * (m_hat / (jnp.sqrt(v_hat) + ADAM_EPS) + ADAM_WD * w_ref[...])
        m2_ref[...] = m2
        v2_ref[...] = v2

    blk = pl.BlockSpec((tr, C), lambda i: (i, 0))
    return pl.pallas_call(body, name=name, grid=(R // tr,), in_specs=[blk] * 4, out_specs=[blk] * 3,
                          out_shape=[_sds((R, C), f32)] * 3, compiler_params=_params(1))(w, g, m, v)


def _pair_sum(name, c_arr, g4, recv, th):
    _, _, h, w = g4.shape
    th = min(th, h)

    def body(c_ref, g_ref, r_ref, o_ref, ob_ref):
        p = g_ref[...] + r_ref[...]
        o_ref[...] = p
        ob_ref[...] = p.astype(bf16)

    blk = pl.BlockSpec((None, th, w), lambda s, i, c_ref: (s, i, 0))
    spec = pltpu.PrefetchScalarGridSpec(
        num_scalar_prefetch=1, grid=(NCHIP, h // th),
        in_specs=[pl.BlockSpec((None, None, th, w), lambda s, i, c_ref: (s, c_ref[0], i, 0)), blk],
        out_specs=[blk, blk])
    return pl.pallas_call(body, name=name, grid_spec=spec,
                          out_shape=[_sds((NCHIP, h, w), f32), _sds((NCHIP, h, w), bf16)],
                          compiler_params=_params(2))(c_arr, g4, recv)


def _chip_sum(name, chip_arr, own4, recv3, th):
    _, h, w = own4.shape
    th = min(th, h)

    def body(s_ref, o_ref, r_ref, out_ref):
        out_ref[...] = ((o_ref[...] + r_ref[0].astype(f32)) + r_ref[1].astype(f32)) + r_ref[2].astype(f32)

    spec = pltpu.PrefetchScalarGridSpec(
        num_scalar_prefetch=1, grid=(h // th,),
        in_specs=[pl.BlockSpec((None, th, w), lambda i, s_ref: (s_ref[0], i, 0)),
                  pl.BlockSpec((3, th, w), lambda i, s_ref: (0, i, 0))],
        out_specs=pl.BlockSpec((th, w), lambda i, s_ref: (i, 0)))
    return pl.pallas_call(body, name=name, grid_spec=spec, out_shape=_sds((h, w), f32),
                          compiler_params=_params(1, True))(chip_arr, own4, recv3)


def _adamw_halves(name, c_arr, w, g_own, g_recv, m, v, th):
    h, wd = g_own.shape
    th = min(th, h)

    def body(c_ref, w_ref, go_ref, gr_ref, m_ref, v_ref, g_ref, d_ref, m2_ref, v2_ref):
        g = jnp.where(c_ref[0] == pl.program_id(0), go_ref[...], gr_ref[...])
        m2 = ADAM_B1 * m_ref[...] + (1.0 - ADAM_B1) * g
        v2 = ADAM_B2 * v_ref[...] + (1.0 - ADAM_B2) * (g * g)
        m_hat = m2 / (1.0 - ADAM_B1 ** ADAM_STEP)
        v_hat = v2 / (1.0 - ADAM_B2 ** ADAM_STEP)
        g_ref[...] = g
        d_ref[...] = -ADAM_LR * (m_hat / (jnp.sqrt(v_hat) + ADAM_EPS) + ADAM_WD * w_ref[...])
        m2_ref[...] = m2
        v2_ref[...] = v2

    nt = h // th
    full = pl.BlockSpec((th, wd), lambda hh, i, c_ref: (hh * nt + i, 0))
    half = pl.BlockSpec((th, wd), lambda hh, i, c_ref: (i, 0))
    spec = pltpu.PrefetchScalarGridSpec(num_scalar_prefetch=1, grid=(2, nt),
                                        in_specs=[full, half, half, full, full], out_specs=[full] * 4)
    return pl.pallas_call(body, name=name, grid_spec=spec, out_shape=[_sds((2 * h, wd), f32)] * 4,
                          compiler_params=_params(2))(c_arr, w, g_own, g_recv, m, v)


def _add2(name, a, b):
    def body(a_ref, b_ref, o_ref):
        o_ref[...] = a_ref[...] + b_ref[...]
    return pl.pallas_call(body, name=name, out_shape=_sds(a.shape, f32))(a, b)


def _sum4(name, b4, th):
    _, h, w = b4.shape
    th = min(th, h)

    def body(b_ref, o_ref):
        o_ref[...] = ((b_ref[0] + b_ref[1]) + b_ref[2]) + b_ref[3]

    return pl.pallas_call(body, name=name, grid=(h // th,),
                          in_specs=[pl.BlockSpec((NCHIP, th, w), lambda i: (0, i, 0))],
                          out_specs=pl.BlockSpec((th, w), lambda i: (i, 0)), out_shape=_sds((h, w), f32),
                          compiler_params=_params(1, True))(b4)


def _coords():
    x, y, c = lax.axis_index("x"), lax.axis_index("y"), lax.axis_index("c")
    return x, y, c, [(1 - x, y), (x, 1 - y), (1 - x, 1 - y)]


def _gather_chips(arrs):
    n = len(arrs)

    def body(*refs):
        ins, outs = refs[:n], refs[n:2 * n]
        send_sems, recv_sems, local_sems = refs[2 * n:2 * n + 3]
        stage = refs[2 * n + 3:]
        x, y, c, chips = _coords()
        s = 2 * x + y
        sib = (x, y, 1 - c)
        load = [pltpu.make_async_copy(ins[a], stage[a], local_sems.at[a]) for a in range(n)]
        local = [pltpu.make_async_copy(stage[a], outs[a].at[s], local_sems.at[n + a]) for a in range(n)]
        for cp in load:
            cp.start()

        def over_ici(k, a, slot, peer):
            return pltpu.make_async_remote_copy(src_ref=ins[a].at[c], dst_ref=outs[a].at[slot, c], send_sem=send_sems.at[k * n + a],
                                                recv_sem=recv_sems.at[k * n + a], device_id=peer, device_id_type=MESH)

        def to_sibling(k, a, slot, half):
            i = (3 + k) * n + a
            return pltpu.make_async_remote_copy(src_ref=outs[a].at[slot, half], dst_ref=outs[a].at[slot, half], send_sem=send_sems.at[i],
                                                recv_sem=recv_sems.at[i], device_id=sib, device_id_type=MESH)

        sends = [over_ici(k, a, s, (px, py, c)) for k, (px, py) in enumerate(chips) for a in range(n)]
        for cp in sends:
            cp.start()
        for a in range(n):
            load[a].wait()
            local[a].start()
        passed = []
        for k, (px, py) in enumerate(chips):
            for a in range(n):
                over_ici(k, a, 2 * px + py, (px, py, c)).wait_recv()
                cp = to_sibling(k, a, 2 * px + py, c)
                cp.start()
                passed.append(cp)
        for k, (px, py) in enumerate(chips):
            for a in range(n):
                to_sibling(k, a, 2 * px + py, 1 - c).wait_recv()
        for cp in sends + passed:
            cp.wait_send()
        for cp in local:
            cp.wait()

    return pl.pallas_call(
        body, name="gather_weights", in_specs=[ANY] * n, out_specs=[ANY] * n,
        out_shape=[_sds((NCHIP,) + a.shape, a.dtype) for a in arrs],
        scratch_shapes=[pltpu.SemaphoreType.DMA((6 * n,)), pltpu.SemaphoreType.DMA((6 * n,)), pltpu.SemaphoreType.DMA((2 * n,))]
        + [pltpu.VMEM(a.shape, a.dtype) for a in arrs],
        compiler_params=pltpu.CompilerParams(vmem_limit_bytes=VMEM_LIMIT),
    )(*arrs)


HBM = pl.BlockSpec(memory_space=pltpu.HBM)
SEM = pl.BlockSpec(memory_space=pltpu.SEMAPHORE)
EFFECT = pltpu.SideEffectType.DATAFLOW_SIDE_EFFECTING


def _split_start(name, n_copies, make_copies, ins, land_shapes, after):
    ni, nl = len(ins), len(land_shapes)

    def body(*refs):
        in_refs, land_refs = refs[:ni], refs[ni:ni + nl]
        send_sems, recv_sems = refs[ni + nl + 1], refs[ni + nl + 2]
        token = refs[-1]
        for cp in make_copies(in_refs, land_refs, send_sems, recv_sems):
            cp.start()
        token[...] = jnp.zeros_like(token)

    lands = [pltpu.with_memory_space_constraint(lax.empty(s.shape, s.dtype), pltpu.HBM) for s in land_shapes]
    res = pl.pallas_call(
        body, name=name,
        out_shape=(pltpu.SemaphoreType.DMA((n_copies,)), pltpu.SemaphoreType.DMA((n_copies,)),
                   *[pltpu.HBM(a.shape, a.dtype) for a in ins], *[pltpu.HBM(s.shape, s.dtype) for s in land_shapes],
                   _sds((8, 128), f32)),
        in_specs=[HBM] * (ni + nl) + [ANY], out_specs=(SEM, SEM, *[HBM] * (ni + nl), pl.BlockSpec(memory_space=pltpu.VMEM)),
        input_output_aliases={i: 2 + i for i in range(ni + nl)},
        compiler_params=pltpu.CompilerParams(has_side_effects=EFFECT),
    )(*[pltpu.with_memory_space_constraint(a, pltpu.HBM) for a in ins], *lands, after)
    return res[0], res[1], list(res[2:2 + ni]), list(res[2 + ni:2 + ni + nl]), res[-1]


def _split_wait(name, make_copies, send_sems, recv_sems, ins, lands, after):
    ni, nl = len(ins), len(lands)

    def body(*refs):
        in_refs, land_refs = refs[:ni], refs[ni:ni + nl]
        s_sems, r_sems = refs[ni + nl], refs[ni + nl + 1]
        for cp in make_copies(in_refs, land_refs, s_sems, r_sems):
            cp.wait_send()
            cp.wait_recv()

    res = pl.pallas_call(
        body, name=name, out_shape=tuple(pltpu.HBM(a.shape, a.dtype) for a in ins + lands),
        in_specs=[HBM] * (ni + nl) + [SEM, SEM, ANY], out_specs=tuple([HBM] * (ni + nl)),
        input_output_aliases={i: i for i in range(ni + nl)},
        compiler_params=pltpu.CompilerParams(has_side_effects=EFFECT),
    )(*ins, *lands, send_sems, recv_sems, after)
    return list(res[:ni]), list(res[ni:])


def _gather_copies(n):
    def make(in_refs, land_refs, send_sems, recv_sems):
        x, y, c, chips = _coords()
        s = 2 * x + y
        return [pltpu.make_async_remote_copy(src_ref=in_refs[a], dst_ref=land_refs[a].at[s], send_sem=send_sems.at[k * n + a],
                                             recv_sem=recv_sems.at[k * n + a], device_id=(px, py, c), device_id_type=MESH)
                for k, (px, py) in enumerate(chips) for a in range(n)]
    return make


def _sibling_half_copies(n):
    def make(in_refs, land_refs, send_sems, recv_sems):
        x, y, c, _ = _coords()
        return [pltpu.make_async_remote_copy(src_ref=in_refs[a].at[:, 1 - c], dst_ref=land_refs[a], send_sem=send_sems.at[a],
                                             recv_sem=recv_sems.at[a], device_id=(x, y, 1 - c), device_id_type=MESH)
                for a in range(n)]
    return make


def _chip_part_copies(n):
    def make(in_refs, land_refs, send_sems, recv_sems):
        x, y, c, chips = _coords()
        return [pltpu.make_async_remote_copy(src_ref=in_refs[a].at[2 * px + py], dst_ref=land_refs[a].at[k],
                                             send_sem=send_sems.at[k * n + a], recv_sem=recv_sems.at[k * n + a],
                                             device_id=(px, py, c), device_id_type=MESH)
                for k, (px, py) in enumerate(chips) for a in range(n)]
    return make


def _sibling_whole_copies(n):
    def make(in_refs, land_refs, send_sems, recv_sems):
        x, y, c, _ = _coords()
        return [pltpu.make_async_remote_copy(src_ref=in_refs[a], dst_ref=land_refs[a], send_sem=send_sems.at[a],
                                             recv_sem=recv_sems.at[a], device_id=(x, y, 1 - c), device_id_type=MESH)
                for a in range(n)]
    return make


def _place_own(chip_arr, owns, lands, steps):
    n = len(owns)

    def body(s_ref, *refs):
        for a in range(n):
            refs[2 * n + a][...] = refs[a][...]

    tiles = [o.shape[0] // steps for o in owns]
    spec = pltpu.PrefetchScalarGridSpec(
        num_scalar_prefetch=1, grid=(steps,),
        in_specs=[pl.BlockSpec((t, o.shape[1]), lambda i, s_ref: (i, 0)) for t, o in zip(tiles, owns)] + [ANY] * n,
        out_specs=[pl.BlockSpec((None, t, o.shape[1]), lambda i, s_ref: (s_ref[0], i, 0)) for t, o in zip(tiles, owns)])
    return pl.pallas_call(body, name="place_own", grid_spec=spec, out_shape=[_sds(l.shape, l.dtype) for l in lands],
                          input_output_aliases={1 + n + a: a for a in range(n)},
                          compiler_params=_params(1))(chip_arr, *owns, *lands)


def _sibling_halves(g4s, small):
    n = len(g4s)

    def body(*refs):
        ins, small_ref = refs[:n], refs[n]
        outs, small_out = refs[n + 1:2 * n + 1], refs[2 * n + 1]
        send_sems, recv_sems = refs[2 * n + 2:]
        x, y, c, _ = _coords()
        sib = (x, y, 1 - c)

        def remote(a, half):
            src = small_ref if a == n else ins[a].at[:, half]
            dst = small_out if a == n else outs[a]
            return pltpu.make_async_remote_copy(src_ref=src, dst_ref=dst, send_sem=send_sems.at[a], recv_sem=recv_sems.at[a],
                                                device_id=sib, device_id_type=MESH)

        sends = [remote(a, 1 - c) for a in range(n + 1)]
        for cp in sends:
            cp.start()
        for a in range(n + 1):
            remote(a, c).wait_recv()
        for cp in sends:
            cp.wait_send()

    return pl.pallas_call(
        body, name="reduce_sibling", in_specs=[ANY] * (n + 1), out_specs=[ANY] * (n + 1),
        out_shape=[_sds((g.shape[0],) + g.shape[2:], f32) for g in g4s] + [_sds(small.shape, f32)],
        scratch_shapes=[pltpu.SemaphoreType.DMA((n + 1,)), pltpu.SemaphoreType.DMA((n + 1,))],
    )(*g4s, small)


def _exchange_chips(parts, small2):
    n = len(parts)

    def body(*refs):
        ins, small_ref = refs[:n], refs[n]
        outs, small_out = refs[n + 1:2 * n + 1], refs[2 * n + 1]
        send_sems, recv_sems, local_sem = refs[2 * n + 2:]
        x, y, c, chips = _coords()
        s = 2 * x + y
        local = pltpu.make_async_copy(small_ref.at[c], small_out.at[s], local_sem)
        local.start()

        def remote(k, a, dest_chip, small_slot, peer):
            if a == n:
                src, dst = small_ref.at[c], small_out.at[small_slot]
            else:
                src, dst = ins[a].at[dest_chip], outs[a].at[k]
            i = k * (n + 1) + a
            return pltpu.make_async_remote_copy(src_ref=src, dst_ref=dst, send_sem=send_sems.at[i], recv_sem=recv_sems.at[i],
                                                device_id=peer, device_id_type=MESH)

        sends = [remote(k, a, 2 * px + py, s, (px, py, c)) for k, (px, py) in enumerate(chips) for a in range(n + 1)]
        for cp in sends:
            cp.start()
        for k, (px, py) in enumerate(chips):
            for a in range(n + 1):
                remote(k, a, s, 2 * px + py, (px, py, c)).wait_recv()
        for cp in sends:
            cp.wait_send()
        local.wait()

    m = 3 * (n + 1)
    return pl.pallas_call(
        body, name="reduce_chips", in_specs=[ANY] * (n + 1), out_specs=[ANY] * (n + 1),
        out_shape=[_sds((3,) + p.shape[1:], p.dtype) for p in parts] + [_sds((NCHIP,) + small2.shape[1:], f32)],
        scratch_shapes=[pltpu.SemaphoreType.DMA((m,)), pltpu.SemaphoreType.DMA((m,)), pltpu.SemaphoreType.DMA],
    )(*parts, small2)


def _share_sibling(halves):
    n = len(halves)

    def body(*refs):
        ins, outs = refs[:n], refs[n:2 * n]
        send_sems, recv_sems = refs[2 * n:]
        x, y, c, _ = _coords()
        sib = (x, y, 1 - c)
        sends = [pltpu.make_async_remote_copy(src_ref=ins[a], dst_ref=outs[a], send_sem=send_sems.at[a], recv_sem=recv_sems.at[a],
                                              device_id=sib, device_id_type=MESH) for a in range(n)]
        for cp in sends:
            cp.start()
        for cp in sends:
            cp.wait()

    return pl.pallas_call(
        body, name="reduce_share", in_specs=[ANY] * n, out_specs=[ANY] * n,
        out_shape=[_sds(h.shape, f32) for h in halves],
        scratch_shapes=[pltpu.SemaphoreType.DMA((n,)), pltpu.SemaphoreType.DMA((n,))],
    )(*halves)


def _block_diag_pairs(w):
    w = w.reshape(NCH, 2, HD, HD)
    z = jnp.zeros((NCH, HD, HD), w.dtype)
    return jnp.concatenate([jnp.concatenate([w[:, 0], z], axis=2), jnp.concatenate([z, w[:, 1]], axis=2)], axis=1)


def _diag_blocks(m):
    return jnp.stack([m[:, :HD, :HD], m[:, HD:, HD:]], axis=1).reshape(NH, HD, HD)


def _pack(vs, rows):
    flat = jnp.concatenate([v.reshape(-1) for v in vs])
    return jnp.pad(flat, (0, rows * 128 - flat.shape[0])).reshape(rows, 128)


def _unpack(packed, shapes):
    flat = packed.reshape(-1)
    out, off = [], 0
    for shp in shapes:
        size = math.prod(shp)
        out.append(flat[off:off + size].reshape(shp))
        off += size
    return out


def _rows_for(sizes, multiple):
    rows = -(-sum(sizes) // 128)
    return -(-rows // multiple) * multiple


def kernel(x, norm_mix_g, w_in, b_gate, conv_w, conv_b, lru_lambda, lru_wa, lru_ba, lru_wx, lru_bx, attn_sink, w_out, norm_ffn_g, w_ffn_in, w_ffn_out, norm_final_g, loss_target, m_norm_mix_g, m_w_in, m_b_gate, m_conv_w, m_conv_b, m_lru_lambda, m_lru_wa, m_lru_ba, m_lru_wx, m_lru_bx, m_attn_sink, m_w_out, m_norm_ffn_g, m_w_ffn_in, m_w_ffn_out, m_norm_final_g, v_norm_mix_g, v_w_in, v_b_gate, v_conv_w, v_conv_b, v_lru_lambda, v_lru_wa, v_lru_ba, v_lru_wx, v_lru_bx, v_attn_sink, v_w_out, v_norm_ffn_g, v_w_ffn_in, v_w_ffn_out, v_norm_final_g):
    S = x.shape[1]
    xs = x[0]
    tgt = loss_target[0]
    cx, cy, cc = lax.axis_index("x"), lax.axis_index("y"), lax.axis_index("c")
    chip = 2 * cx + cy
    SW = D // NCHIP

    small_shard = _pack([conv_w[0], lru_lambda[0], lru_ba[0], lru_bx[0]], 32)
    halves_of = lambda a: a.reshape(2, a.shape[0] // 2, a.shape[1])
    w_in_g, small_g = _gather_chips([halves_of(w_in[0].astype(bf16)), halves_of(small_shard)])
    w_in_g = w_in_g.reshape(NCHIP, D, SHW)
    small_g = small_g.reshape(NCHIP, 32, 128)
    late = [w_ffn_in[0].astype(bf16), w_out[0].astype(bf16), w_ffn_out[0].astype(bf16)]
    late_send, late_recv, late_src, late_land, late_token = _split_start(
        "gather_late_start", 9, _gather_copies(3), late, [_sds((NCHIP,) + a.shape, bf16) for a in late], small_g)
    small_parts = [_unpack(small_g[s], [(4, SW), (2, SW), (2, SW), (2, SW)]) for s in range(NCHIP)]
    conv_w_f, lam_f, ba_f, bx_f = [jnp.concatenate([small_parts[s][p] for s in range(NCHIP)], axis=1) for p in range(4)]
    wbd = jnp.concatenate([_block_diag_pairs(lru_wa[0, 0]), _block_diag_pairs(lru_wx[0, 0]),
                           _block_diag_pairs(lru_wa[0, 1]), _block_diag_pairs(lru_wx[0, 1])], axis=2).astype(bf16)
    conv_b_f = conv_b
    sink = attn_sink

    xn, proj = _rms_matmul("rms_proj", xs, norm_mix_g + late_token[0:1, 0:1], w_in_g, 1024)
    y_a, lru_state = _lru_fwd(proj, conv_w_f, conv_b_f, lam_f, ba_f, bx_f, wbd)
    y_b = _attn_fwd(proj, sink)
    merged = _merge_fwd(proj, b_gate, y_a, y_b, 1024)
    late_src, late_land = _split_wait("gather_late_wait", _gather_copies(3), late_send, late_recv, late_src, late_land, merged)
    chip_arr = chip.reshape(1).astype(jnp.int32)
    w_ffn_in_g, w_out_g, w_ffn_out_g = _place_own(chip_arr, late_src, late_land, 4)
    w_out_f = w_out_g.reshape(D, D)
    w_ffn_out_f = w_ffn_out_g.reshape(FF, D)
    x1 = _mm_residual("out_proj", merged, w_out_f, xs, 512)
    xn2, gu, act = _rms_matmul_swiglu("rms_ffn_in", x1, norm_ffn_g, w_ffn_in_g, 1024)
    dx2, loss_row, dg3 = _ffn_out_loss_bwd(act, w_ffn_out_f, x1, norm_final_g.reshape(1, D), tgt, 512)

    tm = min(1024, S)
    tk = min(2048, S)
    gw_ffn_out = _mm_tn("dw_ffn_out", act, pl.BlockSpec((tk, SHW), lambda i, k: (k, i)),
                        dx2, pl.BlockSpec((tk, D), lambda i, k: (k, 0)),
                        _sds((FF, D), f32), pl.BlockSpec((SHW, D), lambda i, k: (i, 0)), (2, S // tk), (SHW, D))
    dgu = _swiglu_bwd(dx2, w_ffn_out_f, gu, 256)
    gw_ffn_in = _mm_tn("dw_ffn_in", xn2, pl.BlockSpec((tk, D), lambda g, k: (k, 0)),
                       dgu, pl.BlockSpec((None, tk, SHW), lambda g, k: (g // 2, k, g % 2)),
                       _sds((NCHIP, D, SHW), f32), pl.BlockSpec((None, D, SHW), lambda g, k: (g, 0, 0)),
                       (NCHIP, S // tk), (D, SHW))
    c_arr = cc.reshape(1).astype(jnp.int32)
    early_names, early_tiles = ["w_ffn_in", "w_ffn_out"], [256, 352]
    early = [gw_ffn_in.reshape(NCHIP, 2, D // 2, SHW), gw_ffn_out.reshape(NCHIP, 2, FF // NCHIP // 2, D)]
    ea_send, ea_recv, ea_src, ea_land, ea_token = _split_start(
        "reduce_early_sibling_start", 2, _sibling_half_copies(2), early,
        [_sds((NCHIP,) + g.shape[2:], f32) for g in early], dgu)
    dx1, dg2 = _mm_nt_rms_bwd("dxn2_rms_bwd", dgu, pl.BlockSpec((None, tm, SHW), lambda i, g: (g // 2, i, g % 2)), w_ffn_in_g,
                              x1, norm_ffn_g + ea_token[0:1, 0:1], dx2, tm)

    dmerged = _mm_nt_resident("d_merged", dx1, w_out_f, 512)
    gw_out = _mm_tn("dw_out", merged, pl.BlockSpec((tk, D), lambda i, k: (k, 0)),
                    dx1, pl.BlockSpec((tk, D), lambda i, k: (k, 0)),
                    _sds((D, D), f32), pl.BlockSpec((D, D), lambda i, k: (0, 0)), (1, S // tk), (D, D))
    dproj, dy, db_gate = _merge_bwd(proj, b_gate, y_a, y_b, dmerged, 1024)
    ea_src, ea_land = _split_wait("reduce_early_sibling_wait", _sibling_half_copies(2), ea_send, ea_recv, ea_src, ea_land, dy)
    early_pairs = [_pair_sum("pair_sum_" + nm, c_arr, g4, r, th)
                   for nm, g4, r, th in zip(early_names, ea_src, ea_land, early_tiles)]
    eb_send, eb_recv, eb_src, eb_land, eb_token = _split_start(
        "reduce_early_chips_start", 6, _chip_part_copies(2), [p[1] for p in early_pairs],
        [_sds((3,) + p[1].shape[1:], bf16) for p in early_pairs], early_pairs[0][0])
    dproj, dsink = _attn_bwd(proj, sink + eb_token[0:1, 0:1], y_b, dy, dproj)
    _, eb_land = _split_wait("reduce_early_chips_wait", _chip_part_copies(2), eb_send, eb_recv, eb_src, eb_land, dsink)
    early_halves = [_chip_sum("chip_sum_" + nm, chip_arr, p[0], r3, th)
                    for nm, p, r3, th in zip(early_names, early_pairs, eb_land, early_tiles)]
    ec_send, ec_recv, ec_src, ec_land, ec_token = _split_start(
        "reduce_early_share_start", 2, _sibling_whole_copies(2), early_halves, [_sds(h.shape, f32) for h in early_halves], dsink)
    dproj, dcw, dcb, dlam, dba, dbx, dwbd = _lru_bwd(proj, dy, lru_state, dproj, conv_w_f, conv_b_f + ec_token[0:1, 0:1], lam_f,
                                                     ba_f, bx_f, wbd)
    early_halves, early_other = _split_wait("reduce_early_share_wait", _sibling_whole_copies(2), ec_send, ec_recv, ec_src, ec_land, dcb)
    gw_in = _mm_tn("dw_in", xn, pl.BlockSpec((tk, D), lambda g, k: (k, 0)),
                   dproj, pl.BlockSpec((tk, SHW), lambda g, k: (k, g)),
                   _sds((NCHIP, D, SHW), f32), pl.BlockSpec((None, D, SHW), lambda g, k: (g, 0, 0)),
                   (NCHIP, S // tk), (D, SHW))
    wa_send, wa_recv, wa_src, wa_land, wa_token = _split_start(
        "reduce_w_in_sibling_start", 1, _sibling_half_copies(1), [gw_in.reshape(NCHIP, 2, D // 2, SHW)],
        [_sds((NCHIP, D // 2, SHW), f32)], dproj)
    dxn =_mm_nt_groups("dxn", dproj, pl.BlockSpec((tm, SHW), lambda i, g: (i, g)), w_in_g, S, tm)
    wa_src, wa_land = _split_wait("reduce_w_in_sibling_wait", _sibling_half_copies(1), wa_send, wa_recv, wa_src, wa_land, dxn)
    w_in_pair = _pair_sum("pair_sum_w_in", c_arr, wa_src[0], wa_land[0], 256)
    wb_send, wb_recv, wb_src, wb_land, wb_token = _split_start(
        "reduce_w_in_chips_start", 3, _chip_part_copies(1), [w_in_pair[1]], [_sds((3, D // 2, SHW), bf16)], w_in_pair[0])
    grad_x, dg1 = _rms_bwd("rms_mix_bwd", xs, norm_mix_g + wb_token[0:1, 0:1], dxn, dx1, 512)
    _, wb_land = _split_wait("reduce_w_in_chips_wait", _chip_part_copies(1), wb_send, wb_recv, wb_src, wb_land, dg1)
    w_in_half = _chip_sum("chip_sum_w_in", chip_arr, w_in_pair[0], wb_land[0], 256)

    d_wa = jnp.stack([_diag_blocks(dwbd[:, :, 0:CW]), _diag_blocks(dwbd[:, :, 2 * CW:3 * CW])])
    d_wx = jnp.stack([_diag_blocks(dwbd[:, :, CW:2 * CW]), _diag_blocks(dwbd[:, :, 3 * CW:4 * CW])])
    small_full = [dg1, db_gate, dcw, dcb, dlam, d_wa, dba, d_wx, dbx, dsink[:, 0], dg2, dg3,
                  loss_row[0, 0:1]]
    full_shapes = [(1, D), (1, 2 * D), (4, D), (1, D), (2, D), (2, NH, HD, HD), (2, D), (2, NH, HD, HD), (2, D), (NH,),
                   (1, D), (1, D), (1,)]
    rows_full = _rows_for([math.prod(s) for s in full_shapes], 16)
    small_vec = _pack(small_full, rows_full)

    late_names, late_tiles = ["w_in", "w_out"], [256, 128]
    big = [gw_out.reshape(NCHIP, 2, D // NCHIP // 2, D)]
    *recv_a, small_sib = _sibling_halves(big, small_vec)
    w_out_pair = _pair_sum("pair_sum_w_out", c_arr, big[0], recv_a[0], 128)
    small_chip = _add2("pair_sum_small", small_vec, small_sib).reshape(2, rows_full // 2, 128)
    *recv_b, small_all = _exchange_chips([w_out_pair[1]], small_chip)
    w_out_half = _chip_sum("chip_sum_w_out", chip_arr, w_out_pair[0], recv_b[0], 128)
    halves = [w_in_half, w_out_half, _sum4("chip_sum_small", small_all, rows_full // 2)]
    *recv_c, small_other = _share_sibling(halves)
    small_lo = jnp.where(cc == 0, halves[2], small_other)
    small_hi = jnp.where(cc == 0, small_other, halves[2])
    g_full = _unpack(jnp.concatenate([small_lo, small_hi], axis=0), full_shapes)

    out_big = {}
    for nm, w, g_own, g_recv, m, v, th in zip(late_names + early_names, [w_in, w_out, w_ffn_in, w_ffn_out],
                                              halves[:2] + early_halves, recv_c + early_other,
                                              [m_w_in, m_w_out, m_w_ffn_in, m_w_ffn_out],
                                              [v_w_in, v_w_out, v_w_ffn_in, v_w_ffn_out], late_tiles + early_tiles):
        g_, d_, m_, v_ = _adamw_halves("adamw_" + nm, c_arr, w[0], g_own, g_recv, m[0], v[0], th)
        out_big[nm] = (g_[None], d_[None], m_[None], v_[None])

    small_names = ["norm_mix_g", "b_gate", "conv_w", "conv_b", "lru_lambda", "lru_wa", "lru_ba", "lru_wx", "lru_bx", "attn_sink",
                   "norm_ffn_g", "norm_final_g"]
    sharded = {"conv_w", "lru_lambda", "lru_ba", "lru_bx"}
    small_w = [norm_mix_g, b_gate, conv_w, conv_b, lru_lambda, lru_wa, lru_ba, lru_wx, lru_bx, attn_sink, norm_ffn_g, norm_final_g]
    small_m = [m_norm_mix_g, m_b_gate, m_conv_w, m_conv_b, m_lru_lambda, m_lru_wa, m_lru_ba, m_lru_wx, m_lru_bx, m_attn_sink,
               m_norm_ffn_g, m_norm_final_g]
    small_v = [v_norm_mix_g, v_b_gate, v_conv_w, v_conv_b, v_lru_lambda, v_lru_wa, v_lru_ba, v_lru_wx, v_lru_bx, v_attn_sink,
               v_norm_ffn_g, v_norm_final_g]
    g_local = []
    for nm, g, w in zip(small_names, g_full, small_w):
        if nm in sharded:
            g = lax.dynamic_slice_in_dim(g, chip * SW, SW, axis=1)
        g_local.append(g.reshape(w.shape))
    local_shapes = [w.shape for w in small_w]
    rows_local = _rows_for([math.prod(s) for s in local_shapes], 8)
    d_s, m_s, v_s = _adamw("adamw_small", _pack(small_w, rows_local), _pack(g_local, rows_local),
                           _pack(small_m, rows_local), _pack(small_v, rows_local), rows_local)
    d_l, m_l, v_l = _unpack(d_s, local_shapes), _unpack(m_s, local_shapes), _unpack(v_s, local_shapes)
    res = {nm: (g_local[i], d_l[i], m_l[i], v_l[i]) for i, nm in enumerate(small_names)}
    res.update(out_big)

    order = ["norm_mix_g", "w_in", "b_gate", "conv_w", "conv_b", "lru_lambda", "lru_wa", "lru_ba", "lru_wx", "lru_bx", "attn_sink",
             "w_out", "norm_ffn_g", "w_ffn_in", "w_ffn_out", "norm_final_g"]
    outs = [g_full[-1][0], grad_x[None]]
    for k in range(4):
        outs += [res[nm][k] for nm in order]
    return tuple(outs)
```

```python
import functools
import math

import jax
import jax.numpy as jnp
from jax import lax
from jax.experimental import pallas as pl
from jax.experimental.pallas import tpu as pltpu

f32 = jnp.float32
bf16 = jnp.bfloat16

D = 1024
NH = 16
HD = 64
FF = 2816
INW = 5632
NCHIP = 4
SHW = INW // NCHIP
CW = 128
NCH = D // CW
BLK = 128
EPS = 1e-6
NEG_INF = -1e30
RGLRU_C = 8.0
ADAM_LR, ADAM_B1, ADAM_B2, ADAM_EPS, ADAM_WD, ADAM_STEP = 0.001, 0.9, 0.999, 1e-08, 0.01, 10
VMEM_LIMIT = 58 * 1024 * 1024
MESH = pl.DeviceIdType.MESH
ANY = pl.BlockSpec(memory_space=pl.ANY)

COL_U, COL_G, COL_Q, COL_K, COL_V, COL_Z0, COL_Z1 = 0, 4, 8, 12, 13, 14, 18
MERGE_W = 512
MERGE_Z0, MERGE_Z1 = (COL_Z0 * 256) // MERGE_W, (COL_Z1 * 256) // MERGE_W


def _params(n_axes, vmem=False):
    return pltpu.CompilerParams(dimension_semantics=("arbitrary",) * n_axes,
                                vmem_limit_bytes=VMEM_LIMIT if vmem else None)


def _sds(shape, dtype):
    return jax.ShapeDtypeStruct(tuple(shape), dtype)


_DIMS = {"nn": (((1,), (0,)), ((), ())), "nt": (((1,), (1,)), ((), ())), "tn": (((0,), (0,)), ((), ()))}


def _mm(name, mode, a, a_spec, b, b_spec, out_shape, out_spec, grid, nk, acc_shape):
    def body(*refs):
        a_ref, b_ref, o_ref = refs[0], refs[1], refs[2]
        part = lax.dot_general(a_ref[...].astype(bf16), b_ref[...].astype(bf16), _DIMS[mode],
                               preferred_element_type=f32)
        if nk == 1:
            o_ref[...] = part.astype(o_ref.dtype)
            return
        acc_ref = refs[3]
        k = pl.program_id(len(grid) - 1)

        @pl.when(k == 0)
        def _():
            acc_ref[...] = part

        @pl.when(k > 0)
        def _():
            acc_ref[...] += part

        @pl.when(k == nk - 1)
        def _():
            o_ref[...] = acc_ref[...].astype(o_ref.dtype)

    scratch = [pltpu.VMEM(acc_shape, f32)] if nk > 1 else []
    return pl.pallas_call(body, name=name, grid=grid, in_specs=[a_spec, b_spec], out_specs=out_spec, out_shape=out_shape,
                          scratch_shapes=scratch, compiler_params=_params(len(grid), True))(a, b)


def _rms_matmul(name, x, g, w3, tm):
    S, K = x.shape
    G, _, Nw = w3.shape
    tm = min(tm, S)

    def body(x_ref, g_ref, w_ref, xn_ref, o_ref, xs_ref):
        @pl.when(pl.program_id(1) == 0)
        def _():
            xf = x_ref[...]
            r = lax.rsqrt(jnp.mean(xf * xf, axis=-1, keepdims=True) + EPS)
            xn = ((xf * r) * g_ref[...]).astype(bf16)
            xs_ref[...] = xn
            xn_ref[...] = xn

        o_ref[...] = jnp.dot(xs_ref[...], w_ref[...], preferred_element_type=f32).astype(bf16)

    return pl.pallas_call(
        body, name=name, grid=(S // tm, G),
        in_specs=[pl.BlockSpec((tm, K), lambda i, j: (i, 0)), pl.BlockSpec((1, K), lambda i, j: (0, 0)),
                  pl.BlockSpec((None, K, Nw), lambda i, j: (j, 0, 0))],
        out_specs=[pl.BlockSpec((tm, K), lambda i, j: (i, 0)), pl.BlockSpec((tm, Nw), lambda i, j: (i, j))],
        out_shape=[_sds((S, K), bf16), _sds((S, G * Nw), bf16)],
        scratch_shapes=[pltpu.VMEM((tm, K), bf16)], compiler_params=_params(2, True))(x, g, w3)


def _rms_matmul_swiglu(name, x, g, w3, tm):
    S, K = x.shape
    G, _, Nw = w3.shape
    tm = min(tm, S)
    half = G // 2

    def body(x_ref, g_ref, wg_ref, wu_ref, xn_ref, gu_ref, act_ref, xs_ref):
        @pl.when(pl.program_id(1) == 0)
        def _():
            xf = x_ref[...]
            r = lax.rsqrt(jnp.mean(xf * xf, axis=-1, keepdims=True) + EPS)
            xn = ((xf * r) * g_ref[...]).astype(bf16)
            xs_ref[...] = xn
            xn_ref[...] = xn

        xn = xs_ref[...]
        gate = jnp.dot(xn, wg_ref[...], preferred_element_type=f32)
        up = jnp.dot(xn, wu_ref[...], preferred_element_type=f32)
        gu_ref[0] = gate.astype(bf16)
        gu_ref[1] = up.astype(bf16)
        act_ref[...] = ((gate * _sigmoid(gate)) * up).astype(bf16)

    return pl.pallas_call(
        body, name=name, grid=(S // tm, half),
        in_specs=[pl.BlockSpec((tm, K), lambda i, j: (i, 0)), pl.BlockSpec((1, K), lambda i, j: (0, 0)),
                  pl.BlockSpec((None, K, Nw), lambda i, j: (j, 0, 0)),
                  pl.BlockSpec((None, K, Nw), lambda i, j: (half + j, 0, 0))],
        out_specs=[pl.BlockSpec((tm, K), lambda i, j: (i, 0)), pl.BlockSpec((2, tm, Nw), lambda i, j: (0, i, j)),
                   pl.BlockSpec((tm, Nw), lambda i, j: (i, j))],
        out_shape=[_sds((S, K), bf16), _sds((2, S, half * Nw), bf16), _sds((S, half * Nw), bf16)],
        scratch_shapes=[pltpu.VMEM((tm, K), bf16)], compiler_params=_params(2, True))(x, g, w3, w3)


def _mm_nt_groups(name, a, a_spec, w3, S, tm):
    G, Dout, Kw = w3.shape
    return _mm(name, "nt", a, a_spec, w3, pl.BlockSpec((None, Dout, Kw), lambda i, g: (g, 0, 0)),
               _sds((S, Dout), f32), pl.BlockSpec((tm, Dout), lambda i, g: (i, 0)), (S // tm, G), G, (tm, Dout))


def _mm_tn(name, a, a_spec, b, b_spec, out_shape, out_spec, grid, acc_shape):
    return _mm(name, "tn", a, a_spec, b, b_spec, out_shape, out_spec, grid, grid[-1], acc_shape)


def _sigmoid(x):
    return 0.5 * jnp.tanh(0.5 * x) + 0.5


_GELU_C = math.sqrt(2.0 / math.pi)


def _gelu_and_grad(x):
    v = _GELU_C * (x + 0.044715 * (x * x * x))
    t = jnp.tanh(v)
    gl = 0.5 * x * (1.0 + t)
    dgl = 0.5 * (1.0 + t) + 0.5 * x * (1.0 - t * t) * (_GELU_C * (1.0 + 3.0 * 0.044715 * (x * x)))
    return gl, dgl


def _one_minus_exp2x(x, ex):
    y = 2.0 * x
    series = y * (1.0 + y * (0.5 + y * (1.0 / 6.0 + y * (1.0 / 24.0))))
    return jnp.where(y > -1.0 / 64.0, -series, 1.0 - ex * ex)


def _z_specs(tm):
    return [pl.BlockSpec((tm, MERGE_W), lambda i, p=p: (i, MERGE_Z0 + p)) for p in range(2 * D // MERGE_W)]


def _merge_out_proj(proj, b_gate, y_a, y_b, w, res, tm):
    S = proj.shape[0]
    tm = min(tm, S)
    per = D // MERGE_W
    nz = 2 * per

    def body(*refs):
        z = refs[:nz]
        b_ref, ya_ref, yb_ref, w_ref, r_ref, m_ref, x_ref = refs[nz:]
        for p in range(per):
            cols = slice(p * MERGE_W, (p + 1) * MERGE_W)
            g0 = _sigmoid(z[p][...].astype(f32) + b_ref[:, p * MERGE_W:(p + 1) * MERGE_W])
            g1 = _sigmoid(z[per + p][...].astype(f32) + b_ref[:, D + p * MERGE_W:D + (p + 1) * MERGE_W])
            m_ref[:, cols] = (g0 * ya_ref[:, cols].astype(f32) + g1 * yb_ref[:, cols].astype(f32)).astype(bf16)
        x_ref[...] = r_ref[...] + jnp.dot(m_ref[...], w_ref[...], preferred_element_type=f32)

    row = pl.BlockSpec((tm, D), lambda i: (i, 0))
    return pl.pallas_call(
        body, name="merge_out_proj", grid=(S // tm,),
        in_specs=_z_specs(tm) + [pl.BlockSpec((1, 2 * D), lambda i: (0, 0)), row, row, pl.BlockSpec((D, D), lambda i: (0, 0)), row],
        out_specs=[row, row], out_shape=[_sds((S, D), bf16), _sds((S, D), f32)],
        compiler_params=_params(1, True))(*([proj] * nz), b_gate, y_a, y_b, w, res)


def _merge_bwd(proj, b_gate, y_a, y_b, dx, w, tm):
    S = proj.shape[0]
    tm = min(tm, S)
    per = D // MERGE_W
    nz = 2 * per
    nsteps = S // tm
    z_col = MERGE_Z0 * MERGE_W

    def body(*refs):
        z = refs[:nz]
        b_ref, ya_ref, yb_ref, dx_ref, w_ref, dproj_ref, dy_ref, db_ref, dz_buf, sems = refs[nz:]
        i = pl.program_id(0)
        slot = i % 2

        def dz_copy(step):
            rows = pl.ds(pl.multiple_of(step * tm, tm), tm)
            return pltpu.make_async_copy(dz_buf.at[step % 2], dproj_ref.at[rows, pl.ds(z_col, 2 * D)], sems.at[step % 2])

        @pl.when(i >= 2)
        def _():
            dz_copy(i - 2).wait()

        @pl.when(i == 0)
        def _():
            db_ref[...] = jnp.zeros_like(db_ref)

        dm = lax.dot_general(dx_ref[...].astype(bf16), w_ref[...], _DIMS["nt"], preferred_element_type=f32)
        for p in range(nz):
            branch, cols = p // per, slice((p % per) * MERGE_W, (p % per + 1) * MERGE_W)
            zc = slice(p * MERGE_W, (p + 1) * MERGE_W)
            g = _sigmoid(z[p][...].astype(f32) + b_ref[:, zc])
            d = dm[:, cols]
            y = (ya_ref if branch == 0 else yb_ref)[:, cols].astype(f32)
            dz = (d * y) * (g * (1.0 - g))
            dz_buf[slot, :, zc] = dz.astype(bf16)
            dy_ref[branch, :, cols] = (d * g).astype(bf16)
            db_ref[:, zc] += jnp.sum(dz, axis=0, keepdims=True)
        dz_copy(i).start()

        @pl.when(i == nsteps - 1)
        def _():
            if nsteps >= 2:
                dz_copy(i - 1).wait()
            dz_copy(i).wait()

    row = pl.BlockSpec((tm, D), lambda i: (i, 0))
    return pl.pallas_call(
        body, name="merge_bwd", grid=(nsteps,),
        in_specs=_z_specs(tm) + [pl.BlockSpec((1, 2 * D), lambda i: (0, 0)), row, row, row, pl.BlockSpec((D, D), lambda i: (0, 0))],
        out_specs=[ANY, pl.BlockSpec((2, tm, D), lambda i: (0, i, 0)), pl.BlockSpec((1, 2 * D), lambda i: (0, 0))],
        out_shape=[_sds((S, INW), bf16), _sds((2, S, D), bf16), _sds((1, 2 * D), f32)],
        scratch_shapes=[pltpu.VMEM((2, tm, 2 * D), bf16), pltpu.SemaphoreType.DMA((2,))],
        compiler_params=_params(1, True))(*([proj] * nz), b_gate, y_a, y_b, dx, w)


def _swiglu_bwd(dx, w, gu, tm):
    S, K = dx.shape
    tm = min(tm, S)

    def body(dx_ref, w_ref, gu_ref, o_ref):
        d = lax.dot_general(dx_ref[...].astype(bf16), w_ref[...], _DIMS["nt"], preferred_element_type=f32)
        g = gu_ref[0].astype(f32)
        u = gu_ref[1].astype(f32)
        s = _sigmoid(g)
        o_ref[0] = ((d * u) * (s * (1.0 + g * (1.0 - s)))).astype(bf16)
        o_ref[1] = (d * (g * s)).astype(bf16)

    stacked = pl.BlockSpec((2, tm, FF), lambda i: (0, i, 0))
    return pl.pallas_call(body, name="swiglu_bwd", grid=(S // tm,),
                          in_specs=[pl.BlockSpec((tm, K), lambda i: (i, 0)), pl.BlockSpec((FF, K), lambda i: (0, 0)), stacked],
                          out_specs=stacked, out_shape=_sds((2, S, FF), bf16),
                          compiler_params=_params(1, True))(dx, w, gu)


def _ffn_out_loss_bwd(act, w, x1, g3, tgt, tm):
    S, K = act.shape
    tm = min(tm, S)

    def body(a_ref, w_ref, r_ref, g_ref, t_ref, dx_ref, loss_ref, dg_ref):
        @pl.when(pl.program_id(0) == 0)
        def _():
            loss_ref[...] = jnp.zeros_like(loss_ref)
            dg_ref[...] = jnp.zeros_like(dg_ref)

        x = r_ref[...] + jnp.dot(a_ref[...], w_ref[...], preferred_element_type=f32)
        g = g_ref[...]
        r = lax.rsqrt(jnp.mean(x * x, axis=-1, keepdims=True) + EPS)
        xh = x * r
        err = xh * g - t_ref[...]
        row = jnp.mean(err * err, axis=-1, keepdims=True)
        loss_ref[...] += 0.5 * jnp.sum(row, axis=0, keepdims=True)
        dy = err * (1.0 / D)
        dg_ref[...] += jnp.sum(dy * xh, axis=0, keepdims=True)
        dxh = dy * g
        dx_ref[...] = r * (dxh - xh * jnp.mean(dxh * xh, axis=-1, keepdims=True))

    row_blk = pl.BlockSpec((tm, D), lambda i: (i, 0))
    vec = pl.BlockSpec((1, D), lambda i: (0, 0))
    return pl.pallas_call(body, name="ffn_out_loss_bwd", grid=(S // tm,),
                          in_specs=[pl.BlockSpec((tm, K), lambda i: (i, 0)), pl.BlockSpec((K, D), lambda i: (0, 0)),
                                    row_blk, vec, row_blk],
                          out_specs=[row_blk, pl.BlockSpec((1, 128), lambda i: (0, 0)), vec],
                          out_shape=[_sds((S, D), f32), _sds((1, 128), f32), _sds((1, D), f32)],
                          compiler_params=_params(1, True))(act, w, x1, g3, tgt)


def _rms_bwd(name, x, g, dxn, dres, tm):
    S = x.shape[0]
    tm = min(tm, S)

    def body(x_ref, g_ref, d_ref, r_ref, dx_ref, dg_ref):
        @pl.when(pl.program_id(0) == 0)
        def _():
            dg_ref[...] = jnp.zeros_like(dg_ref)

        x = x_ref[...]
        d = d_ref[...]
        r = lax.rsqrt(jnp.mean(x * x, axis=-1, keepdims=True) + EPS)
        xh = x * r
        dg_ref[...] += jnp.sum(d * xh, axis=0, keepdims=True)
        dxh = d * g_ref[...]
        dx_ref[...] = r_ref[...] + r * (dxh - xh * jnp.mean(dxh * xh, axis=-1, keepdims=True))

    row_blk = pl.BlockSpec((tm, D), lambda i: (i, 0))
    vec = pl.BlockSpec((1, D), lambda i: (0, 0))
    return pl.pallas_call(body, name=name, grid=(S // tm,), in_specs=[row_blk, vec, row_blk, row_blk],
                          out_specs=[row_blk, vec], out_shape=[_sds((S, D), f32), _sds((1, D), f32)],
                          compiler_params=_params(1))(x, g, dxn, dres)


def _mm_nt_rms_bwd(name, a, a_spec, w3, x, g, dres, tm):
    S = x.shape[0]
    G, Dout, Kw = w3.shape

    def body(a_ref, w_ref, x_ref, g_ref, r_ref, dx_ref, dg_ref, acc_ref):
        i, k = pl.program_id(0), pl.program_id(1)
        part = lax.dot_general(a_ref[...].astype(bf16), w_ref[...], _DIMS["nt"], preferred_element_type=f32)

        @pl.when(k == 0)
        def _():
            acc_ref[...] = part

        @pl.when(k > 0)
        def _():
            acc_ref[...] += part

        @pl.when(k == G - 1)
        def _():
            @pl.when(i == 0)
            def _():
                dg_ref[...] = jnp.zeros_like(dg_ref)

            for rows in (slice(0, tm // 2), slice(tm // 2, tm)):
                x_t = x_ref[rows, :]
                d = acc_ref[rows, :]
                r = lax.rsqrt(jnp.mean(x_t * x_t, axis=-1, keepdims=True) + EPS)
                xh = x_t * r
                dg_ref[...] += jnp.sum(d * xh, axis=0, keepdims=True)
                dxh = d * g_ref[...]
                dx_ref[rows, :] = r_ref[rows, :] + r * (dxh - xh * jnp.mean(dxh * xh, axis=-1, keepdims=True))

    row_blk = pl.BlockSpec((tm, Dout), lambda i, k: (i, 0))
    vec = pl.BlockSpec((1, Dout), lambda i, k: (0, 0))
    return pl.pallas_call(body, name=name, grid=(S // tm, G),
                          in_specs=[a_spec, pl.BlockSpec((None, Dout, Kw), lambda i, k: (k, 0, 0)), row_blk, vec, row_blk],
                          out_specs=[row_blk, vec], out_shape=[_sds((S, Dout), f32), _sds((1, Dout), f32)],
                          scratch_shapes=[pltpu.VMEM((tm, Dout), f32)], compiler_params=_params(2, True))(a, w3, x, g, dres)


LRU_TT = 256
SCAN_UNROLL = 8


HALO = 16


def _halo(ref, i, S):
    nt = S // LRU_TT
    t0 = pl.multiple_of(i * LRU_TT, LRU_TT)
    p0 = pl.multiple_of(jnp.maximum(t0 - HALO, 0), HALO)
    n0 = pl.multiple_of(jnp.minimum(t0 + LRU_TT, S - HALO), HALO)
    prev = jnp.where(i > 0, ref[pl.ds(p0, HALO), :].astype(f32), 0.0)
    nxt = jnp.where(i < nt - 1, ref[pl.ds(n0, HALO), :].astype(f32), 0.0)
    return jnp.concatenate([prev, ref[pl.ds(t0, LRU_TT), :].astype(f32), nxt], axis=0)


def _shift(ext, k):
    n = LRU_TT + 2 * HALO
    return pltpu.roll(ext, (-k) % n, 0)[HALO:HALO + LRU_TT]


def _lru_gates(uc, wbd, ba, bx):
    pre = jnp.dot(uc.astype(bf16), wbd, preferred_element_type=f32)
    r_f = _sigmoid(pre[:, 0:CW] + ba[0:1])
    i_f = _sigmoid(pre[:, CW:2 * CW] + bx[0:1])
    r_b = _sigmoid(pre[:, 2 * CW:3 * CW] + ba[1:2])
    i_b = _sigmoid(pre[:, 3 * CW:4 * CW] + bx[1:2])
    return r_f, i_f, r_b, i_b


def _lru_coeffs(r, sp):
    log_a = (-RGLRU_C * r) * sp
    a = jnp.exp(log_a)
    beta = jnp.sqrt(jnp.maximum(_one_minus_exp2x(log_a, a), 0.0))
    return a, beta


def _lru_coeffs_inv(r, sp):
    log_a = (-RGLRU_C * r) * sp
    a = jnp.exp(log_a)
    om = jnp.maximum(_one_minus_exp2x(log_a, a), 0.0)
    return a, jnp.sqrt(om), lax.rsqrt(om)


def _conv_tile(u_ref, i, S, cw, cb):
    ext = _halo(u_ref, i, S)
    um2, um1, u0, up1 = _shift(ext, -2), _shift(ext, -1), ext[HALO:HALO + LRU_TT], _shift(ext, 1)
    uc = um2 * cw[0:1] + um1 * cw[1:2] + u0 * cw[2:3] + up1 * cw[3:4] + cb
    return uc, (um2, um1, u0, up1)


def _scan_pair(S, fwd_a, fwd_b, fwd_out, rev_a, rev_b, rev_out):
    ng = S // 8
    idx = lax.broadcasted_iota(jnp.int32, (8, CW), 0)

    def local(a, b, rev):
        for sh in (1, 2, 4):
            if rev:
                keep = idx < 8 - sh
                amt = 8 - sh
            else:
                keep = idx >= sh
                amt = sh
            a_s = jnp.where(keep, pltpu.roll(a, amt, 0), 1.0)
            b_s = jnp.where(keep, pltpu.roll(b, amt, 0), 0.0)
            b = a * b_s + b
            a = a * a_s
        return a, b

    def step(it, carry):
        cf, cr = carry
        fwd_rows = [pl.multiple_of((it * SCAN_UNROLL + j) * 8, 8) for j in range(SCAN_UNROLL)]
        rev_rows = [pl.multiple_of((ng - 1 - (it * SCAN_UNROLL + j)) * 8, 8) for j in range(SCAN_UNROLL)]
        fwd_loc = [local(fwd_a(r), fwd_b(r), False) for r in fwd_rows]
        rev_loc = [local(rev_a(r), rev_b(r), True) for r in rev_rows]
        for j in range(SCAN_UNROLL):
            a, b = fwd_loc[j]
            h = a * cf + b
            fwd_out[pl.ds(fwd_rows[j], 8), :] = h
            cf = jnp.broadcast_to(h[7:8, :], (8, CW))
            a, b = rev_loc[j]
            h = a * cr + b
            rev_out[pl.ds(rev_rows[j], 8), :] = h
            cr = jnp.broadcast_to(h[0:1, :], (8, CW))
        return cf, cr

    zero = jnp.zeros((8, CW), f32)
    lax.fori_loop(0, ng // SCAN_UNROLL, step, (zero, zero))


def _lru_specs(S):
    seq = lambda off: pl.BlockSpec((S, CW), lambda j: (0, off + j))
    par = lambda rows: pl.BlockSpec((rows, CW), lambda j: (0, j))
    return seq, par


def _lru_fwd(proj, conv_w, conv_b, lam, ba, bx, wbd):
    S = proj.shape[0]
    nt = S // LRU_TT

    def body(u_ref, g_ref, cw_ref, cb_ref, lam_ref, ba_ref, bx_ref, wbd_ref, y_ref, state_ref, af_ref, bf_ref, ab_ref, bb_ref,
             sems):
        cw, cb, ba_v, bx_v, wbd_v = cw_ref[...], cb_ref[...], ba_ref[...], bx_ref[...], wbd_ref[...]
        sp = jax.nn.softplus(-lam_ref[...])
        cols = pl.ds(pl.multiple_of(pl.program_id(0) * CW, CW), CW)
        save = [pltpu.make_async_copy(ref, state_ref.at[k, :, cols], sems.at[k])
                for k, ref in enumerate((af_ref, bf_ref, ab_ref, bb_ref))]

        def phase1(i, c):
            uc, _ = _conv_tile(u_ref, i, S, cw, cb)
            r_f, i_f, r_b, i_b = _lru_gates(uc, wbd_v, ba_v, bx_v)
            rows = pl.ds(pl.multiple_of(i * LRU_TT, LRU_TT), LRU_TT)
            a, beta = _lru_coeffs(r_f, sp[0:1])
            af_ref[rows, :] = a
            bf_ref[rows, :] = beta * (i_f * uc)
            a, beta = _lru_coeffs(r_b, sp[1:2])
            ab_ref[rows, :] = a
            bb_ref[rows, :] = beta * (i_b * uc)
            return c

        lax.fori_loop(0, nt, phase1, 0)
        row8 = lambda ref: (lambda r0: ref[pl.ds(r0, 8), :])
        _scan_pair(S, row8(af_ref), row8(bf_ref), bf_ref, row8(ab_ref), row8(bb_ref), bb_ref)
        for cp in save:
            cp.start()

        def phase3(i, c):
            rows = pl.ds(pl.multiple_of(i * LRU_TT, LRU_TT), LRU_TT)
            y = (bf_ref[rows, :] + bb_ref[rows, :]) * jax.nn.gelu(g_ref[rows, :].astype(f32))
            y_ref[rows, :] = y.astype(y_ref.dtype)
            return c

        lax.fori_loop(0, nt, phase3, 0)
        for cp in save:
            cp.wait()

    seq, par = _lru_specs(S)
    return pl.pallas_call(
        body, name="lru_fwd", grid=(NCH,),
        in_specs=[seq(0), seq(NCH), par(4), par(1), par(2), par(2), par(2),
                  pl.BlockSpec((None, CW, 4 * CW), lambda j: (j, 0, 0))],
        out_specs=[seq(0), ANY], out_shape=[_sds((S, D), bf16), _sds((4, S, D), f32)],
        scratch_shapes=[pltpu.VMEM((S, CW), f32)] * 4 + [pltpu.SemaphoreType.DMA((4,))], compiler_params=_params(1, True),
    )(proj, proj, conv_w, conv_b, lam, ba, bx, wbd)


def _lru_bwd(proj, dy, state, dproj, conv_w, conv_b, lam, ba, bx, wbd):
    S = proj.shape[0]
    nt = S // LRU_TT

    def body(u_ref, g_ref, dy_ref, state_ref, dproj_in, cw_ref, cb_ref, lam_ref, ba_ref, bx_ref, wbd_ref,
             dproj_ref, dcw_ref, dcb_ref, dlam_ref, dba_ref, dbx_ref, dwbd_ref,
             af_ref, bf_ref, ab_ref, bb_ref, dh_ref, du_ref, dg_ref, sems):
        cw, cb, ba_v, bx_v, wbd_v = cw_ref[...], cb_ref[...], ba_ref[...], bx_ref[...], wbd_ref[...]
        lam_v = lam_ref[...]
        sp = jax.nn.softplus(-lam_v)
        chunk = pl.program_id(0)

        def out_copies(j):
            c0 = pl.multiple_of(j * CW, CW)
            return [pltpu.make_async_copy(du_ref, dproj_ref.at[:, pl.ds(c0, CW)], sems.at[4]),
                    pltpu.make_async_copy(dg_ref, dproj_ref.at[:, pl.ds(D + c0, CW)], sems.at[5])]

        @pl.when(chunk >= 1)
        def _():
            for cp in out_copies(chunk - 1):
                cp.wait()

        cols = pl.ds(pl.multiple_of(chunk * CW, CW), CW)
        load = [pltpu.make_async_copy(state_ref.at[k, :, cols], ref, sems.at[k])
                for k, ref in enumerate((af_ref, bf_ref, ab_ref, bb_ref))]
        for cp in load:
            cp.start()
        for cp in load:
            cp.wait()
        row8 = lambda ref: (lambda r0: ref[pl.ds(r0, 8), :])

        def phase0(i, c):
            rows = pl.ds(pl.multiple_of(i * LRU_TT, LRU_TT), LRU_TT)
            gl, dgl = _gelu_and_grad(g_ref[rows, :].astype(f32))
            dyt = dy_ref[rows, :].astype(f32)
            dh_ref[rows, :] = dyt * gl
            dg_ref[rows, :] = ((dyt * (bf_ref[rows, :] + bb_ref[rows, :])) * dgl).astype(dg_ref.dtype)
            return c

        lax.fori_loop(0, nt, phase0, 0)

        def scaled_dh(a_ref):
            def f(r0):
                return a_ref[pl.ds(r0, 8), :] * dh_ref[pl.ds(r0, 8), :]
            return f

        _scan_pair(S, row8(ab_ref), scaled_dh(ab_ref), ab_ref, row8(af_ref), scaled_dh(af_ref), af_ref)

        dcw_ref[...] = jnp.zeros_like(dcw_ref)
        dcb_ref[...] = jnp.zeros_like(dcb_ref)
        dlam_ref[...] = jnp.zeros_like(dlam_ref)
        dba_ref[...] = jnp.zeros_like(dba_ref)
        dbx_ref[...] = jnp.zeros_like(dbx_ref)
        dwbd_ref[...] = jnp.zeros_like(dwbd_ref)

        def direction(uc, r, i_g, dht, h_nb, sp_d):
            a, beta, inv_beta = _lru_coeffs_inv(r, sp_d)
            da = dht * h_nb
            dbeta = dht * (i_g * uc)
            d_iu = dht * beta
            dlog_a = da * a - (a * a) * (dbeta * inv_beta)
            dlr = dlog_a * r
            dsp = -RGLRU_C * jnp.sum(dlr, axis=0, keepdims=True)
            dpre_r = (dlr * (1.0 - r)) * (-RGLRU_C * sp_d)
            dpre_i = (d_iu * uc) * (i_g * (1.0 - i_g))
            return dpre_r, dpre_i, d_iu * i_g, dsp

        def phase4(i, c):
            uc, (um2, um1, u0, up1) = _conv_tile(u_ref, i, S, cw, cb)
            r_f, i_f, r_b, i_b = _lru_gates(uc, wbd_v, ba_v, bx_v)
            rows = pl.ds(pl.multiple_of(i * LRU_TT, LRU_TT), LRU_TT)
            dh = dh_ref[rows, :]
            dht_f = dh + _shift(_halo(af_ref, i, S), 1)
            h_prev = _shift(_halo(bf_ref, i, S), -1)
            dht_b = dh + _shift(_halo(ab_ref, i, S), -1)
            h_next = _shift(_halo(bb_ref, i, S), 1)
            prf, pif, duc_f, dsp_f = direction(uc, r_f, i_f, dht_f, h_prev, sp[0:1])
            prb, pib, duc_b, dsp_b = direction(uc, r_b, i_b, dht_b, h_next, sp[1:2])
            dpre = jnp.concatenate([prf, pif, prb, pib], axis=1)
            dpre_b = dpre.astype(bf16)
            duc = (duc_f + duc_b) + lax.dot_general(dpre_b, wbd_v, _DIMS["nt"], preferred_element_type=f32)
            dwbd_ref[...] += lax.dot_general(uc.astype(bf16), dpre_b, _DIMS["tn"], preferred_element_type=f32)
            colsum = lambda v: jnp.sum(v, axis=0, keepdims=True)
            dba_ref[...] += jnp.concatenate([colsum(prf), colsum(prb)], axis=0)
            dbx_ref[...] += jnp.concatenate([colsum(pif), colsum(pib)], axis=0)
            dlam_ref[...] += jnp.concatenate([dsp_f, dsp_b], axis=0)
            dcb_ref[...] += colsum(duc)
            dcw_ref[...] += jnp.concatenate([colsum(duc * um2), colsum(duc * um1), colsum(duc * u0),
                                             colsum(duc * up1)], axis=0)
            af_ref[rows, :] = duc
            return c

        lax.fori_loop(0, nt, phase4, 0)
        dlam_ref[...] = dlam_ref[...] * (-_sigmoid(-lam_v))

        def phase5(i, c):
            ext = _halo(af_ref, i, S)
            rows = pl.ds(pl.multiple_of(i * LRU_TT, LRU_TT), LRU_TT)
            du = (_shift(ext, 2) * cw[0:1] + _shift(ext, 1) * cw[1:2] + ext[HALO:HALO + LRU_TT] * cw[2:3]
                  + _shift(ext, -1) * cw[3:4])
            du_ref[rows, :] = du.astype(du_ref.dtype)
            return c

        lax.fori_loop(0, nt, phase5, 0)
        for cp in out_copies(chunk):
            cp.start()

        @pl.when(chunk == NCH - 1)
        def _():
            for cp in out_copies(chunk):
                cp.wait()

    seq, par = _lru_specs(S)
    return pl.pallas_call(
        body, name="lru_bwd", grid=(NCH,),
        in_specs=[seq(0), seq(NCH), pl.BlockSpec((None, S, CW), lambda j: (0, 0, j)), ANY, ANY,
                  par(4), par(1), par(2), par(2), par(2), pl.BlockSpec((None, CW, 4 * CW), lambda j: (j, 0, 0))],
        out_specs=[ANY, par(4), par(1), par(2), par(2), par(2),
                   pl.BlockSpec((None, CW, 4 * CW), lambda j: (j, 0, 0))],
        out_shape=[_sds(dproj.shape, bf16), _sds((4, D), f32), _sds((1, D), f32), _sds((2, D), f32),
                   _sds((2, D), f32), _sds((2, D), f32), _sds((NCH, CW, 4 * CW), f32)],
        scratch_shapes=[pltpu.VMEM((S, CW), f32)] * 5 + [pltpu.VMEM((S, CW), bf16)] * 2 + [pltpu.SemaphoreType.DMA((6,))],
        input_output_aliases={4: 0}, compiler_params=_params(1, True),
    )(proj, proj, dy, state, dproj, conv_w, conv_b, lam, ba, bx, wbd)


_SLOPES = [2.0 ** (-8.0 * (h + 1) / NH) for h in range(NH)]


def _half_mask(shape, e):
    lane = lax.broadcasted_iota(jnp.int32, shape, 1)
    return (lane < HD) if e == 0 else (lane >= HD)


def _both_halves(x, src):
    return jnp.where(_half_mask(x.shape, src), x, pltpu.roll(x, HD, 1))


def _attn_base(n, S):
    tq = lax.broadcasted_iota(jnp.int32, (BLK, 3 * BLK), 0)
    sk = lax.broadcasted_iota(jnp.int32, (BLK, 3 * BLK), 1)
    dist = jnp.abs(tq + BLK - sk)
    kpos = n * BLK - BLK + sk
    valid = (dist <= BLK) & (kpos >= 0) & (kpos < S)
    return jnp.where(valid, -dist.astype(f32), NEG_INF)


def _group_heads(ref, kvh, scale):
    parts = []
    for i in range(4):
        pair = 2 * kvh + i // 2
        x = ref[:, pair * 128:(pair + 1) * 128].astype(f32)
        parts.append(jnp.where(_half_mask(x.shape, i % 2), x * scale, 0.0))
    return parts


def _stack_bf16(parts):
    return jnp.concatenate([p.astype(bf16) for p in parts], axis=0)


def _attn_softmax(s_raw, base, slope, sink):
    s = s_raw + slope * base
    m = jnp.maximum(jnp.max(s, axis=-1, keepdims=True), sink)
    p = jnp.exp(s - m)
    esink = jnp.exp(sink - m)
    inv = 1.0 / (jnp.sum(p, axis=-1, keepdims=True) + esink)
    return p, inv, esink * inv


def _attn_specs(S):
    nb = S // BLK
    q_spec = pl.BlockSpec((BLK, D), lambda n: (n, 2))
    kv = lambda col: [pl.BlockSpec((BLK, 256), lambda n: (jnp.maximum(n - 1, 0), col)),
                      pl.BlockSpec((BLK, 256), lambda n: (n, col)),
                      pl.BlockSpec((BLK, 256), lambda n: (jnp.minimum(n + 1, nb - 1), col))]
    return nb, q_spec, kv(COL_K), kv(COL_V)


def _attn_fwd(proj, sink):
    S = proj.shape[0]
    nb, q_spec, k_specs, v_specs = _attn_specs(S)

    def body(sink_ref, q_ref, kp_ref, kc_ref, kn_ref, vp_ref, vc_ref, vn_ref, o_ref):
        base = _attn_base(pl.program_id(0), S)
        kcat = jnp.concatenate([kp_ref[...], kc_ref[...], kn_ref[...]], axis=0).astype(f32)
        vcat = jnp.concatenate([vp_ref[...], vc_ref[...], vn_ref[...]], axis=0).astype(f32)
        even = _half_mask((BLK, 128), 0)
        for kvh in range(NH // 4):
            ch, off = kvh // 2, kvh % 2
            kb = _both_halves(kcat[:, ch * 128:(ch + 1) * 128], off).astype(bf16)
            vb = _both_halves(vcat[:, ch * 128:(ch + 1) * 128], off).astype(bf16)
            q4 = _stack_bf16(_group_heads(q_ref, kvh, HD ** -0.5))
            s4 = lax.dot_general(q4, kb, _DIMS["nt"], preferred_element_type=f32)
            ps, invs = [], []
            for i in range(4):
                h = 4 * kvh + i
                p, inv, _ = _attn_softmax(s4[i * BLK:(i + 1) * BLK], base, _SLOPES[h], sink_ref[0, h])
                ps.append(p)
                invs.append(inv)
            o4 = jnp.dot(_stack_bf16(ps), vb, preferred_element_type=f32)
            for pr in range(2):
                lo = o4[(2 * pr) * BLK:(2 * pr + 1) * BLK] * invs[2 * pr]
                hi = o4[(2 * pr + 1) * BLK:(2 * pr + 2) * BLK] * invs[2 * pr + 1]
                pair = 2 * kvh + pr
                o_ref[:, pair * 128:(pair + 1) * 128] = jnp.where(even, lo, hi).astype(o_ref.dtype)

    return pl.pallas_call(
        body, name="attn_fwd", grid=(nb,),
        in_specs=[pl.BlockSpec(memory_space=pltpu.SMEM), q_spec] + k_specs + v_specs,
        out_specs=pl.BlockSpec((BLK, D), lambda n: (n, 0)), out_shape=_sds((S, D), bf16),
        compiler_params=_params(1, True))(sink, proj, proj, proj, proj, proj, proj, proj)


def _attn_bwd(proj, sink, y_b, dy, dproj):
    S = proj.shape[0]
    nb, q_spec, k_specs, v_specs = _attn_specs(S)
    q_col, kv_col = COL_Q * 256, COL_K * 256

    def body(sink_ref, q_ref, kp_ref, kc_ref, kn_ref, vp_ref, vc_ref, vn_ref, o_ref, do_ref, dproj_in,
             dproj_ref, dsink_ref, dk_ref, dv_ref, dq_buf, kv_buf, sems):
        n = pl.program_id(0)
        slot = n % 2
        dq_ref = dq_buf.at[slot]

        def dq_copy(step):
            rows = pl.ds(pl.multiple_of(step * BLK, BLK), BLK)
            return pltpu.make_async_copy(dq_buf.at[step % 2], dproj_ref.at[rows, pl.ds(q_col, D)], sems.at[step % 2])

        @pl.when(n >= 2)
        def _():
            dq_copy(n - 2).wait()

        @pl.when(n == 0)
        def _():
            dk_ref[...] = jnp.zeros_like(dk_ref)
            dv_ref[...] = jnp.zeros_like(dv_ref)
            dsink_ref[...] = jnp.zeros_like(dsink_ref)

        base = _attn_base(n, S)
        kcat = jnp.concatenate([kp_ref[...], kc_ref[...], kn_ref[...]], axis=0).astype(f32)
        vcat = jnp.concatenate([vp_ref[...], vc_ref[...], vn_ref[...]], axis=0).astype(f32)
        dk_rows, dv_rows = [[], []], [[], []]
        scale = HD ** -0.5
        even = _half_mask((BLK, 128), 0)
        for kvh in range(NH // 4):
            ch, off = kvh // 2, kvh % 2
            kb = _both_halves(kcat[:, ch * 128:(ch + 1) * 128], off).astype(bf16)
            vb = _both_halves(vcat[:, ch * 128:(ch + 1) * 128], off).astype(bf16)
            q_parts = _group_heads(q_ref, kvh, scale)
            d_parts = _group_heads(do_ref, kvh, 1.0)
            s4 = lax.dot_general(_stack_bf16(q_parts), kb, _DIMS["nt"], preferred_element_type=f32)
            dp4 = lax.dot_general(_stack_bf16(d_parts), vb, _DIMS["nt"], preferred_element_type=f32)
            ts, ps, qn, dn, invs = [], [], [], [], []
            for i in range(4):
                h = 4 * kvh + i
                pair = 2 * kvh + i // 2
                rows = slice(i * BLK, (i + 1) * BLK)
                p, inv, psink = _attn_softmax(s4[rows], base, _SLOPES[h], sink_ref[0, h])
                delta = jnp.sum(d_parts[i] * o_ref[:, pair * 128:(pair + 1) * 128].astype(f32), axis=-1, keepdims=True)
                dsink_ref[h:h + 1, :] += jnp.broadcast_to(-jnp.sum(psink * delta, axis=0, keepdims=True), (1, 128))
                ts.append(p * (dp4[rows] - delta))
                ps.append(p)
                qn.append(q_parts[i] * inv)
                dn.append(d_parts[i] * inv)
                invs.append(inv)
            t4 = _stack_bf16(ts)
            dq4 = jnp.dot(t4, kb, preferred_element_type=f32)
            for pr in range(2):
                lo = dq4[(2 * pr) * BLK:(2 * pr + 1) * BLK] * invs[2 * pr]
                hi = dq4[(2 * pr + 1) * BLK:(2 * pr + 2) * BLK] * invs[2 * pr + 1]
                pair = 2 * kvh + pr
                dq_ref[:, pair * 128:(pair + 1) * 128] = (jnp.where(even, lo, hi) * scale).astype(dq_ref.dtype)
            dk_t = lax.dot_general(_stack_bf16(qn), t4, _DIMS["tn"], preferred_element_type=f32)
            dv_t = lax.dot_general(_stack_bf16(dn), _stack_bf16(ps), _DIMS["tn"], preferred_element_type=f32)
            dk_rows[ch].append(dk_t[0:HD] + dk_t[HD:2 * HD])
            dv_rows[ch].append(dv_t[0:HD] + dv_t[HD:2 * HD])
        dk_acc = [jnp.concatenate(r, axis=0).T for r in dk_rows]
        dv_acc = [jnp.concatenate(r, axis=0).T for r in dv_rows]
        for j in range(3):
            blk = n + (j - 1)

            @pl.when((blk >= 0) & (blk < nb))
            def _():
                rows = pl.ds(pl.multiple_of(blk * BLK, BLK), BLK)
                for ch in range(2):
                    dk_ref[rows, ch * 128:(ch + 1) * 128] += dk_acc[ch][j * BLK:(j + 1) * BLK]
                    dv_ref[rows, ch * 128:(ch + 1) * 128] += dv_acc[ch][j * BLK:(j + 1) * BLK]

        dq_copy(n).start()

        @pl.when(n == nb - 1)
        def _():
            def cast(i, c):
                rows = pl.ds(pl.multiple_of(i * 4 * BLK, 4 * BLK), 4 * BLK)
                kv_buf[rows, 0:256] = dk_ref[rows, :].astype(bf16)
                kv_buf[rows, 256:512] = dv_ref[rows, :].astype(bf16)
                return c

            lax.fori_loop(0, S // (4 * BLK), cast, 0)
            kv_copy = pltpu.make_async_copy(kv_buf, dproj_ref.at[:, pl.ds(kv_col, 512)], sems.at[2])
            kv_copy.start()
            if nb >= 2:
                dq_copy(n - 1).wait()
            dq_copy(n).wait()
            kv_copy.wait()

    row_blk = pl.BlockSpec((BLK, D), lambda n: (n, 0))
    return pl.pallas_call(
        body, name="attn_bwd", grid=(nb,),
        in_specs=[pl.BlockSpec(memory_space=pltpu.SMEM), q_spec] + k_specs + v_specs
        + [row_blk, pl.BlockSpec((None, BLK, D), lambda n: (1, n, 0)), ANY],
        out_specs=[ANY, pl.BlockSpec((NH, 128), lambda n: (0, 0))],
        out_shape=[_sds(dproj.shape, bf16), _sds((NH, 128), f32)],
        scratch_shapes=[pltpu.VMEM((S, 256), f32), pltpu.VMEM((S, 256), f32), pltpu.VMEM((2, BLK, D), bf16),
                        pltpu.VMEM((S, 512), bf16), pltpu.SemaphoreType.DMA((3,))],
        input_output_aliases={10: 0},
        compiler_params=_params(1, True))(sink, proj, proj, proj, proj, proj, proj, proj, y_b, dy, dproj)


def _adamw(name, w, g, m, v, tr):
    R, C = w.shape
    tr = min(tr, R)

    def body(w_ref, g_ref, m_ref, v_ref, d_ref, m2_ref, v2_ref):
        g = g_ref[...]
        m2 = ADAM_B1 * m_ref[...] + (1.0 - ADAM_B1) * g
        v2 = ADAM_B2 * v_ref[...] + (1.0 - ADAM_B2) * (g * g)
        m_hat = m2 / (1.0 - ADAM_B1 ** ADAM_STEP)
        v_hat = v2 / (1.0 - ADAM_B2 ** ADAM_STEP)
        d_ref[...] = -ADAM_LR * (m_hat / (jnp.sqrt(v_hat) + ADAM_EPS) + ADAM_WD * w_ref[...])
        m2_ref[...] = m2
        v2_ref[...] = v2

    blk = pl.BlockSpec((tr, C), lambda i: (i, 0))
    return pl.pallas_call(body, name=name, grid=(R // tr,), in_specs=[blk] * 4, out_specs=[blk] * 3,
                          out_shape=[_sds((R, C), f32)] * 3, compiler_params=_params(1))(w, g, m, v)


def _pair_sum(name, c_arr, g4, recv, th):
    _, _, h, w = g4.shape
    th = min(th, h)

    def body(c_ref, g_ref, r_ref, o_ref, ob_ref):
        p = g_ref[...] + r_ref[...]
        o_ref[...] = p
        ob_ref[...] = p.astype(bf16)

    blk = pl.BlockSpec((None, th, w), lambda s, i, c_ref: (s, i, 0))
    spec = pltpu.PrefetchScalarGridSpec(
        num_scalar_prefetch=1, grid=(NCHIP, h // th),
        in_specs=[pl.BlockSpec((None, None, th, w), lambda s, i, c_ref: (s, c_ref[0], i, 0)), blk],
        out_specs=[blk, blk])
    return pl.pallas_call(body, name=name, grid_spec=spec,
                          out_shape=[_sds((NCHIP, h, w), f32), _sds((NCHIP, h, w), bf16)],
                          compiler_params=_params(2))(c_arr, g4, recv)


def _chip_sum(name, chip_arr, own4, recv3, th):
    _, h, w = own4.shape
    th = min(th, h)

    def body(s_ref, o_ref, r_ref, out_ref):
        out_ref[...] = ((o_ref[...] + r_ref[0].astype(f32)) + r_ref[1].astype(f32)) + r_ref[2].astype(f32)

    spec = pltpu.PrefetchScalarGridSpec(
        num_scalar_prefetch=1, grid=(h // th,),
        in_specs=[pl.BlockSpec((None, th, w), lambda i, s_ref: (s_ref[0], i, 0)),
                  pl.BlockSpec((3, th, w), lambda i, s_ref: (0, i, 0))],
        out_specs=pl.BlockSpec((th, w), lambda i, s_ref: (i, 0)))
    return pl.pallas_call(body, name=name, grid_spec=spec, out_shape=_sds((h, w), f32),
                          compiler_params=_params(1, True))(chip_arr, own4, recv3)


def _adamw_halves(name, c_arr, w, g_own, g_recv, m, v, th):
    h, wd = g_own.shape
    th = min(th, h)

    def body(c_ref, w_ref, go_ref, gr_ref, m_ref, v_ref, g_ref, d_ref, m2_ref, v2_ref):
        g = jnp.where(c_ref[0] == pl.program_id(0), go_ref[...], gr_ref[...])
        m2 = ADAM_B1 * m_ref[...] + (1.0 - ADAM_B1) * g
        v2 = ADAM_B2 * v_ref[...] + (1.0 - ADAM_B2) * (g * g)
        m_hat = m2 / (1.0 - ADAM_B1 ** ADAM_STEP)
        v_hat = v2 / (1.0 - ADAM_B2 ** ADAM_STEP)
        g_ref[...] = g
        d_ref[...] = -ADAM_LR * (m_hat / (jnp.sqrt(v_hat) + ADAM_EPS) + ADAM_WD * w_ref[...])
        m2_ref[...] = m2
        v2_ref[...] = v2

    nt = h // th
    full = pl.BlockSpec((th, wd), lambda hh, i, c_ref: (hh * nt + i, 0))
    half = pl.BlockSpec((th, wd), lambda hh, i, c_ref: (i, 0))
    spec = pltpu.PrefetchScalarGridSpec(num_scalar_prefetch=1, grid=(2, nt),
                                        in_specs=[full, half, half, full, full], out_specs=[full] * 4)
    return pl.pallas_call(body, name=name, grid_spec=spec, out_shape=[_sds((2 * h, wd), f32)] * 4,
                          compiler_params=_params(2))(c_arr, w, g_own, g_recv, m, v)


def _add2(name, a, b):
    def body(a_ref, b_ref, o_ref):
        o_ref[...] = a_ref[...] + b_ref[...]
    return pl.pallas_call(body, name=name, out_shape=_sds(a.shape, f32))(a, b)


def _sum4(name, b4, th):
    _, h, w = b4.shape
    th = min(th, h)

    def body(b_ref, o_ref):
        o_ref[...] = ((b_ref[0] + b_ref[1]) + b_ref[2]) + b_ref[3]

    return pl.pallas_call(body, name=name, grid=(h // th,),
                          in_specs=[pl.BlockSpec((NCHIP, th, w), lambda i: (0, i, 0))],
                          out_specs=pl.BlockSpec((th, w), lambda i: (i, 0)), out_shape=_sds((h, w), f32),
                          compiler_params=_params(1, True))(b4)


def _coords():
    x, y, c = lax.axis_index("x"), lax.axis_index("y"), lax.axis_index("c")
    return x, y, c, [(1 - x, y), (x, 1 - y), (1 - x, 1 - y)]


def _gather_chips(arrs):
    n = len(arrs)

    def body(*refs):
        ins, outs = refs[:n], refs[n:2 * n]
        send_sems, recv_sems, local_sems = refs[2 * n:2 * n + 3]
        stage = refs[2 * n + 3:]
        x, y, c, chips = _coords()
        s = 2 * x + y
        sib = (x, y, 1 - c)
        load = [pltpu.make_async_copy(ins[a], stage[a], local_sems.at[a]) for a in range(n)]
        local = [pltpu.make_async_copy(stage[a], outs[a].at[s], local_sems.at[n + a]) for a in range(n)]
        for cp in load:
            cp.start()

        def over_ici(k, a, slot, peer):
            return pltpu.make_async_remote_copy(src_ref=ins[a].at[c], dst_ref=outs[a].at[slot, c], send_sem=send_sems.at[k * n + a],
                                                recv_sem=recv_sems.at[k * n + a], device_id=peer, device_id_type=MESH)

        def to_sibling(k, a, slot, half):
            i = (3 + k) * n + a
            return pltpu.make_async_remote_copy(src_ref=outs[a].at[slot, half], dst_ref=outs[a].at[slot, half], send_sem=send_sems.at[i],
                                                recv_sem=recv_sems.at[i], device_id=sib, device_id_type=MESH)

        sends = [over_ici(k, a, s, (px, py, c)) for k, (px, py) in enumerate(chips) for a in range(n)]
        for cp in sends:
            cp.start()
        for a in range(n):
            load[a].wait()
            local[a].start()
        passed = []
        for k, (px, py) in enumerate(chips):
            for a in range(n):
                over_ici(k, a, 2 * px + py, (px, py, c)).wait_recv()
                cp = to_sibling(k, a, 2 * px + py, c)
                cp.start()
                passed.append(cp)
        for k, (px, py) in enumerate(chips):
            for a in range(n):
                to_sibling(k, a, 2 * px + py, 1 - c).wait_recv()
        for cp in sends + passed:
            cp.wait_send()
        for cp in local:
            cp.wait()

    return pl.pallas_call(
        body, name="gather_weights", in_specs=[ANY] * n, out_specs=[ANY] * n,
        out_shape=[_sds((NCHIP,) + a.shape, a.dtype) for a in arrs],
        scratch_shapes=[pltpu.SemaphoreType.DMA((6 * n,)), pltpu.SemaphoreType.DMA((6 * n,)), pltpu.SemaphoreType.DMA((2 * n,))]
        + [pltpu.VMEM(a.shape, a.dtype) for a in arrs],
        compiler_params=pltpu.CompilerParams(vmem_limit_bytes=VMEM_LIMIT),
    )(*arrs)


HBM = pl.BlockSpec(memory_space=pltpu.HBM)
SEM = pl.BlockSpec(memory_space=pltpu.SEMAPHORE)
EFFECT = pltpu.SideEffectType.DATAFLOW_SIDE_EFFECTING


def _split_start(name, n_copies, make_copies, ins, land_shapes, after):
    ni, nl = len(ins), len(land_shapes)

    def body(*refs):
        in_refs, land_refs = refs[:ni], refs[ni:ni + nl]
        send_sems, recv_sems = refs[ni + nl + 1], refs[ni + nl + 2]
        token = refs[-1]
        for cp in make_copies(in_refs, land_refs, send_sems, recv_sems):
            cp.start()
        token[...] = jnp.zeros_like(token)

    lands = [pltpu.with_memory_space_constraint(lax.empty(s.shape, s.dtype), pltpu.HBM) for s in land_shapes]
    res = pl.pallas_call(
        body, name=name,
        out_shape=(pltpu.SemaphoreType.DMA((n_copies,)), pltpu.SemaphoreType.DMA((n_copies,)),
                   *[pltpu.HBM(a.shape, a.dtype) for a in ins], *[pltpu.HBM(s.shape, s.dtype) for s in land_shapes],
                   _sds((8, 128), f32)),
        in_specs=[HBM] * (ni + nl) + [ANY], out_specs=(SEM, SEM, *[HBM] * (ni + nl), pl.BlockSpec(memory_space=pltpu.VMEM)),
        input_output_aliases={i: 2 + i for i in range(ni + nl)},
        compiler_params=pltpu.CompilerParams(has_side_effects=EFFECT),
    )(*[pltpu.with_memory_space_constraint(a, pltpu.HBM) for a in ins], *lands, after)
    return res[0], res[1], list(res[2:2 + ni]), list(res[2 + ni:2 + ni + nl]), res[-1]


def _split_wait(name, make_copies, send_sems, recv_sems, ins, lands, after):
    ni, nl = len(ins), len(lands)

    def body(*refs):
        in_refs, land_refs = refs[:ni], refs[ni:ni + nl]
        s_sems, r_sems = refs[ni + nl], refs[ni + nl + 1]
        for cp in make_copies(in_refs, land_refs, s_sems, r_sems):
            cp.wait_send()
            cp.wait_recv()

    res = pl.pallas_call(
        body, name=name, out_shape=tuple(pltpu.HBM(a.shape, a.dtype) for a in ins + lands),
        in_specs=[HBM] * (ni + nl) + [SEM, SEM, ANY], out_specs=tuple([HBM] * (ni + nl)),
        input_output_aliases={i: i for i in range(ni + nl)},
        compiler_params=pltpu.CompilerParams(has_side_effects=EFFECT),
    )(*ins, *lands, send_sems, recv_sems, after)
    return list(res[:ni]), list(res[ni:])


def _gather_copies(n):
    def make(in_refs, land_refs, send_sems, recv_sems):
        x, y, c, chips = _coords()
        s = 2 * x + y
        return [pltpu.make_async_remote_copy(src_ref=in_refs[a], dst_ref=land_refs[a].at[s], send_sem=send_sems.at[k * n + a],
                                             recv_sem=recv_sems.at[k * n + a], device_id=(px, py, c), device_id_type=MESH)
                for k, (px, py) in enumerate(chips) for a in range(n)]
    return make


def _sibling_half_copies(n):
    def make(in_refs, land_refs, send_sems, recv_sems):
        x, y, c, _ = _coords()
        return [pltpu.make_async_remote_copy(src_ref=in_refs[a].at[:, 1 - c], dst_ref=land_refs[a], send_sem=send_sems.at[a],
                                             recv_sem=recv_sems.at[a], device_id=(x, y, 1 - c), device_id_type=MESH)
                for a in range(n)]
    return make


def _chip_part_copies(n):
    def make(in_refs, land_refs, send_sems, recv_sems):
        x, y, c, chips = _coords()
        return [pltpu.make_async_remote_copy(src_ref=in_refs[a].at[2 * px + py], dst_ref=land_refs[a].at[k],
                                             send_sem=send_sems.at[k * n + a], recv_sem=recv_sems.at[k * n + a],
                                             device_id=(px, py, c), device_id_type=MESH)
                for k, (px, py) in enumerate(chips) for a in range(n)]
    return make


def _sibling_whole_copies(n):
    def make(in_refs, land_refs, send_sems, recv_sems):
        x, y, c, _ = _coords()
        return [pltpu.make_async_remote_copy(src_ref=in_refs[a], dst_ref=land_refs[a], send_sem=send_sems.at[a],
                                             recv_sem=recv_sems.at[a], device_id=(x, y, 1 - c), device_id_type=MESH)
                for a in range(n)]
    return make


def _place_own(chip_arr, owns, lands, steps):
    n = len(owns)

    def body(s_ref, *refs):
        for a in range(n):
            refs[2 * n + a][...] = refs[a][...]

    tiles = [o.shape[0] // steps for o in owns]
    spec = pltpu.PrefetchScalarGridSpec(
        num_scalar_prefetch=1, grid=(steps,),
        in_specs=[pl.BlockSpec((t, o.shape[1]), lambda i, s_ref: (i, 0)) for t, o in zip(tiles, owns)] + [ANY] * n,
        out_specs=[pl.BlockSpec((None, t, o.shape[1]), lambda i, s_ref: (s_ref[0], i, 0)) for t, o in zip(tiles, owns)])
    return pl.pallas_call(body, name="place_own", grid_spec=spec, out_shape=[_sds(l.shape, l.dtype) for l in lands],
                          input_output_aliases={1 + n + a: a for a in range(n)},
                          compiler_params=_params(1))(chip_arr, *owns, *lands)


def _sibling_halves(g4s, small):
    n = len(g4s)

    def body(*refs):
        ins, small_ref = refs[:n], refs[n]
        outs, small_out = refs[n + 1:2 * n + 1], refs[2 * n + 1]
        send_sems, recv_sems = refs[2 * n + 2:]
        x, y, c, _ = _coords()
        sib = (x, y, 1 - c)

        def remote(a, half):
            src = small_ref if a == n else ins[a].at[:, half]
            dst = small_out if a == n else outs[a]
            return pltpu.make_async_remote_copy(src_ref=src, dst_ref=dst, send_sem=send_sems.at[a], recv_sem=recv_sems.at[a],
                                                device_id=sib, device_id_type=MESH)

        sends = [remote(a, 1 - c) for a in range(n + 1)]
        for cp in sends:
            cp.start()
        for a in range(n + 1):
            remote(a, c).wait_recv()
        for cp in sends:
            cp.wait_send()

    return pl.pallas_call(
        body, name="reduce_sibling", in_specs=[ANY] * (n + 1), out_specs=[ANY] * (n + 1),
        out_shape=[_sds((g.shape[0],) + g.shape[2:], f32) for g in g4s] + [_sds(small.shape, f32)],
        scratch_shapes=[pltpu.SemaphoreType.DMA((n + 1,)), pltpu.SemaphoreType.DMA((n + 1,))],
    )(*g4s, small)


def _exchange_chips(parts, small2):
    n = len(parts)

    def body(*refs):
        ins, small_ref = refs[:n], refs[n]
        outs, small_out = refs[n + 1:2 * n + 1], refs[2 * n + 1]
        send_sems, recv_sems, local_sem = refs[2 * n + 2:]
        x, y, c, chips = _coords()
        s = 2 * x + y
        local = pltpu.make_async_copy(small_ref.at[c], small_out.at[s], local_sem)
        local.start()

        def remote(k, a, dest_chip, small_slot, peer):
            if a == n:
                src, dst = small_ref.at[c], small_out.at[small_slot]
            else:
                src, dst = ins[a].at[dest_chip], outs[a].at[k]
            i = k * (n + 1) + a
            return pltpu.make_async_remote_copy(src_ref=src, dst_ref=dst, send_sem=send_sems.at[i], recv_sem=recv_sems.at[i],
                                                device_id=peer, device_id_type=MESH)

        sends = [remote(k, a, 2 * px + py, s, (px, py, c)) for k, (px, py) in enumerate(chips) for a in range(n + 1)]
        for cp in sends:
            cp.start()
        for k, (px, py) in enumerate(chips):
            for a in range(n + 1):
                remote(k, a, s, 2 * px + py, (px, py, c)).wait_recv()
        for cp in sends:
            cp.wait_send()
        local.wait()

    m = 3 * (n + 1)
    return pl.pallas_call(
        body, name="reduce_chips", in_specs=[ANY] * (n + 1), out_specs=[ANY] * (n + 1),
        out_shape=[_sds((3,) + p.shape[1:], p.dtype) for p in parts] + [_sds((NCHIP,) + small2.shape[1:], f32)],
        scratch_shapes=[pltpu.SemaphoreType.DMA((m,)), pltpu.SemaphoreType.DMA((m,)), pltpu.SemaphoreType.DMA],
    )(*parts, small2)


def _share_sibling(halves):
    n = len(halves)

    def body(*refs):
        ins, outs = refs[:n], refs[n:2 * n]
        send_sems, recv_sems = refs[2 * n:]
        x, y, c, _ = _coords()
        sib = (x, y, 1 - c)
        sends = [pltpu.make_async_remote_copy(src_ref=ins[a], dst_ref=outs[a], send_sem=send_sems.at[a], recv_sem=recv_sems.at[a],
                                              device_id=sib, device_id_type=MESH) for a in range(n)]
        for cp in sends:
            cp.start()
        for cp in sends:
            cp.wait()

    return pl.pallas_call(
        body, name="reduce_share", in_specs=[ANY] * n, out_specs=[ANY] * n,
        out_shape=[_sds(h.shape, f32) for h in halves],
        scratch_shapes=[pltpu.SemaphoreType.DMA((n,)), pltpu.SemaphoreType.DMA((n,))],
    )(*halves)


def _block_diag_pairs(w):
    w = w.reshape(NCH, 2, HD, HD)
    z = jnp.zeros((NCH, HD, HD), w.dtype)
    return jnp.concatenate([jnp.concatenate([w[:, 0], z], axis=2), jnp.concatenate([z, w[:, 1]], axis=2)], axis=1)


def _diag_blocks(m):
    return jnp.stack([m[:, :HD, :HD], m[:, HD:, HD:]], axis=1).reshape(NH, HD, HD)


def _pack(vs, rows):
    flat = jnp.concatenate([v.reshape(-1) for v in vs])
    return jnp.pad(flat, (0, rows * 128 - flat.shape[0])).reshape(rows, 128)


def _unpack(packed, shapes):
    flat = packed.reshape(-1)
    out, off = [], 0
    for shp in shapes:
        size = math.prod(shp)
        out.append(flat[off:off + size].reshape(shp))
        off += size
    return out


def _rows_for(sizes, multiple):
    rows = -(-sum(sizes) // 128)
    return -(-rows // multiple) * multiple


def kernel(x, norm_mix_g, w_in, b_gate, conv_w, conv_b, lru_lambda, lru_wa, lru_ba, lru_wx, lru_bx, attn_sink, w_out, norm_ffn_g, w_ffn_in, w_ffn_out, norm_final_g, loss_target, m_norm_mix_g, m_w_in, m_b_gate, m_conv_w, m_conv_b, m_lru_lambda, m_lru_wa, m_lru_ba, m_lru_wx, m_lru_bx, m_attn_sink, m_w_out, m_norm_ffn_g, m_w_ffn_in, m_w_ffn_out, m_norm_final_g, v_norm_mix_g, v_w_in, v_b_gate, v_conv_w, v_conv_b, v_lru_lambda, v_lru_wa, v_lru_ba, v_lru_wx, v_lru_bx, v_attn_sink, v_w_out, v_norm_ffn_g, v_w_ffn_in, v_w_ffn_out, v_norm_final_g):
    S = x.shape[1]
    xs = x[0]
    tgt = loss_target[0]
    cx, cy, cc = lax.axis_index("x"), lax.axis_index("y"), lax.axis_index("c")
    chip = 2 * cx + cy
    SW = D // NCHIP

    small_shard = _pack([conv_w[0], lru_lambda[0], lru_ba[0], lru_bx[0]], 32)
    halves_of = lambda a: a.reshape(2, a.shape[0] // 2, a.shape[1])
    w_in_g, small_g = _gather_chips([halves_of(w_in[0].astype(bf16)), halves_of(small_shard)])
    w_in_g = w_in_g.reshape(NCHIP, D, SHW)
    small_g = small_g.reshape(NCHIP, 32, 128)
    late = [w_ffn_in[0].astype(bf16), w_out[0].astype(bf16), w_ffn_out[0].astype(bf16)]
    late_send, late_recv, late_src, late_land, late_token = _split_start(
        "gather_late_start", 9, _gather_copies(3), late, [_sds((NCHIP,) + a.shape, bf16) for a in late], small_g)
    small_parts = [_unpack(small_g[s], [(4, SW), (2, SW), (2, SW), (2, SW)]) for s in range(NCHIP)]
    conv_w_f, lam_f, ba_f, bx_f = [jnp.concatenate([small_parts[s][p] for s in range(NCHIP)], axis=1) for p in range(4)]
    wbd = jnp.concatenate([_block_diag_pairs(lru_wa[0, 0]), _block_diag_pairs(lru_wx[0, 0]),
                           _block_diag_pairs(lru_wa[0, 1]), _block_diag_pairs(lru_wx[0, 1])], axis=2).astype(bf16)
    conv_b_f = conv_b
    sink = attn_sink

    xn, proj = _rms_matmul("rms_proj", xs, norm_mix_g + late_token[0:1, 0:1], w_in_g, 1024)
    y_a, lru_state = _lru_fwd(proj, conv_w_f, conv_b_f, lam_f, ba_f, bx_f, wbd)
    y_b = _attn_fwd(proj, sink)
    late_src, late_land = _split_wait("gather_late_wait", _gather_copies(3), late_send, late_recv, late_src, late_land, y_b)
    chip_arr = chip.reshape(1).astype(jnp.int32)
    w_ffn_in_g, w_out_g, w_ffn_out_g = _place_own(chip_arr, late_src, late_land, 4)
    w_out_f = w_out_g.reshape(D, D)
    w_ffn_out_f = w_ffn_out_g.reshape(FF, D)
    merged, x1 = _merge_out_proj(proj, b_gate, y_a, y_b, w_out_f, xs, 512)
    xn2, gu, act = _rms_matmul_swiglu("rms_ffn_in", x1, norm_ffn_g, w_ffn_in_g, 1024)
    dx2, loss_row, dg3 = _ffn_out_loss_bwd(act, w_ffn_out_f, x1, norm_final_g.reshape(1, D), tgt, 512)

    tm = min(1024, S)
    tk = min(2048, S)
    gw_ffn_out = _mm_tn("dw_ffn_out", act, pl.BlockSpec((tk, SHW), lambda i, k: (k, i)),
                        dx2, pl.BlockSpec((tk, D), lambda i, k: (k, 0)),
                        _sds((FF, D), f32), pl.BlockSpec((SHW, D), lambda i, k: (i, 0)), (2, S // tk), (SHW, D))
    dgu = _swiglu_bwd(dx2, w_ffn_out_f, gu, 256)
    gw_ffn_in = _mm_tn("dw_ffn_in", xn2, pl.BlockSpec((tk, D), lambda g, k: (k, 0)),
                       dgu, pl.BlockSpec((None, tk, SHW), lambda g, k: (g // 2, k, g % 2)),
                       _sds((NCHIP, D, SHW), f32), pl.BlockSpec((None, D, SHW), lambda g, k: (g, 0, 0)),
                       (NCHIP, S // tk), (D, SHW))
    c_arr = cc.reshape(1).astype(jnp.int32)
    early_names, early_tiles = ["w_ffn_in", "w_ffn_out"], [256, 352]
    early = [gw_ffn_in.reshape(NCHIP, 2, D // 2, SHW), gw_ffn_out.reshape(NCHIP, 2, FF // NCHIP // 2, D)]
    ea_send, ea_recv, ea_src, ea_land, ea_token = _split_start(
        "reduce_early_sibling_start", 2, _sibling_half_copies(2), early,
        [_sds((NCHIP,) + g.shape[2:], f32) for g in early], dgu)
    dx1, dg2 = _mm_nt_rms_bwd("dxn2_rms_bwd", dgu, pl.BlockSpec((None, tm, SHW), lambda i, g: (g // 2, i, g % 2)), w_ffn_in_g,
                              x1, norm_ffn_g + ea_token[0:1, 0:1], dx2, tm)

    gw_out = _mm_tn("dw_out", merged, pl.BlockSpec((tk, D), lambda i, k: (k, 0)),
                    dx1, pl.BlockSpec((tk, D), lambda i, k: (k, 0)),
                    _sds((D, D), f32), pl.BlockSpec((D, D), lambda i, k: (0, 0)), (1, S // tk), (D, D))
    dproj, dy, db_gate = _merge_bwd(proj, b_gate, y_a, y_b, dx1, w_out_f, 512)
    ea_src, ea_land = _split_wait("reduce_early_sibling_wait", _sibling_half_copies(2), ea_send, ea_recv, ea_src, ea_land, dy)
    early_pairs = [_pair_sum("pair_sum_" + nm, c_arr, g4, r, th)
                   for nm, g4, r, th in zip(early_names, ea_src, ea_land, early_tiles)]
    eb_send, eb_recv, eb_src, eb_land, eb_token = _split_start(
        "reduce_early_chips_start", 6, _chip_part_copies(2), [p[1] for p in early_pairs],
        [_sds((3,) + p[1].shape[1:], bf16) for p in early_pairs], early_pairs[0][0])
    dproj, dsink = _attn_bwd(proj, sink + eb_token[0:1, 0:1], y_b, dy, dproj)
    _, eb_land = _split_wait("reduce_early_chips_wait", _chip_part_copies(2), eb_send, eb_recv, eb_src, eb_land, dsink)
    early_halves = [_chip_sum("chip_sum_" + nm, chip_arr, p[0], r3, th)
                    for nm, p, r3, th in zip(early_names, early_pairs, eb_land, early_tiles)]
    ec_send, ec_recv, ec_src, ec_land, ec_token = _split_start(
        "reduce_early_share_start", 2, _sibling_whole_copies(2), early_halves, [_sds(h.shape, f32) for h in early_halves], dsink)
    dproj, dcw, dcb, dlam, dba, dbx, dwbd = _lru_bwd(proj, dy, lru_state, dproj, conv_w_f, conv_b_f + ec_token[0:1, 0:1], lam_f,
                                                     ba_f, bx_f, wbd)
    early_halves, early_other = _split_wait("reduce_early_share_wait", _sibling_whole_copies(2), ec_send, ec_recv, ec_src, ec_land, dcb)
    gw_in = _mm_tn("dw_in", xn, pl.BlockSpec((tk, D), lambda g, k: (k, 0)),
                   dproj, pl.BlockSpec((tk, SHW), lambda g, k: (k, g)),
                   _sds((NCHIP, D, SHW), f32), pl.BlockSpec((None, D, SHW), lambda g, k: (g, 0, 0)),
                   (NCHIP, S // tk), (D, SHW))
    wa_send, wa_recv, wa_src, wa_land, wa_token = _split_start(
        "reduce_w_in_sibling_start", 1, _sibling_half_copies(1), [gw_in.reshape(NCHIP, 2, D // 2, SHW)],
        [_sds((NCHIP, D // 2, SHW), f32)], dproj)
    dxn =_mm_nt_groups("dxn", dproj, pl.BlockSpec((tm, SHW), lambda i, g: (i, g)), w_in_g, S, tm)
    wa_src, wa_land = _split_wait("reduce_w_in_sibling_wait", _sibling_half_copies(1), wa_send, wa_recv, wa_src, wa_land, dxn)
    w_in_pair = _pair_sum("pair_sum_w_in", c_arr, wa_src[0], wa_land[0], 256)
    wb_send, wb_recv, wb_src, wb_land, wb_token = _split_start(
        "reduce_w_in_chips_start", 3, _chip_part_copies(1), [w_in_pair[1]], [_sds((3, D // 2, SHW), bf16)], w_in_pair[0])
    grad_x, dg1 = _rms_bwd("rms_mix_bwd", xs, norm_mix_g + wb_token[0:1, 0:1], dxn, dx1, 512)
    _, wb_land = _split_wait("reduce_w_in_chips_wait", _chip_part_copies(1), wb_send, wb_recv, wb_src, wb_land, dg1)
    w_in_half = _chip_sum("chip_sum_w_in", chip_arr, w_in_pair[0], wb_land[0], 256)

    d_wa = jnp.stack([_diag_blocks(dwbd[:, :, 0:CW]), _diag_blocks(dwbd[:, :, 2 * CW:3 * CW])])
    d_wx = jnp.stack([_diag_blocks(dwbd[:, :, CW:2 * CW]), _diag_blocks(dwbd[:, :, 3 * CW:4 * CW])])
    small_full = [dg1, db_gate, dcw, dcb, dlam, d_wa, dba, d_wx, dbx, dsink[:, 0], dg2, dg3,
                  loss_row[0, 0:1]]
    full_shapes = [(1, D), (1, 2 * D), (4, D), (1, D), (2, D), (2, NH, HD, HD), (2, D), (2, NH, HD, HD), (2, D), (NH,),
                   (1, D), (1, D), (1,)]
    rows_full = _rows_for([math.prod(s) for s in full_shapes], 16)
    small_vec = _pack(small_full, rows_full)

    late_names, late_tiles = ["w_in", "w_out"], [256, 128]
    big = [gw_out.reshape(NCHIP, 2, D // NCHIP // 2, D)]
    *recv_a, small_sib = _sibling_halves(big, small_vec)
    w_out_pair = _pair_sum("pair_sum_w_out", c_arr, big[0], recv_a[0], 128)
    small_chip = _add2("pair_sum_small", small_vec, small_sib).reshape(2, rows_full // 2, 128)
    *recv_b, small_all = _exchange_chips([w_out_pair[1]], small_chip)
    w_out_half = _chip_sum("chip_sum_w_out", chip_arr, w_out_pair[0], recv_b[0], 128)
    halves = [w_in_half, w_out_half, _sum4("chip_sum_small", small_all, rows_full // 2)]
    *recv_c, small_other = _share_sibling(halves)
    small_lo = jnp.where(cc == 0, halves[2], small_other)
    small_hi = jnp.where(cc == 0, small_other, halves[2])
    g_full = _unpack(jnp.concatenate([small_lo, small_hi], axis=0), full_shapes)

    out_big = {}
    for nm, w, g_own, g_recv, m, v, th in zip(late_names + early_names, [w_in, w_out, w_ffn_in, w_ffn_out],
                                              halves[:2] + early_halves, recv_c + early_other,
                                              [m_w_in, m_w_out, m_w_ffn_in, m_w_ffn_out],
                                              [v_w_in, v_w_out, v_w_ffn_in, v_w_ffn_out], late_tiles + early_tiles):
        g_, d_, m_, v_ = _adamw_halves("adamw_" + nm, c_arr, w[0], g_own, g_recv, m[0], v[0], th)
        out_big[nm] = (g_[None], d_[None], m_[None], v_[None])

    small_names = ["norm_mix_g", "b_gate", "conv_w", "conv_b", "lru_lambda", "lru_wa", "lru_ba", "lru_wx", "lru_bx", "attn_sink",
                   "norm_ffn_g", "norm_final_g"]
    sharded = {"conv_w", "lru_lambda", "lru_ba", "lru_bx"}
    small_w = [norm_mix_g, b_gate, conv_w, conv_b, lru_lambda, lru_wa, lru_ba, lru_wx, lru_bx, attn_sink, norm_ffn_g, norm_final_g]
    small_m = [m_norm_mix_g, m_b_gate, m_conv_w, m_conv_b, m_lru_lambda, m_lru_wa, m_lru_ba, m_lru_wx, m_lru_bx, m_attn_sink,
               m_norm_ffn_g, m_norm_final_g]
    small_v = [v_norm_mix_g, v_b_gate, v_conv_w, v_conv_b, v_lru_lambda, v_lru_wa, v_lru_ba, v_lru_wx, v_lru_bx, v_attn_sink,
               v_norm_ffn_g, v_norm_final_g]
    g_local = []
    for nm, g, w in zip(small_names, g_full, small_w):
        if nm in sharded:
            g = lax.dynamic_slice_in_dim(g, chip * SW, SW, axis=1)
        g_local.append(g.reshape(w.shape))
    local_shapes = [w.shape for w in small_w]
    rows_local = _rows_for([math.prod(s) for s in local_shapes], 8)
    d_s, m_s, v_s = _adamw("adamw_small", _pack(small_w, rows_local), _pack(g_local, rows_local),
                           _pack(small_m, rows_local), _pack(small_v, rows_local), rows_local)
    d_l, m_l, v_l = _unpack(d_s, local_shapes), _unpack(m_s, local_shapes), _unpack(v_s, local_shapes)
    res = {nm: (g_local[i], d_l[i], m_l[i], v_l[i]) for i, nm in enumerate(small_names)}
    res.update(out_big)

    order = ["norm_mix_g", "w_in", "b_gate", "conv_w", "conv_b", "lru_lambda", "lru_wa", "lru_ba", "lru_wx", "lru_bx", "attn_sink",
             "w_out", "norm_ffn_g", "w_ffn_in", "w_ffn_out", "norm_final_g"]
    outs = [g_full[-1][0], grad_x[None]]
    for k in range(4):
        outs += [res[nm][k] for nm in order]
    return tuple(outs)
```

```python
import functools
import math

import jax
import jax.numpy as jnp
from jax import lax
from jax.experimental import pallas as pl
from jax.experimental.pallas import tpu as pltpu

f32 = jnp.float32
bf16 = jnp.bfloat16

D = 1024
NH = 16
HD = 64
FF = 2816
INW = 5632
NCHIP = 4
SHW = INW // NCHIP
CW = 128
NCH = D // CW
BLK = 128
EPS = 1e-6
NEG_INF = -1e30
RGLRU_C = 8.0
ADAM_LR, ADAM_B1, ADAM_B2, ADAM_EPS, ADAM_WD, ADAM_STEP = 0.001, 0.9, 0.999, 1e-08, 0.01, 10
VMEM_LIMIT = 58 * 1024 * 1024
MESH = pl.DeviceIdType.MESH
ANY = pl.BlockSpec(memory_space=pl.ANY)

COL_U, COL_G, COL_Q, COL_K, COL_V, COL_Z0, COL_Z1 = 0, 4, 8, 12, 13, 14, 18
MERGE_W = 512
MERGE_Z0, MERGE_Z1 = (COL_Z0 * 256) // MERGE_W, (COL_Z1 * 256) // MERGE_W


def _params(n_axes, vmem=False):
    return pltpu.CompilerParams(dimension_semantics=("arbitrary",) * n_axes,
                                vmem_limit_bytes=VMEM_LIMIT if vmem else None)


def _sds(shape, dtype):
    return jax.ShapeDtypeStruct(tuple(shape), dtype)


_DIMS = {"nn": (((1,), (0,)), ((), ())), "nt": (((1,), (1,)), ((), ())), "tn": (((0,), (0,)), ((), ()))}


def _mm(name, mode, a, a_spec, b, b_spec, out_shape, out_spec, grid, nk, acc_shape):
    def body(*refs):
        a_ref, b_ref, o_ref = refs[0], refs[1], refs[2]
        part = lax.dot_general(a_ref[...].astype(bf16), b_ref[...].astype(bf16), _DIMS[mode],
                               preferred_element_type=f32)
        if nk == 1:
            o_ref[...] = part.astype(o_ref.dtype)
            return
        acc_ref = refs[3]
        k = pl.program_id(len(grid) - 1)

        @pl.when(k == 0)
        def _():
            acc_ref[...] = part

        @pl.when(k > 0)
        def _():
            acc_ref[...] += part

        @pl.when(k == nk - 1)
        def _():
            o_ref[...] = acc_ref[...].astype(o_ref.dtype)

    scratch = [pltpu.VMEM(acc_shape, f32)] if nk > 1 else []
    return pl.pallas_call(body, name=name, grid=grid, in_specs=[a_spec, b_spec], out_specs=out_spec, out_shape=out_shape,
                          scratch_shapes=scratch, compiler_params=_params(len(grid), True))(a, b)


def _rms_matmul(name, x, g, w3, tm):
    S, K = x.shape
    G, _, Nw = w3.shape
    tm = min(tm, S)

    def body(x_ref, g_ref, w_ref, xn_ref, o_ref, xs_ref):
        @pl.when(pl.program_id(1) == 0)
        def _():
            xf = x_ref[...]
            r = lax.rsqrt(jnp.mean(xf * xf, axis=-1, keepdims=True) + EPS)
            xn = ((xf * r) * g_ref[...]).astype(bf16)
            xs_ref[...] = xn
            xn_ref[...] = xn

        o_ref[...] = jnp.dot(xs_ref[...], w_ref[...], preferred_element_type=f32).astype(bf16)

    return pl.pallas_call(
        body, name=name, grid=(S // tm, G),
        in_specs=[pl.BlockSpec((tm, K), lambda i, j: (i, 0)), pl.BlockSpec((1, K), lambda i, j: (0, 0)),
                  pl.BlockSpec((None, K, Nw), lambda i, j: (j, 0, 0))],
        out_specs=[pl.BlockSpec((tm, K), lambda i, j: (i, 0)), pl.BlockSpec((tm, Nw), lambda i, j: (i, j))],
        out_shape=[_sds((S, K), bf16), _sds((S, G * Nw), bf16)],
        scratch_shapes=[pltpu.VMEM((tm, K), bf16)], compiler_params=_params(2, True))(x, g, w3)


def _rms_matmul_swiglu(name, x, g, w3, tm):
    S, K = x.shape
    G, _, Nw = w3.shape
    tm = min(tm, S)
    half = G // 2

    def body(x_ref, g_ref, wg_ref, wu_ref, xn_ref, gu_ref, act_ref, xs_ref):
        @pl.when(pl.program_id(1) == 0)
        def _():
            xf = x_ref[...]
            r = lax.rsqrt(jnp.mean(xf * xf, axis=-1, keepdims=True) + EPS)
            xn = ((xf * r) * g_ref[...]).astype(bf16)
            xs_ref[...] = xn
            xn_ref[...] = xn

        xn = xs_ref[...]
        gate = jnp.dot(xn, wg_ref[...], preferred_element_type=f32)
        up = jnp.dot(xn, wu_ref[...], preferred_element_type=f32)
        gu_ref[0] = gate.astype(bf16)
        gu_ref[1] = up.astype(bf16)
        act_ref[...] = ((gate * _sigmoid(gate)) * up).astype(bf16)

    return pl.pallas_call(
        body, name=name, grid=(S // tm, half),
        in_specs=[pl.BlockSpec((tm, K), lambda i, j: (i, 0)), pl.BlockSpec((1, K), lambda i, j: (0, 0)),
                  pl.BlockSpec((None, K, Nw), lambda i, j: (j, 0, 0)),
                  pl.BlockSpec((None, K, Nw), lambda i, j: (half + j, 0, 0))],
        out_specs=[pl.BlockSpec((tm, K), lambda i, j: (i, 0)), pl.BlockSpec((2, tm, Nw), lambda i, j: (0, i, j)),
                   pl.BlockSpec((tm, Nw), lambda i, j: (i, j))],
        out_shape=[_sds((S, K), bf16), _sds((2, S, half * Nw), bf16), _sds((S, half * Nw), bf16)],
        scratch_shapes=[pltpu.VMEM((tm, K), bf16)], compiler_params=_params(2, True))(x, g, w3, w3)


def _mm_nt_groups(name, a, a_spec, w3, S, tm):
    G, Dout, Kw = w3.shape
    return _mm(name, "nt", a, a_spec, w3, pl.BlockSpec((None, Dout, Kw), lambda i, g: (g, 0, 0)),
               _sds((S, Dout), f32), pl.BlockSpec((tm, Dout), lambda i, g: (i, 0)), (S // tm, G), G, (tm, Dout))


def _mm_tn(name, a, a_spec, b, b_spec, out_shape, out_spec, grid, acc_shape):
    return _mm(name, "tn", a, a_spec, b, b_spec, out_shape, out_spec, grid, grid[-1], acc_shape)


def _sigmoid(x):
    return 0.5 * jnp.tanh(0.5 * x) + 0.5


_GELU_C = math.sqrt(2.0 / math.pi)


def _gelu_and_grad(x):
    v = _GELU_C * (x + 0.044715 * (x * x * x))
    t = jnp.tanh(v)
    gl = 0.5 * x * (1.0 + t)
    dgl = 0.5 * (1.0 + t) + 0.5 * x * (1.0 - t * t) * (_GELU_C * (1.0 + 3.0 * 0.044715 * (x * x)))
    return gl, dgl


def _one_minus_exp2x(x, ex):
    y = 2.0 * x
    series = y * (1.0 + y * (0.5 + y * (1.0 / 6.0 + y * (1.0 / 24.0))))
    return jnp.where(y > -1.0 / 64.0, -series, 1.0 - ex * ex)


def _z_specs(tm):
    return [pl.BlockSpec((tm, MERGE_W), lambda i, p=p: (i, MERGE_Z0 + p)) for p in range(2 * D // MERGE_W)]


def _merge_out_proj(proj, b_gate, y_a, y_b, w, res, tm):
    S = proj.shape[0]
    tm = min(tm, S)
    per = D // MERGE_W
    nz = 2 * per

    def body(*refs):
        z = refs[:nz]
        b_ref, ya_ref, yb_ref, w_ref, r_ref, m_ref, x_ref = refs[nz:]
        for p in range(per):
            cols = slice(p * MERGE_W, (p + 1) * MERGE_W)
            g0 = _sigmoid(z[p][...].astype(f32) + b_ref[:, p * MERGE_W:(p + 1) * MERGE_W])
            g1 = _sigmoid(z[per + p][...].astype(f32) + b_ref[:, D + p * MERGE_W:D + (p + 1) * MERGE_W])
            m_ref[:, cols] = (g0 * ya_ref[:, cols].astype(f32) + g1 * yb_ref[:, cols].astype(f32)).astype(bf16)
        x_ref[...] = r_ref[...] + jnp.dot(m_ref[...], w_ref[...], preferred_element_type=f32)

    row = pl.BlockSpec((tm, D), lambda i: (i, 0))
    return pl.pallas_call(
        body, name="merge_out_proj", grid=(S // tm,),
        in_specs=_z_specs(tm) + [pl.BlockSpec((1, 2 * D), lambda i: (0, 0)), row, row, pl.BlockSpec((D, D), lambda i: (0, 0)), row],
        out_specs=[row, row], out_shape=[_sds((S, D), bf16), _sds((S, D), f32)],
        compiler_params=_params(1, True))(*([proj] * nz), b_gate, y_a, y_b, w, res)


def _merge_bwd(proj, b_gate, y_a, y_b, dx, w, tm):
    S = proj.shape[0]
    tm = min(tm, S)
    per = D // MERGE_W
    nz = 2 * per
    nsteps = S // tm
    z_col = MERGE_Z0 * MERGE_W

    def body(*refs):
        z = refs[:nz]
        b_ref, ya_ref, yb_ref, dx_ref, w_ref, dproj_ref, dy_ref, db_ref, dz_buf, sems = refs[nz:]
        i = pl.program_id(0)
        slot = i % 2

        def dz_copy(step):
            rows = pl.ds(pl.multiple_of(step * tm, tm), tm)
            return pltpu.make_async_copy(dz_buf.at[step % 2], dproj_ref.at[rows, pl.ds(z_col, 2 * D)], sems.at[step % 2])

        @pl.when(i >= 2)
        def _():
            dz_copy(i - 2).wait()

        @pl.when(i == 0)
        def _():
            db_ref[...] = jnp.zeros_like(db_ref)

        dm = lax.dot_general(dx_ref[...].astype(bf16), w_ref[...], _DIMS["nt"], preferred_element_type=f32)
        for p in range(nz):
            branch, cols = p // per, slice((p % per) * MERGE_W, (p % per + 1) * MERGE_W)
            zc = slice(p * MERGE_W, (p + 1) * MERGE_W)
            g = _sigmoid(z[p][...].astype(f32) + b_ref[:, zc])
            d = dm[:, cols]
            y = (ya_ref if branch == 0 else yb_ref)[:, cols].astype(f32)
            dz = (d * y) * (g * (1.0 - g))
            dz_buf[slot, :, zc] = dz.astype(bf16)
            dy_ref[branch, :, cols] = (d * g).astype(bf16)
            db_ref[:, zc] += jnp.sum(dz, axis=0, keepdims=True)
        dz_copy(i).start()

        @pl.when(i == nsteps - 1)
        def _():
            if nsteps >= 2:
                dz_copy(i - 1).wait()
            dz_copy(i).wait()

    row = pl.BlockSpec((tm, D), lambda i: (i, 0))
    return pl.pallas_call(
        body, name="merge_bwd", grid=(nsteps,),
        in_specs=_z_specs(tm) + [pl.BlockSpec((1, 2 * D), lambda i: (0, 0)), row, row, row, pl.BlockSpec((D, D), lambda i: (0, 0))],
        out_specs=[ANY, pl.BlockSpec((2, tm, D), lambda i: (0, i, 0)), pl.BlockSpec((1, 2 * D), lambda i: (0, 0))],
        out_shape=[_sds((S, INW), bf16), _sds((2, S, D), bf16), _sds((1, 2 * D), f32)],
        scratch_shapes=[pltpu.VMEM((2, tm, 2 * D), bf16), pltpu.SemaphoreType.DMA((2,))],
        compiler_params=_params(1, True))(*([proj] * nz), b_gate, y_a, y_b, dx, w)


def _swiglu_bwd(dx, w, gu, tm):
    S, K = dx.shape
    tm = min(tm, S)

    def body(dx_ref, w_ref, gu_ref, o_ref):
        d = lax.dot_general(dx_ref[...].astype(bf16), w_ref[...], _DIMS["nt"], preferred_element_type=f32)
        g = gu_ref[0].astype(f32)
        u = gu_ref[1].astype(f32)
        s = _sigmoid(g)
        o_ref[0] = ((d * u) * (s * (1.0 + g * (1.0 - s)))).astype(bf16)
        o_ref[1] = (d * (g * s)).astype(bf16)

    stacked = pl.BlockSpec((2, tm, FF), lambda i: (0, i, 0))
    return pl.pallas_call(body, name="swiglu_bwd", grid=(S // tm,),
                          in_specs=[pl.BlockSpec((tm, K), lambda i: (i, 0)), pl.BlockSpec((FF, K), lambda i: (0, 0)), stacked],
                          out_specs=stacked, out_shape=_sds((2, S, FF), bf16),
                          compiler_params=_params(1, True))(dx, w, gu)


def _ffn_out_loss_bwd(act, w, x1, g3, tgt, tm):
    S, K = act.shape
    tm = min(tm, S)

    def body(a_ref, w_ref, r_ref, g_ref, t_ref, dx_ref, loss_ref, dg_ref):
        @pl.when(pl.program_id(0) == 0)
        def _():
            loss_ref[...] = jnp.zeros_like(loss_ref)
            dg_ref[...] = jnp.zeros_like(dg_ref)

        x = r_ref[...] + jnp.dot(a_ref[...], w_ref[...], preferred_element_type=f32)
        g = g_ref[...]
        r = lax.rsqrt(jnp.mean(x * x, axis=-1, keepdims=True) + EPS)
        xh = x * r
        err = xh * g - t_ref[...]
        row = jnp.mean(err * err, axis=-1, keepdims=True)
        loss_ref[...] += 0.5 * jnp.sum(row, axis=0, keepdims=True)
        dy = err * (1.0 / D)
        dg_ref[...] += jnp.sum(dy * xh, axis=0, keepdims=True)
        dxh = dy * g
        dx_ref[...] = r * (dxh - xh * jnp.mean(dxh * xh, axis=-1, keepdims=True))

    row_blk = pl.BlockSpec((tm, D), lambda i: (i, 0))
    vec = pl.BlockSpec((1, D), lambda i: (0, 0))
    return pl.pallas_call(body, name="ffn_out_loss_bwd", grid=(S // tm,),
                          in_specs=[pl.BlockSpec((tm, K), lambda i: (i, 0)), pl.BlockSpec((K, D), lambda i: (0, 0)),
                                    row_blk, vec, row_blk],
                          out_specs=[row_blk, pl.BlockSpec((1, 128), lambda i: (0, 0)), vec],
                          out_shape=[_sds((S, D), f32), _sds((1, 128), f32), _sds((1, D), f32)],
                          compiler_params=_params(1, True))(act, w, x1, g3, tgt)


def _rms_bwd(name, x, g, dxn, dres, tm):
    S = x.shape[0]
    tm = min(tm, S)

    def body(x_ref, g_ref, d_ref, r_ref, dx_ref, dg_ref):
        @pl.when(pl.program_id(0) == 0)
        def _():
            dg_ref[...] = jnp.zeros_like(dg_ref)

        x = x_ref[...]
        d = d_ref[...]
        r = lax.rsqrt(jnp.mean(x * x, axis=-1, keepdims=True) + EPS)
        xh = x * r
        dg_ref[...] += jnp.sum(d * xh, axis=0, keepdims=True)
        dxh = d * g_ref[...]
        dx_ref[...] = r_ref[...] + r * (dxh - xh * jnp.mean(dxh * xh, axis=-1, keepdims=True))

    row_blk = pl.BlockSpec((tm, D), lambda i: (i, 0))
    vec = pl.BlockSpec((1, D), lambda i: (0, 0))
    return pl.pallas_call(body, name=name, grid=(S // tm,), in_specs=[row_blk, vec, row_blk, row_blk],
                          out_specs=[row_blk, vec], out_shape=[_sds((S, D), f32), _sds((1, D), f32)],
                          compiler_params=_params(1))(x, g, dxn, dres)


def _mm_nt_rms_bwd(name, a, a_spec, w3, x, g, dres, tm):
    S = x.shape[0]
    G, Dout, Kw = w3.shape

    def body(a_ref, w_ref, x_ref, g_ref, r_ref, dx_ref, dg_ref, acc_ref):
        i, k = pl.program_id(0), pl.program_id(1)
        part = lax.dot_general(a_ref[...].astype(bf16), w_ref[...], _DIMS["nt"], preferred_element_type=f32)

        @pl.when(k == 0)
        def _():
            acc_ref[...] = part

        @pl.when(k > 0)
        def _():
            acc_ref[...] += part

        @pl.when(k == G - 1)
        def _():
            @pl.when(i == 0)
            def _():
                dg_ref[...] = jnp.zeros_like(dg_ref)

            for rows in (slice(0, tm // 2), slice(tm // 2, tm)):
                x_t = x_ref[rows, :]
                d = acc_ref[rows, :]
                r = lax.rsqrt(jnp.mean(x_t * x_t, axis=-1, keepdims=True) + EPS)
                xh = x_t * r
                dg_ref[...] += jnp.sum(d * xh, axis=0, keepdims=True)
                dxh = d * g_ref[...]
                dx_ref[rows, :] = r_ref[rows, :] + r * (dxh - xh * jnp.mean(dxh * xh, axis=-1, keepdims=True))

    row_blk = pl.BlockSpec((tm, Dout), lambda i, k: (i, 0))
    vec = pl.BlockSpec((1, Dout), lambda i, k: (0, 0))
    return pl.pallas_call(body, name=name, grid=(S // tm, G),
                          in_specs=[a_spec, pl.BlockSpec((None, Dout, Kw), lambda i, k: (k, 0, 0)), row_blk, vec, row_blk],
                          out_specs=[row_blk, vec], out_shape=[_sds((S, Dout), f32), _sds((1, Dout), f32)],
                          scratch_shapes=[pltpu.VMEM((tm, Dout), f32)], compiler_params=_params(2, True))(a, w3, x, g, dres)


LRU_TT = 256
SCAN_UNROLL = 8


HALO = 16


def _halo(ref, i, S):
    nt = S // LRU_TT
    t0 = pl.multiple_of(i * LRU_TT, LRU_TT)
    p0 = pl.multiple_of(jnp.maximum(t0 - HALO, 0), HALO)
    n0 = pl.multiple_of(jnp.minimum(t0 + LRU_TT, S - HALO), HALO)
    prev = jnp.where(i > 0, ref[pl.ds(p0, HALO), :].astype(f32), 0.0)
    nxt = jnp.where(i < nt - 1, ref[pl.ds(n0, HALO), :].astype(f32), 0.0)
    return jnp.concatenate([prev, ref[pl.ds(t0, LRU_TT), :].astype(f32), nxt], axis=0)


def _shift(ext, k):
    n = LRU_TT + 2 * HALO
    return pltpu.roll(ext, (-k) % n, 0)[HALO:HALO + LRU_TT]


def _lru_gates(uc, wbd, ba, bx):
    pre = jnp.dot(uc.astype(bf16), wbd, preferred_element_type=f32)
    r_f = _sigmoid(pre[:, 0:CW] + ba[0:1])
    i_f = _sigmoid(pre[:, CW:2 * CW] + bx[0:1])
    r_b = _sigmoid(pre[:, 2 * CW:3 * CW] + ba[1:2])
    i_b = _sigmoid(pre[:, 3 * CW:4 * CW] + bx[1:2])
    return r_f, i_f, r_b, i_b


def _lru_coeffs(r, sp):
    log_a = (-RGLRU_C * r) * sp
    a = jnp.exp(log_a)
    beta = jnp.sqrt(jnp.maximum(_one_minus_exp2x(log_a, a), 0.0))
    return a, beta


def _lru_coeffs_inv(r, sp):
    log_a = (-RGLRU_C * r) * sp
    a = jnp.exp(log_a)
    om = jnp.maximum(_one_minus_exp2x(log_a, a), 0.0)
    return a, jnp.sqrt(om), lax.rsqrt(jnp.maximum(om, 1e-30))


def _conv_tile(u_ref, i, S, cw, cb):
    ext = _halo(u_ref, i, S)
    um2, um1, u0, up1 = _shift(ext, -2), _shift(ext, -1), ext[HALO:HALO + LRU_TT], _shift(ext, 1)
    uc = um2 * cw[0:1] + um1 * cw[1:2] + u0 * cw[2:3] + up1 * cw[3:4] + cb
    return uc, (um2, um1, u0, up1)


def _scan_pair(S, fwd_a, fwd_b, fwd_out, rev_a, rev_b, rev_out):
    ng = S // 8
    idx = lax.broadcasted_iota(jnp.int32, (8, CW), 0)

    def local(a, b, rev):
        for sh in (1, 2, 4):
            if rev:
                keep = idx < 8 - sh
                amt = 8 - sh
            else:
                keep = idx >= sh
                amt = sh
            a_s = jnp.where(keep, pltpu.roll(a, amt, 0), 1.0)
            b_s = jnp.where(keep, pltpu.roll(b, amt, 0), 0.0)
            b = a * b_s + b
            a = a * a_s
        return a, b

    def step(it, carry):
        cf, cr = carry
        fwd_rows = [pl.multiple_of((it * SCAN_UNROLL + j) * 8, 8) for j in range(SCAN_UNROLL)]
        rev_rows = [pl.multiple_of((ng - 1 - (it * SCAN_UNROLL + j)) * 8, 8) for j in range(SCAN_UNROLL)]
        fwd_loc = [local(fwd_a(r), fwd_b(r), False) for r in fwd_rows]
        rev_loc = [local(rev_a(r), rev_b(r), True) for r in rev_rows]
        for j in range(SCAN_UNROLL):
            a, b = fwd_loc[j]
            h = a * cf + b
            fwd_out[pl.ds(fwd_rows[j], 8), :] = h
            cf = jnp.broadcast_to(h[7:8, :], (8, CW))
            a, b = rev_loc[j]
            h = a * cr + b
            rev_out[pl.ds(rev_rows[j], 8), :] = h
            cr = jnp.broadcast_to(h[0:1, :], (8, CW))
        return cf, cr

    zero = jnp.zeros((8, CW), f32)
    lax.fori_loop(0, ng // SCAN_UNROLL, step, (zero, zero))


def _lru_specs(S):
    seq = lambda off: pl.BlockSpec((S, CW), lambda j: (0, off + j))
    par = lambda rows: pl.BlockSpec((rows, CW), lambda j: (0, j))
    return seq, par


def _lru_fwd(proj, conv_w, conv_b, lam, ba, bx, wbd):
    S = proj.shape[0]
    nt = S // LRU_TT

    def body(u_ref, g_ref, cw_ref, cb_ref, lam_ref, ba_ref, bx_ref, wbd_ref, y_ref, state_ref, af_ref, bf_ref, ab_ref, bb_ref,
             sems):
        cw, cb, ba_v, bx_v, wbd_v = cw_ref[...], cb_ref[...], ba_ref[...], bx_ref[...], wbd_ref[...]
        sp = jax.nn.softplus(-lam_ref[...])
        cols = pl.ds(pl.multiple_of(pl.program_id(0) * CW, CW), CW)
        save = [pltpu.make_async_copy(ref, state_ref.at[k, :, cols], sems.at[k])
                for k, ref in enumerate((af_ref, bf_ref, ab_ref, bb_ref))]

        def phase1(i, c):
            uc, _ = _conv_tile(u_ref, i, S, cw, cb)
            r_f, i_f, r_b, i_b = _lru_gates(uc, wbd_v, ba_v, bx_v)
            rows = pl.ds(pl.multiple_of(i * LRU_TT, LRU_TT), LRU_TT)
            a, beta = _lru_coeffs(r_f, sp[0:1])
            af_ref[rows, :] = a
            bf_ref[rows, :] = beta * (i_f * uc)
            a, beta = _lru_coeffs(r_b, sp[1:2])
            ab_ref[rows, :] = a
            bb_ref[rows, :] = beta * (i_b * uc)
            return c

        lax.fori_loop(0, nt, phase1, 0)
        save[0].start()
        save[2].start()
        row8 = lambda ref: (lambda r0: ref[pl.ds(r0, 8), :])
        _scan_pair(S, row8(af_ref), row8(bf_ref), bf_ref, row8(ab_ref), row8(bb_ref), bb_ref)
        save[1].start()
        save[3].start()

        def phase3(i, c):
            rows = pl.ds(pl.multiple_of(i * LRU_TT, LRU_TT), LRU_TT)
            y = (bf_ref[rows, :] + bb_ref[rows, :]) * jax.nn.gelu(g_ref[rows, :].astype(f32))
            y_ref[rows, :] = y.astype(y_ref.dtype)
            return c

        lax.fori_loop(0, nt, phase3, 0)
        for cp in save:
            cp.wait()

    seq, par = _lru_specs(S)
    return pl.pallas_call(
        body, name="lru_fwd", grid=(NCH,),
        in_specs=[seq(0), seq(NCH), par(4), par(1), par(2), par(2), par(2),
                  pl.BlockSpec((None, CW, 4 * CW), lambda j: (j, 0, 0))],
        out_specs=[seq(0), ANY], out_shape=[_sds((S, D), bf16), _sds((4, S, D), f32)],
        scratch_shapes=[pltpu.VMEM((S, CW), f32)] * 4 + [pltpu.SemaphoreType.DMA((4,))], compiler_params=_params(1, True),
    )(proj, proj, conv_w, conv_b, lam, ba, bx, wbd)


def _lru_bwd(proj, dy, state, dproj, conv_w, conv_b, lam, ba, bx, wbd):
    S = proj.shape[0]
    nt = S // LRU_TT

    def body(u_ref, g_ref, dy_ref, state_ref, dproj_in, cw_ref, cb_ref, lam_ref, ba_ref, bx_ref, wbd_ref,
             dproj_ref, dcw_ref, dcb_ref, dlam_ref, dba_ref, dbx_ref, dwbd_ref,
             af_ref, bf_ref, ab_ref, bb_ref, dh_ref, du_ref, dg_ref, sems):
        cw, cb, ba_v, bx_v, wbd_v = cw_ref[...], cb_ref[...], ba_ref[...], bx_ref[...], wbd_ref[...]
        lam_v = lam_ref[...]
        sp = jax.nn.softplus(-lam_v)
        chunk = pl.program_id(0)

        def out_copies(j):
            c0 = pl.multiple_of(j * CW, CW)
            return [pltpu.make_async_copy(du_ref, dproj_ref.at[:, pl.ds(c0, CW)], sems.at[4]),
                    pltpu.make_async_copy(dg_ref, dproj_ref.at[:, pl.ds(D + c0, CW)], sems.at[5])]

        @pl.when(chunk >= 1)
        def _():
            for cp in out_copies(chunk - 1):
                cp.wait()

        cols = pl.ds(pl.multiple_of(chunk * CW, CW), CW)
        load = [pltpu.make_async_copy(state_ref.at[k, :, cols], ref, sems.at[k])
                for k, ref in enumerate((af_ref, bf_ref, ab_ref, bb_ref))]
        for cp in (load[1], load[3], load[0], load[2]):
            cp.start()
        load[1].wait()
        load[3].wait()
        row8 = lambda ref: (lambda r0: ref[pl.ds(r0, 8), :])

        def phase0(i, c):
            rows = pl.ds(pl.multiple_of(i * LRU_TT, LRU_TT), LRU_TT)
            gl, dgl = _gelu_and_grad(g_ref[rows, :].astype(f32))
            dyt = dy_ref[rows, :].astype(f32)
            dh_ref[rows, :] = dyt * gl
            dg_ref[rows, :] = ((dyt * (bf_ref[rows, :] + bb_ref[rows, :])) * dgl).astype(dg_ref.dtype)
            return c

        lax.fori_loop(0, nt, phase0, 0)
        load[0].wait()
        load[2].wait()

        def scaled_dh(a_ref):
            def f(r0):
                return a_ref[pl.ds(r0, 8), :] * dh_ref[pl.ds(r0, 8), :]
            return f

        _scan_pair(S, row8(ab_ref), scaled_dh(ab_ref), ab_ref, row8(af_ref), scaled_dh(af_ref), af_ref)

        dcw_ref[...] = jnp.zeros_like(dcw_ref)
        dcb_ref[...] = jnp.zeros_like(dcb_ref)
        dlam_ref[...] = jnp.zeros_like(dlam_ref)
        dba_ref[...] = jnp.zeros_like(dba_ref)
        dbx_ref[...] = jnp.zeros_like(dbx_ref)
        dwbd_ref[...] = jnp.zeros_like(dwbd_ref)

        def direction(uc, r, i_g, dht, h_nb, sp_d):
            a, beta, inv_beta = _lru_coeffs_inv(r, sp_d)
            da = dht * h_nb
            dbeta = dht * (i_g * uc)
            d_iu = dht * beta
            dlog_a = da * a - (a * a) * (dbeta * inv_beta)
            dlr = dlog_a * r
            dsp = -RGLRU_C * jnp.sum(dlr, axis=0, keepdims=True)
            dpre_r = (dlr * (1.0 - r)) * (-RGLRU_C * sp_d)
            dpre_i = (d_iu * uc) * (i_g * (1.0 - i_g))
            return dpre_r, dpre_i, d_iu * i_g, dsp

        def phase4(i, c):
            uc, (um2, um1, u0, up1) = _conv_tile(u_ref, i, S, cw, cb)
            r_f, i_f, r_b, i_b = _lru_gates(uc, wbd_v, ba_v, bx_v)
            rows = pl.ds(pl.multiple_of(i * LRU_TT, LRU_TT), LRU_TT)
            dh = dh_ref[rows, :]
            dht_f = dh + _shift(_halo(af_ref, i, S), 1)
            h_prev = _shift(_halo(bf_ref, i, S), -1)
            dht_b = dh + _shift(_halo(ab_ref, i, S), -1)
            h_next = _shift(_halo(bb_ref, i, S), 1)
            prf, pif, duc_f, dsp_f = direction(uc, r_f, i_f, dht_f, h_prev, sp[0:1])
            prb, pib, duc_b, dsp_b = direction(uc, r_b, i_b, dht_b, h_next, sp[1:2])
            dpre = jnp.concatenate([prf, pif, prb, pib], axis=1)
            dpre_b = dpre.astype(bf16)
            duc = (duc_f + duc_b) + lax.dot_general(dpre_b, wbd_v, _DIMS["nt"], preferred_element_type=f32)
            dwbd_ref[...] += lax.dot_general(uc.astype(bf16), dpre_b, _DIMS["tn"], preferred_element_type=f32)
            colsum = lambda v: jnp.sum(v, axis=0, keepdims=True)
            dba_ref[...] += jnp.concatenate([colsum(prf), colsum(prb)], axis=0)
            dbx_ref[...] += jnp.concatenate([colsum(pif), colsum(pib)], axis=0)
            dlam_ref[...] += jnp.concatenate([dsp_f, dsp_b], axis=0)
            dcb_ref[...] += colsum(duc)
            dcw_ref[...] += jnp.concatenate([colsum(duc * um2), colsum(duc * um1), colsum(duc * u0),
                                             colsum(duc * up1)], axis=0)
            af_ref[rows, :] = duc
            return c

        lax.fori_loop(0, nt, phase4, 0)
        dlam_ref[...] = dlam_ref[...] * (-_sigmoid(-lam_v))

        def phase5(i, c):
            ext = _halo(af_ref, i, S)
            rows = pl.ds(pl.multiple_of(i * LRU_TT, LRU_TT), LRU_TT)
            du = (_shift(ext, 2) * cw[0:1] + _shift(ext, 1) * cw[1:2] + ext[HALO:HALO + LRU_TT] * cw[2:3]
                  + _shift(ext, -1) * cw[3:4])
            du_ref[rows, :] = du.astype(du_ref.dtype)
            return c

        lax.fori_loop(0, nt, phase5, 0)
        for cp in out_copies(chunk):
            cp.start()

        @pl.when(chunk == NCH - 1)
        def _():
            for cp in out_copies(chunk):
                cp.wait()

    seq, par = _lru_specs(S)
    return pl.pallas_call(
        body, name="lru_bwd", grid=(NCH,),
        in_specs=[seq(0), seq(NCH), pl.BlockSpec((None, S, CW), lambda j: (0, 0, j)), ANY, ANY,
                  par(4), par(1), par(2), par(2), par(2), pl.BlockSpec((None, CW, 4 * CW), lambda j: (j, 0, 0))],
        out_specs=[ANY, par(4), par(1), par(2), par(2), par(2),
                   pl.BlockSpec((None, CW, 4 * CW), lambda j: (j, 0, 0))],
        out_shape=[_sds(dproj.shape, bf16), _sds((4, D), f32), _sds((1, D), f32), _sds((2, D), f32),
                   _sds((2, D), f32), _sds((2, D), f32), _sds((NCH, CW, 4 * CW), f32)],
        scratch_shapes=[pltpu.VMEM((S, CW), f32)] * 5 + [pltpu.VMEM((S, CW), bf16)] * 2 + [pltpu.SemaphoreType.DMA((6,))],
        input_output_aliases={4: 0}, compiler_params=_params(1, True),
    )(proj, proj, dy, state, dproj, conv_w, conv_b, lam, ba, bx, wbd)


_SLOPES = [2.0 ** (-8.0 * (h + 1) / NH) for h in range(NH)]


def _half_mask(shape, e):
    lane = lax.broadcasted_iota(jnp.int32, shape, 1)
    return (lane < HD) if e == 0 else (lane >= HD)


def _both_halves(x, src):
    return jnp.where(_half_mask(x.shape, src), x, pltpu.roll(x, HD, 1))


def _attn_base(n, S):
    tq = lax.broadcasted_iota(jnp.int32, (BLK, 3 * BLK), 0)
    sk = lax.broadcasted_iota(jnp.int32, (BLK, 3 * BLK), 1)
    dist = jnp.abs(tq + BLK - sk)
    kpos = n * BLK - BLK + sk
    valid = (dist <= BLK) & (kpos >= 0) & (kpos < S)
    return jnp.where(valid, -dist.astype(f32), NEG_INF)


def _group_heads(ref, kvh, scale):
    parts = []
    for i in range(4):
        pair = 2 * kvh + i // 2
        x = ref[:, pair * 128:(pair + 1) * 128].astype(f32)
        parts.append(jnp.where(_half_mask(x.shape, i % 2), x * scale, 0.0))
    return parts


def _stack_bf16(parts):
    return jnp.concatenate([p.astype(bf16) for p in parts], axis=0)


def _attn_softmax(s_raw, base, slope, sink):
    s = s_raw + slope * base
    m = jnp.maximum(jnp.max(s, axis=-1, keepdims=True), sink)
    p = jnp.exp(s - m)
    esink = jnp.exp(sink - m)
    inv = 1.0 / (jnp.sum(p, axis=-1, keepdims=True) + esink)
    return p, inv, esink * inv


def _attn_specs(S):
    nb = S // BLK
    q_spec = pl.BlockSpec((BLK, D), lambda n: (n, 2))
    kv = lambda col: [pl.BlockSpec((BLK, 256), lambda n: (jnp.maximum(n - 1, 0), col)),
                      pl.BlockSpec((BLK, 256), lambda n: (n, col)),
                      pl.BlockSpec((BLK, 256), lambda n: (jnp.minimum(n + 1, nb - 1), col))]
    return nb, q_spec, kv(COL_K), kv(COL_V)


def _attn_fwd(proj, sink):
    S = proj.shape[0]
    nb, q_spec, k_specs, v_specs = _attn_specs(S)

    def body(sink_ref, q_ref, kp_ref, kc_ref, kn_ref, vp_ref, vc_ref, vn_ref, o_ref):
        base = _attn_base(pl.program_id(0), S)
        kcat = jnp.concatenate([kp_ref[...], kc_ref[...], kn_ref[...]], axis=0).astype(f32)
        vcat = jnp.concatenate([vp_ref[...], vc_ref[...], vn_ref[...]], axis=0).astype(f32)
        even = _half_mask((BLK, 128), 0)
        for kvh in range(NH // 4):
            ch, off = kvh // 2, kvh % 2
            kb = _both_halves(kcat[:, ch * 128:(ch + 1) * 128], off).astype(bf16)
            vb = _both_halves(vcat[:, ch * 128:(ch + 1) * 128], off).astype(bf16)
            q4 = _stack_bf16(_group_heads(q_ref, kvh, HD ** -0.5))
            s4 = lax.dot_general(q4, kb, _DIMS["nt"], preferred_element_type=f32)
            ps, invs = [], []
            for i in range(4):
                h = 4 * kvh + i
                p, inv, _ = _attn_softmax(s4[i * BLK:(i + 1) * BLK], base, _SLOPES[h], sink_ref[0, h])
                ps.append(p)
                invs.append(inv)
            o4 = jnp.dot(_stack_bf16(ps), vb, preferred_element_type=f32)
            for pr in range(2):
                lo = o4[(2 * pr) * BLK:(2 * pr + 1) * BLK] * invs[2 * pr]
                hi = o4[(2 * pr + 1) * BLK:(2 * pr + 2) * BLK] * invs[2 * pr + 1]
                pair = 2 * kvh + pr
                o_ref[:, pair * 128:(pair + 1) * 128] = jnp.where(even, lo, hi).astype(o_ref.dtype)

    return pl.pallas_call(
        body, name="attn_fwd", grid=(nb,),
        in_specs=[pl.BlockSpec(memory_space=pltpu.SMEM), q_spec] + k_specs + v_specs,
        out_specs=pl.BlockSpec((BLK, D), lambda n: (n, 0)), out_shape=_sds((S, D), bf16),
        compiler_params=_params(1, True))(sink, proj, proj, proj, proj, proj, proj, proj)


def _attn_bwd(proj, sink, y_b, dy, dproj):
    S = proj.shape[0]
    nb, q_spec, k_specs, v_specs = _attn_specs(S)
    q_col, kv_col = COL_Q * 256, COL_K * 256

    def body(sink_ref, q_ref, kp_ref, kc_ref, kn_ref, vp_ref, vc_ref, vn_ref, o_ref, do_ref, dproj_in,
             dproj_ref, dsink_ref, dk_ref, dv_ref, dq_buf, kv_buf, sems):
        n = pl.program_id(0)
        slot = n % 2
        dq_ref = dq_buf.at[slot]

        def dq_copy(step):
            rows = pl.ds(pl.multiple_of(step * BLK, BLK), BLK)
            return pltpu.make_async_copy(dq_buf.at[step % 2], dproj_ref.at[rows, pl.ds(q_col, D)], sems.at[step % 2])

        @pl.when(n >= 2)
        def _():
            dq_copy(n - 2).wait()

        @pl.when(n == 0)
        def _():
            dk_ref[...] = jnp.zeros_like(dk_ref)
            dv_ref[...] = jnp.zeros_like(dv_ref)
            dsink_ref[...] = jnp.zeros_like(dsink_ref)

        base = _attn_base(n, S)
        kcat = jnp.concatenate([kp_ref[...], kc_ref[...], kn_ref[...]], axis=0).astype(f32)
        vcat = jnp.concatenate([vp_ref[...], vc_ref[...], vn_ref[...]], axis=0).astype(f32)
        dk_rows, dv_rows = [[], []], [[], []]
        scale = HD ** -0.5
        even = _half_mask((BLK, 128), 0)
        for kvh in range(NH // 4):
            ch, off = kvh // 2, kvh % 2
            kb = _both_halves(kcat[:, ch * 128:(ch + 1) * 128], off).astype(bf16)
            vb = _both_halves(vcat[:, ch * 128:(ch + 1) * 128], off).astype(bf16)
            q_parts = _group_heads(q_ref, kvh, scale)
            d_parts = _group_heads(do_ref, kvh, 1.0)
            s4 = lax.dot_general(_stack_bf16(q_parts), kb, _DIMS["nt"], preferred_element_type=f32)
            dp4 = lax.dot_general(_stack_bf16(d_parts), vb, _DIMS["nt"], preferred_element_type=f32)
            ts, ps, qn, dn, invs = [], [], [], [], []
            for i in range(4):
                h = 4 * kvh + i
                pair = 2 * kvh + i // 2
                rows = slice(i * BLK, (i + 1) * BLK)
                p, inv, psink = _attn_softmax(s4[rows], base, _SLOPES[h], sink_ref[0, h])
                delta = jnp.sum(d_parts[i] * o_ref[:, pair * 128:(pair + 1) * 128].astype(f32), axis=-1, keepdims=True)
                dsink_ref[h:h + 1, :] += jnp.broadcast_to(-jnp.sum(psink * delta, axis=0, keepdims=True), (1, 128))
                ts.append(p * (dp4[rows] - delta))
                ps.append(p)
                qn.append(q_parts[i] * inv)
                dn.append(d_parts[i] * inv)
                invs.append(inv)
            t4 = _stack_bf16(ts)
            dq4 = jnp.dot(t4, kb, preferred_element_type=f32)
            for pr in range(2):
                lo = dq4[(2 * pr) * BLK:(2 * pr + 1) * BLK] * invs[2 * pr]
                hi = dq4[(2 * pr + 1) * BLK:(2 * pr + 2) * BLK] * invs[2 * pr + 1]
                pair = 2 * kvh + pr
                dq_ref[:, pair * 128:(pair + 1) * 128] = (jnp.where(even, lo, hi) * scale).astype(dq_ref.dtype)
            dk_t = lax.dot_general(_stack_bf16(qn), t4, _DIMS["tn"], preferred_element_type=f32)
            dv_t = lax.dot_general(_stack_bf16(dn), _stack_bf16(ps), _DIMS["tn"], preferred_element_type=f32)
            dk_rows[ch].append(dk_t[0:HD] + dk_t[HD:2 * HD])
            dv_rows[ch].append(dv_t[0:HD] + dv_t[HD:2 * HD])
        dk_acc = [jnp.concatenate(r, axis=0).T for r in dk_rows]
        dv_acc = [jnp.concatenate(r, axis=0).T for r in dv_rows]
        for j in range(3):
            blk = n + (j - 1)

            @pl.when((blk >= 0) & (blk < nb))
            def _():
                rows = pl.ds(pl.multiple_of(blk * BLK, BLK), BLK)
                for ch in range(2):
                    dk_ref[rows, ch * 128:(ch + 1) * 128] += dk_acc[ch][j * BLK:(j + 1) * BLK]
                    dv_ref[rows, ch * 128:(ch + 1) * 128] += dv_acc[ch][j * BLK:(j + 1) * BLK]

        dq_copy(n).start()

        @pl.when(n == nb - 1)
        def _():
            def cast(i, c):
                rows = pl.ds(pl.multiple_of(i * 4 * BLK, 4 * BLK), 4 * BLK)
                kv_buf[rows, 0:256] = dk_ref[rows, :].astype(bf16)
                kv_buf[rows, 256:512] = dv_ref[rows, :].astype(bf16)
                return c

            lax.fori_loop(0, S // (4 * BLK), cast, 0)
            kv_copy = pltpu.make_async_copy(kv_buf, dproj_ref.at[:, pl.ds(kv_col, 512)], sems.at[2])
            kv_copy.start()
            if nb >= 2:
                dq_copy(n - 1).wait()
            dq_copy(n).wait()
            kv_copy.wait()

    row_blk = pl.BlockSpec((BLK, D), lambda n: (n, 0))
    return pl.pallas_call(
        body, name="attn_bwd", grid=(nb,),
        in_specs=[pl.BlockSpec(memory_space=pltpu.SMEM), q_spec] + k_specs + v_specs
        + [row_blk, pl.BlockSpec((None, BLK, D), lambda n: (1, n, 0)), ANY],
        out_specs=[ANY, pl.BlockSpec((NH, 128), lambda n: (0, 0))],
        out_shape=[_sds(dproj.shape, bf16), _sds((NH, 128), f32)],
        scratch_shapes=[pltpu.VMEM((S, 256), f32), pltpu.VMEM((S, 256), f32), pltpu.VMEM((2, BLK, D), bf16),
                        pltpu.VMEM((S, 512), bf16), pltpu.SemaphoreType.DMA((3,))],
        input_output_aliases={10: 0},
        compiler_params=_params(1, True))(sink, proj, proj, proj, proj, proj, proj, proj, y_b, dy, dproj)


def _adamw(name, w, g, m, v, tr):
    R, C = w.shape
    tr = min(tr, R)

    def body(w_ref, g_ref, m_ref, v_ref, d_ref, m2_ref, v2_ref):
        g = g_ref[...]
        m2 = ADAM_B1 * m_ref[...] + (1.0 - ADAM_B1) * g
        v2 = ADAM_B2 * v_ref[...] + (1.0 - ADAM_B2) * (g * g)
        m_hat = m2 / (1.0 - ADAM_B1 ** ADAM_STEP)
        v_hat = v2 / (1.0 - ADAM_B2 ** ADAM_STEP)
        d_ref[...] = -ADAM_LR * (m_hat / (jnp.sqrt(v_hat) + ADAM_EPS) + ADAM_WD * w_ref[...])
        m2_ref[...] = m2
        v2_ref[...] = v2

    blk = pl.BlockSpec((tr, C), lambda i: (i, 0))
    return pl.pallas_call(body, name=name, grid=(R // tr,), in_specs=[blk] * 4, out_specs=[blk] * 3,
                          out_shape=[_sds((R, C), f32)] * 3, compiler_params=_params(1))(w, g, m, v)


def _pair_sum(name, c_arr, g4, recv, th):
    _, _, h, w = g4.shape
    th = min(th, h)

    def body(c_ref, g_ref, r_ref, o_ref, ob_ref):
        p = g_ref[...] + r_ref[...]
        o_ref[...] = p
        ob_ref[...] = p.astype(bf16)

    blk = pl.BlockSpec((None, th, w), lambda s, i, c_ref: (s, i, 0))
    spec = pltpu.PrefetchScalarGridSpec(
        num_scalar_prefetch=1, grid=(NCHIP, h // th),
        in_specs=[pl.BlockSpec((None, None, th, w), lambda s, i, c_ref: (s, c_ref[0], i, 0)), blk],
        out_specs=[blk, blk])
    return pl.pallas_call(body, name=name, grid_spec=spec,
                          out_shape=[_sds((NCHIP, h, w), f32), _sds((NCHIP, h, w), bf16)],
                          compiler_params=_params(2))(c_arr, g4, recv)


def _chip_sum(name, chip_arr, own4, recv3, th):
    _, h, w = own4.shape
    th = min(th, h)

    def body(s_ref, o_ref, r_ref, out_ref):
        out_ref[...] = ((o_ref[...] + r_ref[0].astype(f32)) + r_ref[1].astype(f32)) + r_ref[2].astype(f32)

    spec = pltpu.PrefetchScalarGridSpec(
        num_scalar_prefetch=1, grid=(h // th,),
        in_specs=[pl.BlockSpec((None, th, w), lambda i, s_ref: (s_ref[0], i, 0)),
                  pl.BlockSpec((3, th, w), lambda i, s_ref: (0, i, 0))],
        out_specs=pl.BlockSpec((th, w), lambda i, s_ref: (i, 0)))
    return pl.pallas_call(body, name=name, grid_spec=spec, out_shape=_sds((h, w), f32),
                          compiler_params=_params(1, True))(chip_arr, own4, recv3)


def _adamw_halves(name, c_arr, w, g_own, g_recv, m, v, th):
    h, wd = g_own.shape
    th = min(th, h)

    def body(c_ref, w_ref, go_ref, gr_ref, m_ref, v_ref, g_ref, d_ref, m2_ref, v2_ref):
        g = jnp.where(c_ref[0] == pl.program_id(0), go_ref[...], gr_ref[...])
        m2 = ADAM_B1 * m_ref[...] + (1.0 - ADAM_B1) * g
        v2 = ADAM_B2 * v_ref[...] + (1.0 - ADAM_B2) * (g * g)
        m_hat = m2 / (1.0 - ADAM_B1 ** ADAM_STEP)
        v_hat = v2 / (1.0 - ADAM_B2 ** ADAM_STEP)
        g_ref[...] = g
        d_ref[...] = -ADAM_LR * (m_hat / (jnp.sqrt(v_hat) + ADAM_EPS) + ADAM_WD * w_ref[...])
        m2_ref[...] = m2
        v2_ref[...] = v2

    nt = h // th
    full = pl.BlockSpec((th, wd), lambda hh, i, c_ref: (hh * nt + i, 0))
    half = pl.BlockSpec((th, wd), lambda hh, i, c_ref: (i, 0))
    spec = pltpu.PrefetchScalarGridSpec(num_scalar_prefetch=1, grid=(2, nt),
                                        in_specs=[full, half, half, full, full], out_specs=[full] * 4)
    return pl.pallas_call(body, name=name, grid_spec=spec, out_shape=[_sds((2 * h, wd), f32)] * 4,
                          compiler_params=_params(2))(c_arr, w, g_own, g_recv, m, v)


def _add2(name, a, b):
    def body(a_ref, b_ref, o_ref):
        o_ref[...] = a_ref[...] + b_ref[...]
    return pl.pallas_call(body, name=name, out_shape=_sds(a.shape, f32))(a, b)


def _sum4(name, b4, th):
    _, h, w = b4.shape
    th = min(th, h)

    def body(b_ref, o_ref):
        o_ref[...] = ((b_ref[0] + b_ref[1]) + b_ref[2]) + b_ref[3]

    return pl.pallas_call(body, name=name, grid=(h // th,),
                          in_specs=[pl.BlockSpec((NCHIP, th, w), lambda i: (0, i, 0))],
                          out_specs=pl.BlockSpec((th, w), lambda i: (i, 0)), out_shape=_sds((h, w), f32),
                          compiler_params=_params(1, True))(b4)


def _coords():
    x, y, c = lax.axis_index("x"), lax.axis_index("y"), lax.axis_index("c")
    return x, y, c, [(1 - x, y), (x, 1 - y), (1 - x, 1 - y)]


def _gather_chips(arrs):
    n = len(arrs)

    def body(*refs):
        ins, outs = refs[:n], refs[n:2 * n]
        send_sems, recv_sems, local_sems = refs[2 * n:2 * n + 3]
        stage = refs[2 * n + 3:]
        x, y, c, chips = _coords()
        s = 2 * x + y
        sib = (x, y, 1 - c)
        load = [pltpu.make_async_copy(ins[a], stage[a], local_sems.at[a]) for a in range(n)]
        local = [pltpu.make_async_copy(stage[a], outs[a].at[s], local_sems.at[n + a]) for a in range(n)]
        for cp in load:
            cp.start()

        def over_ici(k, a, slot, peer):
            return pltpu.make_async_remote_copy(src_ref=ins[a].at[c], dst_ref=outs[a].at[slot, c], send_sem=send_sems.at[k * n + a],
                                                recv_sem=recv_sems.at[k * n + a], device_id=peer, device_id_type=MESH)

        def to_sibling(k, a, slot, half):
            i = (3 + k) * n + a
            return pltpu.make_async_remote_copy(src_ref=outs[a].at[slot, half], dst_ref=outs[a].at[slot, half], send_sem=send_sems.at[i],
                                                recv_sem=recv_sems.at[i], device_id=sib, device_id_type=MESH)

        sends = [over_ici(k, a, s, (px, py, c)) for k, (px, py) in enumerate(chips) for a in range(n)]
        for cp in sends:
            cp.start()
        for a in range(n):
            load[a].wait()
            local[a].start()
        passed = []
        for k, (px, py) in enumerate(chips):
            for a in range(n):
                over_ici(k, a, 2 * px + py, (px, py, c)).wait_recv()
                cp = to_sibling(k, a, 2 * px + py, c)
                cp.start()
                passed.append(cp)
        for k, (px, py) in enumerate(chips):
            for a in range(n):
                to_sibling(k, a, 2 * px + py, 1 - c).wait_recv()
        for cp in sends + passed:
            cp.wait_send()
        for cp in local:
            cp.wait()

    return pl.pallas_call(
        body, name="gather_weights", in_specs=[ANY] * n, out_specs=[ANY] * n,
        out_shape=[_sds((NCHIP,) + a.shape, a.dtype) for a in arrs],
        scratch_shapes=[pltpu.SemaphoreType.DMA((6 * n,)), pltpu.SemaphoreType.DMA((6 * n,)), pltpu.SemaphoreType.DMA((2 * n,))]
        + [pltpu.VMEM(a.shape, a.dtype) for a in arrs],
        compiler_params=pltpu.CompilerParams(vmem_limit_bytes=VMEM_LIMIT),
    )(*arrs)


HBM = pl.BlockSpec(memory_space=pltpu.HBM)
SEM = pl.BlockSpec(memory_space=pltpu.SEMAPHORE)
EFFECT = pltpu.SideEffectType.DATAFLOW_SIDE_EFFECTING


def _split_start(name, n_copies, make_copies, ins, land_shapes, after):
    ni, nl = len(ins), len(land_shapes)

    def body(*refs):
        in_refs, land_refs = refs[:ni], refs[ni:ni + nl]
        send_sems, recv_sems = refs[ni + nl + 1], refs[ni + nl + 2]
        token = refs[-1]
        for cp in make_copies(in_refs, land_refs, send_sems, recv_sems):
            cp.start()
        token[...] = jnp.zeros_like(token)

    lands = [pltpu.with_memory_space_constraint(lax.empty(s.shape, s.dtype), pltpu.HBM) for s in land_shapes]
    res = pl.pallas_call(
        body, name=name,
        out_shape=(pltpu.SemaphoreType.DMA((n_copies,)), pltpu.SemaphoreType.DMA((n_copies,)),
                   *[pltpu.HBM(a.shape, a.dtype) for a in ins], *[pltpu.HBM(s.shape, s.dtype) for s in land_shapes],
                   _sds((8, 128), f32)),
        in_specs=[HBM] * (ni + nl) + [ANY], out_specs=(SEM, SEM, *[HBM] * (ni + nl), pl.BlockSpec(memory_space=pltpu.VMEM)),
        input_output_aliases={i: 2 + i for i in range(ni + nl)},
        compiler_params=pltpu.CompilerParams(has_side_effects=EFFECT),
    )(*[pltpu.with_memory_space_constraint(a, pltpu.HBM) for a in ins], *lands, after)
    return res[0], res[1], list(res[2:2 + ni]), list(res[2 + ni:2 + ni + nl]), res[-1]


def _split_wait(name, make_copies, send_sems, recv_sems, ins, lands, after):
    ni, nl = len(ins), len(lands)

    def body(*refs):
        in_refs, land_refs = refs[:ni], refs[ni:ni + nl]
        s_sems, r_sems = refs[ni + nl], refs[ni + nl + 1]
        for cp in make_copies(in_refs, land_refs, s_sems, r_sems):
            cp.wait_send()
            cp.wait_recv()

    res = pl.pallas_call(
        body, name=name, out_shape=tuple(pltpu.HBM(a.shape, a.dtype) for a in ins + lands),
        in_specs=[HBM] * (ni + nl) + [SEM, SEM, ANY], out_specs=tuple([HBM] * (ni + nl)),
        input_output_aliases={i: i for i in range(ni + nl)},
        compiler_params=pltpu.CompilerParams(has_side_effects=EFFECT),
    )(*ins, *lands, send_sems, recv_sems, after)
    return list(res[:ni]), list(res[ni:])


def _gather_copies(n):
    def make(in_refs, land_refs, send_sems, recv_sems):
        x, y, c, chips = _coords()
        s = 2 * x + y
        return [pltpu.make_async_remote_copy(src_ref=in_refs[a], dst_ref=land_refs[a].at[s], send_sem=send_sems.at[k * n + a],
                                             recv_sem=recv_sems.at[k * n + a], device_id=(px, py, c), device_id_type=MESH)
                for k, (px, py) in enumerate(chips) for a in range(n)]
    return make


def _sibling_half_copies(n):
    def make(in_refs, land_refs, send_sems, recv_sems):
        x, y, c, _ = _coords()
        return [pltpu.make_async_remote_copy(src_ref=in_refs[a].at[:, 1 - c], dst_ref=land_refs[a], send_sem=send_sems.at[a],
                                             recv_sem=recv_sems.at[a], device_id=(x, y, 1 - c), device_id_type=MESH)
                for a in range(n)]
    return make


def _chip_part_copies(n):
    def make(in_refs, land_refs, send_sems, recv_sems):
        x, y, c, chips = _coords()
        return [pltpu.make_async_remote_copy(src_ref=in_refs[a].at[2 * px + py], dst_ref=land_refs[a].at[k],
                                             send_sem=send_sems.at[k * n + a], recv_sem=recv_sems.at[k * n + a],
                                             device_id=(px, py, c), device_id_type=MESH)
                for k, (px, py) in enumerate(chips) for a in range(n)]
    return make


def _sibling_whole_copies(n):
    def make(in_refs, land_refs, send_sems, recv_sems):
        x, y, c, _ = _coords()
        return [pltpu.make_async_remote_copy(src_ref=in_refs[a], dst_ref=land_refs[a], send_sem=send_sems.at[a],
                                             recv_sem=recv_sems.at[a], device_id=(x, y, 1 - c), device_id_type=MESH)
                for a in range(n)]
    return make


def _place_own(chip_arr, owns, lands, steps):
    n = len(owns)

    def body(s_ref, *refs):
        for a in range(n):
            refs[2 * n + a][...] = refs[a][...]

    tiles = [o.shape[0] // steps for o in owns]
    spec = pltpu.PrefetchScalarGridSpec(
        num_scalar_prefetch=1, grid=(steps,),
        in_specs=[pl.BlockSpec((t, o.shape[1]), lambda i, s_ref: (i, 0)) for t, o in zip(tiles, owns)] + [ANY] * n,
        out_specs=[pl.BlockSpec((None, t, o.shape[1]), lambda i, s_ref: (s_ref[0], i, 0)) for t, o in zip(tiles, owns)])
    return pl.pallas_call(body, name="place_own", grid_spec=spec, out_shape=[_sds(l.shape, l.dtype) for l in lands],
                          input_output_aliases={1 + n + a: a for a in range(n)},
                          compiler_params=_params(1))(chip_arr, *owns, *lands)


def _sibling_halves(g4s, small):
    n = len(g4s)

    def body(*refs):
        ins, small_ref = refs[:n], refs[n]
        outs, small_out = refs[n + 1:2 * n + 1], refs[2 * n + 1]
        send_sems, recv_sems = refs[2 * n + 2:]
        x, y, c, _ = _coords()
        sib = (x, y, 1 - c)

        def remote(a, half):
            src = small_ref if a == n else ins[a].at[:, half]
            dst = small_out if a == n else outs[a]
            return pltpu.make_async_remote_copy(src_ref=src, dst_ref=dst, send_sem=send_sems.at[a], recv_sem=recv_sems.at[a],
                                                device_id=sib, device_id_type=MESH)

        sends = [remote(a, 1 - c) for a in range(n + 1)]
        for cp in sends:
            cp.start()
        for a in range(n + 1):
            remote(a, c).wait_recv()
        for cp in sends:
            cp.wait_send()

    return pl.pallas_call(
        body, name="reduce_sibling", in_specs=[ANY] * (n + 1), out_specs=[ANY] * (n + 1),
        out_shape=[_sds((g.shape[0],) + g.shape[2:], f32) for g in g4s] + [_sds(small.shape, f32)],
        scratch_shapes=[pltpu.SemaphoreType.DMA((n + 1,)), pltpu.SemaphoreType.DMA((n + 1,))],
    )(*g4s, small)


def _exchange_chips(parts, small2):
    n = len(parts)

    def body(*refs):
        ins, small_ref = refs[:n], refs[n]
        outs, small_out = refs[n + 1:2 * n + 1], refs[2 * n + 1]
        send_sems, recv_sems, local_sem = refs[2 * n + 2:]
        x, y, c, chips = _coords()
        s = 2 * x + y
        local = pltpu.make_async_copy(small_ref.at[c], small_out.at[s], local_sem)
        local.start()

        def remote(k, a, dest_chip, small_slot, peer):
            if a == n:
                src, dst = small_ref.at[c], small_out.at[small_slot]
            else:
                src, dst = ins[a].at[dest_chip], outs[a].at[k]
            i = k * (n + 1) + a
            return pltpu.make_async_remote_copy(src_ref=src, dst_ref=dst, send_sem=send_sems.at[i], recv_sem=recv_sems.at[i],
                                                device_id=peer, device_id_type=MESH)

        sends = [remote(k, a, 2 * px + py, s, (px, py, c)) for k, (px, py) in enumerate(chips) for a in range(n + 1)]
        for cp in sends:
            cp.start()
        for k, (px, py) in enumerate(chips):
            for a in range(n + 1):
                remote(k, a, s, 2 * px + py, (px, py, c)).wait_recv()
        for cp in sends:
            cp.wait_send()
        local.wait()

    m = 3 * (n + 1)
    return pl.pallas_call(
        body, name="reduce_chips", in_specs=[ANY] * (n + 1), out_specs=[ANY] * (n + 1),
        out_shape=[_sds((3,) + p.shape[1:], p.dtype) for p in parts] + [_sds((NCHIP,) + small2.shape[1:], f32)],
        scratch_shapes=[pltpu.SemaphoreType.DMA((m,)), pltpu.SemaphoreType.DMA((m,)), pltpu.SemaphoreType.DMA],
    )(*parts, small2)


def _share_sibling(halves):
    n = len(halves)

    def body(*refs):
        ins, outs = refs[:n], refs[n:2 * n]
        send_sems, recv_sems = refs[2 * n:]
        x, y, c, _ = _coords()
        sib = (x, y, 1 - c)
        sends = [pltpu.make_async_remote_copy(src_ref=ins[a], dst_ref=outs[a], send_sem=send_sems.at[a], recv_sem=recv_sems.at[a],
                                              device_id=sib, device_id_type=MESH) for a in range(n)]
        for cp in sends:
            cp.start()
        for cp in sends:
            cp.wait()

    return pl.pallas_call(
        body, name="reduce_share", in_specs=[ANY] * n, out_specs=[ANY] * n,
        out_shape=[_sds(h.shape, f32) for h in halves],
        scratch_shapes=[pltpu.SemaphoreType.DMA((n,)), pltpu.SemaphoreType.DMA((n,))],
    )(*halves)


def _block_diag_pairs(w):
    w = w.reshape(NCH, 2, HD, HD)
    z = jnp.zeros((NCH, HD, HD), w.dtype)
    return jnp.concatenate([jnp.concatenate([w[:, 0], z], axis=2), jnp.concatenate([z, w[:, 1]], axis=2)], axis=1)


def _diag_blocks(m):
    return jnp.stack([m[:, :HD, :HD], m[:, HD:, HD:]], axis=1).reshape(NH, HD, HD)


def _pack(vs, rows):
    flat = jnp.concatenate([v.reshape(-1) for v in vs])
    return jnp.pad(flat, (0, rows * 128 - flat.shape[0])).reshape(rows, 128)


def _unpack(packed, shapes):
    flat = packed.reshape(-1)
    out, off = [], 0
    for shp in shapes:
        size = math.prod(shp)
        out.append(flat[off:off + size].reshape(shp))
        off += size
    return out


def _rows_for(sizes, multiple):
    rows = -(-sum(sizes) // 128)
    return -(-rows // multiple) * multiple


def kernel(x, norm_mix_g, w_in, b_gate, conv_w, conv_b, lru_lambda, lru_wa, lru_ba, lru_wx, lru_bx, attn_sink, w_out, norm_ffn_g, w_ffn_in, w_ffn_out, norm_final_g, loss_target, m_norm_mix_g, m_w_in, m_b_gate, m_conv_w, m_conv_b, m_lru_lambda, m_lru_wa, m_lru_ba, m_lru_wx, m_lru_bx, m_attn_sink, m_w_out, m_norm_ffn_g, m_w_ffn_in, m_w_ffn_out, m_norm_final_g, v_norm_mix_g, v_w_in, v_b_gate, v_conv_w, v_conv_b, v_lru_lambda, v_lru_wa, v_lru_ba, v_lru_wx, v_lru_bx, v_attn_sink, v_w_out, v_norm_ffn_g, v_w_ffn_in, v_w_ffn_out, v_norm_final_g):
    S = x.shape[1]
    xs = x[0]
    tgt = loss_target[0]
    cx, cy, cc = lax.axis_index("x"), lax.axis_index("y"), lax.axis_index("c")
    chip = 2 * cx + cy
    SW = D // NCHIP

    small_shard = _pack([conv_w[0], lru_lambda[0], lru_ba[0], lru_bx[0]], 32)
    halves_of = lambda a: a.reshape(2, a.shape[0] // 2, a.shape[1])
    w_in_g, small_g = _gather_chips([halves_of(w_in[0].astype(bf16)), halves_of(small_shard)])
    w_in_g = w_in_g.reshape(NCHIP, D, SHW)
    small_g = small_g.reshape(NCHIP, 32, 128)
    late = [w_ffn_in[0].astype(bf16), w_out[0].astype(bf16), w_ffn_out[0].astype(bf16)]
    late_send, late_recv, late_src, late_land, late_token = _split_start(
        "gather_late_start", 9, _gather_copies(3), late, [_sds((NCHIP,) + a.shape, bf16) for a in late], small_g)
    small_parts = [_unpack(small_g[s], [(4, SW), (2, SW), (2, SW), (2, SW)]) for s in range(NCHIP)]
    conv_w_f, lam_f, ba_f, bx_f = [jnp.concatenate([small_parts[s][p] for s in range(NCHIP)], axis=1) for p in range(4)]
    wbd = jnp.concatenate([_block_diag_pairs(lru_wa[0, 0]), _block_diag_pairs(lru_wx[0, 0]),
                           _block_diag_pairs(lru_wa[0, 1]), _block_diag_pairs(lru_wx[0, 1])], axis=2).astype(bf16)
    conv_b_f = conv_b
    sink = attn_sink

    xn, proj = _rms_matmul("rms_proj", xs, norm_mix_g + late_token[0:1, 0:1], w_in_g, 1024)
    y_a, lru_state = _lru_fwd(proj, conv_w_f, conv_b_f, lam_f, ba_f, bx_f, wbd)
    y_b = _attn_fwd(proj, sink)
    late_src, late_land = _split_wait("gather_late_wait", _gather_copies(3), late_send, late_recv, late_src, late_land, y_b)
    chip_arr = chip.reshape(1).astype(jnp.int32)
    w_ffn_in_g, w_out_g, w_ffn_out_g = _place_own(chip_arr, late_src, late_land, 4)
    w_out_f = w_out_g.reshape(D, D)
    w_ffn_out_f = w_ffn_out_g.reshape(FF, D)
    merged, x1 = _merge_out_proj(proj, b_gate, y_a, y_b, w_out_f, xs, 512)
    xn2, gu, act = _rms_matmul_swiglu("rms_ffn_in", x1, norm_ffn_g, w_ffn_in_g, 1024)
    dx2, loss_row, dg3 = _ffn_out_loss_bwd(act, w_ffn_out_f, x1, norm_final_g.reshape(1, D), tgt, 512)

    tm = min(1024, S)
    tk = min(2048, S)
    gw_ffn_out = _mm_tn("dw_ffn_out", act, pl.BlockSpec((tk, SHW), lambda i, k: (k, i)),
                        dx2, pl.BlockSpec((tk, D), lambda i, k: (k, 0)),
                        _sds((FF, D), f32), pl.BlockSpec((SHW, D), lambda i, k: (i, 0)), (2, S // tk), (SHW, D))
    dgu = _swiglu_bwd(dx2, w_ffn_out_f, gu, 256)
    gw_ffn_in = _mm_tn("dw_ffn_in", xn2, pl.BlockSpec((tk, D), lambda g, k: (k, 0)),
                       dgu, pl.BlockSpec((None, tk, SHW), lambda g, k: (g // 2, k, g % 2)),
                       _sds((NCHIP, D, SHW), f32), pl.BlockSpec((None, D, SHW), lambda g, k: (g, 0, 0)),
                       (NCHIP, S // tk), (D, SHW))
    c_arr = cc.reshape(1).astype(jnp.int32)
    early_names, early_tiles = ["w_ffn_in", "w_ffn_out"], [256, 352]
    early = [gw_ffn_in.reshape(NCHIP, 2, D // 2, SHW), gw_ffn_out.reshape(NCHIP, 2, FF // NCHIP // 2, D)]
    ea_send, ea_recv, ea_src, ea_land, ea_token = _split_start(
        "reduce_early_sibling_start", 2, _sibling_half_copies(2), early,
        [_sds((NCHIP,) + g.shape[2:], f32) for g in early], dgu)
    dx1, dg2 = _mm_nt_rms_bwd("dxn2_rms_bwd", dgu, pl.BlockSpec((None, tm, SHW), lambda i, g: (g // 2, i, g % 2)), w_ffn_in_g,
                              x1, norm_ffn_g + ea_token[0:1, 0:1], dx2, tm)

    gw_out = _mm_tn("dw_out", merged, pl.BlockSpec((tk, D), lambda i, k: (k, 0)),
                    dx1, pl.BlockSpec((tk, D), lambda i, k: (k, 0)),
                    _sds((D, D), f32), pl.BlockSpec((D, D), lambda i, k: (0, 0)), (1, S // tk), (D, D))
    dproj, dy, db_gate = _merge_bwd(proj, b_gate, y_a, y_b, dx1, w_out_f, 512)
    ea_src, ea_land = _split_wait("reduce_early_sibling_wait", _sibling_half_copies(2), ea_send, ea_recv, ea_src, ea_land, dy)
    early_pairs = [_pair_sum("pair_sum_" + nm, c_arr, g4, r, th)
                   for nm, g4, r, th in zip(early_names, ea_src, ea_land, early_tiles)]
    eb_send, eb_recv, eb_src, eb_land, eb_token = _split_start(
        "reduce_early_chips_start", 6, _chip_part_copies(2), [p[1] for p in early_pairs],
        [_sds((3,) + p[1].shape[1:], bf16) for p in early_pairs], early_pairs[0][0])
    dproj, dsink = _attn_bwd(proj, sink + eb_token[0:1, 0:1], y_b, dy, dproj)
    _, eb_land = _split_wait("reduce_early_chips_wait", _chip_part_copies(2), eb_send, eb_recv, eb_src, eb_land, dsink)
    early_halves = [_chip_sum("chip_sum_" + nm, chip_arr, p[0], r3, th)
                    for nm, p, r3, th in zip(early_names, early_pairs, eb_land, early_tiles)]
    ec_send, ec_recv, ec_src, ec_land, ec_token = _split_start(
        "reduce_early_share_start", 2, _sibling_whole_copies(2), early_halves, [_sds(h.shape, f32) for h in early_halves], dsink)
    dproj, dcw, dcb, dlam, dba, dbx, dwbd = _lru_bwd(proj, dy, lru_state, dproj, conv_w_f, conv_b_f + ec_token[0:1, 0:1], lam_f,
                                                     ba_f, bx_f, wbd)
    early_halves, early_other = _split_wait("reduce_early_share_wait", _sibling_whole_copies(2), ec_send, ec_recv, ec_src, ec_land, dcb)
    gw_in = _mm_tn("dw_in", xn, pl.BlockSpec((tk, D), lambda g, k: (k, 0)),
                   dproj, pl.BlockSpec((tk, SHW), lambda g, k: (k, g)),
                   _sds((NCHIP, D, SHW), f32), pl.BlockSpec((None, D, SHW), lambda g, k: (g, 0, 0)),
                   (NCHIP, S // tk), (D, SHW))
    wa_send, wa_recv, wa_src, wa_land, wa_token = _split_start(
        "reduce_w_in_sibling_start", 1, _sibling_half_copies(1), [gw_in.reshape(NCHIP, 2, D // 2, SHW)],
        [_sds((NCHIP, D // 2, SHW), f32)], dproj)
    dxn =_mm_nt_groups("dxn", dproj, pl.BlockSpec((tm, SHW), lambda i, g: (i, g)), w_in_g, S, tm)
    wa_src, wa_land = _split_wait("reduce_w_in_sibling_wait", _sibling_half_copies(1), wa_send, wa_recv, wa_src, wa_land, dxn)
    w_in_pair = _pair_sum("pair_sum_w_in", c_arr, wa_src[0], wa_land[0], 256)
    wb_send, wb_recv, wb_src, wb_land, wb_token = _split_start(
        "reduce_w_in_chips_start", 3, _chip_part_copies(1), [w_in_pair[1]], [_sds((3, D // 2, SHW), bf16)], w_in_pair[0])
    grad_x, dg1 = _rms_bwd("rms_mix_bwd", xs, norm_mix_g + wb_token[0:1, 0:1], dxn, dx1, 512)
    _, wb_land = _split_wait("reduce_w_in_chips_wait", _chip_part_copies(1), wb_send, wb_recv, wb_src, wb_land, dg1)
    w_in_half = _chip_sum("chip_sum_w_in", chip_arr, w_in_pair[0], wb_land[0], 256)

    d_wa = jnp.stack([_diag_blocks(dwbd[:, :, 0:CW]), _diag_blocks(dwbd[:, :, 2 * CW:3 * CW])])
    d_wx = jnp.stack([_diag_blocks(dwbd[:, :, CW:2 * CW]), _diag_blocks(dwbd[:, :, 3 * CW:4 * CW])])
    small_full = [dg1, db_gate, dcw, dcb, dlam, d_wa, dba, d_wx, dbx, dsink[:, 0], dg2, dg3,
                  loss_row[0, 0:1]]
    full_shapes = [(1, D), (1, 2 * D), (4, D), (1, D), (2, D), (2, NH, HD, HD), (2, D), (2, NH, HD, HD), (2, D), (NH,),
                   (1, D), (1, D), (1,)]
    rows_full = _rows_for([math.prod(s) for s in full_shapes], 16)
    small_vec = _pack(small_full, rows_full)

    late_names, late_tiles = ["w_in", "w_out"], [256, 128]
    big = [gw_out.reshape(NCHIP, 2, D // NCHIP // 2, D)]
    *recv_a, small_sib = _sibling_halves(big, small_vec)
    w_out_pair = _pair_sum("pair_sum_w_out", c_arr, big[0], recv_a[0], 128)
    small_chip = _add2("pair_sum_small", small_vec, small_sib).reshape(2, rows_full // 2, 128)
    *recv_b, small_all = _exchange_chips([w_out_pair[1]], small_chip)
    w_out_half = _chip_sum("chip_sum_w_out", chip_arr, w_out_pair[0], recv_b[0], 128)
    halves = [w_in_half, w_out_half, _sum4("chip_sum_small", small_all, rows_full // 2)]
    *recv_c, small_other = _share_sibling(halves)
    small_lo = jnp.where(cc == 0, halves[2], small_other)
    small_hi = jnp.where(cc == 0, small_other, halves[2])
    g_full = _unpack(jnp.concatenate([small_lo, small_hi], axis=0), full_shapes)

    out_big = {}
    for nm, w, g_own, g_recv, m, v, th in zip(late_names + early_names, [w_in, w_out, w_ffn_in, w_ffn_out],
                                              halves[:2] + early_halves, recv_c + early_other,
                                              [m_w_in, m_w_out, m_w_ffn_in, m_w_ffn_out],
                                              [v_w_in, v_w_out, v_w_ffn_in, v_w_ffn_out], late_tiles + early_tiles):
        g_, d_, m_, v_ = _adamw_halves("adamw_" + nm, c_arr, w[0], g_own, g_recv, m[0], v[0], th)
        out_big[nm] = (g_[None], d_[None], m_[None], v_[None])

    small_names = ["norm_mix_g", "b_gate", "conv_w", "conv_b", "lru_lambda", "lru_wa", "lru_ba", "lru_wx", "lru_bx", "attn_sink",
                   "norm_ffn_g", "norm_final_g"]
    sharded = {"conv_w", "lru_lambda", "lru_ba", "lru_bx"}
    small_w = [norm_mix_g, b_gate, conv_w, conv_b, lru_lambda, lru_wa, lru_ba, lru_wx, lru_bx, attn_sink, norm_ffn_g, norm_final_g]
    small_m = [m_norm_mix_g, m_b_gate, m_conv_w, m_conv_b, m_lru_lambda, m_lru_wa, m_lru_ba, m_lru_wx, m_lru_bx, m_attn_sink,
               m_norm_ffn_g, m_norm_final_g]
    small_v = [v_norm_mix_g, v_b_gate, v_conv_w, v_conv_b, v_lru_lambda, v_lru_wa, v_lru_ba, v_lru_wx, v_lru_bx, v_attn_sink,
               v_norm_ffn_g, v_norm_final_g]
    g_local = []
    for nm, g, w in zip(small_names, g_full, small_w):
        if nm in sharded:
            g = lax.dynamic_slice_in_dim(g, chip * SW, SW, axis=1)
        g_local.append(g.reshape(w.shape))
    local_shapes = [w.shape for w in small_w]
    rows_local = _rows_for([math.prod(s) for s in local_shapes], 8)
    d_s, m_s, v_s = _adamw("adamw_small", _pack(small_w, rows_local), _pack(g_local, rows_local),
                           _pack(small_m, rows_local), _pack(small_v, rows_local), rows_local)
    d_l, m_l, v_l = _unpack(d_s, local_shapes), _unpack(m_s, local_shapes), _unpack(v_s, local_shapes)
    res = {nm: (g_local[i], d_l[i], m_l[i], v_l[i]) for i, nm in enumerate(small_names)}
    res.update(out_big)

    order = ["norm_mix_g", "w_in", "b_gate", "conv_w", "conv_b", "lru_lambda", "lru_wa", "lru_ba", "lru_wx", "lru_bx", "attn_sink",
             "w_out", "norm_ffn_g", "w_ffn_in", "w_ffn_out", "norm_final_g"]
    outs = [g_full[-1][0], grad_x[None]]
    for k in range(4):
        outs += [res[nm][k] for nm in order]
    return tuple(outs)
```

```python
import functools
import math

import jax
import jax.numpy as jnp
from jax import lax
from jax.experimental import pallas as pl
from jax.experimental.pallas import tpu as pltpu

f32 = jnp.float32
bf16 = jnp.bfloat16

D = 1024
NH = 16
HD = 64
FF = 2816
INW = 5632
NCHIP = 4
SHW = INW // NCHIP
CW = 128
NCH = D // CW
BLK = 128
EPS = 1e-6
NEG_INF = -1e30
RGLRU_C = 8.0
ADAM_LR, ADAM_B1, ADAM_B2, ADAM_EPS, ADAM_WD, ADAM_STEP = 0.001, 0.9, 0.999, 1e-08, 0.01, 10
VMEM_LIMIT = 58 * 1024 * 1024
MESH = pl.DeviceIdType.MESH
ANY = pl.BlockSpec(memory_space=pl.ANY)

COL_U, COL_G, COL_Q, COL_K, COL_V, COL_Z0, COL_Z1 = 0, 4, 8, 12, 13, 14, 18
MERGE_W = 512
MERGE_Z0, MERGE_Z1 = (COL_Z0 * 256) // MERGE_W, (COL_Z1 * 256) // MERGE_W


def _params(n_axes, vmem=False):
    return pltpu.CompilerParams(dimension_semantics=("arbitrary",) * n_axes,
                                vmem_limit_bytes=VMEM_LIMIT if vmem else None)


def _sds(shape, dtype):
    return jax.ShapeDtypeStruct(tuple(shape), dtype)


_DIMS = {"nn": (((1,), (0,)), ((), ())), "nt": (((1,), (1,)), ((), ())), "tn": (((0,), (0,)), ((), ()))}


def _mm(name, mode, a, a_spec, b, b_spec, out_shape, out_spec, grid, nk, acc_shape):
    def body(*refs):
        a_ref, b_ref, o_ref = refs[0], refs[1], refs[2]
        part = lax.dot_general(a_ref[...].astype(bf16), b_ref[...].astype(bf16), _DIMS[mode],
                               preferred_element_type=f32)
        if nk == 1:
            o_ref[...] = part.astype(o_ref.dtype)
            return
        acc_ref = refs[3]
        k = pl.program_id(len(grid) - 1)

        @pl.when(k == 0)
        def _():
            acc_ref[...] = part

        @pl.when(k > 0)
        def _():
            acc_ref[...] += part

        @pl.when(k == nk - 1)
        def _():
            o_ref[...] = acc_ref[...].astype(o_ref.dtype)

    scratch = [pltpu.VMEM(acc_shape, f32)] if nk > 1 else []
    return pl.pallas_call(body, name=name, grid=grid, in_specs=[a_spec, b_spec], out_specs=out_spec, out_shape=out_shape,
                          scratch_shapes=scratch, compiler_params=_params(len(grid), True))(a, b)


def _rms_matmul(name, x, g, w3, tm):
    S, K = x.shape
    G, _, Nw = w3.shape
    tm = min(tm, S)

    def body(x_ref, g_ref, w_ref, xn_ref, o_ref, xs_ref):
        @pl.when(pl.program_id(1) == 0)
        def _():
            xf = x_ref[...]
            r = lax.rsqrt(jnp.mean(xf * xf, axis=-1, keepdims=True) + EPS)
            xn = ((xf * r) * g_ref[...]).astype(bf16)
            xs_ref[...] = xn
            xn_ref[...] = xn

        o_ref[...] = jnp.dot(xs_ref[...], w_ref[...], preferred_element_type=f32).astype(bf16)

    return pl.pallas_call(
        body, name=name, grid=(S // tm, G),
        in_specs=[pl.BlockSpec((tm, K), lambda i, j: (i, 0)), pl.BlockSpec((1, K), lambda i, j: (0, 0)),
                  pl.BlockSpec((None, K, Nw), lambda i, j: (j, 0, 0))],
        out_specs=[pl.BlockSpec((tm, K), lambda i, j: (i, 0)), pl.BlockSpec((tm, Nw), lambda i, j: (i, j))],
        out_shape=[_sds((S, K), bf16), _sds((S, G * Nw), bf16)],
        scratch_shapes=[pltpu.VMEM((tm, K), bf16)], compiler_params=_params(2, True))(x, g, w3)


def _rms_matmul_swiglu(name, x, g, w3, tm):
    S, K = x.shape
    G, _, Nw = w3.shape
    tm = min(tm, S)
    half = G // 2

    def body(x_ref, g_ref, wg_ref, wu_ref, xn_ref, gu_ref, act_ref, xs_ref):
        @pl.when(pl.program_id(1) == 0)
        def _():
            xf = x_ref[...]
            r = lax.rsqrt(jnp.mean(xf * xf, axis=-1, keepdims=True) + EPS)
            xn = ((xf * r) * g_ref[...]).astype(bf16)
            xs_ref[...] = xn
            xn_ref[...] = xn

        xn = xs_ref[...]
        gate = jnp.dot(xn, wg_ref[...], preferred_element_type=f32)
        up = jnp.dot(xn, wu_ref[...], preferred_element_type=f32)
        gu_ref[0] = gate.astype(bf16)
        gu_ref[1] = up.astype(bf16)
        act_ref[...] = ((gate * _sigmoid(gate)) * up).astype(bf16)

    return pl.pallas_call(
        body, name=name, grid=(S // tm, half),
        in_specs=[pl.BlockSpec((tm, K), lambda i, j: (i, 0)), pl.BlockSpec((1, K), lambda i, j: (0, 0)),
                  pl.BlockSpec((None, K, Nw), lambda i, j: (j, 0, 0)),
                  pl.BlockSpec((None, K, Nw), lambda i, j: (half + j, 0, 0))],
        out_specs=[pl.BlockSpec((tm, K), lambda i, j: (i, 0)), pl.BlockSpec((2, tm, Nw), lambda i, j: (0, i, j)),
                   pl.BlockSpec((tm, Nw), lambda i, j: (i, j))],
        out_shape=[_sds((S, K), bf16), _sds((2, S, half * Nw), bf16), _sds((S, half * Nw), bf16)],
        scratch_shapes=[pltpu.VMEM((tm, K), bf16)], compiler_params=_params(2, True))(x, g, w3, w3)


def _mm_nt_groups(name, a, a_spec, w3, S, tm):
    G, Dout, Kw = w3.shape
    return _mm(name, "nt", a, a_spec, w3, pl.BlockSpec((None, Dout, Kw), lambda i, g: (g, 0, 0)),
               _sds((S, Dout), f32), pl.BlockSpec((tm, Dout), lambda i, g: (i, 0)), (S // tm, G), G, (tm, Dout))


def _mm_tn(name, a, a_spec, b, b_spec, out_shape, out_spec, grid, acc_shape):
    return _mm(name, "tn", a, a_spec, b, b_spec, out_shape, out_spec, grid, grid[-1], acc_shape)


def _sigmoid(x):
    return 0.5 * jnp.tanh(0.5 * x) + 0.5


_GELU_C = math.sqrt(2.0 / math.pi)


def _gelu_and_grad(x):
    v = _GELU_C * (x + 0.044715 * (x * x * x))
    t = jnp.tanh(v)
    gl = 0.5 * x * (1.0 + t)
    dgl = 0.5 * (1.0 + t) + 0.5 * x * (1.0 - t * t) * (_GELU_C * (1.0 + 3.0 * 0.044715 * (x * x)))
    return gl, dgl


def _one_minus_exp2x(x, ex):
    y = 2.0 * x
    series = y * (1.0 + y * (0.5 + y * (1.0 / 6.0 + y * (1.0 / 24.0))))
    return jnp.where(y > -1.0 / 64.0, -series, 1.0 - ex * ex)


def _z_specs(tm):
    return [pl.BlockSpec((tm, MERGE_W), lambda i, p=p: (i, MERGE_Z0 + p)) for p in range(2 * D // MERGE_W)]


def _merge_out_proj(proj, b_gate, y_a, y_b, w, res, tm):
    S = proj.shape[0]
    tm = min(tm, S)
    per = D // MERGE_W
    nz = 2 * per

    def body(*refs):
        z = refs[:nz]
        b_ref, ya_ref, yb_ref, w_ref, r_ref, m_ref, x_ref = refs[nz:]
        for p in range(per):
            cols = slice(p * MERGE_W, (p + 1) * MERGE_W)
            g0 = _sigmoid(z[p][...].astype(f32) + b_ref[:, p * MERGE_W:(p + 1) * MERGE_W])
            g1 = _sigmoid(z[per + p][...].astype(f32) + b_ref[:, D + p * MERGE_W:D + (p + 1) * MERGE_W])
            m_ref[:, cols] = (g0 * ya_ref[:, cols].astype(f32) + g1 * yb_ref[:, cols].astype(f32)).astype(bf16)
        x_ref[...] = r_ref[...] + jnp.dot(m_ref[...], w_ref[...], preferred_element_type=f32)

    row = pl.BlockSpec((tm, D), lambda i: (i, 0))
    return pl.pallas_call(
        body, name="merge_out_proj", grid=(S // tm,),
        in_specs=_z_specs(tm) + [pl.BlockSpec((1, 2 * D), lambda i: (0, 0)), row, row, pl.BlockSpec((D, D), lambda i: (0, 0)), row],
        out_specs=[row, row], out_shape=[_sds((S, D), bf16), _sds((S, D), f32)],
        compiler_params=_params(1, True))(*([proj] * nz), b_gate, y_a, y_b, w, res)


def _merge_bwd(proj, b_gate, y_a, y_b, dx, w, tm):
    S = proj.shape[0]
    tm = min(tm, S)
    per = D // MERGE_W
    nz = 2 * per
    nsteps = S // tm
    z_col = MERGE_Z0 * MERGE_W

    def body(*refs):
        z = refs[:nz]
        b_ref, ya_ref, yb_ref, dx_ref, w_ref, dproj_ref, dy_ref, db_ref, dz_buf, sems = refs[nz:]
        i = pl.program_id(0)
        slot = i % 2

        def dz_copy(step):
            rows = pl.ds(pl.multiple_of(step * tm, tm), tm)
            return pltpu.make_async_copy(dz_buf.at[step % 2], dproj_ref.at[rows, pl.ds(z_col, 2 * D)], sems.at[step % 2])

        @pl.when(i >= 2)
        def _():
            dz_copy(i - 2).wait()

        @pl.when(i == 0)
        def _():
            db_ref[...] = jnp.zeros_like(db_ref)

        dm = lax.dot_general(dx_ref[...].astype(bf16), w_ref[...], _DIMS["nt"], preferred_element_type=f32)
        for p in range(nz):
            branch, cols = p // per, slice((p % per) * MERGE_W, (p % per + 1) * MERGE_W)
            zc = slice(p * MERGE_W, (p + 1) * MERGE_W)
            g = _sigmoid(z[p][...].astype(f32) + b_ref[:, zc])
            d = dm[:, cols]
            y = (ya_ref if branch == 0 else yb_ref)[:, cols].astype(f32)
            dz = (d * y) * (g * (1.0 - g))
            dz_buf[slot, :, zc] = dz.astype(bf16)
            dy_ref[branch, :, cols] = (d * g).astype(bf16)
            db_ref[:, zc] += jnp.sum(dz, axis=0, keepdims=True)
        dz_copy(i).start()

        @pl.when(i == nsteps - 1)
        def _():
            if nsteps >= 2:
                dz_copy(i - 1).wait()
            dz_copy(i).wait()

    row = pl.BlockSpec((tm, D), lambda i: (i, 0))
    return pl.pallas_call(
        body, name="merge_bwd", grid=(nsteps,),
        in_specs=_z_specs(tm) + [pl.BlockSpec((1, 2 * D), lambda i: (0, 0)), row, row, row, pl.BlockSpec((D, D), lambda i: (0, 0))],
        out_specs=[ANY, pl.BlockSpec((2, tm, D), lambda i: (0, i, 0)), pl.BlockSpec((1, 2 * D), lambda i: (0, 0))],
        out_shape=[_sds((S, INW), bf16), _sds((2, S, D), bf16), _sds((1, 2 * D), f32)],
        scratch_shapes=[pltpu.VMEM((2, tm, 2 * D), bf16), pltpu.SemaphoreType.DMA((2,))],
        compiler_params=_params(1, True))(*([proj] * nz), b_gate, y_a, y_b, dx, w)


def _swiglu_bwd(dx, w, gu, tm):
    S, K = dx.shape
    tm = min(tm, S)

    def body(dx_ref, w_ref, gu_ref, o_ref):
        d = lax.dot_general(dx_ref[...].astype(bf16), w_ref[...], _DIMS["nt"], preferred_element_type=f32)
        g = gu_ref[0].astype(f32)
        u = gu_ref[1].astype(f32)
        s = _sigmoid(g)
        o_ref[0] = ((d * u) * (s * (1.0 + g * (1.0 - s)))).astype(bf16)
        o_ref[1] = (d * (g * s)).astype(bf16)

    stacked = pl.BlockSpec((2, tm, FF), lambda i: (0, i, 0))
    return pl.pallas_call(body, name="swiglu_bwd", grid=(S // tm,),
                          in_specs=[pl.BlockSpec((tm, K), lambda i: (i, 0)), pl.BlockSpec((FF, K), lambda i: (0, 0)), stacked],
                          out_specs=stacked, out_shape=_sds((2, S, FF), bf16),
                          compiler_params=_params(1, True))(dx, w, gu)


def _ffn_out_loss_bwd(act, w, x1, g3, tgt, tm):
    S, K = act.shape
    tm = min(tm, S)

    def body(a_ref, w_ref, r_ref, g_ref, t_ref, dx_ref, loss_ref, dg_ref):
        @pl.when(pl.program_id(0) == 0)
        def _():
            loss_ref[...] = jnp.zeros_like(loss_ref)
            dg_ref[...] = jnp.zeros_like(dg_ref)

        x = r_ref[...] + jnp.dot(a_ref[...], w_ref[...], preferred_element_type=f32)
        g = g_ref[...]
        r = lax.rsqrt(jnp.mean(x * x, axis=-1, keepdims=True) + EPS)
        xh = x * r
        err = xh * g - t_ref[...]
        row = jnp.mean(err * err, axis=-1, keepdims=True)
        loss_ref[...] += 0.5 * jnp.sum(row, axis=0, keepdims=True)
        dy = err * (1.0 / D)
        dg_ref[...] += jnp.sum(dy * xh, axis=0, keepdims=True)
        dxh = dy * g
        dx_ref[...] = r * (dxh - xh * jnp.mean(dxh * xh, axis=-1, keepdims=True))

    row_blk = pl.BlockSpec((tm, D), lambda i: (i, 0))
    vec = pl.BlockSpec((1, D), lambda i: (0, 0))
    return pl.pallas_call(body, name="ffn_out_loss_bwd", grid=(S // tm,),
                          in_specs=[pl.BlockSpec((tm, K), lambda i: (i, 0)), pl.BlockSpec((K, D), lambda i: (0, 0)),
                                    row_blk, vec, row_blk],
                          out_specs=[row_blk, pl.BlockSpec((1, 128), lambda i: (0, 0)), vec],
                          out_shape=[_sds((S, D), f32), _sds((1, 128), f32), _sds((1, D), f32)],
                          compiler_params=_params(1, True))(act, w, x1, g3, tgt)


def _rms_bwd(name, x, g, dxn, dres, tm):
    S = x.shape[0]
    tm = min(tm, S)

    def body(x_ref, g_ref, d_ref, r_ref, dx_ref, dg_ref):
        @pl.when(pl.program_id(0) == 0)
        def _():
            dg_ref[...] = jnp.zeros_like(dg_ref)

        x = x_ref[...]
        d = d_ref[...]
        r = lax.rsqrt(jnp.mean(x * x, axis=-1, keepdims=True) + EPS)
        xh = x * r
        dg_ref[...] += jnp.sum(d * xh, axis=0, keepdims=True)
        dxh = d * g_ref[...]
        dx_ref[...] = r_ref[...] + r * (dxh - xh * jnp.mean(dxh * xh, axis=-1, keepdims=True))

    row_blk = pl.BlockSpec((tm, D), lambda i: (i, 0))
    vec = pl.BlockSpec((1, D), lambda i: (0, 0))
    return pl.pallas_call(body, name=name, grid=(S // tm,), in_specs=[row_blk, vec, row_blk, row_blk],
                          out_specs=[row_blk, vec], out_shape=[_sds((S, D), f32), _sds((1, D), f32)],
                          compiler_params=_params(1))(x, g, dxn, dres)


def _mm_nt_rms_bwd(name, a, a_spec, w3, x, g, dres, tm):
    S = x.shape[0]
    G, Dout, Kw = w3.shape

    def body(a_ref, w_ref, x_ref, g_ref, r_ref, dx_ref, dg_ref, acc_ref):
        i, k = pl.program_id(0), pl.program_id(1)
        part = lax.dot_general(a_ref[...].astype(bf16), w_ref[...], _DIMS["nt"], preferred_element_type=f32)

        @pl.when(k == 0)
        def _():
            acc_ref[...] = part

        @pl.when(k > 0)
        def _():
            acc_ref[...] += part

        @pl.when(k == G - 1)
        def _():
            @pl.when(i == 0)
            def _():
                dg_ref[...] = jnp.zeros_like(dg_ref)

            for rows in (slice(0, tm // 2), slice(tm // 2, tm)):
                x_t = x_ref[rows, :]
                d = acc_ref[rows, :]
                r = lax.rsqrt(jnp.mean(x_t * x_t, axis=-1, keepdims=True) + EPS)
                xh = x_t * r
                dg_ref[...] += jnp.sum(d * xh, axis=0, keepdims=True)
                dxh = d * g_ref[...]
                dx_ref[rows, :] = r_ref[rows, :] + r * (dxh - xh * jnp.mean(dxh * xh, axis=-1, keepdims=True))

    row_blk = pl.BlockSpec((tm, Dout), lambda i, k: (i, 0))
    vec = pl.BlockSpec((1, Dout), lambda i, k: (0, 0))
    return pl.pallas_call(body, name=name, grid=(S // tm, G),
                          in_specs=[a_spec, pl.BlockSpec((None, Dout, Kw), lambda i, k: (k, 0, 0)), row_blk, vec, row_blk],
                          out_specs=[row_blk, vec], out_shape=[_sds((S, Dout), f32), _sds((1, Dout), f32)],
                          scratch_shapes=[pltpu.VMEM((tm, Dout), f32)], compiler_params=_params(2, True))(a, w3, x, g, dres)


LRU_TT = 256
SCAN_UNROLL = 8


HALO = 16


def _halo(ref, i, S):
    nt = S // LRU_TT
    t0 = pl.multiple_of(i * LRU_TT, LRU_TT)
    p0 = pl.multiple_of(jnp.maximum(t0 - HALO, 0), HALO)
    n0 = pl.multiple_of(jnp.minimum(t0 + LRU_TT, S - HALO), HALO)
    prev = jnp.where(i > 0, ref[pl.ds(p0, HALO), :].astype(f32), 0.0)
    nxt = jnp.where(i < nt - 1, ref[pl.ds(n0, HALO), :].astype(f32), 0.0)
    return jnp.concatenate([prev, ref[pl.ds(t0, LRU_TT), :].astype(f32), nxt], axis=0)


def _shift(ext, k):
    n = LRU_TT + 2 * HALO
    return pltpu.roll(ext, (-k) % n, 0)[HALO:HALO + LRU_TT]


def _lru_gates(uc, wbd, ba, bx):
    pre = jnp.dot(uc.astype(bf16), wbd, preferred_element_type=f32)
    r_f = _sigmoid(pre[:, 0:CW] + ba[0:1])
    i_f = _sigmoid(pre[:, CW:2 * CW] + bx[0:1])
    r_b = _sigmoid(pre[:, 2 * CW:3 * CW] + ba[1:2])
    i_b = _sigmoid(pre[:, 3 * CW:4 * CW] + bx[1:2])
    return r_f, i_f, r_b, i_b


def _lru_coeffs(r, sp):
    log_a = (-RGLRU_C * r) * sp
    a = jnp.exp(log_a)
    beta = jnp.sqrt(jnp.maximum(_one_minus_exp2x(log_a, a), 0.0))
    return a, beta


def _lru_coeffs_inv(r, sp):
    log_a = (-RGLRU_C * r) * sp
    a = jnp.exp(log_a)
    om = jnp.maximum(_one_minus_exp2x(log_a, a), 0.0)
    return a, jnp.sqrt(om), lax.rsqrt(jnp.maximum(om, 1e-30))


def _conv_tile(u_ref, i, S, cw, cb):
    ext = _halo(u_ref, i, S)
    um2, um1, u0, up1 = _shift(ext, -2), _shift(ext, -1), ext[HALO:HALO + LRU_TT], _shift(ext, 1)
    uc = um2 * cw[0:1] + um1 * cw[1:2] + u0 * cw[2:3] + up1 * cw[3:4] + cb
    return uc, (um2, um1, u0, up1)


def _scan_pair(S, fwd_a, fwd_b, fwd_out, rev_a, rev_b, rev_out):
    ng = S // 8
    idx = lax.broadcasted_iota(jnp.int32, (8, CW), 0)

    def local(a, b, rev):
        for sh in (1, 2, 4):
            if rev:
                keep = idx < 8 - sh
                amt = 8 - sh
            else:
                keep = idx >= sh
                amt = sh
            a_s = jnp.where(keep, pltpu.roll(a, amt, 0), 1.0)
            b_s = jnp.where(keep, pltpu.roll(b, amt, 0), 0.0)
            b = a * b_s + b
            a = a * a_s
        return a, b

    def step(it, carry):
        cf, cr = carry
        fwd_rows = [pl.multiple_of((it * SCAN_UNROLL + j) * 8, 8) for j in range(SCAN_UNROLL)]
        rev_rows = [pl.multiple_of((ng - 1 - (it * SCAN_UNROLL + j)) * 8, 8) for j in range(SCAN_UNROLL)]
        fwd_loc = [local(fwd_a(r), fwd_b(r), False) for r in fwd_rows]
        rev_loc = [local(rev_a(r), rev_b(r), True) for r in rev_rows]
        for j in range(SCAN_UNROLL):
            a, b = fwd_loc[j]
            h = a * cf + b
            fwd_out[pl.ds(fwd_rows[j], 8), :] = h
            cf = jnp.broadcast_to(h[7:8, :], (8, CW))
            a, b = rev_loc[j]
            h = a * cr + b
            rev_out[pl.ds(rev_rows[j], 8), :] = h
            cr = jnp.broadcast_to(h[0:1, :], (8, CW))
        return cf, cr

    zero = jnp.zeros((8, CW), f32)
    lax.fori_loop(0, ng // SCAN_UNROLL, step, (zero, zero))


def _lru_specs(S):
    seq = lambda off: pl.BlockSpec((S, CW), lambda j: (0, off + j))
    par = lambda rows: pl.BlockSpec((rows, CW), lambda j: (0, j))
    return seq, par


def _lru_fwd(proj, conv_w, conv_b, lam, ba, bx, wbd):
    S = proj.shape[0]
    nt = S // LRU_TT

    def body(u_ref, g_ref, cw_ref, cb_ref, lam_ref, ba_ref, bx_ref, wbd_ref, y_ref, state_ref, af_ref, bf_ref, ab_ref, bb_ref,
             sems):
        cw, cb, ba_v, bx_v, wbd_v = cw_ref[...], cb_ref[...], ba_ref[...], bx_ref[...], wbd_ref[...]
        sp = jax.nn.softplus(-lam_ref[...])
        cols = pl.ds(pl.multiple_of(pl.program_id(0) * CW, CW), CW)
        save = [pltpu.make_async_copy(ref, state_ref.at[k, :, cols], sems.at[k])
                for k, ref in enumerate((af_ref, bf_ref, ab_ref, bb_ref))]

        def phase1(i, c):
            uc, _ = _conv_tile(u_ref, i, S, cw, cb)
            r_f, i_f, r_b, i_b = _lru_gates(uc, wbd_v, ba_v, bx_v)
            rows = pl.ds(pl.multiple_of(i * LRU_TT, LRU_TT), LRU_TT)
            a, beta = _lru_coeffs(r_f, sp[0:1])
            af_ref[rows, :] = a
            bf_ref[rows, :] = beta * (i_f * uc)
            a, beta = _lru_coeffs(r_b, sp[1:2])
            ab_ref[rows, :] = a
            bb_ref[rows, :] = beta * (i_b * uc)
            return c

        lax.fori_loop(0, nt, phase1, 0)
        save[0].start()
        save[2].start()
        row8 = lambda ref: (lambda r0: ref[pl.ds(r0, 8), :])
        _scan_pair(S, row8(af_ref), row8(bf_ref), bf_ref, row8(ab_ref), row8(bb_ref), bb_ref)
        save[1].start()
        save[3].start()

        def phase3(i, c):
            rows = pl.ds(pl.multiple_of(i * LRU_TT, LRU_TT), LRU_TT)
            y = (bf_ref[rows, :] + bb_ref[rows, :]) * jax.nn.gelu(g_ref[rows, :].astype(f32))
            y_ref[rows, :] = y.astype(y_ref.dtype)
            return c

        lax.fori_loop(0, nt, phase3, 0)
        for cp in save:
            cp.wait()

    seq, par = _lru_specs(S)
    return pl.pallas_call(
        body, name="lru_fwd", grid=(NCH,),
        in_specs=[seq(0), seq(NCH), par(4), par(1), par(2), par(2), par(2),
                  pl.BlockSpec((None, CW, 4 * CW), lambda j: (j, 0, 0))],
        out_specs=[seq(0), ANY], out_shape=[_sds((S, D), bf16), _sds((4, S, D), f32)],
        scratch_shapes=[pltpu.VMEM((S, CW), f32)] * 4 + [pltpu.SemaphoreType.DMA((4,))], compiler_params=_params(1, True),
    )(proj, proj, conv_w, conv_b, lam, ba, bx, wbd)


def _lru_bwd(proj, dy, state, dproj, conv_w, conv_b, lam, ba, bx, wbd):
    S = proj.shape[0]
    nt = S // LRU_TT

    def body(u_ref, g_ref, dy_ref, state_ref, dproj_in, cw_ref, cb_ref, lam_ref, ba_ref, bx_ref, wbd_ref,
             dproj_ref, dcw_ref, dcb_ref, dlam_ref, dba_ref, dbx_ref, dwbd_ref,
             af_ref, hf2_ref, ab_ref, hb2_ref, dh_ref, du_ref, dg_ref, sems):
        cw, cb, ba_v, bx_v, wbd_v = cw_ref[...], cb_ref[...], ba_ref[...], bx_ref[...], wbd_ref[...]
        lam_v = lam_ref[...]
        sp = jax.nn.softplus(-lam_v)
        chunk = pl.program_id(0)
        slot = chunk % 2
        bf_ref, bb_ref = hf2_ref.at[slot], hb2_ref.at[slot]

        def out_copies(j):
            c0 = pl.multiple_of(j * CW, CW)
            return [pltpu.make_async_copy(du_ref, dproj_ref.at[:, pl.ds(c0, CW)], sems.at[4]),
                    pltpu.make_async_copy(dg_ref, dproj_ref.at[:, pl.ds(D + c0, CW)], sems.at[5])]

        @pl.when(chunk >= 1)
        def _():
            for cp in out_copies(chunk - 1):
                cp.wait()

        def state_copy(k, j, dst, sem):
            return pltpu.make_async_copy(state_ref.at[k, :, pl.ds(pl.multiple_of(j * CW, CW), CW)], dst, sem)

        def hidden_loads(j):
            return [state_copy(1, j, hf2_ref.at[j % 2], sems.at[6 + j % 2]), state_copy(3, j, hb2_ref.at[j % 2], sems.at[8 + j % 2])]

        load = [state_copy(0, chunk, af_ref, sems.at[0]), None, state_copy(2, chunk, ab_ref, sems.at[2])]

        @pl.when(chunk == 0)
        def _():
            for cp in hidden_loads(chunk):
                cp.start()

        load[0].start()
        load[2].start()

        @pl.when(chunk + 1 < NCH)
        def _():
            for cp in hidden_loads(chunk + 1):
                cp.start()

        for cp in hidden_loads(chunk):
            cp.wait()
        row8 = lambda ref: (lambda r0: ref[pl.ds(r0, 8), :])

        def phase0(i, c):
            rows = pl.ds(pl.multiple_of(i * LRU_TT, LRU_TT), LRU_TT)
            gl, dgl = _gelu_and_grad(g_ref[rows, :].astype(f32))
            dyt = dy_ref[rows, :].astype(f32)
            dh_ref[rows, :] = dyt * gl
            dg_ref[rows, :] = ((dyt * (bf_ref[rows, :] + bb_ref[rows, :])) * dgl).astype(dg_ref.dtype)
            return c

        lax.fori_loop(0, nt, phase0, 0)
        load[0].wait()
        load[2].wait()

        def scaled_dh(a_ref):
            def f(r0):
                return a_ref[pl.ds(r0, 8), :] * dh_ref[pl.ds(r0, 8), :]
            return f

        _scan_pair(S, row8(ab_ref), scaled_dh(ab_ref), ab_ref, row8(af_ref), scaled_dh(af_ref), af_ref)

        dcw_ref[...] = jnp.zeros_like(dcw_ref)
        dcb_ref[...] = jnp.zeros_like(dcb_ref)
        dlam_ref[...] = jnp.zeros_like(dlam_ref)
        dba_ref[...] = jnp.zeros_like(dba_ref)
        dbx_ref[...] = jnp.zeros_like(dbx_ref)
        dwbd_ref[...] = jnp.zeros_like(dwbd_ref)

        def direction(uc, r, i_g, dht, h_nb, sp_d):
            a, beta, inv_beta = _lru_coeffs_inv(r, sp_d)
            da = dht * h_nb
            dbeta = dht * (i_g * uc)
            d_iu = dht * beta
            dlog_a = da * a - (a * a) * (dbeta * inv_beta)
            dlr = dlog_a * r
            dsp = -RGLRU_C * jnp.sum(dlr, axis=0, keepdims=True)
            dpre_r = (dlr * (1.0 - r)) * (-RGLRU_C * sp_d)
            dpre_i = (d_iu * uc) * (i_g * (1.0 - i_g))
            return dpre_r, dpre_i, d_iu * i_g, dsp

        def phase4(i, c):
            uc, (um2, um1, u0, up1) = _conv_tile(u_ref, i, S, cw, cb)
            r_f, i_f, r_b, i_b = _lru_gates(uc, wbd_v, ba_v, bx_v)
            rows = pl.ds(pl.multiple_of(i * LRU_TT, LRU_TT), LRU_TT)
            dh = dh_ref[rows, :]
            dht_f = dh + _shift(_halo(af_ref, i, S), 1)
            h_prev = _shift(_halo(bf_ref, i, S), -1)
            dht_b = dh + _shift(_halo(ab_ref, i, S), -1)
            h_next = _shift(_halo(bb_ref, i, S), 1)
            prf, pif, duc_f, dsp_f = direction(uc, r_f, i_f, dht_f, h_prev, sp[0:1])
            prb, pib, duc_b, dsp_b = direction(uc, r_b, i_b, dht_b, h_next, sp[1:2])
            dpre = jnp.concatenate([prf, pif, prb, pib], axis=1)
            dpre_b = dpre.astype(bf16)
            duc = (duc_f + duc_b) + lax.dot_general(dpre_b, wbd_v, _DIMS["nt"], preferred_element_type=f32)
            dwbd_ref[...] += lax.dot_general(uc.astype(bf16), dpre_b, _DIMS["tn"], preferred_element_type=f32)
            colsum = lambda v: jnp.sum(v, axis=0, keepdims=True)
            dba_ref[...] += jnp.concatenate([colsum(prf), colsum(prb)], axis=0)
            dbx_ref[...] += jnp.concatenate([colsum(pif), colsum(pib)], axis=0)
            dlam_ref[...] += jnp.concatenate([dsp_f, dsp_b], axis=0)
            dcb_ref[...] += colsum(duc)
            dcw_ref[...] += jnp.concatenate([colsum(duc * um2), colsum(duc * um1), colsum(duc * u0),
                                             colsum(duc * up1)], axis=0)
            af_ref[rows, :] = duc
            return c

        lax.fori_loop(0, nt, phase4, 0)
        dlam_ref[...] = dlam_ref[...] * (-_sigmoid(-lam_v))

        def phase5(i, c):
            ext = _halo(af_ref, i, S)
            rows = pl.ds(pl.multiple_of(i * LRU_TT, LRU_TT), LRU_TT)
            du = (_shift(ext, 2) * cw[0:1] + _shift(ext, 1) * cw[1:2] + ext[HALO:HALO + LRU_TT] * cw[2:3]
                  + _shift(ext, -1) * cw[3:4])
            du_ref[rows, :] = du.astype(du_ref.dtype)
            return c

        lax.fori_loop(0, nt, phase5, 0)
        for cp in out_copies(chunk):
            cp.start()

        @pl.when(chunk == NCH - 1)
        def _():
            for cp in out_copies(chunk):
                cp.wait()

    seq, par = _lru_specs(S)
    return pl.pallas_call(
        body, name="lru_bwd", grid=(NCH,),
        in_specs=[seq(0), seq(NCH), pl.BlockSpec((None, S, CW), lambda j: (0, 0, j)), ANY, ANY,
                  par(4), par(1), par(2), par(2), par(2), pl.BlockSpec((None, CW, 4 * CW), lambda j: (j, 0, 0))],
        out_specs=[ANY, par(4), par(1), par(2), par(2), par(2),
                   pl.BlockSpec((None, CW, 4 * CW), lambda j: (j, 0, 0))],
        out_shape=[_sds(dproj.shape, bf16), _sds((4, D), f32), _sds((1, D), f32), _sds((2, D), f32),
                   _sds((2, D), f32), _sds((2, D), f32), _sds((NCH, CW, 4 * CW), f32)],
        scratch_shapes=[pltpu.VMEM((S, CW), f32), pltpu.VMEM((2, S, CW), f32), pltpu.VMEM((S, CW), f32), pltpu.VMEM((2, S, CW), f32),
                        pltpu.VMEM((S, CW), f32), pltpu.VMEM((S, CW), bf16), pltpu.VMEM((S, CW), bf16),
                        pltpu.SemaphoreType.DMA((10,))],
        input_output_aliases={4: 0}, compiler_params=_params(1, True),
    )(proj, proj, dy, state, dproj, conv_w, conv_b, lam, ba, bx, wbd)


_SLOPES = [2.0 ** (-8.0 * (h + 1) / NH) for h in range(NH)]


def _half_mask(shape, e):
    lane = lax.broadcasted_iota(jnp.int32, shape, 1)
    return (lane < HD) if e == 0 else (lane >= HD)


def _both_halves(x, src):
    return jnp.where(_half_mask(x.shape, src), x, pltpu.roll(x, HD, 1))


def _attn_base(n, S):
    tq = lax.broadcasted_iota(jnp.int32, (BLK, 3 * BLK), 0)
    sk = lax.broadcasted_iota(jnp.int32, (BLK, 3 * BLK), 1)
    dist = jnp.abs(tq + BLK - sk)
    kpos = n * BLK - BLK + sk
    valid = (dist <= BLK) & (kpos >= 0) & (kpos < S)
    return jnp.where(valid, -dist.astype(f32), NEG_INF)


def _group_heads(ref, kvh, scale):
    parts = []
    for i in range(4):
        pair = 2 * kvh + i // 2
        x = ref[:, pair * 128:(pair + 1) * 128].astype(f32)
        parts.append(jnp.where(_half_mask(x.shape, i % 2), x * scale, 0.0))
    return parts


def _stack_bf16(parts):
    return jnp.concatenate([p.astype(bf16) for p in parts], axis=0)


def _attn_softmax(s_raw, base, slope, sink):
    s = s_raw + slope * base
    m = jnp.maximum(jnp.max(s, axis=-1, keepdims=True), sink)
    p = jnp.exp(s - m)
    esink = jnp.exp(sink - m)
    inv = 1.0 / (jnp.sum(p, axis=-1, keepdims=True) + esink)
    return p, inv, esink * inv


def _attn_specs(S):
    nb = S // BLK
    q_spec = pl.BlockSpec((BLK, D), lambda n: (n, 2))
    kv = lambda col: [pl.BlockSpec((BLK, 256), lambda n: (jnp.maximum(n - 1, 0), col)),
                      pl.BlockSpec((BLK, 256), lambda n: (n, col)),
                      pl.BlockSpec((BLK, 256), lambda n: (jnp.minimum(n + 1, nb - 1), col))]
    return nb, q_spec, kv(COL_K), kv(COL_V)


def _attn_fwd(proj, sink):
    S = proj.shape[0]
    nb, q_spec, k_specs, v_specs = _attn_specs(S)

    def body(sink_ref, q_ref, kp_ref, kc_ref, kn_ref, vp_ref, vc_ref, vn_ref, o_ref):
        base = _attn_base(pl.program_id(0), S)
        kcat = jnp.concatenate([kp_ref[...], kc_ref[...], kn_ref[...]], axis=0).astype(f32)
        vcat = jnp.concatenate([vp_ref[...], vc_ref[...], vn_ref[...]], axis=0).astype(f32)
        even = _half_mask((BLK, 128), 0)
        for kvh in range(NH // 4):
            ch, off = kvh // 2, kvh % 2
            kb = _both_halves(kcat[:, ch * 128:(ch + 1) * 128], off).astype(bf16)
            vb = _both_halves(vcat[:, ch * 128:(ch + 1) * 128], off).astype(bf16)
            q4 = _stack_bf16(_group_heads(q_ref, kvh, HD ** -0.5))
            s4 = lax.dot_general(q4, kb, _DIMS["nt"], preferred_element_type=f32)
            ps, invs = [], []
            for i in range(4):
                h = 4 * kvh + i
                p, inv, _ = _attn_softmax(s4[i * BLK:(i + 1) * BLK], base, _SLOPES[h], sink_ref[0, h])
                ps.append(p)
                invs.append(inv)
            o4 = jnp.dot(_stack_bf16(ps), vb, preferred_element_type=f32)
            for pr in range(2):
                lo = o4[(2 * pr) * BLK:(2 * pr + 1) * BLK] * invs[2 * pr]
                hi = o4[(2 * pr + 1) * BLK:(2 * pr + 2) * BLK] * invs[2 * pr + 1]
                pair = 2 * kvh + pr
                o_ref[:, pair * 128:(pair + 1) * 128] = jnp.where(even, lo, hi).astype(o_ref.dtype)

    return pl.pallas_call(
        body, name="attn_fwd", grid=(nb,),
        in_specs=[pl.BlockSpec(memory_space=pltpu.SMEM), q_spec] + k_specs + v_specs,
        out_specs=pl.BlockSpec((BLK, D), lambda n: (n, 0)), out_shape=_sds((S, D), bf16),
        compiler_params=_params(1, True))(sink, proj, proj, proj, proj, proj, proj, proj)


def _attn_bwd(proj, sink, y_b, dy, dproj):
    S = proj.shape[0]
    nb, q_spec, k_specs, v_specs = _attn_specs(S)
    q_col, kv_col = COL_Q * 256, COL_K * 256

    def body(sink_ref, q_ref, kp_ref, kc_ref, kn_ref, vp_ref, vc_ref, vn_ref, o_ref, do_ref, dproj_in,
             dproj_ref, dsink_ref, dk_ref, dv_ref, dq_buf, kv_buf, sems):
        n = pl.program_id(0)
        slot = n % 2
        dq_ref = dq_buf.at[slot]

        def dq_copy(step):
            rows = pl.ds(pl.multiple_of(step * BLK, BLK), BLK)
            return pltpu.make_async_copy(dq_buf.at[step % 2], dproj_ref.at[rows, pl.ds(q_col, D)], sems.at[step % 2])

        @pl.when(n >= 2)
        def _():
            dq_copy(n - 2).wait()

        @pl.when(n == 0)
        def _():
            dk_ref[...] = jnp.zeros_like(dk_ref)
            dv_ref[...] = jnp.zeros_like(dv_ref)
            dsink_ref[...] = jnp.zeros_like(dsink_ref)

        base = _attn_base(n, S)
        kcat = jnp.concatenate([kp_ref[...], kc_ref[...], kn_ref[...]], axis=0).astype(f32)
        vcat = jnp.concatenate([vp_ref[...], vc_ref[...], vn_ref[...]], axis=0).astype(f32)
        dk_rows, dv_rows = [[], []], [[], []]
        scale = HD ** -0.5
        even = _half_mask((BLK, 128), 0)
        for kvh in range(NH // 4):
            ch, off = kvh // 2, kvh % 2
            kb = _both_halves(kcat[:, ch * 128:(ch + 1) * 128], off).astype(bf16)
            vb = _both_halves(vcat[:, ch * 128:(ch + 1) * 128], off).astype(bf16)
            q_parts = _group_heads(q_ref, kvh, scale)
            d_parts = _group_heads(do_ref, kvh, 1.0)
            s4 = lax.dot_general(_stack_bf16(q_parts), kb, _DIMS["nt"], preferred_element_type=f32)
            dp4 = lax.dot_general(_stack_bf16(d_parts), vb, _DIMS["nt"], preferred_element_type=f32)
            ts, ps, qn, dn, invs = [], [], [], [], []
            for i in range(4):
                h = 4 * kvh + i
                pair = 2 * kvh + i // 2
                rows = slice(i * BLK, (i + 1) * BLK)
                p, inv, psink = _attn_softmax(s4[rows], base, _SLOPES[h], sink_ref[0, h])
                delta = jnp.sum(d_parts[i] * o_ref[:, pair * 128:(pair + 1) * 128].astype(f32), axis=-1, keepdims=True)
                dsink_ref[h:h + 1, :] += jnp.broadcast_to(-jnp.sum(psink * delta, axis=0, keepdims=True), (1, 128))
                ts.append(p * (dp4[rows] - delta))
                ps.append(p)
                qn.append(q_parts[i] * inv)
                dn.append(d_parts[i] * inv)
                invs.append(inv)
            t4 = _stack_bf16(ts)
            dq4 = jnp.dot(t4, kb, preferred_element_type=f32)
            for pr in range(2):
                lo = dq4[(2 * pr) * BLK:(2 * pr + 1) * BLK] * invs[2 * pr]
                hi = dq4[(2 * pr + 1) * BLK:(2 * pr + 2) * BLK] * invs[2 * pr + 1]
                pair = 2 * kvh + pr
                dq_ref[:, pair * 128:(pair + 1) * 128] = (jnp.where(even, lo, hi) * scale).astype(dq_ref.dtype)
            dk_t = lax.dot_general(_stack_bf16(qn), t4, _DIMS["tn"], preferred_element_type=f32)
            dv_t = lax.dot_general(_stack_bf16(dn), _stack_bf16(ps), _DIMS["tn"], preferred_element_type=f32)
            dk_rows[ch].append(dk_t[0:HD] + dk_t[HD:2 * HD])
            dv_rows[ch].append(dv_t[0:HD] + dv_t[HD:2 * HD])
        dk_acc = [jnp.concatenate(r, axis=0).T for r in dk_rows]
        dv_acc = [jnp.concatenate(r, axis=0).T for r in dv_rows]
        for j in range(3):
            blk = n + (j - 1)

            @pl.when((blk >= 0) & (blk < nb))
            def _():
                rows = pl.ds(pl.multiple_of(blk * BLK, BLK), BLK)
                for ch in range(2):
                    dk_ref[rows, ch * 128:(ch + 1) * 128] += dk_acc[ch][j * BLK:(j + 1) * BLK]
                    dv_ref[rows, ch * 128:(ch + 1) * 128] += dv_acc[ch][j * BLK:(j + 1) * BLK]

        dq_copy(n).start()

        @pl.when(n == nb - 1)
        def _():
            def cast(i, c):
                rows = pl.ds(pl.multiple_of(i * 4 * BLK, 4 * BLK), 4 * BLK)
                kv_buf[rows, 0:256] = dk_ref[rows, :].astype(bf16)
                kv_buf[rows, 256:512] = dv_ref[rows, :].astype(bf16)
                return c

            lax.fori_loop(0, S // (4 * BLK), cast, 0)
            kv_copy = pltpu.make_async_copy(kv_buf, dproj_ref.at[:, pl.ds(kv_col, 512)], sems.at[2])
            kv_copy.start()
            if nb >= 2:
                dq_copy(n - 1).wait()
            dq_copy(n).wait()
            kv_copy.wait()

    row_blk = pl.BlockSpec((BLK, D), lambda n: (n, 0))
    return pl.pallas_call(
        body, name="attn_bwd", grid=(nb,),
        in_specs=[pl.BlockSpec(memory_space=pltpu.SMEM), q_spec] + k_specs + v_specs
        + [row_blk, pl.BlockSpec((None, BLK, D), lambda n: (1, n, 0)), ANY],
        out_specs=[ANY, pl.BlockSpec((NH, 128), lambda n: (0, 0))],
        out_shape=[_sds(dproj.shape, bf16), _sds((NH, 128), f32)],
        scratch_shapes=[pltpu.VMEM((S, 256), f32), pltpu.VMEM((S, 256), f32), pltpu.VMEM((2, BLK, D), bf16),
                        pltpu.VMEM((S, 512), bf16), pltpu.SemaphoreType.DMA((3,))],
        input_output_aliases={10: 0},
        compiler_params=_params(1, True))(sink, proj, proj, proj, proj, proj, proj, proj, y_b, dy, dproj)


def _adamw(name, w, g, m, v, tr):
    R, C = w.shape
    tr = min(tr, R)

    def body(w_ref, g_ref, m_ref, v_ref, d_ref, m2_ref, v2_ref):
        g = g_ref[...]
        m2 = ADAM_B1 * m_ref[...] + (1.0 - ADAM_B1) * g
        v2 = ADAM_B2 * v_ref[...] + (1.0 - ADAM_B2) * (g * g)
        m_hat = m2 / (1.0 - ADAM_B1 ** ADAM_STEP)
        v_hat = v2 / (1.0 - ADAM_B2 ** ADAM_STEP)
        d_ref[...] = -ADAM_LR * (m_hat / (jnp.sqrt(v_hat) + ADAM_EPS) + ADAM_WD * w_ref[...])
        m2_ref[...] = m2
        v2_ref[...] = v2

    blk = pl.BlockSpec((tr, C), lambda i: (i, 0))
    return pl.pallas_call(body, name=name, grid=(R // tr,), in_specs=[blk] * 4, out_specs=[blk] * 3,
                          out_shape=[_sds((R, C), f32)] * 3, compiler_params=_params(1))(w, g, m, v)


def _pair_sum(name, c_arr, g4, recv, th):
    _, _, h, w = g4.shape
    th = min(th, h)

    def body(c_ref, g_ref, r_ref, o_ref, ob_ref):
        p = g_ref[...] + r_ref[...]
        o_ref[...] = p
        ob_ref[...] = p.astype(bf16)

    blk = pl.BlockSpec((None, th, w), lambda s, i, c_ref: (s, i, 0))
    spec = pltpu.PrefetchScalarGridSpec(
        num_scalar_prefetch=1, grid=(NCHIP, h // th),
        in_specs=[pl.BlockSpec((None, None, th, w), lambda s, i, c_ref: (s, c_ref[0], i, 0)), blk],
        out_specs=[blk, blk])
    return pl.pallas_call(body, name=name, grid_spec=spec,
                          out_shape=[_sds((NCHIP, h, w), f32), _sds((NCHIP, h, w), bf16)],
                          compiler_params=_params(2))(c_arr, g4, recv)


def _chip_sum(name, chip_arr, own4, recv3, th):
    _, h, w = own4.shape
    th = min(th, h)

    def body(s_ref, o_ref, r_ref, out_ref):
        out_ref[...] = ((o_ref[...] + r_ref[0].astype(f32)) + r_ref[1].astype(f32)) + r_ref[2].astype(f32)

    spec = pltpu.PrefetchScalarGridSpec(
        num_scalar_prefetch=1, grid=(h // th,),
        in_specs=[pl.BlockSpec((None, th, w), lambda i, s_ref: (s_ref[0], i, 0)),
                  pl.BlockSpec((3, th, w), lambda i, s_ref: (0, i, 0))],
        out_specs=pl.BlockSpec((th, w), lambda i, s_ref: (i, 0)))
    return pl.pallas_call(body, name=name, grid_spec=spec, out_shape=_sds((h, w), f32),
                          compiler_params=_params(1, True))(chip_arr, own4, recv3)


def _adamw_halves(name, c_arr, w, g_own, g_recv, m, v, th):
    h, wd = g_own.shape
    th = min(th, h)

    def body(c_ref, w_ref, go_ref, gr_ref, m_ref, v_ref, g_ref, d_ref, m2_ref, v2_ref):
        g = jnp.where(c_ref[0] == pl.program_id(0), go_ref[...], gr_ref[...])
        m2 = ADAM_B1 * m_ref[...] + (1.0 - ADAM_B1) * g
        v2 = ADAM_B2 * v_ref[...] + (1.0 - ADAM_B2) * (g * g)
        m_hat = m2 / (1.0 - ADAM_B1 ** ADAM_STEP)
        v_hat = v2 / (1.0 - ADAM_B2 ** ADAM_STEP)
        g_ref[...] = g
        d_ref[...] = -ADAM_LR * (m_hat / (jnp.sqrt(v_hat) + ADAM_EPS) + ADAM_WD * w_ref[...])
        m2_ref[...] = m2
        v2_ref[...] = v2

    nt = h // th
    full = pl.BlockSpec((th, wd), lambda hh, i, c_ref: (hh * nt + i, 0))
    half = pl.BlockSpec((th, wd), lambda hh, i, c_ref: (i, 0))
    spec = pltpu.PrefetchScalarGridSpec(num_scalar_prefetch=1, grid=(2, nt),
                                        in_specs=[full, half, half, full, full], out_specs=[full] * 4)
    return pl.pallas_call(body, name=name, grid_spec=spec, out_shape=[_sds((2 * h, wd), f32)] * 4,
                          compiler_params=_params(2))(c_arr, w, g_own, g_recv, m, v)


def _add2(name, a, b):
    def body(a_ref, b_ref, o_ref):
        o_ref[...] = a_ref[...] + b_ref[...]
    return pl.pallas_call(body, name=name, out_shape=_sds(a.shape, f32))(a, b)


def _sum4(name, b4, th):
    _, h, w = b4.shape
    th = min(th, h)

    def body(b_ref, o_ref):
        o_ref[...] = ((b_ref[0] + b_ref[1]) + b_ref[2]) + b_ref[3]

    return pl.pallas_call(body, name=name, grid=(h // th,),
                          in_specs=[pl.BlockSpec((NCHIP, th, w), lambda i: (0, i, 0))],
                          out_specs=pl.BlockSpec((th, w), lambda i: (i, 0)), out_shape=_sds((h, w), f32),
                          compiler_params=_params(1, True))(b4)


def _coords():
    x, y, c = lax.axis_index("x"), lax.axis_index("y"), lax.axis_index("c")
    return x, y, c, [(1 - x, y), (x, 1 - y), (1 - x, 1 - y)]


def _gather_chips(arrs):
    n = len(arrs)

    def body(*refs):
        ins, outs = refs[:n], refs[n:2 * n]
        send_sems, recv_sems, local_sems = refs[2 * n:2 * n + 3]
        stage = refs[2 * n + 3:]
        x, y, c, chips = _coords()
        s = 2 * x + y
        sib = (x, y, 1 - c)
        load = [pltpu.make_async_copy(ins[a], stage[a], local_sems.at[a]) for a in range(n)]
        local = [pltpu.make_async_copy(stage[a], outs[a].at[s], local_sems.at[n + a]) for a in range(n)]
        for cp in load:
            cp.start()

        def over_ici(k, a, slot, peer):
            return pltpu.make_async_remote_copy(src_ref=ins[a].at[c], dst_ref=outs[a].at[slot, c], send_sem=send_sems.at[k * n + a],
                                                recv_sem=recv_sems.at[k * n + a], device_id=peer, device_id_type=MESH)

        def to_sibling(k, a, slot, half):
            i = (3 + k) * n + a
            return pltpu.make_async_remote_copy(src_ref=outs[a].at[slot, half], dst_ref=outs[a].at[slot, half], send_sem=send_sems.at[i],
                                                recv_sem=recv_sems.at[i], device_id=sib, device_id_type=MESH)

        sends = [over_ici(k, a, s, (px, py, c)) for k, (px, py) in enumerate(chips) for a in range(n)]
        for cp in sends:
            cp.start()
        for a in range(n):
            load[a].wait()
            local[a].start()
        passed = []
        for k, (px, py) in enumerate(chips):
            for a in range(n):
                over_ici(k, a, 2 * px + py, (px, py, c)).wait_recv()
                cp = to_sibling(k, a, 2 * px + py, c)
                cp.start()
                passed.append(cp)
        for k, (px, py) in enumerate(chips):
            for a in range(n):
                to_sibling(k, a, 2 * px + py, 1 - c).wait_recv()
        for cp in sends + passed:
            cp.wait_send()
        for cp in local:
            cp.wait()

    return pl.pallas_call(
        body, name="gather_weights", in_specs=[ANY] * n, out_specs=[ANY] * n,
        out_shape=[_sds((NCHIP,) + a.shape, a.dtype) for a in arrs],
        scratch_shapes=[pltpu.SemaphoreType.DMA((6 * n,)), pltpu.SemaphoreType.DMA((6 * n,)), pltpu.SemaphoreType.DMA((2 * n,))]
        + [pltpu.VMEM(a.shape, a.dtype) for a in arrs],
        compiler_params=pltpu.CompilerParams(vmem_limit_bytes=VMEM_LIMIT),
    )(*arrs)


HBM = pl.BlockSpec(memory_space=pltpu.HBM)
SEM = pl.BlockSpec(memory_space=pltpu.SEMAPHORE)
EFFECT = pltpu.SideEffectType.DATAFLOW_SIDE_EFFECTING


def _split_start(name, n_copies, make_copies, ins, land_shapes, after):
    ni, nl = len(ins), len(land_shapes)

    def body(*refs):
        in_refs, land_refs = refs[:ni], refs[ni:ni + nl]
        send_sems, recv_sems = refs[ni + nl + 1], refs[ni + nl + 2]
        token = refs[-1]
        for cp in make_copies(in_refs, land_refs, send_sems, recv_sems):
            cp.start()
        token[...] = jnp.zeros_like(token)

    lands = [pltpu.with_memory_space_constraint(lax.empty(s.shape, s.dtype), pltpu.HBM) for s in land_shapes]
    res = pl.pallas_call(
        body, name=name,
        out_shape=(pltpu.SemaphoreType.DMA((n_copies,)), pltpu.SemaphoreType.DMA((n_copies,)),
                   *[pltpu.HBM(a.shape, a.dtype) for a in ins], *[pltpu.HBM(s.shape, s.dtype) for s in land_shapes],
                   _sds((8, 128), f32)),
        in_specs=[HBM] * (ni + nl) + [ANY], out_specs=(SEM, SEM, *[HBM] * (ni + nl), pl.BlockSpec(memory_space=pltpu.VMEM)),
        input_output_aliases={i: 2 + i for i in range(ni + nl)},
        compiler_params=pltpu.CompilerParams(has_side_effects=EFFECT),
    )(*[pltpu.with_memory_space_constraint(a, pltpu.HBM) for a in ins], *lands, after)
    return res[0], res[1], list(res[2:2 + ni]), list(res[2 + ni:2 + ni + nl]), res[-1]


def _split_wait(name, make_copies, send_sems, recv_sems, ins, lands, after):
    ni, nl = len(ins), len(lands)

    def body(*refs):
        in_refs, land_refs = refs[:ni], refs[ni:ni + nl]
        s_sems, r_sems = refs[ni + nl], refs[ni + nl + 1]
        for cp in make_copies(in_refs, land_refs, s_sems, r_sems):
            cp.wait_send()
            cp.wait_recv()

    res = pl.pallas_call(
        body, name=name, out_shape=tuple(pltpu.HBM(a.shape, a.dtype) for a in ins + lands),
        in_specs=[HBM] * (ni + nl) + [SEM, SEM, ANY], out_specs=tuple([HBM] * (ni + nl)),
        input_output_aliases={i: i for i in range(ni + nl)},
        compiler_params=pltpu.CompilerParams(has_side_effects=EFFECT),
    )(*ins, *lands, send_sems, recv_sems, after)
    return list(res[:ni]), list(res[ni:])


def _gather_copies(n):
    def make(in_refs, land_refs, send_sems, recv_sems):
        x, y, c, chips = _coords()
        s = 2 * x + y
        return [pltpu.make_async_remote_copy(src_ref=in_refs[a], dst_ref=land_refs[a].at[s], send_sem=send_sems.at[k * n + a],
                                             recv_sem=recv_sems.at[k * n + a], device_id=(px, py, c), device_id_type=MESH)
                for k, (px, py) in enumerate(chips) for a in range(n)]
    return make


def _sibling_half_copies(n):
    def make(in_refs, land_refs, send_sems, recv_sems):
        x, y, c, _ = _coords()
        return [pltpu.make_async_remote_copy(src_ref=in_refs[a].at[:, 1 - c], dst_ref=land_refs[a], send_sem=send_sems.at[a],
                                             recv_sem=recv_sems.at[a], device_id=(x, y, 1 - c), device_id_type=MESH)
                for a in range(n)]
    return make


def _chip_part_copies(n):
    def make(in_refs, land_refs, send_sems, recv_sems):
        x, y, c, chips = _coords()
        return [pltpu.make_async_remote_copy(src_ref=in_refs[a].at[2 * px + py], dst_ref=land_refs[a].at[k],
                                             send_sem=send_sems.at[k * n + a], recv_sem=recv_sems.at[k * n + a],
                                             device_id=(px, py, c), device_id_type=MESH)
                for k, (px, py) in enumerate(chips) for a in range(n)]
    return make


def _sibling_whole_copies(n):
    def make(in_refs, land_refs, send_sems, recv_sems):
        x, y, c, _ = _coords()
        return [pltpu.make_async_remote_copy(src_ref=in_refs[a], dst_ref=land_refs[a], send_sem=send_sems.at[a],
                                             recv_sem=recv_sems.at[a], device_id=(x, y, 1 - c), device_id_type=MESH)
                for a in range(n)]
    return make


def _place_own(chip_arr, owns, lands, steps):
    n = len(owns)

    def body(s_ref, *refs):
        for a in range(n):
            refs[2 * n + a][...] = refs[a][...]

    tiles = [o.shape[0] // steps for o in owns]
    spec = pltpu.PrefetchScalarGridSpec(
        num_scalar_prefetch=1, grid=(steps,),
        in_specs=[pl.BlockSpec((t, o.shape[1]), lambda i, s_ref: (i, 0)) for t, o in zip(tiles, owns)] + [ANY] * n,
        out_specs=[pl.BlockSpec((None, t, o.shape[1]), lambda i, s_ref: (s_ref[0], i, 0)) for t, o in zip(tiles, owns)])
    return pl.pallas_call(body, name="place_own", grid_spec=spec, out_shape=[_sds(l.shape, l.dtype) for l in lands],
                          input_output_aliases={1 + n + a: a for a in range(n)},
                          compiler_params=_params(1))(chip_arr, *owns, *lands)


def _sibling_halves(g4s, small):
    n = len(g4s)

    def body(*refs):
        ins, small_ref = refs[:n], refs[n]
        outs, small_out = refs[n + 1:2 * n + 1], refs[2 * n + 1]
        send_sems, recv_sems = refs[2 * n + 2:]
        x, y, c, _ = _coords()
        sib = (x, y, 1 - c)

        def remote(a, half):
            src = small_ref if a == n else ins[a].at[:, half]
            dst = small_out if a == n else outs[a]
            return pltpu.make_async_remote_copy(src_ref=src, dst_ref=dst, send_sem=send_sems.at[a], recv_sem=recv_sems.at[a],
                                                device_id=sib, device_id_type=MESH)

        sends = [remote(a, 1 - c) for a in range(n + 1)]
        for cp in sends:
            cp.start()
        for a in range(n + 1):
            remote(a, c).wait_recv()
        for cp in sends:
            cp.wait_send()

    return pl.pallas_call(
        body, name="reduce_sibling", in_specs=[ANY] * (n + 1), out_specs=[ANY] * (n + 1),
        out_shape=[_sds((g.shape[0],) + g.shape[2:], f32) for g in g4s] + [_sds(small.shape, f32)],
        scratch_shapes=[pltpu.SemaphoreType.DMA((n + 1,)), pltpu.SemaphoreType.DMA((n + 1,))],
    )(*g4s, small)


def _exchange_chips(parts, small2):
    n = len(parts)

    def body(*refs):
        ins, small_ref = refs[:n], refs[n]
        outs, small_out = refs[n + 1:2 * n + 1], refs[2 * n + 1]
        send_sems, recv_sems, local_sem = refs[2 * n + 2:]
        x, y, c, chips = _coords()
        s = 2 * x + y
        local = pltpu.make_async_copy(small_ref.at[c], small_out.at[s], local_sem)
        local.start()

        def remote(k, a, dest_chip, small_slot, peer):
            if a == n:
                src, dst = small_ref.at[c], small_out.at[small_slot]
            else:
                src, dst = ins[a].at[dest_chip], outs[a].at[k]
            i = k * (n + 1) + a
            return pltpu.make_async_remote_copy(src_ref=src, dst_ref=dst, send_sem=send_sems.at[i], recv_sem=recv_sems.at[i],
                                                device_id=peer, device_id_type=MESH)

        sends = [remote(k, a, 2 * px + py, s, (px, py, c)) for k, (px, py) in enumerate(chips) for a in range(n + 1)]
        for cp in sends:
            cp.start()
        for k, (px, py) in enumerate(chips):
            for a in range(n + 1):
                remote(k, a, s, 2 * px + py, (px, py, c)).wait_recv()
        for cp in sends:
            cp.wait_send()
        local.wait()

    m = 3 * (n + 1)
    return pl.pallas_call(
        body, name="reduce_chips", in_specs=[ANY] * (n + 1), out_specs=[ANY] * (n + 1),
        out_shape=[_sds((3,) + p.shape[1:], p.dtype) for p in parts] + [_sds((NCHIP,) + small2.shape[1:], f32)],
        scratch_shapes=[pltpu.SemaphoreType.DMA((m,)), pltpu.SemaphoreType.DMA((m,)), pltpu.SemaphoreType.DMA],
    )(*parts, small2)


def _share_sibling(halves):
    n = len(halves)

    def body(*refs):
        ins, outs = refs[:n], refs[n:2 * n]
        send_sems, recv_sems = refs[2 * n:]
        x, y, c, _ = _coords()
        sib = (x, y, 1 - c)
        sends = [pltpu.make_async_remote_copy(src_ref=ins[a], dst_ref=outs[a], send_sem=send_sems.at[a], recv_sem=recv_sems.at[a],
                                              device_id=sib, device_id_type=MESH) for a in range(n)]
        for cp in sends:
            cp.start()
        for cp in sends:
            cp.wait()

    return pl.pallas_call(
        body, name="reduce_share", in_specs=[ANY] * n, out_specs=[ANY] * n,
        out_shape=[_sds(h.shape, f32) for h in halves],
        scratch_shapes=[pltpu.SemaphoreType.DMA((n,)), pltpu.SemaphoreType.DMA((n,))],
    )(*halves)


def _block_diag_pairs(w):
    w = w.reshape(NCH, 2, HD, HD)
    z = jnp.zeros((NCH, HD, HD), w.dtype)
    return jnp.concatenate([jnp.concatenate([w[:, 0], z], axis=2), jnp.concatenate([z, w[:, 1]], axis=2)], axis=1)


def _diag_blocks(m):
    return jnp.stack([m[:, :HD, :HD], m[:, HD:, HD:]], axis=1).reshape(NH, HD, HD)


def _pack(vs, rows):
    flat = jnp.concatenate([v.reshape(-1) for v in vs])
    return jnp.pad(flat, (0, rows * 128 - flat.shape[0])).reshape(rows, 128)


def _unpack(packed, shapes):
    flat = packed.reshape(-1)
    out, off = [], 0
    for shp in shapes:
        size = math.prod(shp)
        out.append(flat[off:off + size].reshape(shp))
        off += size
    return out


def _rows_for(sizes, multiple):
    rows = -(-sum(sizes) // 128)
    return -(-rows // multiple) * multiple


def kernel(x, norm_mix_g, w_in, b_gate, conv_w, conv_b, lru_lambda, lru_wa, lru_ba, lru_wx, lru_bx, attn_sink, w_out, norm_ffn_g, w_ffn_in, w_ffn_out, norm_final_g, loss_target, m_norm_mix_g, m_w_in, m_b_gate, m_conv_w, m_conv_b, m_lru_lambda, m_lru_wa, m_lru_ba, m_lru_wx, m_lru_bx, m_attn_sink, m_w_out, m_norm_ffn_g, m_w_ffn_in, m_w_ffn_out, m_norm_final_g, v_norm_mix_g, v_w_in, v_b_gate, v_conv_w, v_conv_b, v_lru_lambda, v_lru_wa, v_lru_ba, v_lru_wx, v_lru_bx, v_attn_sink, v_w_out, v_norm_ffn_g, v_w_ffn_in, v_w_ffn_out, v_norm_final_g):
    S = x.shape[1]
    xs = x[0]
    tgt = loss_target[0]
    cx, cy, cc = lax.axis_index("x"), lax.axis_index("y"), lax.axis_index("c")
    chip = 2 * cx + cy
    SW = D // NCHIP

    small_shard = _pack([conv_w[0], lru_lambda[0], lru_ba[0], lru_bx[0]], 32)
    halves_of = lambda a: a.reshape(2, a.shape[0] // 2, a.shape[1])
    w_in_g, small_g = _gather_chips([halves_of(w_in[0].astype(bf16)), halves_of(small_shard)])
    w_in_g = w_in_g.reshape(NCHIP, D, SHW)
    small_g = small_g.reshape(NCHIP, 32, 128)
    late = [w_ffn_in[0].astype(bf16), w_out[0].astype(bf16), w_ffn_out[0].astype(bf16)]
    late_send, late_recv, late_src, late_land, late_token = _split_start(
        "gather_late_start", 9, _gather_copies(3), late, [_sds((NCHIP,) + a.shape, bf16) for a in late], small_g)
    small_parts = [_unpack(small_g[s], [(4, SW), (2, SW), (2, SW), (2, SW)]) for s in range(NCHIP)]
    conv_w_f, lam_f, ba_f, bx_f = [jnp.concatenate([small_parts[s][p] for s in range(NCHIP)], axis=1) for p in range(4)]
    wbd = jnp.concatenate([_block_diag_pairs(lru_wa[0, 0]), _block_diag_pairs(lru_wx[0, 0]),
                           _block_diag_pairs(lru_wa[0, 1]), _block_diag_pairs(lru_wx[0, 1])], axis=2).astype(bf16)
    conv_b_f = conv_b
    sink = attn_sink

    xn, proj = _rms_matmul("rms_proj", xs, norm_mix_g + late_token[0:1, 0:1], w_in_g, 1024)
    y_a, lru_state = _lru_fwd(proj, conv_w_f, conv_b_f, lam_f, ba_f, bx_f, wbd)
    y_b = _attn_fwd(proj, sink)
    late_src, late_land = _split_wait("gather_late_wait", _gather_copies(3), late_send, late_recv, late_src, late_land, y_b)
    chip_arr = chip.reshape(1).astype(jnp.int32)
    w_ffn_in_g, w_out_g, w_ffn_out_g = _place_own(chip_arr, late_src, late_land, 4)
    w_out_f = w_out_g.reshape(D, D)
    w_ffn_out_f = w_ffn_out_g.reshape(FF, D)
    merged, x1 = _merge_out_proj(proj, b_gate, y_a, y_b, w_out_f, xs, 512)
    xn2, gu, act = _rms_matmul_swiglu("rms_ffn_in", x1, norm_ffn_g, w_ffn_in_g, 1024)
    dx2, loss_row, dg3 = _ffn_out_loss_bwd(act, w_ffn_out_f, x1, norm_final_g.reshape(1, D), tgt, 512)

    tm = min(1024, S)
    tk = min(2048, S)
    gw_ffn_out = _mm_tn("dw_ffn_out", act, pl.BlockSpec((tk, SHW), lambda i, k: (k, i)),
                        dx2, pl.BlockSpec((tk, D), lambda i, k: (k, 0)),
                        _sds((FF, D), f32), pl.BlockSpec((SHW, D), lambda i, k: (i, 0)), (2, S // tk), (SHW, D))
    dgu = _swiglu_bwd(dx2, w_ffn_out_f, gu, 256)
    gw_ffn_in = _mm_tn("dw_ffn_in", xn2, pl.BlockSpec((tk, D), lambda g, k: (k, 0)),
                       dgu, pl.BlockSpec((None, tk, SHW), lambda g, k: (g // 2, k, g % 2)),
                       _sds((NCHIP, D, SHW), f32), pl.BlockSpec((None, D, SHW), lambda g, k: (g, 0, 0)),
                       (NCHIP, S // tk), (D, SHW))
    c_arr = cc.reshape(1).astype(jnp.int32)
    early_names, early_tiles = ["w_ffn_in", "w_ffn_out"], [256, 352]
    early = [gw_ffn_in.reshape(NCHIP, 2, D // 2, SHW), gw_ffn_out.reshape(NCHIP, 2, FF // NCHIP // 2, D)]
    ea_send, ea_recv, ea_src, ea_land, ea_token = _split_start(
        "reduce_early_sibling_start", 2, _sibling_half_copies(2), early,
        [_sds((NCHIP,) + g.shape[2:], f32) for g in early], dgu)
    dx1, dg2 = _mm_nt_rms_bwd("dxn2_rms_bwd", dgu, pl.BlockSpec((None, tm, SHW), lambda i, g: (g // 2, i, g % 2)), w_ffn_in_g,
                              x1, norm_ffn_g + ea_token[0:1, 0:1], dx2, tm)

    gw_out = _mm_tn("dw_out", merged, pl.BlockSpec((tk, D), lambda i, k: (k, 0)),
                    dx1, pl.BlockSpec((tk, D), lambda i, k: (k, 0)),
                    _sds((D, D), f32), pl.BlockSpec((D, D), lambda i, k: (0, 0)), (1, S // tk), (D, D))
    dproj, dy, db_gate = _merge_bwd(proj, b_gate, y_a, y_b, dx1, w_out_f, 512)
    ea_src, ea_land = _split_wait("reduce_early_sibling_wait", _sibling_half_copies(2), ea_send, ea_recv, ea_src, ea_land, dy)
    early_pairs = [_pair_sum("pair_sum_" + nm, c_arr, g4, r, th)
                   for nm, g4, r, th in zip(early_names, ea_src, ea_land, early_tiles)]
    eb_send, eb_recv, eb_src, eb_land, eb_token = _split_start(
        "reduce_early_chips_start", 6, _chip_part_copies(2), [p[1] for p in early_pairs],
        [_sds((3,) + p[1].shape[1:], bf16) for p in early_pairs], early_pairs[0][0])
    dproj, dsink = _attn_bwd(proj, sink + eb_token[0:1, 0:1], y_b, dy, dproj)
    _, eb_land = _split_wait("reduce_early_chips_wait", _chip_part_copies(2), eb_send, eb_recv, eb_src, eb_land, dsink)
    early_halves = [_chip_sum("chip_sum_" + nm, chip_arr, p[0], r3, th)
                    for nm, p, r3, th in zip(early_names, early_pairs, eb_land, early_tiles)]
    ec_send, ec_recv, ec_src, ec_land, ec_token = _split_start(
        "reduce_early_share_start", 2, _sibling_whole_copies(2), early_halves, [_sds(h.shape, f32) for h in early_halves], dsink)
    dproj, dcw, dcb, dlam, dba, dbx, dwbd = _lru_bwd(proj, dy, lru_state, dproj, conv_w_f, conv_b_f + ec_token[0:1, 0:1], lam_f,
                                                     ba_f, bx_f, wbd)
    early_halves, early_other = _split_wait("reduce_early_share_wait", _sibling_whole_copies(2), ec_send, ec_recv, ec_src, ec_land, dcb)
    gw_in = _mm_tn("dw_in", xn, pl.BlockSpec((tk, D), lambda g, k: (k, 0)),
                   dproj, pl.BlockSpec((tk, SHW), lambda g, k: (k, g)),
                   _sds((NCHIP, D, SHW), f32), pl.BlockSpec((None, D, SHW), lambda g, k: (g, 0, 0)),
                   (NCHIP, S // tk), (D, SHW))
    wa_send, wa_recv, wa_src, wa_land, wa_token = _split_start(
        "reduce_w_in_sibling_start", 1, _sibling_half_copies(1), [gw_in.reshape(NCHIP, 2, D // 2, SHW)],
        [_sds((NCHIP, D // 2, SHW), f32)], dproj)
    dxn =_mm_nt_groups("dxn", dproj, pl.BlockSpec((tm, SHW), lambda i, g: (i, g)), w_in_g, S, tm)
    wa_src, wa_land = _split_wait("reduce_w_in_sibling_wait", _sibling_half_copies(1), wa_send, wa_recv, wa_src, wa_land, dxn)
    w_in_pair = _pair_sum("pair_sum_w_in", c_arr, wa_src[0], wa_land[0], 256)
    wb_send, wb_recv, wb_src, wb_land, wb_token = _split_start(
        "reduce_w_in_chips_start", 3, _chip_part_copies(1), [w_in_pair[1]], [_sds((3, D // 2, SHW), bf16)], w_in_pair[0])
    grad_x, dg1 = _rms_bwd("rms_mix_bwd", xs, norm_mix_g + wb_token[0:1, 0:1], dxn, dx1, 512)
    _, wb_land = _split_wait("reduce_w_in_chips_wait", _chip_part_copies(1), wb_send, wb_recv, wb_src, wb_land, dg1)
    w_in_half = _chip_sum("chip_sum_w_in", chip_arr, w_in_pair[0], wb_land[0], 256)

    d_wa = jnp.stack([_diag_blocks(dwbd[:, :, 0:CW]), _diag_blocks(dwbd[:, :, 2 * CW:3 * CW])])
    d_wx = jnp.stack([_diag_blocks(dwbd[:, :, CW:2 * CW]), _diag_blocks(dwbd[:, :, 3 * CW:4 * CW])])
    small_full = [dg1, db_gate, dcw, dcb, dlam, d_wa, dba, d_wx, dbx, dsink[:, 0], dg2, dg3,
                  loss_row[0, 0:1]]
    full_shapes = [(1, D), (1, 2 * D), (4, D), (1, D), (2, D), (2, NH, HD, HD), (2, D), (2, NH, HD, HD), (2, D), (NH,),
                   (1, D), (1, D), (1,)]
    rows_full = _rows_for([math.prod(s) for s in full_shapes], 16)
    small_vec = _pack(small_full, rows_full)

    late_names, late_tiles = ["w_in", "w_out"], [256, 128]
    big = [gw_out.reshape(NCHIP, 2, D // NCHIP // 2, D)]
    *recv_a, small_sib = _sibling_halves(big, small_vec)
    w_out_pair = _pair_sum("pair_sum_w_out", c_arr, big[0], recv_a[0], 128)
    small_chip = _add2("pair_sum_small", small_vec, small_sib).reshape(2, rows_full // 2, 128)
    *recv_b, small_all = _exchange_chips([w_out_pair[1]], small_chip)
    w_out_half = _chip_sum("chip_sum_w_out", chip_arr, w_out_pair[0], recv_b[0], 128)
    halves = [w_in_half, w_out_half, _sum4("chip_sum_small", small_all, rows_full // 2)]
    *recv_c, small_other = _share_sibling(halves)
    small_lo = jnp.where(cc == 0, halves[2], small_other)
    small_hi = jnp.where(cc == 0, small_other, halves[2])
    g_full = _unpack(jnp.concatenate([small_lo, small_hi], axis=0), full_shapes)

    out_big = {}
    for nm, w, g_own, g_recv, m, v, th in zip(late_names + early_names, [w_in, w_out, w_ffn_in, w_ffn_out],
                                              halves[:2] + early_halves, recv_c + early_other,
                                              [m_w_in, m_w_out, m_w_ffn_in, m_w_ffn_out],
                                              [v_w_in, v_w_out, v_w_ffn_in, v_w_ffn_out], late_tiles + early_tiles):
        g_, d_, m_, v_ = _adamw_halves("adamw_" + nm, c_arr, w[0], g_own, g_recv, m[0], v[0], th)
        out_big[nm] = (g_[None], d_[None], m_[None], v_[None])

    small_names = ["norm_mix_g", "b_gate", "conv_w", "conv_b", "lru_lambda", "lru_wa", "lru_ba", "lru_wx", "lru_bx", "attn_sink",
                   "norm_ffn_g", "norm_final_g"]
    sharded = {"conv_w", "lru_lambda", "lru_ba", "lru_bx"}
    small_w = [norm_mix_g, b_gate, conv_w, conv_b, lru_lambda, lru_wa, lru_ba, lru_wx, lru_bx, attn_sink, norm_ffn_g, norm_final_g]
    small_m = [m_norm_mix_g, m_b_gate, m_conv_w, m_conv_b, m_lru_lambda, m_lru_wa, m_lru_ba, m_lru_wx, m_lru_bx, m_attn_sink,
               m_norm_ffn_g, m_norm_final_g]
    small_v = [v_norm_mix_g, v_b_gate, v_conv_w, v_conv_b, v_lru_lambda, v_lru_wa, v_lru_ba, v_lru_wx, v_lru_bx, v_attn_sink,
               v_norm_ffn_g, v_norm_final_g]
    g_local = []
    for nm, g, w in zip(small_names, g_full, small_w):
        if nm in sharded:
            g = lax.dynamic_slice_in_dim(g, chip * SW, SW, axis=1)
        g_local.append(g.reshape(w.shape))
    local_shapes = [w.shape for w in small_w]
    rows_local = _rows_for([math.prod(s) for s in local_shapes], 8)
    d_s, m_s, v_s = _adamw("adamw_small", _pack(small_w, rows_local), _pack(g_local, rows_local),
                           _pack(small_m, rows_local), _pack(small_v, rows_local), rows_local)
    d_l, m_l, v_l = _unpack(d_s, local_shapes), _unpack(m_s, local_shapes), _unpack(v_s, local_shapes)
    res = {nm: (g_local[i], d_l[i], m_l[i], v_l[i]) for i, nm in enumerate(small_names)}
    res.update(out_big)

    order = ["norm_mix_g", "w_in", "b_gate", "conv_w", "conv_b", "lru_lambda", "lru_wa", "lru_ba", "lru_wx", "lru_bx", "attn_sink",
             "w_out", "norm_ffn_g", "w_ffn_in", "w_ffn_out", "norm_final_g"]
    outs = [g_full[-1][0], grad_x[None]]
    for k in range(4):
        outs += [res[nm][k] for nm in order]
    return tuple(outs)
```

```python
import functools
import math

import jax
import jax.numpy as jnp
from jax import lax
from jax.experimental import pallas as pl
from jax.experimental.pallas import tpu as pltpu

f32 = jnp.float32
bf16 = jnp.bfloat16

D = 1024
NH = 16
HD = 64
FF = 2816
INW = 5632
NCHIP = 4
SHW = INW // NCHIP
CW = 128
NCH = D // CW
BLK = 128
EPS = 1e-6
NEG_INF = -1e30
RGLRU_C = 8.0
ADAM_LR, ADAM_B1, ADAM_B2, ADAM_EPS, ADAM_WD, ADAM_STEP = 0.001, 0.9, 0.999, 1e-08, 0.01, 10
VMEM_LIMIT = 58 * 1024 * 1024
MESH = pl.DeviceIdType.MESH
ANY = pl.BlockSpec(memory_space=pl.ANY)

COL_U, COL_G, COL_Q, COL_K, COL_V, COL_Z0, COL_Z1 = 0, 4, 8, 12, 13, 14, 18
MERGE_W = 512
MERGE_Z0, MERGE_Z1 = (COL_Z0 * 256) // MERGE_W, (COL_Z1 * 256) // MERGE_W


def _params(n_axes, vmem=False):
    return pltpu.CompilerParams(dimension_semantics=("arbitrary",) * n_axes,
                                vmem_limit_bytes=VMEM_LIMIT if vmem else None)


def _sds(shape, dtype):
    return jax.ShapeDtypeStruct(tuple(shape), dtype)


_DIMS = {"nn": (((1,), (0,)), ((), ())), "nt": (((1,), (1,)), ((), ())), "tn": (((0,), (0,)), ((), ()))}


def _mm(name, mode, a, a_spec, b, b_spec, out_shape, out_spec, grid, nk, acc_shape):
    def body(*refs):
        a_ref, b_ref, o_ref = refs[0], refs[1], refs[2]
        part = lax.dot_general(a_ref[...].astype(bf16), b_ref[...].astype(bf16), _DIMS[mode],
                               preferred_element_type=f32)
        if nk == 1:
            o_ref[...] = part.astype(o_ref.dtype)
            return
        acc_ref = refs[3]
        k = pl.program_id(len(grid) - 1)

        @pl.when(k == 0)
        def _():
            acc_ref[...] = part

        @pl.when(k > 0)
        def _():
            acc_ref[...] += part

        @pl.when(k == nk - 1)
        def _():
            o_ref[...] = acc_ref[...].astype(o_ref.dtype)

    scratch = [pltpu.VMEM(acc_shape, f32)] if nk > 1 else []
    return pl.pallas_call(body, name=name, grid=grid, in_specs=[a_spec, b_spec], out_specs=out_spec, out_shape=out_shape,
                          scratch_shapes=scratch, compiler_params=_params(len(grid), True))(a, b)


def _rms_matmul(name, x, g, w3, tm):
    S, K = x.shape
    G, _, Nw = w3.shape
    tm = min(tm, S)

    def body(x_ref, g_ref, w_ref, xn_ref, o_ref):
        xf = x_ref[...]
        r = lax.rsqrt(jnp.mean(xf * xf, axis=-1, keepdims=True) + EPS)
        xn = ((xf * r) * g_ref[...]).astype(bf16)
        xn_ref[...] = xn
        for j in range(G):
            o_ref[:, j * Nw:(j + 1) * Nw] = jnp.dot(xn, w_ref[j], preferred_element_type=f32).astype(bf16)

    return pl.pallas_call(
        body, name=name, grid=(S // tm,),
        in_specs=[pl.BlockSpec((tm, K), lambda i: (i, 0)), pl.BlockSpec((1, K), lambda i: (0, 0)),
                  pl.BlockSpec((G, K, Nw), lambda i: (0, 0, 0))],
        out_specs=[pl.BlockSpec((tm, K), lambda i: (i, 0)), pl.BlockSpec((tm, G * Nw), lambda i: (i, 0))],
        out_shape=[_sds((S, K), bf16), _sds((S, G * Nw), bf16)],
        compiler_params=_params(1, True))(x, g, w3)


def _rms_matmul_swiglu(name, x, g, w3, tm):
    S, K = x.shape
    G, _, Nw = w3.shape
    tm = min(tm, S)
    half = G // 2

    def body(x_ref, g_ref, w_ref, xn_ref, gu_ref, act_ref):
        xf = x_ref[...]
        r = lax.rsqrt(jnp.mean(xf * xf, axis=-1, keepdims=True) + EPS)
        xn = ((xf * r) * g_ref[...]).astype(bf16)
        xn_ref[...] = xn
        for j in range(half):
            cols = slice(j * Nw, (j + 1) * Nw)
            gate = jnp.dot(xn, w_ref[j], preferred_element_type=f32)
            up = jnp.dot(xn, w_ref[half + j], preferred_element_type=f32)
            gu_ref[0, :, cols] = gate.astype(bf16)
            gu_ref[1, :, cols] = up.astype(bf16)
            act_ref[:, cols] = ((gate * _sigmoid(gate)) * up).astype(bf16)

    return pl.pallas_call(
        body, name=name, grid=(S // tm,),
        in_specs=[pl.BlockSpec((tm, K), lambda i: (i, 0)), pl.BlockSpec((1, K), lambda i: (0, 0)),
                  pl.BlockSpec((G, K, Nw), lambda i: (0, 0, 0))],
        out_specs=[pl.BlockSpec((tm, K), lambda i: (i, 0)), pl.BlockSpec((2, tm, half * Nw), lambda i: (0, i, 0)),
                   pl.BlockSpec((tm, half * Nw), lambda i: (i, 0))],
        out_shape=[_sds((S, K), bf16), _sds((2, S, half * Nw), bf16), _sds((S, half * Nw), bf16)],
        compiler_params=_params(1, True))(x, g, w3)


def _mm_nt_groups(name, a, w3, tm):
    S = a.shape[0]
    G, Dout, Kw = w3.shape
    tm = min(tm, S)

    def body(a_ref, w_ref, o_ref):
        acc = None
        for g in range(G):
            part = lax.dot_general(a_ref[:, g * Kw:(g + 1) * Kw], w_ref[g], _DIMS["nt"], preferred_element_type=f32)
            acc = part if acc is None else acc + part
        o_ref[...] = acc

    return pl.pallas_call(body, name=name, grid=(S // tm,),
                          in_specs=[pl.BlockSpec((tm, G * Kw), lambda i: (i, 0)), pl.BlockSpec((G, Dout, Kw), lambda i: (0, 0, 0))],
                          out_specs=pl.BlockSpec((tm, Dout), lambda i: (i, 0)), out_shape=_sds((S, Dout), f32),
                          compiler_params=_params(1, True))(a, w3)


def _mm_tn(name, a, a_spec, b, b_spec, out_shape, out_spec, grid, acc_shape):
    return _mm(name, "tn", a, a_spec, b, b_spec, out_shape, out_spec, grid, grid[-1], acc_shape)


def _sigmoid(x):
    return 0.5 * jnp.tanh(0.5 * x) + 0.5


_GELU_C = math.sqrt(2.0 / math.pi)


def _gelu_and_grad(x):
    v = _GELU_C * (x + 0.044715 * (x * x * x))
    t = jnp.tanh(v)
    gl = 0.5 * x * (1.0 + t)
    dgl = 0.5 * (1.0 + t) + 0.5 * x * (1.0 - t * t) * (_GELU_C * (1.0 + 3.0 * 0.044715 * (x * x)))
    return gl, dgl


def _one_minus_exp2x(x, ex):
    y = 2.0 * x
    series = y * (1.0 + y * (0.5 + y * (1.0 / 6.0 + y * (1.0 / 24.0))))
    return jnp.where(y > -1.0 / 64.0, -series, 1.0 - ex * ex)


def _z_specs(tm):
    return [pl.BlockSpec((tm, MERGE_W), lambda i, p=p: (i, MERGE_Z0 + p)) for p in range(2 * D // MERGE_W)]


def _merge_out_proj(proj, b_gate, y_a, y_b, w, res, tm):
    S = proj.shape[0]
    tm = min(tm, S)
    per = D // MERGE_W
    nz = 2 * per

    def body(*refs):
        z = refs[:nz]
        b_ref, ya_ref, yb_ref, w_ref, r_ref, m_ref, x_ref = refs[nz:]
        for p in range(per):
            cols = slice(p * MERGE_W, (p + 1) * MERGE_W)
            g0 = _sigmoid(z[p][...].astype(f32) + b_ref[:, p * MERGE_W:(p + 1) * MERGE_W])
            g1 = _sigmoid(z[per + p][...].astype(f32) + b_ref[:, D + p * MERGE_W:D + (p + 1) * MERGE_W])
            m_ref[:, cols] = (g0 * ya_ref[:, cols].astype(f32) + g1 * yb_ref[:, cols].astype(f32)).astype(bf16)
        x_ref[...] = r_ref[...] + jnp.dot(m_ref[...], w_ref[...], preferred_element_type=f32)

    row = pl.BlockSpec((tm, D), lambda i: (i, 0))
    return pl.pallas_call(
        body, name="merge_out_proj", grid=(S // tm,),
        in_specs=_z_specs(tm) + [pl.BlockSpec((1, 2 * D), lambda i: (0, 0)), row, row, pl.BlockSpec((D, D), lambda i: (0, 0)), row],
        out_specs=[row, row], out_shape=[_sds((S, D), bf16), _sds((S, D), f32)],
        compiler_params=_params(1, True))(*([proj] * nz), b_gate, y_a, y_b, w, res)


def _merge_bwd(proj, b_gate, y_a, y_b, dx, w, tm):
    S = proj.shape[0]
    tm = min(tm, S)
    per = D // MERGE_W
    nz = 2 * per
    nsteps = S // tm
    z_col = MERGE_Z0 * MERGE_W

    def body(*refs):
        z = refs[:nz]
        b_ref, ya_ref, yb_ref, dx_ref, w_ref, dproj_ref, dy_ref, db_ref, dz_buf, sems = refs[nz:]
        i = pl.program_id(0)
        slot = i % 2

        def dz_copy(step):
            rows = pl.ds(pl.multiple_of(step * tm, tm), tm)
            return pltpu.make_async_copy(dz_buf.at[step % 2], dproj_ref.at[rows, pl.ds(z_col, 2 * D)], sems.at[step % 2])

        @pl.when(i >= 2)
        def _():
            dz_copy(i - 2).wait()

        @pl.when(i == 0)
        def _():
            db_ref[...] = jnp.zeros_like(db_ref)

        dm = lax.dot_general(dx_ref[...].astype(bf16), w_ref[...], _DIMS["nt"], preferred_element_type=f32)
        for p in range(nz):
            branch, cols = p // per, slice((p % per) * MERGE_W, (p % per + 1) * MERGE_W)
            zc = slice(p * MERGE_W, (p + 1) * MERGE_W)
            g = _sigmoid(z[p][...].astype(f32) + b_ref[:, zc])
            d = dm[:, cols]
            y = (ya_ref if branch == 0 else yb_ref)[:, cols].astype(f32)
            dz = (d * y) * (g * (1.0 - g))
            dz_buf[slot, :, zc] = dz.astype(bf16)
            dy_ref[branch, :, cols] = (d * g).astype(bf16)
            db_ref[:, zc] += jnp.sum(dz, axis=0, keepdims=True)
        dz_copy(i).start()

        @pl.when(i == nsteps - 1)
        def _():
            if nsteps >= 2:
                dz_copy(i - 1).wait()
            dz_copy(i).wait()

    row = pl.BlockSpec((tm, D), lambda i: (i, 0))
    return pl.pallas_call(
        body, name="merge_bwd", grid=(nsteps,),
        in_specs=_z_specs(tm) + [pl.BlockSpec((1, 2 * D), lambda i: (0, 0)), row, row, row, pl.BlockSpec((D, D), lambda i: (0, 0))],
        out_specs=[ANY, pl.BlockSpec((2, tm, D), lambda i: (0, i, 0)), pl.BlockSpec((1, 2 * D), lambda i: (0, 0))],
        out_shape=[_sds((S, INW), bf16), _sds((2, S, D), bf16), _sds((1, 2 * D), f32)],
        scratch_shapes=[pltpu.VMEM((2, tm, 2 * D), bf16), pltpu.SemaphoreType.DMA((2,))],
        compiler_params=_params(1, True))(*([proj] * nz), b_gate, y_a, y_b, dx, w)


def _swiglu_bwd(dx, w, gu, tm):
    S, K = dx.shape
    tm = min(tm, S)

    def body(dx_ref, w_ref, gu_ref, o_ref):
        d = lax.dot_general(dx_ref[...].astype(bf16), w_ref[...], _DIMS["nt"], preferred_element_type=f32)
        g = gu_ref[0].astype(f32)
        u = gu_ref[1].astype(f32)
        s = _sigmoid(g)
        o_ref[0] = ((d * u) * (s * (1.0 + g * (1.0 - s)))).astype(bf16)
        o_ref[1] = (d * (g * s)).astype(bf16)

    stacked = pl.BlockSpec((2, tm, FF), lambda i: (0, i, 0))
    return pl.pallas_call(body, name="swiglu_bwd", grid=(S // tm,),
                          in_specs=[pl.BlockSpec((tm, K), lambda i: (i, 0)), pl.BlockSpec((FF, K), lambda i: (0, 0)), stacked],
                          out_specs=stacked, out_shape=_sds((2, S, FF), bf16),
                          compiler_params=_params(1, True))(dx, w, gu)


def _ffn_out_loss_bwd(act, w, x1, g3, tgt, tm):
    S, K = act.shape
    tm = min(tm, S)

    def body(a_ref, w_ref, r_ref, g_ref, t_ref, dx_ref, loss_ref, dg_ref):
        @pl.when(pl.program_id(0) == 0)
        def _():
            loss_ref[...] = jnp.zeros_like(loss_ref)
            dg_ref[...] = jnp.zeros_like(dg_ref)

        x = r_ref[...] + jnp.dot(a_ref[...], w_ref[...], preferred_element_type=f32)
        g = g_ref[...]
        r = lax.rsqrt(jnp.mean(x * x, axis=-1, keepdims=True) + EPS)
        xh = x * r
        err = xh * g - t_ref[...]
        row = jnp.mean(err * err, axis=-1, keepdims=True)
        loss_ref[...] += 0.5 * jnp.sum(row, axis=0, keepdims=True)
        dy = err * (1.0 / D)
        dg_ref[...] += jnp.sum(dy * xh, axis=0, keepdims=True)
        dxh = dy * g
        dx_ref[...] = r * (dxh - xh * jnp.mean(dxh * xh, axis=-1, keepdims=True))

    row_blk = pl.BlockSpec((tm, D), lambda i: (i, 0))
    vec = pl.BlockSpec((1, D), lambda i: (0, 0))
    return pl.pallas_call(body, name="ffn_out_loss_bwd", grid=(S // tm,),
                          in_specs=[pl.BlockSpec((tm, K), lambda i: (i, 0)), pl.BlockSpec((K, D), lambda i: (0, 0)),
                                    row_blk, vec, row_blk],
                          out_specs=[row_blk, pl.BlockSpec((1, 128), lambda i: (0, 0)), vec],
                          out_shape=[_sds((S, D), f32), _sds((1, 128), f32), _sds((1, D), f32)],
                          compiler_params=_params(1, True))(act, w, x1, g3, tgt)


def _rms_bwd(name, x, g, dxn, dres, tm):
    S = x.shape[0]
    tm = min(tm, S)

    def body(x_ref, g_ref, d_ref, r_ref, dx_ref, dg_ref):
        @pl.when(pl.program_id(0) == 0)
        def _():
            dg_ref[...] = jnp.zeros_like(dg_ref)

        x = x_ref[...]
        d = d_ref[...]
        r = lax.rsqrt(jnp.mean(x * x, axis=-1, keepdims=True) + EPS)
        xh = x * r
        dg_ref[...] += jnp.sum(d * xh, axis=0, keepdims=True)
        dxh = d * g_ref[...]
        dx_ref[...] = r_ref[...] + r * (dxh - xh * jnp.mean(dxh * xh, axis=-1, keepdims=True))

    row_blk = pl.BlockSpec((tm, D), lambda i: (i, 0))
    vec = pl.BlockSpec((1, D), lambda i: (0, 0))
    return pl.pallas_call(body, name=name, grid=(S // tm,), in_specs=[row_blk, vec, row_blk, row_blk],
                          out_specs=[row_blk, vec], out_shape=[_sds((S, D), f32), _sds((1, D), f32)],
                          compiler_params=_params(1))(x, g, dxn, dres)


def _mm_nt_rms_bwd(name, a2, w3, x, g, dres, tm):
    S = x.shape[0]
    G, Dout, Kw = w3.shape
    tm = min(tm, S)
    per = G // 2

    def body(a_ref, w_ref, x_ref, g_ref, r_ref, dx_ref, dg_ref):
        @pl.when(pl.program_id(0) == 0)
        def _():
            dg_ref[...] = jnp.zeros_like(dg_ref)

        d = None
        for k in range(G):
            cols = slice((k % per) * Kw, (k % per + 1) * Kw)
            part = lax.dot_general(a_ref[k // per, :, cols], w_ref[k], _DIMS["nt"], preferred_element_type=f32)
            d = part if d is None else d + part
        x_t = x_ref[...]
        r = lax.rsqrt(jnp.mean(x_t * x_t, axis=-1, keepdims=True) + EPS)
        xh = x_t * r
        dg_ref[...] += jnp.sum(d * xh, axis=0, keepdims=True)
        dxh = d * g_ref[...]
        dx_ref[...] = r_ref[...] + r * (dxh - xh * jnp.mean(dxh * xh, axis=-1, keepdims=True))

    row_blk = pl.BlockSpec((tm, Dout), lambda i: (i, 0))
    vec = pl.BlockSpec((1, Dout), lambda i: (0, 0))
    return pl.pallas_call(body, name=name, grid=(S // tm,),
                          in_specs=[pl.BlockSpec((2, tm, per * Kw), lambda i: (0, i, 0)),
                                    pl.BlockSpec((G, Dout, Kw), lambda i: (0, 0, 0)), row_blk, vec, row_blk],
                          out_specs=[row_blk, vec], out_shape=[_sds((S, Dout), f32), _sds((1, Dout), f32)],
                          compiler_params=_params(1, True))(a2, w3, x, g, dres)


LRU_TT = 256
SCAN_UNROLL = 8


HALO = 16


def _halo(ref, i, S):
    nt = S // LRU_TT
    t0 = pl.multiple_of(i * LRU_TT, LRU_TT)
    p0 = pl.multiple_of(jnp.maximum(t0 - HALO, 0), HALO)
    n0 = pl.multiple_of(jnp.minimum(t0 + LRU_TT, S - HALO), HALO)
    prev = jnp.where(i > 0, ref[pl.ds(p0, HALO), :].astype(f32), 0.0)
    nxt = jnp.where(i < nt - 1, ref[pl.ds(n0, HALO), :].astype(f32), 0.0)
    return jnp.concatenate([prev, ref[pl.ds(t0, LRU_TT), :].astype(f32), nxt], axis=0)


def _shift(ext, k):
    n = LRU_TT + 2 * HALO
    return pltpu.roll(ext, (-k) % n, 0)[HALO:HALO + LRU_TT]


def _lru_gates(uc, wbd, ba, bx):
    pre = jnp.dot(uc.astype(bf16), wbd, preferred_element_type=f32)
    r_f = _sigmoid(pre[:, 0:CW] + ba[0:1])
    i_f = _sigmoid(pre[:, CW:2 * CW] + bx[0:1])
    r_b = _sigmoid(pre[:, 2 * CW:3 * CW] + ba[1:2])
    i_b = _sigmoid(pre[:, 3 * CW:4 * CW] + bx[1:2])
    return r_f, i_f, r_b, i_b


def _lru_coeffs(r, sp):
    log_a = (-RGLRU_C * r) * sp
    a = jnp.exp(log_a)
    beta = jnp.sqrt(jnp.maximum(_one_minus_exp2x(log_a, a), 0.0))
    return a, beta


def _lru_coeffs_inv(r, sp):
    log_a = (-RGLRU_C * r) * sp
    a = jnp.exp(log_a)
    om = jnp.maximum(_one_minus_exp2x(log_a, a), 0.0)
    return a, jnp.sqrt(om), lax.rsqrt(jnp.maximum(om, 1e-30))


def _conv_tile(u_ref, i, S, cw, cb):
    ext = _halo(u_ref, i, S)
    um2, um1, u0, up1 = _shift(ext, -2), _shift(ext, -1), ext[HALO:HALO + LRU_TT], _shift(ext, 1)
    uc = um2 * cw[0:1] + um1 * cw[1:2] + u0 * cw[2:3] + up1 * cw[3:4] + cb
    return uc, (um2, um1, u0, up1)


def _scan_pair(S, fwd_a, fwd_b, fwd_out, rev_a, rev_b, rev_out):
    ng = S // 8
    idx = lax.broadcasted_iota(jnp.int32, (8, CW), 0)

    def local(a, b, rev):
        for sh in (1, 2, 4):
            if rev:
                keep = idx < 8 - sh
                amt = 8 - sh
            else:
                keep = idx >= sh
                amt = sh
            a_s = jnp.where(keep, pltpu.roll(a, amt, 0), 1.0)
            b_s = jnp.where(keep, pltpu.roll(b, amt, 0), 0.0)
            b = a * b_s + b
            a = a * a_s
        return a, b

    def step(it, carry):
        cf, cr = carry
        fwd_rows = [pl.multiple_of((it * SCAN_UNROLL + j) * 8, 8) for j in range(SCAN_UNROLL)]
        rev_rows = [pl.multiple_of((ng - 1 - (it * SCAN_UNROLL + j)) * 8, 8) for j in range(SCAN_UNROLL)]
        fwd_loc = [local(fwd_a(r), fwd_b(r), False) for r in fwd_rows]
        rev_loc = [local(rev_a(r), rev_b(r), True) for r in rev_rows]
        for j in range(SCAN_UNROLL):
            a, b = fwd_loc[j]
            h = a * cf + b
            fwd_out[pl.ds(fwd_rows[j], 8), :] = h
            cf = jnp.broadcast_to(h[7:8, :], (8, CW))
            a, b = rev_loc[j]
            h = a * cr + b
            rev_out[pl.ds(rev_rows[j], 8), :] = h
            cr = jnp.broadcast_to(h[0:1, :], (8, CW))
        return cf, cr

    zero = jnp.zeros((8, CW), f32)
    lax.fori_loop(0, ng // SCAN_UNROLL, step, (zero, zero))


def _lru_specs(S):
    seq = lambda off: pl.BlockSpec((S, CW), lambda j: (0, off + j))
    par = lambda rows: pl.BlockSpec((rows, CW), lambda j: (0, j))
    return seq, par


def _lru_fwd(proj, conv_w, conv_b, lam, ba, bx, wbd):
    S = proj.shape[0]
    nt = S // LRU_TT

    def body(u_ref, g_ref, cw_ref, cb_ref, lam_ref, ba_ref, bx_ref, wbd_ref, y_ref, state_ref, af_ref, bf_ref, ab_ref, bb_ref,
             sems):
        cw, cb, ba_v, bx_v, wbd_v = cw_ref[...], cb_ref[...], ba_ref[...], bx_ref[...], wbd_ref[...]
        sp = jax.nn.softplus(-lam_ref[...])
        cols = pl.ds(pl.multiple_of(pl.program_id(0) * CW, CW), CW)
        save = [pltpu.make_async_copy(ref, state_ref.at[k, :, cols], sems.at[k])
                for k, ref in enumerate((af_ref, bf_ref, ab_ref, bb_ref))]

        def phase1(i, c):
            uc, _ = _conv_tile(u_ref, i, S, cw, cb)
            r_f, i_f, r_b, i_b = _lru_gates(uc, wbd_v, ba_v, bx_v)
            rows = pl.ds(pl.multiple_of(i * LRU_TT, LRU_TT), LRU_TT)
            a, beta = _lru_coeffs(r_f, sp[0:1])
            af_ref[rows, :] = a
            bf_ref[rows, :] = beta * (i_f * uc)
            a, beta = _lru_coeffs(r_b, sp[1:2])
            ab_ref[rows, :] = a
            bb_ref[rows, :] = beta * (i_b * uc)
            return c

        lax.fori_loop(0, nt, phase1, 0)
        save[0].start()
        save[2].start()
        row8 = lambda ref: (lambda r0: ref[pl.ds(r0, 8), :])
        _scan_pair(S, row8(af_ref), row8(bf_ref), bf_ref, row8(ab_ref), row8(bb_ref), bb_ref)
        save[1].start()
        save[3].start()

        def phase3(i, c):
            rows = pl.ds(pl.multiple_of(i * LRU_TT, LRU_TT), LRU_TT)
            y = (bf_ref[rows, :] + bb_ref[rows, :]) * jax.nn.gelu(g_ref[rows, :].astype(f32))
            y_ref[rows, :] = y.astype(y_ref.dtype)
            return c

        lax.fori_loop(0, nt, phase3, 0)
        for cp in save:
            cp.wait()

    seq, par = _lru_specs(S)
    return pl.pallas_call(
        body, name="lru_fwd", grid=(NCH,),
        in_specs=[seq(0), seq(NCH), par(4), par(1), par(2), par(2), par(2),
                  pl.BlockSpec((None, CW, 4 * CW), lambda j: (j, 0, 0))],
        out_specs=[seq(0), ANY], out_shape=[_sds((S, D), bf16), _sds((4, S, D), f32)],
        scratch_shapes=[pltpu.VMEM((S, CW), f32)] * 4 + [pltpu.SemaphoreType.DMA((4,))], compiler_params=_params(1, True),
    )(proj, proj, conv_w, conv_b, lam, ba, bx, wbd)


def _lru_bwd(proj, dy, state, dproj, conv_w, conv_b, lam, ba, bx, wbd):
    S = proj.shape[0]
    nt = S // LRU_TT

    def body(u_ref, g_ref, dy_ref, state_ref, dproj_in, cw_ref, cb_ref, lam_ref, ba_ref, bx_ref, wbd_ref,
             dproj_ref, dcw_ref, dcb_ref, dlam_ref, dba_ref, dbx_ref, dwbd_ref,
             af_ref, hf2_ref, ab_ref, hb2_ref, dh_ref, du_ref, dg_ref, sems):
        cw, cb, ba_v, bx_v, wbd_v = cw_ref[...], cb_ref[...], ba_ref[...], bx_ref[...], wbd_ref[...]
        lam_v = lam_ref[...]
        sp = jax.nn.softplus(-lam_v)
        chunk = pl.program_id(0)
        slot = chunk % 2
        bf_ref, bb_ref = hf2_ref.at[slot], hb2_ref.at[slot]

        def out_copies(j):
            c0 = pl.multiple_of(j * CW, CW)
            return [pltpu.make_async_copy(du_ref, dproj_ref.at[:, pl.ds(c0, CW)], sems.at[4]),
                    pltpu.make_async_copy(dg_ref, dproj_ref.at[:, pl.ds(D + c0, CW)], sems.at[5])]

        @pl.when(chunk >= 1)
        def _():
            for cp in out_copies(chunk - 1):
                cp.wait()

        def state_copy(k, j, dst, sem):
            return pltpu.make_async_copy(state_ref.at[k, :, pl.ds(pl.multiple_of(j * CW, CW), CW)], dst, sem)

        def hidden_loads(j):
            return [state_copy(1, j, hf2_ref.at[j % 2], sems.at[6 + j % 2]), state_copy(3, j, hb2_ref.at[j % 2], sems.at[8 + j % 2])]

        load = [state_copy(0, chunk, af_ref, sems.at[0]), None, state_copy(2, chunk, ab_ref, sems.at[2])]

        @pl.when(chunk == 0)
        def _():
            for cp in hidden_loads(chunk):
                cp.start()

        load[0].start()
        load[2].start()

        @pl.when(chunk + 1 < NCH)
        def _():
            for cp in hidden_loads(chunk + 1):
                cp.start()

        for cp in hidden_loads(chunk):
            cp.wait()
        row8 = lambda ref: (lambda r0: ref[pl.ds(r0, 8), :])

        def phase0(i, c):
            rows = pl.ds(pl.multiple_of(i * LRU_TT, LRU_TT), LRU_TT)
            gl, dgl = _gelu_and_grad(g_ref[rows, :].astype(f32))
            dyt = dy_ref[rows, :].astype(f32)
            dh_ref[rows, :] = dyt * gl
            dg_ref[rows, :] = ((dyt * (bf_ref[rows, :] + bb_ref[rows, :])) * dgl).astype(dg_ref.dtype)
            return c

        lax.fori_loop(0, nt, phase0, 0)
        load[0].wait()
        load[2].wait()

        def scaled_dh(a_ref):
            def f(r0):
                return a_ref[pl.ds(r0, 8), :] * dh_ref[pl.ds(r0, 8), :]
            return f

        _scan_pair(S, row8(ab_ref), scaled_dh(ab_ref), ab_ref, row8(af_ref), scaled_dh(af_ref), af_ref)

        dcw_ref[...] = jnp.zeros_like(dcw_ref)
        dcb_ref[...] = jnp.zeros_like(dcb_ref)
        dlam_ref[...] = jnp.zeros_like(dlam_ref)
        dba_ref[...] = jnp.zeros_like(dba_ref)
        dbx_ref[...] = jnp.zeros_like(dbx_ref)
        dwbd_ref[...] = jnp.zeros_like(dwbd_ref)

        def direction(uc, r, i_g, dht, h_nb, sp_d):
            a, beta, inv_beta = _lru_coeffs_inv(r, sp_d)
            da = dht * h_nb
            dbeta = dht * (i_g * uc)
            d_iu = dht * beta
            dlog_a = da * a - (a * a) * (dbeta * inv_beta)
            dlr = dlog_a * r
            dsp = -RGLRU_C * jnp.sum(dlr, axis=0, keepdims=True)
            dpre_r = (dlr * (1.0 - r)) * (-RGLRU_C * sp_d)
            dpre_i = (d_iu * uc) * (i_g * (1.0 - i_g))
            return dpre_r, dpre_i, d_iu * i_g, dsp

        def phase4(i, c):
            uc, (um2, um1, u0, up1) = _conv_tile(u_ref, i, S, cw, cb)
            r_f, i_f, r_b, i_b = _lru_gates(uc, wbd_v, ba_v, bx_v)
            rows = pl.ds(pl.multiple_of(i * LRU_TT, LRU_TT), LRU_TT)
            dh = dh_ref[rows, :]
            dht_f = dh + _shift(_halo(af_ref, i, S), 1)
            h_prev = _shift(_halo(bf_ref, i, S), -1)
            dht_b = dh + _shift(_halo(ab_ref, i, S), -1)
            h_next = _shift(_halo(bb_ref, i, S), 1)
            prf, pif, duc_f, dsp_f = direction(uc, r_f, i_f, dht_f, h_prev, sp[0:1])
            prb, pib, duc_b, dsp_b = direction(uc, r_b, i_b, dht_b, h_next, sp[1:2])
            dpre = jnp.concatenate([prf, pif, prb, pib], axis=1)
            dpre_b = dpre.astype(bf16)
            duc = (duc_f + duc_b) + lax.dot_general(dpre_b, wbd_v, _DIMS["nt"], preferred_element_type=f32)
            dwbd_ref[...] += lax.dot_general(uc.astype(bf16), dpre_b, _DIMS["tn"], preferred_element_type=f32)
            colsum = lambda v: jnp.sum(v, axis=0, keepdims=True)
            dba_ref[...] += jnp.concatenate([colsum(prf), colsum(prb)], axis=0)
            dbx_ref[...] += jnp.concatenate([colsum(pif), colsum(pib)], axis=0)
            dlam_ref[...] += jnp.concatenate([dsp_f, dsp_b], axis=0)
            dcb_ref[...] += colsum(duc)
            dcw_ref[...] += jnp.concatenate([colsum(duc * um2), colsum(duc * um1), colsum(duc * u0),
                                             colsum(duc * up1)], axis=0)
            af_ref[rows, :] = duc
            return c

        lax.fori_loop(0, nt, phase4, 0)
        dlam_ref[...] = dlam_ref[...] * (-_sigmoid(-lam_v))

        def phase5(i, c):
            ext = _halo(af_ref, i, S)
            rows = pl.ds(pl.multiple_of(i * LRU_TT, LRU_TT), LRU_TT)
            du = (_shift(ext, 2) * cw[0:1] + _shift(ext, 1) * cw[1:2] + ext[HALO:HALO + LRU_TT] * cw[2:3]
                  + _shift(ext, -1) * cw[3:4])
            du_ref[rows, :] = du.astype(du_ref.dtype)
            return c

        lax.fori_loop(0, nt, phase5, 0)
        for cp in out_copies(chunk):
            cp.start()

        @pl.when(chunk == NCH - 1)
        def _():
            for cp in out_copies(chunk):
                cp.wait()

    seq, par = _lru_specs(S)
    return pl.pallas_call(
        body, name="lru_bwd", grid=(NCH,),
        in_specs=[seq(0), seq(NCH), pl.BlockSpec((None, S, CW), lambda j: (0, 0, j)), ANY, ANY,
                  par(4), par(1), par(2), par(2), par(2), pl.BlockSpec((None, CW, 4 * CW), lambda j: (j, 0, 0))],
        out_specs=[ANY, par(4), par(1), par(2), par(2), par(2),
                   pl.BlockSpec((None, CW, 4 * CW), lambda j: (j, 0, 0))],
        out_shape=[_sds(dproj.shape, bf16), _sds((4, D), f32), _sds((1, D), f32), _sds((2, D), f32),
                   _sds((2, D), f32), _sds((2, D), f32), _sds((NCH, CW, 4 * CW), f32)],
        scratch_shapes=[pltpu.VMEM((S, CW), f32), pltpu.VMEM((2, S, CW), f32), pltpu.VMEM((S, CW), f32), pltpu.VMEM((2, S, CW), f32),
                        pltpu.VMEM((S, CW), f32), pltpu.VMEM((S, CW), bf16), pltpu.VMEM((S, CW), bf16),
                        pltpu.SemaphoreType.DMA((10,))],
        input_output_aliases={4: 0}, compiler_params=_params(1, True),
    )(proj, proj, dy, state, dproj, conv_w, conv_b, lam, ba, bx, wbd)


_SLOPES = [2.0 ** (-8.0 * (h + 1) / NH) for h in range(NH)]


def _half_mask(shape, e):
    lane = lax.broadcasted_iota(jnp.int32, shape, 1)
    return (lane < HD) if e == 0 else (lane >= HD)


def _both_halves(x, src):
    return jnp.where(_half_mask(x.shape, src), x, pltpu.roll(x, HD, 1))


def _attn_base(n, S):
    tq = lax.broadcasted_iota(jnp.int32, (BLK, 3 * BLK), 0)
    sk = lax.broadcasted_iota(jnp.int32, (BLK, 3 * BLK), 1)
    dist = jnp.abs(tq + BLK - sk)
    kpos = n * BLK - BLK + sk
    valid = (dist <= BLK) & (kpos >= 0) & (kpos < S)
    return jnp.where(valid, -dist.astype(f32), NEG_INF)


def _group_heads(ref, kvh, scale):
    parts = []
    for i in range(4):
        pair = 2 * kvh + i // 2
        x = ref[:, pair * 128:(pair + 1) * 128].astype(f32)
        parts.append(jnp.where(_half_mask(x.shape, i % 2), x * scale, 0.0))
    return parts


def _stack_bf16(parts):
    return jnp.concatenate([p.astype(bf16) for p in parts], axis=0)


def _attn_softmax(s_raw, base, slope, sink):
    s = s_raw + slope * base
    m = jnp.maximum(jnp.max(s, axis=-1, keepdims=True), sink)
    p = jnp.exp(s - m)
    esink = jnp.exp(sink - m)
    inv = 1.0 / (jnp.sum(p, axis=-1, keepdims=True) + esink)
    return p, inv, esink * inv


def _attn_specs(S):
    nb = S // BLK
    q_spec = pl.BlockSpec((BLK, D), lambda n: (n, 2))
    kv = lambda col: [pl.BlockSpec((BLK, 256), lambda n: (jnp.maximum(n - 1, 0), col)),
                      pl.BlockSpec((BLK, 256), lambda n: (n, col)),
                      pl.BlockSpec((BLK, 256), lambda n: (jnp.minimum(n + 1, nb - 1), col))]
    return nb, q_spec, kv(COL_K), kv(COL_V)


def _attn_fwd(proj, sink):
    S = proj.shape[0]
    nb, q_spec, k_specs, v_specs = _attn_specs(S)

    def body(sink_ref, q_ref, kp_ref, kc_ref, kn_ref, vp_ref, vc_ref, vn_ref, o_ref):
        base = _attn_base(pl.program_id(0), S)
        kcat = jnp.concatenate([kp_ref[...], kc_ref[...], kn_ref[...]], axis=0).astype(f32)
        vcat = jnp.concatenate([vp_ref[...], vc_ref[...], vn_ref[...]], axis=0).astype(f32)
        even = _half_mask((BLK, 128), 0)
        for kvh in range(NH // 4):
            ch, off = kvh // 2, kvh % 2
            kb = _both_halves(kcat[:, ch * 128:(ch + 1) * 128], off).astype(bf16)
            vb = _both_halves(vcat[:, ch * 128:(ch + 1) * 128], off).astype(bf16)
            q4 = _stack_bf16(_group_heads(q_ref, kvh, HD ** -0.5))
            s4 = lax.dot_general(q4, kb, _DIMS["nt"], preferred_element_type=f32)
            ps, invs = [], []
            for i in range(4):
                h = 4 * kvh + i
                p, inv, _ = _attn_softmax(s4[i * BLK:(i + 1) * BLK], base, _SLOPES[h], sink_ref[0, h])
                ps.append(p)
                invs.append(inv)
            o4 = jnp.dot(_stack_bf16(ps), vb, preferred_element_type=f32)
            for pr in range(2):
                lo = o4[(2 * pr) * BLK:(2 * pr + 1) * BLK] * invs[2 * pr]
                hi = o4[(2 * pr + 1) * BLK:(2 * pr + 2) * BLK] * invs[2 * pr + 1]
                pair = 2 * kvh + pr
                o_ref[:, pair * 128:(pair + 1) * 128] = jnp.where(even, lo, hi).astype(o_ref.dtype)

    return pl.pallas_call(
        body, name="attn_fwd", grid=(nb,),
        in_specs=[pl.BlockSpec(memory_space=pltpu.SMEM), q_spec] + k_specs + v_specs,
        out_specs=pl.BlockSpec((BLK, D), lambda n: (n, 0)), out_shape=_sds((S, D), bf16),
        compiler_params=_params(1, True))(sink, proj, proj, proj, proj, proj, proj, proj)


def _attn_bwd(proj, sink, y_b, dy, dproj):
    S = proj.shape[0]
    nb, q_spec, k_specs, v_specs = _attn_specs(S)
    q_col, kv_col = COL_Q * 256, COL_K * 256

    def body(sink_ref, q_ref, kp_ref, kc_ref, kn_ref, vp_ref, vc_ref, vn_ref, o_ref, do_ref, dproj_in,
             dproj_ref, dsink_ref, dk_ref, dv_ref, dq_buf, kv_buf, sems):
        n = pl.program_id(0)
        slot = n % 2
        dq_ref = dq_buf.at[slot]

        def dq_copy(step):
            rows = pl.ds(pl.multiple_of(step * BLK, BLK), BLK)
            return pltpu.make_async_copy(dq_buf.at[step % 2], dproj_ref.at[rows, pl.ds(q_col, D)], sems.at[step % 2])

        @pl.when(n >= 2)
        def _():
            dq_copy(n - 2).wait()

        @pl.when(n == 0)
        def _():
            dk_ref[...] = jnp.zeros_like(dk_ref)
            dv_ref[...] = jnp.zeros_like(dv_ref)
            dsink_ref[...] = jnp.zeros_like(dsink_ref)

        base = _attn_base(n, S)
        kcat = jnp.concatenate([kp_ref[...], kc_ref[...], kn_ref[...]], axis=0).astype(f32)
        vcat = jnp.concatenate([vp_ref[...], vc_ref[...], vn_ref[...]], axis=0).astype(f32)
        dk_rows, dv_rows = [[], []], [[], []]
        scale = HD ** -0.5
        even = _half_mask((BLK, 128), 0)
        for kvh in range(NH // 4):
            ch, off = kvh // 2, kvh % 2
            kb = _both_halves(kcat[:, ch * 128:(ch + 1) * 128], off).astype(bf16)
            vb = _both_halves(vcat[:, ch * 128:(ch + 1) * 128], off).astype(bf16)
            q_parts = _group_heads(q_ref, kvh, scale)
            d_parts = _group_heads(do_ref, kvh, 1.0)
            s4 = lax.dot_general(_stack_bf16(q_parts), kb, _DIMS["nt"], preferred_element_type=f32)
            dp4 = lax.dot_general(_stack_bf16(d_parts), vb, _DIMS["nt"], preferred_element_type=f32)
            ts, ps, qn, dn, invs = [], [], [], [], []
            for i in range(4):
                h = 4 * kvh + i
                pair = 2 * kvh + i // 2
                rows = slice(i * BLK, (i + 1) * BLK)
                p, inv, psink = _attn_softmax(s4[rows], base, _SLOPES[h], sink_ref[0, h])
                delta = jnp.sum(d_parts[i] * o_ref[:, pair * 128:(pair + 1) * 128].astype(f32), axis=-1, keepdims=True)
                dsink_ref[h:h + 1, :] += jnp.broadcast_to(-jnp.sum(psink * delta, axis=0, keepdims=True), (1, 128))
                ts.append(p * (dp4[rows] - delta))
                ps.append(p)
                qn.append(q_parts[i] * inv)
                dn.append(d_parts[i] * inv)
                invs.append(inv)
            t4 = _stack_bf16(ts)
            dq4 = jnp.dot(t4, kb, preferred_element_type=f32)
            for pr in range(2):
                lo = dq4[(2 * pr) * BLK:(2 * pr + 1) * BLK] * invs[2 * pr]
                hi = dq4[(2 * pr + 1) * BLK:(2 * pr + 2) * BLK] * invs[2 * pr + 1]
                pair = 2 * kvh + pr
                dq_ref[:, pair * 128:(pair + 1) * 128] = (jnp.where(even, lo, hi) * scale).astype(dq_ref.dtype)
            dk_t = lax.dot_general(_stack_bf16(qn), t4, _DIMS["tn"], preferred_element_type=f32)
            dv_t = lax.dot_general(_stack_bf16(dn), _stack_bf16(ps), _DIMS["tn"], preferred_element_type=f32)
            dk_rows[ch].append(dk_t[0:HD] + dk_t[HD:2 * HD])
            dv_rows[ch].append(dv_t[0:HD] + dv_t[HD:2 * HD])
        dk_acc = [jnp.concatenate(r, axis=0).T for r in dk_rows]
        dv_acc = [jnp.concatenate(r, axis=0).T for r in dv_rows]
        for j in range(3):
            blk = n + (j - 1)

            @pl.when((blk >= 0) & (blk < nb))
            def _():
                rows = pl.ds(pl.multiple_of(blk * BLK, BLK), BLK)
                for ch in range(2):
                    dk_ref[rows, ch * 128:(ch + 1) * 128] += dk_acc[ch][j * BLK:(j + 1) * BLK]
                    dv_ref[rows, ch * 128:(ch + 1) * 128] += dv_acc[ch][j * BLK:(j + 1) * BLK]

        dq_copy(n).start()

        @pl.when(n == nb - 1)
        def _():
            def cast(i, c):
                rows = pl.ds(pl.multiple_of(i * 4 * BLK, 4 * BLK), 4 * BLK)
                kv_buf[rows, 0:256] = dk_ref[rows, :].astype(bf16)
                kv_buf[rows, 256:512] = dv_ref[rows, :].astype(bf16)
                return c

            lax.fori_loop(0, S // (4 * BLK), cast, 0)
            kv_copy = pltpu.make_async_copy(kv_buf, dproj_ref.at[:, pl.ds(kv_col, 512)], sems.at[2])
            kv_copy.start()
            if nb >= 2:
                dq_copy(n - 1).wait()
            dq_copy(n).wait()
            kv_copy.wait()

    row_blk = pl.BlockSpec((BLK, D), lambda n: (n, 0))
    return pl.pallas_call(
        body, name="attn_bwd", grid=(nb,),
        in_specs=[pl.BlockSpec(memory_space=pltpu.SMEM), q_spec] + k_specs + v_specs
        + [row_blk, pl.BlockSpec((None, BLK, D), lambda n: (1, n, 0)), ANY],
        out_specs=[ANY, pl.BlockSpec((NH, 128), lambda n: (0, 0))],
        out_shape=[_sds(dproj.shape, bf16), _sds((NH, 128), f32)],
        scratch_shapes=[pltpu.VMEM((S, 256), f32), pltpu.VMEM((S, 256), f32), pltpu.VMEM((2, BLK, D), bf16),
                        pltpu.VMEM((S, 512), bf16), pltpu.SemaphoreType.DMA((3,))],
        input_output_aliases={10: 0},
        compiler_params=_params(1, True))(sink, proj, proj, proj, proj, proj, proj, proj, y_b, dy, dproj)


def _adamw(name, w, g, m, v, tr):
    R, C = w.shape
    tr = min(tr, R)

    def body(w_ref, g_ref, m_ref, v_ref, d_ref, m2_ref, v2_ref):
        g = g_ref[...]
        m2 = ADAM_B1 * m_ref[...] + (1.0 - ADAM_B1) * g
        v2 = ADAM_B2 * v_ref[...] + (1.0 - ADAM_B2) * (g * g)
        m_hat = m2 / (1.0 - ADAM_B1 ** ADAM_STEP)
        v_hat = v2 / (1.0 - ADAM_B2 ** ADAM_STEP)
        d_ref[...] = -ADAM_LR * (m_hat / (jnp.sqrt(v_hat) + ADAM_EPS) + ADAM_WD * w_ref[...])
        m2_ref[...] = m2
        v2_ref[...] = v2

    blk = pl.BlockSpec((tr, C), lambda i: (i, 0))
    return pl.pallas_call(body, name=name, grid=(R // tr,), in_specs=[blk] * 4, out_specs=[blk] * 3,
                          out_shape=[_sds((R, C), f32)] * 3, compiler_params=_params(1))(w, g, m, v)


def _pair_sum(name, c_arr, g4, recv, th):
    _, _, h, w = g4.shape
    th = min(th, h)

    def body(c_ref, g_ref, r_ref, o_ref, ob_ref):
        p = g_ref[...] + r_ref[...]
        o_ref[...] = p
        ob_ref[...] = p.astype(bf16)

    blk = pl.BlockSpec((None, th, w), lambda s, i, c_ref: (s, i, 0))
    spec = pltpu.PrefetchScalarGridSpec(
        num_scalar_prefetch=1, grid=(NCHIP, h // th),
        in_specs=[pl.BlockSpec((None, None, th, w), lambda s, i, c_ref: (s, c_ref[0], i, 0)), blk],
        out_specs=[blk, blk])
    return pl.pallas_call(body, name=name, grid_spec=spec,
                          out_shape=[_sds((NCHIP, h, w), f32), _sds((NCHIP, h, w), bf16)],
                          compiler_params=_params(2))(c_arr, g4, recv)


def _chip_sum(name, chip_arr, own4, recv3, th):
    _, h, w = own4.shape
    th = min(th, h)

    def body(s_ref, o_ref, r_ref, out_ref):
        out_ref[...] = ((o_ref[...] + r_ref[0].astype(f32)) + r_ref[1].astype(f32)) + r_ref[2].astype(f32)

    spec = pltpu.PrefetchScalarGridSpec(
        num_scalar_prefetch=1, grid=(h // th,),
        in_specs=[pl.BlockSpec((None, th, w), lambda i, s_ref: (s_ref[0], i, 0)),
                  pl.BlockSpec((3, th, w), lambda i, s_ref: (0, i, 0))],
        out_specs=pl.BlockSpec((th, w), lambda i, s_ref: (i, 0)))
    return pl.pallas_call(body, name=name, grid_spec=spec, out_shape=_sds((h, w), f32),
                          compiler_params=_params(1, True))(chip_arr, own4, recv3)


def _adamw_halves(name, c_arr, w, g_own, g_recv, m, v, th):
    h, wd = g_own.shape
    th = min(th, h)

    def body(c_ref, w_ref, go_ref, gr_ref, m_ref, v_ref, g_ref, d_ref, m2_ref, v2_ref):
        g = jnp.where(c_ref[0] == pl.program_id(0), go_ref[...], gr_ref[...])
        m2 = ADAM_B1 * m_ref[...] + (1.0 - ADAM_B1) * g
        v2 = ADAM_B2 * v_ref[...] + (1.0 - ADAM_B2) * (g * g)
        m_hat = m2 / (1.0 - ADAM_B1 ** ADAM_STEP)
        v_hat = v2 / (1.0 - ADAM_B2 ** ADAM_STEP)
        g_ref[...] = g
        d_ref[...] = -ADAM_LR * (m_hat / (jnp.sqrt(v_hat) + ADAM_EPS) + ADAM_WD * w_ref[...])
        m2_ref[...] = m2
        v2_ref[...] = v2

    nt = h // th
    full = pl.BlockSpec((th, wd), lambda hh, i, c_ref: (hh * nt + i, 0))
    half = pl.BlockSpec((th, wd), lambda hh, i, c_ref: (i, 0))
    spec = pltpu.PrefetchScalarGridSpec(num_scalar_prefetch=1, grid=(2, nt),
                                        in_specs=[full, half, half, full, full], out_specs=[full] * 4)
    return pl.pallas_call(body, name=name, grid_spec=spec, out_shape=[_sds((2 * h, wd), f32)] * 4,
                          compiler_params=_params(2))(c_arr, w, g_own, g_recv, m, v)


def _add2(name, a, b):
    def body(a_ref, b_ref, o_ref):
        o_ref[...] = a_ref[...] + b_ref[...]
    return pl.pallas_call(body, name=name, out_shape=_sds(a.shape, f32))(a, b)


def _sum4(name, b4, th):
    _, h, w = b4.shape
    th = min(th, h)

    def body(b_ref, o_ref):
        o_ref[...] = ((b_ref[0] + b_ref[1]) + b_ref[2]) + b_ref[3]

    return pl.pallas_call(body, name=name, grid=(h // th,),
                          in_specs=[pl.BlockSpec((NCHIP, th, w), lambda i: (0, i, 0))],
                          out_specs=pl.BlockSpec((th, w), lambda i: (i, 0)), out_shape=_sds((h, w), f32),
                          compiler_params=_params(1, True))(b4)


def _coords():
    x, y, c = lax.axis_index("x"), lax.axis_index("y"), lax.axis_index("c")
    return x, y, c, [(1 - x, y), (x, 1 - y), (1 - x, 1 - y)]


def _gather_chips(arrs):
    n = len(arrs)

    def body(*refs):
        ins, outs = refs[:n], refs[n:2 * n]
        send_sems, recv_sems, local_sems = refs[2 * n:2 * n + 3]
        stage = refs[2 * n + 3:]
        x, y, c, chips = _coords()
        s = 2 * x + y
        sib = (x, y, 1 - c)
        load = [pltpu.make_async_copy(ins[a], stage[a], local_sems.at[a]) for a in range(n)]
        local = [pltpu.make_async_copy(stage[a], outs[a].at[s], local_sems.at[n + a]) for a in range(n)]
        for cp in load:
            cp.start()

        def over_ici(k, a, slot, peer):
            return pltpu.make_async_remote_copy(src_ref=ins[a].at[c], dst_ref=outs[a].at[slot, c], send_sem=send_sems.at[k * n + a],
                                                recv_sem=recv_sems.at[k * n + a], device_id=peer, device_id_type=MESH)

        def to_sibling(k, a, slot, half):
            i = (3 + k) * n + a
            return pltpu.make_async_remote_copy(src_ref=outs[a].at[slot, half], dst_ref=outs[a].at[slot, half], send_sem=send_sems.at[i],
                                                recv_sem=recv_sems.at[i], device_id=sib, device_id_type=MESH)

        sends = [over_ici(k, a, s, (px, py, c)) for k, (px, py) in enumerate(chips) for a in range(n)]
        for cp in sends:
            cp.start()
        for a in range(n):
            load[a].wait()
            local[a].start()
        passed = []
        for k, (px, py) in enumerate(chips):
            for a in range(n):
                over_ici(k, a, 2 * px + py, (px, py, c)).wait_recv()
                cp = to_sibling(k, a, 2 * px + py, c)
                cp.start()
                passed.append(cp)
        for k, (px, py) in enumerate(chips):
            for a in range(n):
                to_sibling(k, a, 2 * px + py, 1 - c).wait_recv()
        for cp in sends + passed:
            cp.wait_send()
        for cp in local:
            cp.wait()

    return pl.pallas_call(
        body, name="gather_weights", in_specs=[ANY] * n, out_specs=[ANY] * n,
        out_shape=[_sds((NCHIP,) + a.shape, a.dtype) for a in arrs],
        scratch_shapes=[pltpu.SemaphoreType.DMA((6 * n,)), pltpu.SemaphoreType.DMA((6 * n,)), pltpu.SemaphoreType.DMA((2 * n,))]
        + [pltpu.VMEM(a.shape, a.dtype) for a in arrs],
        compiler_params=pltpu.CompilerParams(vmem_limit_bytes=VMEM_LIMIT),
    )(*arrs)


HBM = pl.BlockSpec(memory_space=pltpu.HBM)
SEM = pl.BlockSpec(memory_space=pltpu.SEMAPHORE)
EFFECT = pltpu.SideEffectType.DATAFLOW_SIDE_EFFECTING


def _split_start(name, n_copies, make_copies, ins, land_shapes, after):
    ni, nl = len(ins), len(land_shapes)

    def body(*refs):
        in_refs, land_refs = refs[:ni], refs[ni:ni + nl]
        send_sems, recv_sems = refs[ni + nl + 1], refs[ni + nl + 2]
        token = refs[-1]
        for cp in make_copies(in_refs, land_refs, send_sems, recv_sems):
            cp.start()
        token[...] = jnp.zeros_like(token)

    lands = [pltpu.with_memory_space_constraint(lax.empty(s.shape, s.dtype), pltpu.HBM) for s in land_shapes]
    res = pl.pallas_call(
        body, name=name,
        out_shape=(pltpu.SemaphoreType.DMA((n_copies,)), pltpu.SemaphoreType.DMA((n_copies,)),
                   *[pltpu.HBM(a.shape, a.dtype) for a in ins], *[pltpu.HBM(s.shape, s.dtype) for s in land_shapes],
                   _sds((8, 128), f32)),
        in_specs=[HBM] * (ni + nl) + [ANY], out_specs=(SEM, SEM, *[HBM] * (ni + nl), pl.BlockSpec(memory_space=pltpu.VMEM)),
        input_output_aliases={i: 2 + i for i in range(ni + nl)},
        compiler_params=pltpu.CompilerParams(has_side_effects=EFFECT),
    )(*[pltpu.with_memory_space_constraint(a, pltpu.HBM) for a in ins], *lands, after)
    return res[0], res[1], list(res[2:2 + ni]), list(res[2 + ni:2 + ni + nl]), res[-1]


def _split_wait(name, make_copies, send_sems, recv_sems, ins, lands, after):
    ni, nl = len(ins), len(lands)

    def body(*refs):
        in_refs, land_refs = refs[:ni], refs[ni:ni + nl]
        s_sems, r_sems = refs[ni + nl], refs[ni + nl + 1]
        for cp in make_copies(in_refs, land_refs, s_sems, r_sems):
            cp.wait_send()
            cp.wait_recv()

    res = pl.pallas_call(
        body, name=name, out_shape=tuple(pltpu.HBM(a.shape, a.dtype) for a in ins + lands),
        in_specs=[HBM] * (ni + nl) + [SEM, SEM, ANY], out_specs=tuple([HBM] * (ni + nl)),
        input_output_aliases={i: i for i in range(ni + nl)},
        compiler_params=pltpu.CompilerParams(has_side_effects=EFFECT),
    )(*ins, *lands, send_sems, recv_sems, after)
    return list(res[:ni]), list(res[ni:])


def _gather_copies(n):
    def make(in_refs, land_refs, send_sems, recv_sems):
        x, y, c, chips = _coords()
        s = 2 * x + y
        return [pltpu.make_async_remote_copy(src_ref=in_refs[a], dst_ref=land_refs[a].at[s], send_sem=send_sems.at[k * n + a],
                                             recv_sem=recv_sems.at[k * n + a], device_id=(px, py, c), device_id_type=MESH)
                for k, (px, py) in enumerate(chips) for a in range(n)]
    return make


def _sibling_half_copies(n):
    def make(in_refs, land_refs, send_sems, recv_sems):
        x, y, c, _ = _coords()
        return [pltpu.make_async_remote_copy(src_ref=in_refs[a].at[:, 1 - c], dst_ref=land_refs[a], send_sem=send_sems.at[a],
                                             recv_sem=recv_sems.at[a], device_id=(x, y, 1 - c), device_id_type=MESH)
                for a in range(n)]
    return make


def _chip_part_copies(n):
    def make(in_refs, land_refs, send_sems, recv_sems):
        x, y, c, chips = _coords()
        return [pltpu.make_async_remote_copy(src_ref=in_refs[a].at[2 * px + py], dst_ref=land_refs[a].at[k],
                                             send_sem=send_sems.at[k * n + a], recv_sem=recv_sems.at[k * n + a],
                                             device_id=(px, py, c), device_id_type=MESH)
                for k, (px, py) in enumerate(chips) for a in range(n)]
    return make


def _sibling_whole_copies(n):
    def make(in_refs, land_refs, send_sems, recv_sems):
        x, y, c, _ = _coords()
        return [pltpu.make_async_remote_copy(src_ref=in_refs[a], dst_ref=land_refs[a], send_sem=send_sems.at[a],
                                             recv_sem=recv_sems.at[a], device_id=(x, y, 1 - c), device_id_type=MESH)
                for a in range(n)]
    return make


def _place_own(chip_arr, owns, lands, steps):
    n = len(owns)

    def body(s_ref, *refs):
        for a in range(n):
            refs[2 * n + a][...] = refs[a][...]

    tiles = [o.shape[0] // steps for o in owns]
    spec = pltpu.PrefetchScalarGridSpec(
        num_scalar_prefetch=1, grid=(steps,),
        in_specs=[pl.BlockSpec((t, o.shape[1]), lambda i, s_ref: (i, 0)) for t, o in zip(tiles, owns)] + [ANY] * n,
        out_specs=[pl.BlockSpec((None, t, o.shape[1]), lambda i, s_ref: (s_ref[0], i, 0)) for t, o in zip(tiles, owns)])
    return pl.pallas_call(body, name="place_own", grid_spec=spec, out_shape=[_sds(l.shape, l.dtype) for l in lands],
                          input_output_aliases={1 + n + a: a for a in range(n)},
                          compiler_params=_params(1))(chip_arr, *owns, *lands)


def _sibling_halves(g4s, small):
    n = len(g4s)

    def body(*refs):
        ins, small_ref = refs[:n], refs[n]
        outs, small_out = refs[n + 1:2 * n + 1], refs[2 * n + 1]
        send_sems, recv_sems = refs[2 * n + 2:]
        x, y, c, _ = _coords()
        sib = (x, y, 1 - c)

        def remote(a, half):
            src = small_ref if a == n else ins[a].at[:, half]
            dst = small_out if a == n else outs[a]
            return pltpu.make_async_remote_copy(src_ref=src, dst_ref=dst, send_sem=send_sems.at[a], recv_sem=recv_sems.at[a],
                                                device_id=sib, device_id_type=MESH)

        sends = [remote(a, 1 - c) for a in range(n + 1)]
        for cp in sends:
            cp.start()
        for a in range(n + 1):
            remote(a, c).wait_recv()
        for cp in sends:
            cp.wait_send()

    return pl.pallas_call(
        body, name="reduce_sibling", in_specs=[ANY] * (n + 1), out_specs=[ANY] * (n + 1),
        out_shape=[_sds((g.shape[0],) + g.shape[2:], f32) for g in g4s] + [_sds(small.shape, f32)],
        scratch_shapes=[pltpu.SemaphoreType.DMA((n + 1,)), pltpu.SemaphoreType.DMA((n + 1,))],
    )(*g4s, small)


def _exchange_chips(parts, small2):
    n = len(parts)

    def body(*refs):
        ins, small_ref = refs[:n], refs[n]
        outs, small_out = refs[n + 1:2 * n + 1], refs[2 * n + 1]
        send_sems, recv_sems, local_sem = refs[2 * n + 2:]
        x, y, c, chips = _coords()
        s = 2 * x + y
        local = pltpu.make_async_copy(small_ref.at[c], small_out.at[s], local_sem)
        local.start()

        def remote(k, a, dest_chip, small_slot, peer):
            if a == n:
                src, dst = small_ref.at[c], small_out.at[small_slot]
            else:
                src, dst = ins[a].at[dest_chip], outs[a].at[k]
            i = k * (n + 1) + a
            return pltpu.make_async_remote_copy(src_ref=src, dst_ref=dst, send_sem=send_sems.at[i], recv_sem=recv_sems.at[i],
                                                device_id=peer, device_id_type=MESH)

        sends = [remote(k, a, 2 * px + py, s, (px, py, c)) for k, (px, py) in enumerate(chips) for a in range(n + 1)]
        for cp in sends:
            cp.start()
        for k, (px, py) in enumerate(chips):
            for a in range(n + 1):
                remote(k, a, s, 2 * px + py, (px, py, c)).wait_recv()
        for cp in sends:
            cp.wait_send()
        local.wait()

    m = 3 * (n + 1)
    return pl.pallas_call(
        body, name="reduce_chips", in_specs=[ANY] * (n + 1), out_specs=[ANY] * (n + 1),
        out_shape=[_sds((3,) + p.shape[1:], p.dtype) for p in parts] + [_sds((NCHIP,) + small2.shape[1:], f32)],
        scratch_shapes=[pltpu.SemaphoreType.DMA((m,)), pltpu.SemaphoreType.DMA((m,)), pltpu.SemaphoreType.DMA],
    )(*parts, small2)


def _share_sibling(halves):
    n = len(halves)

    def body(*refs):
        ins, outs = refs[:n], refs[n:2 * n]
        send_sems, recv_sems = refs[2 * n:]
        x, y, c, _ = _coords()
        sib = (x, y, 1 - c)
        sends = [pltpu.make_async_remote_copy(src_ref=ins[a], dst_ref=outs[a], send_sem=send_sems.at[a], recv_sem=recv_sems.at[a],
                                              device_id=sib, device_id_type=MESH) for a in range(n)]
        for cp in sends:
            cp.start()
        for cp in sends:
            cp.wait()

    return pl.pallas_call(
        body, name="reduce_share", in_specs=[ANY] * n, out_specs=[ANY] * n,
        out_shape=[_sds(h.shape, f32) for h in halves],
        scratch_shapes=[pltpu.SemaphoreType.DMA((n,)), pltpu.SemaphoreType.DMA((n,))],
    )(*halves)


def _block_diag_pairs(w):
    w = w.reshape(NCH, 2, HD, HD)
    z = jnp.zeros((NCH, HD, HD), w.dtype)
    return jnp.concatenate([jnp.concatenate([w[:, 0], z], axis=2), jnp.concatenate([z, w[:, 1]], axis=2)], axis=1)


def _diag_blocks(m):
    return jnp.stack([m[:, :HD, :HD], m[:, HD:, HD:]], axis=1).reshape(NH, HD, HD)


def _pack(vs, rows):
    flat = jnp.concatenate([v.reshape(-1) for v in vs])
    return jnp.pad(flat, (0, rows * 128 - flat.shape[0])).reshape(rows, 128)


def _unpack(packed, shapes):
    flat = packed.reshape(-1)
    out, off = [], 0
    for shp in shapes:
        size = math.prod(shp)
        out.append(flat[off:off + size].reshape(shp))
        off += size
    return out


def _rows_for(sizes, multiple):
    rows = -(-sum(sizes) // 128)
    return -(-rows // multiple) * multiple


def kernel(x, norm_mix_g, w_in, b_gate, conv_w, conv_b, lru_lambda, lru_wa, lru_ba, lru_wx, lru_bx, attn_sink, w_out, norm_ffn_g, w_ffn_in, w_ffn_out, norm_final_g, loss_target, m_norm_mix_g, m_w_in, m_b_gate, m_conv_w, m_conv_b, m_lru_lambda, m_lru_wa, m_lru_ba, m_lru_wx, m_lru_bx, m_attn_sink, m_w_out, m_norm_ffn_g, m_w_ffn_in, m_w_ffn_out, m_norm_final_g, v_norm_mix_g, v_w_in, v_b_gate, v_conv_w, v_conv_b, v_lru_lambda, v_lru_wa, v_lru_ba, v_lru_wx, v_lru_bx, v_attn_sink, v_w_out, v_norm_ffn_g, v_w_ffn_in, v_w_ffn_out, v_norm_final_g):
    S = x.shape[1]
    xs = x[0]
    tgt = loss_target[0]
    cx, cy, cc = lax.axis_index("x"), lax.axis_index("y"), lax.axis_index("c")
    chip = 2 * cx + cy
    SW = D // NCHIP

    small_shard = _pack([conv_w[0], lru_lambda[0], lru_ba[0], lru_bx[0]], 32)
    halves_of = lambda a: a.reshape(2, a.shape[0] // 2, a.shape[1])
    w_in_g, small_g = _gather_chips([halves_of(w_in[0].astype(bf16)), halves_of(small_shard)])
    w_in_g = w_in_g.reshape(NCHIP, D, SHW)
    small_g = small_g.reshape(NCHIP, 32, 128)
    late = [w_ffn_in[0].astype(bf16), w_out[0].astype(bf16), w_ffn_out[0].astype(bf16)]
    late_send, late_recv, late_src, late_land, late_token = _split_start(
        "gather_late_start", 9, _gather_copies(3), late, [_sds((NCHIP,) + a.shape, bf16) for a in late], small_g)
    small_parts = [_unpack(small_g[s], [(4, SW), (2, SW), (2, SW), (2, SW)]) for s in range(NCHIP)]
    conv_w_f, lam_f, ba_f, bx_f = [jnp.concatenate([small_parts[s][p] for s in range(NCHIP)], axis=1) for p in range(4)]
    wbd = jnp.concatenate([_block_diag_pairs(lru_wa[0, 0]), _block_diag_pairs(lru_wx[0, 0]),
                           _block_diag_pairs(lru_wa[0, 1]), _block_diag_pairs(lru_wx[0, 1])], axis=2).astype(bf16)
    conv_b_f = conv_b
    sink = attn_sink

    xn, proj = _rms_matmul("rms_proj", xs, norm_mix_g + late_token[0:1, 0:1], w_in_g, 512)
    y_a, lru_state = _lru_fwd(proj, conv_w_f, conv_b_f, lam_f, ba_f, bx_f, wbd)
    y_b = _attn_fwd(proj, sink)
    late_src, late_land = _split_wait("gather_late_wait", _gather_copies(3), late_send, late_recv, late_src, late_land, y_b)
    chip_arr = chip.reshape(1).astype(jnp.int32)
    w_ffn_in_g, w_out_g, w_ffn_out_g = _place_own(chip_arr, late_src, late_land, 4)
    w_out_f = w_out_g.reshape(D, D)
    w_ffn_out_f = w_ffn_out_g.reshape(FF, D)
    merged, x1 = _merge_out_proj(proj, b_gate, y_a, y_b, w_out_f, xs, 512)
    xn2, gu, act = _rms_matmul_swiglu("rms_ffn_in", x1, norm_ffn_g, w_ffn_in_g, 512)
    dx2, loss_row, dg3 = _ffn_out_loss_bwd(act, w_ffn_out_f, x1, norm_final_g.reshape(1, D), tgt, 512)

    tk = min(2048, S)
    gw_ffn_out = _mm_tn("dw_ffn_out", act, pl.BlockSpec((tk, SHW), lambda i, k: (k, i)),
                        dx2, pl.BlockSpec((tk, D), lambda i, k: (k, 0)),
                        _sds((FF, D), f32), pl.BlockSpec((SHW, D), lambda i, k: (i, 0)), (2, S // tk), (SHW, D))
    dgu = _swiglu_bwd(dx2, w_ffn_out_f, gu, 256)
    gw_ffn_in = _mm_tn("dw_ffn_in", xn2, pl.BlockSpec((tk, D), lambda g, k: (k, 0)),
                       dgu, pl.BlockSpec((None, tk, SHW), lambda g, k: (g // 2, k, g % 2)),
                       _sds((NCHIP, D, SHW), f32), pl.BlockSpec((None, D, SHW), lambda g, k: (g, 0, 0)),
                       (NCHIP, S // tk), (D, SHW))
    c_arr = cc.reshape(1).astype(jnp.int32)
    early_names, early_tiles = ["w_ffn_in", "w_ffn_out"], [256, 352]
    early = [gw_ffn_in.reshape(NCHIP, 2, D // 2, SHW), gw_ffn_out.reshape(NCHIP, 2, FF // NCHIP // 2, D)]
    ea_send, ea_recv, ea_src, ea_land, ea_token = _split_start(
        "reduce_early_sibling_start", 2, _sibling_half_copies(2), early,
        [_sds((NCHIP,) + g.shape[2:], f32) for g in early], dgu)
    dx1, dg2 = _mm_nt_rms_bwd("dxn2_rms_bwd", dgu, w_ffn_in_g, x1, norm_ffn_g + ea_token[0:1, 0:1], dx2, 512)

    gw_out = _mm_tn("dw_out", merged, pl.BlockSpec((tk, D), lambda i, k: (k, 0)),
                    dx1, pl.BlockSpec((tk, D), lambda i, k: (k, 0)),
                    _sds((D, D), f32), pl.BlockSpec((D, D), lambda i, k: (0, 0)), (1, S // tk), (D, D))
    dproj, dy, db_gate = _merge_bwd(proj, b_gate, y_a, y_b, dx1, w_out_f, 512)
    ea_src, ea_land = _split_wait("reduce_early_sibling_wait", _sibling_half_copies(2), ea_send, ea_recv, ea_src, ea_land, dy)
    early_pairs = [_pair_sum("pair_sum_" + nm, c_arr, g4, r, th)
                   for nm, g4, r, th in zip(early_names, ea_src, ea_land, early_tiles)]
    eb_send, eb_recv, eb_src, eb_land, eb_token = _split_start(
        "reduce_early_chips_start", 6, _chip_part_copies(2), [p[1] for p in early_pairs],
        [_sds((3,) + p[1].shape[1:], bf16) for p in early_pairs], early_pairs[0][0])
    dproj, dsink = _attn_bwd(proj, sink + eb_token[0:1, 0:1], y_b, dy, dproj)
    _, eb_land = _split_wait("reduce_early_chips_wait", _chip_part_copies(2), eb_send, eb_recv, eb_src, eb_land, dsink)
    early_halves = [_chip_sum("chip_sum_" + nm, chip_arr, p[0], r3, th)
                    for nm, p, r3, th in zip(early_names, early_pairs, eb_land, early_tiles)]
    ec_send, ec_recv, ec_src, ec_land, ec_token = _split_start(
        "reduce_early_share_start", 2, _sibling_whole_copies(2), early_halves, [_sds(h.shape, f32) for h in early_halves], dsink)
    dproj, dcw, dcb, dlam, dba, dbx, dwbd = _lru_bwd(proj, dy, lru_state, dproj, conv_w_f, conv_b_f + ec_token[0:1, 0:1], lam_f,
                                                     ba_f, bx_f, wbd)
    early_halves, early_other = _split_wait("reduce_early_share_wait", _sibling_whole_copies(2), ec_send, ec_recv, ec_src, ec_land, dcb)
    gw_in = _mm_tn("dw_in", xn, pl.BlockSpec((tk, D), lambda g, k: (k, 0)),
                   dproj, pl.BlockSpec((tk, SHW), lambda g, k: (k, g)),
                   _sds((NCHIP, D, SHW), f32), pl.BlockSpec((None, D, SHW), lambda g, k: (g, 0, 0)),
                   (NCHIP, S // tk), (D, SHW))
    wa_send, wa_recv, wa_src, wa_land, wa_token = _split_start(
        "reduce_w_in_sibling_start", 1, _sibling_half_copies(1), [gw_in.reshape(NCHIP, 2, D // 2, SHW)],
        [_sds((NCHIP, D // 2, SHW), f32)], dproj)
    dxn = _mm_nt_groups("dxn", dproj, w_in_g, 512)
    wa_src, wa_land = _split_wait("reduce_w_in_sibling_wait", _sibling_half_copies(1), wa_send, wa_recv, wa_src, wa_land, dxn)
    w_in_pair = _pair_sum("pair_sum_w_in", c_arr, wa_src[0], wa_land[0], 256)
    wb_send, wb_recv, wb_src, wb_land, wb_token = _split_start(
        "reduce_w_in_chips_start", 3, _chip_part_copies(1), [w_in_pair[1]], [_sds((3, D // 2, SHW), bf16)], w_in_pair[0])
    grad_x, dg1 = _rms_bwd("rms_mix_bwd", xs, norm_mix_g + wb_token[0:1, 0:1], dxn, dx1, 512)
    _, wb_land = _split_wait("reduce_w_in_chips_wait", _chip_part_copies(1), wb_send, wb_recv, wb_src, wb_land, dg1)
    w_in_half = _chip_sum("chip_sum_w_in", chip_arr, w_in_pair[0], wb_land[0], 256)

    d_wa = jnp.stack([_diag_blocks(dwbd[:, :, 0:CW]), _diag_blocks(dwbd[:, :, 2 * CW:3 * CW])])
    d_wx = jnp.stack([_diag_blocks(dwbd[:, :, CW:2 * CW]), _diag_blocks(dwbd[:, :, 3 * CW:4 * CW])])
    small_full = [dg1, db_gate, dcw, dcb, dlam, d_wa, dba, d_wx, dbx, dsink[:, 0], dg2, dg3,
                  loss_row[0, 0:1]]
    full_shapes = [(1, D), (1, 2 * D), (4, D), (1, D), (2, D), (2, NH, HD, HD), (2, D), (2, NH, HD, HD), (2, D), (NH,),
                   (1, D), (1, D), (1,)]
    rows_full = _rows_for([math.prod(s) for s in full_shapes], 16)
    small_vec = _pack(small_full, rows_full)

    late_names, late_tiles = ["w_in", "w_out"], [256, 128]
    big = [gw_out.reshape(NCHIP, 2, D // NCHIP // 2, D)]
    *recv_a, small_sib = _sibling_halves(big, small_vec)
    w_out_pair = _pair_sum("pair_sum_w_out", c_arr, big[0], recv_a[0], 128)
    small_chip = _add2("pair_sum_small", small_vec, small_sib).reshape(2, rows_full // 2, 128)
    *recv_b, small_all = _exchange_chips([w_out_pair[1]], small_chip)
    w_out_half = _chip_sum("chip_sum_w_out", chip_arr, w_out_pair[0], recv_b[0], 128)
    halves = [w_in_half, w_out_half, _sum4("chip_sum_small", small_all, rows_full // 2)]
    *recv_c, small_other = _share_sibling(halves)
    small_lo = jnp.where(cc == 0, halves[2], small_other)
    small_hi = jnp.where(cc == 0, small_other, halves[2])
    g_full = _unpack(jnp.concatenate([small_lo, small_hi], axis=0), full_shapes)

    out_big = {}
    for nm, w, g_own, g_recv, m, v, th in zip(late_names + early_names, [w_in, w_out, w_ffn_in, w_ffn_out],
                                              halves[:2] + early_halves, recv_c + early_other,
                                              [m_w_in, m_w_out, m_w_ffn_in, m_w_ffn_out],
                                              [v_w_in, v_w_out, v_w_ffn_in, v_w_ffn_out], late_tiles + early_tiles):
        g_, d_, m_, v_ = _adamw_halves("adamw_" + nm, c_arr, w[0], g_own, g_recv, m[0], v[0], th)
        out_big[nm] = (g_[None], d_[None], m_[None], v_[None])

    small_names = ["norm_mix_g", "b_gate", "conv_w", "conv_b", "lru_lambda", "lru_wa", "lru_ba", "lru_wx", "lru_bx", "attn_sink",
                   "norm_ffn_g", "norm_final_g"]
    sharded = {"conv_w", "lru_lambda", "lru_ba", "lru_bx"}
    small_w = [norm_mix_g, b_gate, conv_w, conv_b, lru_lambda, lru_wa, lru_ba, lru_wx, lru_bx, attn_sink, norm_ffn_g, norm_final_g]
    small_m = [m_norm_mix_g, m_b_gate, m_conv_w, m_conv_b, m_lru_lambda, m_lru_wa, m_lru_ba, m_lru_wx, m_lru_bx, m_attn_sink,
               m_norm_ffn_g, m_norm_final_g]
    small_v = [v_norm_mix_g, v_b_gate, v_conv_w, v_conv_b, v_lru_lambda, v_lru_wa, v_lru_ba, v_lru_wx, v_lru_bx, v_attn_sink,
               v_norm_ffn_g, v_norm_final_g]
    g_local = []
    for nm, g, w in zip(small_names, g_full, small_w):
        if nm in sharded:
            g = lax.dynamic_slice_in_dim(g, chip * SW, SW, axis=1)
        g_local.append(g.reshape(w.shape))
    local_shapes = [w.shape for w in small_w]
    rows_local = _rows_for([math.prod(s) for s in local_shapes], 8)
    d_s, m_s, v_s = _adamw("adamw_small", _pack(small_w, rows_local), _pack(g_local, rows_local),
                           _pack(small_m, rows_local), _pack(small_v, rows_local), rows_local)
    d_l, m_l, v_l = _unpack(d_s, local_shapes), _unpack(m_s, local_shapes), _unpack(v_s, local_shapes)
    res = {nm: (g_local[i], d_l[i], m_l[i], v_l[i]) for i, nm in enumerate(small_names)}
    res.update(out_big)

    order = ["norm_mix_g", "w_in", "b_gate", "conv_w", "conv_b", "lru_lambda", "lru_wa", "lru_ba", "lru_wx", "lru_bx", "attn_sink",
             "w_out", "norm_ffn_g", "w_ffn_in", "w_ffn_out", "norm_final_g"]
    outs = [g_full[-1][0], grad_x[None]]
    for k in range(4):
        outs += [res[nm][k] for nm in order]
    return tuple(outs)
```

```python
import functools
import math

import jax
import jax.numpy as jnp
from jax import lax
from jax.experimental import pallas as pl
from jax.experimental.pallas import tpu as pltpu

f32 = jnp.float32
bf16 = jnp.bfloat16

D = 1024
NH = 16
HD = 64
FF = 2816
INW = 5632
NCHIP = 4
SHW = INW // NCHIP
CW = 128
NCH = D // CW
BLK = 128
EPS = 1e-6
NEG_INF = -1e30
RGLRU_C = 8.0
ADAM_LR, ADAM_B1, ADAM_B2, ADAM_EPS, ADAM_WD, ADAM_STEP = 0.001, 0.9, 0.999, 1e-08, 0.01, 10
VMEM_LIMIT = 58 * 1024 * 1024
MESH = pl.DeviceIdType.MESH
ANY = pl.BlockSpec(memory_space=pl.ANY)

COL_U, COL_G, COL_Q, COL_K, COL_V, COL_Z0, COL_Z1 = 0, 4, 8, 12, 13, 14, 18
MERGE_W = 512
MERGE_Z0, MERGE_Z1 = (COL_Z0 * 256) // MERGE_W, (COL_Z1 * 256) // MERGE_W


def _params(n_axes, vmem=False):
    return pltpu.CompilerParams(dimension_semantics=("arbitrary",) * n_axes,
                                vmem_limit_bytes=VMEM_LIMIT if vmem else None)


def _sds(shape, dtype):
    return jax.ShapeDtypeStruct(tuple(shape), dtype)


_DIMS = {"nn": (((1,), (0,)), ((), ())), "nt": (((1,), (1,)), ((), ())), "tn": (((0,), (0,)), ((), ()))}


def _mm(name, mode, a, a_spec, b, b_spec, out_shape, out_spec, grid, nk, acc_shape):
    def body(*refs):
        a_ref, b_ref, o_ref = refs[0], refs[1], refs[2]
        part = lax.dot_general(a_ref[...].astype(bf16), b_ref[...].astype(bf16), _DIMS[mode],
                               preferred_element_type=f32)
        if nk == 1:
            o_ref[...] = part.astype(o_ref.dtype)
            return
        acc_ref = refs[3]
        k = pl.program_id(len(grid) - 1)

        @pl.when(k == 0)
        def _():
            acc_ref[...] = part

        @pl.when(k > 0)
        def _():
            acc_ref[...] += part

        @pl.when(k == nk - 1)
        def _():
            o_ref[...] = acc_ref[...].astype(o_ref.dtype)

    scratch = [pltpu.VMEM(acc_shape, f32)] if nk > 1 else []
    return pl.pallas_call(body, name=name, grid=grid, in_specs=[a_spec, b_spec], out_specs=out_spec, out_shape=out_shape,
                          scratch_shapes=scratch, compiler_params=_params(len(grid), True))(a, b)


def _rms_matmul(name, x, g, w3, tm):
    S, K = x.shape
    G, _, Nw = w3.shape
    tm = min(tm, S)

    def body(x_ref, g_ref, w_ref, xn_ref, o_ref):
        xf = x_ref[...]
        r = lax.rsqrt(jnp.mean(xf * xf, axis=-1, keepdims=True) + EPS)
        xn = ((xf * r) * g_ref[...]).astype(bf16)
        xn_ref[...] = xn
        for j in range(G):
            o_ref[:, j * Nw:(j + 1) * Nw] = jnp.dot(xn, w_ref[j], preferred_element_type=f32).astype(bf16)

    return pl.pallas_call(
        body, name=name, grid=(S // tm,),
        in_specs=[pl.BlockSpec((tm, K), lambda i: (i, 0)), pl.BlockSpec((1, K), lambda i: (0, 0)),
                  pl.BlockSpec((G, K, Nw), lambda i: (0, 0, 0))],
        out_specs=[pl.BlockSpec((tm, K), lambda i: (i, 0)), pl.BlockSpec((tm, G * Nw), lambda i: (i, 0))],
        out_shape=[_sds((S, K), bf16), _sds((S, G * Nw), bf16)],
        compiler_params=_params(1, True))(x, g, w3)


def _rms_matmul_swiglu(name, x, g, w3, tm):
    S, K = x.shape
    G, _, Nw = w3.shape
    tm = min(tm, S)
    half = G // 2

    def body(x_ref, g_ref, w_ref, xn_ref, gu_ref, act_ref):
        xf = x_ref[...]
        r = lax.rsqrt(jnp.mean(xf * xf, axis=-1, keepdims=True) + EPS)
        xn = ((xf * r) * g_ref[...]).astype(bf16)
        xn_ref[...] = xn
        for j in range(half):
            cols = slice(j * Nw, (j + 1) * Nw)
            gate = jnp.dot(xn, w_ref[j], preferred_element_type=f32)
            up = jnp.dot(xn, w_ref[half + j], preferred_element_type=f32)
            gu_ref[0, :, cols] = gate.astype(bf16)
            gu_ref[1, :, cols] = up.astype(bf16)
            act_ref[:, cols] = ((gate * _sigmoid(gate)) * up).astype(bf16)

    return pl.pallas_call(
        body, name=name, grid=(S // tm,),
        in_specs=[pl.BlockSpec((tm, K), lambda i: (i, 0)), pl.BlockSpec((1, K), lambda i: (0, 0)),
                  pl.BlockSpec((G, K, Nw), lambda i: (0, 0, 0))],
        out_specs=[pl.BlockSpec((tm, K), lambda i: (i, 0)), pl.BlockSpec((2, tm, half * Nw), lambda i: (0, i, 0)),
                   pl.BlockSpec((tm, half * Nw), lambda i: (i, 0))],
        out_shape=[_sds((S, K), bf16), _sds((2, S, half * Nw), bf16), _sds((S, half * Nw), bf16)],
        compiler_params=_params(1, True))(x, g, w3)


def _mm_tn(name, a, a_spec, b, b_spec, out_shape, out_spec, grid, acc_shape):
    return _mm(name, "tn", a, a_spec, b, b_spec, out_shape, out_spec, grid, grid[-1], acc_shape)


def _sigmoid(x):
    return 0.5 * jnp.tanh(0.5 * x) + 0.5


_GELU_C = math.sqrt(2.0 / math.pi)


def _gelu_and_grad(x):
    v = _GELU_C * (x + 0.044715 * (x * x * x))
    t = jnp.tanh(v)
    gl = 0.5 * x * (1.0 + t)
    dgl = 0.5 * (1.0 + t) + 0.5 * x * (1.0 - t * t) * (_GELU_C * (1.0 + 3.0 * 0.044715 * (x * x)))
    return gl, dgl


def _one_minus_exp2x(x, ex):
    y = 2.0 * x
    series = y * (1.0 + y * (0.5 + y * (1.0 / 6.0 + y * (1.0 / 24.0))))
    return jnp.where(y > -1.0 / 64.0, -series, 1.0 - ex * ex)


def _z_specs(tm):
    return [pl.BlockSpec((tm, MERGE_W), lambda i, p=p: (i, MERGE_Z0 + p)) for p in range(2 * D // MERGE_W)]


def _merge_out_proj(proj, b_gate, y_a, y_b, w, res, tm):
    S = proj.shape[0]
    tm = min(tm, S)
    per = D // MERGE_W
    nz = 2 * per

    def body(*refs):
        z = refs[:nz]
        b_ref, ya_ref, yb_ref, w_ref, r_ref, m_ref, x_ref = refs[nz:]
        for p in range(per):
            cols = slice(p * MERGE_W, (p + 1) * MERGE_W)
            g0 = _sigmoid(z[p][...].astype(f32) + b_ref[:, p * MERGE_W:(p + 1) * MERGE_W])
            g1 = _sigmoid(z[per + p][...].astype(f32) + b_ref[:, D + p * MERGE_W:D + (p + 1) * MERGE_W])
            m_ref[:, cols] = (g0 * ya_ref[:, cols].astype(f32) + g1 * yb_ref[:, cols].astype(f32)).astype(bf16)
        x_ref[...] = r_ref[...] + jnp.dot(m_ref[...], w_ref[...], preferred_element_type=f32)

    row = pl.BlockSpec((tm, D), lambda i: (i, 0))
    return pl.pallas_call(
        body, name="merge_out_proj", grid=(S // tm,),
        in_specs=_z_specs(tm) + [pl.BlockSpec((1, 2 * D), lambda i: (0, 0)), row, row, pl.BlockSpec((D, D), lambda i: (0, 0)), row],
        out_specs=[row, row], out_shape=[_sds((S, D), bf16), _sds((S, D), f32)],
        compiler_params=_params(1, True))(*([proj] * nz), b_gate, y_a, y_b, w, res)


def _merge_bwd(proj, b_gate, y_a, y_b, dx, w, tm):
    S = proj.shape[0]
    tm = min(tm, S)
    per = D // MERGE_W
    nz = 2 * per
    nsteps = S // tm
    z_col = MERGE_Z0 * MERGE_W

    def body(*refs):
        z = refs[:nz]
        b_ref, ya_ref, yb_ref, dx_ref, w_ref, dproj_ref, dy_ref, db_ref, dz_buf, sems = refs[nz:]
        i = pl.program_id(0)
        slot = i % 2

        def dz_copy(step):
            rows = pl.ds(pl.multiple_of(step * tm, tm), tm)
            return pltpu.make_async_copy(dz_buf.at[step % 2], dproj_ref.at[rows, pl.ds(z_col, 2 * D)], sems.at[step % 2])

        @pl.when(i >= 2)
        def _():
            dz_copy(i - 2).wait()

        @pl.when(i == 0)
        def _():
            db_ref[...] = jnp.zeros_like(db_ref)

        dm = lax.dot_general(dx_ref[...].astype(bf16), w_ref[...], _DIMS["nt"], preferred_element_type=f32)
        for p in range(nz):
            branch, cols = p // per, slice((p % per) * MERGE_W, (p % per + 1) * MERGE_W)
            zc = slice(p * MERGE_W, (p + 1) * MERGE_W)
            g = _sigmoid(z[p][...].astype(f32) + b_ref[:, zc])
            d = dm[:, cols]
            y = (ya_ref if branch == 0 else yb_ref)[:, cols].astype(f32)
            dz = (d * y) * (g * (1.0 - g))
            dz_buf[slot, :, zc] = dz.astype(bf16)
            dy_ref[branch, :, cols] = (d * g).astype(bf16)
            db_ref[:, zc] += jnp.sum(dz, axis=0, keepdims=True)
        dz_copy(i).start()

        @pl.when(i == nsteps - 1)
        def _():
            if nsteps >= 2:
                dz_copy(i - 1).wait()
            dz_copy(i).wait()

    row = pl.BlockSpec((tm, D), lambda i: (i, 0))
    return pl.pallas_call(
        body, name="merge_bwd", grid=(nsteps,),
        in_specs=_z_specs(tm) + [pl.BlockSpec((1, 2 * D), lambda i: (0, 0)), row, row, row, pl.BlockSpec((D, D), lambda i: (0, 0))],
        out_specs=[ANY, pl.BlockSpec((2, tm, D), lambda i: (0, i, 0)), pl.BlockSpec((1, 2 * D), lambda i: (0, 0))],
        out_shape=[_sds((S, INW), bf16), _sds((2, S, D), bf16), _sds((1, 2 * D), f32)],
        scratch_shapes=[pltpu.VMEM((2, tm, 2 * D), bf16), pltpu.SemaphoreType.DMA((2,))],
        compiler_params=_params(1, True))(*([proj] * nz), b_gate, y_a, y_b, dx, w)


def _swiglu_bwd(dx, w, gu, tm):
    S, K = dx.shape
    tm = min(tm, S)

    def body(dx_ref, w_ref, gu_ref, o_ref):
        d = lax.dot_general(dx_ref[...].astype(bf16), w_ref[...], _DIMS["nt"], preferred_element_type=f32)
        g = gu_ref[0].astype(f32)
        u = gu_ref[1].astype(f32)
        s = _sigmoid(g)
        o_ref[0] = ((d * u) * (s * (1.0 + g * (1.0 - s)))).astype(bf16)
        o_ref[1] = (d * (g * s)).astype(bf16)

    stacked = pl.BlockSpec((2, tm, FF), lambda i: (0, i, 0))
    return pl.pallas_call(body, name="swiglu_bwd", grid=(S // tm,),
                          in_specs=[pl.BlockSpec((tm, K), lambda i: (i, 0)), pl.BlockSpec((FF, K), lambda i: (0, 0)), stacked],
                          out_specs=stacked, out_shape=_sds((2, S, FF), bf16),
                          compiler_params=_params(1, True))(dx, w, gu)


def _ffn_out_loss_bwd(act, w, x1, g3, tgt, tm):
    S, K = act.shape
    tm = min(tm, S)

    def body(a_ref, w_ref, r_ref, g_ref, t_ref, dx_ref, loss_ref, dg_ref):
        @pl.when(pl.program_id(0) == 0)
        def _():
            loss_ref[...] = jnp.zeros_like(loss_ref)
            dg_ref[...] = jnp.zeros_like(dg_ref)

        x = r_ref[...] + jnp.dot(a_ref[...], w_ref[...], preferred_element_type=f32)
        g = g_ref[...]
        r = lax.rsqrt(jnp.mean(x * x, axis=-1, keepdims=True) + EPS)
        xh = x * r
        err = xh * g - t_ref[...]
        row = jnp.mean(err * err, axis=-1, keepdims=True)
        loss_ref[...] += 0.5 * jnp.sum(row, axis=0, keepdims=True)
        dy = err * (1.0 / D)
        dg_ref[...] += jnp.sum(dy * xh, axis=0, keepdims=True)
        dxh = dy * g
        dx_ref[...] = r * (dxh - xh * jnp.mean(dxh * xh, axis=-1, keepdims=True))

    row_blk = pl.BlockSpec((tm, D), lambda i: (i, 0))
    vec = pl.BlockSpec((1, D), lambda i: (0, 0))
    return pl.pallas_call(body, name="ffn_out_loss_bwd", grid=(S // tm,),
                          in_specs=[pl.BlockSpec((tm, K), lambda i: (i, 0)), pl.BlockSpec((K, D), lambda i: (0, 0)),
                                    row_blk, vec, row_blk],
                          out_specs=[row_blk, pl.BlockSpec((1, 128), lambda i: (0, 0)), vec],
                          out_shape=[_sds((S, D), f32), _sds((1, 128), f32), _sds((1, D), f32)],
                          compiler_params=_params(1, True))(act, w, x1, g3, tgt)


def _mm_nt_rms_bwd(name, a, w3, x, g, dres, tm):
    S = x.shape[0]
    G, Dout, Kw = w3.shape
    tm = min(tm, S)
    planes = a.shape[0] if a.ndim == 3 else 1
    per = G // planes

    def body(a_ref, w_ref, x_ref, g_ref, r_ref, dx_ref, dg_ref):
        @pl.when(pl.program_id(0) == 0)
        def _():
            dg_ref[...] = jnp.zeros_like(dg_ref)

        d = None
        for k in range(G):
            cols = slice((k % per) * Kw, (k % per + 1) * Kw)
            a_k = a_ref[k // per, :, cols] if a.ndim == 3 else a_ref[:, cols]
            part = lax.dot_general(a_k, w_ref[k], _DIMS["nt"], preferred_element_type=f32)
            d = part if d is None else d + part
        x_t = x_ref[...]
        r = lax.rsqrt(jnp.mean(x_t * x_t, axis=-1, keepdims=True) + EPS)
        xh = x_t * r
        dg_ref[...] += jnp.sum(d * xh, axis=0, keepdims=True)
        dxh = d * g_ref[...]
        dx_ref[...] = r_ref[...] + r * (dxh - xh * jnp.mean(dxh * xh, axis=-1, keepdims=True))

    row_blk = pl.BlockSpec((tm, Dout), lambda i: (i, 0))
    vec = pl.BlockSpec((1, Dout), lambda i: (0, 0))
    a_spec = (pl.BlockSpec((planes, tm, per * Kw), lambda i: (0, i, 0)) if a.ndim == 3
              else pl.BlockSpec((tm, G * Kw), lambda i: (i, 0)))
    return pl.pallas_call(body, name=name, grid=(S // tm,),
                          in_specs=[a_spec, pl.BlockSpec((G, Dout, Kw), lambda i: (0, 0, 0)), row_blk, vec, row_blk],
                          out_specs=[row_blk, vec], out_shape=[_sds((S, Dout), f32), _sds((1, Dout), f32)],
                          compiler_params=_params(1, True))(a, w3, x, g, dres)


LRU_TT = 256
SCAN_UNROLL = 8


HALO = 16


def _halo(ref, i, S):
    nt = S // LRU_TT
    t0 = pl.multiple_of(i * LRU_TT, LRU_TT)
    p0 = pl.multiple_of(jnp.maximum(t0 - HALO, 0), HALO)
    n0 = pl.multiple_of(jnp.minimum(t0 + LRU_TT, S - HALO), HALO)
    prev = jnp.where(i > 0, ref[pl.ds(p0, HALO), :].astype(f32), 0.0)
    nxt = jnp.where(i < nt - 1, ref[pl.ds(n0, HALO), :].astype(f32), 0.0)
    return jnp.concatenate([prev, ref[pl.ds(t0, LRU_TT), :].astype(f32), nxt], axis=0)


def _shift(ext, k):
    n = LRU_TT + 2 * HALO
    return pltpu.roll(ext, (-k) % n, 0)[HALO:HALO + LRU_TT]


def _lru_gates(uc, wbd, ba, bx):
    pre = jnp.dot(uc.astype(bf16), wbd, preferred_element_type=f32)
    r_f = _sigmoid(pre[:, 0:CW] + ba[0:1])
    i_f = _sigmoid(pre[:, CW:2 * CW] + bx[0:1])
    r_b = _sigmoid(pre[:, 2 * CW:3 * CW] + ba[1:2])
    i_b = _sigmoid(pre[:, 3 * CW:4 * CW] + bx[1:2])
    return r_f, i_f, r_b, i_b


def _lru_coeffs(r, sp):
    log_a = (-RGLRU_C * r) * sp
    a = jnp.exp(log_a)
    beta = jnp.sqrt(jnp.maximum(_one_minus_exp2x(log_a, a), 0.0))
    return a, beta


def _lru_coeffs_inv(r, sp):
    log_a = (-RGLRU_C * r) * sp
    a = jnp.exp(log_a)
    om = jnp.maximum(_one_minus_exp2x(log_a, a), 0.0)
    return a, jnp.sqrt(om), lax.rsqrt(jnp.maximum(om, 1e-30))


def _conv_tile(u_ref, i, S, cw, cb):
    ext = _halo(u_ref, i, S)
    um2, um1, u0, up1 = _shift(ext, -2), _shift(ext, -1), ext[HALO:HALO + LRU_TT], _shift(ext, 1)
    uc = um2 * cw[0:1] + um1 * cw[1:2] + u0 * cw[2:3] + up1 * cw[3:4] + cb
    return uc, (um2, um1, u0, up1)


def _scan_pair(S, fwd_a, fwd_b, fwd_out, rev_a, rev_b, rev_out):
    ng = S // 8
    idx = lax.broadcasted_iota(jnp.int32, (8, CW), 0)

    def local(a, b, rev):
        for sh in (1, 2, 4):
            if rev:
                keep = idx < 8 - sh
                amt = 8 - sh
            else:
                keep = idx >= sh
                amt = sh
            a_s = jnp.where(keep, pltpu.roll(a, amt, 0), 1.0)
            b_s = jnp.where(keep, pltpu.roll(b, amt, 0), 0.0)
            b = a * b_s + b
            a = a * a_s
        return a, b

    def step(it, carry):
        cf, cr = carry
        fwd_rows = [pl.multiple_of((it * SCAN_UNROLL + j) * 8, 8) for j in range(SCAN_UNROLL)]
        rev_rows = [pl.multiple_of((ng - 1 - (it * SCAN_UNROLL + j)) * 8, 8) for j in range(SCAN_UNROLL)]
        fwd_loc = [local(fwd_a(r), fwd_b(r), False) for r in fwd_rows]
        rev_loc = [local(rev_a(r), rev_b(r), True) for r in rev_rows]
        for j in range(SCAN_UNROLL):
            a, b = fwd_loc[j]
            h = a * cf + b
            fwd_out[pl.ds(fwd_rows[j], 8), :] = h
            cf = jnp.broadcast_to(h[7:8, :], (8, CW))
            a, b = rev_loc[j]
            h = a * cr + b
            rev_out[pl.ds(rev_rows[j], 8), :] = h
            cr = jnp.broadcast_to(h[0:1, :], (8, CW))
        return cf, cr

    zero = jnp.zeros((8, CW), f32)
    lax.fori_loop(0, ng // SCAN_UNROLL, step, (zero, zero))


def _lru_specs(S):
    seq = lambda off: pl.BlockSpec((S, CW), lambda j: (0, off + j))
    par = lambda rows: pl.BlockSpec((rows, CW), lambda j: (0, j))
    return seq, par


def _lru_fwd(proj, conv_w, conv_b, lam, ba, bx, wbd):
    S = proj.shape[0]
    nt = S // LRU_TT

    def body(u_ref, g_ref, cw_ref, cb_ref, lam_ref, ba_ref, bx_ref, wbd_ref, y_ref, state_ref, af_ref, bf_ref, ab_ref, bb_ref,
             sems):
        cw, cb, ba_v, bx_v, wbd_v = cw_ref[...], cb_ref[...], ba_ref[...], bx_ref[...], wbd_ref[...]
        sp = jax.nn.softplus(-lam_ref[...])
        cols = pl.ds(pl.multiple_of(pl.program_id(0) * CW, CW), CW)
        save = [pltpu.make_async_copy(ref, state_ref.at[k, :, cols], sems.at[k])
                for k, ref in enumerate((af_ref, bf_ref, ab_ref, bb_ref))]

        def phase1(i, c):
            uc, _ = _conv_tile(u_ref, i, S, cw, cb)
            r_f, i_f, r_b, i_b = _lru_gates(uc, wbd_v, ba_v, bx_v)
            rows = pl.ds(pl.multiple_of(i * LRU_TT, LRU_TT), LRU_TT)
            a, beta = _lru_coeffs(r_f, sp[0:1])
            af_ref[rows, :] = a
            bf_ref[rows, :] = beta * (i_f * uc)
            a, beta = _lru_coeffs(r_b, sp[1:2])
            ab_ref[rows, :] = a
            bb_ref[rows, :] = beta * (i_b * uc)
            return c

        lax.fori_loop(0, nt, phase1, 0)
        save[0].start()
        save[2].start()
        row8 = lambda ref: (lambda r0: ref[pl.ds(r0, 8), :])
        _scan_pair(S, row8(af_ref), row8(bf_ref), bf_ref, row8(ab_ref), row8(bb_ref), bb_ref)
        save[1].start()
        save[3].start()

        def phase3(i, c):
            rows = pl.ds(pl.multiple_of(i * LRU_TT, LRU_TT), LRU_TT)
            y = (bf_ref[rows, :] + bb_ref[rows, :]) * jax.nn.gelu(g_ref[rows, :].astype(f32))
            y_ref[rows, :] = y.astype(y_ref.dtype)
            return c

        lax.fori_loop(0, nt, phase3, 0)
        for cp in save:
            cp.wait()

    seq, par = _lru_specs(S)
    return pl.pallas_call(
        body, name="lru_fwd", grid=(NCH,),
        in_specs=[seq(0), seq(NCH), par(4), par(1), par(2), par(2), par(2),
                  pl.BlockSpec((None, CW, 4 * CW), lambda j: (j, 0, 0))],
        out_specs=[seq(0), ANY], out_shape=[_sds((S, D), bf16), _sds((4, S, D), f32)],
        scratch_shapes=[pltpu.VMEM((S, CW), f32)] * 4 + [pltpu.SemaphoreType.DMA((4,))], compiler_params=_params(1, True),
    )(proj, proj, conv_w, conv_b, lam, ba, bx, wbd)


def _lru_bwd(proj, dy, state, dproj, conv_w, conv_b, lam, ba, bx, wbd):
    S = proj.shape[0]
    nt = S // LRU_TT

    def body(u_ref, g_ref, dy_ref, state_ref, dproj_in, cw_ref, cb_ref, lam_ref, ba_ref, bx_ref, wbd_ref,
             dproj_ref, dcw_ref, dcb_ref, dlam_ref, dba_ref, dbx_ref, dwbd_ref,
             af_ref, hf2_ref, ab_ref, hb2_ref, dh_ref, du_ref, dg_ref, sems):
        cw, cb, ba_v, bx_v, wbd_v = cw_ref[...], cb_ref[...], ba_ref[...], bx_ref[...], wbd_ref[...]
        lam_v = lam_ref[...]
        sp = jax.nn.softplus(-lam_v)
        chunk = pl.program_id(0)
        slot = chunk % 2
        bf_ref, bb_ref = hf2_ref.at[slot], hb2_ref.at[slot]

        def out_copies(j):
            c0 = pl.multiple_of(j * CW, CW)
            return [pltpu.make_async_copy(du_ref, dproj_ref.at[:, pl.ds(c0, CW)], sems.at[4]),
                    pltpu.make_async_copy(dg_ref, dproj_ref.at[:, pl.ds(D + c0, CW)], sems.at[5])]

        @pl.when(chunk >= 1)
        def _():
            for cp in out_copies(chunk - 1):
                cp.wait()

        def state_copy(k, j, dst, sem):
            return pltpu.make_async_copy(state_ref.at[k, :, pl.ds(pl.multiple_of(j * CW, CW), CW)], dst, sem)

        def hidden_loads(j):
            return [state_copy(1, j, hf2_ref.at[j % 2], sems.at[6 + j % 2]), state_copy(3, j, hb2_ref.at[j % 2], sems.at[8 + j % 2])]

        load = [state_copy(0, chunk, af_ref, sems.at[0]), None, state_copy(2, chunk, ab_ref, sems.at[2])]

        @pl.when(chunk == 0)
        def _():
            for cp in hidden_loads(chunk):
                cp.start()

        load[0].start()
        load[2].start()

        @pl.when(chunk + 1 < NCH)
        def _():
            for cp in hidden_loads(chunk + 1):
                cp.start()

        for cp in hidden_loads(chunk):
            cp.wait()
        row8 = lambda ref: (lambda r0: ref[pl.ds(r0, 8), :])

        def phase0(i, c):
            rows = pl.ds(pl.multiple_of(i * LRU_TT, LRU_TT), LRU_TT)
            gl, dgl = _gelu_and_grad(g_ref[rows, :].astype(f32))
            dyt = dy_ref[rows, :].astype(f32)
            dh_ref[rows, :] = dyt * gl
            dg_ref[rows, :] = ((dyt * (bf_ref[rows, :] + bb_ref[rows, :])) * dgl).astype(dg_ref.dtype)
            return c

        lax.fori_loop(0, nt, phase0, 0)
        load[0].wait()
        load[2].wait()

        def scaled_dh(a_ref):
            def f(r0):
                return a_ref[pl.ds(r0, 8), :] * dh_ref[pl.ds(r0, 8), :]
            return f

        _scan_pair(S, row8(ab_ref), scaled_dh(ab_ref), ab_ref, row8(af_ref), scaled_dh(af_ref), af_ref)

        dcw_ref[...] = jnp.zeros_like(dcw_ref)
        dcb_ref[...] = jnp.zeros_like(dcb_ref)
        dlam_ref[...] = jnp.zeros_like(dlam_ref)
        dba_ref[...] = jnp.zeros_like(dba_ref)
        dbx_ref[...] = jnp.zeros_like(dbx_ref)
        dwbd_ref[...] = jnp.zeros_like(dwbd_ref)

        def direction(uc, r, i_g, dht, h_nb, sp_d):
            a, beta, inv_beta = _lru_coeffs_inv(r, sp_d)
            da = dht * h_nb
            dbeta = dht * (i_g * uc)
            d_iu = dht * beta
            dlog_a = da * a - (a * a) * (dbeta * inv_beta)
            dlr = dlog_a * r
            dsp = -RGLRU_C * jnp.sum(dlr, axis=0, keepdims=True)
            dpre_r = (dlr * (1.0 - r)) * (-RGLRU_C * sp_d)
            dpre_i = (d_iu * uc) * (i_g * (1.0 - i_g))
            return dpre_r, dpre_i, d_iu * i_g, dsp

        def phase4(i, c):
            uc, (um2, um1, u0, up1) = _conv_tile(u_ref, i, S, cw, cb)
            r_f, i_f, r_b, i_b = _lru_gates(uc, wbd_v, ba_v, bx_v)
            rows = pl.ds(pl.multiple_of(i * LRU_TT, LRU_TT), LRU_TT)
            dh = dh_ref[rows, :]
            dht_f = dh + _shift(_halo(af_ref, i, S), 1)
            h_prev = _shift(_halo(bf_ref, i, S), -1)
            dht_b = dh + _shift(_halo(ab_ref, i, S), -1)
            h_next = _shift(_halo(bb_ref, i, S), 1)
            prf, pif, duc_f, dsp_f = direction(uc, r_f, i_f, dht_f, h_prev, sp[0:1])
            prb, pib, duc_b, dsp_b = direction(uc, r_b, i_b, dht_b, h_next, sp[1:2])
            dpre = jnp.concatenate([prf, pif, prb, pib], axis=1)
            dpre_b = dpre.astype(bf16)
            duc = (duc_f + duc_b) + lax.dot_general(dpre_b, wbd_v, _DIMS["nt"], preferred_element_type=f32)
            dwbd_ref[...] += lax.dot_general(uc.astype(bf16), dpre_b, _DIMS["tn"], preferred_element_type=f32)
            colsum = lambda v: jnp.sum(v, axis=0, keepdims=True)
            dba_ref[...] += jnp.concatenate([colsum(prf), colsum(prb)], axis=0)
            dbx_ref[...] += jnp.concatenate([colsum(pif), colsum(pib)], axis=0)
            dlam_ref[...] += jnp.concatenate([dsp_f, dsp_b], axis=0)
            dcb_ref[...] += colsum(duc)
            dcw_ref[...] += jnp.concatenate([colsum(duc * um2), colsum(duc * um1), colsum(duc * u0),
                                             colsum(duc * up1)], axis=0)
            af_ref[rows, :] = duc
            return c

        lax.fori_loop(0, nt, phase4, 0)
        dlam_ref[...] = dlam_ref[...] * (-_sigmoid(-lam_v))

        def phase5(i, c):
            ext = _halo(af_ref, i, S)
            rows = pl.ds(pl.multiple_of(i * LRU_TT, LRU_TT), LRU_TT)
            du = (_shift(ext, 2) * cw[0:1] + _shift(ext, 1) * cw[1:2] + ext[HALO:HALO + LRU_TT] * cw[2:3]
                  + _shift(ext, -1) * cw[3:4])
            du_ref[rows, :] = du.astype(du_ref.dtype)
            return c

        lax.fori_loop(0, nt, phase5, 0)
        for cp in out_copies(chunk):
            cp.start()

        @pl.when(chunk == NCH - 1)
        def _():
            for cp in out_copies(chunk):
                cp.wait()

    seq, par = _lru_specs(S)
    return pl.pallas_call(
        body, name="lru_bwd", grid=(NCH,),
        in_specs=[seq(0), seq(NCH), pl.BlockSpec((None, S, CW), lambda j: (0, 0, j)), ANY, ANY,
                  par(4), par(1), par(2), par(2), par(2), pl.BlockSpec((None, CW, 4 * CW), lambda j: (j, 0, 0))],
        out_specs=[ANY, par(4), par(1), par(2), par(2), par(2),
                   pl.BlockSpec((None, CW, 4 * CW), lambda j: (j, 0, 0))],
        out_shape=[_sds(dproj.shape, bf16), _sds((4, D), f32), _sds((1, D), f32), _sds((2, D), f32),
                   _sds((2, D), f32), _sds((2, D), f32), _sds((NCH, CW, 4 * CW), f32)],
        scratch_shapes=[pltpu.VMEM((S, CW), f32), pltpu.VMEM((2, S, CW), f32), pltpu.VMEM((S, CW), f32), pltpu.VMEM((2, S, CW), f32),
                        pltpu.VMEM((S, CW), f32), pltpu.VMEM((S, CW), bf16), pltpu.VMEM((S, CW), bf16),
                        pltpu.SemaphoreType.DMA((10,))],
        input_output_aliases={4: 0}, compiler_params=_params(1, True),
    )(proj, proj, dy, state, dproj, conv_w, conv_b, lam, ba, bx, wbd)


_SLOPES = [2.0 ** (-8.0 * (h + 1) / NH) for h in range(NH)]


def _half_mask(shape, e):
    lane = lax.broadcasted_iota(jnp.int32, shape, 1)
    return (lane < HD) if e == 0 else (lane >= HD)


def _both_halves(x, src):
    return jnp.where(_half_mask(x.shape, src), x, pltpu.roll(x, HD, 1))


def _attn_base(n, S):
    tq = lax.broadcasted_iota(jnp.int32, (BLK, 3 * BLK), 0)
    sk = lax.broadcasted_iota(jnp.int32, (BLK, 3 * BLK), 1)
    dist = jnp.abs(tq + BLK - sk)
    kpos = n * BLK - BLK + sk
    valid = (dist <= BLK) & (kpos >= 0) & (kpos < S)
    return jnp.where(valid, -dist.astype(f32), NEG_INF)


def _group_heads(ref, kvh, scale):
    parts = []
    for i in range(4):
        pair = 2 * kvh + i // 2
        x = ref[:, pair * 128:(pair + 1) * 128].astype(f32)
        parts.append(jnp.where(_half_mask(x.shape, i % 2), x * scale, 0.0))
    return parts


def _stack_bf16(parts):
    return jnp.concatenate([p.astype(bf16) for p in parts], axis=0)


def _attn_softmax(s_raw, base, slope, sink):
    s = s_raw + slope * base
    m = jnp.maximum(jnp.max(s, axis=-1, keepdims=True), sink)
    p = jnp.exp(s - m)
    esink = jnp.exp(sink - m)
    inv = 1.0 / (jnp.sum(p, axis=-1, keepdims=True) + esink)
    return p, inv, esink * inv


def _attn_specs(S):
    nb = S // BLK
    q_spec = pl.BlockSpec((BLK, D), lambda n: (n, 2))
    kv = lambda col: [pl.BlockSpec((BLK, 256), lambda n: (jnp.maximum(n - 1, 0), col)),
                      pl.BlockSpec((BLK, 256), lambda n: (n, col)),
                      pl.BlockSpec((BLK, 256), lambda n: (jnp.minimum(n + 1, nb - 1), col))]
    return nb, q_spec, kv(COL_K), kv(COL_V)


def _attn_fwd(proj, sink):
    S = proj.shape[0]
    nb, q_spec, k_specs, v_specs = _attn_specs(S)

    def body(sink_ref, q_ref, kp_ref, kc_ref, kn_ref, vp_ref, vc_ref, vn_ref, o_ref):
        base = _attn_base(pl.program_id(0), S)
        kcat = jnp.concatenate([kp_ref[...], kc_ref[...], kn_ref[...]], axis=0).astype(f32)
        vcat = jnp.concatenate([vp_ref[...], vc_ref[...], vn_ref[...]], axis=0).astype(f32)
        even = _half_mask((BLK, 128), 0)
        for kvh in range(NH // 4):
            ch, off = kvh // 2, kvh % 2
            kb = _both_halves(kcat[:, ch * 128:(ch + 1) * 128], off).astype(bf16)
            vb = _both_halves(vcat[:, ch * 128:(ch + 1) * 128], off).astype(bf16)
            q4 = _stack_bf16(_group_heads(q_ref, kvh, HD ** -0.5))
            s4 = lax.dot_general(q4, kb, _DIMS["nt"], preferred_element_type=f32)
            ps, invs = [], []
            for i in range(4):
                h = 4 * kvh + i
                p, inv, _ = _attn_softmax(s4[i * BLK:(i + 1) * BLK], base, _SLOPES[h], sink_ref[0, h])
                ps.append(p)
                invs.append(inv)
            o4 = jnp.dot(_stack_bf16(ps), vb, preferred_element_type=f32)
            for pr in range(2):
                lo = o4[(2 * pr) * BLK:(2 * pr + 1) * BLK] * invs[2 * pr]
                hi = o4[(2 * pr + 1) * BLK:(2 * pr + 2) * BLK] * invs[2 * pr + 1]
                pair = 2 * kvh + pr
                o_ref[:, pair * 128:(pair + 1) * 128] = jnp.where(even, lo, hi).astype(o_ref.dtype)

    return pl.pallas_call(
        body, name="attn_fwd", grid=(nb,),
        in_specs=[pl.BlockSpec(memory_space=pltpu.SMEM), q_spec] + k_specs + v_specs,
        out_specs=pl.BlockSpec((BLK, D), lambda n: (n, 0)), out_shape=_sds((S, D), bf16),
        compiler_params=_params(1, True))(sink, proj, proj, proj, proj, proj, proj, proj)


def _attn_bwd(proj, sink, y_b, dy, dproj):
    S = proj.shape[0]
    nb, q_spec, k_specs, v_specs = _attn_specs(S)
    q_col, kv_col = COL_Q * 256, COL_K * 256

    def body(sink_ref, q_ref, kp_ref, kc_ref, kn_ref, vp_ref, vc_ref, vn_ref, o_ref, do_ref, dproj_in,
             dproj_ref, dsink_ref, dk_ref, dv_ref, dq_buf, kv_buf, sems):
        n = pl.program_id(0)
        slot = n % 2
        dq_ref = dq_buf.at[slot]

        def dq_copy(step):
            rows = pl.ds(pl.multiple_of(step * BLK, BLK), BLK)
            return pltpu.make_async_copy(dq_buf.at[step % 2], dproj_ref.at[rows, pl.ds(q_col, D)], sems.at[step % 2])

        @pl.when(n >= 2)
        def _():
            dq_copy(n - 2).wait()

        @pl.when(n == 0)
        def _():
            dk_ref[...] = jnp.zeros_like(dk_ref)
            dv_ref[...] = jnp.zeros_like(dv_ref)
            dsink_ref[...] = jnp.zeros_like(dsink_ref)

        base = _attn_base(n, S)
        kcat = jnp.concatenate([kp_ref[...], kc_ref[...], kn_ref[...]], axis=0).astype(f32)
        vcat = jnp.concatenate([vp_ref[...], vc_ref[...], vn_ref[...]], axis=0).astype(f32)
        dk_rows, dv_rows = [[], []], [[], []]
        scale = HD ** -0.5
        even = _half_mask((BLK, 128), 0)
        for kvh in range(NH // 4):
            ch, off = kvh // 2, kvh % 2
            kb = _both_halves(kcat[:, ch * 128:(ch + 1) * 128], off).astype(bf16)
            vb = _both_halves(vcat[:, ch * 128:(ch + 1) * 128], off).astype(bf16)
            q_parts = _group_heads(q_ref, kvh, scale)
            d_parts = _group_heads(do_ref, kvh, 1.0)
            s4 = lax.dot_general(_stack_bf16(q_parts), kb, _DIMS["nt"], preferred_element_type=f32)
            dp4 = lax.dot_general(_stack_bf16(d_parts), vb, _DIMS["nt"], preferred_element_type=f32)
            ts, ps, qn, dn, invs = [], [], [], [], []
            for i in range(4):
                h = 4 * kvh + i
                pair = 2 * kvh + i // 2
                rows = slice(i * BLK, (i + 1) * BLK)
                p, inv, psink = _attn_softmax(s4[rows], base, _SLOPES[h], sink_ref[0, h])
                delta = jnp.sum(d_parts[i] * o_ref[:, pair * 128:(pair + 1) * 128].astype(f32), axis=-1, keepdims=True)
                dsink_ref[h:h + 1, :] += jnp.broadcast_to(-jnp.sum(psink * delta, axis=0, keepdims=True), (1, 128))
                ts.append(p * (dp4[rows] - delta))
                ps.append(p)
                qn.append(q_parts[i] * inv)
                dn.append(d_parts[i] * inv)
                invs.append(inv)
            t4 = _stack_bf16(ts)
            dq4 = jnp.dot(t4, kb, preferred_element_type=f32)
            for pr in range(2):
                lo = dq4[(2 * pr) * BLK:(2 * pr + 1) * BLK] * invs[2 * pr]
                hi = dq4[(2 * pr + 1) * BLK:(2 * pr + 2) * BLK] * invs[2 * pr + 1]
                pair = 2 * kvh + pr
                dq_ref[:, pair * 128:(pair + 1) * 128] = (jnp.where(even, lo, hi) * scale).astype(dq_ref.dtype)
            dk_t = lax.dot_general(_stack_bf16(qn), t4, _DIMS["tn"], preferred_element_type=f32)
            dv_t = lax.dot_general(_stack_bf16(dn), _stack_bf16(ps), _DIMS["tn"], preferred_element_type=f32)
            dk_rows[ch].append(dk_t[0:HD] + dk_t[HD:2 * HD])
            dv_rows[ch].append(dv_t[0:HD] + dv_t[HD:2 * HD])
        dk_acc = [jnp.concatenate(r, axis=0).T for r in dk_rows]
        dv_acc = [jnp.concatenate(r, axis=0).T for r in dv_rows]
        for j in range(3):
            blk = n + (j - 1)

            @pl.when((blk >= 0) & (blk < nb))
            def _():
                rows = pl.ds(pl.multiple_of(blk * BLK, BLK), BLK)
                for ch in range(2):
                    dk_ref[rows, ch * 128:(ch + 1) * 128] += dk_acc[ch][j * BLK:(j + 1) * BLK]
                    dv_ref[rows, ch * 128:(ch + 1) * 128] += dv_acc[ch][j * BLK:(j + 1) * BLK]

        dq_copy(n).start()

        @pl.when(n == nb - 1)
        def _():
            def cast(i, c):
                rows = pl.ds(pl.multiple_of(i * 4 * BLK, 4 * BLK), 4 * BLK)
                kv_buf[rows, 0:256] = dk_ref[rows, :].astype(bf16)
                kv_buf[rows, 256:512] = dv_ref[rows, :].astype(bf16)
                return c

            lax.fori_loop(0, S // (4 * BLK), cast, 0)
            kv_copy = pltpu.make_async_copy(kv_buf, dproj_ref.at[:, pl.ds(kv_col, 512)], sems.at[2])
            kv_copy.start()
            if nb >= 2:
                dq_copy(n - 1).wait()
            dq_copy(n).wait()
            kv_copy.wait()

    row_blk = pl.BlockSpec((BLK, D), lambda n: (n, 0))
    return pl.pallas_call(
        body, name="attn_bwd", grid=(nb,),
        in_specs=[pl.BlockSpec(memory_space=pltpu.SMEM), q_spec] + k_specs + v_specs
        + [row_blk, pl.BlockSpec((None, BLK, D), lambda n: (1, n, 0)), ANY],
        out_specs=[ANY, pl.BlockSpec((NH, 128), lambda n: (0, 0))],
        out_shape=[_sds(dproj.shape, bf16), _sds((NH, 128), f32)],
        scratch_shapes=[pltpu.VMEM((S, 256), f32), pltpu.VMEM((S, 256), f32), pltpu.VMEM((2, BLK, D), bf16),
                        pltpu.VMEM((S, 512), bf16), pltpu.SemaphoreType.DMA((3,))],
        input_output_aliases={10: 0},
        compiler_params=_params(1, True))(sink, proj, proj, proj, proj, proj, proj, proj, y_b, dy, dproj)


def _adamw(name, w, g, m, v, tr):
    R, C = w.shape
    tr = min(tr, R)

    def body(w_ref, g_ref, m_ref, v_ref, d_ref, m2_ref, v2_ref):
        g = g_ref[...]
        m2 = ADAM_B1 * m_ref[...] + (1.0 - ADAM_B1) * g
        v2 = ADAM_B2 * v_ref[...] + (1.0 - ADAM_B2) * (g * g)
        m_hat = m2 / (1.0 - ADAM_B1 ** ADAM_STEP)
        v_hat = v2 / (1.0 - ADAM_B2 ** ADAM_STEP)
        d_ref[...] = -ADAM_LR * (m_hat / (jnp.sqrt(v_hat) + ADAM_EPS) + ADAM_WD * w_ref[...])
        m2_ref[...] = m2
        v2_ref[...] = v2

    blk = pl.BlockSpec((tr, C), lambda i: (i, 0))
    return pl.pallas_call(body, name=name, grid=(R // tr,), in_specs=[blk] * 4, out_specs=[blk] * 3,
                          out_shape=[_sds((R, C), f32)] * 3, compiler_params=_params(1))(w, g, m, v)


def _pair_sum(name, c_arr, g4, recv, th):
    _, _, h, w = g4.shape
    th = min(th, h)

    def body(c_ref, g_ref, r_ref, o_ref, ob_ref):
        p = g_ref[...] + r_ref[...]
        o_ref[...] = p
        ob_ref[...] = p.astype(bf16)

    blk = pl.BlockSpec((None, th, w), lambda s, i, c_ref: (s, i, 0))
    spec = pltpu.PrefetchScalarGridSpec(
        num_scalar_prefetch=1, grid=(NCHIP, h // th),
        in_specs=[pl.BlockSpec((None, None, th, w), lambda s, i, c_ref: (s, c_ref[0], i, 0)), blk],
        out_specs=[blk, blk])
    return pl.pallas_call(body, name=name, grid_spec=spec,
                          out_shape=[_sds((NCHIP, h, w), f32), _sds((NCHIP, h, w), bf16)],
                          compiler_params=_params(2))(c_arr, g4, recv)


def _chip_sum(name, chip_arr, own4, recv3, th):
    _, h, w = own4.shape
    th = min(th, h)

    def body(s_ref, o_ref, r_ref, out_ref):
        out_ref[...] = ((o_ref[...] + r_ref[0].astype(f32)) + r_ref[1].astype(f32)) + r_ref[2].astype(f32)

    spec = pltpu.PrefetchScalarGridSpec(
        num_scalar_prefetch=1, grid=(h // th,),
        in_specs=[pl.BlockSpec((None, th, w), lambda i, s_ref: (s_ref[0], i, 0)),
                  pl.BlockSpec((3, th, w), lambda i, s_ref: (0, i, 0))],
        out_specs=pl.BlockSpec((th, w), lambda i, s_ref: (i, 0)))
    return pl.pallas_call(body, name=name, grid_spec=spec, out_shape=_sds((h, w), f32),
                          compiler_params=_params(1, True))(chip_arr, own4, recv3)


def _adamw_halves(name, c_arr, w, g_own, g_recv, m, v, th):
    h, wd = g_own.shape
    th = min(th, h)

    def body(c_ref, w_ref, go_ref, gr_ref, m_ref, v_ref, g_ref, d_ref, m2_ref, v2_ref):
        g = jnp.where(c_ref[0] == pl.program_id(0), go_ref[...], gr_ref[...])
        m2 = ADAM_B1 * m_ref[...] + (1.0 - ADAM_B1) * g
        v2 = ADAM_B2 * v_ref[...] + (1.0 - ADAM_B2) * (g * g)
        m_hat = m2 / (1.0 - ADAM_B1 ** ADAM_STEP)
        v_hat = v2 / (1.0 - ADAM_B2 ** ADAM_STEP)
        g_ref[...] = g
        d_ref[...] = -ADAM_LR * (m_hat / (jnp.sqrt(v_hat) + ADAM_EPS) + ADAM_WD * w_ref[...])
        m2_ref[...] = m2
        v2_ref[...] = v2

    nt = h // th
    full = pl.BlockSpec((th, wd), lambda hh, i, c_ref: (hh * nt + i, 0))
    half = pl.BlockSpec((th, wd), lambda hh, i, c_ref: (i, 0))
    spec = pltpu.PrefetchScalarGridSpec(num_scalar_prefetch=1, grid=(2, nt),
                                        in_specs=[full, half, half, full, full], out_specs=[full] * 4)
    return pl.pallas_call(body, name=name, grid_spec=spec, out_shape=[_sds((2 * h, wd), f32)] * 4,
                          compiler_params=_params(2))(c_arr, w, g_own, g_recv, m, v)


def _add2(name, a, b):
    def body(a_ref, b_ref, o_ref):
        o_ref[...] = a_ref[...] + b_ref[...]
    return pl.pallas_call(body, name=name, out_shape=_sds(a.shape, f32))(a, b)


def _sum4(name, b4, th):
    _, h, w = b4.shape
    th = min(th, h)

    def body(b_ref, o_ref):
        o_ref[...] = ((b_ref[0] + b_ref[1]) + b_ref[2]) + b_ref[3]

    return pl.pallas_call(body, name=name, grid=(h // th,),
                          in_specs=[pl.BlockSpec((NCHIP, th, w), lambda i: (0, i, 0))],
                          out_specs=pl.BlockSpec((th, w), lambda i: (i, 0)), out_shape=_sds((h, w), f32),
                          compiler_params=_params(1, True))(b4)


def _coords():
    x, y, c = lax.axis_index("x"), lax.axis_index("y"), lax.axis_index("c")
    return x, y, c, [(1 - x, y), (x, 1 - y), (1 - x, 1 - y)]


def _gather_chips(arrs):
    n = len(arrs)

    def body(*refs):
        ins, outs = refs[:n], refs[n:2 * n]
        send_sems, recv_sems, local_sems = refs[2 * n:2 * n + 3]
        stage = refs[2 * n + 3:]
        x, y, c, chips = _coords()
        s = 2 * x + y
        sib = (x, y, 1 - c)
        load = [pltpu.make_async_copy(ins[a], stage[a], local_sems.at[a]) for a in range(n)]
        local = [pltpu.make_async_copy(stage[a], outs[a].at[s], local_sems.at[n + a]) for a in range(n)]
        for cp in load:
            cp.start()

        def over_ici(k, a, slot, peer):
            return pltpu.make_async_remote_copy(src_ref=ins[a].at[c], dst_ref=outs[a].at[slot, c], send_sem=send_sems.at[k * n + a],
                                                recv_sem=recv_sems.at[k * n + a], device_id=peer, device_id_type=MESH)

        def to_sibling(k, a, slot, half):
            i = (3 + k) * n + a
            return pltpu.make_async_remote_copy(src_ref=outs[a].at[slot, half], dst_ref=outs[a].at[slot, half], send_sem=send_sems.at[i],
                                                recv_sem=recv_sems.at[i], device_id=sib, device_id_type=MESH)

        sends = [over_ici(k, a, s, (px, py, c)) for k, (px, py) in enumerate(chips) for a in range(n)]
        for cp in sends:
            cp.start()
        for a in range(n):
            load[a].wait()
            local[a].start()
        passed = []
        for k, (px, py) in enumerate(chips):
            for a in range(n):
                over_ici(k, a, 2 * px + py, (px, py, c)).wait_recv()
                cp = to_sibling(k, a, 2 * px + py, c)
                cp.start()
                passed.append(cp)
        for k, (px, py) in enumerate(chips):
            for a in range(n):
                to_sibling(k, a, 2 * px + py, 1 - c).wait_recv()
        for cp in sends + passed:
            cp.wait_send()
        for cp in local:
            cp.wait()

    return pl.pallas_call(
        body, name="gather_weights", in_specs=[ANY] * n, out_specs=[ANY] * n,
        out_shape=[_sds((NCHIP,) + a.shape, a.dtype) for a in arrs],
        scratch_shapes=[pltpu.SemaphoreType.DMA((6 * n,)), pltpu.SemaphoreType.DMA((6 * n,)), pltpu.SemaphoreType.DMA((2 * n,))]
        + [pltpu.VMEM(a.shape, a.dtype) for a in arrs],
        compiler_params=pltpu.CompilerParams(vmem_limit_bytes=VMEM_LIMIT),
    )(*arrs)


HBM = pl.BlockSpec(memory_space=pltpu.HBM)
SEM = pl.BlockSpec(memory_space=pltpu.SEMAPHORE)
EFFECT = pltpu.SideEffectType.DATAFLOW_SIDE_EFFECTING


def _split_start(name, n_copies, make_copies, ins, land_shapes, after):
    ni, nl = len(ins), len(land_shapes)

    def body(*refs):
        in_refs, land_refs = refs[:ni], refs[ni:ni + nl]
        send_sems, recv_sems = refs[ni + nl + 1], refs[ni + nl + 2]
        token = refs[-1]
        for cp in make_copies(in_refs, land_refs, send_sems, recv_sems):
            cp.start()
        token[...] = jnp.zeros_like(token)

    lands = [pltpu.with_memory_space_constraint(lax.empty(s.shape, s.dtype), pltpu.HBM) for s in land_shapes]
    res = pl.pallas_call(
        body, name=name,
        out_shape=(pltpu.SemaphoreType.DMA((n_copies,)), pltpu.SemaphoreType.DMA((n_copies,)),
                   *[pltpu.HBM(a.shape, a.dtype) for a in ins], *[pltpu.HBM(s.shape, s.dtype) for s in land_shapes],
                   _sds((8, 128), f32)),
        in_specs=[HBM] * (ni + nl) + [ANY], out_specs=(SEM, SEM, *[HBM] * (ni + nl), pl.BlockSpec(memory_space=pltpu.VMEM)),
        input_output_aliases={i: 2 + i for i in range(ni + nl)},
        compiler_params=pltpu.CompilerParams(has_side_effects=EFFECT),
    )(*[pltpu.with_memory_space_constraint(a, pltpu.HBM) for a in ins], *lands, after)
    return res[0], res[1], list(res[2:2 + ni]), list(res[2 + ni:2 + ni + nl]), res[-1]


def _split_wait(name, make_copies, send_sems, recv_sems, ins, lands, after):
    ni, nl = len(ins), len(lands)

    def body(*refs):
        in_refs, land_refs = refs[:ni], refs[ni:ni + nl]
        s_sems, r_sems = refs[ni + nl], refs[ni + nl + 1]
        for cp in make_copies(in_refs, land_refs, s_sems, r_sems):
            cp.wait_send()
            cp.wait_recv()

    res = pl.pallas_call(
        body, name=name, out_shape=tuple(pltpu.HBM(a.shape, a.dtype) for a in ins + lands),
        in_specs=[HBM] * (ni + nl) + [SEM, SEM, ANY], out_specs=tuple([HBM] * (ni + nl)),
        input_output_aliases={i: i for i in range(ni + nl)},
        compiler_params=pltpu.CompilerParams(has_side_effects=EFFECT),
    )(*ins, *lands, send_sems, recv_sems, after)
    return list(res[:ni]), list(res[ni:])


def _gather_copies(n):
    def make(in_refs, land_refs, send_sems, recv_sems):
        x, y, c, chips = _coords()
        s = 2 * x + y
        return [pltpu.make_async_remote_copy(src_ref=in_refs[a], dst_ref=land_refs[a].at[s], send_sem=send_sems.at[k * n + a],
                                             recv_sem=recv_sems.at[k * n + a], device_id=(px, py, c), device_id_type=MESH)
                for k, (px, py) in enumerate(chips) for a in range(n)]
    return make


def _sibling_half_copies(n):
    def make(in_refs, land_refs, send_sems, recv_sems):
        x, y, c, _ = _coords()
        return [pltpu.make_async_remote_copy(src_ref=in_refs[a].at[:, 1 - c], dst_ref=land_refs[a], send_sem=send_sems.at[a],
                                             recv_sem=recv_sems.at[a], device_id=(x, y, 1 - c), device_id_type=MESH)
                for a in range(n)]
    return make


def _chip_part_copies(n):
    def make(in_refs, land_refs, send_sems, recv_sems):
        x, y, c, chips = _coords()
        return [pltpu.make_async_remote_copy(src_ref=in_refs[a].at[2 * px + py], dst_ref=land_refs[a].at[k],
                                             send_sem=send_sems.at[k * n + a], recv_sem=recv_sems.at[k * n + a],
                                             device_id=(px, py, c), device_id_type=MESH)
                for k, (px, py) in enumerate(chips) for a in range(n)]
    return make


def _sibling_whole_copies(n):
    def make(in_refs, land_refs, send_sems, recv_sems):
        x, y, c, _ = _coords()
        return [pltpu.make_async_remote_copy(src_ref=in_refs[a], dst_ref=land_refs[a], send_sem=send_sems.at[a],
                                             recv_sem=recv_sems.at[a], device_id=(x, y, 1 - c), device_id_type=MESH)
                for a in range(n)]
    return make


def _place_own(chip_arr, owns, lands, steps):
    n = len(owns)

    def body(s_ref, *refs):
        for a in range(n):
            refs[2 * n + a][...] = refs[a][...]

    tiles = [o.shape[0] // steps for o in owns]
    spec = pltpu.PrefetchScalarGridSpec(
        num_scalar_prefetch=1, grid=(steps,),
        in_specs=[pl.BlockSpec((t, o.shape[1]), lambda i, s_ref: (i, 0)) for t, o in zip(tiles, owns)] + [ANY] * n,
        out_specs=[pl.BlockSpec((None, t, o.shape[1]), lambda i, s_ref: (s_ref[0], i, 0)) for t, o in zip(tiles, owns)])
    return pl.pallas_call(body, name="place_own", grid_spec=spec, out_shape=[_sds(l.shape, l.dtype) for l in lands],
                          input_output_aliases={1 + n + a: a for a in range(n)},
                          compiler_params=_params(1))(chip_arr, *owns, *lands)


def _sibling_halves(g4s, small):
    n = len(g4s)

    def body(*refs):
        ins, small_ref = refs[:n], refs[n]
        outs, small_out = refs[n + 1:2 * n + 1], refs[2 * n + 1]
        send_sems, recv_sems = refs[2 * n + 2:]
        x, y, c, _ = _coords()
        sib = (x, y, 1 - c)

        def remote(a, half):
            src = small_ref if a == n else ins[a].at[:, half]
            dst = small_out if a == n else outs[a]
            return pltpu.make_async_remote_copy(src_ref=src, dst_ref=dst, send_sem=send_sems.at[a], recv_sem=recv_sems.at[a],
                                                device_id=sib, device_id_type=MESH)

        sends = [remote(a, 1 - c) for a in range(n + 1)]
        for cp in sends:
            cp.start()
        for a in range(n + 1):
            remote(a, c).wait_recv()
        for cp in sends:
            cp.wait_send()

    return pl.pallas_call(
        body, name="reduce_sibling", in_specs=[ANY] * (n + 1), out_specs=[ANY] * (n + 1),
        out_shape=[_sds((g.shape[0],) + g.shape[2:], f32) for g in g4s] + [_sds(small.shape, f32)],
        scratch_shapes=[pltpu.SemaphoreType.DMA((n + 1,)), pltpu.SemaphoreType.DMA((n + 1,))],
    )(*g4s, small)


def _exchange_chips(parts, small2):
    n = len(parts)

    def body(*refs):
        ins, small_ref = refs[:n], refs[n]
        outs, small_out = refs[n + 1:2 * n + 1], refs[2 * n + 1]
        send_sems, recv_sems, local_sem = refs[2 * n + 2:]
        x, y, c, chips = _coords()
        s = 2 * x + y
        local = pltpu.make_async_copy(small_ref.at[c], small_out.at[s], local_sem)
        local.start()

        def remote(k, a, dest_chip, small_slot, peer):
            if a == n:
                src, dst = small_ref.at[c], small_out.at[small_slot]
            else:
                src, dst = ins[a].at[dest_chip], outs[a].at[k]
            i = k * (n + 1) + a
            return pltpu.make_async_remote_copy(src_ref=src, dst_ref=dst, send_sem=send_sems.at[i], recv_sem=recv_sems.at[i],
                                                device_id=peer, device_id_type=MESH)

        sends = [remote(k, a, 2 * px + py, s, (px, py, c)) for k, (px, py) in enumerate(chips) for a in range(n + 1)]
        for cp in sends:
            cp.start()
        for k, (px, py) in enumerate(chips):
            for a in range(n + 1):
                remote(k, a, s, 2 * px + py, (px, py, c)).wait_recv()
        for cp in sends:
            cp.wait_send()
        local.wait()

    m = 3 * (n + 1)
    return pl.pallas_call(
        body, name="reduce_chips", in_specs=[ANY] * (n + 1), out_specs=[ANY] * (n + 1),
        out_shape=[_sds((3,) + p.shape[1:], p.dtype) for p in parts] + [_sds((NCHIP,) + small2.shape[1:], f32)],
        scratch_shapes=[pltpu.SemaphoreType.DMA((m,)), pltpu.SemaphoreType.DMA((m,)), pltpu.SemaphoreType.DMA],
    )(*parts, small2)


def _share_sibling(halves):
    n = len(halves)

    def body(*refs):
        ins, outs = refs[:n], refs[n:2 * n]
        send_sems, recv_sems = refs[2 * n:]
        x, y, c, _ = _coords()
        sib = (x, y, 1 - c)
        sends = [pltpu.make_async_remote_copy(src_ref=ins[a], dst_ref=outs[a], send_sem=send_sems.at[a], recv_sem=recv_sems.at[a],
                                              device_id=sib, device_id_type=MESH) for a in range(n)]
        for cp in sends:
            cp.start()
        for cp in sends:
            cp.wait()

    return pl.pallas_call(
        body, name="reduce_share", in_specs=[ANY] * n, out_specs=[ANY] * n,
        out_shape=[_sds(h.shape, f32) for h in halves],
        scratch_shapes=[pltpu.SemaphoreType.DMA((n,)), pltpu.SemaphoreType.DMA((n,))],
    )(*halves)


def _block_diag_pairs(w):
    w = w.reshape(NCH, 2, HD, HD)
    z = jnp.zeros((NCH, HD, HD), w.dtype)
    return jnp.concatenate([jnp.concatenate([w[:, 0], z], axis=2), jnp.concatenate([z, w[:, 1]], axis=2)], axis=1)


def _diag_blocks(m):
    return jnp.stack([m[:, :HD, :HD], m[:, HD:, HD:]], axis=1).reshape(NH, HD, HD)


def _pack(vs, rows):
    flat = jnp.concatenate([v.reshape(-1) for v in vs])
    return jnp.pad(flat, (0, rows * 128 - flat.shape[0])).reshape(rows, 128)


def _unpack(packed, shapes):
    flat = packed.reshape(-1)
    out, off = [], 0
    for shp in shapes:
        size = math.prod(shp)
        out.append(flat[off:off + size].reshape(shp))
        off += size
    return out


def _rows_for(sizes, multiple):
    rows = -(-sum(sizes) // 128)
    return -(-rows // multiple) * multiple


def kernel(x, norm_mix_g, w_in, b_gate, conv_w, conv_b, lru_lambda, lru_wa, lru_ba, lru_wx, lru_bx, attn_sink, w_out, norm_ffn_g, w_ffn_in, w_ffn_out, norm_final_g, loss_target, m_norm_mix_g, m_w_in, m_b_gate, m_conv_w, m_conv_b, m_lru_lambda, m_lru_wa, m_lru_ba, m_lru_wx, m_lru_bx, m_attn_sink, m_w_out, m_norm_ffn_g, m_w_ffn_in, m_w_ffn_out, m_norm_final_g, v_norm_mix_g, v_w_in, v_b_gate, v_conv_w, v_conv_b, v_lru_lambda, v_lru_wa, v_lru_ba, v_lru_wx, v_lru_bx, v_attn_sink, v_w_out, v_norm_ffn_g, v_w_ffn_in, v_w_ffn_out, v_norm_final_g):
    S = x.shape[1]
    xs = x[0]
    tgt = loss_target[0]
    cx, cy, cc = lax.axis_index("x"), lax.axis_index("y"), lax.axis_index("c")
    chip = 2 * cx + cy
    SW = D // NCHIP

    small_shard = _pack([conv_w[0], lru_lambda[0], lru_ba[0], lru_bx[0]], 32)
    halves_of = lambda a: a.reshape(2, a.shape[0] // 2, a.shape[1])
    w_in_g, small_g = _gather_chips([halves_of(w_in[0].astype(bf16)), halves_of(small_shard)])
    w_in_g = w_in_g.reshape(NCHIP, D, SHW)
    small_g = small_g.reshape(NCHIP, 32, 128)
    late = [w_ffn_in[0].astype(bf16), w_out[0].astype(bf16), w_ffn_out[0].astype(bf16)]
    late_send, late_recv, late_src, late_land, late_token = _split_start(
        "gather_late_start", 9, _gather_copies(3), late, [_sds((NCHIP,) + a.shape, bf16) for a in late], small_g)
    small_parts = [_unpack(small_g[s], [(4, SW), (2, SW), (2, SW), (2, SW)]) for s in range(NCHIP)]
    conv_w_f, lam_f, ba_f, bx_f = [jnp.concatenate([small_parts[s][p] for s in range(NCHIP)], axis=1) for p in range(4)]
    wbd = jnp.concatenate([_block_diag_pairs(lru_wa[0, 0]), _block_diag_pairs(lru_wx[0, 0]),
                           _block_diag_pairs(lru_wa[0, 1]), _block_diag_pairs(lru_wx[0, 1])], axis=2).astype(bf16)
    conv_b_f = conv_b
    sink = attn_sink

    xn, proj = _rms_matmul("rms_proj", xs, norm_mix_g + late_token[0:1, 0:1], w_in_g, 512)
    y_a, lru_state = _lru_fwd(proj, conv_w_f, conv_b_f, lam_f, ba_f, bx_f, wbd)
    y_b = _attn_fwd(proj, sink)
    late_src, late_land = _split_wait("gather_late_wait", _gather_copies(3), late_send, late_recv, late_src, late_land, y_b)
    chip_arr = chip.reshape(1).astype(jnp.int32)
    w_ffn_in_g, w_out_g, w_ffn_out_g = _place_own(chip_arr, late_src, late_land, 4)
    w_out_f = w_out_g.reshape(D, D)
    w_ffn_out_f = w_ffn_out_g.reshape(FF, D)
    merged, x1 = _merge_out_proj(proj, b_gate, y_a, y_b, w_out_f, xs, 512)
    xn2, gu, act = _rms_matmul_swiglu("rms_ffn_in", x1, norm_ffn_g, w_ffn_in_g, 512)
    dx2, loss_row, dg3 = _ffn_out_loss_bwd(act, w_ffn_out_f, x1, norm_final_g.reshape(1, D), tgt, 512)

    tk = min(2048, S)
    gw_ffn_out = _mm_tn("dw_ffn_out", act, pl.BlockSpec((tk, SHW), lambda i, k: (k, i)),
                        dx2, pl.BlockSpec((tk, D), lambda i, k: (k, 0)),
                        _sds((FF, D), f32), pl.BlockSpec((SHW, D), lambda i, k: (i, 0)), (2, S // tk), (SHW, D))
    dgu = _swiglu_bwd(dx2, w_ffn_out_f, gu, 256)
    gw_ffn_in = _mm_tn("dw_ffn_in", xn2, pl.BlockSpec((tk, D), lambda g, k: (k, 0)),
                       dgu, pl.BlockSpec((None, tk, SHW), lambda g, k: (g // 2, k, g % 2)),
                       _sds((NCHIP, D, SHW), f32), pl.BlockSpec((None, D, SHW), lambda g, k: (g, 0, 0)),
                       (NCHIP, S // tk), (D, SHW))
    c_arr = cc.reshape(1).astype(jnp.int32)
    early_names, early_tiles = ["w_ffn_in", "w_ffn_out"], [256, 352]
    early = [gw_ffn_in.reshape(NCHIP, 2, D // 2, SHW), gw_ffn_out.reshape(NCHIP, 2, FF // NCHIP // 2, D)]
    ea_send, ea_recv, ea_src, ea_land, ea_token = _split_start(
        "reduce_early_sibling_start", 2, _sibling_half_copies(2), early,
        [_sds((NCHIP,) + g.shape[2:], f32) for g in early], dgu)
    dx1, dg2 = _mm_nt_rms_bwd("dxn2_rms_bwd", dgu, w_ffn_in_g, x1, norm_ffn_g + ea_token[0:1, 0:1], dx2, 512)

    gw_out = _mm_tn("dw_out", merged, pl.BlockSpec((tk, D), lambda i, k: (k, 0)),
                    dx1, pl.BlockSpec((tk, D), lambda i, k: (k, 0)),
                    _sds((D, D), f32), pl.BlockSpec((D, D), lambda i, k: (0, 0)), (1, S // tk), (D, D))
    dproj, dy, db_gate = _merge_bwd(proj, b_gate, y_a, y_b, dx1, w_out_f, 512)
    ea_src, ea_land = _split_wait("reduce_early_sibling_wait", _sibling_half_copies(2), ea_send, ea_recv, ea_src, ea_land, dy)
    early_pairs = [_pair_sum("pair_sum_" + nm, c_arr, g4, r, th)
                   for nm, g4, r, th in zip(early_names, ea_src, ea_land, early_tiles)]
    eb_send, eb_recv, eb_src, eb_land, eb_token = _split_start(
        "reduce_early_chips_start", 6, _chip_part_copies(2), [p[1] for p in early_pairs],
        [_sds((3,) + p[1].shape[1:], bf16) for p in early_pairs], early_pairs[0][0])
    dproj, dsink = _attn_bwd(proj, sink + eb_token[0:1, 0:1], y_b, dy, dproj)
    _, eb_land = _split_wait("reduce_early_chips_wait", _chip_part_copies(2), eb_send, eb_recv, eb_src, eb_land, dsink)
    early_halves = [_chip_sum("chip_sum_" + nm, chip_arr, p[0], r3, th)
                    for nm, p, r3, th in zip(early_names, early_pairs, eb_land, early_tiles)]
    ec_send, ec_recv, ec_src, ec_land, ec_token = _split_start(
        "reduce_early_share_start", 2, _sibling_whole_copies(2), early_halves, [_sds(h.shape, f32) for h in early_halves], dsink)
    dproj, dcw, dcb, dlam, dba, dbx, dwbd = _lru_bwd(proj, dy, lru_state, dproj, conv_w_f, conv_b_f + ec_token[0:1, 0:1], lam_f,
                                                     ba_f, bx_f, wbd)
    early_halves, early_other = _split_wait("reduce_early_share_wait", _sibling_whole_copies(2), ec_send, ec_recv, ec_src, ec_land, dcb)
    gw_in = _mm_tn("dw_in", xn, pl.BlockSpec((tk, D), lambda g, k: (k, 0)),
                   dproj, pl.BlockSpec((tk, SHW), lambda g, k: (k, g)),
                   _sds((NCHIP, D, SHW), f32), pl.BlockSpec((None, D, SHW), lambda g, k: (g, 0, 0)),
                   (NCHIP, S // tk), (D, SHW))
    wa_send, wa_recv, wa_src, wa_land, wa_token = _split_start(
        "reduce_w_in_sibling_start", 1, _sibling_half_copies(1), [gw_in.reshape(NCHIP, 2, D // 2, SHW)],
        [_sds((NCHIP, D // 2, SHW), f32)], dproj)
    grad_x, dg1 = _mm_nt_rms_bwd("dxn_rms_bwd", dproj, w_in_g, xs, norm_mix_g + wa_token[0:1, 0:1], dx1, 512)
    wa_src, wa_land = _split_wait("reduce_w_in_sibling_wait", _sibling_half_copies(1), wa_send, wa_recv, wa_src, wa_land, dg1)
    w_in_pair = _pair_sum("pair_sum_w_in", c_arr, wa_src[0], wa_land[0], 256)
    wb_send, wb_recv, wb_src, wb_land, wb_token = _split_start(
        "reduce_w_in_chips_start", 3, _chip_part_copies(1), [w_in_pair[1]], [_sds((3, D // 2, SHW), bf16)], w_in_pair[0])

    d_wa = jnp.stack([_diag_blocks(dwbd[:, :, 0:CW]), _diag_blocks(dwbd[:, :, 2 * CW:3 * CW])])
    d_wx = jnp.stack([_diag_blocks(dwbd[:, :, CW:2 * CW]), _diag_blocks(dwbd[:, :, 3 * CW:4 * CW])])
    small_full = [dg1, db_gate, dcw, dcb, dlam, d_wa, dba, d_wx, dbx, dsink[:, 0], dg2, dg3,
                  loss_row[0, 0:1]]
    full_shapes = [(1, D), (1, 2 * D), (4, D), (1, D), (2, D), (2, NH, HD, HD), (2, D), (2, NH, HD, HD), (2, D), (NH,),
                   (1, D), (1, D), (1,)]
    rows_full = _rows_for([math.prod(s) for s in full_shapes], 16)
    small_vec = _pack(small_full, rows_full)

    late_names, late_tiles = ["w_in", "w_out"], [256, 128]
    big = [gw_out.reshape(NCHIP, 2, D // NCHIP // 2, D)]
    *recv_a, small_sib = _sibling_halves(big, small_vec + wb_token[0:1, 0:1])
    w_out_pair = _pair_sum("pair_sum_w_out", c_arr, big[0], recv_a[0], 128)
    small_chip = _add2("pair_sum_small", small_vec, small_sib).reshape(2, rows_full // 2, 128)
    *recv_b, small_all = _exchange_chips([w_out_pair[1]], small_chip)
    w_out_half = _chip_sum("chip_sum_w_out", chip_arr, w_out_pair[0], recv_b[0], 128)
    _, wb_land = _split_wait("reduce_w_in_chips_wait", _chip_part_copies(1), wb_send, wb_recv, wb_src, wb_land, small_all)
    w_in_half = _chip_sum("chip_sum_w_in", chip_arr, w_in_pair[0], wb_land[0], 256)
    halves = [w_in_half, w_out_half, _sum4("chip_sum_small", small_all, rows_full // 2)]
    *recv_c, small_other = _share_sibling(halves)
    small_lo = jnp.where(cc == 0, halves[2], small_other)
    small_hi = jnp.where(cc == 0, small_other, halves[2])
    g_full = _unpack(jnp.concatenate([small_lo, small_hi], axis=0), full_shapes)

    out_big = {}
    for nm, w, g_own, g_recv, m, v, th in zip(late_names + early_names, [w_in, w_out, w_ffn_in, w_ffn_out],
                                              halves[:2] + early_halves, recv_c + early_other,
                                              [m_w_in, m_w_out, m_w_ffn_in, m_w_ffn_out],
                                              [v_w_in, v_w_out, v_w_ffn_in, v_w_ffn_out], late_tiles + early_tiles):
        g_, d_, m_, v_ = _adamw_halves("adamw_" + nm, c_arr, w[0], g_own, g_recv, m[0], v[0], th)
        out_big[nm] = (g_[None], d_[None], m_[None], v_[None])

    small_names = ["norm_mix_g", "b_gate", "conv_w", "conv_b", "lru_lambda", "lru_wa", "lru_ba", "lru_wx", "lru_bx", "attn_sink",
                   "norm_ffn_g", "norm_final_g"]
    sharded = {"conv_w", "lru_lambda", "lru_ba", "lru_bx"}
    small_w = [norm_mix_g, b_gate, conv_w, conv_b, lru_lambda, lru_wa, lru_ba, lru_wx, lru_bx, attn_sink, norm_ffn_g, norm_final_g]
    small_m = [m_norm_mix_g, m_b_gate, m_conv_w, m_conv_b, m_lru_lambda, m_lru_wa, m_lru_ba, m_lru_wx, m_lru_bx, m_attn_sink,
               m_norm_ffn_g, m_norm_final_g]
    small_v = [v_norm_mix_g, v_b_gate, v_conv_w, v_conv_b, v_lru_lambda, v_lru_wa, v_lru_ba, v_lru_wx, v_lru_bx, v_attn_sink,
               v_norm_ffn_g, v_norm_final_g]
    g_local = []
    for nm, g, w in zip(small_names, g_full, small_w):
        if nm in sharded:
            g = lax.dynamic_slice_in_dim(g, chip * SW, SW, axis=1)
        g_local.append(g.reshape(w.shape))
    local_shapes = [w.shape for w in small_w]
    rows_local = _rows_for([math.prod(s) for s in local_shapes], 8)
    d_s, m_s, v_s = _adamw("adamw_small", _pack(small_w, rows_local), _pack(g_local, rows_local),
                           _pack(small_m, rows_local), _pack(small_v, rows_local), rows_local)
    d_l, m_l, v_l = _unpack(d_s, local_shapes), _unpack(m_s, local_shapes), _unpack(v_s, local_shapes)
    res = {nm: (g_local[i], d_l[i], m_l[i], v_l[i]) for i, nm in enumerate(small_names)}
    res.update(out_big)

    order = ["norm_mix_g", "w_in", "b_gate", "conv_w", "conv_b", "lru_lambda", "lru_wa", "lru_ba", "lru_wx", "lru_bx", "attn_sink",
             "w_out", "norm_ffn_g", "w_ffn_in", "w_ffn_out", "norm_final_g"]
    outs = [g_full[-1][0], grad_x[None]]
    for k in range(4):
        outs += [res[nm][k] for nm in order]
    return tuple(outs)
```

```python
import math

import jax
import jax.numpy as jnp
from jax import lax
from jax.experimental import pallas as pl
from jax.experimental.pallas import tpu as pltpu

f32 = jnp.float32
bf16 = jnp.bfloat16

D = 1024
NH = 16
HD = 64
FF = 2816
INW = 5632
NCHIP = 4
SHW = INW // NCHIP
CW = 128
NCH = D // CW
BLK = 128
EPS = 1e-6
NEG_INF = -1e30
RGLRU_C = 8.0
ADAM_LR, ADAM_B1, ADAM_B2, ADAM_EPS, ADAM_WD, ADAM_STEP = 0.001, 0.9, 0.999, 1e-08, 0.01, 10
VMEM_LIMIT = 58 * 1024 * 1024
MESH = pl.DeviceIdType.MESH
ANY = pl.BlockSpec(memory_space=pl.ANY)

COL_Q, COL_K, COL_V, COL_Z0 = 8, 12, 13, 14
MERGE_W = 512
MERGE_Z0 = (COL_Z0 * 256) // MERGE_W


def _params(n_axes, vmem=False):
    return pltpu.CompilerParams(dimension_semantics=("arbitrary",) * n_axes,
                                vmem_limit_bytes=VMEM_LIMIT if vmem else None)


def _sds(shape, dtype):
    return jax.ShapeDtypeStruct(tuple(shape), dtype)


_DIMS = {"nn": (((1,), (0,)), ((), ())), "nt": (((1,), (1,)), ((), ())), "tn": (((0,), (0,)), ((), ()))}


def _mm(name, mode, a, a_spec, b, b_spec, out_shape, out_spec, grid, nk, acc_shape):
    def body(*refs):
        a_ref, b_ref, o_ref = refs[0], refs[1], refs[2]
        part = lax.dot_general(a_ref[...].astype(bf16), b_ref[...].astype(bf16), _DIMS[mode],
                               preferred_element_type=f32)
        if nk == 1:
            o_ref[...] = part.astype(o_ref.dtype)
            return
        acc_ref = refs[3]
        k = pl.program_id(len(grid) - 1)

        @pl.when(k == 0)
        def _():
            acc_ref[...] = part

        @pl.when(k > 0)
        def _():
            acc_ref[...] += part

        @pl.when(k == nk - 1)
        def _():
            o_ref[...] = acc_ref[...].astype(o_ref.dtype)

    scratch = [pltpu.VMEM(acc_shape, f32)] if nk > 1 else []
    return pl.pallas_call(body, name=name, grid=grid, in_specs=[a_spec, b_spec], out_specs=out_spec, out_shape=out_shape,
                          scratch_shapes=scratch, compiler_params=_params(len(grid), True))(a, b)


def _rms_matmul(name, x, g, w3, tm):
    S, K = x.shape
    G, _, Nw = w3.shape
    tm = min(tm, S)

    def body(x_ref, g_ref, w_ref, xn_ref, o_ref):
        xf = x_ref[...]
        r = lax.rsqrt(jnp.mean(xf * xf, axis=-1, keepdims=True) + EPS)
        xn = ((xf * r) * g_ref[...]).astype(bf16)
        xn_ref[...] = xn
        for j in range(G):
            o_ref[:, j * Nw:(j + 1) * Nw] = jnp.dot(xn, w_ref[j], preferred_element_type=f32).astype(bf16)

    return pl.pallas_call(
        body, name=name, grid=(S // tm,),
        in_specs=[pl.BlockSpec((tm, K), lambda i: (i, 0)), pl.BlockSpec((1, K), lambda i: (0, 0)),
                  pl.BlockSpec((G, K, Nw), lambda i: (0, 0, 0))],
        out_specs=[pl.BlockSpec((tm, K), lambda i: (i, 0)), pl.BlockSpec((tm, G * Nw), lambda i: (i, 0))],
        out_shape=[_sds((S, K), bf16), _sds((S, G * Nw), bf16)],
        compiler_params=_params(1, True))(x, g, w3)


def _rms_matmul_swiglu(name, x, g, w3, tm):
    S, K = x.shape
    G, _, Nw = w3.shape
    tm = min(tm, S)
    half = G // 2

    def body(x_ref, g_ref, w_ref, xn_ref, gu_ref, act_ref):
        xf = x_ref[...]
        r = lax.rsqrt(jnp.mean(xf * xf, axis=-1, keepdims=True) + EPS)
        xn = ((xf * r) * g_ref[...]).astype(bf16)
        xn_ref[...] = xn
        for j in range(half):
            cols = slice(j * Nw, (j + 1) * Nw)
            gate = jnp.dot(xn, w_ref[j], preferred_element_type=f32)
            up = jnp.dot(xn, w_ref[half + j], preferred_element_type=f32)
            gu_ref[0, :, cols] = gate.astype(bf16)
            gu_ref[1, :, cols] = up.astype(bf16)
            act_ref[:, cols] = ((gate * _sigmoid(gate)) * up).astype(bf16)

    return pl.pallas_call(
        body, name=name, grid=(S // tm,),
        in_specs=[pl.BlockSpec((tm, K), lambda i: (i, 0)), pl.BlockSpec((1, K), lambda i: (0, 0)),
                  pl.BlockSpec((G, K, Nw), lambda i: (0, 0, 0))],
        out_specs=[pl.BlockSpec((tm, K), lambda i: (i, 0)), pl.BlockSpec((2, tm, half * Nw), lambda i: (0, i, 0)),
                   pl.BlockSpec((tm, half * Nw), lambda i: (i, 0))],
        out_shape=[_sds((S, K), bf16), _sds((2, S, half * Nw), bf16), _sds((S, half * Nw), bf16)],
        compiler_params=_params(1, True))(x, g, w3)


def _mm_tn(name, a, a_spec, b, b_spec, out_shape, out_spec, grid, acc_shape):
    return _mm(name, "tn", a, a_spec, b, b_spec, out_shape, out_spec, grid, grid[-1], acc_shape)


def _sigmoid(x):
    return 0.5 * jnp.tanh(0.5 * x) + 0.5


_GELU_C = math.sqrt(2.0 / math.pi)


def _gelu_and_grad(x):
    v = _GELU_C * (x + 0.044715 * (x * x * x))
    t = jnp.tanh(v)
    gl = 0.5 * x * (1.0 + t)
    dgl = 0.5 * (1.0 + t) + 0.5 * x * (1.0 - t * t) * (_GELU_C * (1.0 + 3.0 * 0.044715 * (x * x)))
    return gl, dgl


def _one_minus_exp2x(x, ex):
    y = 2.0 * x
    series = y * (1.0 + y * (0.5 + y * (1.0 / 6.0 + y * (1.0 / 24.0))))
    return jnp.where(y > -1.0 / 64.0, -series, 1.0 - ex * ex)


def _z_specs(tm):
    return [pl.BlockSpec((tm, MERGE_W), lambda i, p=p: (i, MERGE_Z0 + p)) for p in range(2 * D // MERGE_W)]


def _merge_out_proj(proj, b_gate, y_a, y_b, w, res, tm):
    S = proj.shape[0]
    tm = min(tm, S)
    per = D // MERGE_W
    nz = 2 * per

    def body(*refs):
        z = refs[:nz]
        b_ref, ya_ref, yb_ref, w_ref, r_ref, m_ref, x_ref = refs[nz:]
        for p in range(per):
            cols = slice(p * MERGE_W, (p + 1) * MERGE_W)
            g0 = _sigmoid(z[p][...].astype(f32) + b_ref[:, p * MERGE_W:(p + 1) * MERGE_W])
            g1 = _sigmoid(z[per + p][...].astype(f32) + b_ref[:, D + p * MERGE_W:D + (p + 1) * MERGE_W])
            m_ref[:, cols] = (g0 * ya_ref[:, cols].astype(f32) + g1 * yb_ref[:, cols].astype(f32)).astype(bf16)
        x_ref[...] = r_ref[...] + jnp.dot(m_ref[...], w_ref[...], preferred_element_type=f32)

    row = pl.BlockSpec((tm, D), lambda i: (i, 0))
    return pl.pallas_call(
        body, name="merge_out_proj", grid=(S // tm,),
        in_specs=_z_specs(tm) + [pl.BlockSpec((1, 2 * D), lambda i: (0, 0)), row, row, pl.BlockSpec((D, D), lambda i: (0, 0)), row],
        out_specs=[row, row], out_shape=[_sds((S, D), bf16), _sds((S, D), f32)],
        compiler_params=_params(1, True))(*([proj] * nz), b_gate, y_a, y_b, w, res)


def _merge_bwd(proj, b_gate, y_a, y_b, dx, w, tm):
    S = proj.shape[0]
    tm = min(tm, S)
    per = D // MERGE_W
    nz = 2 * per
    nsteps = S // tm
    z_col = MERGE_Z0 * MERGE_W

    def body(*refs):
        z = refs[:nz]
        b_ref, ya_ref, yb_ref, dx_ref, w_ref, dproj_ref, dy_ref, db_ref, dz_buf, sems = refs[nz:]
        i = pl.program_id(0)
        slot = i % 2

        def dz_copy(step):
            rows = pl.ds(pl.multiple_of(step * tm, tm), tm)
            return pltpu.make_async_copy(dz_buf.at[step % 2], dproj_ref.at[rows, pl.ds(z_col, 2 * D)], sems.at[step % 2])

        @pl.when(i >= 2)
        def _():
            dz_copy(i - 2).wait()

        @pl.when(i == 0)
        def _():
            db_ref[...] = jnp.zeros_like(db_ref)

        dm = lax.dot_general(dx_ref[...].astype(bf16), w_ref[...], _DIMS["nt"], preferred_element_type=f32)
        for p in range(nz):
            branch, cols = p // per, slice((p % per) * MERGE_W, (p % per + 1) * MERGE_W)
            zc = slice(p * MERGE_W, (p + 1) * MERGE_W)
            g = _sigmoid(z[p][...].astype(f32) + b_ref[:, zc])
            d = dm[:, cols]
            y = (ya_ref if branch == 0 else yb_ref)[:, cols].astype(f32)
            dz = (d * y) * (g * (1.0 - g))
            dz_buf[slot, :, zc] = dz.astype(bf16)
            dy_ref[branch, :, cols] = (d * g).astype(bf16)
            db_ref[:, zc] += jnp.sum(dz, axis=0, keepdims=True)
        dz_copy(i).start()

        @pl.when(i == nsteps - 1)
        def _():
            if nsteps >= 2:
                dz_copy(i - 1).wait()
            dz_copy(i).wait()

    row = pl.BlockSpec((tm, D), lambda i: (i, 0))
    return pl.pallas_call(
        body, name="merge_bwd", grid=(nsteps,),
        in_specs=_z_specs(tm) + [pl.BlockSpec((1, 2 * D), lambda i: (0, 0)), row, row, row, pl.BlockSpec((D, D), lambda i: (0, 0))],
        out_specs=[ANY, pl.BlockSpec((2, tm, D), lambda i: (0, i, 0)), pl.BlockSpec((1, 2 * D), lambda i: (0, 0))],
        out_shape=[_sds((S, INW), bf16), _sds((2, S, D), bf16), _sds((1, 2 * D), f32)],
        scratch_shapes=[pltpu.VMEM((2, tm, 2 * D), bf16), pltpu.SemaphoreType.DMA((2,))],
        compiler_params=_params(1, True))(*([proj] * nz), b_gate, y_a, y_b, dx, w)


def _swiglu_bwd(dx, w, gu, tm):
    S, K = dx.shape
    tm = min(tm, S)

    def body(dx_ref, w_ref, gu_ref, o_ref):
        d = lax.dot_general(dx_ref[...].astype(bf16), w_ref[...], _DIMS["nt"], preferred_element_type=f32)
        g = gu_ref[0].astype(f32)
        u = gu_ref[1].astype(f32)
        s = _sigmoid(g)
        o_ref[0] = ((d * u) * (s * (1.0 + g * (1.0 - s)))).astype(bf16)
        o_ref[1] = (d * (g * s)).astype(bf16)

    stacked = pl.BlockSpec((2, tm, FF), lambda i: (0, i, 0))
    return pl.pallas_call(body, name="swiglu_bwd", grid=(S // tm,),
                          in_specs=[pl.BlockSpec((tm, K), lambda i: (i, 0)), pl.BlockSpec((FF, K), lambda i: (0, 0)), stacked],
                          out_specs=stacked, out_shape=_sds((2, S, FF), bf16),
                          compiler_params=_params(1, True))(dx, w, gu)


def _ffn_out_loss_bwd(act, w, x1, g3, tgt, tm):
    S, K = act.shape
    tm = min(tm, S)

    def body(a_ref, w_ref, r_ref, g_ref, t_ref, dx_ref, loss_ref, dg_ref):
        @pl.when(pl.program_id(0) == 0)
        def _():
            loss_ref[...] = jnp.zeros_like(loss_ref)
            dg_ref[...] = jnp.zeros_like(dg_ref)

        x = r_ref[...] + jnp.dot(a_ref[...], w_ref[...], preferred_element_type=f32)
        g = g_ref[...]
        r = lax.rsqrt(jnp.mean(x * x, axis=-1, keepdims=True) + EPS)
        xh = x * r
        err = xh * g - t_ref[...]
        row = jnp.mean(err * err, axis=-1, keepdims=True)
        loss_ref[...] += 0.5 * jnp.sum(row, axis=0, keepdims=True)
        dy = err * (1.0 / D)
        dg_ref[...] += jnp.sum(dy * xh, axis=0, keepdims=True)
        dxh = dy * g
        dx_ref[...] = r * (dxh - xh * jnp.mean(dxh * xh, axis=-1, keepdims=True))

    row_blk = pl.BlockSpec((tm, D), lambda i: (i, 0))
    vec = pl.BlockSpec((1, D), lambda i: (0, 0))
    return pl.pallas_call(body, name="ffn_out_loss_bwd", grid=(S // tm,),
                          in_specs=[pl.BlockSpec((tm, K), lambda i: (i, 0)), pl.BlockSpec((K, D), lambda i: (0, 0)),
                                    row_blk, vec, row_blk],
                          out_specs=[row_blk, pl.BlockSpec((1, 128), lambda i: (0, 0)), vec],
                          out_shape=[_sds((S, D), f32), _sds((1, 128), f32), _sds((1, D), f32)],
                          compiler_params=_params(1, True))(act, w, x1, g3, tgt)


def _mm_nt_rms_bwd(name, a, w3, x, g, dres, tm):
    S = x.shape[0]
    G, Dout, Kw = w3.shape
    tm = min(tm, S)
    planes = a.shape[0] if a.ndim == 3 else 1
    per = G // planes

    def body(a_ref, w_ref, x_ref, g_ref, r_ref, dx_ref, dg_ref):
        @pl.when(pl.program_id(0) == 0)
        def _():
            dg_ref[...] = jnp.zeros_like(dg_ref)

        d = None
        for k in range(G):
            cols = slice((k % per) * Kw, (k % per + 1) * Kw)
            a_k = a_ref[k // per, :, cols] if a.ndim == 3 else a_ref[:, cols]
            part = lax.dot_general(a_k, w_ref[k], _DIMS["nt"], preferred_element_type=f32)
            d = part if d is None else d + part
        x_t = x_ref[...]
        r = lax.rsqrt(jnp.mean(x_t * x_t, axis=-1, keepdims=True) + EPS)
        xh = x_t * r
        dg_ref[...] += jnp.sum(d * xh, axis=0, keepdims=True)
        dxh = d * g_ref[...]
        dx_ref[...] = r_ref[...] + r * (dxh - xh * jnp.mean(dxh * xh, axis=-1, keepdims=True))

    row_blk = pl.BlockSpec((tm, Dout), lambda i: (i, 0))
    vec = pl.BlockSpec((1, Dout), lambda i: (0, 0))
    a_spec = (pl.BlockSpec((planes, tm, per * Kw), lambda i: (0, i, 0)) if a.ndim == 3
              else pl.BlockSpec((tm, G * Kw), lambda i: (i, 0)))
    return pl.pallas_call(body, name=name, grid=(S // tm,),
                          in_specs=[a_spec, pl.BlockSpec((G, Dout, Kw), lambda i: (0, 0, 0)), row_blk, vec, row_blk],
                          out_specs=[row_blk, vec], out_shape=[_sds((S, Dout), f32), _sds((1, Dout), f32)],
                          compiler_params=_params(1, True))(a, w3, x, g, dres)


LRU_TT = 512
SCAN_UNROLL = 8


HALO = 16


def _halo(ref, i, S):
    nt = S // LRU_TT
    t0 = pl.multiple_of(i * LRU_TT, LRU_TT)
    p0 = pl.multiple_of(jnp.maximum(t0 - HALO, 0), HALO)
    n0 = pl.multiple_of(jnp.minimum(t0 + LRU_TT, S - HALO), HALO)
    prev = jnp.where(i > 0, ref[pl.ds(p0, HALO), :].astype(f32), 0.0)
    nxt = jnp.where(i < nt - 1, ref[pl.ds(n0, HALO), :].astype(f32), 0.0)
    return jnp.concatenate([prev, ref[pl.ds(t0, LRU_TT), :].astype(f32), nxt], axis=0)


def _shift(ext, k):
    n = LRU_TT + 2 * HALO
    return pltpu.roll(ext, (-k) % n, 0)[HALO:HALO + LRU_TT]


def _lru_gates(uc, wbd, ba, bx):
    pre = jnp.dot(uc.astype(bf16), wbd, preferred_element_type=f32)
    r_f = _sigmoid(pre[:, 0:CW] + ba[0:1])
    i_f = _sigmoid(pre[:, CW:2 * CW] + bx[0:1])
    r_b = _sigmoid(pre[:, 2 * CW:3 * CW] + ba[1:2])
    i_b = _sigmoid(pre[:, 3 * CW:4 * CW] + bx[1:2])
    return r_f, i_f, r_b, i_b


def _lru_coeffs(r, sp):
    log_a = (-RGLRU_C * r) * sp
    a = jnp.exp(log_a)
    beta = jnp.sqrt(jnp.maximum(_one_minus_exp2x(log_a, a), 0.0))
    return a, beta


def _lru_coeffs_inv(r, sp):
    log_a = (-RGLRU_C * r) * sp
    a = jnp.exp(log_a)
    om = jnp.maximum(_one_minus_exp2x(log_a, a), 0.0)
    return a, jnp.sqrt(om), lax.rsqrt(jnp.maximum(om, 1e-30))


def _conv_tile(u_ref, i, S, cw, cb):
    ext = _halo(u_ref, i, S)
    um2, um1, u0, up1 = _shift(ext, -2), _shift(ext, -1), ext[HALO:HALO + LRU_TT], _shift(ext, 1)
    uc = um2 * cw[0:1] + um1 * cw[1:2] + u0 * cw[2:3] + up1 * cw[3:4] + cb
    return uc, (um2, um1, u0, up1)


def _scan_pair(S, fwd_a, fwd_b, fwd_out, rev_a, rev_b, rev_out):
    ng = S // 8
    idx = lax.broadcasted_iota(jnp.int32, (8, CW), 0)

    def local(a, b, rev):
        for sh in (1, 2, 4):
            if rev:
                keep = idx < 8 - sh
                amt = 8 - sh
            else:
                keep = idx >= sh
                amt = sh
            a_s = jnp.where(keep, pltpu.roll(a, amt, 0), 1.0)
            b_s = jnp.where(keep, pltpu.roll(b, amt, 0), 0.0)
            b = a * b_s + b
            a = a * a_s
        return a, b

    def step(it, carry):
        cf, cr = carry
        fwd_rows = [pl.multiple_of((it * SCAN_UNROLL + j) * 8, 8) for j in range(SCAN_UNROLL)]
        rev_rows = [pl.multiple_of((ng - 1 - (it * SCAN_UNROLL + j)) * 8, 8) for j in range(SCAN_UNROLL)]
        fwd_loc = [local(fwd_a(r), fwd_b(r), False) for r in fwd_rows]
        rev_loc = [local(rev_a(r), rev_b(r), True) for r in rev_rows]
        for j in range(SCAN_UNROLL):
            a, b = fwd_loc[j]
            h = a * cf + b
            fwd_out[pl.ds(fwd_rows[j], 8), :] = h
            cf = jnp.broadcast_to(h[7:8, :], (8, CW))
            a, b = rev_loc[j]
            h = a * cr + b
            rev_out[pl.ds(rev_rows[j], 8), :] = h
            cr = jnp.broadcast_to(h[0:1, :], (8, CW))
        return cf, cr

    zero = jnp.zeros((8, CW), f32)
    lax.fori_loop(0, ng // SCAN_UNROLL, step, (zero, zero))


def _lru_specs(S):
    seq = lambda off: pl.BlockSpec((S, CW), lambda j: (0, off + j))
    par = lambda rows: pl.BlockSpec((rows, CW), lambda j: (0, j))
    return seq, par


def _lru_fwd(proj, conv_w, conv_b, lam, ba, bx, wbd):
    S = proj.shape[0]
    nt = S // LRU_TT

    def body(u_ref, g_ref, cw_ref, cb_ref, lam_ref, ba_ref, bx_ref, wbd_ref, y_ref, state_ref, af_ref, bf_ref, ab_ref, bb_ref,
             sems):
        cw, cb, ba_v, bx_v, wbd_v = cw_ref[...], cb_ref[...], ba_ref[...], bx_ref[...], wbd_ref[...]
        sp = jax.nn.softplus(-lam_ref[...])
        cols = pl.ds(pl.multiple_of(pl.program_id(0) * CW, CW), CW)
        save = [pltpu.make_async_copy(ref, state_ref.at[k, :, cols], sems.at[k])
                for k, ref in enumerate((af_ref, bf_ref, ab_ref, bb_ref))]

        def phase1(i, c):
            uc, _ = _conv_tile(u_ref, i, S, cw, cb)
            r_f, i_f, r_b, i_b = _lru_gates(uc, wbd_v, ba_v, bx_v)
            rows = pl.ds(pl.multiple_of(i * LRU_TT, LRU_TT), LRU_TT)
            a, beta = _lru_coeffs(r_f, sp[0:1])
            af_ref[rows, :] = a
            bf_ref[rows, :] = beta * (i_f * uc)
            a, beta = _lru_coeffs(r_b, sp[1:2])
            ab_ref[rows, :] = a
            bb_ref[rows, :] = beta * (i_b * uc)
            return c

        lax.fori_loop(0, nt, phase1, 0)
        save[0].start()
        save[2].start()
        row8 = lambda ref: (lambda r0: ref[pl.ds(r0, 8), :])
        _scan_pair(S, row8(af_ref), row8(bf_ref), bf_ref, row8(ab_ref), row8(bb_ref), bb_ref)
        save[1].start()
        save[3].start()

        def phase3(i, c):
            rows = pl.ds(pl.multiple_of(i * LRU_TT, LRU_TT), LRU_TT)
            y = (bf_ref[rows, :] + bb_ref[rows, :]) * jax.nn.gelu(g_ref[rows, :].astype(f32))
            y_ref[rows, :] = y.astype(y_ref.dtype)
            return c

        lax.fori_loop(0, nt, phase3, 0)
        for cp in save:
            cp.wait()

    seq, par = _lru_specs(S)
    return pl.pallas_call(
        body, name="lru_fwd", grid=(NCH,),
        in_specs=[seq(0), seq(NCH), par(4), par(1), par(2), par(2), par(2),
                  pl.BlockSpec((None, CW, 4 * CW), lambda j: (j, 0, 0))],
        out_specs=[seq(0), ANY], out_shape=[_sds((S, D), bf16), _sds((4, S, D), f32)],
        scratch_shapes=[pltpu.VMEM((S, CW), f32)] * 4 + [pltpu.SemaphoreType.DMA((4,))], compiler_params=_params(1, True),
    )(proj, proj, conv_w, conv_b, lam, ba, bx, wbd)


def _lru_bwd(proj, dy, state, dproj, conv_w, conv_b, lam, ba, bx, wbd):
    S = proj.shape[0]
    nt = S // LRU_TT

    def body(u_ref, g_ref, dy_ref, state_ref, dproj_in, cw_ref, cb_ref, lam_ref, ba_ref, bx_ref, wbd_ref,
             dproj_ref, dcw_ref, dcb_ref, dlam_ref, dba_ref, dbx_ref, dwbd_ref,
             af_ref, hf2_ref, ab_ref, hb2_ref, dh_ref, du_ref, dg_ref, sems):
        cw, cb, ba_v, bx_v, wbd_v = cw_ref[...], cb_ref[...], ba_ref[...], bx_ref[...], wbd_ref[...]
        lam_v = lam_ref[...]
        sp = jax.nn.softplus(-lam_v)
        chunk = pl.program_id(0)
        slot = chunk % 2
        bf_ref, bb_ref = hf2_ref.at[slot], hb2_ref.at[slot]

        def out_copies(j):
            c0 = pl.multiple_of(j * CW, CW)
            return [pltpu.make_async_copy(du_ref, dproj_ref.at[:, pl.ds(c0, CW)], sems.at[4]),
                    pltpu.make_async_copy(dg_ref, dproj_ref.at[:, pl.ds(D + c0, CW)], sems.at[5])]

        @pl.when(chunk >= 1)
        def _():
            for cp in out_copies(chunk - 1):
                cp.wait()

        def state_copy(k, j, dst, sem):
            return pltpu.make_async_copy(state_ref.at[k, :, pl.ds(pl.multiple_of(j * CW, CW), CW)], dst, sem)

        def hidden_loads(j):
            return [state_copy(1, j, hf2_ref.at[j % 2], sems.at[6 + j % 2]), state_copy(3, j, hb2_ref.at[j % 2], sems.at[8 + j % 2])]

        load = [state_copy(0, chunk, af_ref, sems.at[0]), None, state_copy(2, chunk, ab_ref, sems.at[2])]

        @pl.when(chunk == 0)
        def _():
            for cp in hidden_loads(chunk):
                cp.start()

        load[0].start()
        load[2].start()

        @pl.when(chunk + 1 < NCH)
        def _():
            for cp in hidden_loads(chunk + 1):
                cp.start()

        for cp in hidden_loads(chunk):
            cp.wait()
        row8 = lambda ref: (lambda r0: ref[pl.ds(r0, 8), :])

        def phase0(i, c):
            rows = pl.ds(pl.multiple_of(i * LRU_TT, LRU_TT), LRU_TT)
            gl, dgl = _gelu_and_grad(g_ref[rows, :].astype(f32))
            dyt = dy_ref[rows, :].astype(f32)
            dh_ref[rows, :] = dyt * gl
            dg_ref[rows, :] = ((dyt * (bf_ref[rows, :] + bb_ref[rows, :])) * dgl).astype(dg_ref.dtype)
            return c

        lax.fori_loop(0, nt, phase0, 0)
        load[0].wait()
        load[2].wait()

        def scaled_dh(a_ref):
            def f(r0):
                return a_ref[pl.ds(r0, 8), :] * dh_ref[pl.ds(r0, 8), :]
            return f

        _scan_pair(S, row8(ab_ref), scaled_dh(ab_ref), ab_ref, row8(af_ref), scaled_dh(af_ref), af_ref)

        dcw_ref[...] = jnp.zeros_like(dcw_ref)
        dcb_ref[...] = jnp.zeros_like(dcb_ref)
        dlam_ref[...] = jnp.zeros_like(dlam_ref)
        dba_ref[...] = jnp.zeros_like(dba_ref)
        dbx_ref[...] = jnp.zeros_like(dbx_ref)
        dwbd_ref[...] = jnp.zeros_like(dwbd_ref)

        def direction(uc, r, i_g, dht, h_nb, sp_d):
            a, beta, inv_beta = _lru_coeffs_inv(r, sp_d)
            da = dht * h_nb
            dbeta = dht * (i_g * uc)
            d_iu = dht * beta
            dlog_a = da * a - (a * a) * (dbeta * inv_beta)
            dlr = dlog_a * r
            dsp = -RGLRU_C * jnp.sum(dlr, axis=0, keepdims=True)
            dpre_r = (dlr * (1.0 - r)) * (-RGLRU_C * sp_d)
            dpre_i = (d_iu * uc) * (i_g * (1.0 - i_g))
            return dpre_r, dpre_i, d_iu * i_g, dsp

        def phase4(i, c):
            uc, (um2, um1, u0, up1) = _conv_tile(u_ref, i, S, cw, cb)
            r_f, i_f, r_b, i_b = _lru_gates(uc, wbd_v, ba_v, bx_v)
            rows = pl.ds(pl.multiple_of(i * LRU_TT, LRU_TT), LRU_TT)
            dh = dh_ref[rows, :]
            dht_f = dh + _shift(_halo(af_ref, i, S), 1)
            h_prev = _shift(_halo(bf_ref, i, S), -1)
            dht_b = dh + _shift(_halo(ab_ref, i, S), -1)
            h_next = _shift(_halo(bb_ref, i, S), 1)
            prf, pif, duc_f, dsp_f = direction(uc, r_f, i_f, dht_f, h_prev, sp[0:1])
            prb, pib, duc_b, dsp_b = direction(uc, r_b, i_b, dht_b, h_next, sp[1:2])
            dpre = jnp.concatenate([prf, pif, prb, pib], axis=1)
            dpre_b = dpre.astype(bf16)
            duc = (duc_f + duc_b) + lax.dot_general(dpre_b, wbd_v, _DIMS["nt"], preferred_element_type=f32)
            dwbd_ref[...] += lax.dot_general(uc.astype(bf16), dpre_b, _DIMS["tn"], preferred_element_type=f32)
            colsum = lambda v: jnp.sum(v, axis=0, keepdims=True)
            dba_ref[...] += jnp.concatenate([colsum(prf), colsum(prb)], axis=0)
            dbx_ref[...] += jnp.concatenate([colsum(pif), colsum(pib)], axis=0)
            dlam_ref[...] += jnp.concatenate([dsp_f, dsp_b], axis=0)
            dcb_ref[...] += colsum(duc)
            dcw_ref[...] += jnp.concatenate([colsum(duc * um2), colsum(duc * um1), colsum(duc * u0),
                                             colsum(duc * up1)], axis=0)
            af_ref[rows, :] = duc
            return c

        lax.fori_loop(0, nt, phase4, 0)
        dlam_ref[...] = dlam_ref[...] * (-_sigmoid(-lam_v))

        def phase5(i, c):
            ext = _halo(af_ref, i, S)
            rows = pl.ds(pl.multiple_of(i * LRU_TT, LRU_TT), LRU_TT)
            du = (_shift(ext, 2) * cw[0:1] + _shift(ext, 1) * cw[1:2] + ext[HALO:HALO + LRU_TT] * cw[2:3]
                  + _shift(ext, -1) * cw[3:4])
            du_ref[rows, :] = du.astype(du_ref.dtype)
            return c

        lax.fori_loop(0, nt, phase5, 0)
        for cp in out_copies(chunk):
            cp.start()

        @pl.when(chunk == NCH - 1)
        def _():
            for cp in out_copies(chunk):
                cp.wait()

    seq, par = _lru_specs(S)
    return pl.pallas_call(
        body, name="lru_bwd", grid=(NCH,),
        in_specs=[seq(0), seq(NCH), pl.BlockSpec((None, S, CW), lambda j: (0, 0, j)), ANY, ANY,
                  par(4), par(1), par(2), par(2), par(2), pl.BlockSpec((None, CW, 4 * CW), lambda j: (j, 0, 0))],
        out_specs=[ANY, par(4), par(1), par(2), par(2), par(2),
                   pl.BlockSpec((None, CW, 4 * CW), lambda j: (j, 0, 0))],
        out_shape=[_sds(dproj.shape, bf16), _sds((4, D), f32), _sds((1, D), f32), _sds((2, D), f32),
                   _sds((2, D), f32), _sds((2, D), f32), _sds((NCH, CW, 4 * CW), f32)],
        scratch_shapes=[pltpu.VMEM((S, CW), f32), pltpu.VMEM((2, S, CW), f32), pltpu.VMEM((S, CW), f32), pltpu.VMEM((2, S, CW), f32),
                        pltpu.VMEM((S, CW), f32), pltpu.VMEM((S, CW), bf16), pltpu.VMEM((S, CW), bf16),
                        pltpu.SemaphoreType.DMA((10,))],
        input_output_aliases={4: 0}, compiler_params=_params(1, True),
    )(proj, proj, dy, state, dproj, conv_w, conv_b, lam, ba, bx, wbd)


_SLOPES = [2.0 ** (-8.0 * (h + 1) / NH) for h in range(NH)]


def _half_mask(shape, e):
    lane = lax.broadcasted_iota(jnp.int32, shape, 1)
    return (lane < HD) if e == 0 else (lane >= HD)


def _both_halves(x, src):
    return jnp.where(_half_mask(x.shape, src), x, pltpu.roll(x, HD, 1))


def _attn_base(n, S):
    tq = lax.broadcasted_iota(jnp.int32, (BLK, 3 * BLK), 0)
    sk = lax.broadcasted_iota(jnp.int32, (BLK, 3 * BLK), 1)
    dist = jnp.abs(tq + BLK - sk)
    kpos = n * BLK - BLK + sk
    valid = (dist <= BLK) & (kpos >= 0) & (kpos < S)
    return jnp.where(valid, -dist.astype(f32), NEG_INF)


def _group_heads(ref, kvh, scale):
    parts = []
    for i in range(4):
        pair = 2 * kvh + i // 2
        x = ref[:, pair * 128:(pair + 1) * 128].astype(f32)
        parts.append(jnp.where(_half_mask(x.shape, i % 2), x * scale, 0.0))
    return parts


def _stack_bf16(parts):
    return jnp.concatenate([p.astype(bf16) for p in parts], axis=0)


def _attn_softmax(s_raw, base, slope, sink):
    s = s_raw + slope * base
    m = jnp.maximum(jnp.max(s, axis=-1, keepdims=True), sink)
    p = jnp.exp(s - m)
    esink = jnp.exp(sink - m)
    inv = 1.0 / (jnp.sum(p, axis=-1, keepdims=True) + esink)
    return p, inv, esink * inv


def _attn_specs(S):
    nb = S // BLK
    q_spec = pl.BlockSpec((BLK, D), lambda n: (n, 2))
    kv = lambda col: [pl.BlockSpec((BLK, 256), lambda n: (jnp.maximum(n - 1, 0), col)),
                      pl.BlockSpec((BLK, 256), lambda n: (n, col)),
                      pl.BlockSpec((BLK, 256), lambda n: (jnp.minimum(n + 1, nb - 1), col))]
    return nb, q_spec, kv(COL_K), kv(COL_V)


def _attn_fwd(proj, sink):
    S = proj.shape[0]
    nb, q_spec, k_specs, v_specs = _attn_specs(S)

    def body(sink_ref, q_ref, kp_ref, kc_ref, kn_ref, vp_ref, vc_ref, vn_ref, o_ref):
        base = _attn_base(pl.program_id(0), S)
        kcat = jnp.concatenate([kp_ref[...], kc_ref[...], kn_ref[...]], axis=0).astype(f32)
        vcat = jnp.concatenate([vp_ref[...], vc_ref[...], vn_ref[...]], axis=0).astype(f32)
        even = _half_mask((BLK, 128), 0)
        for kvh in range(NH // 4):
            ch, off = kvh // 2, kvh % 2
            kb = _both_halves(kcat[:, ch * 128:(ch + 1) * 128], off).astype(bf16)
            vb = _both_halves(vcat[:, ch * 128:(ch + 1) * 128], off).astype(bf16)
            q4 = _stack_bf16(_group_heads(q_ref, kvh, HD ** -0.5))
            s4 = lax.dot_general(q4, kb, _DIMS["nt"], preferred_element_type=f32)
            ps, invs = [], []
            for i in range(4):
                h = 4 * kvh + i
                p, inv, _ = _attn_softmax(s4[i * BLK:(i + 1) * BLK], base, _SLOPES[h], sink_ref[0, h])
                ps.append(p)
                invs.append(inv)
            o4 = jnp.dot(_stack_bf16(ps), vb, preferred_element_type=f32)
            for pr in range(2):
                lo = o4[(2 * pr) * BLK:(2 * pr + 1) * BLK] * invs[2 * pr]
                hi = o4[(2 * pr + 1) * BLK:(2 * pr + 2) * BLK] * invs[2 * pr + 1]
                pair = 2 * kvh + pr
                o_ref[:, pair * 128:(pair + 1) * 128] = jnp.where(even, lo, hi).astype(o_ref.dtype)

    return pl.pallas_call(
        body, name="attn_fwd", grid=(nb,),
        in_specs=[pl.BlockSpec(memory_space=pltpu.SMEM), q_spec] + k_specs + v_specs,
        out_specs=pl.BlockSpec((BLK, D), lambda n: (n, 0)), out_shape=_sds((S, D), bf16),
        compiler_params=_params(1, True))(sink, proj, proj, proj, proj, proj, proj, proj)


def _attn_bwd(proj, sink, y_b, dy, dproj):
    S = proj.shape[0]
    nb, q_spec, k_specs, v_specs = _attn_specs(S)
    q_col, kv_col = COL_Q * 256, COL_K * 256

    def body(sink_ref, q_ref, kp_ref, kc_ref, kn_ref, vp_ref, vc_ref, vn_ref, o_ref, do_ref, dproj_in,
             dproj_ref, dsink_ref, dk_ref, dv_ref, dq_buf, kv_buf, sems):
        n = pl.program_id(0)
        slot = n % 2
        dq_ref = dq_buf.at[slot]

        def dq_copy(step):
            rows = pl.ds(pl.multiple_of(step * BLK, BLK), BLK)
            return pltpu.make_async_copy(dq_buf.at[step % 2], dproj_ref.at[rows, pl.ds(q_col, D)], sems.at[step % 2])

        @pl.when(n >= 2)
        def _():
            dq_copy(n - 2).wait()

        @pl.when(n == 0)
        def _():
            dk_ref[...] = jnp.zeros_like(dk_ref)
            dv_ref[...] = jnp.zeros_like(dv_ref)
            dsink_ref[...] = jnp.zeros_like(dsink_ref)

        base = _attn_base(n, S)
        kcat = jnp.concatenate([kp_ref[...], kc_ref[...], kn_ref[...]], axis=0).astype(f32)
        vcat = jnp.concatenate([vp_ref[...], vc_ref[...], vn_ref[...]], axis=0).astype(f32)
        dk_rows, dv_rows = [[], []], [[], []]
        scale = HD ** -0.5
        even = _half_mask((BLK, 128), 0)
        for kvh in range(NH // 4):
            ch, off = kvh // 2, kvh % 2
            kb = _both_halves(kcat[:, ch * 128:(ch + 1) * 128], off).astype(bf16)
            vb = _both_halves(vcat[:, ch * 128:(ch + 1) * 128], off).astype(bf16)
            q_parts = _group_heads(q_ref, kvh, scale)
            d_parts = _group_heads(do_ref, kvh, 1.0)
            s4 = lax.dot_general(_stack_bf16(q_parts), kb, _DIMS["nt"], preferred_element_type=f32)
            dp4 = lax.dot_general(_stack_bf16(d_parts), vb, _DIMS["nt"], preferred_element_type=f32)
            ts, ps, qn, dn, invs = [], [], [], [], []
            for i in range(4):
                h = 4 * kvh + i
                pair = 2 * kvh + i // 2
                rows = slice(i * BLK, (i + 1) * BLK)
                p, inv, psink = _attn_softmax(s4[rows], base, _SLOPES[h], sink_ref[0, h])
                delta = jnp.sum(d_parts[i] * o_ref[:, pair * 128:(pair + 1) * 128].astype(f32), axis=-1, keepdims=True)
                dsink_ref[h:h + 1, :] += jnp.broadcast_to(-jnp.sum(psink * delta, axis=0, keepdims=True), (1, 128))
                ts.append(p * (dp4[rows] - delta))
                ps.append(p)
                qn.append(q_parts[i] * inv)
                dn.append(d_parts[i] * inv)
                invs.append(inv)
            t4 = _stack_bf16(ts)
            dq4 = jnp.dot(t4, kb, preferred_element_type=f32)
            for pr in range(2):
                lo = dq4[(2 * pr) * BLK:(2 * pr + 1) * BLK] * invs[2 * pr]
                hi = dq4[(2 * pr + 1) * BLK:(2 * pr + 2) * BLK] * invs[2 * pr + 1]
                pair = 2 * kvh + pr
                dq_ref[:, pair * 128:(pair + 1) * 128] = (jnp.where(even, lo, hi) * scale).astype(dq_ref.dtype)
            dk_t = lax.dot_general(_stack_bf16(qn), t4, _DIMS["tn"], preferred_element_type=f32)
            dv_t = lax.dot_general(_stack_bf16(dn), _stack_bf16(ps), _DIMS["tn"], preferred_element_type=f32)
            dk_rows[ch].append(dk_t[0:HD] + dk_t[HD:2 * HD])
            dv_rows[ch].append(dv_t[0:HD] + dv_t[HD:2 * HD])
        dk_acc = [jnp.concatenate(r, axis=0).T for r in dk_rows]
        dv_acc = [jnp.concatenate(r, axis=0).T for r in dv_rows]
        for j in range(3):
            blk = n + (j - 1)

            @pl.when((blk >= 0) & (blk < nb))
            def _():
                rows = pl.ds(pl.multiple_of(blk * BLK, BLK), BLK)
                for ch in range(2):
                    dk_ref[rows, ch * 128:(ch + 1) * 128] += dk_acc[ch][j * BLK:(j + 1) * BLK]
                    dv_ref[rows, ch * 128:(ch + 1) * 128] += dv_acc[ch][j * BLK:(j + 1) * BLK]

        dq_copy(n).start()

        @pl.when(n == nb - 1)
        def _():
            def cast(i, c):
                rows = pl.ds(pl.multiple_of(i * 4 * BLK, 4 * BLK), 4 * BLK)
                kv_buf[rows, 0:256] = dk_ref[rows, :].astype(bf16)
                kv_buf[rows, 256:512] = dv_ref[rows, :].astype(bf16)
                return c

            lax.fori_loop(0, S // (4 * BLK), cast, 0)
            kv_copy = pltpu.make_async_copy(kv_buf, dproj_ref.at[:, pl.ds(kv_col, 512)], sems.at[2])
            kv_copy.start()
            if nb >= 2:
                dq_copy(n - 1).wait()
            dq_copy(n).wait()
            kv_copy.wait()

    row_blk = pl.BlockSpec((BLK, D), lambda n: (n, 0))
    return pl.pallas_call(
        body, name="attn_bwd", grid=(nb,),
        in_specs=[pl.BlockSpec(memory_space=pltpu.SMEM), q_spec] + k_specs + v_specs
        + [row_blk, pl.BlockSpec((None, BLK, D), lambda n: (1, n, 0)), ANY],
        out_specs=[ANY, pl.BlockSpec((NH, 128), lambda n: (0, 0))],
        out_shape=[_sds(dproj.shape, bf16), _sds((NH, 128), f32)],
        scratch_shapes=[pltpu.VMEM((S, 256), f32), pltpu.VMEM((S, 256), f32), pltpu.VMEM((2, BLK, D), bf16),
                        pltpu.VMEM((S, 512), bf16), pltpu.SemaphoreType.DMA((3,))],
        input_output_aliases={10: 0},
        compiler_params=_params(1, True))(sink, proj, proj, proj, proj, proj, proj, proj, y_b, dy, dproj)


def _adamw(name, w, g, m, v, tr):
    R, C = w.shape
    tr = min(tr, R)

    def body(w_ref, g_ref, m_ref, v_ref, d_ref, m2_ref, v2_ref):
        g = g_ref[...]
        m2 = ADAM_B1 * m_ref[...] + (1.0 - ADAM_B1) * g
        v2 = ADAM_B2 * v_ref[...] + (1.0 - ADAM_B2) * (g * g)
        m_hat = m2 / (1.0 - ADAM_B1 ** ADAM_STEP)
        v_hat = v2 / (1.0 - ADAM_B2 ** ADAM_STEP)
        d_ref[...] = -ADAM_LR * (m_hat / (jnp.sqrt(v_hat) + ADAM_EPS) + ADAM_WD * w_ref[...])
        m2_ref[...] = m2
        v2_ref[...] = v2

    blk = pl.BlockSpec((tr, C), lambda i: (i, 0))
    return pl.pallas_call(body, name=name, grid=(R // tr,), in_specs=[blk] * 4, out_specs=[blk] * 3,
                          out_shape=[_sds((R, C), f32)] * 3, compiler_params=_params(1))(w, g, m, v)


def _pair_sum(name, c_arr, g4, recv, th):
    _, _, h, w = g4.shape
    th = min(th, h)

    def body(c_ref, g_ref, r_ref, o_ref, ob_ref):
        p = g_ref[...] + r_ref[...]
        o_ref[...] = p
        ob_ref[...] = p.astype(bf16)

    blk = pl.BlockSpec((None, th, w), lambda s, i, c_ref: (s, i, 0))
    spec = pltpu.PrefetchScalarGridSpec(
        num_scalar_prefetch=1, grid=(NCHIP, h // th),
        in_specs=[pl.BlockSpec((None, None, th, w), lambda s, i, c_ref: (s, c_ref[0], i, 0)), blk],
        out_specs=[blk, blk])
    return pl.pallas_call(body, name=name, grid_spec=spec,
                          out_shape=[_sds((NCHIP, h, w), f32), _sds((NCHIP, h, w), bf16)],
                          compiler_params=_params(2))(c_arr, g4, recv)


def _chip_sum(name, chip_arr, own4, recv3, th):
    _, h, w = own4.shape
    th = min(th, h)

    def body(s_ref, o_ref, r_ref, out_ref):
        out_ref[...] = ((o_ref[...] + r_ref[0].astype(f32)) + r_ref[1].astype(f32)) + r_ref[2].astype(f32)

    spec = pltpu.PrefetchScalarGridSpec(
        num_scalar_prefetch=1, grid=(h // th,),
        in_specs=[pl.BlockSpec((None, th, w), lambda i, s_ref: (s_ref[0], i, 0)),
                  pl.BlockSpec((3, th, w), lambda i, s_ref: (0, i, 0))],
        out_specs=pl.BlockSpec((th, w), lambda i, s_ref: (i, 0)))
    return pl.pallas_call(body, name=name, grid_spec=spec, out_shape=_sds((h, w), f32),
                          compiler_params=_params(1, True))(chip_arr, own4, recv3)


def _adamw_halves(name, c_arr, w, g_own, g_recv, m, v, th):
    h, wd = g_own.shape
    th = min(th, h)

    def body(c_ref, w_ref, go_ref, gr_ref, m_ref, v_ref, g_ref, d_ref, m2_ref, v2_ref):
        g = jnp.where(c_ref[0] == pl.program_id(0), go_ref[...], gr_ref[...])
        m2 = ADAM_B1 * m_ref[...] + (1.0 - ADAM_B1) * g
        v2 = ADAM_B2 * v_ref[...] + (1.0 - ADAM_B2) * (g * g)
        m_hat = m2 / (1.0 - ADAM_B1 ** ADAM_STEP)
        v_hat = v2 / (1.0 - ADAM_B2 ** ADAM_STEP)
        g_ref[...] = g
        d_ref[...] = -ADAM_LR * (m_hat / (jnp.sqrt(v_hat) + ADAM_EPS) + ADAM_WD * w_ref[...])
        m2_ref[...] = m2
        v2_ref[...] = v2

    nt = h // th
    full = pl.BlockSpec((th, wd), lambda hh, i, c_ref: (hh * nt + i, 0))
    half = pl.BlockSpec((th, wd), lambda hh, i, c_ref: (i, 0))
    spec = pltpu.PrefetchScalarGridSpec(num_scalar_prefetch=1, grid=(2, nt),
                                        in_specs=[full, half, half, full, full], out_specs=[full] * 4)
    return pl.pallas_call(body, name=name, grid_spec=spec, out_shape=[_sds((2 * h, wd), f32)] * 4,
                          compiler_params=_params(2))(c_arr, w, g_own, g_recv, m, v)


def _add2(name, a, b):
    def body(a_ref, b_ref, o_ref):
        o_ref[...] = a_ref[...] + b_ref[...]
    return pl.pallas_call(body, name=name, out_shape=_sds(a.shape, f32))(a, b)


def _sum4(name, b4, th):
    _, h, w = b4.shape
    th = min(th, h)

    def body(b_ref, o_ref):
        o_ref[...] = ((b_ref[0] + b_ref[1]) + b_ref[2]) + b_ref[3]

    return pl.pallas_call(body, name=name, grid=(h // th,),
                          in_specs=[pl.BlockSpec((NCHIP, th, w), lambda i: (0, i, 0))],
                          out_specs=pl.BlockSpec((th, w), lambda i: (i, 0)), out_shape=_sds((h, w), f32),
                          compiler_params=_params(1, True))(b4)


def _coords():
    x, y, c = lax.axis_index("x"), lax.axis_index("y"), lax.axis_index("c")
    return x, y, c, [(1 - x, y), (x, 1 - y), (1 - x, 1 - y)]


def _gather_chips(arrs):
    n = len(arrs)

    def body(*refs):
        ins, outs = refs[:n], refs[n:2 * n]
        send_sems, recv_sems, local_sems = refs[2 * n:2 * n + 3]
        stage = refs[2 * n + 3:]
        x, y, c, chips = _coords()
        s = 2 * x + y
        sib = (x, y, 1 - c)
        load = [pltpu.make_async_copy(ins[a], stage[a], local_sems.at[a]) for a in range(n)]
        local = [pltpu.make_async_copy(stage[a], outs[a].at[s], local_sems.at[n + a]) for a in range(n)]
        for cp in load:
            cp.start()

        def over_ici(k, a, slot, peer):
            return pltpu.make_async_remote_copy(src_ref=ins[a].at[c], dst_ref=outs[a].at[slot, c], send_sem=send_sems.at[k * n + a],
                                                recv_sem=recv_sems.at[k * n + a], device_id=peer, device_id_type=MESH)

        def to_sibling(k, a, slot, half):
            i = (3 + k) * n + a
            return pltpu.make_async_remote_copy(src_ref=outs[a].at[slot, half], dst_ref=outs[a].at[slot, half], send_sem=send_sems.at[i],
                                                recv_sem=recv_sems.at[i], device_id=sib, device_id_type=MESH)

        sends = [over_ici(k, a, s, (px, py, c)) for k, (px, py) in enumerate(chips) for a in range(n)]
        for cp in sends:
            cp.start()
        for a in range(n):
            load[a].wait()
            local[a].start()
        passed = []
        for k, (px, py) in enumerate(chips):
            for a in range(n):
                over_ici(k, a, 2 * px + py, (px, py, c)).wait_recv()
                cp = to_sibling(k, a, 2 * px + py, c)
                cp.start()
                passed.append(cp)
        for k, (px, py) in enumerate(chips):
            for a in range(n):
                to_sibling(k, a, 2 * px + py, 1 - c).wait_recv()
        for cp in sends + passed:
            cp.wait_send()
        for cp in local:
            cp.wait()

    return pl.pallas_call(
        body, name="gather_weights", in_specs=[ANY] * n, out_specs=[ANY] * n,
        out_shape=[_sds((NCHIP,) + a.shape, a.dtype) for a in arrs],
        scratch_shapes=[pltpu.SemaphoreType.DMA((6 * n,)), pltpu.SemaphoreType.DMA((6 * n,)), pltpu.SemaphoreType.DMA((2 * n,))]
        + [pltpu.VMEM(a.shape, a.dtype) for a in arrs],
        compiler_params=pltpu.CompilerParams(vmem_limit_bytes=VMEM_LIMIT),
    )(*arrs)


HBM = pl.BlockSpec(memory_space=pltpu.HBM)
SEM = pl.BlockSpec(memory_space=pltpu.SEMAPHORE)
EFFECT = pltpu.SideEffectType.DATAFLOW_SIDE_EFFECTING


def _split_start(name, n_copies, make_copies, ins, land_shapes, after):
    ni, nl = len(ins), len(land_shapes)

    def body(*refs):
        in_refs, land_refs = refs[:ni], refs[ni:ni + nl]
        send_sems, recv_sems = refs[ni + nl + 1], refs[ni + nl + 2]
        token = refs[-1]
        for cp in make_copies(in_refs, land_refs, send_sems, recv_sems):
            cp.start()
        token[...] = jnp.zeros_like(token)

    lands = [pltpu.with_memory_space_constraint(lax.empty(s.shape, s.dtype), pltpu.HBM) for s in land_shapes]
    res = pl.pallas_call(
        body, name=name,
        out_shape=(pltpu.SemaphoreType.DMA((n_copies,)), pltpu.SemaphoreType.DMA((n_copies,)),
                   *[pltpu.HBM(a.shape, a.dtype) for a in ins], *[pltpu.HBM(s.shape, s.dtype) for s in land_shapes],
                   _sds((8, 128), f32)),
        in_specs=[HBM] * (ni + nl) + [ANY], out_specs=(SEM, SEM, *[HBM] * (ni + nl), pl.BlockSpec(memory_space=pltpu.VMEM)),
        input_output_aliases={i: 2 + i for i in range(ni + nl)},
        compiler_params=pltpu.CompilerParams(has_side_effects=EFFECT),
    )(*[pltpu.with_memory_space_constraint(a, pltpu.HBM) for a in ins], *lands, after)
    return res[0], res[1], list(res[2:2 + ni]), list(res[2 + ni:2 + ni + nl]), res[-1]


def _split_wait(name, make_copies, send_sems, recv_sems, ins, lands, after):
    ni, nl = len(ins), len(lands)

    def body(*refs):
        in_refs, land_refs = refs[:ni], refs[ni:ni + nl]
        s_sems, r_sems = refs[ni + nl], refs[ni + nl + 1]
        for cp in make_copies(in_refs, land_refs, s_sems, r_sems):
            cp.wait_send()
            cp.wait_recv()

    res = pl.pallas_call(
        body, name=name, out_shape=tuple(pltpu.HBM(a.shape, a.dtype) for a in ins + lands),
        in_specs=[HBM] * (ni + nl) + [SEM, SEM, ANY], out_specs=tuple([HBM] * (ni + nl)),
        input_output_aliases={i: i for i in range(ni + nl)},
        compiler_params=pltpu.CompilerParams(has_side_effects=EFFECT),
    )(*ins, *lands, send_sems, recv_sems, after)
    return list(res[:ni]), list(res[ni:])


def _gather_copies(n):
    def make(in_refs, land_refs, send_sems, recv_sems):
        x, y, c, chips = _coords()
        s = 2 * x + y
        return [pltpu.make_async_remote_copy(src_ref=in_refs[a], dst_ref=land_refs[a].at[s], send_sem=send_sems.at[k * n + a],
                                             recv_sem=recv_sems.at[k * n + a], device_id=(px, py, c), device_id_type=MESH)
                for k, (px, py) in enumerate(chips) for a in range(n)]
    return make


def _sibling_half_copies(n):
    def make(in_refs, land_refs, send_sems, recv_sems):
        x, y, c, _ = _coords()
        return [pltpu.make_async_remote_copy(src_ref=in_refs[a].at[:, 1 - c], dst_ref=land_refs[a], send_sem=send_sems.at[a],
                                             recv_sem=recv_sems.at[a], device_id=(x, y, 1 - c), device_id_type=MESH)
                for a in range(n)]
    return make


def _chip_part_copies(n):
    def make(in_refs, land_refs, send_sems, recv_sems):
        x, y, c, chips = _coords()
        return [pltpu.make_async_remote_copy(src_ref=in_refs[a].at[2 * px + py], dst_ref=land_refs[a].at[k],
                                             send_sem=send_sems.at[k * n + a], recv_sem=recv_sems.at[k * n + a],
                                             device_id=(px, py, c), device_id_type=MESH)
                for k, (px, py) in enumerate(chips) for a in range(n)]
    return make


def _sibling_whole_copies(n):
    def make(in_refs, land_refs, send_sems, recv_sems):
        x, y, c, _ = _coords()
        return [pltpu.make_async_remote_copy(src_ref=in_refs[a], dst_ref=land_refs[a], send_sem=send_sems.at[a],
                                             recv_sem=recv_sems.at[a], device_id=(x, y, 1 - c), device_id_type=MESH)
                for a in range(n)]
    return make


def _place_own(chip_arr, owns, lands, steps):
    n = len(owns)

    def body(s_ref, *refs):
        for a in range(n):
            refs[2 * n + a][...] = refs[a][...]

    tiles = [o.shape[0] // steps for o in owns]
    spec = pltpu.PrefetchScalarGridSpec(
        num_scalar_prefetch=1, grid=(steps,),
        in_specs=[pl.BlockSpec((t, o.shape[1]), lambda i, s_ref: (i, 0)) for t, o in zip(tiles, owns)] + [ANY] * n,
        out_specs=[pl.BlockSpec((None, t, o.shape[1]), lambda i, s_ref: (s_ref[0], i, 0)) for t, o in zip(tiles, owns)])
    return pl.pallas_call(body, name="place_own", grid_spec=spec, out_shape=[_sds(l.shape, l.dtype) for l in lands],
                          input_output_aliases={1 + n + a: a for a in range(n)},
                          compiler_params=_params(1))(chip_arr, *owns, *lands)


def _sibling_halves(g4s, small):
    n = len(g4s)

    def body(*refs):
        ins, small_ref = refs[:n], refs[n]
        outs, small_out = refs[n + 1:2 * n + 1], refs[2 * n + 1]
        send_sems, recv_sems = refs[2 * n + 2:]
        x, y, c, _ = _coords()
        sib = (x, y, 1 - c)

        def remote(a, half):
            src = small_ref if a == n else ins[a].at[:, half]
            dst = small_out if a == n else outs[a]
            return pltpu.make_async_remote_copy(src_ref=src, dst_ref=dst, send_sem=send_sems.at[a], recv_sem=recv_sems.at[a],
                                                device_id=sib, device_id_type=MESH)

        sends = [remote(a, 1 - c) for a in range(n + 1)]
        for cp in sends:
            cp.start()
        for a in range(n + 1):
            remote(a, c).wait_recv()
        for cp in sends:
            cp.wait_send()

    return pl.pallas_call(
        body, name="reduce_sibling", in_specs=[ANY] * (n + 1), out_specs=[ANY] * (n + 1),
        out_shape=[_sds((g.shape[0],) + g.shape[2:], f32) for g in g4s] + [_sds(small.shape, f32)],
        scratch_shapes=[pltpu.SemaphoreType.DMA((n + 1,)), pltpu.SemaphoreType.DMA((n + 1,))],
    )(*g4s, small)


def _exchange_chips(parts, small2):
    n = len(parts)

    def body(*refs):
        ins, small_ref = refs[:n], refs[n]
        outs, small_out = refs[n + 1:2 * n + 1], refs[2 * n + 1]
        send_sems, recv_sems, local_sem = refs[2 * n + 2:]
        x, y, c, chips = _coords()
        s = 2 * x + y
        local = pltpu.make_async_copy(small_ref.at[c], small_out.at[s], local_sem)
        local.start()

        def remote(k, a, dest_chip, small_slot, peer):
            if a == n:
                src, dst = small_ref.at[c], small_out.at[small_slot]
            else:
                src, dst = ins[a].at[dest_chip], outs[a].at[k]
            i = k * (n + 1) + a
            return pltpu.make_async_remote_copy(src_ref=src, dst_ref=dst, send_sem=send_sems.at[i], recv_sem=recv_sems.at[i],
                                                device_id=peer, device_id_type=MESH)

        sends = [remote(k, a, 2 * px + py, s, (px, py, c)) for k, (px, py) in enumerate(chips) for a in range(n + 1)]
        for cp in sends:
            cp.start()
        for k, (px, py) in enumerate(chips):
            for a in range(n + 1):
                remote(k, a, s, 2 * px + py, (px, py, c)).wait_recv()
        for cp in sends:
            cp.wait_send()
        local.wait()

    m = 3 * (n + 1)
    return pl.pallas_call(
        body, name="reduce_chips", in_specs=[ANY] * (n + 1), out_specs=[ANY] * (n + 1),
        out_shape=[_sds((3,) + p.shape[1:], p.dtype) for p in parts] + [_sds((NCHIP,) + small2.shape[1:], f32)],
        scratch_shapes=[pltpu.SemaphoreType.DMA((m,)), pltpu.SemaphoreType.DMA((m,)), pltpu.SemaphoreType.DMA],
    )(*parts, small2)


def _share_sibling(halves):
    n = len(halves)

    def body(*refs):
        ins, outs = refs[:n], refs[n:2 * n]
        send_sems, recv_sems = refs[2 * n:]
        x, y, c, _ = _coords()
        sib = (x, y, 1 - c)
        sends = [pltpu.make_async_remote_copy(src_ref=ins[a], dst_ref=outs[a], send_sem=send_sems.at[a], recv_sem=recv_sems.at[a],
                                              device_id=sib, device_id_type=MESH) for a in range(n)]
        for cp in sends:
            cp.start()
        for cp in sends:
            cp.wait()

    return pl.pallas_call(
        body, name="reduce_share", in_specs=[ANY] * n, out_specs=[ANY] * n,
        out_shape=[_sds(h.shape, f32) for h in halves],
        scratch_shapes=[pltpu.SemaphoreType.DMA((n,)), pltpu.SemaphoreType.DMA((n,))],
    )(*halves)


def _block_diag_pairs(w):
    w = w.reshape(NCH, 2, HD, HD)
    z = jnp.zeros((NCH, HD, HD), w.dtype)
    return jnp.concatenate([jnp.concatenate([w[:, 0], z], axis=2), jnp.concatenate([z, w[:, 1]], axis=2)], axis=1)


def _diag_blocks(m):
    return jnp.stack([m[:, :HD, :HD], m[:, HD:, HD:]], axis=1).reshape(NH, HD, HD)


def _pack(vs, rows):
    flat = jnp.concatenate([v.reshape(-1) for v in vs])
    return jnp.pad(flat, (0, rows * 128 - flat.shape[0])).reshape(rows, 128)


def _unpack(packed, shapes):
    flat = packed.reshape(-1)
    out, off = [], 0
    for shp in shapes:
        size = math.prod(shp)
        out.append(flat[off:off + size].reshape(shp))
        off += size
    return out


def _rows_for(sizes, multiple):
    rows = -(-sum(sizes) // 128)
    return -(-rows // multiple) * multiple


def kernel(x, norm_mix_g, w_in, b_gate, conv_w, conv_b, lru_lambda, lru_wa, lru_ba, lru_wx, lru_bx, attn_sink, w_out, norm_ffn_g, w_ffn_in, w_ffn_out, norm_final_g, loss_target, m_norm_mix_g, m_w_in, m_b_gate, m_conv_w, m_conv_b, m_lru_lambda, m_lru_wa, m_lru_ba, m_lru_wx, m_lru_bx, m_attn_sink, m_w_out, m_norm_ffn_g, m_w_ffn_in, m_w_ffn_out, m_norm_final_g, v_norm_mix_g, v_w_in, v_b_gate, v_conv_w, v_conv_b, v_lru_lambda, v_lru_wa, v_lru_ba, v_lru_wx, v_lru_bx, v_attn_sink, v_w_out, v_norm_ffn_g, v_w_ffn_in, v_w_ffn_out, v_norm_final_g):
    S = x.shape[1]
    xs = x[0]
    tgt = loss_target[0]
    cx, cy, cc = lax.axis_index("x"), lax.axis_index("y"), lax.axis_index("c")
    chip = 2 * cx + cy
    SW = D // NCHIP

    small_shard = _pack([conv_w[0], lru_lambda[0], lru_ba[0], lru_bx[0]], 32)
    halves_of = lambda a: a.reshape(2, a.shape[0] // 2, a.shape[1])
    w_in_g, small_g = _gather_chips([halves_of(w_in[0].astype(bf16)), halves_of(small_shard)])
    w_in_g = w_in_g.reshape(NCHIP, D, SHW)
    small_g = small_g.reshape(NCHIP, 32, 128)
    late = [w_ffn_in[0].astype(bf16), w_out[0].astype(bf16), w_ffn_out[0].astype(bf16)]
    late_send, late_recv, late_src, late_land, late_token = _split_start(
        "gather_late_start", 9, _gather_copies(3), late, [_sds((NCHIP,) + a.shape, bf16) for a in late], small_g)
    small_parts = [_unpack(small_g[s], [(4, SW), (2, SW), (2, SW), (2, SW)]) for s in range(NCHIP)]
    conv_w_f, lam_f, ba_f, bx_f = [jnp.concatenate([small_parts[s][p] for s in range(NCHIP)], axis=1) for p in range(4)]
    wbd = jnp.concatenate([_block_diag_pairs(lru_wa[0, 0]), _block_diag_pairs(lru_wx[0, 0]),
                           _block_diag_pairs(lru_wa[0, 1]), _block_diag_pairs(lru_wx[0, 1])], axis=2).astype(bf16)
    conv_b_f = conv_b
    sink = attn_sink

    xn, proj = _rms_matmul("rms_proj", xs, norm_mix_g + late_token[0:1, 0:1], w_in_g, 512)
    y_a, lru_state = _lru_fwd(proj, conv_w_f, conv_b_f, lam_f, ba_f, bx_f, wbd)
    y_b = _attn_fwd(proj, sink)
    late_src, late_land = _split_wait("gather_late_wait", _gather_copies(3), late_send, late_recv, late_src, late_land, y_b)
    chip_arr = chip.reshape(1).astype(jnp.int32)
    w_ffn_in_g, w_out_g, w_ffn_out_g = _place_own(chip_arr, late_src, late_land, 4)
    w_out_f = w_out_g.reshape(D, D)
    w_ffn_out_f = w_ffn_out_g.reshape(FF, D)
    merged, x1 = _merge_out_proj(proj, b_gate, y_a, y_b, w_out_f, xs, 512)
    xn2, gu, act = _rms_matmul_swiglu("rms_ffn_in", x1, norm_ffn_g, w_ffn_in_g, 512)
    dx2, loss_row, dg3 = _ffn_out_loss_bwd(act, w_ffn_out_f, x1, norm_final_g.reshape(1, D), tgt, 512)

    tk = min(2048, S)
    gw_ffn_out = _mm_tn("dw_ffn_out", act, pl.BlockSpec((tk, SHW), lambda i, k: (k, i)),
                        dx2, pl.BlockSpec((tk, D), lambda i, k: (k, 0)),
                        _sds((FF, D), f32), pl.BlockSpec((SHW, D), lambda i, k: (i, 0)), (2, S // tk), (SHW, D))
    dgu = _swiglu_bwd(dx2, w_ffn_out_f, gu, 256)
    gw_ffn_in = _mm_tn("dw_ffn_in", xn2, pl.BlockSpec((tk, D), lambda g, k: (k, 0)),
                       dgu, pl.BlockSpec((None, tk, SHW), lambda g, k: (g // 2, k, g % 2)),
                       _sds((NCHIP, D, SHW), f32), pl.BlockSpec((None, D, SHW), lambda g, k: (g, 0, 0)),
                       (NCHIP, S // tk), (D, SHW))
    c_arr = cc.reshape(1).astype(jnp.int32)
    early_names, early_tiles = ["w_ffn_in", "w_ffn_out"], [256, 352]
    early = [gw_ffn_in.reshape(NCHIP, 2, D // 2, SHW), gw_ffn_out.reshape(NCHIP, 2, FF // NCHIP // 2, D)]
    ea_send, ea_recv, ea_src, ea_land, ea_token = _split_start(
        "reduce_early_sibling_start", 2, _sibling_half_copies(2), early,
        [_sds((NCHIP,) + g.shape[2:], f32) for g in early], dgu)
    dx1, dg2 = _mm_nt_rms_bwd("dxn2_rms_bwd", dgu, w_ffn_in_g, x1, norm_ffn_g + ea_token[0:1, 0:1], dx2, 512)

    gw_out = _mm_tn("dw_out", merged, pl.BlockSpec((tk, D), lambda i, k: (k, 0)),
                    dx1, pl.BlockSpec((tk, D), lambda i, k: (k, 0)),
                    _sds((D, D), f32), pl.BlockSpec((D, D), lambda i, k: (0, 0)), (1, S // tk), (D, D))
    dproj, dy, db_gate = _merge_bwd(proj, b_gate, y_a, y_b, dx1, w_out_f, 512)
    ea_src, ea_land = _split_wait("reduce_early_sibling_wait", _sibling_half_copies(2), ea_send, ea_recv, ea_src, ea_land, dy)
    early_pairs = [_pair_sum("pair_sum_" + nm, c_arr, g4, r, th)
                   for nm, g4, r, th in zip(early_names, ea_src, ea_land, early_tiles)]
    eb_send, eb_recv, eb_src, eb_land, eb_token = _split_start(
        "reduce_early_chips_start", 6, _chip_part_copies(2), [p[1] for p in early_pairs],
        [_sds((3,) + p[1].shape[1:], bf16) for p in early_pairs], early_pairs[0][0])
    dproj, dsink = _attn_bwd(proj, sink + eb_token[0:1, 0:1], y_b, dy, dproj)
    _, eb_land = _split_wait("reduce_early_chips_wait", _chip_part_copies(2), eb_send, eb_recv, eb_src, eb_land, dsink)
    early_halves = [_chip_sum("chip_sum_" + nm, chip_arr, p[0], r3, th)
                    for nm, p, r3, th in zip(early_names, early_pairs, eb_land, early_tiles)]
    ec_send, ec_recv, ec_src, ec_land, ec_token = _split_start(
        "reduce_early_share_start", 2, _sibling_whole_copies(2), early_halves, [_sds(h.shape, f32) for h in early_halves], dsink)
    dproj, dcw, dcb, dlam, dba, dbx, dwbd = _lru_bwd(proj, dy, lru_state, dproj, conv_w_f, conv_b_f + ec_token[0:1, 0:1], lam_f,
                                                     ba_f, bx_f, wbd)
    early_halves, early_other = _split_wait("reduce_early_share_wait", _sibling_whole_copies(2), ec_send, ec_recv, ec_src, ec_land, dcb)
    gw_in = _mm_tn("dw_in", xn, pl.BlockSpec((tk, D), lambda g, k: (k, 0)),
                   dproj, pl.BlockSpec((tk, SHW), lambda g, k: (k, g)),
                   _sds((NCHIP, D, SHW), f32), pl.BlockSpec((None, D, SHW), lambda g, k: (g, 0, 0)),
                   (NCHIP, S // tk), (D, SHW))
    wa_send, wa_recv, wa_src, wa_land, wa_token = _split_start(
        "reduce_w_in_sibling_start", 1, _sibling_half_copies(1), [gw_in.reshape(NCHIP, 2, D // 2, SHW)],
        [_sds((NCHIP, D // 2, SHW), f32)], dproj)
    grad_x, dg1 = _mm_nt_rms_bwd("dxn_rms_bwd", dproj, w_in_g, xs, norm_mix_g + wa_token[0:1, 0:1], dx1, 512)
    wa_src, wa_land = _split_wait("reduce_w_in_sibling_wait", _sibling_half_copies(1), wa_send, wa_recv, wa_src, wa_land, dg1)
    w_in_pair = _pair_sum("pair_sum_w_in", c_arr, wa_src[0], wa_land[0], 256)
    wb_send, wb_recv, wb_src, wb_land, wb_token = _split_start(
        "reduce_w_in_chips_start", 3, _chip_part_copies(1), [w_in_pair[1]], [_sds((3, D // 2, SHW), bf16)], w_in_pair[0])

    d_wa = jnp.stack([_diag_blocks(dwbd[:, :, 0:CW]), _diag_blocks(dwbd[:, :, 2 * CW:3 * CW])])
    d_wx = jnp.stack([_diag_blocks(dwbd[:, :, CW:2 * CW]), _diag_blocks(dwbd[:, :, 3 * CW:4 * CW])])
    small_full = [dg1, db_gate, dcw, dcb, dlam, d_wa, dba, d_wx, dbx, dsink[:, 0], dg2, dg3,
                  loss_row[0, 0:1]]
    full_shapes = [(1, D), (1, 2 * D), (4, D), (1, D), (2, D), (2, NH, HD, HD), (2, D), (2, NH, HD, HD), (2, D), (NH,),
                   (1, D), (1, D), (1,)]
    rows_full = _rows_for([math.prod(s) for s in full_shapes], 16)
    small_vec = _pack(small_full, rows_full)

    late_names, late_tiles = ["w_in", "w_out"], [256, 128]
    big = [gw_out.reshape(NCHIP, 2, D // NCHIP // 2, D)]
    *recv_a, small_sib = _sibling_halves(big, small_vec + wb_token[0:1, 0:1])
    w_out_pair = _pair_sum("pair_sum_w_out", c_arr, big[0], recv_a[0], 128)
    small_chip = _add2("pair_sum_small", small_vec, small_sib).reshape(2, rows_full // 2, 128)
    *recv_b, small_all = _exchange_chips([w_out_pair[1]], small_chip)
    w_out_half = _chip_sum("chip_sum_w_out", chip_arr, w_out_pair[0], recv_b[0], 128)
    _, wb_land = _split_wait("reduce_w_in_chips_wait", _chip_part_copies(1), wb_send, wb_recv, wb_src, wb_land, small_all)
    w_in_half = _chip_sum("chip_sum_w_in", chip_arr, w_in_pair[0], wb_land[0], 256)
    halves = [w_in_half, w_out_half, _sum4("chip_sum_small", small_all, rows_full // 2)]
    *recv_c, small_other = _share_sibling(halves)
    small_lo = jnp.where(cc == 0, halves[2], small_other)
    small_hi = jnp.where(cc == 0, small_other, halves[2])
    g_full = _unpack(jnp.concatenate([small_lo, small_hi], axis=0), full_shapes)

    out_big = {}
    for nm, w, g_own, g_recv, m, v, th in zip(late_names + early_names, [w_in, w_out, w_ffn_in, w_ffn_out],
                                              halves[:2] + early_halves, recv_c + early_other,
                                              [m_w_in, m_w_out, m_w_ffn_in, m_w_ffn_out],
                                              [v_w_in, v_w_out, v_w_ffn_in, v_w_ffn_out], late_tiles + early_tiles):
        g_, d_, m_, v_ = _adamw_halves("adamw_" + nm, c_arr, w[0], g_own, g_recv, m[0], v[0], th)
        out_big[nm] = (g_[None], d_[None], m_[None], v_[None])

    small_names = ["norm_mix_g", "b_gate", "conv_w", "conv_b", "lru_lambda", "lru_wa", "lru_ba", "lru_wx", "lru_bx", "attn_sink",
                   "norm_ffn_g", "norm_final_g"]
    sharded = {"conv_w", "lru_lambda", "lru_ba", "lru_bx"}
    small_w = [norm_mix_g, b_gate, conv_w, conv_b, lru_lambda, lru_wa, lru_ba, lru_wx, lru_bx, attn_sink, norm_ffn_g, norm_final_g]
    small_m = [m_norm_mix_g, m_b_gate, m_conv_w, m_conv_b, m_lru_lambda, m_lru_wa, m_lru_ba, m_lru_wx, m_lru_bx, m_attn_sink,
               m_norm_ffn_g, m_norm_final_g]
    small_v = [v_norm_mix_g, v_b_gate, v_conv_w, v_conv_b, v_lru_lambda, v_lru_wa, v_lru_ba, v_lru_wx, v_lru_bx, v_attn_sink,
               v_norm_ffn_g, v_norm_final_g]
    g_local = []
    for nm, g, w in zip(small_names, g_full, small_w):
        if nm in sharded:
            g = lax.dynamic_slice_in_dim(g, chip * SW, SW, axis=1)
        g_local.append(g.reshape(w.shape))
    local_shapes = [w.shape for w in small_w]
    rows_local = _rows_for([math.prod(s) for s in local_shapes], 8)
    d_s, m_s, v_s = _adamw("adamw_small", _pack(small_w, rows_local), _pack(g_local, rows_local),
                           _pack(small_m, rows_local), _pack(small_v, rows_local), rows_local)
    d_l, m_l, v_l = _unpack(d_s, local_shapes), _unpack(m_s, local_shapes), _unpack(v_s, local_shapes)
    res = {nm: (g_local[i], d_l[i], m_l[i], v_l[i]) for i, nm in enumerate(small_names)}
    res.update(out_big)

    order = ["norm_mix_g", "w_in", "b_gate", "conv_w", "conv_b", "lru_lambda", "lru_wa", "lru_ba", "lru_wx", "lru_bx", "attn_sink",
             "w_out", "norm_ffn_g", "w_ffn_in", "w_ffn_out", "norm_final_g"]
    outs = [g_full[-1][0], grad_x[None]]
    for k in range(4):
        outs += [res[nm][k] for nm in order]
    return tuple(outs)
```

```python
import math

import jax
import jax.numpy as jnp
from jax import lax
from jax.experimental import pallas as pl
from jax.experimental.pallas import tpu as pltpu

f32 = jnp.float32
bf16 = jnp.bfloat16

D = 1024
NH = 16
HD = 64
FF = 2816
INW = 5632
NCHIP = 4
SHW = INW // NCHIP
CW = 128
NCH = D // CW
BLK = 128
EPS = 1e-6
NEG_INF = -1e30
RGLRU_C = 8.0
ADAM_LR, ADAM_B1, ADAM_B2, ADAM_EPS, ADAM_WD, ADAM_STEP = 0.001, 0.9, 0.999, 1e-08, 0.01, 10
VMEM_LIMIT = 58 * 1024 * 1024
MESH = pl.DeviceIdType.MESH
ANY = pl.BlockSpec(memory_space=pl.ANY)

COL_Q, COL_K, COL_V, COL_Z0 = 8, 12, 13, 14
MERGE_W = 512
MERGE_Z0 = (COL_Z0 * 256) // MERGE_W


def _params(n_axes, vmem=False):
    return pltpu.CompilerParams(dimension_semantics=("arbitrary",) * n_axes,
                                vmem_limit_bytes=VMEM_LIMIT if vmem else None)


def _sds(shape, dtype):
    return jax.ShapeDtypeStruct(tuple(shape), dtype)


_DIMS = {"nn": (((1,), (0,)), ((), ())), "nt": (((1,), (1,)), ((), ())), "tn": (((0,), (0,)), ((), ()))}


def _mm(name, mode, a, a_spec, b, b_spec, out_shape, out_spec, grid, nk, acc_shape):
    def body(*refs):
        a_ref, b_ref, o_ref = refs[0], refs[1], refs[2]
        part = lax.dot_general(a_ref[...].astype(bf16), b_ref[...].astype(bf16), _DIMS[mode],
                               preferred_element_type=f32)
        if nk == 1:
            o_ref[...] = part.astype(o_ref.dtype)
            return
        acc_ref = refs[3]
        k = pl.program_id(len(grid) - 1)

        @pl.when(k == 0)
        def _():
            acc_ref[...] = part

        @pl.when(k > 0)
        def _():
            acc_ref[...] += part

        @pl.when(k == nk - 1)
        def _():
            o_ref[...] = acc_ref[...].astype(o_ref.dtype)

    scratch = [pltpu.VMEM(acc_shape, f32)] if nk > 1 else []
    return pl.pallas_call(body, name=name, grid=grid, in_specs=[a_spec, b_spec], out_specs=out_spec, out_shape=out_shape,
                          scratch_shapes=scratch, compiler_params=_params(len(grid), True))(a, b)


def _rms_matmul(name, x, g, w3, tm):
    S, K = x.shape
    G, _, Nw = w3.shape
    tm = min(tm, S)

    def body(x_ref, g_ref, w_ref, xn_ref, o_ref):
        xf = x_ref[...]
        r = lax.rsqrt(jnp.mean(xf * xf, axis=-1, keepdims=True) + EPS)
        xn = ((xf * r) * g_ref[...]).astype(bf16)
        xn_ref[...] = xn
        for j in range(G):
            o_ref[:, j * Nw:(j + 1) * Nw] = jnp.dot(xn, w_ref[j], preferred_element_type=f32).astype(bf16)

    return pl.pallas_call(
        body, name=name, grid=(S // tm,),
        in_specs=[pl.BlockSpec((tm, K), lambda i: (i, 0)), pl.BlockSpec((1, K), lambda i: (0, 0)),
                  pl.BlockSpec((G, K, Nw), lambda i: (0, 0, 0))],
        out_specs=[pl.BlockSpec((tm, K), lambda i: (i, 0)), pl.BlockSpec((tm, G * Nw), lambda i: (i, 0))],
        out_shape=[_sds((S, K), bf16), _sds((S, G * Nw), bf16)],
        compiler_params=_params(1, True))(x, g, w3)


def _rms_matmul_swiglu(name, x, g, w3, tm):
    S, K = x.shape
    G, _, Nw = w3.shape
    tm = min(tm, S)
    half = G // 2

    def body(x_ref, g_ref, w_ref, xn_ref, gu_ref, act_ref):
        xf = x_ref[...]
        r = lax.rsqrt(jnp.mean(xf * xf, axis=-1, keepdims=True) + EPS)
        xn = ((xf * r) * g_ref[...]).astype(bf16)
        xn_ref[...] = xn
        for j in range(half):
            cols = slice(j * Nw, (j + 1) * Nw)
            gate = jnp.dot(xn, w_ref[j], preferred_element_type=f32)
            up = jnp.dot(xn, w_ref[half + j], preferred_element_type=f32)
            gu_ref[0, :, cols] = gate.astype(bf16)
            gu_ref[1, :, cols] = up.astype(bf16)
            act_ref[:, cols] = ((gate * _sigmoid(gate)) * up).astype(bf16)

    return pl.pallas_call(
        body, name=name, grid=(S // tm,),
        in_specs=[pl.BlockSpec((tm, K), lambda i: (i, 0)), pl.BlockSpec((1, K), lambda i: (0, 0)),
                  pl.BlockSpec((G, K, Nw), lambda i: (0, 0, 0))],
        out_specs=[pl.BlockSpec((tm, K), lambda i: (i, 0)), pl.BlockSpec((2, tm, half * Nw), lambda i: (0, i, 0)),
                   pl.BlockSpec((tm, half * Nw), lambda i: (i, 0))],
        out_shape=[_sds((S, K), bf16), _sds((2, S, half * Nw), bf16), _sds((S, half * Nw), bf16)],
        compiler_params=_params(1, True))(x, g, w3)


def _mm_tn(name, a, a_spec, b, b_spec, out_shape, out_spec, grid, acc_shape):
    return _mm(name, "tn", a, a_spec, b, b_spec, out_shape, out_spec, grid, grid[-1], acc_shape)


def _sigmoid(x):
    return 0.5 * jnp.tanh(0.5 * x) + 0.5


_GELU_C = math.sqrt(2.0 / math.pi)


def _gelu_and_grad(x):
    v = _GELU_C * (x + 0.044715 * (x * x * x))
    t = jnp.tanh(v)
    gl = 0.5 * x * (1.0 + t)
    dgl = 0.5 * (1.0 + t) + 0.5 * x * (1.0 - t * t) * (_GELU_C * (1.0 + 3.0 * 0.044715 * (x * x)))
    return gl, dgl


def _one_minus_exp2x(x, ex):
    y = 2.0 * x
    series = y * (1.0 + y * (0.5 + y * (1.0 / 6.0 + y * (1.0 / 24.0))))
    return jnp.where(y > -1.0 / 64.0, -series, 1.0 - ex * ex)


def _z_specs(tm):
    return [pl.BlockSpec((tm, MERGE_W), lambda i, p=p: (i, MERGE_Z0 + p)) for p in range(2 * D // MERGE_W)]


def _merge_out_proj(proj, b_gate, y_a, y_b, w, res, tm):
    S = proj.shape[0]
    tm = min(tm, S)
    per = D // MERGE_W
    nz = 2 * per

    def body(*refs):
        z = refs[:nz]
        b_ref, ya_ref, yb_ref, w_ref, r_ref, m_ref, x_ref = refs[nz:]
        for p in range(per):
            cols = slice(p * MERGE_W, (p + 1) * MERGE_W)
            g0 = _sigmoid(z[p][...].astype(f32) + b_ref[:, p * MERGE_W:(p + 1) * MERGE_W])
            g1 = _sigmoid(z[per + p][...].astype(f32) + b_ref[:, D + p * MERGE_W:D + (p + 1) * MERGE_W])
            m_ref[:, cols] = (g0 * ya_ref[:, cols].astype(f32) + g1 * yb_ref[:, cols].astype(f32)).astype(bf16)
        x_ref[...] = r_ref[...] + jnp.dot(m_ref[...], w_ref[...], preferred_element_type=f32)

    row = pl.BlockSpec((tm, D), lambda i: (i, 0))
    return pl.pallas_call(
        body, name="merge_out_proj", grid=(S // tm,),
        in_specs=_z_specs(tm) + [pl.BlockSpec((1, 2 * D), lambda i: (0, 0)), row, row, pl.BlockSpec((D, D), lambda i: (0, 0)), row],
        out_specs=[row, row], out_shape=[_sds((S, D), bf16), _sds((S, D), f32)],
        compiler_params=_params(1, True))(*([proj] * nz), b_gate, y_a, y_b, w, res)


def _merge_bwd(proj, b_gate, y_a, y_b, dx, w, tm):
    S = proj.shape[0]
    tm = min(tm, S)
    per = D // MERGE_W
    nz = 2 * per
    nsteps = S // tm
    z_col = MERGE_Z0 * MERGE_W

    def body(*refs):
        z = refs[:nz]
        b_ref, ya_ref, yb_ref, dx_ref, w_ref, dproj_ref, dy_ref, db_ref, dz_buf, sems = refs[nz:]
        i = pl.program_id(0)
        slot = i % 2

        def dz_copy(step):
            rows = pl.ds(pl.multiple_of(step * tm, tm), tm)
            return pltpu.make_async_copy(dz_buf.at[step % 2], dproj_ref.at[rows, pl.ds(z_col, 2 * D)], sems.at[step % 2])

        @pl.when(i >= 2)
        def _():
            dz_copy(i - 2).wait()

        @pl.when(i == 0)
        def _():
            db_ref[...] = jnp.zeros_like(db_ref)

        dm = lax.dot_general(dx_ref[...].astype(bf16), w_ref[...], _DIMS["nt"], preferred_element_type=f32)
        for p in range(nz):
            branch, cols = p // per, slice((p % per) * MERGE_W, (p % per + 1) * MERGE_W)
            zc = slice(p * MERGE_W, (p + 1) * MERGE_W)
            g = _sigmoid(z[p][...].astype(f32) + b_ref[:, zc])
            d = dm[:, cols]
            y = (ya_ref if branch == 0 else yb_ref)[:, cols].astype(f32)
            dz = (d * y) * (g * (1.0 - g))
            dz_buf[slot, :, zc] = dz.astype(bf16)
            dy_ref[branch, :, cols] = (d * g).astype(bf16)
            db_ref[:, zc] += jnp.sum(dz, axis=0, keepdims=True)
        dz_copy(i).start()

        @pl.when(i == nsteps - 1)
        def _():
            if nsteps >= 2:
                dz_copy(i - 1).wait()
            dz_copy(i).wait()

    row = pl.BlockSpec((tm, D), lambda i: (i, 0))
    return pl.pallas_call(
        body, name="merge_bwd", grid=(nsteps,),
        in_specs=_z_specs(tm) + [pl.BlockSpec((1, 2 * D), lambda i: (0, 0)), row, row, row, pl.BlockSpec((D, D), lambda i: (0, 0))],
        out_specs=[ANY, pl.BlockSpec((2, tm, D), lambda i: (0, i, 0)), pl.BlockSpec((1, 2 * D), lambda i: (0, 0))],
        out_shape=[_sds((S, INW), bf16), _sds((2, S, D), bf16), _sds((1, 2 * D), f32)],
        scratch_shapes=[pltpu.VMEM((2, tm, 2 * D), bf16), pltpu.SemaphoreType.DMA((2,))],
        compiler_params=_params(1, True))(*([proj] * nz), b_gate, y_a, y_b, dx, w)


def _swiglu_bwd(dx, w, gu, tm):
    S, K = dx.shape
    tm = min(tm, S)

    def body(dx_ref, w_ref, gu_ref, o_ref):
        d = lax.dot_general(dx_ref[...].astype(bf16), w_ref[...], _DIMS["nt"], preferred_element_type=f32)
        g = gu_ref[0].astype(f32)
        u = gu_ref[1].astype(f32)
        s = _sigmoid(g)
        o_ref[0] = ((d * u) * (s * (1.0 + g * (1.0 - s)))).astype(bf16)
        o_ref[1] = (d * (g * s)).astype(bf16)

    stacked = pl.BlockSpec((2, tm, FF), lambda i: (0, i, 0))
    return pl.pallas_call(body, name="swiglu_bwd", grid=(S // tm,),
                          in_specs=[pl.BlockSpec((tm, K), lambda i: (i, 0)), pl.BlockSpec((FF, K), lambda i: (0, 0)), stacked],
                          out_specs=stacked, out_shape=_sds((2, S, FF), bf16),
                          compiler_params=_params(1, True))(dx, w, gu)


def _ffn_out_loss_bwd(act, w, x1, g3, tgt, tm):
    S, K = act.shape
    tm = min(tm, S)

    def body(a_ref, w_ref, r_ref, g_ref, t_ref, dx_ref, loss_ref, dg_ref):
        @pl.when(pl.program_id(0) == 0)
        def _():
            loss_ref[...] = jnp.zeros_like(loss_ref)
            dg_ref[...] = jnp.zeros_like(dg_ref)

        x = r_ref[...] + jnp.dot(a_ref[...], w_ref[...], preferred_element_type=f32)
        g = g_ref[...]
        r = lax.rsqrt(jnp.mean(x * x, axis=-1, keepdims=True) + EPS)
        xh = x * r
        err = xh * g - t_ref[...]
        row = jnp.mean(err * err, axis=-1, keepdims=True)
        loss_ref[...] += 0.5 * jnp.sum(row, axis=0, keepdims=True)
        dy = err * (1.0 / D)
        dg_ref[...] += jnp.sum(dy * xh, axis=0, keepdims=True)
        dxh = dy * g
        dx_ref[...] = r * (dxh - xh * jnp.mean(dxh * xh, axis=-1, keepdims=True))

    row_blk = pl.BlockSpec((tm, D), lambda i: (i, 0))
    vec = pl.BlockSpec((1, D), lambda i: (0, 0))
    return pl.pallas_call(body, name="ffn_out_loss_bwd", grid=(S // tm,),
                          in_specs=[pl.BlockSpec((tm, K), lambda i: (i, 0)), pl.BlockSpec((K, D), lambda i: (0, 0)),
                                    row_blk, vec, row_blk],
                          out_specs=[row_blk, pl.BlockSpec((1, 128), lambda i: (0, 0)), vec],
                          out_shape=[_sds((S, D), f32), _sds((1, 128), f32), _sds((1, D), f32)],
                          compiler_params=_params(1, True))(act, w, x1, g3, tgt)


def _mm_nt_rms_bwd(name, a, w3, x, g, dres, tm):
    S = x.shape[0]
    G, Dout, Kw = w3.shape
    tm = min(tm, S)
    planes = a.shape[0] if a.ndim == 3 else 1
    per = G // planes

    def body(a_ref, w_ref, x_ref, g_ref, r_ref, dx_ref, dg_ref):
        @pl.when(pl.program_id(0) == 0)
        def _():
            dg_ref[...] = jnp.zeros_like(dg_ref)

        d = None
        for k in range(G):
            cols = slice((k % per) * Kw, (k % per + 1) * Kw)
            a_k = a_ref[k // per, :, cols] if a.ndim == 3 else a_ref[:, cols]
            part = lax.dot_general(a_k, w_ref[k], _DIMS["nt"], preferred_element_type=f32)
            d = part if d is None else d + part
        x_t = x_ref[...]
        r = lax.rsqrt(jnp.mean(x_t * x_t, axis=-1, keepdims=True) + EPS)
        xh = x_t * r
        dg_ref[...] += jnp.sum(d * xh, axis=0, keepdims=True)
        dxh = d * g_ref[...]
        dx_ref[...] = r_ref[...] + r * (dxh - xh * jnp.mean(dxh * xh, axis=-1, keepdims=True))

    row_blk = pl.BlockSpec((tm, Dout), lambda i: (i, 0))
    vec = pl.BlockSpec((1, Dout), lambda i: (0, 0))
    a_spec = (pl.BlockSpec((planes, tm, per * Kw), lambda i: (0, i, 0)) if a.ndim == 3
              else pl.BlockSpec((tm, G * Kw), lambda i: (i, 0)))
    return pl.pallas_call(body, name=name, grid=(S // tm,),
                          in_specs=[a_spec, pl.BlockSpec((G, Dout, Kw), lambda i: (0, 0, 0)), row_blk, vec, row_blk],
                          out_specs=[row_blk, vec], out_shape=[_sds((S, Dout), f32), _sds((1, Dout), f32)],
                          compiler_params=_params(1, True))(a, w3, x, g, dres)


LRU_TT = 512
SCAN_UNROLL = 16


HALO = 16


def _halo(ref, i, S):
    nt = S // LRU_TT
    t0 = pl.multiple_of(i * LRU_TT, LRU_TT)
    p0 = pl.multiple_of(jnp.maximum(t0 - HALO, 0), HALO)
    n0 = pl.multiple_of(jnp.minimum(t0 + LRU_TT, S - HALO), HALO)
    prev = jnp.where(i > 0, ref[pl.ds(p0, HALO), :].astype(f32), 0.0)
    nxt = jnp.where(i < nt - 1, ref[pl.ds(n0, HALO), :].astype(f32), 0.0)
    return jnp.concatenate([prev, ref[pl.ds(t0, LRU_TT), :].astype(f32), nxt], axis=0)


def _shift(ext, k):
    n = LRU_TT + 2 * HALO
    return pltpu.roll(ext, (-k) % n, 0)[HALO:HALO + LRU_TT]


def _lru_gates(uc, wbd, ba, bx):
    pre = jnp.dot(uc.astype(bf16), wbd, preferred_element_type=f32)
    r_f = _sigmoid(pre[:, 0:CW] + ba[0:1])
    i_f = _sigmoid(pre[:, CW:2 * CW] + bx[0:1])
    r_b = _sigmoid(pre[:, 2 * CW:3 * CW] + ba[1:2])
    i_b = _sigmoid(pre[:, 3 * CW:4 * CW] + bx[1:2])
    return r_f, i_f, r_b, i_b


def _lru_coeffs(r, sp):
    log_a = (-RGLRU_C * r) * sp
    a = jnp.exp(log_a)
    beta = jnp.sqrt(jnp.maximum(_one_minus_exp2x(log_a, a), 0.0))
    return a, beta


def _lru_coeffs_inv(r, sp):
    log_a = (-RGLRU_C * r) * sp
    a = jnp.exp(log_a)
    om = jnp.maximum(_one_minus_exp2x(log_a, a), 0.0)
    return a, jnp.sqrt(om), lax.rsqrt(jnp.maximum(om, 1e-30))


def _conv_tile(u_ref, i, S, cw, cb):
    ext = _halo(u_ref, i, S)
    um2, um1, u0, up1 = _shift(ext, -2), _shift(ext, -1), ext[HALO:HALO + LRU_TT], _shift(ext, 1)
    uc = um2 * cw[0:1] + um1 * cw[1:2] + u0 * cw[2:3] + up1 * cw[3:4] + cb
    return uc, (um2, um1, u0, up1)


def _scan_pair(S, fwd_a, fwd_b, fwd_out, rev_a, rev_b, rev_out):
    ng = S // 8
    idx = lax.broadcasted_iota(jnp.int32, (8, CW), 0)

    def local(a, b, rev):
        for sh in (1, 2, 4):
            if rev:
                keep = idx < 8 - sh
                amt = 8 - sh
            else:
                keep = idx >= sh
                amt = sh
            a_s = jnp.where(keep, pltpu.roll(a, amt, 0), 1.0)
            b_s = jnp.where(keep, pltpu.roll(b, amt, 0), 0.0)
            b = a * b_s + b
            a = a * a_s
        return a, b

    def step(it, carry):
        cf, cr = carry
        fwd_rows = [pl.multiple_of((it * SCAN_UNROLL + j) * 8, 8) for j in range(SCAN_UNROLL)]
        rev_rows = [pl.multiple_of((ng - 1 - (it * SCAN_UNROLL + j)) * 8, 8) for j in range(SCAN_UNROLL)]
        fwd_loc = [local(fwd_a(r), fwd_b(r), False) for r in fwd_rows]
        rev_loc = [local(rev_a(r), rev_b(r), True) for r in rev_rows]
        for j in range(SCAN_UNROLL):
            a, b = fwd_loc[j]
            h = a * cf + b
            fwd_out[pl.ds(fwd_rows[j], 8), :] = h
            cf = jnp.broadcast_to(h[7:8, :], (8, CW))
            a, b = rev_loc[j]
            h = a * cr + b
            rev_out[pl.ds(rev_rows[j], 8), :] = h
            cr = jnp.broadcast_to(h[0:1, :], (8, CW))
        return cf, cr

    zero = jnp.zeros((8, CW), f32)
    lax.fori_loop(0, ng // SCAN_UNROLL, step, (zero, zero))


def _lru_specs(S):
    seq = lambda off: pl.BlockSpec((S, CW), lambda j: (0, off + j))
    par = lambda rows: pl.BlockSpec((rows, CW), lambda j: (0, j))
    return seq, par


def _lru_fwd(proj, conv_w, conv_b, lam, ba, bx, wbd):
    S = proj.shape[0]
    nt = S // LRU_TT

    def body(u_ref, g_ref, cw_ref, cb_ref, lam_ref, ba_ref, bx_ref, wbd_ref, y_ref, state_ref, af_ref, bf_ref, ab_ref, bb_ref,
             sems):
        cw, cb, ba_v, bx_v, wbd_v = cw_ref[...], cb_ref[...], ba_ref[...], bx_ref[...], wbd_ref[...]
        sp = jax.nn.softplus(-lam_ref[...])
        cols = pl.ds(pl.multiple_of(pl.program_id(0) * CW, CW), CW)
        save = [pltpu.make_async_copy(ref, state_ref.at[k, :, cols], sems.at[k])
                for k, ref in enumerate((af_ref, bf_ref, ab_ref, bb_ref))]

        def phase1(i, c):
            uc, _ = _conv_tile(u_ref, i, S, cw, cb)
            r_f, i_f, r_b, i_b = _lru_gates(uc, wbd_v, ba_v, bx_v)
            rows = pl.ds(pl.multiple_of(i * LRU_TT, LRU_TT), LRU_TT)
            a, beta = _lru_coeffs(r_f, sp[0:1])
            af_ref[rows, :] = a
            bf_ref[rows, :] = beta * (i_f * uc)
            a, beta = _lru_coeffs(r_b, sp[1:2])
            ab_ref[rows, :] = a
            bb_ref[rows, :] = beta * (i_b * uc)
            return c

        lax.fori_loop(0, nt, phase1, 0)
        save[0].start()
        save[2].start()
        row8 = lambda ref: (lambda r0: ref[pl.ds(r0, 8), :])
        _scan_pair(S, row8(af_ref), row8(bf_ref), bf_ref, row8(ab_ref), row8(bb_ref), bb_ref)
        save[1].start()
        save[3].start()

        def phase3(i, c):
            rows = pl.ds(pl.multiple_of(i * LRU_TT, LRU_TT), LRU_TT)
            y = (bf_ref[rows, :] + bb_ref[rows, :]) * jax.nn.gelu(g_ref[rows, :].astype(f32))
            y_ref[rows, :] = y.astype(y_ref.dtype)
            return c

        lax.fori_loop(0, nt, phase3, 0)
        for cp in save:
            cp.wait()

    seq, par = _lru_specs(S)
    return pl.pallas_call(
        body, name="lru_fwd", grid=(NCH,),
        in_specs=[seq(0), seq(NCH), par(4), par(1), par(2), par(2), par(2),
                  pl.BlockSpec((None, CW, 4 * CW), lambda j: (j, 0, 0))],
        out_specs=[seq(0), ANY], out_shape=[_sds((S, D), bf16), _sds((4, S, D), f32)],
        scratch_shapes=[pltpu.VMEM((S, CW), f32)] * 4 + [pltpu.SemaphoreType.DMA((4,))], compiler_params=_params(1, True),
    )(proj, proj, conv_w, conv_b, lam, ba, bx, wbd)


def _lru_bwd(proj, dy, state, dproj, conv_w, conv_b, lam, ba, bx, wbd):
    S = proj.shape[0]
    nt = S // LRU_TT

    def body(u_ref, g_ref, dy_ref, state_ref, dproj_in, cw_ref, cb_ref, lam_ref, ba_ref, bx_ref, wbd_ref,
             dproj_ref, dcw_ref, dcb_ref, dlam_ref, dba_ref, dbx_ref, dwbd_ref,
             af_ref, hf2_ref, ab_ref, hb2_ref, dh_ref, du_ref, dg_ref, sems):
        cw, cb, ba_v, bx_v, wbd_v = cw_ref[...], cb_ref[...], ba_ref[...], bx_ref[...], wbd_ref[...]
        lam_v = lam_ref[...]
        sp = jax.nn.softplus(-lam_v)
        chunk = pl.program_id(0)
        slot = chunk % 2
        bf_ref, bb_ref = hf2_ref.at[slot], hb2_ref.at[slot]

        def out_copies(j):
            c0 = pl.multiple_of(j * CW, CW)
            return [pltpu.make_async_copy(du_ref, dproj_ref.at[:, pl.ds(c0, CW)], sems.at[4]),
                    pltpu.make_async_copy(dg_ref, dproj_ref.at[:, pl.ds(D + c0, CW)], sems.at[5])]

        @pl.when(chunk >= 1)
        def _():
            for cp in out_copies(chunk - 1):
                cp.wait()

        def state_copy(k, j, dst, sem):
            return pltpu.make_async_copy(state_ref.at[k, :, pl.ds(pl.multiple_of(j * CW, CW), CW)], dst, sem)

        def hidden_loads(j):
            return [state_copy(1, j, hf2_ref.at[j % 2], sems.at[6 + j % 2]), state_copy(3, j, hb2_ref.at[j % 2], sems.at[8 + j % 2])]

        load = [state_copy(0, chunk, af_ref, sems.at[0]), None, state_copy(2, chunk, ab_ref, sems.at[2])]

        @pl.when(chunk == 0)
        def _():
            for cp in hidden_loads(chunk):
                cp.start()

        load[0].start()
        load[2].start()

        @pl.when(chunk + 1 < NCH)
        def _():
            for cp in hidden_loads(chunk + 1):
                cp.start()

        for cp in hidden_loads(chunk):
            cp.wait()
        row8 = lambda ref: (lambda r0: ref[pl.ds(r0, 8), :])

        def phase0(i, c):
            rows = pl.ds(pl.multiple_of(i * LRU_TT, LRU_TT), LRU_TT)
            gl, dgl = _gelu_and_grad(g_ref[rows, :].astype(f32))
            dyt = dy_ref[rows, :].astype(f32)
            dh_ref[rows, :] = dyt * gl
            dg_ref[rows, :] = ((dyt * (bf_ref[rows, :] + bb_ref[rows, :])) * dgl).astype(dg_ref.dtype)
            return c

        lax.fori_loop(0, nt, phase0, 0)
        load[0].wait()
        load[2].wait()

        def scaled_dh(a_ref):
            def f(r0):
                return a_ref[pl.ds(r0, 8), :] * dh_ref[pl.ds(r0, 8), :]
            return f

        _scan_pair(S, row8(ab_ref), scaled_dh(ab_ref), ab_ref, row8(af_ref), scaled_dh(af_ref), af_ref)

        dcw_ref[...] = jnp.zeros_like(dcw_ref)
        dcb_ref[...] = jnp.zeros_like(dcb_ref)
        dlam_ref[...] = jnp.zeros_like(dlam_ref)
        dba_ref[...] = jnp.zeros_like(dba_ref)
        dbx_ref[...] = jnp.zeros_like(dbx_ref)
        dwbd_ref[...] = jnp.zeros_like(dwbd_ref)

        def direction(uc, r, i_g, dht, h_nb, sp_d):
            a, beta, inv_beta = _lru_coeffs_inv(r, sp_d)
            da = dht * h_nb
            dbeta = dht * (i_g * uc)
            d_iu = dht * beta
            dlog_a = da * a - (a * a) * (dbeta * inv_beta)
            dlr = dlog_a * r
            dsp = -RGLRU_C * jnp.sum(dlr, axis=0, keepdims=True)
            dpre_r = (dlr * (1.0 - r)) * (-RGLRU_C * sp_d)
            dpre_i = (d_iu * uc) * (i_g * (1.0 - i_g))
            return dpre_r, dpre_i, d_iu * i_g, dsp

        def phase4(i, c):
            uc, (um2, um1, u0, up1) = _conv_tile(u_ref, i, S, cw, cb)
            r_f, i_f, r_b, i_b = _lru_gates(uc, wbd_v, ba_v, bx_v)
            rows = pl.ds(pl.multiple_of(i * LRU_TT, LRU_TT), LRU_TT)
            dh = dh_ref[rows, :]
            dht_f = dh + _shift(_halo(af_ref, i, S), 1)
            h_prev = _shift(_halo(bf_ref, i, S), -1)
            dht_b = dh + _shift(_halo(ab_ref, i, S), -1)
            h_next = _shift(_halo(bb_ref, i, S), 1)
            prf, pif, duc_f, dsp_f = direction(uc, r_f, i_f, dht_f, h_prev, sp[0:1])
            prb, pib, duc_b, dsp_b = direction(uc, r_b, i_b, dht_b, h_next, sp[1:2])
            dpre = jnp.concatenate([prf, pif, prb, pib], axis=1)
            dpre_b = dpre.astype(bf16)
            duc = (duc_f + duc_b) + lax.dot_general(dpre_b, wbd_v, _DIMS["nt"], preferred_element_type=f32)
            dwbd_ref[...] += lax.dot_general(uc.astype(bf16), dpre_b, _DIMS["tn"], preferred_element_type=f32)
            colsum = lambda v: jnp.sum(v, axis=0, keepdims=True)
            dba_ref[...] += jnp.concatenate([colsum(prf), colsum(prb)], axis=0)
            dbx_ref[...] += jnp.concatenate([colsum(pif), colsum(pib)], axis=0)
            dlam_ref[...] += jnp.concatenate([dsp_f, dsp_b], axis=0)
            dcb_ref[...] += colsum(duc)
            dcw_ref[...] += jnp.concatenate([colsum(duc * um2), colsum(duc * um1), colsum(duc * u0),
                                             colsum(duc * up1)], axis=0)
            af_ref[rows, :] = duc
            return c

        lax.fori_loop(0, nt, phase4, 0)
        dlam_ref[...] = dlam_ref[...] * (-_sigmoid(-lam_v))

        def phase5(i, c):
            ext = _halo(af_ref, i, S)
            rows = pl.ds(pl.multiple_of(i * LRU_TT, LRU_TT), LRU_TT)
            du = (_shift(ext, 2) * cw[0:1] + _shift(ext, 1) * cw[1:2] + ext[HALO:HALO + LRU_TT] * cw[2:3]
                  + _shift(ext, -1) * cw[3:4])
            du_ref[rows, :] = du.astype(du_ref.dtype)
            return c

        lax.fori_loop(0, nt, phase5, 0)
        for cp in out_copies(chunk):
            cp.start()

        @pl.when(chunk == NCH - 1)
        def _():
            for cp in out_copies(chunk):
                cp.wait()

    seq, par = _lru_specs(S)
    return pl.pallas_call(
        body, name="lru_bwd", grid=(NCH,),
        in_specs=[seq(0), seq(NCH), pl.BlockSpec((None, S, CW), lambda j: (0, 0, j)), ANY, ANY,
                  par(4), par(1), par(2), par(2), par(2), pl.BlockSpec((None, CW, 4 * CW), lambda j: (j, 0, 0))],
        out_specs=[ANY, par(4), par(1), par(2), par(2), par(2),
                   pl.BlockSpec((None, CW, 4 * CW), lambda j: (j, 0, 0))],
        out_shape=[_sds(dproj.shape, bf16), _sds((4, D), f32), _sds((1, D), f32), _sds((2, D), f32),
                   _sds((2, D), f32), _sds((2, D), f32), _sds((NCH, CW, 4 * CW), f32)],
        scratch_shapes=[pltpu.VMEM((S, CW), f32), pltpu.VMEM((2, S, CW), f32), pltpu.VMEM((S, CW), f32), pltpu.VMEM((2, S, CW), f32),
                        pltpu.VMEM((S, CW), f32), pltpu.VMEM((S, CW), bf16), pltpu.VMEM((S, CW), bf16),
                        pltpu.SemaphoreType.DMA((10,))],
        input_output_aliases={4: 0}, compiler_params=_params(1, True),
    )(proj, proj, dy, state, dproj, conv_w, conv_b, lam, ba, bx, wbd)


_SLOPES = [2.0 ** (-8.0 * (h + 1) / NH) for h in range(NH)]


def _half_mask(shape, e):
    lane = lax.broadcasted_iota(jnp.int32, shape, 1)
    return (lane < HD) if e == 0 else (lane >= HD)


def _both_halves(x, src):
    return jnp.where(_half_mask(x.shape, src), x, pltpu.roll(x, HD, 1))


def _attn_base(n, S):
    tq = lax.broadcasted_iota(jnp.int32, (BLK, 3 * BLK), 0)
    sk = lax.broadcasted_iota(jnp.int32, (BLK, 3 * BLK), 1)
    dist = jnp.abs(tq + BLK - sk)
    kpos = n * BLK - BLK + sk
    valid = (dist <= BLK) & (kpos >= 0) & (kpos < S)
    return jnp.where(valid, -dist.astype(f32), NEG_INF)


def _group_heads(ref, kvh, scale):
    parts = []
    for i in range(4):
        pair = 2 * kvh + i // 2
        x = ref[:, pair * 128:(pair + 1) * 128].astype(f32)
        parts.append(jnp.where(_half_mask(x.shape, i % 2), x * scale, 0.0))
    return parts


def _stack_bf16(parts):
    return jnp.concatenate([p.astype(bf16) for p in parts], axis=0)


def _attn_softmax(s_raw, base, slope, sink):
    s = s_raw + slope * base
    m = jnp.maximum(jnp.max(s, axis=-1, keepdims=True), sink)
    p = jnp.exp(s - m)
    esink = jnp.exp(sink - m)
    inv = 1.0 / (jnp.sum(p, axis=-1, keepdims=True) + esink)
    return p, inv, esink * inv


def _attn_specs(S):
    nb = S // BLK
    q_spec = pl.BlockSpec((BLK, D), lambda n: (n, 2))
    kv = lambda col: [pl.BlockSpec((BLK, 256), lambda n: (jnp.maximum(n - 1, 0), col)),
                      pl.BlockSpec((BLK, 256), lambda n: (n, col)),
                      pl.BlockSpec((BLK, 256), lambda n: (jnp.minimum(n + 1, nb - 1), col))]
    return nb, q_spec, kv(COL_K), kv(COL_V)


def _attn_fwd(proj, sink):
    S = proj.shape[0]
    nb, q_spec, k_specs, v_specs = _attn_specs(S)

    def body(sink_ref, q_ref, kp_ref, kc_ref, kn_ref, vp_ref, vc_ref, vn_ref, o_ref):
        base = _attn_base(pl.program_id(0), S)
        kcat = jnp.concatenate([kp_ref[...], kc_ref[...], kn_ref[...]], axis=0).astype(f32)
        vcat = jnp.concatenate([vp_ref[...], vc_ref[...], vn_ref[...]], axis=0).astype(f32)
        even = _half_mask((BLK, 128), 0)
        for kvh in range(NH // 4):
            ch, off = kvh // 2, kvh % 2
            kb = _both_halves(kcat[:, ch * 128:(ch + 1) * 128], off).astype(bf16)
            vb = _both_halves(vcat[:, ch * 128:(ch + 1) * 128], off).astype(bf16)
            q4 = _stack_bf16(_group_heads(q_ref, kvh, HD ** -0.5))
            s4 = lax.dot_general(q4, kb, _DIMS["nt"], preferred_element_type=f32)
            ps, invs = [], []
            for i in range(4):
                h = 4 * kvh + i
                p, inv, _ = _attn_softmax(s4[i * BLK:(i + 1) * BLK], base, _SLOPES[h], sink_ref[0, h])
                ps.append(p)
                invs.append(inv)
            o4 = jnp.dot(_stack_bf16(ps), vb, preferred_element_type=f32)
            for pr in range(2):
                lo = o4[(2 * pr) * BLK:(2 * pr + 1) * BLK] * invs[2 * pr]
                hi = o4[(2 * pr + 1) * BLK:(2 * pr + 2) * BLK] * invs[2 * pr + 1]
                pair = 2 * kvh + pr
                o_ref[:, pair * 128:(pair + 1) * 128] = jnp.where(even, lo, hi).astype(o_ref.dtype)

    return pl.pallas_call(
        body, name="attn_fwd", grid=(nb,),
        in_specs=[pl.BlockSpec(memory_space=pltpu.SMEM), q_spec] + k_specs + v_specs,
        out_specs=pl.BlockSpec((BLK, D), lambda n: (n, 0)), out_shape=_sds((S, D), bf16),
        compiler_params=_params(1, True))(sink, proj, proj, proj, proj, proj, proj, proj)


def _attn_bwd(proj, sink, y_b, dy, dproj):
    S = proj.shape[0]
    nb, q_spec, k_specs, v_specs = _attn_specs(S)
    q_col, kv_col = COL_Q * 256, COL_K * 256

    def body(sink_ref, q_ref, kp_ref, kc_ref, kn_ref, vp_ref, vc_ref, vn_ref, o_ref, do_ref, dproj_in,
             dproj_ref, dsink_ref, dk_ref, dv_ref, dq_buf, kv_buf, sems):
        n = pl.program_id(0)
        slot = n % 2
        dq_ref = dq_buf.at[slot]

        def dq_copy(step):
            rows = pl.ds(pl.multiple_of(step * BLK, BLK), BLK)
            return pltpu.make_async_copy(dq_buf.at[step % 2], dproj_ref.at[rows, pl.ds(q_col, D)], sems.at[step % 2])

        @pl.when(n >= 2)
        def _():
            dq_copy(n - 2).wait()

        @pl.when(n == 0)
        def _():
            dk_ref[...] = jnp.zeros_like(dk_ref)
            dv_ref[...] = jnp.zeros_like(dv_ref)
            dsink_ref[...] = jnp.zeros_like(dsink_ref)

        base = _attn_base(n, S)
        kcat = jnp.concatenate([kp_ref[...], kc_ref[...], kn_ref[...]], axis=0).astype(f32)
        vcat = jnp.concatenate([vp_ref[...], vc_ref[...], vn_ref[...]], axis=0).astype(f32)
        dk_rows, dv_rows = [[], []], [[], []]
        scale = HD ** -0.5
        even = _half_mask((BLK, 128), 0)
        for kvh in range(NH // 4):
            ch, off = kvh // 2, kvh % 2
            kb = _both_halves(kcat[:, ch * 128:(ch + 1) * 128], off).astype(bf16)
            vb = _both_halves(vcat[:, ch * 128:(ch + 1) * 128], off).astype(bf16)
            q_parts = _group_heads(q_ref, kvh, scale)
            d_parts = _group_heads(do_ref, kvh, 1.0)
            s4 = lax.dot_general(_stack_bf16(q_parts), kb, _DIMS["nt"], preferred_element_type=f32)
            dp4 = lax.dot_general(_stack_bf16(d_parts), vb, _DIMS["nt"], preferred_element_type=f32)
            ts, ps, qn, dn, invs = [], [], [], [], []
            for i in range(4):
                h = 4 * kvh + i
                pair = 2 * kvh + i // 2
                rows = slice(i * BLK, (i + 1) * BLK)
                p, inv, psink = _attn_softmax(s4[rows], base, _SLOPES[h], sink_ref[0, h])
                delta = jnp.sum(d_parts[i] * o_ref[:, pair * 128:(pair + 1) * 128].astype(f32), axis=-1, keepdims=True)
                dsink_ref[h:h + 1, :] += jnp.broadcast_to(-jnp.sum(psink * delta, axis=0, keepdims=True), (1, 128))
                ts.append(p * (dp4[rows] - delta))
                ps.append(p)
                qn.append(q_parts[i] * inv)
                dn.append(d_parts[i] * inv)
                invs.append(inv)
            t4 = _stack_bf16(ts)
            dq4 = jnp.dot(t4, kb, preferred_element_type=f32)
            for pr in range(2):
                lo = dq4[(2 * pr) * BLK:(2 * pr + 1) * BLK] * invs[2 * pr]
                hi = dq4[(2 * pr + 1) * BLK:(2 * pr + 2) * BLK] * invs[2 * pr + 1]
                pair = 2 * kvh + pr
                dq_ref[:, pair * 128:(pair + 1) * 128] = (jnp.where(even, lo, hi) * scale).astype(dq_ref.dtype)
            dk_t = lax.dot_general(_stack_bf16(qn), t4, _DIMS["tn"], preferred_element_type=f32)
            dv_t = lax.dot_general(_stack_bf16(dn), _stack_bf16(ps), _DIMS["tn"], preferred_element_type=f32)
            dk_rows[ch].append(dk_t[0:HD] + dk_t[HD:2 * HD])
            dv_rows[ch].append(dv_t[0:HD] + dv_t[HD:2 * HD])
        dk_acc = [jnp.concatenate(r, axis=0).T for r in dk_rows]
        dv_acc = [jnp.concatenate(r, axis=0).T for r in dv_rows]
        for j in range(3):
            blk = n + (j - 1)

            @pl.when((blk >= 0) & (blk < nb))
            def _():
                rows = pl.ds(pl.multiple_of(blk * BLK, BLK), BLK)
                for ch in range(2):
                    dk_ref[rows, ch * 128:(ch + 1) * 128] += dk_acc[ch][j * BLK:(j + 1) * BLK]
                    dv_ref[rows, ch * 128:(ch + 1) * 128] += dv_acc[ch][j * BLK:(j + 1) * BLK]

        dq_copy(n).start()

        @pl.when(n == nb - 1)
        def _():
            def cast(i, c):
                rows = pl.ds(pl.multiple_of(i * 4 * BLK, 4 * BLK), 4 * BLK)
                kv_buf[rows, 0:256] = dk_ref[rows, :].astype(bf16)
                kv_buf[rows, 256:512] = dv_ref[rows, :].astype(bf16)
                return c

            lax.fori_loop(0, S // (4 * BLK), cast, 0)
            kv_copy = pltpu.make_async_copy(kv_buf, dproj_ref.at[:, pl.ds(kv_col, 512)], sems.at[2])
            kv_copy.start()
            if nb >= 2:
                dq_copy(n - 1).wait()
            dq_copy(n).wait()
            kv_copy.wait()

    row_blk = pl.BlockSpec((BLK, D), lambda n: (n, 0))
    return pl.pallas_call(
        body, name="attn_bwd", grid=(nb,),
        in_specs=[pl.BlockSpec(memory_space=pltpu.SMEM), q_spec] + k_specs + v_specs
        + [row_blk, pl.BlockSpec((None, BLK, D), lambda n: (1, n, 0)), ANY],
        out_specs=[ANY, pl.BlockSpec((NH, 128), lambda n: (0, 0))],
        out_shape=[_sds(dproj.shape, bf16), _sds((NH, 128), f32)],
        scratch_shapes=[pltpu.VMEM((S, 256), f32), pltpu.VMEM((S, 256), f32), pltpu.VMEM((2, BLK, D), bf16),
                        pltpu.VMEM((S, 512), bf16), pltpu.SemaphoreType.DMA((3,))],
        input_output_aliases={10: 0},
        compiler_params=_params(1, True))(sink, proj, proj, proj, proj, proj, proj, proj, y_b, dy, dproj)


def _adamw(name, w, g, m, v, tr):
    R, C = w.shape
    tr = min(tr, R)

    def body(w_ref, g_ref, m_ref, v_ref, d_ref, m2_ref, v2_ref):
        g = g_ref[...]
        m2 = ADAM_B1 * m_ref[...] + (1.0 - ADAM_B1) * g
        v2 = ADAM_B2 * v_ref[...] + (1.0 - ADAM_B2) * (g * g)
        m_hat = m2 / (1.0 - ADAM_B1 ** ADAM_STEP)
        v_hat = v2 / (1.0 - ADAM_B2 ** ADAM_STEP)
        d_ref[...] = -ADAM_LR * (m_hat / (jnp.sqrt(v_hat) + ADAM_EPS) + ADAM_WD * w_ref[...])
        m2_ref[...] = m2
        v2_ref[...] = v2

    blk = pl.BlockSpec((tr, C), lambda i: (i, 0))
    return pl.pallas_call(body, name=name, grid=(R // tr,), in_specs=[blk] * 4, out_specs=[blk] * 3,
                          out_shape=[_sds((R, C), f32)] * 3, compiler_params=_params(1))(w, g, m, v)


def _pair_sum(name, c_arr, g4, recv, th):
    _, _, h, w = g4.shape
    th = min(th, h)

    def body(c_ref, g_ref, r_ref, o_ref, ob_ref):
        p = g_ref[...] + r_ref[...]
        o_ref[...] = p
        ob_ref[...] = p.astype(bf16)

    blk = pl.BlockSpec((None, th, w), lambda s, i, c_ref: (s, i, 0))
    spec = pltpu.PrefetchScalarGridSpec(
        num_scalar_prefetch=1, grid=(NCHIP, h // th),
        in_specs=[pl.BlockSpec((None, None, th, w), lambda s, i, c_ref: (s, c_ref[0], i, 0)), blk],
        out_specs=[blk, blk])
    return pl.pallas_call(body, name=name, grid_spec=spec,
                          out_shape=[_sds((NCHIP, h, w), f32), _sds((NCHIP, h, w), bf16)],
                          compiler_params=_params(2))(c_arr, g4, recv)


def _chip_sum(name, chip_arr, own4, recv3, th):
    _, h, w = own4.shape
    th = min(th, h)

    def body(s_ref, o_ref, r_ref, out_ref):
        out_ref[...] = ((o_ref[...] + r_ref[0].astype(f32)) + r_ref[1].astype(f32)) + r_ref[2].astype(f32)

    spec = pltpu.PrefetchScalarGridSpec(
        num_scalar_prefetch=1, grid=(h // th,),
        in_specs=[pl.BlockSpec((None, th, w), lambda i, s_ref: (s_ref[0], i, 0)),
                  pl.BlockSpec((3, th, w), lambda i, s_ref: (0, i, 0))],
        out_specs=pl.BlockSpec((th, w), lambda i, s_ref: (i, 0)))
    return pl.pallas_call(body, name=name, grid_spec=spec, out_shape=_sds((h, w), f32),
                          compiler_params=_params(1, True))(chip_arr, own4, recv3)


def _adamw_halves(name, c_arr, w, g_own, g_recv, m, v, th):
    h, wd = g_own.shape
    th = min(th, h)

    def body(c_ref, w_ref, go_ref, gr_ref, m_ref, v_ref, g_ref, d_ref, m2_ref, v2_ref):
        g = jnp.where(c_ref[0] == pl.program_id(0), go_ref[...], gr_ref[...])
        m2 = ADAM_B1 * m_ref[...] + (1.0 - ADAM_B1) * g
        v2 = ADAM_B2 * v_ref[...] + (1.0 - ADAM_B2) * (g * g)
        m_hat = m2 / (1.0 - ADAM_B1 ** ADAM_STEP)
        v_hat = v2 / (1.0 - ADAM_B2 ** ADAM_STEP)
        g_ref[...] = g
        d_ref[...] = -ADAM_LR * (m_hat / (jnp.sqrt(v_hat) + ADAM_EPS) + ADAM_WD * w_ref[...])
        m2_ref[...] = m2
        v2_ref[...] = v2

    nt = h // th
    full = pl.BlockSpec((th, wd), lambda hh, i, c_ref: (hh * nt + i, 0))
    half = pl.BlockSpec((th, wd), lambda hh, i, c_ref: (i, 0))
    spec = pltpu.PrefetchScalarGridSpec(num_scalar_prefetch=1, grid=(2, nt),
                                        in_specs=[full, half, half, full, full], out_specs=[full] * 4)
    return pl.pallas_call(body, name=name, grid_spec=spec, out_shape=[_sds((2 * h, wd), f32)] * 4,
                          compiler_params=_params(2))(c_arr, w, g_own, g_recv, m, v)


def _add2(name, a, b):
    def body(a_ref, b_ref, o_ref):
        o_ref[...] = a_ref[...] + b_ref[...]
    return pl.pallas_call(body, name=name, out_shape=_sds(a.shape, f32))(a, b)


def _sum4(name, b4, th):
    _, h, w = b4.shape
    th = min(th, h)

    def body(b_ref, o_ref):
        o_ref[...] = ((b_ref[0] + b_ref[1]) + b_ref[2]) + b_ref[3]

    return pl.pallas_call(body, name=name, grid=(h // th,),
                          in_specs=[pl.BlockSpec((NCHIP, th, w), lambda i: (0, i, 0))],
                          out_specs=pl.BlockSpec((th, w), lambda i: (i, 0)), out_shape=_sds((h, w), f32),
                          compiler_params=_params(1, True))(b4)


def _coords():
    x, y, c = lax.axis_index("x"), lax.axis_index("y"), lax.axis_index("c")
    return x, y, c, [(1 - x, y), (x, 1 - y), (1 - x, 1 - y)]


def _gather_chips(arrs):
    n = len(arrs)

    def body(*refs):
        ins, outs = refs[:n], refs[n:2 * n]
        send_sems, recv_sems, local_sems = refs[2 * n:2 * n + 3]
        stage = refs[2 * n + 3:]
        x, y, c, chips = _coords()
        s = 2 * x + y
        sib = (x, y, 1 - c)
        load = [pltpu.make_async_copy(ins[a], stage[a], local_sems.at[a]) for a in range(n)]
        local = [pltpu.make_async_copy(stage[a], outs[a].at[s], local_sems.at[n + a]) for a in range(n)]
        for cp in load:
            cp.start()

        def over_ici(k, a, slot, peer):
            return pltpu.make_async_remote_copy(src_ref=ins[a].at[c], dst_ref=outs[a].at[slot, c], send_sem=send_sems.at[k * n + a],
                                                recv_sem=recv_sems.at[k * n + a], device_id=peer, device_id_type=MESH)

        def to_sibling(k, a, slot, half):
            i = (3 + k) * n + a
            return pltpu.make_async_remote_copy(src_ref=outs[a].at[slot, half], dst_ref=outs[a].at[slot, half], send_sem=send_sems.at[i],
                                                recv_sem=recv_sems.at[i], device_id=sib, device_id_type=MESH)

        sends = [over_ici(k, a, s, (px, py, c)) for k, (px, py) in enumerate(chips) for a in range(n)]
        for cp in sends:
            cp.start()
        for a in range(n):
            load[a].wait()
            local[a].start()
        passed = []
        for k, (px, py) in enumerate(chips):
            for a in range(n):
                over_ici(k, a, 2 * px + py, (px, py, c)).wait_recv()
                cp = to_sibling(k, a, 2 * px + py, c)
                cp.start()
                passed.append(cp)
        for k, (px, py) in enumerate(chips):
            for a in range(n):
                to_sibling(k, a, 2 * px + py, 1 - c).wait_recv()
        for cp in sends + passed:
            cp.wait_send()
        for cp in local:
            cp.wait()

    return pl.pallas_call(
        body, name="gather_weights", in_specs=[ANY] * n, out_specs=[ANY] * n,
        out_shape=[_sds((NCHIP,) + a.shape, a.dtype) for a in arrs],
        scratch_shapes=[pltpu.SemaphoreType.DMA((6 * n,)), pltpu.SemaphoreType.DMA((6 * n,)), pltpu.SemaphoreType.DMA((2 * n,))]
        + [pltpu.VMEM(a.shape, a.dtype) for a in arrs],
        compiler_params=pltpu.CompilerParams(vmem_limit_bytes=VMEM_LIMIT),
    )(*arrs)


HBM = pl.BlockSpec(memory_space=pltpu.HBM)
SEM = pl.BlockSpec(memory_space=pltpu.SEMAPHORE)
EFFECT = pltpu.SideEffectType.DATAFLOW_SIDE_EFFECTING


def _split_start(name, n_copies, make_copies, ins, land_shapes, after):
    ni, nl = len(ins), len(land_shapes)

    def body(*refs):
        in_refs, land_refs = refs[:ni], refs[ni:ni + nl]
        send_sems, recv_sems = refs[ni + nl + 1], refs[ni + nl + 2]
        token = refs[-1]
        for cp in make_copies(in_refs, land_refs, send_sems, recv_sems):
            cp.start()
        token[...] = jnp.zeros_like(token)

    lands = [pltpu.with_memory_space_constraint(lax.empty(s.shape, s.dtype), pltpu.HBM) for s in land_shapes]
    res = pl.pallas_call(
        body, name=name,
        out_shape=(pltpu.SemaphoreType.DMA((n_copies,)), pltpu.SemaphoreType.DMA((n_copies,)),
                   *[pltpu.HBM(a.shape, a.dtype) for a in ins], *[pltpu.HBM(s.shape, s.dtype) for s in land_shapes],
                   _sds((8, 128), f32)),
        in_specs=[HBM] * (ni + nl) + [ANY], out_specs=(SEM, SEM, *[HBM] * (ni + nl), pl.BlockSpec(memory_space=pltpu.VMEM)),
        input_output_aliases={i: 2 + i for i in range(ni + nl)},
        compiler_params=pltpu.CompilerParams(has_side_effects=EFFECT),
    )(*[pltpu.with_memory_space_constraint(a, pltpu.HBM) for a in ins], *lands, after)
    return res[0], res[1], list(res[2:2 + ni]), list(res[2 + ni:2 + ni + nl]), res[-1]


def _split_wait(name, make_copies, send_sems, recv_sems, ins, lands, after):
    ni, nl = len(ins), len(lands)

    def body(*refs):
        in_refs, land_refs = refs[:ni], refs[ni:ni + nl]
        s_sems, r_sems = refs[ni + nl], refs[ni + nl + 1]
        for cp in make_copies(in_refs, land_refs, s_sems, r_sems):
            cp.wait_send()
            cp.wait_recv()

    res = pl.pallas_call(
        body, name=name, out_shape=tuple(pltpu.HBM(a.shape, a.dtype) for a in ins + lands),
        in_specs=[HBM] * (ni + nl) + [SEM, SEM, ANY], out_specs=tuple([HBM] * (ni + nl)),
        input_output_aliases={i: i for i in range(ni + nl)},
        compiler_params=pltpu.CompilerParams(has_side_effects=EFFECT),
    )(*ins, *lands, send_sems, recv_sems, after)
    return list(res[:ni]), list(res[ni:])


def _gather_copies(n):
    def make(in_refs, land_refs, send_sems, recv_sems):
        x, y, c, chips = _coords()
        s = 2 * x + y
        return [pltpu.make_async_remote_copy(src_ref=in_refs[a], dst_ref=land_refs[a].at[s], send_sem=send_sems.at[k * n + a],
                                             recv_sem=recv_sems.at[k * n + a], device_id=(px, py, c), device_id_type=MESH)
                for k, (px, py) in enumerate(chips) for a in range(n)]
    return make


def _sibling_half_copies(n):
    def make(in_refs, land_refs, send_sems, recv_sems):
        x, y, c, _ = _coords()
        return [pltpu.make_async_remote_copy(src_ref=in_refs[a].at[:, 1 - c], dst_ref=land_refs[a], send_sem=send_sems.at[a],
                                             recv_sem=recv_sems.at[a], device_id=(x, y, 1 - c), device_id_type=MESH)
                for a in range(n)]
    return make


def _chip_part_copies(n):
    def make(in_refs, land_refs, send_sems, recv_sems):
        x, y, c, chips = _coords()
        return [pltpu.make_async_remote_copy(src_ref=in_refs[a].at[2 * px + py], dst_ref=land_refs[a].at[k],
                                             send_sem=send_sems.at[k * n + a], recv_sem=recv_sems.at[k * n + a],
                                             device_id=(px, py, c), device_id_type=MESH)
                for k, (px, py) in enumerate(chips) for a in range(n)]
    return make


def _sibling_whole_copies(n):
    def make(in_refs, land_refs, send_sems, recv_sems):
        x, y, c, _ = _coords()
        return [pltpu.make_async_remote_copy(src_ref=in_refs[a], dst_ref=land_refs[a], send_sem=send_sems.at[a],
                                             recv_sem=recv_sems.at[a], device_id=(x, y, 1 - c), device_id_type=MESH)
                for a in range(n)]
    return make


def _place_own(chip_arr, owns, lands, steps):
    n = len(owns)

    def body(s_ref, *refs):
        for a in range(n):
            refs[2 * n + a][...] = refs[a][...]

    tiles = [o.shape[0] // steps for o in owns]
    spec = pltpu.PrefetchScalarGridSpec(
        num_scalar_prefetch=1, grid=(steps,),
        in_specs=[pl.BlockSpec((t, o.shape[1]), lambda i, s_ref: (i, 0)) for t, o in zip(tiles, owns)] + [ANY] * n,
        out_specs=[pl.BlockSpec((None, t, o.shape[1]), lambda i, s_ref: (s_ref[0], i, 0)) for t, o in zip(tiles, owns)])
    return pl.pallas_call(body, name="place_own", grid_spec=spec, out_shape=[_sds(l.shape, l.dtype) for l in lands],
                          input_output_aliases={1 + n + a: a for a in range(n)},
                          compiler_params=_params(1))(chip_arr, *owns, *lands)


def _sibling_halves(g4s, small):
    n = len(g4s)

    def body(*refs):
        ins, small_ref = refs[:n], refs[n]
        outs, small_out = refs[n + 1:2 * n + 1], refs[2 * n + 1]
        send_sems, recv_sems = refs[2 * n + 2:]
        x, y, c, _ = _coords()
        sib = (x, y, 1 - c)

        def remote(a, half):
            src = small_ref if a == n else ins[a].at[:, half]
            dst = small_out if a == n else outs[a]
            return pltpu.make_async_remote_copy(src_ref=src, dst_ref=dst, send_sem=send_sems.at[a], recv_sem=recv_sems.at[a],
                                                device_id=sib, device_id_type=MESH)

        sends = [remote(a, 1 - c) for a in range(n + 1)]
        for cp in sends:
            cp.start()
        for a in range(n + 1):
            remote(a, c).wait_recv()
        for cp in sends:
            cp.wait_send()

    return pl.pallas_call(
        body, name="reduce_sibling", in_specs=[ANY] * (n + 1), out_specs=[ANY] * (n + 1),
        out_shape=[_sds((g.shape[0],) + g.shape[2:], f32) for g in g4s] + [_sds(small.shape, f32)],
        scratch_shapes=[pltpu.SemaphoreType.DMA((n + 1,)), pltpu.SemaphoreType.DMA((n + 1,))],
    )(*g4s, small)


def _exchange_chips(parts, small2):
    n = len(parts)

    def body(*refs):
        ins, small_ref = refs[:n], refs[n]
        outs, small_out = refs[n + 1:2 * n + 1], refs[2 * n + 1]
        send_sems, recv_sems, local_sem = refs[2 * n + 2:]
        x, y, c, chips = _coords()
        s = 2 * x + y
        local = pltpu.make_async_copy(small_ref.at[c], small_out.at[s], local_sem)
        local.start()

        def remote(k, a, dest_chip, small_slot, peer):
            if a == n:
                src, dst = small_ref.at[c], small_out.at[small_slot]
            else:
                src, dst = ins[a].at[dest_chip], outs[a].at[k]
            i = k * (n + 1) + a
            return pltpu.make_async_remote_copy(src_ref=src, dst_ref=dst, send_sem=send_sems.at[i], recv_sem=recv_sems.at[i],
                                                device_id=peer, device_id_type=MESH)

        sends = [remote(k, a, 2 * px + py, s, (px, py, c)) for k, (px, py) in enumerate(chips) for a in range(n + 1)]
        for cp in sends:
            cp.start()
        for k, (px, py) in enumerate(chips):
            for a in range(n + 1):
                remote(k, a, s, 2 * px + py, (px, py, c)).wait_recv()
        for cp in sends:
            cp.wait_send()
        local.wait()

    m = 3 * (n + 1)
    return pl.pallas_call(
        body, name="reduce_chips", in_specs=[ANY] * (n + 1), out_specs=[ANY] * (n + 1),
        out_shape=[_sds((3,) + p.shape[1:], p.dtype) for p in parts] + [_sds((NCHIP,) + small2.shape[1:], f32)],
        scratch_shapes=[pltpu.SemaphoreType.DMA((m,)), pltpu.SemaphoreType.DMA((m,)), pltpu.SemaphoreType.DMA],
    )(*parts, small2)


def _share_sibling(halves):
    n = len(halves)

    def body(*refs):
        ins, outs = refs[:n], refs[n:2 * n]
        send_sems, recv_sems = refs[2 * n:]
        x, y, c, _ = _coords()
        sib = (x, y, 1 - c)
        sends = [pltpu.make_async_remote_copy(src_ref=ins[a], dst_ref=outs[a], send_sem=send_sems.at[a], recv_sem=recv_sems.at[a],
                                              device_id=sib, device_id_type=MESH) for a in range(n)]
        for cp in sends:
            cp.start()
        for cp in sends:
            cp.wait()

    return pl.pallas_call(
        body, name="reduce_share", in_specs=[ANY] * n, out_specs=[ANY] * n,
        out_shape=[_sds(h.shape, f32) for h in halves],
        scratch_shapes=[pltpu.SemaphoreType.DMA((n,)), pltpu.SemaphoreType.DMA((n,))],
    )(*halves)


def _block_diag_pairs(w):
    w = w.reshape(NCH, 2, HD, HD)
    z = jnp.zeros((NCH, HD, HD), w.dtype)
    return jnp.concatenate([jnp.concatenate([w[:, 0], z], axis=2), jnp.concatenate([z, w[:, 1]], axis=2)], axis=1)


def _diag_blocks(m):
    return jnp.stack([m[:, :HD, :HD], m[:, HD:, HD:]], axis=1).reshape(NH, HD, HD)


def _pack(vs, rows):
    flat = jnp.concatenate([v.reshape(-1) for v in vs])
    return jnp.pad(flat, (0, rows * 128 - flat.shape[0])).reshape(rows, 128)


def _unpack(packed, shapes):
    flat = packed.reshape(-1)
    out, off = [], 0
    for shp in shapes:
        size = math.prod(shp)
        out.append(flat[off:off + size].reshape(shp))
        off += size
    return out


def _rows_for(sizes, multiple):
    rows = -(-sum(sizes) // 128)
    return -(-rows // multiple) * multiple


def kernel(x, norm_mix_g, w_in, b_gate, conv_w, conv_b, lru_lambda, lru_wa, lru_ba, lru_wx, lru_bx, attn_sink, w_out, norm_ffn_g, w_ffn_in, w_ffn_out, norm_final_g, loss_target, m_norm_mix_g, m_w_in, m_b_gate, m_conv_w, m_conv_b, m_lru_lambda, m_lru_wa, m_lru_ba, m_lru_wx, m_lru_bx, m_attn_sink, m_w_out, m_norm_ffn_g, m_w_ffn_in, m_w_ffn_out, m_norm_final_g, v_norm_mix_g, v_w_in, v_b_gate, v_conv_w, v_conv_b, v_lru_lambda, v_lru_wa, v_lru_ba, v_lru_wx, v_lru_bx, v_attn_sink, v_w_out, v_norm_ffn_g, v_w_ffn_in, v_w_ffn_out, v_norm_final_g):
    S = x.shape[1]
    xs = x[0]
    tgt = loss_target[0]
    cx, cy, cc = lax.axis_index("x"), lax.axis_index("y"), lax.axis_index("c")
    chip = 2 * cx + cy
    SW = D // NCHIP

    small_shard = _pack([conv_w[0], lru_lambda[0], lru_ba[0], lru_bx[0]], 32)
    halves_of = lambda a: a.reshape(2, a.shape[0] // 2, a.shape[1])
    w_in_g, small_g = _gather_chips([halves_of(w_in[0].astype(bf16)), halves_of(small_shard)])
    w_in_g = w_in_g.reshape(NCHIP, D, SHW)
    small_g = small_g.reshape(NCHIP, 32, 128)
    late = [w_ffn_in[0].astype(bf16), w_out[0].astype(bf16), w_ffn_out[0].astype(bf16)]
    late_send, late_recv, late_src, late_land, late_token = _split_start(
        "gather_late_start", 9, _gather_copies(3), late, [_sds((NCHIP,) + a.shape, bf16) for a in late], small_g)
    small_parts = [_unpack(small_g[s], [(4, SW), (2, SW), (2, SW), (2, SW)]) for s in range(NCHIP)]
    conv_w_f, lam_f, ba_f, bx_f = [jnp.concatenate([small_parts[s][p] for s in range(NCHIP)], axis=1) for p in range(4)]
    wbd = jnp.concatenate([_block_diag_pairs(lru_wa[0, 0]), _block_diag_pairs(lru_wx[0, 0]),
                           _block_diag_pairs(lru_wa[0, 1]), _block_diag_pairs(lru_wx[0, 1])], axis=2).astype(bf16)
    conv_b_f = conv_b
    sink = attn_sink

    xn, proj = _rms_matmul("rms_proj", xs, norm_mix_g + late_token[0:1, 0:1], w_in_g, 512)
    y_a, lru_state = _lru_fwd(proj, conv_w_f, conv_b_f, lam_f, ba_f, bx_f, wbd)
    y_b = _attn_fwd(proj, sink)
    late_src, late_land = _split_wait("gather_late_wait", _gather_copies(3), late_send, late_recv, late_src, late_land, y_b)
    chip_arr = chip.reshape(1).astype(jnp.int32)
    w_ffn_in_g, w_out_g, w_ffn_out_g = _place_own(chip_arr, late_src, late_land, 4)
    w_out_f = w_out_g.reshape(D, D)
    w_ffn_out_f = w_ffn_out_g.reshape(FF, D)
    merged, x1 = _merge_out_proj(proj, b_gate, y_a, y_b, w_out_f, xs, 512)
    xn2, gu, act = _rms_matmul_swiglu("rms_ffn_in", x1, norm_ffn_g, w_ffn_in_g, 512)
    dx2, loss_row, dg3 = _ffn_out_loss_bwd(act, w_ffn_out_f, x1, norm_final_g.reshape(1, D), tgt, 512)

    tk = min(2048, S)
    gw_ffn_out = _mm_tn("dw_ffn_out", act, pl.BlockSpec((tk, SHW), lambda i, k: (k, i)),
                        dx2, pl.BlockSpec((tk, D), lambda i, k: (k, 0)),
                        _sds((FF, D), f32), pl.BlockSpec((SHW, D), lambda i, k: (i, 0)), (2, S // tk), (SHW, D))
    dgu = _swiglu_bwd(dx2, w_ffn_out_f, gu, 256)
    gw_ffn_in = _mm_tn("dw_ffn_in", xn2, pl.BlockSpec((tk, D), lambda g, k: (k, 0)),
                       dgu, pl.BlockSpec((None, tk, SHW), lambda g, k: (g // 2, k, g % 2)),
                       _sds((NCHIP, D, SHW), f32), pl.BlockSpec((None, D, SHW), lambda g, k: (g, 0, 0)),
                       (NCHIP, S // tk), (D, SHW))
    c_arr = cc.reshape(1).astype(jnp.int32)
    early_names, early_tiles = ["w_ffn_in", "w_ffn_out"], [256, 352]
    early = [gw_ffn_in.reshape(NCHIP, 2, D // 2, SHW), gw_ffn_out.reshape(NCHIP, 2, FF // NCHIP // 2, D)]
    ea_send, ea_recv, ea_src, ea_land, ea_token = _split_start(
        "reduce_early_sibling_start", 2, _sibling_half_copies(2), early,
        [_sds((NCHIP,) + g.shape[2:], f32) for g in early], dgu)
    dx1, dg2 = _mm_nt_rms_bwd("dxn2_rms_bwd", dgu, w_ffn_in_g, x1, norm_ffn_g + ea_token[0:1, 0:1], dx2, 512)

    gw_out = _mm_tn("dw_out", merged, pl.BlockSpec((tk, D), lambda i, k: (k, 0)),
                    dx1, pl.BlockSpec((tk, D), lambda i, k: (k, 0)),
                    _sds((D, D), f32), pl.BlockSpec((D, D), lambda i, k: (0, 0)), (1, S // tk), (D, D))
    dproj, dy, db_gate = _merge_bwd(proj, b_gate, y_a, y_b, dx1, w_out_f, 512)
    ea_src, ea_land = _split_wait("reduce_early_sibling_wait", _sibling_half_copies(2), ea_send, ea_recv, ea_src, ea_land, dy)
    early_pairs = [_pair_sum("pair_sum_" + nm, c_arr, g4, r, th)
                   for nm, g4, r, th in zip(early_names, ea_src, ea_land, early_tiles)]
    eb_send, eb_recv, eb_src, eb_land, eb_token = _split_start(
        "reduce_early_chips_start", 6, _chip_part_copies(2), [p[1] for p in early_pairs],
        [_sds((3,) + p[1].shape[1:], bf16) for p in early_pairs], early_pairs[0][0])
    dproj, dsink = _attn_bwd(proj, sink + eb_token[0:1, 0:1], y_b, dy, dproj)
    _, eb_land = _split_wait("reduce_early_chips_wait", _chip_part_copies(2), eb_send, eb_recv, eb_src, eb_land, dsink)
    early_halves = [_chip_sum("chip_sum_" + nm, chip_arr, p[0], r3, th)
                    for nm, p, r3, th in zip(early_names, early_pairs, eb_land, early_tiles)]
    ec_send, ec_recv, ec_src, ec_land, ec_token = _split_start(
        "reduce_early_share_start", 2, _sibling_whole_copies(2), early_halves, [_sds(h.shape, f32) for h in early_halves], dsink)
    dproj, dcw, dcb, dlam, dba, dbx, dwbd = _lru_bwd(proj, dy, lru_state, dproj, conv_w_f, conv_b_f + ec_token[0:1, 0:1], lam_f,
                                                     ba_f, bx_f, wbd)
    early_halves, early_other = _split_wait("reduce_early_share_wait", _sibling_whole_copies(2), ec_send, ec_recv, ec_src, ec_land, dcb)
    gw_in = _mm_tn("dw_in", xn, pl.BlockSpec((tk, D), lambda g, k: (k, 0)),
                   dproj, pl.BlockSpec((tk, SHW), lambda g, k: (k, g)),
                   _sds((NCHIP, D, SHW), f32), pl.BlockSpec((None, D, SHW), lambda g, k: (g, 0, 0)),
                   (NCHIP, S // tk), (D, SHW))
    wa_send, wa_recv, wa_src, wa_land, wa_token = _split_start(
        "reduce_w_in_sibling_start", 1, _sibling_half_copies(1), [gw_in.reshape(NCHIP, 2, D // 2, SHW)],
        [_sds((NCHIP, D // 2, SHW), f32)], dproj)
    grad_x, dg1 = _mm_nt_rms_bwd("dxn_rms_bwd", dproj, w_in_g, xs, norm_mix_g + wa_token[0:1, 0:1], dx1, 512)
    wa_src, wa_land = _split_wait("reduce_w_in_sibling_wait", _sibling_half_copies(1), wa_send, wa_recv, wa_src, wa_land, dg1)
    w_in_pair = _pair_sum("pair_sum_w_in", c_arr, wa_src[0], wa_land[0], 256)
    wb_send, wb_recv, wb_src, wb_land, wb_token = _split_start(
        "reduce_w_in_chips_start", 3, _chip_part_copies(1), [w_in_pair[1]], [_sds((3, D // 2, SHW), bf16)], w_in_pair[0])

    d_wa = jnp.stack([_diag_blocks(dwbd[:, :, 0:CW]), _diag_blocks(dwbd[:, :, 2 * CW:3 * CW])])
    d_wx = jnp.stack([_diag_blocks(dwbd[:, :, CW:2 * CW]), _diag_blocks(dwbd[:, :, 3 * CW:4 * CW])])
    small_full = [dg1, db_gate, dcw, dcb, dlam, d_wa, dba, d_wx, dbx, dsink[:, 0], dg2, dg3,
                  loss_row[0, 0:1]]
    full_shapes = [(1, D), (1, 2 * D), (4, D), (1, D), (2, D), (2, NH, HD, HD), (2, D), (2, NH, HD, HD), (2, D), (NH,),
                   (1, D), (1, D), (1,)]
    rows_full = _rows_for([math.prod(s) for s in full_shapes], 16)
    small_vec = _pack(small_full, rows_full)

    late_names, late_tiles = ["w_in", "w_out"], [256, 128]
    big = [gw_out.reshape(NCHIP, 2, D // NCHIP // 2, D)]
    *recv_a, small_sib = _sibling_halves(big, small_vec + wb_token[0:1, 0:1])
    w_out_pair = _pair_sum("pair_sum_w_out", c_arr, big[0], recv_a[0], 128)
    small_chip = _add2("pair_sum_small", small_vec, small_sib).reshape(2, rows_full // 2, 128)
    *recv_b, small_all = _exchange_chips([w_out_pair[1]], small_chip)
    w_out_half = _chip_sum("chip_sum_w_out", chip_arr, w_out_pair[0], recv_b[0], 128)
    _, wb_land = _split_wait("reduce_w_in_chips_wait", _chip_part_copies(1), wb_send, wb_recv, wb_src, wb_land, small_all)
    w_in_half = _chip_sum("chip_sum_w_in", chip_arr, w_in_pair[0], wb_land[0], 256)
    halves = [w_in_half, w_out_half, _sum4("chip_sum_small", small_all, rows_full // 2)]
    *recv_c, small_other = _share_sibling(halves)
    small_lo = jnp.where(cc == 0, halves[2], small_other)
    small_hi = jnp.where(cc == 0, small_other, halves[2])
    g_full = _unpack(jnp.concatenate([small_lo, small_hi], axis=0), full_shapes)

    out_big = {}
    for nm, w, g_own, g_recv, m, v, th in zip(late_names + early_names, [w_in, w_out, w_ffn_in, w_ffn_out],
                                              halves[:2] + early_halves, recv_c + early_other,
                                              [m_w_in, m_w_out, m_w_ffn_in, m_w_ffn_out],
                                              [v_w_in, v_w_out, v_w_ffn_in, v_w_ffn_out], late_tiles + early_tiles):
        g_, d_, m_, v_ = _adamw_halves("adamw_" + nm, c_arr, w[0], g_own, g_recv, m[0], v[0], th)
        out_big[nm] = (g_[None], d_[None], m_[None], v_[None])

    small_names = ["norm_mix_g", "b_gate", "conv_w", "conv_b", "lru_lambda", "lru_wa", "lru_ba", "lru_wx", "lru_bx", "attn_sink",
                   "norm_ffn_g", "norm_final_g"]
    sharded = {"conv_w", "lru_lambda", "lru_ba", "lru_bx"}
    small_w = [norm_mix_g, b_gate, conv_w, conv_b, lru_lambda, lru_wa, lru_ba, lru_wx, lru_bx, attn_sink, norm_ffn_g, norm_final_g]
    small_m = [m_norm_mix_g, m_b_gate, m_conv_w, m_conv_b, m_lru_lambda, m_lru_wa, m_lru_ba, m_lru_wx, m_lru_bx, m_attn_sink,
               m_norm_ffn_g, m_norm_final_g]
    small_v = [v_norm_mix_g, v_b_gate, v_conv_w, v_conv_b, v_lru_lambda, v_lru_wa, v_lru_ba, v_lru_wx, v_lru_bx, v_attn_sink,
               v_norm_ffn_g, v_norm_final_g]
    g_local = []
    for nm, g, w in zip(small_names, g_full, small_w):
        if nm in sharded:
            g = lax.dynamic_slice_in_dim(g, chip * SW, SW, axis=1)
        g_local.append(g.reshape(w.shape))
    local_shapes = [w.shape for w in small_w]
    rows_local = _rows_for([math.prod(s) for s in local_shapes], 8)
    d_s, m_s, v_s = _adamw("adamw_small", _pack(small_w, rows_local), _pack(g_local, rows_local),
                           _pack(small_m, rows_local), _pack(small_v, rows_local), rows_local)
    d_l, m_l, v_l = _unpack(d_s, local_shapes), _unpack(m_s, local_shapes), _unpack(v_s, local_shapes)
    res = {nm: (g_local[i], d_l[i], m_l[i], v_l[i]) for i, nm in enumerate(small_names)}
    res.update(out_big)

    order = ["norm_mix_g", "w_in", "b_gate", "conv_w", "conv_b", "lru_lambda", "lru_wa", "lru_ba", "lru_wx", "lru_bx", "attn_sink",
             "w_out", "norm_ffn_g", "w_ffn_in", "w_ffn_out", "norm_final_g"]
    outs = [g_full[-1][0], grad_x[None]]
    for k in range(4):
        outs += [res[nm][k] for nm in order]
    return tuple(outs)
```
